```python
import jax
import jax.numpy as jnp
from jax import lax
import numpy as np

D_MODEL = 1024
BATCH = 1
SEQ = 16384
DEPTH = 2

GRID_W = 64
CTX_LEN = 256
HEAD_DIM = D_MODEL // 16
POOL_GROUPS = 4
POOL_WINDOWS = (2, 4, 8, 16)
POOL_GROUP_DIM = HEAD_DIM
POOL_DIM = POOL_GROUPS * POOL_GROUP_DIM
NA_HEADS = 8
NA_DIM = NA_HEADS * HEAD_DIM
NA_WIN_ROWS = 8
NA_WIN_COLS = 16
SG_HEADS = 4
SG_DIM = SG_HEADS * HEAD_DIM
SG_CHUNK = 128
MIX_DIM = POOL_DIM + NA_DIM + SG_DIM
Q_OFF = POOL_DIM
K_OFF = Q_OFF + NA_DIM
V_OFF = K_OFF + NA_DIM
U_OFF = V_OFF + NA_DIM
G_OFF = U_OFF + SG_DIM
IN_DIM = G_OFF + SG_DIM
N_EXPERTS = 16
N_GROUPS = 4
TOP_K = 2
D_EXPERT = D_MODEL // 2
EPS = 1e-6

kernel_name = 'hybrid_pool_natten_gmlp_moe_dit'


def rms_norm(x, g):
    xf = x.astype(jnp.float32)
    y = xf * lax.rsqrt(jnp.mean(xf * xf, axis=-1, keepdims=True) + EPS)
    return (y * g.astype(jnp.float32)).astype(x.dtype)


def modulate(x, shift, scale):
    return x * (1 + scale) + shift


def centred_pool_minus_self(x, window):
    L = x.shape[1]
    xf = x.astype(jnp.float32)
    cs = jnp.concatenate([jnp.zeros_like(xf[:, :1]), lax.cumsum(xf, axis=1)], axis=1)
    pos = jnp.arange(L)
    lo = jnp.clip(pos - window // 2, 0, L)
    hi = jnp.clip(pos + window // 2, 0, L)
    mean = (cs[:, hi] - cs[:, lo]) / (hi - lo).astype(jnp.float32)[None, :, None]
    return (mean - xf).astype(x.dtype)


def pool_mixer(p, w_pool, pool_scale):
    B, L, _ = p.shape
    pg = p.reshape(B, L, POOL_GROUPS, POOL_GROUP_DIM)
    y = jnp.stack([centred_pool_minus_self(pg[:, :, g], w) for g, w in enumerate(POOL_WINDOWS)], axis=2)
    y = jnp.einsum('blgc,gcd->blgd', y, w_pool)
    return y.reshape(B, L, POOL_DIM) * pool_scale


def spatial_gate(u, v, w_sg, b_sg, sg_norm):
    B, L, _ = u.shape
    nc = L // SG_CHUNK
    u = jax.nn.gelu(u)
    v = rms_norm(jax.nn.gelu(v).reshape(B, nc, SG_CHUNK, SG_HEADS, HEAD_DIM), sg_norm)
    mixed = jnp.einsum('hpq,bnqhd->bnphd', w_sg, v) + b_sg.T[:, :, None]
    return u * mixed.reshape(B, L, SG_DIM)


def neighbourhood_attention(q, k, v, k_ctx, v_ctx, rpb):
    B, N, H, dh = q.shape
    rows = N // GRID_W
    wr = min(NA_WIN_ROWS, rows)
    scale = HEAD_DIM ** -0.5
    qg = q.reshape(B, rows, GRID_W, H, dh)
    kg = k.reshape(B, rows, GRID_W, H, dh)
    vg = v.reshape(B, rows, GRID_W, H, dh)
    cols = jnp.arange(GRID_W)
    col_start = jnp.clip(cols - NA_WIN_COLS // 2, 0, GRID_W - NA_WIN_COLS)
    col_idx = col_start[:, None] + jnp.arange(NA_WIN_COLS)[None, :]
    dc = col_idx - cols[:, None] + NA_WIN_COLS - 1

    def one_row(r):
        rs = jnp.clip(r - wr // 2, 0, rows - wr)
        kb = lax.dynamic_slice_in_dim(kg, rs, wr, axis=1)[:, :, col_idx]
        vb = lax.dynamic_slice_in_dim(vg, rs, wr, axis=1)[:, :, col_idx]
        qr = lax.dynamic_index_in_dim(qg, r, axis=1, keepdims=False)
        dr = rs + jnp.arange(wr) - r + NA_WIN_ROWS - 1
        bias = rpb[:, dr[None, :, None], dc[:, None, :]]
        s_loc = jnp.einsum('bqhd,baqwhd->bhqaw', qr, kb).astype(jnp.float32) * scale + bias.astype(jnp.float32)[None]
        s_loc = s_loc.reshape(B, H, GRID_W, wr * NA_WIN_COLS)
        s_ctx = jnp.einsum('bqhd,bchd->bhqc', qr, k_ctx).astype(jnp.float32) * scale
        p = jax.nn.softmax(jnp.concatenate([s_loc, s_ctx], axis=-1), axis=-1)
        p_loc = p[..., :wr * NA_WIN_COLS].reshape(B, H, GRID_W, wr, NA_WIN_COLS).astype(v.dtype)
        p_ctx = p[..., wr * NA_WIN_COLS:].astype(v.dtype)
        return (jnp.einsum('bhqaw,baqwhd->bqhd', p_loc, vb)
                + jnp.einsum('bhqc,bchd->bqhd', p_ctx, v_ctx))

    out = lax.map(one_row, jnp.arange(rows))
    return jnp.moveaxis(out, 0, 1).reshape(B, N, H * dh)


def context_attention(q, k, v):
    B, Lc, H, dh = q.shape
    s = jnp.einsum('bqhd,bkhd->bhqk', q, k).astype(jnp.float32) * (HEAD_DIM ** -0.5)
    p = jax.nn.softmax(s, axis=-1).astype(v.dtype)
    return jnp.einsum('bhqk,bkhd->bqhd', p, v).reshape(B, Lc, H * dh)


def moe(h, w_router, b_router, w_gate, w_up, w_down):
    B, L, _ = h.shape
    logits = (h @ w_router).astype(jnp.float32) + b_router.astype(jnp.float32)
    scores = jax.nn.softmax(logits, axis=-1)
    sg = scores.reshape(B, L, N_GROUPS, N_EXPERTS // N_GROUPS)
    grp_score = lax.top_k(sg, TOP_K)[0].sum(-1)
    grp_mask = jax.nn.one_hot(jnp.argmax(grp_score, axis=-1), N_GROUPS, dtype=jnp.bool_)
    masked = jnp.where(grp_mask[..., None], sg, -1.0).reshape(B, L, N_EXPERTS)
    top_w, top_i = lax.top_k(masked, TOP_K)
    top_w = top_w / jnp.sum(top_w, axis=-1, keepdims=True)
    gates = jnp.sum(jax.nn.one_hot(top_i, N_EXPERTS, dtype=jnp.float32) * top_w[..., None], axis=-2)
    gates = gates.astype(h.dtype)
    y = jnp.zeros_like(h)
    for e in range(N_EXPERTS):
        he = jax.nn.silu(h @ w_gate[e]) * (h @ w_up[e])
        y = y + gates[..., e:e + 1] * (he @ w_down[e])
    return y


def setup_inputs(seed: int = 0) -> dict:
    key = jax.random.key(seed)
    ks = jax.random.split(key, 24)
    f32 = jnp.float32
    D = D_MODEL

    def nrm(k, shape, s):
        return jax.random.normal(k, shape, f32) * s

    return {
        'x': nrm(ks[0], (BATCH, SEQ, D), 1.0),
        'c': nrm(ks[1], (BATCH, D), 1.0),
        'ctx': nrm(ks[2], (BATCH, CTX_LEN, D), 1.0),
        'c_ctx': nrm(ks[3], (D,), 1.0),
        'w_ada': nrm(ks[4], (DEPTH, D, 6 * D), 0.5 * D ** -0.5),
        'b_ada': nrm(ks[5], (DEPTH, 6 * D), 0.02),
        'norm1': 1.0 + nrm(ks[6], (DEPTH, D), 0.01),
        'w_in': nrm(ks[7], (DEPTH, D, IN_DIM), D ** -0.5),
        'pool_w': nrm(ks[8], (DEPTH, POOL_GROUPS, POOL_GROUP_DIM, POOL_GROUP_DIM), POOL_GROUP_DIM ** -0.5),
        'pool_scale': 1.0 + nrm(ks[9], (DEPTH, POOL_DIM), 0.02),
        'q_norm': 1.0 + nrm(ks[10], (DEPTH, NA_HEADS, HEAD_DIM), 0.01),
        'k_norm': 1.0 + nrm(ks[11], (DEPTH, NA_HEADS, HEAD_DIM), 0.01),
        'rpb': nrm(ks[12], (DEPTH, NA_HEADS, 2 * NA_WIN_ROWS - 1, 2 * NA_WIN_COLS - 1), 0.1),
        'sg_w': nrm(ks[13], (DEPTH, SG_HEADS, SG_CHUNK, SG_CHUNK), SG_CHUNK ** -0.5),
        'sg_b': 1.0 + nrm(ks[14], (DEPTH, SG_HEADS, SG_CHUNK), 0.01),
        'sg_norm': 1.0 + nrm(ks[15], (DEPTH, SG_HEADS, HEAD_DIM), 0.01),
        'w_out': nrm(ks[16], (DEPTH, MIX_DIM, D), MIX_DIM ** -0.5),
        'norm2': 1.0 + nrm(ks[17], (DEPTH, D), 0.01),
        'w_router': nrm(ks[18], (D, N_EXPERTS), D ** -0.5),
        'b_router': nrm(ks[19], (N_EXPERTS,), 0.01),
        'w_gate': nrm(ks[20], (DEPTH, N_EXPERTS, D, D_EXPERT), D ** -0.5),
        'w_up': nrm(ks[21], (DEPTH, N_EXPERTS, D, D_EXPERT), D ** -0.5),
        'w_down': nrm(ks[22], (DEPTH, N_EXPERTS, D_EXPERT, D), D_EXPERT ** -0.5),
    }


def reference(x, c, ctx, c_ctx, w_ada, b_ada, norm1, w_in, pool_w, pool_scale, q_norm, k_norm, rpb,
              sg_w, sg_b, sg_norm, w_out, norm2, w_router, b_router, w_gate, w_up, w_down):
    cond_lat = jax.nn.silu(c)[:, None]
    cond_ctx = jax.nn.silu(c_ctx)[None, None]
    h_lat, h_ctx = x, ctx
    for l in range(DEPTH):
        last = l == DEPTH - 1
        sh1, sc1, g1, sh2, sc2, g2 = jnp.split(cond_lat @ w_ada[l] + b_ada[l], 6, axis=-1)
        csh1, csc1, cg1, csh2, csc2, cg2 = jnp.split(cond_ctx @ w_ada[l] + b_ada[l], 6, axis=-1)

        hn_ctx = modulate(rms_norm(h_ctx, norm1[l]), csh1, csc1)
        Bc, Lc, _ = hn_ctx.shape
        if last:
            a_c = hn_ctx @ w_in[l][:, K_OFF:U_OFF]
            kc_raw, vc_raw = a_c[..., :NA_DIM], a_c[..., NA_DIM:]
        else:
            a_c = hn_ctx @ w_in[l]
            kc_raw, vc_raw = a_c[..., K_OFF:V_OFF], a_c[..., V_OFF:U_OFF]
        k_ctx = rms_norm(kc_raw.reshape(Bc, Lc, NA_HEADS, HEAD_DIM), k_norm[l])
        v_ctx = vc_raw.reshape(Bc, Lc, NA_HEADS, HEAD_DIM)

        hn_lat = modulate(rms_norm(h_lat, norm1[l]), sh1, sc1)
        a = hn_lat @ w_in[l]
        B, N, _ = a.shape
        q = rms_norm(a[..., Q_OFF:K_OFF].reshape(B, N, NA_HEADS, HEAD_DIM), q_norm[l])
        k = rms_norm(a[..., K_OFF:V_OFF].reshape(B, N, NA_HEADS, HEAD_DIM), k_norm[l])
        v = a[..., V_OFF:U_OFF].reshape(B, N, NA_HEADS, HEAD_DIM)
        mix_lat = jnp.concatenate([
            pool_mixer(a[..., :Q_OFF], pool_w[l], pool_scale[l]),
            neighbourhood_attention(q, k, v, k_ctx, v_ctx, rpb[l]),
            spatial_gate(a[..., U_OFF:G_OFF], a[..., G_OFF:], sg_w[l], sg_b[l], sg_norm[l]),
        ], axis=-1)
        h_lat = h_lat + g1 * (mix_lat @ w_out[l])
        h_lat = h_lat + g2 * moe(modulate(rms_norm(h_lat, norm2[l]), sh2, sc2),
                                 w_router, b_router, w_gate[l], w_up[l], w_down[l])

        if not last:
            qc = rms_norm(a_c[..., Q_OFF:K_OFF].reshape(Bc, Lc, NA_HEADS, HEAD_DIM), q_norm[l])
            mix_ctx = jnp.concatenate([
                pool_mixer(a_c[..., :Q_OFF], pool_w[l], pool_scale[l]),
                context_attention(qc, k_ctx, v_ctx),
                spatial_gate(a_c[..., U_OFF:G_OFF], a_c[..., G_OFF:], sg_w[l], sg_b[l], sg_norm[l]),
            ], axis=-1)
            h_ctx = h_ctx + cg1 * (mix_ctx @ w_out[l])
            h_ctx = h_ctx + cg2 * moe(modulate(rms_norm(h_ctx, norm2[l]), csh2, csc2),
                                     w_router, b_router, w_gate[l], w_up[l], w_down[l])
    return h_lat
```

```python
import functools

import jax
import jax.numpy as jnp
import numpy as np
from jax import lax
from jax.experimental import pallas as pl
from jax.experimental.pallas import tpu as pltpu

D_MODEL = 1024
GRID_W = 64
HEAD_DIM = 64
POOL_WINDOWS = (2, 4, 8, 16)
POOL_DIM = 256
NA_HEADS = 8
NA_DIM = 512
NA_WIN_ROWS = 8
NA_WIN_COLS = 16
SG_DIM = 256
SG_CHUNK = 128
Q_OFF = POOL_DIM
K_OFF = Q_OFF + NA_DIM
V_OFF = K_OFF + NA_DIM
U_OFF = V_OFF + NA_DIM
G_OFF = U_OFF + SG_DIM
IN_DIM = G_OFF + SG_DIM
N_EXPERTS = 16
GROUP_SIZE = 4
D_EXPERT = 512
EPS = 1e-6

LANES = 128
SUBLANES = 8
HEAD_PAIRS = NA_DIM // LANES
VMEM_LIMIT = 48 * 1024 * 1024

PAIRS = ((0, 1), (0, 2), (0, 3), (1, 2), (1, 3), (2, 3))
N_CLASSES = (N_EXPERTS // GROUP_SIZE) * len(PAIRS)
CLASS_E1 = np.array([4 * g + i for g in range(4) for (i, j) in PAIRS], np.int32)
CLASS_E2 = np.array([4 * g + j for g in range(4) for (i, j) in PAIRS], np.int32)

ROUTE_ROWS = 8
PAYLOAD_W = D_MODEL + LANES
MOE_TM = 256
NEG_BIG = -1e30


def _cparams(sem):
    return pltpu.CompilerParams(dimension_semantics=sem, vmem_limit_bytes=VMEM_LIMIT)


def _dot(a, b):
    return jnp.dot(a, b, preferred_element_type=jnp.float32)


def _dot_nt(a, b):
    return lax.dot_general(a, b, (((1,), (1,)), ((), ())), preferred_element_type=jnp.float32)


def _gelu_tanh(x):
    return 0.5 * x * (1.0 + jnp.tanh(0.7978845608028654 * (x + 0.044715 * (x * x * x))))


def _silu(x):
    return x * (1.0 / (1.0 + jnp.exp(-x)))


def _ada_kernel(cond_ref, w_ref, b_ref, o_ref):
    cond = _silu(cond_ref[...])
    o_ref[...] = jnp.dot(cond, w_ref[...], preferred_element_type=jnp.float32,
                         precision=lax.Precision.HIGHEST) + b_ref[...]


def _ada_call(cond, w_ada, b_ada):
    depth = w_ada.shape[0]
    tn = 1536
    return pl.pallas_call(
        _ada_kernel,
        grid=(depth, 6 * D_MODEL // tn),
        in_specs=[
            pl.BlockSpec((SUBLANES, D_MODEL), lambda l, j: (0, 0)),
            pl.BlockSpec((None, D_MODEL, tn), lambda l, j: (l, 0, j)),
            pl.BlockSpec((None, 1, tn), lambda l, j: (l, 0, j)),
        ],
        out_specs=pl.BlockSpec((None, SUBLANES, tn), lambda l, j: (l, 0, j)),
        out_shape=jax.ShapeDtypeStruct((depth, SUBLANES, 6 * D_MODEL), jnp.float32),
        compiler_params=_cparams(("arbitrary", "arbitrary")),
        name="adaln",
    )(cond, w_ada, b_ada.reshape(depth, 1, 6 * D_MODEL))


def _inproj_kernel(h_ref, n1_ref, sh_ref, sc_ref, w_ref, qg_ref, kg_ref, b512_ref, b256_ref,
                   sgn_ref, sgw_ref, sgb_ref,
                   pool_ref, q_ref, k_ref, v_ref, sg_ref):
    tm = h_ref.shape[0]
    x = h_ref[...]
    ms = jnp.mean(x * x, axis=-1, keepdims=True)
    xn = x * lax.rsqrt(ms + EPS) * n1_ref[...]
    hn = (xn * (1.0 + sc_ref[...]) + sh_ref[...]).astype(jnp.bfloat16)

    pool_ref[...] = _dot(hn, w_ref[:, 0:Q_OFF])

    a_q = _dot(hn, w_ref[:, Q_OFF:K_OFF])
    msq = _dot((a_q * a_q).astype(jnp.bfloat16), b512_ref[...])
    q_ref[...] = (a_q * lax.rsqrt(msq + EPS) * qg_ref[...]).astype(jnp.bfloat16)

    a_k = _dot(hn, w_ref[:, K_OFF:V_OFF])
    msk = _dot((a_k * a_k).astype(jnp.bfloat16), b512_ref[...])
    k_ref[...] = (a_k * lax.rsqrt(msk + EPS) * kg_ref[...]).astype(jnp.bfloat16)

    v_ref[...] = _dot(hn, w_ref[:, V_OFF:U_OFF]).astype(jnp.bfloat16)

    u = _gelu_tanh(_dot(hn, w_ref[:, U_OFF:G_OFF]))
    gv = _gelu_tanh(_dot(hn, w_ref[:, G_OFF:IN_DIM]))
    msv = _dot((gv * gv).astype(jnp.bfloat16), b256_ref[...])
    vn = (gv * lax.rsqrt(msv + EPS) * sgn_ref[...]).astype(jnp.bfloat16)
    low = lax.broadcasted_iota(jnp.int32, (SG_CHUNK, LANES), 1) < HEAD_DIM
    for c in range(tm // SG_CHUNK):
        rows = slice(c * SG_CHUNK, (c + 1) * SG_CHUNK)
        for s in range(SG_DIM // LANES):
            cols = slice(s * LANES, (s + 1) * LANES)
            m = _dot(sgw_ref[s], vn[rows, cols]) + sgb_ref[s]
            mixed = jnp.where(low, m[:SG_CHUNK], m[SG_CHUNK:])
            sg_ref[rows, cols] = (u[rows, cols] * mixed).astype(jnp.bfloat16)


def _inproj_call(h, n1, sh, sc, w_in, qg, kg, b512, b256, sgn, sgw, sgb, tm):
    n = h.shape[0]
    row = lambda i: (i, 0)
    fixed2 = lambda i: (0, 0)
    fixed3 = lambda i: (0, 0, 0)
    vec = lambda w: pl.BlockSpec((1, w), fixed2)
    return pl.pallas_call(
        _inproj_kernel,
        grid=(n // tm,),
        in_specs=[
            pl.BlockSpec((tm, D_MODEL), row),
            vec(D_MODEL), vec(D_MODEL), vec(D_MODEL),
            pl.BlockSpec((D_MODEL, IN_DIM), fixed2),
            vec(NA_DIM), vec(NA_DIM),
            pl.BlockSpec((NA_DIM, NA_DIM), fixed2),
            pl.BlockSpec((SG_DIM, SG_DIM), fixed2),
            vec(SG_DIM),
            pl.BlockSpec((SG_DIM // LANES, 2 * SG_CHUNK, SG_CHUNK), fixed3),
            pl.BlockSpec((SG_DIM // LANES, 2 * SG_CHUNK, LANES), fixed3),
        ],
        out_specs=[
            pl.BlockSpec((tm, POOL_DIM), row),
            pl.BlockSpec((tm, NA_DIM), row),
            pl.BlockSpec((tm, NA_DIM), row),
            pl.BlockSpec((tm, NA_DIM), row),
            pl.BlockSpec((tm, SG_DIM), row),
        ],
        out_shape=[
            jax.ShapeDtypeStruct((n, POOL_DIM), jnp.float32),
            jax.ShapeDtypeStruct((n, NA_DIM), jnp.bfloat16),
            jax.ShapeDtypeStruct((n, NA_DIM), jnp.bfloat16),
            jax.ShapeDtypeStruct((n, NA_DIM), jnp.bfloat16),
            jax.ShapeDtypeStruct((n, SG_DIM), jnp.bfloat16),
        ],
        compiler_params=_cparams(("arbitrary",)),
        name="inproj",
    )(h, n1, sh, sc, w_in, qg, kg, b512, b256, sgn, sgw, sgb)


POOL_HALO = 8


def _pool_kernel(prev_ref, x_ref, next_ref, w_ref, scale_ref, o_ref, xe_ref, *, seq_len):
    tm = x_ref.shape[0]
    i = pl.program_id(0)
    last = pl.num_programs(0) - 1
    xe_ref[0:POOL_HALO, :] = jnp.where(i > 0, prev_ref[...], 0.0)
    xe_ref[POOL_HALO:POOL_HALO + tm, :] = x_ref[...]
    xe_ref[POOL_HALO + tm:, :] = jnp.where(i < last, next_ref[...], 0.0)

    t = i * tm + lax.broadcasted_iota(jnp.int32, (tm, LANES), 0)
    low = lax.broadcasted_iota(jnp.int32, (tm, LANES), 1) < HEAD_DIM

    def count(half):
        return (jnp.minimum(t + half, seq_len) - jnp.maximum(t - half, 0)).astype(jnp.float32)

    def window_sums(xs, n_levels):
        sums = []
        s = xs
        for k in range(n_levels):
            step = 1 << k
            s = s[:-step] + s[step:]
            sums.append(s)
        return sums

    outs = []
    for half_block, windows in enumerate(((2, 4), (8, 16))):
        xs = xe_ref[:, half_block * LANES:(half_block + 1) * LANES]
        sums = window_sums(xs, int(np.log2(windows[1])))
        parts = []
        for w in windows:
            half = w // 2
            s = sums[int(np.log2(w)) - 1][POOL_HALO - half:POOL_HALO - half + tm]
            parts.append(s / count(half))
        mean = jnp.where(low, parts[0], parts[1])
        outs.append(mean - xs[POOL_HALO:POOL_HALO + tm])
    d = jnp.concatenate(outs, axis=-1).astype(jnp.bfloat16)
    o_ref[...] = (_dot(d, w_ref[...]) * scale_ref[...]).astype(jnp.bfloat16)


def _pool_call(a_pool, w_bd, scale, tm):
    n = a_pool.shape[0]
    nb8 = n // POOL_HALO
    r8 = tm // POOL_HALO
    return pl.pallas_call(
        functools.partial(_pool_kernel, seq_len=n),
        grid=(n // tm,),
        in_specs=[
            pl.BlockSpec((POOL_HALO, POOL_DIM), lambda i: (jnp.maximum(i * r8 - 1, 0), 0)),
            pl.BlockSpec((tm, POOL_DIM), lambda i: (i, 0)),
            pl.BlockSpec((POOL_HALO, POOL_DIM), lambda i: (jnp.minimum((i + 1) * r8, nb8 - 1), 0)),
            pl.BlockSpec((POOL_DIM, POOL_DIM), lambda i: (0, 0)),
            pl.BlockSpec((1, POOL_DIM), lambda i: (0, 0)),
        ],
        out_specs=pl.BlockSpec((tm, POOL_DIM), lambda i: (i, 0)),
        out_shape=jax.ShapeDtypeStruct((n, POOL_DIM), jnp.bfloat16),
        scratch_shapes=[pltpu.VMEM((tm + 2 * POOL_HALO, POOL_DIM), jnp.float32)],
        compiler_params=_cparams(("arbitrary",)),
        name="pool",
    )(a_pool, a_pool, a_pool, w_bd, scale)


NA_ROWS_PER_BLOCK = 8
NA_BLOCK = NA_ROWS_PER_BLOCK * GRID_W
NA_HALO = (NA_WIN_ROWS // 2) * GRID_W
NA_LOCAL = NA_WIN_ROWS * GRID_W


def _stack_heads(x, low):
    zero = jnp.zeros_like(x)
    return jnp.concatenate([jnp.where(low, x, zero), jnp.where(low, zero, x)], axis=0)


def _natten_kernel(q_ref, kp_ref, kc_ref, kn_ref, vp_ref, vc_ref, vn_ref, kx_ref, vx_ref, bias_ref,
                   o_ref, kwin_ref, vwin_ref, *, grid_rows):
    b = pl.program_id(1)
    kwin_ref[0:NA_HALO, :] = kp_ref[...]
    kwin_ref[NA_HALO:NA_HALO + NA_BLOCK, :] = kc_ref[...]
    kwin_ref[NA_HALO + NA_BLOCK:, :] = kn_ref[...]
    vwin_ref[0:NA_HALO, :] = vp_ref[...]
    vwin_ref[NA_HALO:NA_HALO + NA_BLOCK, :] = vc_ref[...]
    vwin_ref[NA_HALO + NA_BLOCK:, :] = vn_ref[...]
    low_q = lax.broadcasted_iota(jnp.int32, (GRID_W, LANES), 1) < HEAD_DIM

    def one_row(j, carry):
        r = b * NA_ROWS_PER_BLOCK + j
        rs = jnp.clip(r - NA_WIN_ROWS // 2, 0, grid_rows - NA_WIN_ROWS)
        off = rs - r + NA_WIN_ROWS - 1
        start = pl.multiple_of((rs - b * NA_ROWS_PER_BLOCK + NA_WIN_ROWS // 2) * GRID_W, GRID_W)
        qrow = pl.multiple_of(j * GRID_W, GRID_W)
        lhs = _stack_heads(q_ref[pl.ds(qrow, GRID_W), :], low_q)
        kl = kwin_ref[pl.ds(start, NA_LOCAL), :]
        vl = vwin_ref[pl.ds(start, NA_LOCAL), :]
        s_loc = _dot_nt(lhs, kl) + bias_ref[off]
        s_ctx = _dot_nt(lhs, kx_ref[...])
        m = jnp.maximum(jnp.max(s_loc, axis=-1, keepdims=True), jnp.max(s_ctx, axis=-1, keepdims=True))
        p_loc = jnp.exp(s_loc - m)
        p_ctx = jnp.exp(s_ctx - m)
        denom = jnp.sum(p_loc, axis=-1, keepdims=True) + jnp.sum(p_ctx, axis=-1, keepdims=True)
        o = _dot(p_loc.astype(jnp.bfloat16), vl) + _dot(p_ctx.astype(jnp.bfloat16), vx_ref[...])
        o = o * (1.0 / denom)
        o_ref[pl.ds(qrow, GRID_W), :] = jnp.where(low_q, o[:GRID_W], o[GRID_W:]).astype(jnp.bfloat16)
        return carry

    lax.fori_loop(0, NA_ROWS_PER_BLOCK, one_row, 0)


def _natten_call(q, k, v, k_ctx, v_ctx, bias):
    n = q.shape[0]
    grid_rows = n // GRID_W
    nblk = n // NA_BLOCK
    nhalo = n // NA_HALO
    hb = NA_BLOCK // NA_HALO
    cur = pl.BlockSpec((NA_BLOCK, LANES), lambda p, b: (b, p))
    prev = pl.BlockSpec((NA_HALO, LANES), lambda p, b: (jnp.maximum(b * hb - 1, 0), p))
    nxt = pl.BlockSpec((NA_HALO, LANES), lambda p, b: (jnp.minimum((b + 1) * hb, nhalo - 1), p))
    ctx = pl.BlockSpec((k_ctx.shape[0], LANES), lambda p, b: (0, p))
    return pl.pallas_call(
        functools.partial(_natten_kernel, grid_rows=grid_rows),
        grid=(HEAD_PAIRS, nblk),
        in_specs=[cur, prev, cur, nxt, prev, cur, nxt, ctx, ctx,
                  pl.BlockSpec((None, NA_WIN_ROWS, 2 * GRID_W, NA_LOCAL), lambda p, b: (p, 0, 0, 0))],
        out_specs=cur,
        out_shape=jax.ShapeDtypeStruct((n, NA_DIM), jnp.bfloat16),
        scratch_shapes=[pltpu.VMEM((NA_BLOCK + 2 * NA_HALO, LANES), jnp.bfloat16),
                        pltpu.VMEM((NA_BLOCK + 2 * NA_HALO, LANES), jnp.bfloat16)],
        compiler_params=_cparams(("arbitrary", "arbitrary")),
        name="natten",
    )(q, k, k, k, v, v, v, k_ctx, v_ctx, bias)


def _natten_bias(rpb):
    cols = np.arange(GRID_W)
    col_start = np.clip(cols - NA_WIN_COLS // 2, 0, GRID_W - NA_WIN_COLS)
    kc = np.arange(GRID_W)
    in_win = (kc[None, :] >= col_start[:, None]) & (kc[None, :] < col_start[:, None] + NA_WIN_COLS)
    dc = np.clip(kc[None, :] - cols[:, None] + NA_WIN_COLS - 1, 0, 2 * NA_WIN_COLS - 2)
    dr = np.arange(NA_WIN_ROWS)[:, None] + np.arange(NA_WIN_ROWS)[None, :]
    t = rpb[:, dr[:, :, None, None], dc[None, None, :, :]]
    t = jnp.where(in_win[None, None, None], t, NEG_BIG)
    t = jnp.transpose(t, (0, 1, 3, 2, 4)).reshape(NA_HEADS, NA_WIN_ROWS, GRID_W, NA_LOCAL)
    t = t.reshape(HEAD_PAIRS, 2, NA_WIN_ROWS, GRID_W, NA_LOCAL)
    return jnp.transpose(t, (0, 2, 1, 3, 4)).reshape(HEAD_PAIRS, NA_WIN_ROWS, 2 * GRID_W, NA_LOCAL)


def _ctxatt_kernel(q_ref, k_ref, v_ref, o_ref):
    lc = q_ref.shape[0]
    low = lax.broadcasted_iota(jnp.int32, (lc, LANES), 1) < HEAD_DIM
    lhs = _stack_heads(q_ref[...], low)
    s = _dot_nt(lhs, k_ref[...])
    m = jnp.max(s, axis=-1, keepdims=True)
    p = jnp.exp(s - m)
    denom = jnp.sum(p, axis=-1, keepdims=True)
    o = _dot(p.astype(jnp.bfloat16), v_ref[...]) * (1.0 / denom)
    o_ref[...] = jnp.where(low, o[:lc], o[lc:]).astype(jnp.bfloat16)


def _ctxatt_call(q, k, v):
    lc = q.shape[0]
    spec = pl.BlockSpec((lc, LANES), lambda p: (0, p))
    return pl.pallas_call(
        _ctxatt_kernel,
        grid=(HEAD_PAIRS,),
        in_specs=[spec, spec, spec],
        out_specs=spec,
        out_shape=jax.ShapeDtypeStruct((lc, NA_DIM), jnp.bfloat16),
        compiler_params=_cparams(("arbitrary",)),
        name="ctxatt",
    )(q, k, v)


def _outproj_kernel(h_ref, mp_ref, att_ref, sg_ref, wo_ref, g1_ref, n2_ref, sh_ref, sc_ref,
                    wr_ref, br_ref, h1_ref, pay_ref, route_ref):
    tm = h_ref.shape[0]
    mix = (_dot(mp_ref[...], wo_ref[0:POOL_DIM, :])
           + _dot(att_ref[...], wo_ref[POOL_DIM:POOL_DIM + NA_DIM, :])
           + _dot(sg_ref[...], wo_ref[POOL_DIM + NA_DIM:, :]))
    h1 = h_ref[...] + g1_ref[...] * mix
    h1_ref[...] = h1
    ms = jnp.mean(h1 * h1, axis=-1, keepdims=True)
    hm = h1 * lax.rsqrt(ms + EPS) * n2_ref[...] * (1.0 + sc_ref[...]) + sh_ref[...]
    pay_ref[:, 0:D_MODEL] = hm

    hm_hi = hm.astype(jnp.bfloat16)
    hm_lo = (hm - hm_hi.astype(jnp.float32)).astype(jnp.bfloat16)
    lt = _dot_nt(wr_ref[...], hm_hi)
    logits = (lt[:N_EXPERTS] + lt[N_EXPERTS:] + _dot_nt(wr_ref[0:N_EXPERTS, :], hm_lo) + br_ref[...])
    e = jnp.exp(logits - jnp.max(logits, axis=0, keepdims=True))

    best = ga = gb = e1 = e2 = cls = None
    for c in range(N_CLASSES):
        a, b2 = int(CLASS_E1[c]), int(CLASS_E2[c])
        ea, eb = e[a:a + 1, :], e[b2:b2 + 1, :]
        s = ea + eb
        if best is None:
            best, ga, gb = s, ea, eb
            e1 = jnp.full_like(s, float(a))
            e2 = jnp.full_like(s, float(b2))
            cls = jnp.zeros_like(s)
        else:
            better = s > best
            best = jnp.where(better, s, best)
            ga = jnp.where(better, ea, ga)
            gb = jnp.where(better, eb, gb)
            e1 = jnp.where(better, float(a), e1)
            e2 = jnp.where(better, float(b2), e2)
            cls = jnp.where(better, float(c), cls)
    inv = 1.0 / best
    row = lax.broadcasted_iota(jnp.int32, (ROUTE_ROWS, tm), 0)
    rec = jnp.where(row == 0, ga * inv,
          jnp.where(row == 1, gb * inv,
          jnp.where(row == 2, e1,
          jnp.where(row == 3, e2,
          jnp.where(row == 4, cls, 0.0)))))
    route_ref[...] = rec
    wide = jnp.concatenate([rec, jnp.zeros((LANES - ROUTE_ROWS, tm), jnp.float32)], axis=0)
    pay_ref[:, D_MODEL:] = wide.T


def _outproj_call(h, mp, att, sg, w_out, g1, n2, sh2, sc2, wr, br, tm):
    n = h.shape[0]
    row = lambda i: (i, 0)
    fixed = lambda i: (0, 0)
    vec = pl.BlockSpec((1, D_MODEL), fixed)
    return pl.pallas_call(
        _outproj_kernel,
        grid=(n // tm,),
        in_specs=[
            pl.BlockSpec((tm, D_MODEL), row),
            pl.BlockSpec((tm, POOL_DIM), row),
            pl.BlockSpec((tm, NA_DIM), row),
            pl.BlockSpec((tm, SG_DIM), row),
            pl.BlockSpec((D_MODEL, D_MODEL), fixed),
            vec, vec, vec, vec,
            pl.BlockSpec((2 * N_EXPERTS, D_MODEL), fixed),
            pl.BlockSpec((N_EXPERTS, 1), fixed),
        ],
        out_specs=[
            pl.BlockSpec((tm, D_MODEL), row),
            pl.BlockSpec((tm, PAYLOAD_W), row),
            pl.BlockSpec((ROUTE_ROWS, tm), lambda i: (0, i)),
        ],
        out_shape=[
            jax.ShapeDtypeStruct((n, D_MODEL), jnp.float32),
            jax.ShapeDtypeStruct((n, PAYLOAD_W), jnp.float32),
            jax.ShapeDtypeStruct((ROUTE_ROWS, n), jnp.float32),
        ],
        compiler_params=_cparams(("arbitrary",)),
        name="outproj",
    )(h, mp, att, sg, w_out, g1, n2, sh2, sc2, wr, br)


def _expert_pair(x, ga, gb, wga, wua, wda, wgb, wub, wdb):
    ha = (_silu(_dot(x, wga)) * _dot(x, wua) * ga).astype(jnp.bfloat16)
    hb = (_silu(_dot(x, wgb)) * _dot(x, wub) * gb).astype(jnp.bfloat16)
    return _dot(ha, wda) + _dot(hb, wdb)


def _moe_sorted_kernel(blk_ref, e1_ref, e2_ref, nact_ref, pay_ref,
                       wga_ref, wua_ref, wda_ref, wgb_ref, wub_ref, wdb_ref, o_ref):
    i = pl.program_id(0)

    @pl.when(i < nact_ref[0])
    def _():
        x = pay_ref[:, 0:D_MODEL].astype(jnp.bfloat16)
        ga = pay_ref[:, D_MODEL:D_MODEL + 1]
        gb = pay_ref[:, D_MODEL + 1:D_MODEL + 2]
        o_ref[...] = _expert_pair(x, ga, gb, wga_ref[...], wua_ref[...], wda_ref[...],
                                  wgb_ref[...], wub_ref[...], wdb_ref[...])


def _moe_sorted_call(blk, e1, e2, nact, pay_sorted, wg, wu, wd):
    n_tiles = blk.shape[0]
    rows = lambda i, blk, e1, e2, na: (blk[i], 0)
    wa = lambda i, blk, e1, e2, na: (e1[i], 0, 0)
    wb = lambda i, blk, e1, e2, na: (e2[i], 0, 0)
    up = lambda m: pl.BlockSpec((None, D_MODEL, D_EXPERT), m)
    down = lambda m: pl.BlockSpec((None, D_EXPERT, D_MODEL), m)
    return pl.pallas_call(
        _moe_sorted_kernel,
        grid_spec=pltpu.PrefetchScalarGridSpec(
            num_scalar_prefetch=4,
            grid=(n_tiles,),
            in_specs=[pl.BlockSpec((MOE_TM, PAYLOAD_W), rows),
                      up(wa), up(wa), down(wa), up(wb), up(wb), down(wb)],
            out_specs=pl.BlockSpec((MOE_TM, D_MODEL), rows),
        ),
        out_shape=jax.ShapeDtypeStruct((n_tiles * MOE_TM, D_MODEL), jnp.float32),
        compiler_params=_cparams(("arbitrary",)),
        name="moe_sorted",
    )(blk, e1, e2, nact, pay_sorted, wg, wu, wd, wg, wu, wd)


def _moe_dense_kernel(pay_ref, wg_ref, wu_ref, wd_ref, o_ref):
    e = pl.program_id(0)

    @pl.when(e == 0)
    def _():
        o_ref[...] = jnp.zeros_like(o_ref)

    x = pay_ref[:, 0:D_MODEL].astype(jnp.bfloat16)
    ef = e.astype(jnp.float32)
    gate = (jnp.where(pay_ref[:, D_MODEL + 2:D_MODEL + 3] == ef, pay_ref[:, D_MODEL:D_MODEL + 1], 0.0)
            + jnp.where(pay_ref[:, D_MODEL + 3:D_MODEL + 4] == ef, pay_ref[:, D_MODEL + 1:D_MODEL + 2], 0.0))
    he = (_silu(_dot(x, wg_ref[...])) * _dot(x, wu_ref[...]) * gate).astype(jnp.bfloat16)
    o_ref[...] += _dot(he, wd_ref[...])


def _moe_dense_call(pay, wg, wu, wd):
    n = pay.shape[0]
    return pl.pallas_call(
        _moe_dense_kernel,
        grid=(N_EXPERTS,),
        in_specs=[pl.BlockSpec((n, PAYLOAD_W), lambda e: (0, 0)),
                  pl.BlockSpec((None, D_MODEL, D_EXPERT), lambda e: (e, 0, 0)),
                  pl.BlockSpec((None, D_MODEL, D_EXPERT), lambda e: (e, 0, 0)),
                  pl.BlockSpec((None, D_EXPERT, D_MODEL), lambda e: (e, 0, 0))],
        out_specs=pl.BlockSpec((n, D_MODEL), lambda e: (0, 0)),
        out_shape=jax.ShapeDtypeStruct((n, D_MODEL), jnp.float32),
        compiler_params=_cparams(("arbitrary",)),
        name="moe_dense",
    )(pay, wg, wu, wd)


def _residual_kernel(h_ref, y_ref, g_ref, o_ref):
    o_ref[...] = h_ref[...] + g_ref[...] * y_ref[...]


def _residual_call(h, y, g, tm):
    n = h.shape[0]
    row = pl.BlockSpec((tm, D_MODEL), lambda i: (i, 0))
    return pl.pallas_call(
        _residual_kernel,
        grid=(n // tm,),
        in_specs=[row, row, pl.BlockSpec((1, D_MODEL), lambda i: (0, 0))],
        out_specs=row,
        out_shape=jax.ShapeDtypeStruct((n, D_MODEL), jnp.float32),
        compiler_params=_cparams(("arbitrary",)),
        name="residual",
    )(h, y, g)


def _routing_plan(cls, n_tiles):
    onehot = (cls[:, None] == jnp.arange(N_CLASSES, dtype=jnp.int32)[None, :]).astype(jnp.int32)
    counts = jnp.sum(onehot, axis=0)
    rank = jnp.sum((jnp.cumsum(onehot, axis=0) - onehot) * onehot, axis=1)
    tiles = (counts + MOE_TM - 1) // MOE_TM
    tile_end = jnp.cumsum(tiles)
    tile_start = tile_end - tiles
    dest = jnp.sum(onehot * tile_start[None, :], axis=1) * MOE_TM + rank
    nact = tile_end[-1]
    blk = jnp.minimum(jnp.arange(n_tiles, dtype=jnp.int32), nact - 1)
    tile_cls = jnp.sum((blk[:, None] >= tile_end[None, :]).astype(jnp.int32), axis=1)
    e1 = jnp.asarray(CLASS_E1)[tile_cls]
    e2 = jnp.asarray(CLASS_E2)[tile_cls]
    return dest.astype(jnp.int32), blk, e1, e2, nact.reshape(1).astype(jnp.int32)


def _block_diag_mean(width):
    idx = np.arange(width) // HEAD_DIM
    return jnp.asarray((idx[:, None] == idx[None, :]).astype(np.float32) / HEAD_DIM, jnp.bfloat16)


def _row_tile(n):
    return 512 if n % 512 == 0 else 256


def kernel(x, c, ctx, c_ctx, w_ada, b_ada, norm1, w_in, pool_w, pool_scale, q_norm, k_norm, rpb,
           sg_w, sg_b, sg_norm, w_out, norm2, w_router, b_router, w_gate, w_up, w_down):
    depth = w_ada.shape[0]
    n = x.shape[1]
    lc = ctx.shape[1]
    bf = jnp.bfloat16
    h_lat = x[0]
    h_ctx = ctx[0]

    cond = jnp.zeros((SUBLANES, D_MODEL), jnp.float32).at[0].set(c[0]).at[1].set(c_ctx)
    mod = _ada_call(cond, w_ada, b_ada)

    b512 = _block_diag_mean(NA_DIM)
    b256 = _block_diag_mean(SG_DIM)
    wr_t = w_router.T
    wr_hi = wr_t.astype(bf)
    wr_lo = (wr_t - wr_hi.astype(jnp.float32)).astype(bf)
    wr = jnp.concatenate([wr_hi, wr_lo], axis=0)
    br = b_router.reshape(N_EXPERTS, 1)
    n_tiles = n // MOE_TM + N_CLASSES

    for l in range(depth):
        last = l == depth - 1
        w_in_l = w_in[l].astype(bf)
        w_out_l = w_out[l].astype(bf)
        wg_l, wu_l, wd_l = w_gate[l].astype(bf), w_up[l].astype(bf), w_down[l].astype(bf)
        qg = (q_norm[l] * (HEAD_DIM ** -0.5)).reshape(1, NA_DIM)
        kg = k_norm[l].reshape(1, NA_DIM)
        sgn = sg_norm[l].reshape(1, SG_DIM)
        sgw = sg_w[l].astype(bf).reshape(SG_DIM // LANES, 2 * SG_CHUNK, SG_CHUNK)
        sgb = jnp.broadcast_to(sg_b[l].reshape(SG_DIM // LANES, 2 * SG_CHUNK, 1),
                               (SG_DIM // LANES, 2 * SG_CHUNK, LANES))
        pool_bd = jax.scipy.linalg.block_diag(*[pool_w[l, g] for g in range(len(POOL_WINDOWS))]).astype(bf)
        pscale = pool_scale[l].reshape(1, POOL_DIM)
        bias = _natten_bias(rpb[l])
        n1 = norm1[l].reshape(1, D_MODEL)
        n2 = norm2[l].reshape(1, D_MODEL)

        def mods(row):
            return [mod[l, row:row + 1, i * D_MODEL:(i + 1) * D_MODEL] for i in range(6)]

        sh1, sc1, g1, sh2, sc2, g2 = mods(0)
        csh1, csc1, cg1, csh2, csc2, cg2 = mods(1)

        tc = _row_tile(lc)
        pool_c, q_c, k_c, v_c, sg_c = _inproj_call(h_ctx, n1, csh1, csc1, w_in_l, qg, kg, b512, b256,
                                                   sgn, sgw, sgb, tc)

        tm = _row_tile(n)
        pool_a, q, k, v, sg = _inproj_call(h_lat, n1, sh1, sc1, w_in_l, qg, kg, b512, b256, sgn, sgw, sgb, tm)
        mix_pool = _pool_call(pool_a, pool_bd, pscale, tm)
        att = _natten_call(q, k, v, k_c, v_c, bias)
        h1, pay, route = _outproj_call(h_lat, mix_pool, att, sg, w_out_l, g1, n2, sh2, sc2, wr, br, tm)
        cls = route[4].astype(jnp.int32)
        dest, blk, e1, e2, nact = _routing_plan(cls, n_tiles)
        pay_sorted = jnp.zeros((n_tiles * MOE_TM, PAYLOAD_W), jnp.float32).at[dest].set(pay)
        y_sorted = _moe_sorted_call(blk, e1, e2, nact, pay_sorted, wg_l, wu_l, wd_l)
        y = y_sorted[dest]
        h_lat = _residual_call(h1, y, g2, tm)

        if not last:
            mix_pool_c = _pool_call(pool_c, pool_bd, pscale, tc)
            att_c = _ctxatt_call(q_c, k_c, v_c)
            h1_c, pay_c, _ = _outproj_call(h_ctx, mix_pool_c, att_c, sg_c, w_out_l, cg1, n2, csh2, csc2,
                                           wr, br, tc)
            y_c = _moe_dense_call(pay_c, wg_l, wu_l, wd_l)
            h_ctx = _residual_call(h1_c, y_c, cg2, tc)

    return h_lat[None]
```

```python
import functools

import jax
import jax.numpy as jnp
import numpy as np
from jax import lax
from jax.experimental import pallas as pl
from jax.experimental.pallas import tpu as pltpu
from jax.experimental.pallas import tpu_sc as plsc

D_MODEL = 1024
GRID_W = 64
HEAD_DIM = 64
POOL_WINDOWS = (2, 4, 8, 16)
POOL_DIM = 256
NA_HEADS = 8
NA_DIM = 512
NA_WIN_ROWS = 8
NA_WIN_COLS = 16
SG_DIM = 256
SG_CHUNK = 128
Q_OFF = POOL_DIM
K_OFF = Q_OFF + NA_DIM
V_OFF = K_OFF + NA_DIM
U_OFF = V_OFF + NA_DIM
G_OFF = U_OFF + SG_DIM
IN_DIM = G_OFF + SG_DIM
N_EXPERTS = 16
GROUP_SIZE = 4
D_EXPERT = 512
EPS = 1e-6

LANES = 128
SUBLANES = 8
HEAD_PAIRS = NA_DIM // LANES
VMEM_LIMIT = 48 * 1024 * 1024

PAIRS = ((0, 1), (0, 2), (0, 3), (1, 2), (1, 3), (2, 3))
N_CLASSES = (N_EXPERTS // GROUP_SIZE) * len(PAIRS)
CLASS_E1 = np.array([4 * g + i for g in range(4) for (i, j) in PAIRS], np.int32)
CLASS_E2 = np.array([4 * g + j for g in range(4) for (i, j) in PAIRS], np.int32)

ROUTE_ROWS = 8
PAYLOAD_W = D_MODEL + LANES
MOE_TM = 256
NEG_BIG = -1e30
LOG2_E = 1.4426950408889634


def _cparams(sem):
    return pltpu.CompilerParams(dimension_semantics=sem, vmem_limit_bytes=VMEM_LIMIT)


def _dot(a, b):
    return jnp.dot(a, b, preferred_element_type=jnp.float32)


def _dot_nt(a, b):
    return lax.dot_general(a, b, (((1,), (1,)), ((), ())), preferred_element_type=jnp.float32)


def _gelu_tanh(x):
    return 0.5 * x * (1.0 + jnp.tanh(0.7978845608028654 * (x + 0.044715 * (x * x * x))))


def _silu(x):
    return x * (1.0 / (1.0 + jnp.exp(-x)))


def _ada_kernel(cond_ref, w_ref, b_ref, o_ref):
    cond = _silu(cond_ref[...])
    o_ref[...] = jnp.dot(cond, w_ref[...], preferred_element_type=jnp.float32,
                         precision=lax.Precision.HIGHEST) + b_ref[...]


def _ada_call(cond, w_ada, b_ada):
    depth = w_ada.shape[0]
    tn = 1536
    return pl.pallas_call(
        _ada_kernel,
        grid=(depth, 6 * D_MODEL // tn),
        in_specs=[
            pl.BlockSpec((SUBLANES, D_MODEL), lambda l, j: (0, 0)),
            pl.BlockSpec((None, D_MODEL, tn), lambda l, j: (l, 0, j)),
            pl.BlockSpec((None, 1, tn), lambda l, j: (l, 0, j)),
        ],
        out_specs=pl.BlockSpec((None, SUBLANES, tn), lambda l, j: (l, 0, j)),
        out_shape=jax.ShapeDtypeStruct((depth, SUBLANES, 6 * D_MODEL), jnp.float32),
        compiler_params=_cparams(("arbitrary", "arbitrary")),
        name="adaln",
    )(cond, w_ada, b_ada.reshape(depth, 1, 6 * D_MODEL))


def _inproj_kernel(h_ref, n1_ref, sh_ref, sc_ref, w_ref, qg_ref, kg_ref, b512_ref, b256_ref,
                   sgn_ref, sgw_ref, sgb_ref,
                   pool_ref, q_ref, k_ref, v_ref, sg_ref):
    tm = h_ref.shape[0]
    x = h_ref[...]
    ms = jnp.mean(x * x, axis=-1, keepdims=True)
    xn = x * lax.rsqrt(ms + EPS) * n1_ref[...]
    hn = (xn * (1.0 + sc_ref[...]) + sh_ref[...]).astype(jnp.bfloat16)

    pool_ref[...] = _dot(hn, w_ref[:, 0:Q_OFF])

    a_q = _dot(hn, w_ref[:, Q_OFF:K_OFF])
    msq = _dot((a_q * a_q).astype(jnp.bfloat16), b512_ref[...])
    q_ref[...] = (a_q * lax.rsqrt(msq + EPS) * qg_ref[...]).astype(jnp.bfloat16)

    a_k = _dot(hn, w_ref[:, K_OFF:V_OFF])
    msk = _dot((a_k * a_k).astype(jnp.bfloat16), b512_ref[...])
    k_ref[...] = (a_k * lax.rsqrt(msk + EPS) * kg_ref[...]).astype(jnp.bfloat16)

    v_ref[...] = _dot(hn, w_ref[:, V_OFF:U_OFF]).astype(jnp.bfloat16)

    u = _gelu_tanh(_dot(hn, w_ref[:, U_OFF:G_OFF]))
    gv = _gelu_tanh(_dot(hn, w_ref[:, G_OFF:IN_DIM]))
    msv = _dot((gv * gv).astype(jnp.bfloat16), b256_ref[...])
    vn = (gv * lax.rsqrt(msv + EPS) * sgn_ref[...]).astype(jnp.bfloat16)
    low = lax.broadcasted_iota(jnp.int32, (SG_CHUNK, LANES), 1) < HEAD_DIM
    for c in range(tm // SG_CHUNK):
        rows = slice(c * SG_CHUNK, (c + 1) * SG_CHUNK)
        for s in range(SG_DIM // LANES):
            cols = slice(s * LANES, (s + 1) * LANES)
            m = _dot(sgw_ref[s], vn[rows, cols]) + sgb_ref[s]
            mixed = jnp.where(low, m[:SG_CHUNK], m[SG_CHUNK:])
            sg_ref[rows, cols] = (u[rows, cols] * mixed).astype(jnp.bfloat16)


def _inproj_call(h, n1, sh, sc, w_in, qg, kg, b512, b256, sgn, sgw, sgb, tm):
    n = h.shape[0]
    row = lambda i: (i, 0)
    fixed2 = lambda i: (0, 0)
    fixed3 = lambda i: (0, 0, 0)
    vec = lambda w: pl.BlockSpec((1, w), fixed2)
    return pl.pallas_call(
        _inproj_kernel,
        grid=(n // tm,),
        in_specs=[
            pl.BlockSpec((tm, D_MODEL), row),
            vec(D_MODEL), vec(D_MODEL), vec(D_MODEL),
            pl.BlockSpec((D_MODEL, IN_DIM), fixed2),
            vec(NA_DIM), vec(NA_DIM),
            pl.BlockSpec((NA_DIM, NA_DIM), fixed2),
            pl.BlockSpec((SG_DIM, SG_DIM), fixed2),
            vec(SG_DIM),
            pl.BlockSpec((SG_DIM // LANES, 2 * SG_CHUNK, SG_CHUNK), fixed3),
            pl.BlockSpec((SG_DIM // LANES, 2 * SG_CHUNK, LANES), fixed3),
        ],
        out_specs=[
            pl.BlockSpec((tm, POOL_DIM), row),
            pl.BlockSpec((tm, NA_DIM), row),
            pl.BlockSpec((tm, NA_DIM), row),
            pl.BlockSpec((tm, NA_DIM), row),
            pl.BlockSpec((tm, SG_DIM), row),
        ],
        out_shape=[
            jax.ShapeDtypeStruct((n, POOL_DIM), jnp.float32),
            jax.ShapeDtypeStruct((n, NA_DIM), jnp.bfloat16),
            jax.ShapeDtypeStruct((n, NA_DIM), jnp.bfloat16),
            jax.ShapeDtypeStruct((n, NA_DIM), jnp.bfloat16),
            jax.ShapeDtypeStruct((n, SG_DIM), jnp.bfloat16),
        ],
        compiler_params=_cparams(("arbitrary",)),
        name="inproj",
    )(h, n1, sh, sc, w_in, qg, kg, b512, b256, sgn, sgw, sgb)


POOL_HALO = 8


def _pool_kernel(prev_ref, x_ref, next_ref, w_ref, scale_ref, o_ref, xe_ref, *, seq_len):
    tm = x_ref.shape[0]
    i = pl.program_id(0)
    last = pl.num_programs(0) - 1
    xe_ref[0:POOL_HALO, :] = jnp.where(i > 0, prev_ref[...], 0.0)
    xe_ref[POOL_HALO:POOL_HALO + tm, :] = x_ref[...]
    xe_ref[POOL_HALO + tm:, :] = jnp.where(i < last, next_ref[...], 0.0)

    t = i * tm + lax.broadcasted_iota(jnp.int32, (tm, LANES), 0)
    low = lax.broadcasted_iota(jnp.int32, (tm, LANES), 1) < HEAD_DIM

    def count(half):
        return (jnp.minimum(t + half, seq_len) - jnp.maximum(t - half, 0)).astype(jnp.float32)

    def window_sums(xs, n_levels):
        sums = []
        s = xs
        for k in range(n_levels):
            step = 1 << k
            s = s[:-step] + s[step:]
            sums.append(s)
        return sums

    outs = []
    for half_block, windows in enumerate(((2, 4), (8, 16))):
        xs = xe_ref[:, half_block * LANES:(half_block + 1) * LANES]
        sums = window_sums(xs, int(np.log2(windows[1])))
        parts = []
        for w in windows:
            half = w // 2
            s = sums[int(np.log2(w)) - 1][POOL_HALO - half:POOL_HALO - half + tm]
            parts.append(s / count(half))
        mean = jnp.where(low, parts[0], parts[1])
        outs.append(mean - xs[POOL_HALO:POOL_HALO + tm])
    d = jnp.concatenate(outs, axis=-1).astype(jnp.bfloat16)
    o_ref[...] = (_dot(d, w_ref[...]) * scale_ref[...]).astype(jnp.bfloat16)


def _pool_call(a_pool, w_bd, scale, tm):
    n = a_pool.shape[0]
    nb8 = n // POOL_HALO
    r8 = tm // POOL_HALO
    return pl.pallas_call(
        functools.partial(_pool_kernel, seq_len=n),
        grid=(n // tm,),
        in_specs=[
            pl.BlockSpec((POOL_HALO, POOL_DIM), lambda i: (jnp.maximum(i * r8 - 1, 0), 0)),
            pl.BlockSpec((tm, POOL_DIM), lambda i: (i, 0)),
            pl.BlockSpec((POOL_HALO, POOL_DIM), lambda i: (jnp.minimum((i + 1) * r8, nb8 - 1), 0)),
            pl.BlockSpec((POOL_DIM, POOL_DIM), lambda i: (0, 0)),
            pl.BlockSpec((1, POOL_DIM), lambda i: (0, 0)),
        ],
        out_specs=pl.BlockSpec((tm, POOL_DIM), lambda i: (i, 0)),
        out_shape=jax.ShapeDtypeStruct((n, POOL_DIM), jnp.bfloat16),
        scratch_shapes=[pltpu.VMEM((tm + 2 * POOL_HALO, POOL_DIM), jnp.float32)],
        compiler_params=_cparams(("arbitrary",)),
        name="pool",
    )(a_pool, a_pool, a_pool, w_bd, scale)


NA_ROWS_PER_BLOCK = 16
NA_GROUP_ROWS = 4
NA_WINDOW_ROWS = NA_GROUP_ROWS + NA_WIN_ROWS
NA_BLOCK = NA_ROWS_PER_BLOCK * GRID_W
NA_GROUP = NA_GROUP_ROWS * GRID_W
NA_HALO = (NA_WIN_ROWS // 2) * GRID_W
NA_WINDOW = NA_WINDOW_ROWS * GRID_W
NA_EDGE_FIRST, NA_EDGE_NONE, NA_EDGE_LAST = 0, 1, 2


def _stack_heads(x, low):
    zero = jnp.zeros_like(x)
    return jnp.concatenate([jnp.where(low, x, zero), jnp.where(low, zero, x)], axis=0)


def _natten_kernel(q_ref, kp_ref, kc_ref, kn_ref, vp_ref, vc_ref, vn_ref, kx_ref, vx_ref, bias_ref,
                   o_ref, kwin_ref, vwin_ref, vxe_ref, *, grid_rows):
    b = pl.program_id(1)
    kwin_ref[0:NA_HALO, :] = kp_ref[...]
    kwin_ref[NA_HALO:NA_HALO + NA_BLOCK, :] = kc_ref[...]
    kwin_ref[NA_HALO + NA_BLOCK:, :] = kn_ref[...]
    vwin_ref[0:NA_HALO, 0:LANES] = vp_ref[...]
    vwin_ref[NA_HALO:NA_HALO + NA_BLOCK, 0:LANES] = vc_ref[...]
    vwin_ref[NA_HALO + NA_BLOCK:, 0:LANES] = vn_ref[...]
    vwin_ref[:, LANES:] = jnp.ones((vwin_ref.shape[0], LANES), jnp.bfloat16)
    vxe_ref[:, 0:LANES] = vx_ref[...]
    vxe_ref[:, LANES:] = jnp.ones((vxe_ref.shape[0], LANES), jnp.bfloat16)
    low_q = lax.broadcasted_iota(jnp.int32, (NA_GROUP, LANES), 1) < HEAD_DIM

    for g in range(NA_ROWS_PER_BLOCK // NA_GROUP_ROWS):
        r0 = b * NA_ROWS_PER_BLOCK + g * NA_GROUP_ROWS
        ws = jnp.clip(r0 - NA_WIN_ROWS // 2, 0, grid_rows - NA_WINDOW_ROWS)
        edge = jnp.where(r0 == 0, NA_EDGE_FIRST,
                         jnp.where(r0 == grid_rows - NA_GROUP_ROWS, NA_EDGE_LAST, NA_EDGE_NONE))
        start = pl.multiple_of((ws - b * NA_ROWS_PER_BLOCK + NA_WIN_ROWS // 2) * GRID_W, GRID_W)
        qrows = slice(g * NA_GROUP, (g + 1) * NA_GROUP)
        lhs = _stack_heads(q_ref[qrows, :], low_q)
        kl = kwin_ref[pl.ds(start, NA_WINDOW), :]
        vl = vwin_ref[pl.ds(start, NA_WINDOW), :]
        s = jnp.concatenate([_dot_nt(lhs, kl) + bias_ref[edge], _dot_nt(lhs, kx_ref[...])], axis=-1)
        m = jnp.max(s, axis=-1, keepdims=True)
        pb = jnp.exp2((s - m).astype(jnp.bfloat16))
        o = _dot(pb[:, :NA_WINDOW], vl) + _dot(pb[:, NA_WINDOW:], vxe_ref[...])
        o = o[:, :LANES] * (1.0 / o[:, LANES:])
        o_ref[qrows, :] = jnp.where(low_q, o[:NA_GROUP], o[NA_GROUP:]).astype(jnp.bfloat16)


def _natten_call(q, k, v, k_ctx, v_ctx, bias):
    n = q.shape[0]
    grid_rows = n // GRID_W
    assert grid_rows % NA_ROWS_PER_BLOCK == 0 and grid_rows >= 2 * NA_ROWS_PER_BLOCK
    nblk = n // NA_BLOCK
    nhalo = n // NA_HALO
    hb = NA_BLOCK // NA_HALO
    cur = pl.BlockSpec((NA_BLOCK, LANES), lambda p, b: (b, p))
    prev = pl.BlockSpec((NA_HALO, LANES), lambda p, b: (jnp.maximum(b * hb - 1, 0), p))
    nxt = pl.BlockSpec((NA_HALO, LANES), lambda p, b: (jnp.minimum((b + 1) * hb, nhalo - 1), p))
    ctx = pl.BlockSpec((k_ctx.shape[0], LANES), lambda p, b: (0, p))
    return pl.pallas_call(
        functools.partial(_natten_kernel, grid_rows=grid_rows),
        grid=(HEAD_PAIRS, nblk),
        in_specs=[cur, prev, cur, nxt, prev, cur, nxt, ctx, ctx,
                  pl.BlockSpec((None, 3, 2 * NA_GROUP, NA_WINDOW), lambda p, b: (p, 0, 0, 0))],
        out_specs=cur,
        out_shape=jax.ShapeDtypeStruct((n, NA_DIM), jnp.bfloat16),
        scratch_shapes=[pltpu.VMEM((NA_BLOCK + 2 * NA_HALO, LANES), jnp.bfloat16),
                        pltpu.VMEM((NA_BLOCK + 2 * NA_HALO, 2 * LANES), jnp.bfloat16),
                        pltpu.VMEM((k_ctx.shape[0], 2 * LANES), jnp.bfloat16)],
        compiler_params=_cparams(("arbitrary", "arbitrary")),
        name="natten",
    )(q, k, k, k, v, v, v, k_ctx, v_ctx, bias)


def _natten_bias(rpb):
    cols = np.arange(GRID_W)
    col_start = np.clip(cols - NA_WIN_COLS // 2, 0, GRID_W - NA_WIN_COLS)
    kc = np.arange(GRID_W)
    in_win = (kc[None, :] >= col_start[:, None]) & (kc[None, :] < col_start[:, None] + NA_WIN_COLS)
    dc = kc[None, :] - cols[:, None] + NA_WIN_COLS - 1
    sel = (np.arange(2 * NA_WIN_COLS - 1)[:, None, None] == dc[None]) & in_win[None]
    t2 = jnp.einsum("hdj,jqk->hdqk", rpb, jnp.asarray(sel, jnp.float32), precision=lax.Precision.HIGHEST)
    t2 = jnp.where(in_win[None, None], t2 * LOG2_E, NEG_BIG)
    t2 = jnp.transpose(t2, (0, 2, 1, 3))
    spare = NA_WINDOW_ROWS - NA_WIN_ROWS

    def masked(rows):
        return jnp.full((NA_HEADS, GRID_W, rows, GRID_W), NEG_BIG, jnp.float32)

    def query_row(lo, base):
        return jnp.concatenate([masked(lo), t2[:, :, base:base + NA_WIN_ROWS], masked(spare - lo)], axis=2)

    half = NA_WIN_ROWS // 2
    cases = []
    for edge in (NA_EDGE_FIRST, NA_EDGE_NONE, NA_EDGE_LAST):
        rows = []
        for j in range(NA_GROUP_ROWS):
            if edge == NA_EDGE_FIRST:
                rows.append(query_row(0, NA_WIN_ROWS - 1 - j))
            elif edge == NA_EDGE_NONE:
                rows.append(query_row(j, half - 1))
            else:
                rows.append(query_row(spare, half - 1 - j))
        t = jnp.stack(rows, axis=1)
        cases.append(t.reshape(HEAD_PAIRS, 2 * NA_GROUP, NA_WINDOW))
    return jnp.stack(cases, axis=1)


def _ctxatt_kernel(q_ref, k_ref, v_ref, o_ref):
    lc = q_ref.shape[0]
    low = lax.broadcasted_iota(jnp.int32, (lc, LANES), 1) < HEAD_DIM
    lhs = _stack_heads(q_ref[...], low)
    s = _dot_nt(lhs, k_ref[...])
    m = jnp.max(s, axis=-1, keepdims=True)
    p = jnp.exp2(s - m)
    denom = jnp.sum(p, axis=-1, keepdims=True)
    o = _dot(p.astype(jnp.bfloat16), v_ref[...]) * (1.0 / denom)
    o_ref[...] = jnp.where(low, o[:lc], o[lc:]).astype(jnp.bfloat16)


def _ctxatt_call(q, k, v):
    lc = q.shape[0]
    spec = pl.BlockSpec((lc, LANES), lambda p: (0, p))
    return pl.pallas_call(
        _ctxatt_kernel,
        grid=(HEAD_PAIRS,),
        in_specs=[spec, spec, spec],
        out_specs=spec,
        out_shape=jax.ShapeDtypeStruct((lc, NA_DIM), jnp.bfloat16),
        compiler_params=_cparams(("arbitrary",)),
        name="ctxatt",
    )(q, k, v)


def _outproj_kernel(h_ref, mp_ref, att_ref, sg_ref, wo_ref, g1_ref, n2_ref, sh_ref, sc_ref,
                    wr_ref, br_ref, h1_ref, pay_ref, route_ref):
    tm = h_ref.shape[0]
    mix = (_dot(mp_ref[...], wo_ref[0:POOL_DIM, :])
           + _dot(att_ref[...], wo_ref[POOL_DIM:POOL_DIM + NA_DIM, :])
           + _dot(sg_ref[...], wo_ref[POOL_DIM + NA_DIM:, :]))
    h1 = h_ref[...] + g1_ref[...] * mix
    h1_ref[...] = h1
    ms = jnp.mean(h1 * h1, axis=-1, keepdims=True)
    hm = h1 * lax.rsqrt(ms + EPS) * n2_ref[...] * (1.0 + sc_ref[...]) + sh_ref[...]
    pay_ref[:, 0:D_MODEL] = hm

    hm_hi = hm.astype(jnp.bfloat16)
    hm_lo = (hm - hm_hi.astype(jnp.float32)).astype(jnp.bfloat16)
    lt = _dot_nt(wr_ref[...], hm_hi)
    logits = (lt[:N_EXPERTS] + lt[N_EXPERTS:] + _dot_nt(wr_ref[0:N_EXPERTS, :], hm_lo) + br_ref[...])
    e = jnp.exp(logits - jnp.max(logits, axis=0, keepdims=True))

    best = ga = gb = e1 = e2 = cls = None
    for c in range(N_CLASSES):
        a, b2 = int(CLASS_E1[c]), int(CLASS_E2[c])
        ea, eb = e[a:a + 1, :], e[b2:b2 + 1, :]
        s = ea + eb
        if best is None:
            best, ga, gb = s, ea, eb
            e1 = jnp.full_like(s, float(a))
            e2 = jnp.full_like(s, float(b2))
            cls = jnp.zeros_like(s)
        else:
            better = s > best
            best = jnp.where(better, s, best)
            ga = jnp.where(better, ea, ga)
            gb = jnp.where(better, eb, gb)
            e1 = jnp.where(better, float(a), e1)
            e2 = jnp.where(better, float(b2), e2)
            cls = jnp.where(better, float(c), cls)
    inv = 1.0 / best
    row = lax.broadcasted_iota(jnp.int32, (ROUTE_ROWS, tm), 0)
    rec = jnp.where(row == 0, ga * inv,
          jnp.where(row == 1, gb * inv,
          jnp.where(row == 2, e1,
          jnp.where(row == 3, e2,
          jnp.where(row == 4, cls, 0.0)))))
    route_ref[...] = rec
    wide = jnp.concatenate([rec, jnp.zeros((LANES - ROUTE_ROWS, tm), jnp.float32)], axis=0)
    pay_ref[:, D_MODEL:] = wide.T


def _outproj_call(h, mp, att, sg, w_out, g1, n2, sh2, sc2, wr, br, tm):
    n = h.shape[0]
    row = lambda i: (i, 0)
    fixed = lambda i: (0, 0)
    vec = pl.BlockSpec((1, D_MODEL), fixed)
    return pl.pallas_call(
        _outproj_kernel,
        grid=(n // tm,),
        in_specs=[
            pl.BlockSpec((tm, D_MODEL), row),
            pl.BlockSpec((tm, POOL_DIM), row),
            pl.BlockSpec((tm, NA_DIM), row),
            pl.BlockSpec((tm, SG_DIM), row),
            pl.BlockSpec((D_MODEL, D_MODEL), fixed),
            vec, vec, vec, vec,
            pl.BlockSpec((2 * N_EXPERTS, D_MODEL), fixed),
            pl.BlockSpec((N_EXPERTS, 1), fixed),
        ],
        out_specs=[
            pl.BlockSpec((tm, D_MODEL), row),
            pl.BlockSpec((tm, PAYLOAD_W), row),
            pl.BlockSpec((ROUTE_ROWS, tm), lambda i: (0, i)),
        ],
        out_shape=[
            jax.ShapeDtypeStruct((n, D_MODEL), jnp.float32),
            jax.ShapeDtypeStruct((n, PAYLOAD_W), jnp.float32),
            jax.ShapeDtypeStruct((ROUTE_ROWS, n), jnp.float32),
        ],
        compiler_params=_cparams(("arbitrary",)),
        name="outproj",
    )(h, mp, att, sg, w_out, g1, n2, sh2, sc2, wr, br)


def _expert_pair(x, ga, gb, wga, wua, wda, wgb, wub, wdb):
    ha = (_silu(_dot(x, wga)) * _dot(x, wua) * ga).astype(jnp.bfloat16)
    hb = (_silu(_dot(x, wgb)) * _dot(x, wub) * gb).astype(jnp.bfloat16)
    return _dot(ha, wda) + _dot(hb, wdb)


def _moe_sorted_kernel(blk_ref, e1_ref, e2_ref, nact_ref, pay_ref,
                       wga_ref, wua_ref, wda_ref, wgb_ref, wub_ref, wdb_ref, o_ref):
    i = pl.program_id(0)

    @pl.when(i < nact_ref[0])
    def _():
        x = pay_ref[:, 0:D_MODEL].astype(jnp.bfloat16)
        ga = pay_ref[:, D_MODEL:D_MODEL + 1]
        gb = pay_ref[:, D_MODEL + 1:D_MODEL + 2]
        o_ref[...] = _expert_pair(x, ga, gb, wga_ref[...], wua_ref[...], wda_ref[...],
                                  wgb_ref[...], wub_ref[...], wdb_ref[...])


def _moe_sorted_call(blk, e1, e2, nact, pay_sorted, wg, wu, wd):
    n_tiles = blk.shape[0]
    rows = lambda i, blk, e1, e2, na: (blk[i], 0)
    wa = lambda i, blk, e1, e2, na: (e1[i], 0, 0)
    wb = lambda i, blk, e1, e2, na: (e2[i], 0, 0)
    up = lambda m: pl.BlockSpec((None, D_MODEL, D_EXPERT), m)
    down = lambda m: pl.BlockSpec((None, D_EXPERT, D_MODEL), m)
    return pl.pallas_call(
        _moe_sorted_kernel,
        grid_spec=pltpu.PrefetchScalarGridSpec(
            num_scalar_prefetch=4,
            grid=(n_tiles,),
            in_specs=[pl.BlockSpec((MOE_TM, PAYLOAD_W), rows),
                      up(wa), up(wa), down(wa), up(wb), up(wb), down(wb)],
            out_specs=pl.BlockSpec((MOE_TM, D_MODEL), rows),
        ),
        out_shape=jax.ShapeDtypeStruct((n_tiles * MOE_TM, D_MODEL), jnp.float32),
        compiler_params=_cparams(("arbitrary",)),
        name="moe_sorted",
    )(blk, e1, e2, nact, pay_sorted, wg, wu, wd, wg, wu, wd)


def _moe_dense_kernel(pay_ref, wg_ref, wu_ref, wd_ref, o_ref):
    e = pl.program_id(0)

    @pl.when(e == 0)
    def _():
        o_ref[...] = jnp.zeros_like(o_ref)

    x = pay_ref[:, 0:D_MODEL].astype(jnp.bfloat16)
    ef = e.astype(jnp.float32)
    gate = (jnp.where(pay_ref[:, D_MODEL + 2:D_MODEL + 3] == ef, pay_ref[:, D_MODEL:D_MODEL + 1], 0.0)
            + jnp.where(pay_ref[:, D_MODEL + 3:D_MODEL + 4] == ef, pay_ref[:, D_MODEL + 1:D_MODEL + 2], 0.0))
    he = (_silu(_dot(x, wg_ref[...])) * _dot(x, wu_ref[...]) * gate).astype(jnp.bfloat16)
    o_ref[...] += _dot(he, wd_ref[...])


def _moe_dense_call(pay, wg, wu, wd):
    n = pay.shape[0]
    return pl.pallas_call(
        _moe_dense_kernel,
        grid=(N_EXPERTS,),
        in_specs=[pl.BlockSpec((n, PAYLOAD_W), lambda e: (0, 0)),
                  pl.BlockSpec((None, D_MODEL, D_EXPERT), lambda e: (e, 0, 0)),
                  pl.BlockSpec((None, D_MODEL, D_EXPERT), lambda e: (e, 0, 0)),
                  pl.BlockSpec((None, D_EXPERT, D_MODEL), lambda e: (e, 0, 0))],
        out_specs=pl.BlockSpec((n, D_MODEL), lambda e: (0, 0)),
        out_shape=jax.ShapeDtypeStruct((n, D_MODEL), jnp.float32),
        compiler_params=_cparams(("arbitrary",)),
        name="moe_dense",
    )(pay, wg, wu, wd)


def _residual_kernel(h_ref, y_ref, g_ref, o_ref):
    o_ref[...] = h_ref[...] + g_ref[...] * y_ref[...]


def _residual_call(h, y, g, tm):
    n = h.shape[0]
    row = pl.BlockSpec((tm, D_MODEL), lambda i: (i, 0))
    return pl.pallas_call(
        _residual_kernel,
        grid=(n // tm,),
        in_specs=[row, row, pl.BlockSpec((1, D_MODEL), lambda i: (0, 0))],
        out_specs=row,
        out_shape=jax.ShapeDtypeStruct((n, D_MODEL), jnp.float32),
        compiler_params=_cparams(("arbitrary",)),
        name="residual",
    )(h, y, g)


SC_ROWS = 32


SC_CORES = 2
SC_SUBCORES = 16
SC_WORKERS = SC_CORES * SC_SUBCORES


def _sc_mesh():
    return plsc.VectorSubcoreMesh(core_axis_name="core", subcore_axis_name="subcore")


def _sc_worker():
    return lax.axis_index("subcore") * SC_CORES + lax.axis_index("core")


def _scatter_rows(x, dest, n_out):
    n, w = x.shape

    per_worker = n // SC_WORKERS
    assert per_worker % SC_ROWS == 0

    @functools.partial(pl.kernel, out_type=jax.ShapeDtypeStruct((n_out, w), x.dtype), mesh=_sc_mesh(),
                       scratch_types=[pltpu.VMEM((SC_ROWS,), jnp.int32), pltpu.VMEM((SC_ROWS, w), x.dtype)])
    def scatter(x_hbm, i_hbm, o_hbm, idx_v, rows_v):
        first = _sc_worker() * per_worker

        @pl.loop(0, per_worker // SC_ROWS)
        def _(i):
            base = pl.multiple_of(first + i * SC_ROWS, SC_ROWS)
            pltpu.sync_copy(i_hbm.at[pl.ds(base, SC_ROWS)], idx_v)
            pltpu.sync_copy(x_hbm.at[pl.ds(base, SC_ROWS)], rows_v)
            pltpu.sync_copy(rows_v, o_hbm.at[idx_v])

    return scatter(x, dest)


def _gather_rows(x, src):
    n = src.shape[0]
    w = x.shape[1]

    per_worker = n // SC_WORKERS
    assert per_worker % SC_ROWS == 0

    @functools.partial(pl.kernel, out_type=jax.ShapeDtypeStruct((n, w), x.dtype), mesh=_sc_mesh(),
                       scratch_types=[pltpu.VMEM((SC_ROWS,), jnp.int32), pltpu.VMEM((SC_ROWS, w), x.dtype)])
    def gather(x_hbm, i_hbm, o_hbm, idx_v, rows_v):
        first = _sc_worker() * per_worker

        @pl.loop(0, per_worker // SC_ROWS)
        def _(i):
            base = pl.multiple_of(first + i * SC_ROWS, SC_ROWS)
            pltpu.sync_copy(i_hbm.at[pl.ds(base, SC_ROWS)], idx_v)
            pltpu.sync_copy(x_hbm.at[idx_v], rows_v)
            pltpu.sync_copy(rows_v, o_hbm.at[pl.ds(base, SC_ROWS)])

    return gather(x, src)


def _routing_plan(cls, n_tiles):
    onehot = (cls[:, None] == jnp.arange(N_CLASSES, dtype=jnp.int32)[None, :]).astype(jnp.int32)
    counts = jnp.sum(onehot, axis=0)
    rank = jnp.sum((jnp.cumsum(onehot, axis=0) - onehot) * onehot, axis=1)
    tiles = (counts + MOE_TM - 1) // MOE_TM
    tile_end = jnp.cumsum(tiles)
    tile_start = tile_end - tiles
    dest = jnp.sum(onehot * tile_start[None, :], axis=1) * MOE_TM + rank
    nact = tile_end[-1]
    blk = jnp.minimum(jnp.arange(n_tiles, dtype=jnp.int32), nact - 1)
    tile_cls = jnp.sum((blk[:, None] >= tile_end[None, :]).astype(jnp.int32), axis=1)
    e1 = jnp.asarray(CLASS_E1)[tile_cls]
    e2 = jnp.asarray(CLASS_E2)[tile_cls]
    return dest.astype(jnp.int32), blk, e1, e2, nact.reshape(1).astype(jnp.int32)


def _block_diag_mean(width):
    idx = np.arange(width) // HEAD_DIM
    return jnp.asarray((idx[:, None] == idx[None, :]).astype(np.float32) / HEAD_DIM, jnp.bfloat16)


def _row_tile(n):
    return 512 if n % 512 == 0 else 256


def kernel(x, c, ctx, c_ctx, w_ada, b_ada, norm1, w_in, pool_w, pool_scale, q_norm, k_norm, rpb,
           sg_w, sg_b, sg_norm, w_out, norm2, w_router, b_router, w_gate, w_up, w_down):
    depth = w_ada.shape[0]
    n = x.shape[1]
    lc = ctx.shape[1]
    bf = jnp.bfloat16
    h_lat = x[0]
    h_ctx = ctx[0]

    cond = jnp.zeros((SUBLANES, D_MODEL), jnp.float32).at[0].set(c[0]).at[1].set(c_ctx)
    mod = _ada_call(cond, w_ada, b_ada)

    b512 = _block_diag_mean(NA_DIM)
    b256 = _block_diag_mean(SG_DIM)
    wr_t = w_router.T
    wr_hi = wr_t.astype(bf)
    wr_lo = (wr_t - wr_hi.astype(jnp.float32)).astype(bf)
    wr = jnp.concatenate([wr_hi, wr_lo], axis=0)
    br = b_router.reshape(N_EXPERTS, 1)
    n_tiles = n // MOE_TM + N_CLASSES

    for l in range(depth):
        last = l == depth - 1
        w_in_l = w_in[l].astype(bf)
        w_out_l = w_out[l].astype(bf)
        wg_l, wu_l, wd_l = w_gate[l].astype(bf), w_up[l].astype(bf), w_down[l].astype(bf)
        qg = (q_norm[l] * (HEAD_DIM ** -0.5 * LOG2_E)).reshape(1, NA_DIM)
        kg = k_norm[l].reshape(1, NA_DIM)
        sgn = sg_norm[l].reshape(1, SG_DIM)
        sgw = sg_w[l].astype(bf).reshape(SG_DIM // LANES, 2 * SG_CHUNK, SG_CHUNK)
        sgb = jnp.broadcast_to(sg_b[l].reshape(SG_DIM // LANES, 2 * SG_CHUNK, 1),
                               (SG_DIM // LANES, 2 * SG_CHUNK, LANES))
        pool_bd = jax.scipy.linalg.block_diag(*[pool_w[l, g] for g in range(len(POOL_WINDOWS))]).astype(bf)
        pscale = pool_scale[l].reshape(1, POOL_DIM)
        bias = _natten_bias(rpb[l])
        n1 = norm1[l].reshape(1, D_MODEL)
        n2 = norm2[l].reshape(1, D_MODEL)

        def mods(row):
            return [mod[l, row:row + 1, i * D_MODEL:(i + 1) * D_MODEL] for i in range(6)]

        sh1, sc1, g1, sh2, sc2, g2 = mods(0)
        csh1, csc1, cg1, csh2, csc2, cg2 = mods(1)

        tc = _row_tile(lc)
        pool_c, q_c, k_c, v_c, sg_c = _inproj_call(h_ctx, n1, csh1, csc1, w_in_l, qg, kg, b512, b256,
                                                   sgn, sgw, sgb, tc)

        tm = _row_tile(n)
        pool_a, q, k, v, sg = _inproj_call(h_lat, n1, sh1, sc1, w_in_l, qg, kg, b512, b256, sgn, sgw, sgb, tm)
        mix_pool = _pool_call(pool_a, pool_bd, pscale, tm)
        att = _natten_call(q, k, v, k_c, v_c, bias)
        h1, pay, route = _outproj_call(h_lat, mix_pool, att, sg, w_out_l, g1, n2, sh2, sc2, wr, br, tm)
        cls = route[4].astype(jnp.int32)
        dest, blk, e1, e2, nact = _routing_plan(cls, n_tiles)
        pay_sorted = _scatter_rows(pay, dest, n_tiles * MOE_TM)
        y_sorted = _moe_sorted_call(blk, e1, e2, nact, pay_sorted, wg_l, wu_l, wd_l)
        y = _gather_rows(y_sorted, dest)
        h_lat = _residual_call(h1, y, g2, tm)

        if not last:
            mix_pool_c = _pool_call(pool_c, pool_bd, pscale, tc)
            att_c = _ctxatt_call(q_c, k_c, v_c)
            h1_c, pay_c, _ = _outproj_call(h_ctx, mix_pool_c, att_c, sg_c, w_out_l, cg1, n2, csh2, csc2,
                                           wr, br, tc)
            y_c = _moe_dense_call(pay_c, wg_l, wu_l, wd_l)
            h_ctx = _residual_call(h1_c, y_c, cg2, tc)

    return h_lat[None]
```

```python
import functools

import jax
import jax.numpy as jnp
import numpy as np
from jax import lax
from jax.experimental import pallas as pl
from jax.experimental.pallas import tpu as pltpu
from jax.experimental.pallas import tpu_sc as plsc

D_MODEL = 1024
GRID_W = 64
HEAD_DIM = 64
POOL_WINDOWS = (2, 4, 8, 16)
POOL_DIM = 256
NA_HEADS = 8
NA_DIM = 512
NA_WIN_ROWS = 8
NA_WIN_COLS = 16
SG_DIM = 256
SG_CHUNK = 128
Q_OFF = POOL_DIM
K_OFF = Q_OFF + NA_DIM
V_OFF = K_OFF + NA_DIM
U_OFF = V_OFF + NA_DIM
G_OFF = U_OFF + SG_DIM
IN_DIM = G_OFF + SG_DIM
N_EXPERTS = 16
GROUP_SIZE = 4
D_EXPERT = 512
EPS = 1e-6

LANES = 128
SUBLANES = 8
HEAD_PAIRS = NA_DIM // LANES
VMEM_LIMIT = 48 * 1024 * 1024

PAIRS = ((0, 1), (0, 2), (0, 3), (1, 2), (1, 3), (2, 3))
N_CLASSES = (N_EXPERTS // GROUP_SIZE) * len(PAIRS)
CLASS_E1 = np.array([4 * g + i for g in range(4) for (i, j) in PAIRS], np.int32)
CLASS_E2 = np.array([4 * g + j for g in range(4) for (i, j) in PAIRS], np.int32)

ROUTE_ROWS = 8
PAYLOAD_W = D_MODEL + LANES
MOE_TM = 256
NEG_BIG = -1e30
LOG2_E = 1.4426950408889634


def _cparams(sem):
    return pltpu.CompilerParams(dimension_semantics=sem, vmem_limit_bytes=VMEM_LIMIT)


def _dot(a, b):
    return jnp.dot(a, b, preferred_element_type=jnp.float32)


def _dot_nt(a, b):
    return lax.dot_general(a, b, (((1,), (1,)), ((), ())), preferred_element_type=jnp.float32)


def _gelu_tanh(x):
    return 0.5 * x * (1.0 + jnp.tanh(0.7978845608028654 * (x + 0.044715 * (x * x * x))))


def _silu(x):
    return x * (1.0 / (1.0 + jnp.exp(-x)))


def _ada_kernel(cond_ref, w_ref, b_ref, o_ref):
    cond = _silu(cond_ref[...])
    o_ref[...] = jnp.dot(cond, w_ref[...], preferred_element_type=jnp.float32,
                         precision=lax.Precision.HIGHEST) + b_ref[...]


def _ada_call(cond, w_ada, b_ada):
    depth = w_ada.shape[0]
    tn = 1536
    return pl.pallas_call(
        _ada_kernel,
        grid=(depth, 6 * D_MODEL // tn),
        in_specs=[
            pl.BlockSpec((SUBLANES, D_MODEL), lambda l, j: (0, 0)),
            pl.BlockSpec((None, D_MODEL, tn), lambda l, j: (l, 0, j)),
            pl.BlockSpec((None, 1, tn), lambda l, j: (l, 0, j)),
        ],
        out_specs=pl.BlockSpec((None, SUBLANES, tn), lambda l, j: (l, 0, j)),
        out_shape=jax.ShapeDtypeStruct((depth, SUBLANES, 6 * D_MODEL), jnp.float32),
        compiler_params=_cparams(("arbitrary", "arbitrary")),
        name="adaln",
    )(cond, w_ada, b_ada.reshape(depth, 1, 6 * D_MODEL))


def _inproj_kernel(*refs, pending):
    if pending:
        h_ref, y_ref, g_ref = refs[:3]
        refs = refs[3:]
        hres_ref = refs[-1]
        refs = refs[:-1]
        x = h_ref[...] + g_ref[...] * y_ref[...]
        hres_ref[...] = x
    else:
        h_ref = refs[0]
        refs = refs[1:]
        x = h_ref[...]
    (n1_ref, sh_ref, sc_ref, w_ref, qg_ref, kg_ref, b512_ref, b256_ref, sgn_ref, sgw_ref, sgb_ref,
     pool_ref, q_ref, k_ref, v_ref, sg_ref) = refs
    tm = h_ref.shape[0]
    ms = jnp.mean(x * x, axis=-1, keepdims=True)
    hn = ((x * lax.rsqrt(ms + EPS)) * (n1_ref[...] * (1.0 + sc_ref[...])) + sh_ref[...]).astype(jnp.bfloat16)

    pool_ref[...] = _dot(hn, w_ref[:, 0:Q_OFF])

    a_q = _dot(hn, w_ref[:, Q_OFF:K_OFF])
    msq = _dot((a_q * a_q).astype(jnp.bfloat16), b512_ref[...])
    q_ref[...] = (a_q * lax.rsqrt(msq + EPS) * qg_ref[...]).astype(jnp.bfloat16)

    a_k = _dot(hn, w_ref[:, K_OFF:V_OFF])
    msk = _dot((a_k * a_k).astype(jnp.bfloat16), b512_ref[...])
    k_ref[...] = (a_k * lax.rsqrt(msk + EPS) * kg_ref[...]).astype(jnp.bfloat16)

    v_ref[...] = _dot(hn, w_ref[:, V_OFF:U_OFF]).astype(jnp.bfloat16)

    u = _gelu_tanh(_dot(hn, w_ref[:, U_OFF:G_OFF]))
    gv = _gelu_tanh(_dot(hn, w_ref[:, G_OFF:IN_DIM]))
    msv = _dot((gv * gv).astype(jnp.bfloat16), b256_ref[...])
    vn = (gv * lax.rsqrt(msv + EPS) * sgn_ref[...]).astype(jnp.bfloat16)
    low = lax.broadcasted_iota(jnp.int32, (SG_CHUNK, LANES), 1) < HEAD_DIM
    for c in range(tm // SG_CHUNK):
        rows = slice(c * SG_CHUNK, (c + 1) * SG_CHUNK)
        for s in range(SG_DIM // LANES):
            cols = slice(s * LANES, (s + 1) * LANES)
            m = _dot(sgw_ref[s], vn[rows, cols]) + sgb_ref[s]
            mixed = jnp.where(low, m[:SG_CHUNK], m[SG_CHUNK:])
            sg_ref[rows, cols] = (u[rows, cols] * mixed).astype(jnp.bfloat16)


def _inproj_call(stream, n1, sh, sc, w_in, qg, kg, b512, b256, sgn, sgw, sgb, tm):
    pending = len(stream) == 3
    n = stream[0].shape[0]
    row = lambda i: (i, 0)
    fixed2 = lambda i: (0, 0)
    fixed3 = lambda i: (0, 0, 0)
    vec = lambda w: pl.BlockSpec((1, w), fixed2)
    rows = pl.BlockSpec((tm, D_MODEL), row)
    stream_specs = [rows, rows, vec(D_MODEL)] if pending else [rows]
    extra_out_specs = [rows] if pending else []
    extra_out_shape = [jax.ShapeDtypeStruct((n, D_MODEL), jnp.float32)] if pending else []
    return pl.pallas_call(
        functools.partial(_inproj_kernel, pending=pending),
        grid=(n // tm,),
        in_specs=stream_specs + [
            vec(D_MODEL), vec(D_MODEL), vec(D_MODEL),
            pl.BlockSpec((D_MODEL, IN_DIM), fixed2),
            vec(NA_DIM), vec(NA_DIM),
            pl.BlockSpec((NA_DIM, NA_DIM), fixed2),
            pl.BlockSpec((SG_DIM, SG_DIM), fixed2),
            vec(SG_DIM),
            pl.BlockSpec((SG_DIM // LANES, 2 * SG_CHUNK, SG_CHUNK), fixed3),
            pl.BlockSpec((SG_DIM // LANES, 2 * SG_CHUNK, LANES), fixed3),
        ],
        out_specs=[
            pl.BlockSpec((tm, POOL_DIM), row),
            pl.BlockSpec((tm, NA_DIM), row),
            pl.BlockSpec((tm, NA_DIM), row),
            pl.BlockSpec((tm, NA_DIM), row),
            pl.BlockSpec((tm, SG_DIM), row),
        ] + extra_out_specs,
        out_shape=[
            jax.ShapeDtypeStruct((n, POOL_DIM), jnp.float32),
            jax.ShapeDtypeStruct((n, NA_DIM), jnp.bfloat16),
            jax.ShapeDtypeStruct((n, NA_DIM), jnp.bfloat16),
            jax.ShapeDtypeStruct((n, NA_DIM), jnp.bfloat16),
            jax.ShapeDtypeStruct((n, SG_DIM), jnp.bfloat16),
        ] + extra_out_shape,
        compiler_params=_cparams(("arbitrary",)),
        name="inproj",
    )(*stream, n1, sh, sc, w_in, qg, kg, b512, b256, sgn, sgw, sgb)


POOL_HALO = 8


def _pool_kernel(prev_ref, x_ref, next_ref, w_ref, scale_ref, o_ref, xe_ref, *, seq_len):
    tm = x_ref.shape[0]
    i = pl.program_id(0)
    last = pl.num_programs(0) - 1
    xe_ref[0:POOL_HALO, :] = jnp.where(i > 0, prev_ref[...], 0.0)
    xe_ref[POOL_HALO:POOL_HALO + tm, :] = x_ref[...]
    xe_ref[POOL_HALO + tm:, :] = jnp.where(i < last, next_ref[...], 0.0)

    t = i * tm + lax.broadcasted_iota(jnp.int32, (tm, LANES), 0)
    low = lax.broadcasted_iota(jnp.int32, (tm, LANES), 1) < HEAD_DIM

    def count(half):
        return (jnp.minimum(t + half, seq_len) - jnp.maximum(t - half, 0)).astype(jnp.float32)

    def window_sums(xs, n_levels):
        sums = []
        s = xs
        for k in range(n_levels):
            step = 1 << k
            s = s[:-step] + s[step:]
            sums.append(s)
        return sums

    outs = []
    for half_block, windows in enumerate(((2, 4), (8, 16))):
        xs = xe_ref[:, half_block * LANES:(half_block + 1) * LANES]
        sums = window_sums(xs, int(np.log2(windows[1])))
        parts = []
        for w in windows:
            half = w // 2
            s = sums[int(np.log2(w)) - 1][POOL_HALO - half:POOL_HALO - half + tm]
            parts.append(s / count(half))
        mean = jnp.where(low, parts[0], parts[1])
        outs.append(mean - xs[POOL_HALO:POOL_HALO + tm])
    d = jnp.concatenate(outs, axis=-1).astype(jnp.bfloat16)
    o_ref[...] = (_dot(d, w_ref[...]) * scale_ref[...]).astype(jnp.bfloat16)


def _pool_call(a_pool, w_bd, scale, tm):
    n = a_pool.shape[0]
    nb8 = n // POOL_HALO
    r8 = tm // POOL_HALO
    return pl.pallas_call(
        functools.partial(_pool_kernel, seq_len=n),
        grid=(n // tm,),
        in_specs=[
            pl.BlockSpec((POOL_HALO, POOL_DIM), lambda i: (jnp.maximum(i * r8 - 1, 0), 0)),
            pl.BlockSpec((tm, POOL_DIM), lambda i: (i, 0)),
            pl.BlockSpec((POOL_HALO, POOL_DIM), lambda i: (jnp.minimum((i + 1) * r8, nb8 - 1), 0)),
            pl.BlockSpec((POOL_DIM, POOL_DIM), lambda i: (0, 0)),
            pl.BlockSpec((1, POOL_DIM), lambda i: (0, 0)),
        ],
        out_specs=pl.BlockSpec((tm, POOL_DIM), lambda i: (i, 0)),
        out_shape=jax.ShapeDtypeStruct((n, POOL_DIM), jnp.bfloat16),
        scratch_shapes=[pltpu.VMEM((tm + 2 * POOL_HALO, POOL_DIM), jnp.float32)],
        compiler_params=_cparams(("arbitrary",)),
        name="pool",
    )(a_pool, a_pool, a_pool, w_bd, scale)


NA_ROWS_PER_BLOCK = 16
NA_GROUP_ROWS = 4
NA_WINDOW_ROWS = NA_GROUP_ROWS + NA_WIN_ROWS
NA_BLOCK = NA_ROWS_PER_BLOCK * GRID_W
NA_GROUP = NA_GROUP_ROWS * GRID_W
NA_HALO = (NA_WIN_ROWS // 2) * GRID_W
NA_WINDOW = NA_WINDOW_ROWS * GRID_W
NA_EDGE_FIRST, NA_EDGE_NONE, NA_EDGE_LAST = 0, 1, 2


def _stack_heads(x, low):
    zero = jnp.zeros_like(x)
    return jnp.concatenate([jnp.where(low, x, zero), jnp.where(low, zero, x)], axis=0)


def _natten_kernel(q_ref, kp_ref, kc_ref, kn_ref, vp_ref, vc_ref, vn_ref, kx_ref, vx_ref, bias_ref,
                   o_ref, kwin_ref, vwin_ref, vxe_ref, *, grid_rows):
    b = pl.program_id(1)
    kwin_ref[0:NA_HALO, :] = kp_ref[...]
    kwin_ref[NA_HALO:NA_HALO + NA_BLOCK, :] = kc_ref[...]
    kwin_ref[NA_HALO + NA_BLOCK:, :] = kn_ref[...]
    vwin_ref[0:NA_HALO, 0:LANES] = vp_ref[...]
    vwin_ref[NA_HALO:NA_HALO + NA_BLOCK, 0:LANES] = vc_ref[...]
    vwin_ref[NA_HALO + NA_BLOCK:, 0:LANES] = vn_ref[...]
    vwin_ref[:, LANES:] = jnp.ones((vwin_ref.shape[0], LANES), jnp.bfloat16)
    vxe_ref[:, 0:LANES] = vx_ref[...]
    vxe_ref[:, LANES:] = jnp.ones((vxe_ref.shape[0], LANES), jnp.bfloat16)
    low_q = lax.broadcasted_iota(jnp.int32, (NA_GROUP, LANES), 1) < HEAD_DIM

    for g in range(NA_ROWS_PER_BLOCK // NA_GROUP_ROWS):
        r0 = b * NA_ROWS_PER_BLOCK + g * NA_GROUP_ROWS
        ws = jnp.clip(r0 - NA_WIN_ROWS // 2, 0, grid_rows - NA_WINDOW_ROWS)
        edge = jnp.where(r0 == 0, NA_EDGE_FIRST,
                         jnp.where(r0 == grid_rows - NA_GROUP_ROWS, NA_EDGE_LAST, NA_EDGE_NONE))
        start = pl.multiple_of((ws - b * NA_ROWS_PER_BLOCK + NA_WIN_ROWS // 2) * GRID_W, GRID_W)
        qrows = slice(g * NA_GROUP, (g + 1) * NA_GROUP)
        lhs = _stack_heads(q_ref[qrows, :], low_q)
        kl = kwin_ref[pl.ds(start, NA_WINDOW), :]
        vl = vwin_ref[pl.ds(start, NA_WINDOW), :]
        s = jnp.concatenate([_dot_nt(lhs, kl) + bias_ref[edge], _dot_nt(lhs, kx_ref[...])], axis=-1)
        m = jnp.max(s, axis=-1, keepdims=True)
        pb = jnp.exp2((s - m).astype(jnp.bfloat16))
        o = _dot(pb[:, :NA_WINDOW], vl) + _dot(pb[:, NA_WINDOW:], vxe_ref[...])
        o = o[:, :LANES] * (1.0 / o[:, LANES:])
        o_ref[qrows, :] = jnp.where(low_q, o[:NA_GROUP], o[NA_GROUP:]).astype(jnp.bfloat16)


def _natten_call(q, k, v, k_ctx, v_ctx, bias):
    n = q.shape[0]
    grid_rows = n // GRID_W
    assert grid_rows % NA_ROWS_PER_BLOCK == 0 and grid_rows >= 2 * NA_ROWS_PER_BLOCK
    nblk = n // NA_BLOCK
    nhalo = n // NA_HALO
    hb = NA_BLOCK // NA_HALO
    cur = pl.BlockSpec((NA_BLOCK, LANES), lambda p, b: (b, p))
    prev = pl.BlockSpec((NA_HALO, LANES), lambda p, b: (jnp.maximum(b * hb - 1, 0), p))
    nxt = pl.BlockSpec((NA_HALO, LANES), lambda p, b: (jnp.minimum((b + 1) * hb, nhalo - 1), p))
    ctx = pl.BlockSpec((k_ctx.shape[0], LANES), lambda p, b: (0, p))
    return pl.pallas_call(
        functools.partial(_natten_kernel, grid_rows=grid_rows),
        grid=(HEAD_PAIRS, nblk),
        in_specs=[cur, prev, cur, nxt, prev, cur, nxt, ctx, ctx,
                  pl.BlockSpec((None, 3, 2 * NA_GROUP, NA_WINDOW), lambda p, b: (p, 0, 0, 0))],
        out_specs=cur,
        out_shape=jax.ShapeDtypeStruct((n, NA_DIM), jnp.bfloat16),
        scratch_shapes=[pltpu.VMEM((NA_BLOCK + 2 * NA_HALO, LANES), jnp.bfloat16),
                        pltpu.VMEM((NA_BLOCK + 2 * NA_HALO, 2 * LANES), jnp.bfloat16),
                        pltpu.VMEM((k_ctx.shape[0], 2 * LANES), jnp.bfloat16)],
        compiler_params=_cparams(("arbitrary", "arbitrary")),
        name="natten",
    )(q, k, k, k, v, v, v, k_ctx, v_ctx, bias)


def _natten_bias(rpb):
    cols = np.arange(GRID_W)
    col_start = np.clip(cols - NA_WIN_COLS // 2, 0, GRID_W - NA_WIN_COLS)
    kc = np.arange(GRID_W)
    in_win = (kc[None, :] >= col_start[:, None]) & (kc[None, :] < col_start[:, None] + NA_WIN_COLS)
    dc = kc[None, :] - cols[:, None] + NA_WIN_COLS - 1
    sel = (np.arange(2 * NA_WIN_COLS - 1)[:, None, None] == dc[None]) & in_win[None]
    t2 = jnp.einsum("hdj,jqk->hdqk", rpb, jnp.asarray(sel, jnp.float32), precision=lax.Precision.HIGHEST)
    t2 = jnp.where(in_win[None, None], t2 * LOG2_E, NEG_BIG)
    neg = jnp.full((NA_HEADS, 1, GRID_W, GRID_W), NEG_BIG, jnp.float32)
    t2e = jnp.concatenate([neg, t2, neg], axis=1)
    u = jnp.concatenate([t2e[:, :-1], t2e[:, 1:]], axis=-1)
    u = u.reshape(HEAD_PAIRS, 2, 2 * NA_WIN_ROWS, GRID_W, LANES)
    return pl.pallas_call(
        _bias_expand_kernel,
        grid=(HEAD_PAIRS,),
        in_specs=[pl.BlockSpec((None, 2, 2 * NA_WIN_ROWS, GRID_W, LANES), lambda p: (p, 0, 0, 0, 0))],
        out_specs=pl.BlockSpec((None, 3, 2 * NA_GROUP, NA_WINDOW), lambda p: (p, 0, 0, 0)),
        out_shape=jax.ShapeDtypeStruct((HEAD_PAIRS, 3, 2 * NA_GROUP, NA_WINDOW), jnp.float32),
        compiler_params=_cparams(("arbitrary",)),
        name="bias_expand",
    )(u)


def _bias_expand_kernel(u_ref, o_ref):
    spare = NA_WINDOW_ROWS - NA_WIN_ROWS
    half = NA_WIN_ROWS // 2
    low = lax.broadcasted_iota(jnp.int32, (GRID_W, LANES), 1) < GRID_W
    neg = jnp.full((GRID_W, LANES), NEG_BIG, jnp.float32)
    for edge in (NA_EDGE_FIRST, NA_EDGE_NONE, NA_EDGE_LAST):
        for j in range(NA_GROUP_ROWS):
            if edge == NA_EDGE_FIRST:
                lo, base = 0, NA_WIN_ROWS - 1 - j
            elif edge == NA_EDGE_NONE:
                lo, base = j, half - 1
            else:
                lo, base = spare, half - 1 - j
            for hd in range(2):
                rows = slice(hd * NA_GROUP + j * GRID_W, hd * NA_GROUP + (j + 1) * GRID_W)
                for i in range(NA_WINDOW_ROWS // 2):
                    a0, a1 = 2 * i, 2 * i + 1
                    ok0 = lo <= a0 < lo + NA_WIN_ROWS
                    ok1 = lo <= a1 < lo + NA_WIN_ROWS
                    if not (ok0 or ok1):
                        tile = neg
                    else:
                        tile = u_ref[hd, base + a1 - lo]
                        if not ok0:
                            tile = jnp.where(low, neg, tile)
                        if not ok1:
                            tile = jnp.where(low, tile, neg)
                    o_ref[edge, rows, i * LANES:(i + 1) * LANES] = tile


def _ctxatt_kernel(q_ref, k_ref, v_ref, o_ref):
    lc = q_ref.shape[0]
    low = lax.broadcasted_iota(jnp.int32, (lc, LANES), 1) < HEAD_DIM
    lhs = _stack_heads(q_ref[...], low)
    s = _dot_nt(lhs, k_ref[...])
    m = jnp.max(s, axis=-1, keepdims=True)
    p = jnp.exp2(s - m)
    denom = jnp.sum(p, axis=-1, keepdims=True)
    o = _dot(p.astype(jnp.bfloat16), v_ref[...]) * (1.0 / denom)
    o_ref[...] = jnp.where(low, o[:lc], o[lc:]).astype(jnp.bfloat16)


def _ctxatt_call(q, k, v):
    lc = q.shape[0]
    spec = pl.BlockSpec((lc, LANES), lambda p: (0, p))
    return pl.pallas_call(
        _ctxatt_kernel,
        grid=(HEAD_PAIRS,),
        in_specs=[spec, spec, spec],
        out_specs=spec,
        out_shape=jax.ShapeDtypeStruct((lc, NA_DIM), jnp.bfloat16),
        compiler_params=_cparams(("arbitrary",)),
        name="ctxatt",
    )(q, k, v)


OUTPROJ_CHAIN = 256


def _outproj_kernel(h_ref, mp_ref, att_ref, sg_ref, wo_ref, g1_ref, n2_ref, sh_ref, sc_ref,
                    wr_ref, br_ref, h1_ref, pay_ref, route_ref):
    for c in range(h_ref.shape[0] // OUTPROJ_CHAIN):
        rows = slice(c * OUTPROJ_CHAIN, (c + 1) * OUTPROJ_CHAIN)
        _outproj_rows(rows, h_ref, mp_ref, att_ref, sg_ref, wo_ref, g1_ref, n2_ref, sh_ref, sc_ref,
                      wr_ref, br_ref, h1_ref, pay_ref, route_ref)


def _outproj_rows(rows, h_ref, mp_ref, att_ref, sg_ref, wo_ref, g1_ref, n2_ref, sh_ref, sc_ref,
                  wr_ref, br_ref, h1_ref, pay_ref, route_ref):
    tm = OUTPROJ_CHAIN
    mix = (_dot(mp_ref[rows, :], wo_ref[0:POOL_DIM, :])
           + _dot(att_ref[rows, :], wo_ref[POOL_DIM:POOL_DIM + NA_DIM, :])
           + _dot(sg_ref[rows, :], wo_ref[POOL_DIM + NA_DIM:, :]))
    h1 = h_ref[rows, :] + g1_ref[...] * mix
    h1_ref[rows, :] = h1
    ms = jnp.mean(h1 * h1, axis=-1, keepdims=True)
    hm = (h1 * lax.rsqrt(ms + EPS)) * (n2_ref[...] * (1.0 + sc_ref[...])) + sh_ref[...]
    pay_ref[rows, 0:D_MODEL] = hm

    hm_hi = hm.astype(jnp.bfloat16)
    lt = _dot_nt(wr_ref[...], hm_hi)
    logits = lt[:N_EXPERTS] + lt[N_EXPERTS:] + br_ref[...]
    e = jnp.exp(logits - jnp.max(logits, axis=0, keepdims=True))

    best = ga = gb = e1 = e2 = cls = None
    for c in range(N_CLASSES):
        a, b2 = int(CLASS_E1[c]), int(CLASS_E2[c])
        ea, eb = e[a:a + 1, :], e[b2:b2 + 1, :]
        s = ea + eb
        if best is None:
            best, ga, gb = s, ea, eb
            e1 = jnp.full_like(s, float(a))
            e2 = jnp.full_like(s, float(b2))
            cls = jnp.zeros_like(s)
        else:
            better = s > best
            best = jnp.where(better, s, best)
            ga = jnp.where(better, ea, ga)
            gb = jnp.where(better, eb, gb)
            e1 = jnp.where(better, float(a), e1)
            e2 = jnp.where(better, float(b2), e2)
            cls = jnp.where(better, float(c), cls)
    inv = 1.0 / best
    row = lax.broadcasted_iota(jnp.int32, (ROUTE_ROWS, tm), 0)
    rec = jnp.where(row == 0, ga * inv,
          jnp.where(row == 1, gb * inv,
          jnp.where(row == 2, e1,
          jnp.where(row == 3, e2,
          jnp.where(row == 4, cls, 0.0)))))
    route_ref[:, rows] = rec
    wide = jnp.concatenate([rec, jnp.zeros((LANES - ROUTE_ROWS, tm), jnp.float32)], axis=0)
    pay_ref[rows, D_MODEL:] = wide.T


def _outproj_call(h, mp, att, sg, w_out, g1, n2, sh2, sc2, wr, br, tm):
    n = h.shape[0]
    row = lambda i: (i, 0)
    fixed = lambda i: (0, 0)
    vec = pl.BlockSpec((1, D_MODEL), fixed)
    return pl.pallas_call(
        _outproj_kernel,
        grid=(n // tm,),
        in_specs=[
            pl.BlockSpec((tm, D_MODEL), row),
            pl.BlockSpec((tm, POOL_DIM), row),
            pl.BlockSpec((tm, NA_DIM), row),
            pl.BlockSpec((tm, SG_DIM), row),
            pl.BlockSpec((D_MODEL, D_MODEL), fixed),
            vec, vec, vec, vec,
            pl.BlockSpec((2 * N_EXPERTS, D_MODEL), fixed),
            pl.BlockSpec((N_EXPERTS, 1), fixed),
        ],
        out_specs=[
            pl.BlockSpec((tm, D_MODEL), row),
            pl.BlockSpec((tm, PAYLOAD_W), row),
            pl.BlockSpec((ROUTE_ROWS, tm), lambda i: (0, i)),
        ],
        out_shape=[
            jax.ShapeDtypeStruct((n, D_MODEL), jnp.float32),
            jax.ShapeDtypeStruct((n, PAYLOAD_W), jnp.float32),
            jax.ShapeDtypeStruct((ROUTE_ROWS, n), jnp.float32),
        ],
        compiler_params=_cparams(("arbitrary",)),
        name="outproj",
    )(h, mp, att, sg, w_out, g1, n2, sh2, sc2, wr, br)


def _expert_pair(x, ga, gb, wga, wua, wda, wgb, wub, wdb):
    ha = (_silu(_dot(x, wga)) * _dot(x, wua) * ga).astype(jnp.bfloat16)
    hb = (_silu(_dot(x, wgb)) * _dot(x, wub) * gb).astype(jnp.bfloat16)
    return _dot(ha, wda) + _dot(hb, wdb)


def _moe_sorted_kernel(blk_ref, e1_ref, e2_ref, nact_ref, pay_ref,
                       wga_ref, wua_ref, wda_ref, wgb_ref, wub_ref, wdb_ref, o_ref):
    i = pl.program_id(0)

    @pl.when(i < nact_ref[0])
    def _():
        x = pay_ref[:, 0:D_MODEL].astype(jnp.bfloat16)
        ga = pay_ref[:, D_MODEL:D_MODEL + 1]
        gb = pay_ref[:, D_MODEL + 1:D_MODEL + 2]
        o_ref[...] = _expert_pair(x, ga, gb, wga_ref[...], wua_ref[...], wda_ref[...],
                                  wgb_ref[...], wub_ref[...], wdb_ref[...])


def _moe_sorted_call(blk, e1, e2, nact, pay_sorted, wg, wu, wd):
    n_tiles = blk.shape[0]
    rows = lambda i, blk, e1, e2, na: (blk[i], 0)
    wa = lambda i, blk, e1, e2, na: (e1[i], 0, 0)
    wb = lambda i, blk, e1, e2, na: (e2[i], 0, 0)
    up = lambda m: pl.BlockSpec((None, D_MODEL, D_EXPERT), m)
    down = lambda m: pl.BlockSpec((None, D_EXPERT, D_MODEL), m)
    return pl.pallas_call(
        _moe_sorted_kernel,
        grid_spec=pltpu.PrefetchScalarGridSpec(
            num_scalar_prefetch=4,
            grid=(n_tiles,),
            in_specs=[pl.BlockSpec((MOE_TM, PAYLOAD_W), rows),
                      up(wa), up(wa), down(wa), up(wb), up(wb), down(wb)],
            out_specs=pl.BlockSpec((MOE_TM, D_MODEL), rows),
        ),
        out_shape=jax.ShapeDtypeStruct((n_tiles * MOE_TM, D_MODEL), jnp.float32),
        compiler_params=_cparams(("arbitrary",)),
        name="moe_sorted",
    )(blk, e1, e2, nact, pay_sorted, wg, wu, wd, wg, wu, wd)


def _moe_dense_kernel(pay_ref, wg_ref, wu_ref, wd_ref, o_ref):
    e = pl.program_id(0)

    @pl.when(e == 0)
    def _():
        o_ref[...] = jnp.zeros_like(o_ref)

    x = pay_ref[:, 0:D_MODEL].astype(jnp.bfloat16)
    ef = e.astype(jnp.float32)
    gate = (jnp.where(pay_ref[:, D_MODEL + 2:D_MODEL + 3] == ef, pay_ref[:, D_MODEL:D_MODEL + 1], 0.0)
            + jnp.where(pay_ref[:, D_MODEL + 3:D_MODEL + 4] == ef, pay_ref[:, D_MODEL + 1:D_MODEL + 2], 0.0))
    he = (_silu(_dot(x, wg_ref[...])) * _dot(x, wu_ref[...]) * gate).astype(jnp.bfloat16)
    o_ref[...] += _dot(he, wd_ref[...])


def _moe_dense_call(pay, wg, wu, wd):
    n = pay.shape[0]
    return pl.pallas_call(
        _moe_dense_kernel,
        grid=(N_EXPERTS,),
        in_specs=[pl.BlockSpec((n, PAYLOAD_W), lambda e: (0, 0)),
                  pl.BlockSpec((None, D_MODEL, D_EXPERT), lambda e: (e, 0, 0)),
                  pl.BlockSpec((None, D_MODEL, D_EXPERT), lambda e: (e, 0, 0)),
                  pl.BlockSpec((None, D_EXPERT, D_MODEL), lambda e: (e, 0, 0))],
        out_specs=pl.BlockSpec((n, D_MODEL), lambda e: (0, 0)),
        out_shape=jax.ShapeDtypeStruct((n, D_MODEL), jnp.float32),
        compiler_params=_cparams(("arbitrary",)),
        name="moe_dense",
    )(pay, wg, wu, wd)


def _residual_kernel(h_ref, y_ref, g_ref, o_ref):
    o_ref[...] = h_ref[...] + g_ref[...] * y_ref[...]


def _residual_call(h, y, g, tm):
    n = h.shape[0]
    row = pl.BlockSpec((tm, D_MODEL), lambda i: (i, 0))
    return pl.pallas_call(
        _residual_kernel,
        grid=(n // tm,),
        in_specs=[row, row, pl.BlockSpec((1, D_MODEL), lambda i: (0, 0))],
        out_specs=row,
        out_shape=jax.ShapeDtypeStruct((n, D_MODEL), jnp.float32),
        compiler_params=_cparams(("arbitrary",)),
        name="residual",
    )(h, y, g)


SC_ROWS = 32


SC_CORES = 2
SC_SUBCORES = 16
SC_WORKERS = SC_CORES * SC_SUBCORES


def _sc_mesh():
    return plsc.VectorSubcoreMesh(core_axis_name="core", subcore_axis_name="subcore")


def _sc_worker():
    return lax.axis_index("subcore") * SC_CORES + lax.axis_index("core")


def _scatter_rows(x, dest, n_out):
    n, w = x.shape

    per_worker = n // SC_WORKERS
    assert per_worker % SC_ROWS == 0

    @functools.partial(pl.kernel, out_type=jax.ShapeDtypeStruct((n_out, w), x.dtype), mesh=_sc_mesh(),
                       scratch_types=[pltpu.VMEM((SC_ROWS,), jnp.int32), pltpu.VMEM((SC_ROWS, w), x.dtype)])
    def scatter(x_hbm, i_hbm, o_hbm, idx_v, rows_v):
        first = _sc_worker() * per_worker

        @pl.loop(0, per_worker // SC_ROWS)
        def _(i):
            base = pl.multiple_of(first + i * SC_ROWS, SC_ROWS)
            pltpu.sync_copy(i_hbm.at[pl.ds(base, SC_ROWS)], idx_v)
            pltpu.sync_copy(x_hbm.at[pl.ds(base, SC_ROWS)], rows_v)
            pltpu.sync_copy(rows_v, o_hbm.at[idx_v])

    return scatter(x, dest)


def _gather_rows(x, src):
    n = src.shape[0]
    w = x.shape[1]

    per_worker = n // SC_WORKERS
    assert per_worker % SC_ROWS == 0

    @functools.partial(pl.kernel, out_type=jax.ShapeDtypeStruct((n, w), x.dtype), mesh=_sc_mesh(),
                       scratch_types=[pltpu.VMEM((SC_ROWS,), jnp.int32), pltpu.VMEM((SC_ROWS, w), x.dtype)])
    def gather(x_hbm, i_hbm, o_hbm, idx_v, rows_v):
        first = _sc_worker() * per_worker

        @pl.loop(0, per_worker // SC_ROWS)
        def _(i):
            base = pl.multiple_of(first + i * SC_ROWS, SC_ROWS)
            pltpu.sync_copy(i_hbm.at[pl.ds(base, SC_ROWS)], idx_v)
            pltpu.sync_copy(x_hbm.at[idx_v], rows_v)
            pltpu.sync_copy(rows_v, o_hbm.at[pl.ds(base, SC_ROWS)])

    return gather(x, src)


def _routing_plan(cls, n_tiles):
    onehot = (cls[:, None] == jnp.arange(N_CLASSES, dtype=jnp.int32)[None, :]).astype(jnp.int32)
    counts = jnp.sum(onehot, axis=0)
    rank = jnp.sum((jnp.cumsum(onehot, axis=0) - onehot) * onehot, axis=1)
    tiles = (counts + MOE_TM - 1) // MOE_TM
    tile_end = jnp.cumsum(tiles)
    tile_start = tile_end - tiles
    dest = jnp.sum(onehot * tile_start[None, :], axis=1) * MOE_TM + rank
    nact = tile_end[-1]
    blk = jnp.minimum(jnp.arange(n_tiles, dtype=jnp.int32), nact - 1)
    tile_cls = jnp.sum((blk[:, None] >= tile_end[None, :]).astype(jnp.int32), axis=1)
    e1 = jnp.asarray(CLASS_E1)[tile_cls]
    e2 = jnp.asarray(CLASS_E2)[tile_cls]
    return dest.astype(jnp.int32), blk, e1, e2, nact.reshape(1).astype(jnp.int32)


def _block_diag_mean(width):
    idx = np.arange(width) // HEAD_DIM
    return jnp.asarray((idx[:, None] == idx[None, :]).astype(np.float32) / HEAD_DIM, jnp.bfloat16)


def _row_tile(n):
    return 512 if n % 512 == 0 else 256


def kernel(x, c, ctx, c_ctx, w_ada, b_ada, norm1, w_in, pool_w, pool_scale, q_norm, k_norm, rpb,
           sg_w, sg_b, sg_norm, w_out, norm2, w_router, b_router, w_gate, w_up, w_down):
    depth = w_ada.shape[0]
    n = x.shape[1]
    lc = ctx.shape[1]
    bf = jnp.bfloat16
    lat_stream = (x[0],)
    h_ctx = ctx[0]

    cond = jnp.zeros((SUBLANES, D_MODEL), jnp.float32).at[0].set(c[0]).at[1].set(c_ctx)
    mod = _ada_call(cond, w_ada, b_ada)

    b512 = _block_diag_mean(NA_DIM)
    b256 = _block_diag_mean(SG_DIM)
    wr_t = w_router.T
    wr_hi = wr_t.astype(bf)
    wr_lo = (wr_t - wr_hi.astype(jnp.float32)).astype(bf)
    wr = jnp.concatenate([wr_hi, wr_lo], axis=0)
    br = b_router.reshape(N_EXPERTS, 1)
    n_tiles = n // MOE_TM + N_CLASSES

    for l in range(depth):
        last = l == depth - 1
        w_in_l = w_in[l].astype(bf)
        w_out_l = w_out[l].astype(bf)
        wg_l, wu_l, wd_l = w_gate[l].astype(bf), w_up[l].astype(bf), w_down[l].astype(bf)
        qg = (q_norm[l] * (HEAD_DIM ** -0.5 * LOG2_E)).reshape(1, NA_DIM)
        kg = k_norm[l].reshape(1, NA_DIM)
        sgn = sg_norm[l].reshape(1, SG_DIM)
        sgw = sg_w[l].astype(bf).reshape(SG_DIM // LANES, 2 * SG_CHUNK, SG_CHUNK)
        sgb = jnp.broadcast_to(sg_b[l].reshape(SG_DIM // LANES, 2 * SG_CHUNK, 1),
                               (SG_DIM // LANES, 2 * SG_CHUNK, LANES))
        pool_bd = jax.scipy.linalg.block_diag(*[pool_w[l, g] for g in range(len(POOL_WINDOWS))]).astype(bf)
        pscale = pool_scale[l].reshape(1, POOL_DIM)
        bias = _natten_bias(rpb[l])
        n1 = norm1[l].reshape(1, D_MODEL)
        n2 = norm2[l].reshape(1, D_MODEL)

        def mods(row):
            return [mod[l, row:row + 1, i * D_MODEL:(i + 1) * D_MODEL] for i in range(6)]

        sh1, sc1, g1, sh2, sc2, g2 = mods(0)
        csh1, csc1, cg1, csh2, csc2, cg2 = mods(1)

        tc = _row_tile(lc)
        pool_c, q_c, k_c, v_c, sg_c = _inproj_call((h_ctx,), n1, csh1, csc1, w_in_l, qg, kg, b512, b256,
                                                   sgn, sgw, sgb, tc)

        tm = _row_tile(n)
        outs = _inproj_call(lat_stream, n1, sh1, sc1, w_in_l, qg, kg, b512, b256, sgn, sgw, sgb, tm)
        pool_a, q, k, v, sg = outs[:5]
        h_lat = outs[5] if len(lat_stream) == 3 else lat_stream[0]
        mix_pool = _pool_call(pool_a, pool_bd, pscale, tm)
        att = _natten_call(q, k, v, k_c, v_c, bias)
        h1, pay, route = _outproj_call(h_lat, mix_pool, att, sg, w_out_l, g1, n2, sh2, sc2, wr, br, tm)
        cls = route[4].astype(jnp.int32)
        dest, blk, e1, e2, nact = _routing_plan(cls, n_tiles)
        pay_sorted = _scatter_rows(pay, dest, n_tiles * MOE_TM)
        y_sorted = _moe_sorted_call(blk, e1, e2, nact, pay_sorted, wg_l, wu_l, wd_l)
        y = _gather_rows(y_sorted, dest)
        lat_stream = (h1, y, g2)

        if not last:
            mix_pool_c = _pool_call(pool_c, pool_bd, pscale, tc)
            att_c = _ctxatt_call(q_c, k_c, v_c)
            h1_c, pay_c, _ = _outproj_call(h_ctx, mix_pool_c, att_c, sg_c, w_out_l, cg1, n2, csh2, csc2,
                                           wr, br, tc)
            y_c = _moe_dense_call(pay_c, wg_l, wu_l, wd_l)
            h_ctx = _residual_call(h1_c, y_c, cg2, tc)

    return _residual_call(*lat_stream, _row_tile(n))[None]
```

```python
import functools

import jax
import jax.numpy as jnp
import numpy as np
from jax import lax
from jax.experimental import pallas as pl
from jax.experimental.pallas import tpu as pltpu
from jax.experimental.pallas import tpu_sc as plsc

D_MODEL = 1024
GRID_W = 64
HEAD_DIM = 64
POOL_WINDOWS = (2, 4, 8, 16)
POOL_DIM = 256
NA_HEADS = 8
NA_DIM = 512
NA_WIN_ROWS = 8
NA_WIN_COLS = 16
SG_DIM = 256
SG_CHUNK = 128
Q_OFF = POOL_DIM
K_OFF = Q_OFF + NA_DIM
V_OFF = K_OFF + NA_DIM
U_OFF = V_OFF + NA_DIM
G_OFF = U_OFF + SG_DIM
IN_DIM = G_OFF + SG_DIM
N_EXPERTS = 16
GROUP_SIZE = 4
D_EXPERT = 512
EPS = 1e-6

LANES = 128
SUBLANES = 8
HEAD_PAIRS = NA_DIM // LANES
VMEM_LIMIT = 48 * 1024 * 1024

PAIRS = ((0, 1), (0, 2), (0, 3), (1, 2), (1, 3), (2, 3))
N_CLASSES = (N_EXPERTS // GROUP_SIZE) * len(PAIRS)
CLASS_E1 = np.array([4 * g + i for g in range(4) for (i, j) in PAIRS], np.int32)
CLASS_E2 = np.array([4 * g + j for g in range(4) for (i, j) in PAIRS], np.int32)

ROUTE_ROWS = 8
HALF_D = D_MODEL // 2
PAYLOAD_W = HALF_D + LANES
MOE_TM = 256
NEG_BIG = -1e30
LOG2_E = 1.4426950408889634


def _cparams(sem):
    return pltpu.CompilerParams(dimension_semantics=sem, vmem_limit_bytes=VMEM_LIMIT)


def _dot(a, b):
    return jnp.dot(a, b, preferred_element_type=jnp.float32)


def _dot_nt(a, b):
    return lax.dot_general(a, b, (((1,), (1,)), ((), ())), preferred_element_type=jnp.float32)


def _gelu_tanh(x):
    return 0.5 * x * (1.0 + jnp.tanh(0.7978845608028654 * (x + 0.044715 * (x * x * x))))


def _silu(x):
    return x * (1.0 / (1.0 + jnp.exp(-x)))


def _pack_halves(x):
    w = x.shape[1] // 2
    lo = pltpu.bitcast(x[:, :w].astype(jnp.bfloat16).astype(jnp.float32), jnp.uint32) >> 16
    hi = pltpu.bitcast(x[:, w:].astype(jnp.bfloat16).astype(jnp.float32), jnp.uint32) & jnp.uint32(0xFFFF0000)
    return lo | hi


def _unpack_halves(words):
    lo = pltpu.bitcast(words << 16, jnp.float32)
    hi = pltpu.bitcast(words & jnp.uint32(0xFFFF0000), jnp.float32)
    return lo, hi


def _ada_kernel(cond_ref, w_ref, b_ref, o_ref):
    cond = _silu(cond_ref[...])
    o_ref[...] = jnp.dot(cond, w_ref[...], preferred_element_type=jnp.float32,
                         precision=lax.Precision.HIGHEST) + b_ref[...]


def _ada_call(cond, w_ada, b_ada):
    depth = w_ada.shape[0]
    tn = 1536
    return pl.pallas_call(
        _ada_kernel,
        grid=(depth, 6 * D_MODEL // tn),
        in_specs=[
            pl.BlockSpec((SUBLANES, D_MODEL), lambda l, j: (0, 0)),
            pl.BlockSpec((None, D_MODEL, tn), lambda l, j: (l, 0, j)),
            pl.BlockSpec((None, 1, tn), lambda l, j: (l, 0, j)),
        ],
        out_specs=pl.BlockSpec((None, SUBLANES, tn), lambda l, j: (l, 0, j)),
        out_shape=jax.ShapeDtypeStruct((depth, SUBLANES, 6 * D_MODEL), jnp.float32),
        compiler_params=_cparams(("arbitrary", "arbitrary")),
        name="adaln",
    )(cond, w_ada, b_ada.reshape(depth, 1, 6 * D_MODEL))


def _inproj_kernel(*refs, pending):
    if pending:
        h_ref, y_ref, g_ref = refs[:3]
        refs = refs[3:]
        hres_ref = refs[-1]
        refs = refs[:-1]
        x = h_ref[...] + g_ref[...] * _moe_out(y_ref)
        hres_ref[...] = x
    else:
        h_ref = refs[0]
        refs = refs[1:]
        x = h_ref[...]
    (n1_ref, sh_ref, sc_ref, w_ref, qg_ref, kg_ref, b512_ref, b256_ref, sgn_ref, sgw_ref, sgb_ref,
     pool_ref, q_ref, k_ref, v_ref, sg_ref) = refs
    tm = h_ref.shape[0]
    ms = jnp.mean(x * x, axis=-1, keepdims=True)
    hn = ((x * lax.rsqrt(ms + EPS)) * (n1_ref[...] * (1.0 + sc_ref[...])) + sh_ref[...]).astype(jnp.bfloat16)

    pool_ref[...] = _dot(hn, w_ref[:, 0:Q_OFF])

    a_q = _dot(hn, w_ref[:, Q_OFF:K_OFF])
    msq = _dot((a_q * a_q).astype(jnp.bfloat16), b512_ref[...])
    q_ref[...] = (a_q * lax.rsqrt(msq + EPS) * qg_ref[...]).astype(jnp.bfloat16)

    a_k = _dot(hn, w_ref[:, K_OFF:V_OFF])
    msk = _dot((a_k * a_k).astype(jnp.bfloat16), b512_ref[...])
    k_ref[...] = (a_k * lax.rsqrt(msk + EPS) * kg_ref[...]).astype(jnp.bfloat16)

    v_ref[...] = _dot(hn, w_ref[:, V_OFF:U_OFF]).astype(jnp.bfloat16)

    u = _gelu_tanh(_dot(hn, w_ref[:, U_OFF:G_OFF]))
    gv = _gelu_tanh(_dot(hn, w_ref[:, G_OFF:IN_DIM]))
    msv = _dot((gv * gv).astype(jnp.bfloat16), b256_ref[...])
    vn = (gv * lax.rsqrt(msv + EPS) * sgn_ref[...]).astype(jnp.bfloat16)
    low = lax.broadcasted_iota(jnp.int32, (SG_CHUNK, LANES), 1) < HEAD_DIM
    for c in range(tm // SG_CHUNK):
        rows = slice(c * SG_CHUNK, (c + 1) * SG_CHUNK)
        for s in range(SG_DIM // LANES):
            cols = slice(s * LANES, (s + 1) * LANES)
            m = _dot(sgw_ref[s], vn[rows, cols]) + sgb_ref[s]
            mixed = jnp.where(low, m[:SG_CHUNK], m[SG_CHUNK:])
            sg_ref[rows, cols] = (u[rows, cols] * mixed).astype(jnp.bfloat16)


def _inproj_call(stream, n1, sh, sc, w_in, layer, qg, kg, b512, b256, sgn, sgw, sgb, tm):
    pending = len(stream) == 3
    n = stream[0].shape[0]
    row = lambda i: (i, 0)
    fixed2 = lambda i: (0, 0)
    fixed3 = lambda i: (0, 0, 0)
    vec = lambda w: pl.BlockSpec((1, w), fixed2)
    rows = pl.BlockSpec((tm, D_MODEL), row)
    stream_specs = [rows]
    if pending:
        stream_specs += [pl.BlockSpec((tm, stream[1].shape[1]), row), vec(D_MODEL)]
    extra_out_specs = [rows] if pending else []
    extra_out_shape = [jax.ShapeDtypeStruct((n, D_MODEL), jnp.float32)] if pending else []
    return pl.pallas_call(
        functools.partial(_inproj_kernel, pending=pending),
        grid=(n // tm,),
        in_specs=stream_specs + [
            vec(D_MODEL), vec(D_MODEL), vec(D_MODEL),
            pl.BlockSpec((None, D_MODEL, IN_DIM), lambda i: (layer, 0, 0)),
            vec(NA_DIM), vec(NA_DIM),
            pl.BlockSpec((NA_DIM, NA_DIM), fixed2),
            pl.BlockSpec((SG_DIM, SG_DIM), fixed2),
            vec(SG_DIM),
            pl.BlockSpec((SG_DIM // LANES, 2 * SG_CHUNK, SG_CHUNK), fixed3),
            pl.BlockSpec((SG_DIM // LANES, 2 * SG_CHUNK, LANES), fixed3),
        ],
        out_specs=[
            pl.BlockSpec((tm, POOL_DIM), row),
            pl.BlockSpec((tm, NA_DIM), row),
            pl.BlockSpec((tm, NA_DIM), row),
            pl.BlockSpec((tm, NA_DIM), row),
            pl.BlockSpec((tm, SG_DIM), row),
        ] + extra_out_specs,
        out_shape=[
            jax.ShapeDtypeStruct((n, POOL_DIM), jnp.float32),
            jax.ShapeDtypeStruct((n, NA_DIM), jnp.bfloat16),
            jax.ShapeDtypeStruct((n, NA_DIM), jnp.bfloat16),
            jax.ShapeDtypeStruct((n, NA_DIM), jnp.bfloat16),
            jax.ShapeDtypeStruct((n, SG_DIM), jnp.bfloat16),
        ] + extra_out_shape,
        compiler_params=_cparams(("arbitrary",)),
        name="inproj",
    )(*stream, n1, sh, sc, w_in, qg, kg, b512, b256, sgn, sgw, sgb)


POOL_HALO = 8


def _pool_kernel(prev_ref, x_ref, next_ref, w_ref, scale_ref, o_ref, xe_ref, *, seq_len):
    tm = x_ref.shape[0]
    i = pl.program_id(0)
    last = pl.num_programs(0) - 1
    xe_ref[0:POOL_HALO, :] = jnp.where(i > 0, prev_ref[...], 0.0)
    xe_ref[POOL_HALO:POOL_HALO + tm, :] = x_ref[...]
    xe_ref[POOL_HALO + tm:, :] = jnp.where(i < last, next_ref[...], 0.0)

    t = i * tm + lax.broadcasted_iota(jnp.int32, (tm, LANES), 0)
    low = lax.broadcasted_iota(jnp.int32, (tm, LANES), 1) < HEAD_DIM

    def count(half):
        return (jnp.minimum(t + half, seq_len) - jnp.maximum(t - half, 0)).astype(jnp.float32)

    def window_sums(xs, n_levels):
        sums = []
        s = xs
        for k in range(n_levels):
            step = 1 << k
            s = s[:-step] + s[step:]
            sums.append(s)
        return sums

    outs = []
    for half_block, windows in enumerate(((2, 4), (8, 16))):
        xs = xe_ref[:, half_block * LANES:(half_block + 1) * LANES]
        sums = window_sums(xs, int(np.log2(windows[1])))
        parts = []
        for w in windows:
            half = w // 2
            s = sums[int(np.log2(w)) - 1][POOL_HALO - half:POOL_HALO - half + tm]
            parts.append(s / count(half))
        mean = jnp.where(low, parts[0], parts[1])
        outs.append(mean - xs[POOL_HALO:POOL_HALO + tm])
    d = jnp.concatenate(outs, axis=-1).astype(jnp.bfloat16)
    o_ref[...] = (_dot(d, w_ref[...]) * scale_ref[...]).astype(jnp.bfloat16)


def _pool_call(a_pool, w_bd, scale, tm):
    n = a_pool.shape[0]
    nb8 = n // POOL_HALO
    r8 = tm // POOL_HALO
    return pl.pallas_call(
        functools.partial(_pool_kernel, seq_len=n),
        grid=(n // tm,),
        in_specs=[
            pl.BlockSpec((POOL_HALO, POOL_DIM), lambda i: (jnp.maximum(i * r8 - 1, 0), 0)),
            pl.BlockSpec((tm, POOL_DIM), lambda i: (i, 0)),
            pl.BlockSpec((POOL_HALO, POOL_DIM), lambda i: (jnp.minimum((i + 1) * r8, nb8 - 1), 0)),
            pl.BlockSpec((POOL_DIM, POOL_DIM), lambda i: (0, 0)),
            pl.BlockSpec((1, POOL_DIM), lambda i: (0, 0)),
        ],
        out_specs=pl.BlockSpec((tm, POOL_DIM), lambda i: (i, 0)),
        out_shape=jax.ShapeDtypeStruct((n, POOL_DIM), jnp.bfloat16),
        scratch_shapes=[pltpu.VMEM((tm + 2 * POOL_HALO, POOL_DIM), jnp.float32)],
        compiler_params=_cparams(("arbitrary",)),
        name="pool",
    )(a_pool, a_pool, a_pool, w_bd, scale)


NA_ROWS_PER_BLOCK = 16
NA_GROUP_ROWS = 4
NA_WINDOW_ROWS = NA_GROUP_ROWS + NA_WIN_ROWS
NA_BLOCK = NA_ROWS_PER_BLOCK * GRID_W
NA_GROUP = NA_GROUP_ROWS * GRID_W
NA_HALO = (NA_WIN_ROWS // 2) * GRID_W
NA_WINDOW = NA_WINDOW_ROWS * GRID_W
NA_EDGE_FIRST, NA_EDGE_NONE, NA_EDGE_LAST = 0, 1, 2


def _stack_heads(x, low):
    zero = jnp.zeros_like(x)
    return jnp.concatenate([jnp.where(low, x, zero), jnp.where(low, zero, x)], axis=0)


def _natten_kernel(q_ref, kp_ref, kc_ref, kn_ref, vp_ref, vc_ref, vn_ref, kx_ref, vx_ref, bias_ref,
                   wg_ref, wu_ref, wd_ref,
                   o_ref, wg_bf_ref, wu_bf_ref, wd_bf_ref, kwin_ref, vwin_ref, vxe_ref, *, grid_rows):
    b = pl.program_id(1)
    wg_bf_ref[...] = wg_ref[...].astype(jnp.bfloat16)
    wu_bf_ref[...] = wu_ref[...].astype(jnp.bfloat16)
    wd_bf_ref[...] = wd_ref[...].astype(jnp.bfloat16)
    kwin_ref[0:NA_HALO, :] = kp_ref[...]
    kwin_ref[NA_HALO:NA_HALO + NA_BLOCK, :] = kc_ref[...]
    kwin_ref[NA_HALO + NA_BLOCK:, :] = kn_ref[...]
    vwin_ref[0:NA_HALO, 0:LANES] = vp_ref[...]
    vwin_ref[NA_HALO:NA_HALO + NA_BLOCK, 0:LANES] = vc_ref[...]
    vwin_ref[NA_HALO + NA_BLOCK:, 0:LANES] = vn_ref[...]
    vwin_ref[:, LANES:] = jnp.ones((vwin_ref.shape[0], LANES), jnp.bfloat16)
    vxe_ref[:, 0:LANES] = vx_ref[...]
    vxe_ref[:, LANES:] = jnp.ones((vxe_ref.shape[0], LANES), jnp.bfloat16)
    low_q = lax.broadcasted_iota(jnp.int32, (NA_GROUP, LANES), 1) < HEAD_DIM

    for g in range(NA_ROWS_PER_BLOCK // NA_GROUP_ROWS):
        r0 = b * NA_ROWS_PER_BLOCK + g * NA_GROUP_ROWS
        ws = jnp.clip(r0 - NA_WIN_ROWS // 2, 0, grid_rows - NA_WINDOW_ROWS)
        edge = jnp.where(r0 == 0, NA_EDGE_FIRST,
                         jnp.where(r0 == grid_rows - NA_GROUP_ROWS, NA_EDGE_LAST, NA_EDGE_NONE))
        start = pl.multiple_of((ws - b * NA_ROWS_PER_BLOCK + NA_WIN_ROWS // 2) * GRID_W, GRID_W)
        qrows = slice(g * NA_GROUP, (g + 1) * NA_GROUP)
        lhs = _stack_heads(q_ref[qrows, :], low_q)
        kl = kwin_ref[pl.ds(start, NA_WINDOW), :]
        vl = vwin_ref[pl.ds(start, NA_WINDOW), :]
        s = jnp.concatenate([_dot_nt(lhs, kl) + bias_ref[edge], _dot_nt(lhs, kx_ref[...])], axis=-1)
        m = jnp.max(s, axis=-1, keepdims=True)
        pb = jnp.exp2((s - m).astype(jnp.bfloat16))
        o = _dot(pb[:, :NA_WINDOW], vl) + _dot(pb[:, NA_WINDOW:], vxe_ref[...])
        o = o[:, :LANES] * (1.0 / o[:, LANES:])
        o_ref[qrows, :] = jnp.where(low_q, o[:NA_GROUP], o[NA_GROUP:]).astype(jnp.bfloat16)


def _natten_call(q, k, v, k_ctx, v_ctx, bias, w_gate, w_up, w_down, layer):
    n = q.shape[0]
    grid_rows = n // GRID_W
    assert grid_rows % NA_ROWS_PER_BLOCK == 0 and grid_rows >= 2 * NA_ROWS_PER_BLOCK
    nblk = n // NA_BLOCK
    steps = HEAD_PAIRS * nblk
    depth = w_gate.shape[0]
    up_rows = N_EXPERTS * D_MODEL
    down_rows = N_EXPERTS * D_EXPERT
    assert up_rows % steps == 0 and down_rows % steps == 0
    wg2 = w_gate.reshape(depth * up_rows, D_EXPERT)
    wu2 = w_up.reshape(depth * up_rows, D_EXPERT)
    wd2 = w_down.reshape(depth * down_rows, D_MODEL)
    up_in = pl.BlockSpec((up_rows // steps, D_EXPERT), lambda p, b: (layer * steps + p * nblk + b, 0))
    down_in = pl.BlockSpec((down_rows // steps, D_MODEL), lambda p, b: (layer * steps + p * nblk + b, 0))
    up_out = pl.BlockSpec((up_rows // steps, D_EXPERT), lambda p, b: (p * nblk + b, 0))
    down_out = pl.BlockSpec((down_rows // steps, D_MODEL), lambda p, b: (p * nblk + b, 0))
    nhalo = n // NA_HALO
    hb = NA_BLOCK // NA_HALO
    cur = pl.BlockSpec((NA_BLOCK, LANES), lambda p, b: (b, p))
    prev = pl.BlockSpec((NA_HALO, LANES), lambda p, b: (jnp.maximum(b * hb - 1, 0), p))
    nxt = pl.BlockSpec((NA_HALO, LANES), lambda p, b: (jnp.minimum((b + 1) * hb, nhalo - 1), p))
    ctx = pl.BlockSpec((k_ctx.shape[0], LANES), lambda p, b: (0, p))
    att, wg_bf, wu_bf, wd_bf = pl.pallas_call(
        functools.partial(_natten_kernel, grid_rows=grid_rows),
        grid=(HEAD_PAIRS, nblk),
        in_specs=[cur, prev, cur, nxt, prev, cur, nxt, ctx, ctx,
                  pl.BlockSpec((None, 3, 2 * NA_GROUP, NA_WINDOW), lambda p, b: (p, 0, 0, 0)),
                  up_in, up_in, down_in],
        out_specs=[cur, up_out, up_out, down_out],
        out_shape=[jax.ShapeDtypeStruct((n, NA_DIM), jnp.bfloat16),
                   jax.ShapeDtypeStruct((up_rows, D_EXPERT), jnp.bfloat16),
                   jax.ShapeDtypeStruct((up_rows, D_EXPERT), jnp.bfloat16),
                   jax.ShapeDtypeStruct((down_rows, D_MODEL), jnp.bfloat16)],
        scratch_shapes=[pltpu.VMEM((NA_BLOCK + 2 * NA_HALO, LANES), jnp.bfloat16),
                        pltpu.VMEM((NA_BLOCK + 2 * NA_HALO, 2 * LANES), jnp.bfloat16),
                        pltpu.VMEM((k_ctx.shape[0], 2 * LANES), jnp.bfloat16)],
        compiler_params=_cparams(("arbitrary", "arbitrary")),
        name="natten",
    )(q, k, k, k, v, v, v, k_ctx, v_ctx, bias, wg2, wu2, wd2)
    return (att, wg_bf.reshape(N_EXPERTS, D_MODEL, D_EXPERT), wu_bf.reshape(N_EXPERTS, D_MODEL, D_EXPERT),
            wd_bf.reshape(N_EXPERTS, D_EXPERT, D_MODEL))


def _natten_bias(rpb):
    cols = np.arange(GRID_W)
    col_start = np.clip(cols - NA_WIN_COLS // 2, 0, GRID_W - NA_WIN_COLS)
    kc = np.arange(GRID_W)
    in_win = (kc[None, :] >= col_start[:, None]) & (kc[None, :] < col_start[:, None] + NA_WIN_COLS)
    dc = kc[None, :] - cols[:, None] + NA_WIN_COLS - 1
    sel = (np.arange(2 * NA_WIN_COLS - 1)[:, None, None] == dc[None]) & in_win[None]
    t2 = jnp.einsum("hdj,jqk->hdqk", rpb, jnp.asarray(sel, jnp.float32), precision=lax.Precision.HIGHEST)
    t2 = jnp.where(in_win[None, None], t2 * LOG2_E, NEG_BIG)
    neg = jnp.full((NA_HEADS, 1, GRID_W, GRID_W), NEG_BIG, jnp.float32)
    t2e = jnp.concatenate([neg, t2, neg], axis=1)
    u = jnp.concatenate([t2e[:, :-1], t2e[:, 1:]], axis=-1)
    u = u.reshape(HEAD_PAIRS, 2, 2 * NA_WIN_ROWS, GRID_W, LANES)
    return pl.pallas_call(
        _bias_expand_kernel,
        grid=(HEAD_PAIRS,),
        in_specs=[pl.BlockSpec((None, 2, 2 * NA_WIN_ROWS, GRID_W, LANES), lambda p: (p, 0, 0, 0, 0))],
        out_specs=pl.BlockSpec((None, 3, 2 * NA_GROUP, NA_WINDOW), lambda p: (p, 0, 0, 0)),
        out_shape=jax.ShapeDtypeStruct((HEAD_PAIRS, 3, 2 * NA_GROUP, NA_WINDOW), jnp.float32),
        compiler_params=_cparams(("arbitrary",)),
        name="bias_expand",
    )(u)


def _bias_expand_kernel(u_ref, o_ref):
    spare = NA_WINDOW_ROWS - NA_WIN_ROWS
    half = NA_WIN_ROWS // 2
    low = lax.broadcasted_iota(jnp.int32, (GRID_W, LANES), 1) < GRID_W
    neg = jnp.full((GRID_W, LANES), NEG_BIG, jnp.float32)
    for edge in (NA_EDGE_FIRST, NA_EDGE_NONE, NA_EDGE_LAST):
        for j in range(NA_GROUP_ROWS):
            if edge == NA_EDGE_FIRST:
                lo, base = 0, NA_WIN_ROWS - 1 - j
            elif edge == NA_EDGE_NONE:
                lo, base = j, half - 1
            else:
                lo, base = spare, half - 1 - j
            for hd in range(2):
                rows = slice(hd * NA_GROUP + j * GRID_W, hd * NA_GROUP + (j + 1) * GRID_W)
                for i in range(NA_WINDOW_ROWS // 2):
                    a0, a1 = 2 * i, 2 * i + 1
                    ok0 = lo <= a0 < lo + NA_WIN_ROWS
                    ok1 = lo <= a1 < lo + NA_WIN_ROWS
                    if not (ok0 or ok1):
                        tile = neg
                    else:
                        tile = u_ref[hd, base + a1 - lo]
                        if not ok0:
                            tile = jnp.where(low, neg, tile)
                        if not ok1:
                            tile = jnp.where(low, tile, neg)
                    o_ref[edge, rows, i * LANES:(i + 1) * LANES] = tile


def _ctxatt_kernel(q_ref, k_ref, v_ref, o_ref):
    lc = q_ref.shape[0]
    low = lax.broadcasted_iota(jnp.int32, (lc, LANES), 1) < HEAD_DIM
    lhs = _stack_heads(q_ref[...], low)
    s = _dot_nt(lhs, k_ref[...])
    m = jnp.max(s, axis=-1, keepdims=True)
    p = jnp.exp2(s - m)
    denom = jnp.sum(p, axis=-1, keepdims=True)
    o = _dot(p.astype(jnp.bfloat16), v_ref[...]) * (1.0 / denom)
    o_ref[...] = jnp.where(low, o[:lc], o[lc:]).astype(jnp.bfloat16)


def _ctxatt_call(q, k, v):
    lc = q.shape[0]
    spec = pl.BlockSpec((lc, LANES), lambda p: (0, p))
    return pl.pallas_call(
        _ctxatt_kernel,
        grid=(HEAD_PAIRS,),
        in_specs=[spec, spec, spec],
        out_specs=spec,
        out_shape=jax.ShapeDtypeStruct((lc, NA_DIM), jnp.bfloat16),
        compiler_params=_cparams(("arbitrary",)),
        name="ctxatt",
    )(q, k, v)


OUTPROJ_CHAIN = 256


def _outproj_kernel(h_ref, mp_ref, att_ref, sg_ref, wo_ref, g1_ref, n2_ref, sh_ref, sc_ref,
                    wr_ref, br_ref, h1_ref, pay_ref, route_ref):
    for c in range(h_ref.shape[0] // OUTPROJ_CHAIN):
        rows = slice(c * OUTPROJ_CHAIN, (c + 1) * OUTPROJ_CHAIN)
        _outproj_rows(rows, h_ref, mp_ref, att_ref, sg_ref, wo_ref, g1_ref, n2_ref, sh_ref, sc_ref,
                      wr_ref, br_ref, h1_ref, pay_ref, route_ref)


def _outproj_rows(rows, h_ref, mp_ref, att_ref, sg_ref, wo_ref, g1_ref, n2_ref, sh_ref, sc_ref,
                  wr_ref, br_ref, h1_ref, pay_ref, route_ref):
    tm = OUTPROJ_CHAIN
    mix = (_dot(mp_ref[rows, :], wo_ref[0:POOL_DIM, :])
           + _dot(att_ref[rows, :], wo_ref[POOL_DIM:POOL_DIM + NA_DIM, :])
           + _dot(sg_ref[rows, :], wo_ref[POOL_DIM + NA_DIM:, :]))
    h1 = h_ref[rows, :] + g1_ref[...] * mix
    h1_ref[rows, :] = h1
    ms = jnp.mean(h1 * h1, axis=-1, keepdims=True)
    hm = (h1 * lax.rsqrt(ms + EPS)) * (n2_ref[...] * (1.0 + sc_ref[...])) + sh_ref[...]
    pay_ref[rows, 0:HALF_D] = _pack_halves(hm)

    hm_hi = hm.astype(jnp.bfloat16)
    lt = _dot_nt(wr_ref[...], hm_hi)
    logits = lt[:N_EXPERTS] + lt[N_EXPERTS:] + br_ref[...]
    e = jnp.exp(logits - jnp.max(logits, axis=0, keepdims=True))

    best = ga = gb = e1 = e2 = cls = None
    for c in range(N_CLASSES):
        a, b2 = int(CLASS_E1[c]), int(CLASS_E2[c])
        ea, eb = e[a:a + 1, :], e[b2:b2 + 1, :]
        s = ea + eb
        if best is None:
            best, ga, gb = s, ea, eb
            e1 = jnp.full_like(s, float(a))
            e2 = jnp.full_like(s, float(b2))
            cls = jnp.zeros_like(s)
        else:
            better = s > best
            best = jnp.where(better, s, best)
            ga = jnp.where(better, ea, ga)
            gb = jnp.where(better, eb, gb)
            e1 = jnp.where(better, float(a), e1)
            e2 = jnp.where(better, float(b2), e2)
            cls = jnp.where(better, float(c), cls)
    inv = 1.0 / best
    row = lax.broadcasted_iota(jnp.int32, (ROUTE_ROWS, tm), 0)
    rec = jnp.where(row == 0, ga * inv,
          jnp.where(row == 1, gb * inv,
          jnp.where(row == 2, e1,
          jnp.where(row == 3, e2,
          jnp.where(row == 4, cls, 0.0)))))
    route_ref[:, rows] = rec
    wide = jnp.concatenate([rec, jnp.zeros((LANES - ROUTE_ROWS, tm), jnp.float32)], axis=0)
    pay_ref[rows, HALF_D:] = pltpu.bitcast(wide.T, jnp.uint32)


def _outproj_call(h, mp, att, sg, w_out, layer, g1, n2, sh2, sc2, wr, br, tm):
    n = h.shape[0]
    row = lambda i: (i, 0)
    fixed = lambda i: (0, 0)
    vec = pl.BlockSpec((1, D_MODEL), fixed)
    return pl.pallas_call(
        _outproj_kernel,
        grid=(n // tm,),
        in_specs=[
            pl.BlockSpec((tm, D_MODEL), row),
            pl.BlockSpec((tm, POOL_DIM), row),
            pl.BlockSpec((tm, NA_DIM), row),
            pl.BlockSpec((tm, SG_DIM), row),
            pl.BlockSpec((None, D_MODEL, D_MODEL), lambda i: (layer, 0, 0)),
            vec, vec, vec, vec,
            pl.BlockSpec((2 * N_EXPERTS, D_MODEL), fixed),
            pl.BlockSpec((N_EXPERTS, 1), fixed),
        ],
        out_specs=[
            pl.BlockSpec((tm, D_MODEL), row),
            pl.BlockSpec((tm, PAYLOAD_W), row),
            pl.BlockSpec((ROUTE_ROWS, tm), lambda i: (0, i)),
        ],
        out_shape=[
            jax.ShapeDtypeStruct((n, D_MODEL), jnp.float32),
            jax.ShapeDtypeStruct((n, PAYLOAD_W), jnp.uint32),
            jax.ShapeDtypeStruct((ROUTE_ROWS, n), jnp.float32),
        ],
        compiler_params=_cparams(("arbitrary",)),
        name="outproj",
    )(h, mp, att, sg, w_out, g1, n2, sh2, sc2, wr, br)


def _payload_parts(pay_ref):
    lo, hi = _unpack_halves(pay_ref[:, 0:HALF_D])
    x = jnp.concatenate([lo, hi], axis=-1).astype(jnp.bfloat16)
    return x, pltpu.bitcast(pay_ref[:, HALF_D:], jnp.float32)


def _expert_pair(x, ga, gb, wga, wua, wda, wgb, wub, wdb):
    ha = (_silu(_dot(x, wga)) * _dot(x, wua) * ga).astype(jnp.bfloat16)
    hb = (_silu(_dot(x, wgb)) * _dot(x, wub) * gb).astype(jnp.bfloat16)
    return _dot(ha, wda) + _dot(hb, wdb)


def _moe_sorted_kernel(blk_ref, e1_ref, e2_ref, nact_ref, pay_ref,
                       wga_ref, wua_ref, wda_ref, wgb_ref, wub_ref, wdb_ref, o_ref):
    i = pl.program_id(0)

    @pl.when(i < nact_ref[0])
    def _():
        x, route = _payload_parts(pay_ref)
        y = _expert_pair(x, route[:, 0:1], route[:, 1:2], wga_ref[...], wua_ref[...], wda_ref[...],
                         wgb_ref[...], wub_ref[...], wdb_ref[...])
        o_ref[...] = _pack_halves(y)


def _moe_sorted_call(blk, e1, e2, nact, pay_sorted, wg, wu, wd):
    n_tiles = blk.shape[0]
    rows = lambda i, blk, e1, e2, na: (blk[i], 0)
    wa = lambda i, blk, e1, e2, na: (e1[i], 0, 0)
    wb = lambda i, blk, e1, e2, na: (e2[i], 0, 0)
    up = lambda m: pl.BlockSpec((None, D_MODEL, D_EXPERT), m)
    down = lambda m: pl.BlockSpec((None, D_EXPERT, D_MODEL), m)
    return pl.pallas_call(
        _moe_sorted_kernel,
        grid_spec=pltpu.PrefetchScalarGridSpec(
            num_scalar_prefetch=4,
            grid=(n_tiles,),
            in_specs=[pl.BlockSpec((MOE_TM, PAYLOAD_W), rows),
                      up(wa), up(wa), down(wa), up(wb), up(wb), down(wb)],
            out_specs=pl.BlockSpec((MOE_TM, HALF_D), rows),
        ),
        out_shape=jax.ShapeDtypeStruct((n_tiles * MOE_TM, HALF_D), jnp.uint32),
        compiler_params=_cparams(("arbitrary",)),
        name="moe_sorted",
    )(blk, e1, e2, nact, pay_sorted, wg, wu, wd, wg, wu, wd)


def _moe_dense_kernel(pay_ref, wg_ref, wu_ref, wd_ref, o_ref):
    e = pl.program_id(0)

    @pl.when(e == 0)
    def _():
        o_ref[...] = jnp.zeros_like(o_ref)

    x, route = _payload_parts(pay_ref)
    ef = e.astype(jnp.float32)
    gate = (jnp.where(route[:, 2:3] == ef, route[:, 0:1], 0.0)
            + jnp.where(route[:, 3:4] == ef, route[:, 1:2], 0.0))
    he = (_silu(_dot(x, wg_ref[...])) * _dot(x, wu_ref[...]) * gate).astype(jnp.bfloat16)
    o_ref[...] += _dot(he, wd_ref[...])


def _moe_dense_call(pay, wg, wu, wd):
    n = pay.shape[0]
    return pl.pallas_call(
        _moe_dense_kernel,
        grid=(N_EXPERTS,),
        in_specs=[pl.BlockSpec((n, PAYLOAD_W), lambda e: (0, 0)),
                  pl.BlockSpec((None, D_MODEL, D_EXPERT), lambda e: (e, 0, 0)),
                  pl.BlockSpec((None, D_MODEL, D_EXPERT), lambda e: (e, 0, 0)),
                  pl.BlockSpec((None, D_EXPERT, D_MODEL), lambda e: (e, 0, 0))],
        out_specs=pl.BlockSpec((n, D_MODEL), lambda e: (0, 0)),
        out_shape=jax.ShapeDtypeStruct((n, D_MODEL), jnp.float32),
        compiler_params=_cparams(("arbitrary",)),
        name="moe_dense",
    )(pay, wg, wu, wd)


def _moe_out(y_ref):
    if y_ref.dtype == jnp.uint32:
        return jnp.concatenate(_unpack_halves(y_ref[...]), axis=-1)
    return y_ref[...]


def _residual_kernel(h_ref, y_ref, g_ref, o_ref):
    o_ref[...] = h_ref[...] + g_ref[...] * _moe_out(y_ref)


def _residual_call(h, y, g, tm):
    n = h.shape[0]
    row = pl.BlockSpec((tm, D_MODEL), lambda i: (i, 0))
    return pl.pallas_call(
        _residual_kernel,
        grid=(n // tm,),
        in_specs=[row, pl.BlockSpec((tm, y.shape[1]), lambda i: (i, 0)),
                  pl.BlockSpec((1, D_MODEL), lambda i: (0, 0))],
        out_specs=row,
        out_shape=jax.ShapeDtypeStruct((n, D_MODEL), jnp.float32),
        compiler_params=_cparams(("arbitrary",)),
        name="residual",
    )(h, y, g)


SC_ROWS = 32


SC_CORES = 2
SC_SUBCORES = 16
SC_WORKERS = SC_CORES * SC_SUBCORES


def _sc_mesh():
    return plsc.VectorSubcoreMesh(core_axis_name="core", subcore_axis_name="subcore")


def _sc_worker():
    return lax.axis_index("subcore") * SC_CORES + lax.axis_index("core")


def _scatter_rows(x, dest, n_out):
    n, w = x.shape

    per_worker = n // SC_WORKERS
    assert per_worker % SC_ROWS == 0

    @functools.partial(pl.kernel, out_type=jax.ShapeDtypeStruct((n_out, w), x.dtype), mesh=_sc_mesh(),
                       scratch_types=[pltpu.VMEM((SC_ROWS,), jnp.int32), pltpu.VMEM((SC_ROWS, w), x.dtype)])
    def scatter(x_hbm, i_hbm, o_hbm, idx_v, rows_v):
        first = _sc_worker() * per_worker

        @pl.loop(0, per_worker // SC_ROWS)
        def _(i):
            base = pl.multiple_of(first + i * SC_ROWS, SC_ROWS)
            pltpu.sync_copy(i_hbm.at[pl.ds(base, SC_ROWS)], idx_v)
            pltpu.sync_copy(x_hbm.at[pl.ds(base, SC_ROWS)], rows_v)
            pltpu.sync_copy(rows_v, o_hbm.at[idx_v])

    return scatter(x, dest)


def _gather_rows(x, src):
    n = src.shape[0]
    w = x.shape[1]

    per_worker = n // SC_WORKERS
    assert per_worker % SC_ROWS == 0

    @functools.partial(pl.kernel, out_type=jax.ShapeDtypeStruct((n, w), x.dtype), mesh=_sc_mesh(),
                       scratch_types=[pltpu.VMEM((SC_ROWS,), jnp.int32), pltpu.VMEM((SC_ROWS, w), x.dtype)])
    def gather(x_hbm, i_hbm, o_hbm, idx_v, rows_v):
        first = _sc_worker() * per_worker

        @pl.loop(0, per_worker // SC_ROWS)
        def _(i):
            base = pl.multiple_of(first + i * SC_ROWS, SC_ROWS)
            pltpu.sync_copy(i_hbm.at[pl.ds(base, SC_ROWS)], idx_v)
            pltpu.sync_copy(x_hbm.at[idx_v], rows_v)
            pltpu.sync_copy(rows_v, o_hbm.at[pl.ds(base, SC_ROWS)])

    return gather(x, src)


def _routing_plan(cls, n_tiles):
    onehot = (cls[:, None] == jnp.arange(N_CLASSES, dtype=jnp.int32)[None, :]).astype(jnp.int32)
    counts = jnp.sum(onehot, axis=0)
    rank = jnp.sum((jnp.cumsum(onehot, axis=0) - onehot) * onehot, axis=1)
    tiles = (counts + MOE_TM - 1) // MOE_TM
    tile_end = jnp.cumsum(tiles)
    tile_start = tile_end - tiles
    dest = jnp.sum(onehot * tile_start[None, :], axis=1) * MOE_TM + rank
    nact = tile_end[-1]
    blk = jnp.minimum(jnp.arange(n_tiles, dtype=jnp.int32), nact - 1)
    tile_cls = jnp.sum((blk[:, None] >= tile_end[None, :]).astype(jnp.int32), axis=1)
    e1 = jnp.asarray(CLASS_E1)[tile_cls]
    e2 = jnp.asarray(CLASS_E2)[tile_cls]
    return dest.astype(jnp.int32), blk, e1, e2, nact.reshape(1).astype(jnp.int32)


def _block_diag_mean(width):
    idx = np.arange(width) // HEAD_DIM
    return jnp.asarray((idx[:, None] == idx[None, :]).astype(np.float32) / HEAD_DIM, jnp.bfloat16)


def _row_tile(n):
    return 512 if n % 512 == 0 else 256


def kernel(x, c, ctx, c_ctx, w_ada, b_ada, norm1, w_in, pool_w, pool_scale, q_norm, k_norm, rpb,
           sg_w, sg_b, sg_norm, w_out, norm2, w_router, b_router, w_gate, w_up, w_down):
    depth = w_ada.shape[0]
    n = x.shape[1]
    lc = ctx.shape[1]
    bf = jnp.bfloat16
    lat_stream = (x[0],)
    h_ctx = ctx[0]

    cond = jnp.zeros((SUBLANES, D_MODEL), jnp.float32).at[0].set(c[0]).at[1].set(c_ctx)
    mod = _ada_call(cond, w_ada, b_ada)

    b512 = _block_diag_mean(NA_DIM)
    b256 = _block_diag_mean(SG_DIM)
    wr_t = w_router.T
    wr_hi = wr_t.astype(bf)
    wr_lo = (wr_t - wr_hi.astype(jnp.float32)).astype(bf)
    wr = jnp.concatenate([wr_hi, wr_lo], axis=0)
    br = b_router.reshape(N_EXPERTS, 1)
    n_tiles = n // MOE_TM + N_CLASSES
    w_in_bf = w_in.astype(bf)
    w_out_bf = w_out.astype(bf)

    for l in range(depth):
        last = l == depth - 1
        qg =(q_norm[l] * (HEAD_DIM ** -0.5 * LOG2_E)).reshape(1, NA_DIM)
        kg = k_norm[l].reshape(1, NA_DIM)
        sgn = sg_norm[l].reshape(1, SG_DIM)
        sgw = sg_w[l].astype(bf).reshape(SG_DIM // LANES, 2 * SG_CHUNK, SG_CHUNK)
        sgb = jnp.broadcast_to(sg_b[l].reshape(SG_DIM // LANES, 2 * SG_CHUNK, 1),
                               (SG_DIM // LANES, 2 * SG_CHUNK, LANES))
        pool_bd = jax.scipy.linalg.block_diag(*[pool_w[l, g] for g in range(len(POOL_WINDOWS))]).astype(bf)
        pscale = pool_scale[l].reshape(1, POOL_DIM)
        bias = _natten_bias(rpb[l])
        n1 = norm1[l].reshape(1, D_MODEL)
        n2 = norm2[l].reshape(1, D_MODEL)

        def mods(row):
            return [mod[l, row:row + 1, i * D_MODEL:(i + 1) * D_MODEL] for i in range(6)]

        sh1, sc1, g1, sh2, sc2, g2 = mods(0)
        csh1, csc1, cg1, csh2, csc2, cg2 = mods(1)

        tc = _row_tile(lc)
        pool_c, q_c, k_c, v_c, sg_c = _inproj_call((h_ctx,), n1, csh1, csc1, w_in_bf, l, qg, kg, b512, b256,
                                                   sgn, sgw, sgb, tc)

        tm = _row_tile(n)
        outs = _inproj_call(lat_stream, n1, sh1, sc1, w_in_bf, l, qg, kg, b512, b256, sgn, sgw, sgb, tm)
        pool_a, q, k, v, sg = outs[:5]
        h_lat = outs[5] if len(lat_stream) == 3 else lat_stream[0]
        mix_pool = _pool_call(pool_a, pool_bd, pscale, tm)
        att, wg_l, wu_l, wd_l = _natten_call(q, k, v, k_c, v_c, bias, w_gate, w_up, w_down, l)
        h1, pay, route = _outproj_call(h_lat, mix_pool, att, sg, w_out_bf, l, g1, n2, sh2, sc2, wr, br, tm)
        cls = route[4].astype(jnp.int32)
        dest, blk, e1, e2, nact = _routing_plan(cls, n_tiles)
        pay_sorted = _scatter_rows(pay, dest, n_tiles * MOE_TM)
        y_sorted = _moe_sorted_call(blk, e1, e2, nact, pay_sorted, wg_l, wu_l, wd_l)
        y = _gather_rows(y_sorted, dest)
        lat_stream = (h1, y, g2)

        if not last:
            mix_pool_c = _pool_call(pool_c, pool_bd, pscale, tc)
            att_c = _ctxatt_call(q_c, k_c, v_c)
            h1_c, pay_c, _ = _outproj_call(h_ctx, mix_pool_c, att_c, sg_c, w_out_bf, l, cg1, n2, csh2, csc2,
                                           wr, br, tc)
            y_c = _moe_dense_call(pay_c, wg_l, wu_l, wd_l)
            h_ctx = _residual_call(h1_c, y_c, cg2, tc)

    return _residual_call(*lat_stream, _row_tile(n))[None]
```

```python
import functools

import jax
import jax.numpy as jnp
import numpy as np
from jax import lax
from jax.experimental import pallas as pl
from jax.experimental.pallas import tpu as pltpu
from jax.experimental.pallas import tpu_sc as plsc

D_MODEL = 1024
GRID_W = 64
HEAD_DIM = 64
POOL_WINDOWS = (2, 4, 8, 16)
POOL_DIM = 256
NA_HEADS = 8
NA_DIM = 512
NA_WIN_ROWS = 8
NA_WIN_COLS = 16
SG_DIM = 256
SG_CHUNK = 128
Q_OFF = POOL_DIM
K_OFF = Q_OFF + NA_DIM
V_OFF = K_OFF + NA_DIM
U_OFF = V_OFF + NA_DIM
G_OFF = U_OFF + SG_DIM
IN_DIM = G_OFF + SG_DIM
N_EXPERTS = 16
GROUP_SIZE = 4
D_EXPERT = 512
EPS = 1e-6

LANES = 128
SUBLANES = 8
HEAD_PAIRS = NA_DIM // LANES
VMEM_LIMIT = 48 * 1024 * 1024

PAIRS = ((0, 1), (0, 2), (0, 3), (1, 2), (1, 3), (2, 3))
N_CLASSES = (N_EXPERTS // GROUP_SIZE) * len(PAIRS)
CLASS_E1 = np.array([4 * g + i for g in range(4) for (i, j) in PAIRS], np.int32)
CLASS_E2 = np.array([4 * g + j for g in range(4) for (i, j) in PAIRS], np.int32)

ROUTE_ROWS = 8
HALF_D = D_MODEL // 2
PAYLOAD_W = HALF_D + LANES
MOE_TM = 256
NEG_BIG = -1e30
LOG2_E = 1.4426950408889634


def _cparams(sem):
    return pltpu.CompilerParams(dimension_semantics=sem, vmem_limit_bytes=VMEM_LIMIT)


def _dot(a, b):
    return jnp.dot(a, b, preferred_element_type=jnp.float32)


def _dot_nt(a, b):
    return lax.dot_general(a, b, (((1,), (1,)), ((), ())), preferred_element_type=jnp.float32)


def _gelu_tanh(x):
    return 0.5 * x * (1.0 + jnp.tanh(0.7978845608028654 * (x + 0.044715 * (x * x * x))))


def _silu(x):
    return x * (1.0 / (1.0 + jnp.exp(-x)))


def _pack_halves(x):
    w = x.shape[1] // 2
    lo = pltpu.bitcast(x[:, :w].astype(jnp.bfloat16).astype(jnp.float32), jnp.uint32) >> 16
    hi = pltpu.bitcast(x[:, w:].astype(jnp.bfloat16).astype(jnp.float32), jnp.uint32) & jnp.uint32(0xFFFF0000)
    return lo | hi


def _unpack_halves(words):
    lo = pltpu.bitcast(words << 16, jnp.float32)
    hi = pltpu.bitcast(words & jnp.uint32(0xFFFF0000), jnp.float32)
    return lo, hi


def _ada_kernel(cond_ref, w_ref, b_ref, o_ref):
    cond = _silu(cond_ref[...])
    o_ref[...] = jnp.dot(cond, w_ref[...], preferred_element_type=jnp.float32,
                         precision=lax.Precision.HIGHEST) + b_ref[...]


def _ada_call(cond, w_ada, b_ada):
    depth = w_ada.shape[0]
    tn = 1536
    return pl.pallas_call(
        _ada_kernel,
        grid=(depth, 6 * D_MODEL // tn),
        in_specs=[
            pl.BlockSpec((SUBLANES, D_MODEL), lambda l, j: (0, 0)),
            pl.BlockSpec((None, D_MODEL, tn), lambda l, j: (l, 0, j)),
            pl.BlockSpec((None, 1, tn), lambda l, j: (l, 0, j)),
        ],
        out_specs=pl.BlockSpec((None, SUBLANES, tn), lambda l, j: (l, 0, j)),
        out_shape=jax.ShapeDtypeStruct((depth, SUBLANES, 6 * D_MODEL), jnp.float32),
        compiler_params=_cparams(("arbitrary", "arbitrary")),
        name="adaln",
    )(cond, w_ada, b_ada.reshape(depth, 1, 6 * D_MODEL))


def _norm_modulate(x, n_ref, sh_ref, sc_ref):
    ms = jnp.mean(x * x, axis=-1, keepdims=True)
    return ((x * lax.rsqrt(ms + EPS)) * (n_ref[...] * (1.0 + sc_ref[...])) + sh_ref[...]).astype(jnp.bfloat16)


POOL_HALO = 8


def _pool_mix(xe_ref, w_ref, scale_ref, tm, seq_len):
    i = pl.program_id(0)
    t = i * tm + lax.broadcasted_iota(jnp.int32, (tm, LANES), 0)
    low = lax.broadcasted_iota(jnp.int32, (tm, LANES), 1) < HEAD_DIM

    def count(half):
        return (jnp.minimum(t + half, seq_len) - jnp.maximum(t - half, 0)).astype(jnp.float32)

    def window_sums(xs, n_levels):
        sums = []
        s = xs
        for k in range(n_levels):
            step = 1 << k
            s = s[:-step] + s[step:]
            sums.append(s)
        return sums

    outs = []
    for half_block, windows in enumerate(((2, 4), (8, 16))):
        xs = xe_ref[:, half_block * LANES:(half_block + 1) * LANES]
        sums = window_sums(xs, int(np.log2(windows[1])))
        parts = []
        for w in windows:
            half = w // 2
            s = sums[int(np.log2(w)) - 1][POOL_HALO - half:POOL_HALO - half + tm]
            parts.append(s / count(half))
        mean = jnp.where(low, parts[0], parts[1])
        outs.append(mean - xs[POOL_HALO:POOL_HALO + tm])
    d = jnp.concatenate(outs, axis=-1).astype(jnp.bfloat16)
    return (_dot(d, w_ref[...]) * scale_ref[...]).astype(jnp.bfloat16)


def _inproj_kernel(*refs, pending, seq_len):
    n_stream = 7 if pending else 3
    stream, refs = refs[:n_stream], refs[n_stream:]
    if pending:
        h_ref, y_ref, g_ref, hp_ref, hn_ref, yp_ref, yn_ref = stream
        hres_ref, refs = refs[-2], refs[:-2] + refs[-1:]
        x = h_ref[...] + g_ref[...] * _moe_out(y_ref)
        hres_ref[...] = x
        x_halo = jnp.concatenate([hp_ref[...] + g_ref[...] * _moe_out(yp_ref),
                                  hn_ref[...] + g_ref[...] * _moe_out(yn_ref)], axis=0)
    else:
        h_ref, hp_ref, hn_ref = stream
        x = h_ref[...]
        x_halo = jnp.concatenate([hp_ref[...], hn_ref[...]], axis=0)
    (n1_ref, sh_ref, sc_ref, w_ref, qg_ref, kg_ref, b512_ref, b256_ref, sgn_ref, sgw_ref, sgb_ref,
     pw_ref, ps_ref, pool_ref, q_ref, k_ref, v_ref, sg_ref, xe_ref) = refs
    tm = h_ref.shape[0]
    i = pl.program_id(0)
    hn = _norm_modulate(x, n1_ref, sh_ref, sc_ref)
    hn_halo = _norm_modulate(x_halo, n1_ref, sh_ref, sc_ref)

    a_halo = _dot(hn_halo, w_ref[:, 0:Q_OFF])
    xe_ref[0:POOL_HALO, :] = jnp.where(i > 0, a_halo[:POOL_HALO], 0.0)
    xe_ref[POOL_HALO:POOL_HALO + tm, :] = _dot(hn, w_ref[:, 0:Q_OFF])
    xe_ref[POOL_HALO + tm:, :] = jnp.where(i < pl.num_programs(0) - 1, a_halo[POOL_HALO:], 0.0)

    a_q = _dot(hn, w_ref[:, Q_OFF:K_OFF])
    a_k = _dot(hn, w_ref[:, K_OFF:V_OFF])
    v_ref[...] = _dot(hn, w_ref[:, V_OFF:U_OFF]).astype(jnp.bfloat16)
    a_u = _dot(hn, w_ref[:, U_OFF:G_OFF])
    a_g = _dot(hn, w_ref[:, G_OFF:IN_DIM])

    msq = _dot((a_q * a_q).astype(jnp.bfloat16), b512_ref[...])
    msk = _dot((a_k * a_k).astype(jnp.bfloat16), b512_ref[...])
    gv = _gelu_tanh(a_g)
    msv = _dot((gv * gv).astype(jnp.bfloat16), b256_ref[...])
    q_ref[...] = (a_q * lax.rsqrt(msq + EPS) * qg_ref[...]).astype(jnp.bfloat16)
    k_ref[...] = (a_k * lax.rsqrt(msk + EPS) * kg_ref[...]).astype(jnp.bfloat16)

    u = _gelu_tanh(a_u)
    vn = (gv * lax.rsqrt(msv + EPS) * sgn_ref[...]).astype(jnp.bfloat16)
    low = lax.broadcasted_iota(jnp.int32, (SG_CHUNK, LANES), 1) < HEAD_DIM
    for c in range(tm // SG_CHUNK):
        rows = slice(c * SG_CHUNK, (c + 1) * SG_CHUNK)
        for s in range(SG_DIM // LANES):
            cols = slice(s * LANES, (s + 1) * LANES)
            m = _dot(sgw_ref[s], vn[rows, cols]) + sgb_ref[s]
            mixed = jnp.where(low, m[:SG_CHUNK], m[SG_CHUNK:])
            sg_ref[rows, cols] = (u[rows, cols] * mixed).astype(jnp.bfloat16)

    pool_ref[...] = _pool_mix(xe_ref, pw_ref, ps_ref, tm, seq_len)


def _inproj_call(stream, n1, sh, sc, w_in, layer, qg, kg, b512, b256, sgn, sgw, sgb, pool_w, pool_scale, tm):
    pending = len(stream) == 3
    n = stream[0].shape[0]
    row = lambda i: (i, 0)
    fixed2 = lambda i: (0, 0)
    fixed3 = lambda i: (0, 0, 0)
    vec = lambda w: pl.BlockSpec((1, w), fixed2)
    rows = pl.BlockSpec((tm, D_MODEL), row)
    r8 = tm // POOL_HALO
    before = lambda i: (jnp.maximum(i * r8 - 1, 0), 0)
    after = lambda i: (jnp.minimum((i + 1) * r8, n // POOL_HALO - 1), 0)
    halo = lambda w, m: pl.BlockSpec((POOL_HALO, w), m)
    h = stream[0]
    if pending:
        y, g = stream[1], stream[2]
        yw = y.shape[1]
        args = [h, y, g, h, h, y, y]
        stream_specs = [rows, pl.BlockSpec((tm, yw), row), vec(D_MODEL),
                        halo(D_MODEL, before), halo(D_MODEL, after), halo(yw, before), halo(yw, after)]
    else:
        args = [h, h, h]
        stream_specs = [rows, halo(D_MODEL, before), halo(D_MODEL, after)]
    extra_out_specs = [rows] if pending else []
    extra_out_shape = [jax.ShapeDtypeStruct((n, D_MODEL), jnp.float32)] if pending else []
    return pl.pallas_call(
        functools.partial(_inproj_kernel, pending=pending, seq_len=n),
        grid=(n // tm,),
        in_specs=stream_specs + [
            vec(D_MODEL), vec(D_MODEL), vec(D_MODEL),
            pl.BlockSpec((None, D_MODEL, IN_DIM), lambda i: (layer, 0, 0)),
            vec(NA_DIM), vec(NA_DIM),
            pl.BlockSpec((NA_DIM, NA_DIM), fixed2),
            pl.BlockSpec((SG_DIM, SG_DIM), fixed2),
            vec(SG_DIM),
            pl.BlockSpec((SG_DIM // LANES, 2 * SG_CHUNK, SG_CHUNK), fixed3),
            pl.BlockSpec((SG_DIM // LANES, 2 * SG_CHUNK, LANES), fixed3),
            pl.BlockSpec((POOL_DIM, POOL_DIM), fixed2),
            vec(POOL_DIM),
        ],
        out_specs=[
            pl.BlockSpec((tm, POOL_DIM), row),
            pl.BlockSpec((tm, NA_DIM), row),
            pl.BlockSpec((tm, NA_DIM), row),
            pl.BlockSpec((tm, NA_DIM), row),
            pl.BlockSpec((tm, SG_DIM), row),
        ] + extra_out_specs,
        out_shape=[
            jax.ShapeDtypeStruct((n, POOL_DIM), jnp.bfloat16),
            jax.ShapeDtypeStruct((n, NA_DIM), jnp.bfloat16),
            jax.ShapeDtypeStruct((n, NA_DIM), jnp.bfloat16),
            jax.ShapeDtypeStruct((n, NA_DIM), jnp.bfloat16),
            jax.ShapeDtypeStruct((n, SG_DIM), jnp.bfloat16),
        ] + extra_out_shape,
        scratch_shapes=[pltpu.VMEM((tm + 2 * POOL_HALO, POOL_DIM), jnp.float32)],
        compiler_params=_cparams(("arbitrary",)),
        name="inproj",
    )(*args, n1, sh, sc, w_in, qg, kg, b512, b256, sgn, sgw, sgb, pool_w, pool_scale)


NA_ROWS_PER_BLOCK = 32
NA_GROUP_ROWS = 4
NA_WINDOW_ROWS = NA_GROUP_ROWS + NA_WIN_ROWS
NA_BLOCK = NA_ROWS_PER_BLOCK * GRID_W
NA_GROUP = NA_GROUP_ROWS * GRID_W
NA_HALO = (NA_WIN_ROWS // 2) * GRID_W
NA_WINDOW = NA_WINDOW_ROWS * GRID_W
NA_EDGE_FIRST, NA_EDGE_NONE, NA_EDGE_LAST = 0, 1, 2


def _stack_heads(x, low):
    zero = jnp.zeros_like(x)
    return jnp.concatenate([jnp.where(low, x, zero), jnp.where(low, zero, x)], axis=0)


def _natten_kernel(q_ref, kp_ref, kc_ref, kn_ref, vp_ref, vc_ref, vn_ref, kx_ref, vx_ref, bias_ref,
                   wg_ref, wu_ref, wd_ref,
                   o_ref, wg_bf_ref, wu_bf_ref, wd_bf_ref, kwin_ref, vwin_ref, vxe_ref, *, grid_rows):
    b = pl.program_id(1)
    wg_bf_ref[...] = wg_ref[...].astype(jnp.bfloat16)
    wu_bf_ref[...] = wu_ref[...].astype(jnp.bfloat16)
    wd_bf_ref[...] = wd_ref[...].astype(jnp.bfloat16)
    kwin_ref[0:NA_HALO, :] = kp_ref[...]
    kwin_ref[NA_HALO:NA_HALO + NA_BLOCK, :] = kc_ref[...]
    kwin_ref[NA_HALO + NA_BLOCK:, :] = kn_ref[...]
    vwin_ref[0:NA_HALO, 0:LANES] = vp_ref[...]
    vwin_ref[NA_HALO:NA_HALO + NA_BLOCK, 0:LANES] = vc_ref[...]
    vwin_ref[NA_HALO + NA_BLOCK:, 0:LANES] = vn_ref[...]
    vwin_ref[:, LANES:] = jnp.ones((vwin_ref.shape[0], LANES), jnp.bfloat16)
    vxe_ref[:, 0:LANES] = vx_ref[...]
    vxe_ref[:, LANES:] = jnp.ones((vxe_ref.shape[0], LANES), jnp.bfloat16)
    low_q = lax.broadcasted_iota(jnp.int32, (NA_GROUP, LANES), 1) < HEAD_DIM

    def window_start(g):
        r0 = b * NA_ROWS_PER_BLOCK + g * NA_GROUP_ROWS
        ws = jnp.clip(r0 - NA_WIN_ROWS // 2, 0, grid_rows - NA_WINDOW_ROWS)
        edge = jnp.where(r0 == 0, NA_EDGE_FIRST,
                         jnp.where(r0 == grid_rows - NA_GROUP_ROWS, NA_EDGE_LAST, NA_EDGE_NONE))
        start = pl.multiple_of((ws - b * NA_ROWS_PER_BLOCK + NA_WIN_ROWS // 2) * GRID_W, GRID_W)
        return start, edge

    def scores(g):
        start, edge = window_start(g)
        lhs = _stack_heads(q_ref[g * NA_GROUP:(g + 1) * NA_GROUP, :], low_q)
        kl = kwin_ref[pl.ds(start, NA_WINDOW), :]
        return jnp.concatenate([_dot_nt(lhs, kl).astype(jnp.bfloat16) + bias_ref[edge],
                                _dot_nt(lhs, kx_ref[...]).astype(jnp.bfloat16)], axis=-1)

    n_groups = NA_ROWS_PER_BLOCK // NA_GROUP_ROWS
    s_next = scores(0)
    for g in range(n_groups):
        s = s_next
        if g + 1 < n_groups:
            s_next = scores(g + 1)
        start, _ = window_start(g)
        vl = vwin_ref[pl.ds(start, NA_WINDOW), :]
        m = jnp.max(s, axis=-1, keepdims=True)
        pb = jnp.exp2(s - m)
        o = _dot(pb[:, :NA_WINDOW], vl) + _dot(pb[:, NA_WINDOW:], vxe_ref[...])
        o = o[:, :LANES] * (1.0 / o[:, LANES:])
        qrows = slice(g * NA_GROUP, (g + 1) * NA_GROUP)
        o_ref[qrows, :] = jnp.where(low_q, o[:NA_GROUP], o[NA_GROUP:]).astype(jnp.bfloat16)


def _natten_call(q, k, v, k_ctx, v_ctx, bias, w_gate, w_up, w_down, layer):
    n = q.shape[0]
    grid_rows = n // GRID_W
    assert grid_rows % NA_ROWS_PER_BLOCK == 0 and grid_rows >= 2 * NA_ROWS_PER_BLOCK
    nblk = n // NA_BLOCK
    steps = HEAD_PAIRS * nblk
    depth = w_gate.shape[0]
    up_rows = N_EXPERTS * D_MODEL
    down_rows = N_EXPERTS * D_EXPERT
    assert up_rows % steps == 0 and down_rows % steps == 0
    wg2 = w_gate.reshape(depth * up_rows, D_EXPERT)
    wu2 = w_up.reshape(depth * up_rows, D_EXPERT)
    wd2 = w_down.reshape(depth * down_rows, D_MODEL)
    up_in = pl.BlockSpec((up_rows // steps, D_EXPERT), lambda p, b: (layer * steps + p * nblk + b, 0))
    down_in = pl.BlockSpec((down_rows // steps, D_MODEL), lambda p, b: (layer * steps + p * nblk + b, 0))
    up_out = pl.BlockSpec((up_rows // steps, D_EXPERT), lambda p, b: (p * nblk + b, 0))
    down_out = pl.BlockSpec((down_rows // steps, D_MODEL), lambda p, b: (p * nblk + b, 0))
    nhalo = n // NA_HALO
    hb = NA_BLOCK // NA_HALO
    cur = pl.BlockSpec((NA_BLOCK, LANES), lambda p, b: (b, p))
    prev = pl.BlockSpec((NA_HALO, LANES), lambda p, b: (jnp.maximum(b * hb - 1, 0), p))
    nxt = pl.BlockSpec((NA_HALO, LANES), lambda p, b: (jnp.minimum((b + 1) * hb, nhalo - 1), p))
    ctx = pl.BlockSpec((k_ctx.shape[0], LANES), lambda p, b: (0, p))
    att, wg_bf, wu_bf, wd_bf = pl.pallas_call(
        functools.partial(_natten_kernel, grid_rows=grid_rows),
        grid=(HEAD_PAIRS, nblk),
        in_specs=[cur, prev, cur, nxt, prev, cur, nxt, ctx, ctx,
                  pl.BlockSpec((None, 3, 2 * NA_GROUP, NA_WINDOW), lambda p, b: (p, 0, 0, 0)),
                  up_in, up_in, down_in],
        out_specs=[cur, up_out, up_out, down_out],
        out_shape=[jax.ShapeDtypeStruct((n, NA_DIM), jnp.bfloat16),
                   jax.ShapeDtypeStruct((up_rows, D_EXPERT), jnp.bfloat16),
                   jax.ShapeDtypeStruct((up_rows, D_EXPERT), jnp.bfloat16),
                   jax.ShapeDtypeStruct((down_rows, D_MODEL), jnp.bfloat16)],
        scratch_shapes=[pltpu.VMEM((NA_BLOCK + 2 * NA_HALO, LANES), jnp.bfloat16),
                        pltpu.VMEM((NA_BLOCK + 2 * NA_HALO, 2 * LANES), jnp.bfloat16),
                        pltpu.VMEM((k_ctx.shape[0], 2 * LANES), jnp.bfloat16)],
        compiler_params=_cparams(("arbitrary", "arbitrary")),
        name="natten",
    )(q, k, k, k, v, v, v, k_ctx, v_ctx, bias, wg2, wu2, wd2)
    return (att, wg_bf.reshape(N_EXPERTS, D_MODEL, D_EXPERT), wu_bf.reshape(N_EXPERTS, D_MODEL, D_EXPERT),
            wd_bf.reshape(N_EXPERTS, D_EXPERT, D_MODEL))


def _natten_bias(rpb):
    cols = np.arange(GRID_W)
    col_start = np.clip(cols - NA_WIN_COLS // 2, 0, GRID_W - NA_WIN_COLS)
    kc = np.arange(GRID_W)
    in_win = (kc[None, :] >= col_start[:, None]) & (kc[None, :] < col_start[:, None] + NA_WIN_COLS)
    dc = kc[None, :] - cols[:, None] + NA_WIN_COLS - 1
    sel = (np.arange(2 * NA_WIN_COLS - 1)[:, None, None] == dc[None]) & in_win[None]
    t2 = jnp.einsum("hdj,jqk->hdqk", rpb, jnp.asarray(sel, jnp.float32), precision=lax.Precision.HIGHEST)
    t2 = jnp.where(in_win[None, None], t2 * LOG2_E, NEG_BIG)
    neg = jnp.full((NA_HEADS, 1, GRID_W, GRID_W), NEG_BIG, jnp.float32)
    t2e = jnp.concatenate([neg, t2, neg], axis=1)
    u = jnp.concatenate([t2e[:, :-1], t2e[:, 1:]], axis=-1)
    u = u.reshape(HEAD_PAIRS, 2, 2 * NA_WIN_ROWS, GRID_W, LANES)
    return pl.pallas_call(
        _bias_expand_kernel,
        grid=(HEAD_PAIRS,),
        in_specs=[pl.BlockSpec((None, 2, 2 * NA_WIN_ROWS, GRID_W, LANES), lambda p: (p, 0, 0, 0, 0))],
        out_specs=pl.BlockSpec((None, 3, 2 * NA_GROUP, NA_WINDOW), lambda p: (p, 0, 0, 0)),
        out_shape=jax.ShapeDtypeStruct((HEAD_PAIRS, 3, 2 * NA_GROUP, NA_WINDOW), jnp.bfloat16),
        compiler_params=_cparams(("arbitrary",)),
        name="bias_expand",
    )(u)


def _bias_expand_kernel(u_ref, o_ref):
    spare = NA_WINDOW_ROWS - NA_WIN_ROWS
    half = NA_WIN_ROWS // 2
    low = lax.broadcasted_iota(jnp.int32, (GRID_W, LANES), 1) < GRID_W
    neg = jnp.full((GRID_W, LANES), NEG_BIG, jnp.float32)
    for edge in (NA_EDGE_FIRST, NA_EDGE_NONE, NA_EDGE_LAST):
        for j in range(NA_GROUP_ROWS):
            if edge == NA_EDGE_FIRST:
                lo, base = 0, NA_WIN_ROWS - 1 - j
            elif edge == NA_EDGE_NONE:
                lo, base = j, half - 1
            else:
                lo, base = spare, half - 1 - j
            for hd in range(2):
                rows = slice(hd * NA_GROUP + j * GRID_W, hd * NA_GROUP + (j + 1) * GRID_W)
                for i in range(NA_WINDOW_ROWS // 2):
                    a0, a1 = 2 * i, 2 * i + 1
                    ok0 = lo <= a0 < lo + NA_WIN_ROWS
                    ok1 = lo <= a1 < lo + NA_WIN_ROWS
                    if not (ok0 or ok1):
                        tile = neg
                    else:
                        tile = u_ref[hd, base + a1 - lo]
                        if not ok0:
                            tile = jnp.where(low, neg, tile)
                        if not ok1:
                            tile = jnp.where(low, tile, neg)
                    o_ref[edge, rows, i * LANES:(i + 1) * LANES] = tile.astype(o_ref.dtype)


def _ctxatt_kernel(q_ref, k_ref, v_ref, o_ref):
    lc = q_ref.shape[0]
    low = lax.broadcasted_iota(jnp.int32, (lc, LANES), 1) < HEAD_DIM
    lhs = _stack_heads(q_ref[...], low)
    s = _dot_nt(lhs, k_ref[...])
    m = jnp.max(s, axis=-1, keepdims=True)
    p = jnp.exp2(s - m)
    denom = jnp.sum(p, axis=-1, keepdims=True)
    o = _dot(p.astype(jnp.bfloat16), v_ref[...]) * (1.0 / denom)
    o_ref[...] = jnp.where(low, o[:lc], o[lc:]).astype(jnp.bfloat16)


def _ctxatt_call(q, k, v):
    lc = q.shape[0]
    spec = pl.BlockSpec((lc, LANES), lambda p: (0, p))
    return pl.pallas_call(
        _ctxatt_kernel,
        grid=(HEAD_PAIRS,),
        in_specs=[spec, spec, spec],
        out_specs=spec,
        out_shape=jax.ShapeDtypeStruct((lc, NA_DIM), jnp.bfloat16),
        compiler_params=_cparams(("arbitrary",)),
        name="ctxatt",
    )(q, k, v)


OUTPROJ_CHAIN = 256


def _outproj_kernel(h_ref, mp_ref, att_ref, sg_ref, wo_ref, g1_ref, n2_ref, sh_ref, sc_ref,
                    wr_ref, br_ref, h1_ref, pay_ref, route_ref):
    chains = [slice(c * OUTPROJ_CHAIN, (c + 1) * OUTPROJ_CHAIN) for c in range(h_ref.shape[0] // OUTPROJ_CHAIN)]
    h1s = []
    for rows in chains:
        mix = (_dot(mp_ref[rows, :], wo_ref[0:POOL_DIM, :])
               + _dot(att_ref[rows, :], wo_ref[POOL_DIM:POOL_DIM + NA_DIM, :])
               + _dot(sg_ref[rows, :], wo_ref[POOL_DIM + NA_DIM:, :]))
        h1 = h_ref[rows, :] + g1_ref[...] * mix
        h1_ref[rows, :] = h1
        h1s.append(h1)
    for rows, h1 in zip(chains, h1s):
        _outproj_route(rows, h1, n2_ref, sh_ref, sc_ref, wr_ref, br_ref, pay_ref, route_ref)


def _outproj_route(rows, h1, n2_ref, sh_ref, sc_ref, wr_ref, br_ref, pay_ref, route_ref):
    tm = OUTPROJ_CHAIN
    ms = jnp.mean(h1 * h1, axis=-1, keepdims=True)
    hm = (h1 * lax.rsqrt(ms + EPS)) * (n2_ref[...] * (1.0 + sc_ref[...])) + sh_ref[...]
    pay_ref[rows, 0:HALF_D] = _pack_halves(hm)

    hm_hi = hm.astype(jnp.bfloat16)
    lt = _dot_nt(wr_ref[...], hm_hi)
    logits = lt[:N_EXPERTS] + lt[N_EXPERTS:] + br_ref[...]
    e = jnp.exp(logits - jnp.max(logits, axis=0, keepdims=True))

    best = ga = gb = e1 = e2 = cls = None
    for c in range(N_CLASSES):
        a, b2 = int(CLASS_E1[c]), int(CLASS_E2[c])
        ea, eb = e[a:a + 1, :], e[b2:b2 + 1, :]
        s = ea + eb
        if best is None:
            best, ga, gb = s, ea, eb
            e1 = jnp.full_like(s, float(a))
            e2 = jnp.full_like(s, float(b2))
            cls = jnp.zeros_like(s)
        else:
            better = s > best
            best = jnp.where(better, s, best)
            ga = jnp.where(better, ea, ga)
            gb = jnp.where(better, eb, gb)
            e1 = jnp.where(better, float(a), e1)
            e2 = jnp.where(better, float(b2), e2)
            cls = jnp.where(better, float(c), cls)
    inv = 1.0 / best
    row = lax.broadcasted_iota(jnp.int32, (ROUTE_ROWS, tm), 0)
    rec = jnp.where(row == 0, ga * inv,
          jnp.where(row == 1, gb * inv,
          jnp.where(row == 2, e1,
          jnp.where(row == 3, e2,
          jnp.where(row == 4, cls, 0.0)))))
    route_ref[:, rows] = rec
    wide = jnp.concatenate([rec, jnp.zeros((LANES - ROUTE_ROWS, tm), jnp.float32)], axis=0)
    pay_ref[rows, HALF_D:] = pltpu.bitcast(wide.T, jnp.uint32)


def _outproj_call(h, mp, att, sg, w_out, layer, g1, n2, sh2, sc2, wr, br, tm):
    n = h.shape[0]
    row = lambda i: (i, 0)
    fixed = lambda i: (0, 0)
    vec = pl.BlockSpec((1, D_MODEL), fixed)
    return pl.pallas_call(
        _outproj_kernel,
        grid=(n // tm,),
        in_specs=[
            pl.BlockSpec((tm, D_MODEL), row),
            pl.BlockSpec((tm, POOL_DIM), row),
            pl.BlockSpec((tm, NA_DIM), row),
            pl.BlockSpec((tm, SG_DIM), row),
            pl.BlockSpec((None, D_MODEL, D_MODEL), lambda i: (layer, 0, 0)),
            vec, vec, vec, vec,
            pl.BlockSpec((2 * N_EXPERTS, D_MODEL), fixed),
            pl.BlockSpec((N_EXPERTS, 1), fixed),
        ],
        out_specs=[
            pl.BlockSpec((tm, D_MODEL), row),
            pl.BlockSpec((tm, PAYLOAD_W), row),
            pl.BlockSpec((ROUTE_ROWS, tm), lambda i: (0, i)),
        ],
        out_shape=[
            jax.ShapeDtypeStruct((n, D_MODEL), jnp.float32),
            jax.ShapeDtypeStruct((n, PAYLOAD_W), jnp.uint32),
            jax.ShapeDtypeStruct((ROUTE_ROWS, n), jnp.float32),
        ],
        compiler_params=_cparams(("arbitrary",)),
        name="outproj",
    )(h, mp, att, sg, w_out, g1, n2, sh2, sc2, wr, br)


def _payload_parts(pay_ref):
    lo, hi = _unpack_halves(pay_ref[:, 0:HALF_D])
    x = jnp.concatenate([lo, hi], axis=-1).astype(jnp.bfloat16)
    return x, pltpu.bitcast(pay_ref[:, HALF_D:], jnp.float32)


def _expert_pair(x, ga, gb, wga, wua, wda, wgb, wub, wdb):
    ha = (_silu(_dot(x, wga)) * _dot(x, wua) * ga).astype(jnp.bfloat16)
    hb = (_silu(_dot(x, wgb)) * _dot(x, wub) * gb).astype(jnp.bfloat16)
    return _dot(ha, wda) + _dot(hb, wdb)


def _moe_sorted_kernel(blk_ref, e1_ref, e2_ref, nact_ref, pay_ref,
                       wga_ref, wua_ref, wda_ref, wgb_ref, wub_ref, wdb_ref, o_ref):
    i = pl.program_id(0)

    @pl.when(i < nact_ref[0])
    def _():
        x, route = _payload_parts(pay_ref)
        y = _expert_pair(x, route[:, 0:1], route[:, 1:2], wga_ref[...], wua_ref[...], wda_ref[...],
                         wgb_ref[...], wub_ref[...], wdb_ref[...])
        o_ref[...] = _pack_halves(y)


def _moe_sorted_call(blk, e1, e2, nact, pay_sorted, wg, wu, wd):
    n_tiles = blk.shape[0]
    rows = lambda i, blk, e1, e2, na: (blk[i], 0)
    wa = lambda i, blk, e1, e2, na: (e1[i], 0, 0)
    wb = lambda i, blk, e1, e2, na: (e2[i], 0, 0)
    up = lambda m: pl.BlockSpec((None, D_MODEL, D_EXPERT), m)
    down = lambda m: pl.BlockSpec((None, D_EXPERT, D_MODEL), m)
    return pl.pallas_call(
        _moe_sorted_kernel,
        grid_spec=pltpu.PrefetchScalarGridSpec(
            num_scalar_prefetch=4,
            grid=(n_tiles,),
            in_specs=[pl.BlockSpec((MOE_TM, PAYLOAD_W), rows),
                      up(wa), up(wa), down(wa), up(wb), up(wb), down(wb)],
            out_specs=pl.BlockSpec((MOE_TM, HALF_D), rows),
        ),
        out_shape=jax.ShapeDtypeStruct((n_tiles * MOE_TM, HALF_D), jnp.uint32),
        compiler_params=_cparams(("arbitrary",)),
        name="moe_sorted",
    )(blk, e1, e2, nact, pay_sorted, wg, wu, wd, wg, wu, wd)


def _moe_dense_kernel(pay_ref, wg_ref, wu_ref, wd_ref, o_ref):
    e = pl.program_id(0)

    @pl.when(e == 0)
    def _():
        o_ref[...] = jnp.zeros_like(o_ref)

    x, route = _payload_parts(pay_ref)
    ef = e.astype(jnp.float32)
    gate = (jnp.where(route[:, 2:3] == ef, route[:, 0:1], 0.0)
            + jnp.where(route[:, 3:4] == ef, route[:, 1:2], 0.0))
    he = (_silu(_dot(x, wg_ref[...])) * _dot(x, wu_ref[...]) * gate).astype(jnp.bfloat16)
    o_ref[...] += _dot(he, wd_ref[...])


def _moe_dense_call(pay, wg, wu, wd):
    n = pay.shape[0]
    return pl.pallas_call(
        _moe_dense_kernel,
        grid=(N_EXPERTS,),
        in_specs=[pl.BlockSpec((n, PAYLOAD_W), lambda e: (0, 0)),
                  pl.BlockSpec((None, D_MODEL, D_EXPERT), lambda e: (e, 0, 0)),
                  pl.BlockSpec((None, D_MODEL, D_EXPERT), lambda e: (e, 0, 0)),
                  pl.BlockSpec((None, D_EXPERT, D_MODEL), lambda e: (e, 0, 0))],
        out_specs=pl.BlockSpec((n, D_MODEL), lambda e: (0, 0)),
        out_shape=jax.ShapeDtypeStruct((n, D_MODEL), jnp.float32),
        compiler_params=_cparams(("arbitrary",)),
        name="moe_dense",
    )(pay, wg, wu, wd)


def _moe_out(y_ref):
    if y_ref.dtype == jnp.uint32:
        return jnp.concatenate(_unpack_halves(y_ref[...]), axis=-1)
    return y_ref[...]


def _residual_kernel(h_ref, y_ref, g_ref, o_ref):
    o_ref[...] = h_ref[...] + g_ref[...] * _moe_out(y_ref)


def _residual_call(h, y, g, tm):
    n = h.shape[0]
    row = pl.BlockSpec((tm, D_MODEL), lambda i: (i, 0))
    return pl.pallas_call(
        _residual_kernel,
        grid=(n // tm,),
        in_specs=[row, pl.BlockSpec((tm, y.shape[1]), lambda i: (i, 0)),
                  pl.BlockSpec((1, D_MODEL), lambda i: (0, 0))],
        out_specs=row,
        out_shape=jax.ShapeDtypeStruct((n, D_MODEL), jnp.float32),
        compiler_params=_cparams(("arbitrary",)),
        name="residual",
    )(h, y, g)


SC_ROWS = 32


SC_CORES = 2
SC_SUBCORES = 16
SC_WORKERS = SC_CORES * SC_SUBCORES


def _sc_mesh():
    return plsc.VectorSubcoreMesh(core_axis_name="core", subcore_axis_name="subcore")


def _sc_worker():
    return lax.axis_index("subcore") * SC_CORES + lax.axis_index("core")


def _scatter_rows(x, dest, n_out):
    n, w = x.shape

    per_worker = n // SC_WORKERS
    assert per_worker % SC_ROWS == 0

    @functools.partial(pl.kernel, out_type=jax.ShapeDtypeStruct((n_out, w), x.dtype), mesh=_sc_mesh(),
                       scratch_types=[pltpu.VMEM((SC_ROWS,), jnp.int32), pltpu.VMEM((SC_ROWS, w), x.dtype)])
    def scatter(x_hbm, i_hbm, o_hbm, idx_v, rows_v):
        first = _sc_worker() * per_worker

        @pl.loop(0, per_worker // SC_ROWS)
        def _(i):
            base = pl.multiple_of(first + i * SC_ROWS, SC_ROWS)
            pltpu.sync_copy(i_hbm.at[pl.ds(base, SC_ROWS)], idx_v)
            pltpu.sync_copy(x_hbm.at[pl.ds(base, SC_ROWS)], rows_v)
            pltpu.sync_copy(rows_v, o_hbm.at[idx_v])

    return scatter(x, dest)


def _gather_rows(x, src):
    n = src.shape[0]
    w = x.shape[1]

    per_worker = n // SC_WORKERS
    assert per_worker % SC_ROWS == 0

    @functools.partial(pl.kernel, out_type=jax.ShapeDtypeStruct((n, w), x.dtype), mesh=_sc_mesh(),
                       scratch_types=[pltpu.VMEM((SC_ROWS,), jnp.int32), pltpu.VMEM((SC_ROWS, w), x.dtype)])
    def gather(x_hbm, i_hbm, o_hbm, idx_v, rows_v):
        first = _sc_worker() * per_worker

        @pl.loop(0, per_worker // SC_ROWS)
        def _(i):
            base = pl.multiple_of(first + i * SC_ROWS, SC_ROWS)
            pltpu.sync_copy(i_hbm.at[pl.ds(base, SC_ROWS)], idx_v)
            pltpu.sync_copy(x_hbm.at[idx_v], rows_v)
            pltpu.sync_copy(rows_v, o_hbm.at[pl.ds(base, SC_ROWS)])

    return gather(x, src)


def _routing_plan(cls, n_tiles):
    onehot = (cls[:, None] == jnp.arange(N_CLASSES, dtype=jnp.int32)[None, :]).astype(jnp.int32)
    counts = jnp.sum(onehot, axis=0)
    rank = jnp.sum((jnp.cumsum(onehot, axis=0) - onehot) * onehot, axis=1)
    tiles = (counts + MOE_TM - 1) // MOE_TM
    tile_end = jnp.cumsum(tiles)
    tile_start = tile_end - tiles
    dest = jnp.sum(onehot * tile_start[None, :], axis=1) * MOE_TM + rank
    nact = tile_end[-1]
    blk = jnp.minimum(jnp.arange(n_tiles, dtype=jnp.int32), nact - 1)
    tile_cls = jnp.sum((blk[:, None] >= tile_end[None, :]).astype(jnp.int32), axis=1)
    e1 = jnp.asarray(CLASS_E1)[tile_cls]
    e2 = jnp.asarray(CLASS_E2)[tile_cls]
    return dest.astype(jnp.int32), blk, e1, e2, nact.reshape(1).astype(jnp.int32)


def _block_diag_mean(width):
    idx = np.arange(width) // HEAD_DIM
    return jnp.asarray((idx[:, None] == idx[None, :]).astype(np.float32) / HEAD_DIM, jnp.bfloat16)


def _row_tile(n):
    return 512 if n % 512 == 0 else 256


def kernel(x, c, ctx, c_ctx, w_ada, b_ada, norm1, w_in, pool_w, pool_scale, q_norm, k_norm, rpb,
           sg_w, sg_b, sg_norm, w_out, norm2, w_router, b_router, w_gate, w_up, w_down):
    depth = w_ada.shape[0]
    n = x.shape[1]
    lc = ctx.shape[1]
    bf = jnp.bfloat16
    lat_stream = (x[0],)
    h_ctx = ctx[0]

    cond = jnp.zeros((SUBLANES, D_MODEL), jnp.float32).at[0].set(c[0]).at[1].set(c_ctx)
    mod = _ada_call(cond, w_ada, b_ada)

    b512 = _block_diag_mean(NA_DIM)
    b256 = _block_diag_mean(SG_DIM)
    wr_t = w_router.T
    wr_hi = wr_t.astype(bf)
    wr_lo = (wr_t - wr_hi.astype(jnp.float32)).astype(bf)
    wr = jnp.concatenate([wr_hi, wr_lo], axis=0)
    br = b_router.reshape(N_EXPERTS, 1)
    n_tiles = n // MOE_TM + N_CLASSES
    w_in_bf = w_in.astype(bf)
    w_out_bf = w_out.astype(bf)

    for l in range(depth):
        last = l == depth - 1
        qg =(q_norm[l] * (HEAD_DIM ** -0.5 * LOG2_E)).reshape(1, NA_DIM)
        kg = k_norm[l].reshape(1, NA_DIM)
        sgn = sg_norm[l].reshape(1, SG_DIM)
        sgw = sg_w[l].astype(bf).reshape(SG_DIM // LANES, 2 * SG_CHUNK, SG_CHUNK)
        sgb = jnp.broadcast_to(sg_b[l].reshape(SG_DIM // LANES, 2 * SG_CHUNK, 1),
                               (SG_DIM // LANES, 2 * SG_CHUNK, LANES))
        pool_bd = jax.scipy.linalg.block_diag(*[pool_w[l, g] for g in range(len(POOL_WINDOWS))]).astype(bf)
        pscale = pool_scale[l].reshape(1, POOL_DIM)
        bias = _natten_bias(rpb[l])
        n1 = norm1[l].reshape(1, D_MODEL)
        n2 = norm2[l].reshape(1, D_MODEL)

        def mods(row):
            return [mod[l, row:row + 1, i * D_MODEL:(i + 1) * D_MODEL] for i in range(6)]

        sh1, sc1, g1, sh2, sc2, g2 = mods(0)
        csh1, csc1, cg1, csh2, csc2, cg2 = mods(1)

        tc = _row_tile(lc)
        mix_pool_c, q_c, k_c, v_c, sg_c = _inproj_call((h_ctx,), n1, csh1, csc1, w_in_bf, l, qg, kg, b512, b256,
                                                       sgn, sgw, sgb, pool_bd, pscale, tc)

        tm = _row_tile(n)
        outs = _inproj_call(lat_stream, n1, sh1, sc1, w_in_bf, l, qg, kg, b512, b256, sgn, sgw, sgb,
                            pool_bd, pscale, tm)
        mix_pool, q, k, v, sg = outs[:5]
        h_lat = outs[5] if len(lat_stream) == 3 else lat_stream[0]
        att, wg_l, wu_l, wd_l = _natten_call(q, k, v, k_c, v_c, bias, w_gate, w_up, w_down, l)
        h1, pay, route = _outproj_call(h_lat, mix_pool, att, sg, w_out_bf, l, g1, n2, sh2, sc2, wr, br, tm)
        cls = route[4].astype(jnp.int32)
        dest, blk, e1, e2, nact = _routing_plan(cls, n_tiles)
        pay_sorted = _scatter_rows(pay, dest, n_tiles * MOE_TM)
        y_sorted = _moe_sorted_call(blk, e1, e2, nact, pay_sorted, wg_l, wu_l, wd_l)
        y = _gather_rows(y_sorted, dest)
        lat_stream = (h1, y, g2)

        if not last:
            att_c = _ctxatt_call(q_c, k_c, v_c)
            h1_c, pay_c, _ = _outproj_call(h_ctx, mix_pool_c, att_c, sg_c, w_out_bf, l, cg1, n2, csh2, csc2,
                                           wr, br, tc)
            y_c = _moe_dense_call(pay_c, wg_l, wu_l, wd_l)
            h_ctx = _residual_call(h1_c, y_c, cg2, tc)

    return _residual_call(*lat_stream, _row_tile(n))[None]
```

```python
import functools

import jax
import jax.numpy as jnp
import numpy as np
from jax import lax
from jax.experimental import pallas as pl
from jax.experimental.pallas import tpu as pltpu
from jax.experimental.pallas import tpu_sc as plsc

D_MODEL = 1024
GRID_W = 64
HEAD_DIM = 64
POOL_WINDOWS = (2, 4, 8, 16)
POOL_DIM = 256
NA_HEADS = 8
NA_DIM = 512
NA_WIN_ROWS = 8
NA_WIN_COLS = 16
SG_DIM = 256
SG_CHUNK = 128
Q_OFF = POOL_DIM
K_OFF = Q_OFF + NA_DIM
V_OFF = K_OFF + NA_DIM
U_OFF = V_OFF + NA_DIM
G_OFF = U_OFF + SG_DIM
IN_DIM = G_OFF + SG_DIM
N_EXPERTS = 16
GROUP_SIZE = 4
D_EXPERT = 512
EPS = 1e-6

LANES = 128
SUBLANES = 8
HEAD_PAIRS = NA_DIM // LANES
VMEM_LIMIT = 48 * 1024 * 1024

PAIRS = ((0, 1), (0, 2), (0, 3), (1, 2), (1, 3), (2, 3))
N_CLASSES = (N_EXPERTS // GROUP_SIZE) * len(PAIRS)
CLASS_E1 = np.array([4 * g + i for g in range(4) for (i, j) in PAIRS], np.int32)
CLASS_E2 = np.array([4 * g + j for g in range(4) for (i, j) in PAIRS], np.int32)

ROUTE_ROWS = 8
HALF_D = D_MODEL // 2
PAYLOAD_W = HALF_D + LANES
MOE_TM = 256
NEG_BIG = -1e30
LOG2_E = 1.4426950408889634


def _cparams(sem):
    return pltpu.CompilerParams(dimension_semantics=sem, vmem_limit_bytes=VMEM_LIMIT)


def _dot(a, b):
    return jnp.dot(a, b, preferred_element_type=jnp.float32)


def _dot_nt(a, b):
    return lax.dot_general(a, b, (((1,), (1,)), ((), ())), preferred_element_type=jnp.float32)


def _gelu_tanh(x):
    return 0.5 * x * (1.0 + jnp.tanh(0.7978845608028654 * (x + 0.044715 * (x * x * x))))


def _silu(x):
    return x * (1.0 / (1.0 + jnp.exp(-x)))


def _pack_halves(x):
    w = x.shape[1] // 2
    lo = pltpu.bitcast(x[:, :w].astype(jnp.bfloat16).astype(jnp.float32), jnp.uint32) >> 16
    hi = pltpu.bitcast(x[:, w:].astype(jnp.bfloat16).astype(jnp.float32), jnp.uint32) & jnp.uint32(0xFFFF0000)
    return lo | hi


def _unpack_halves(words):
    lo = pltpu.bitcast(words << 16, jnp.float32)
    hi = pltpu.bitcast(words & jnp.uint32(0xFFFF0000), jnp.float32)
    return lo, hi


def _ada_kernel(cond_ref, w_ref, b_ref, o_ref):
    cond = _silu(cond_ref[...])
    o_ref[...] = jnp.dot(cond, w_ref[...], preferred_element_type=jnp.float32,
                         precision=lax.Precision.HIGHEST) + b_ref[...]


def _ada_call(cond, w_ada, b_ada):
    depth = w_ada.shape[0]
    tn = 1536
    return pl.pallas_call(
        _ada_kernel,
        grid=(depth, 6 * D_MODEL // tn),
        in_specs=[
            pl.BlockSpec((SUBLANES, D_MODEL), lambda l, j: (0, 0)),
            pl.BlockSpec((None, D_MODEL, tn), lambda l, j: (l, 0, j)),
            pl.BlockSpec((None, 1, tn), lambda l, j: (l, 0, j)),
        ],
        out_specs=pl.BlockSpec((None, SUBLANES, tn), lambda l, j: (l, 0, j)),
        out_shape=jax.ShapeDtypeStruct((depth, SUBLANES, 6 * D_MODEL), jnp.float32),
        compiler_params=_cparams(("arbitrary", "arbitrary")),
        name="adaln",
    )(cond, w_ada, b_ada.reshape(depth, 1, 6 * D_MODEL))


def _norm_modulate(x, n_ref, sh_ref, sc_ref):
    ms = jnp.mean(x * x, axis=-1, keepdims=True)
    return ((x * lax.rsqrt(ms + EPS)) * (n_ref[...] * (1.0 + sc_ref[...])) + sh_ref[...]).astype(jnp.bfloat16)


def _head_rms_scale(a):
    low = lax.broadcasted_iota(jnp.int32, (a.shape[0], LANES), 1) < HEAD_DIM
    blocks = []
    for p in range(a.shape[1] // LANES):
        sq = jnp.square(a[:, p * LANES:(p + 1) * LANES])
        s_lo = jnp.sum(jnp.where(low, sq, 0.0), axis=-1, keepdims=True)
        s_hi = jnp.sum(jnp.where(low, 0.0, sq), axis=-1, keepdims=True)
        blocks.append(jnp.where(low, lax.rsqrt(s_lo * (1.0 / HEAD_DIM) + EPS),
                                lax.rsqrt(s_hi * (1.0 / HEAD_DIM) + EPS)))
    return jnp.concatenate(blocks, axis=-1)


POOL_HALO = 8


POOL_EDGE = 16


def _pool_mix(xe_ref, w_ref, scale_ref, tm, seq_len):
    low = lax.broadcasted_iota(jnp.int32, (tm, LANES), 1) < HEAD_DIM
    t_edge = pl.program_id(0) * tm + lax.broadcasted_iota(jnp.int32, (POOL_EDGE, LANES), 0)

    def window_mean(s, half):
        mean = s * (1.0 / (2 * half))

        def clip_fix(t):
            count = (jnp.minimum(t + half, seq_len) - jnp.maximum(t - half, 0)).astype(jnp.float32)
            return (2.0 * half) / count

        return jnp.concatenate([mean[:POOL_EDGE] * clip_fix(t_edge),
                                mean[POOL_EDGE:tm - POOL_EDGE],
                                mean[tm - POOL_EDGE:] * clip_fix(t_edge + (tm - POOL_EDGE))], axis=0)

    def window_sums(xs, n_levels):
        sums = []
        s = xs
        for k in range(n_levels):
            step = 1 << k
            s = s[:-step] + s[step:]
            sums.append(s)
        return sums

    outs = []
    for half_block, windows in enumerate(((2, 4), (8, 16))):
        xs = xe_ref[:, half_block * LANES:(half_block + 1) * LANES]
        sums = window_sums(xs, int(np.log2(windows[1])))
        parts = []
        for w in windows:
            half = w // 2
            s = sums[int(np.log2(w)) - 1][POOL_HALO - half:POOL_HALO - half + tm]
            parts.append(window_mean(s, half))
        mean = jnp.where(low, parts[0], parts[1])
        outs.append(mean - xs[POOL_HALO:POOL_HALO + tm])
    d = jnp.concatenate(outs, axis=-1).astype(jnp.bfloat16)
    return (_dot(d, w_ref[...]) * scale_ref[...]).astype(jnp.bfloat16)


def _inproj_kernel(*refs, pending, seq_len):
    n_stream = 7 if pending else 3
    stream, refs = refs[:n_stream], refs[n_stream:]
    if pending:
        h_ref, y_ref, g_ref, hp_ref, hn_ref, yp_ref, yn_ref = stream
        hres_ref, refs = refs[-2], refs[:-2] + refs[-1:]
        x = h_ref[...] + g_ref[...] * _moe_out(y_ref)
        hres_ref[...] = x
        x_halo = jnp.concatenate([hp_ref[...] + g_ref[...] * _moe_out(yp_ref),
                                  hn_ref[...] + g_ref[...] * _moe_out(yn_ref)], axis=0)
    else:
        h_ref, hp_ref, hn_ref = stream
        x = h_ref[...]
        x_halo = jnp.concatenate([hp_ref[...], hn_ref[...]], axis=0)
    (n1_ref, sh_ref, sc_ref, w_ref, qg_ref, kg_ref, sgn_ref, sgw_ref, sgb_ref,
     pw_ref, ps_ref, pool_ref, q_ref, k_ref, v_ref, sg_ref, xe_ref) = refs
    tm = h_ref.shape[0]
    i = pl.program_id(0)
    hn = _norm_modulate(x, n1_ref, sh_ref, sc_ref)
    hn_halo = _norm_modulate(x_halo, n1_ref, sh_ref, sc_ref)

    a_halo = _dot(hn_halo, w_ref[:, 0:Q_OFF])
    xe_ref[0:POOL_HALO, :] = jnp.where(i > 0, a_halo[:POOL_HALO], 0.0)
    xe_ref[POOL_HALO:POOL_HALO + tm, :] = _dot(hn, w_ref[:, 0:Q_OFF])
    xe_ref[POOL_HALO + tm:, :] = jnp.where(i < pl.num_programs(0) - 1, a_halo[POOL_HALO:], 0.0)

    a_g = _dot(hn, w_ref[:, G_OFF:IN_DIM])
    a_u = _dot(hn, w_ref[:, U_OFF:G_OFF])
    a_q = _dot(hn, w_ref[:, Q_OFF:K_OFF])
    a_k = _dot(hn, w_ref[:, K_OFF:V_OFF])
    v_ref[...] = _dot(hn, w_ref[:, V_OFF:U_OFF]).astype(jnp.bfloat16)

    gv = _gelu_tanh(a_g)
    q_ref[...] = (a_q * _head_rms_scale(a_q) * qg_ref[...]).astype(jnp.bfloat16)
    k_ref[...] = (a_k * _head_rms_scale(a_k) * kg_ref[...]).astype(jnp.bfloat16)

    u = _gelu_tanh(a_u)
    vn = (gv * _head_rms_scale(gv) * sgn_ref[...]).astype(jnp.bfloat16)
    low = lax.broadcasted_iota(jnp.int32, (SG_CHUNK, LANES), 1) < HEAD_DIM
    for c in range(tm // SG_CHUNK):
        rows = slice(c * SG_CHUNK, (c + 1) * SG_CHUNK)
        for s in range(SG_DIM // LANES):
            cols = slice(s * LANES, (s + 1) * LANES)
            m = _dot(sgw_ref[s], vn[rows, cols]) + sgb_ref[s]
            mixed = jnp.where(low, m[:SG_CHUNK], m[SG_CHUNK:])
            sg_ref[rows, cols] = (u[rows, cols] * mixed).astype(jnp.bfloat16)

    pool_ref[...] = _pool_mix(xe_ref, pw_ref, ps_ref, tm, seq_len)


def _inproj_call(stream, n1, sh, sc, w_in, layer, qg, kg, sgn, sgw, sgb, pool_w, pool_scale, tm):
    pending = len(stream) == 3
    n = stream[0].shape[0]
    row = lambda i: (i, 0)
    fixed2 = lambda i: (0, 0)
    fixed3 = lambda i: (0, 0, 0)
    vec = lambda w: pl.BlockSpec((1, w), fixed2)
    rows = pl.BlockSpec((tm, D_MODEL), row)
    r8 = tm // POOL_HALO
    before = lambda i: (jnp.maximum(i * r8 - 1, 0), 0)
    after = lambda i: (jnp.minimum((i + 1) * r8, n // POOL_HALO - 1), 0)
    halo = lambda w, m: pl.BlockSpec((POOL_HALO, w), m)
    h = stream[0]
    if pending:
        y, g = stream[1], stream[2]
        yw = y.shape[1]
        args = [h, y, g, h, h, y, y]
        stream_specs = [rows, pl.BlockSpec((tm, yw), row), vec(D_MODEL),
                        halo(D_MODEL, before), halo(D_MODEL, after), halo(yw, before), halo(yw, after)]
    else:
        args = [h, h, h]
        stream_specs = [rows, halo(D_MODEL, before), halo(D_MODEL, after)]
    extra_out_specs = [rows] if pending else []
    extra_out_shape = [jax.ShapeDtypeStruct((n, D_MODEL), jnp.float32)] if pending else []
    return pl.pallas_call(
        functools.partial(_inproj_kernel, pending=pending, seq_len=n),
        grid=(n // tm,),
        in_specs=stream_specs + [
            vec(D_MODEL), vec(D_MODEL), vec(D_MODEL),
            pl.BlockSpec((None, D_MODEL, IN_DIM), lambda i: (layer, 0, 0)),
            vec(NA_DIM), vec(NA_DIM),
            vec(SG_DIM),
            pl.BlockSpec((SG_DIM // LANES, 2 * SG_CHUNK, SG_CHUNK), fixed3),
            pl.BlockSpec((SG_DIM // LANES, 2 * SG_CHUNK, LANES), fixed3),
            pl.BlockSpec((POOL_DIM, POOL_DIM), fixed2),
            vec(POOL_DIM),
        ],
        out_specs=[
            pl.BlockSpec((tm, POOL_DIM), row),
            pl.BlockSpec((tm, NA_DIM), row),
            pl.BlockSpec((tm, NA_DIM), row),
            pl.BlockSpec((tm, NA_DIM), row),
            pl.BlockSpec((tm, SG_DIM), row),
        ] + extra_out_specs,
        out_shape=[
            jax.ShapeDtypeStruct((n, POOL_DIM), jnp.bfloat16),
            jax.ShapeDtypeStruct((n, NA_DIM), jnp.bfloat16),
            jax.ShapeDtypeStruct((n, NA_DIM), jnp.bfloat16),
            jax.ShapeDtypeStruct((n, NA_DIM), jnp.bfloat16),
            jax.ShapeDtypeStruct((n, SG_DIM), jnp.bfloat16),
        ] + extra_out_shape,
        scratch_shapes=[pltpu.VMEM((tm + 2 * POOL_HALO, POOL_DIM), jnp.float32)],
        compiler_params=_cparams(("arbitrary",)),
        name="inproj",
    )(*args, n1, sh, sc, w_in, qg, kg, sgn, sgw, sgb, pool_w, pool_scale)


NA_ROWS_PER_BLOCK = 32
NA_GROUP_ROWS = 4
NA_WINDOW_ROWS = NA_GROUP_ROWS + NA_WIN_ROWS
NA_BLOCK = NA_ROWS_PER_BLOCK * GRID_W
NA_GROUP = NA_GROUP_ROWS * GRID_W
NA_HALO = (NA_WIN_ROWS // 2) * GRID_W
NA_WINDOW = NA_WINDOW_ROWS * GRID_W
NA_EDGE_FIRST, NA_EDGE_NONE, NA_EDGE_LAST = 0, 1, 2


def _stack_heads(x, low):
    zero = jnp.zeros_like(x)
    return jnp.concatenate([jnp.where(low, x, zero), jnp.where(low, zero, x)], axis=0)


def _natten_kernel(q_ref, kp_ref, kc_ref, kn_ref, vp_ref, vc_ref, vn_ref, kx_ref, vx_ref, bias_ref,
                   wg_ref, wu_ref, wd_ref,
                   o_ref, wg_bf_ref, wu_bf_ref, wd_bf_ref, kwin_ref, vwin_ref, vxe_ref, *, grid_rows):
    b = pl.program_id(1)
    wg_bf_ref[...] = wg_ref[...].astype(jnp.bfloat16)
    wu_bf_ref[...] = wu_ref[...].astype(jnp.bfloat16)
    wd_bf_ref[...] = wd_ref[...].astype(jnp.bfloat16)
    kwin_ref[0:NA_HALO, :] = kp_ref[...]
    kwin_ref[NA_HALO:NA_HALO + NA_BLOCK, :] = kc_ref[...]
    kwin_ref[NA_HALO + NA_BLOCK:, :] = kn_ref[...]
    vwin_ref[0:NA_HALO, 0:LANES] = vp_ref[...]
    vwin_ref[NA_HALO:NA_HALO + NA_BLOCK, 0:LANES] = vc_ref[...]
    vwin_ref[NA_HALO + NA_BLOCK:, 0:LANES] = vn_ref[...]
    vwin_ref[:, LANES:] = jnp.ones((vwin_ref.shape[0], LANES), jnp.bfloat16)
    vxe_ref[:, 0:LANES] = vx_ref[...]
    vxe_ref[:, LANES:] = jnp.ones((vxe_ref.shape[0], LANES), jnp.bfloat16)
    low_q = lax.broadcasted_iota(jnp.int32, (NA_GROUP, LANES), 1) < HEAD_DIM

    def window_start(g):
        r0 = b * NA_ROWS_PER_BLOCK + g * NA_GROUP_ROWS
        ws = jnp.clip(r0 - NA_WIN_ROWS // 2, 0, grid_rows - NA_WINDOW_ROWS)
        edge = jnp.where(r0 == 0, NA_EDGE_FIRST,
                         jnp.where(r0 == grid_rows - NA_GROUP_ROWS, NA_EDGE_LAST, NA_EDGE_NONE))
        start = pl.multiple_of((ws - b * NA_ROWS_PER_BLOCK + NA_WIN_ROWS // 2) * GRID_W, GRID_W)
        return start, edge

    def scores(g):
        start, edge = window_start(g)
        lhs = _stack_heads(q_ref[g * NA_GROUP:(g + 1) * NA_GROUP, :], low_q)
        kl = kwin_ref[pl.ds(start, NA_WINDOW), :]
        return jnp.concatenate([_dot_nt(lhs, kl).astype(jnp.bfloat16) + bias_ref[edge],
                                _dot_nt(lhs, kx_ref[...]).astype(jnp.bfloat16)], axis=-1)

    n_groups = NA_ROWS_PER_BLOCK // NA_GROUP_ROWS
    s_next = scores(0)
    for g in range(n_groups):
        s = s_next
        if g + 1 < n_groups:
            s_next = scores(g + 1)
        start, _ = window_start(g)
        vl = vwin_ref[pl.ds(start, NA_WINDOW), :]
        m = jnp.max(s, axis=-1, keepdims=True)
        pb = jnp.exp2(s - m)
        o = _dot(pb[:, :NA_WINDOW], vl) + _dot(pb[:, NA_WINDOW:], vxe_ref[...])
        o = o[:, :LANES] * (1.0 / o[:, LANES:])
        qrows = slice(g * NA_GROUP, (g + 1) * NA_GROUP)
        o_ref[qrows, :] = jnp.where(low_q, o[:NA_GROUP], o[NA_GROUP:]).astype(jnp.bfloat16)


def _natten_call(q, k, v, k_ctx, v_ctx, bias, w_gate, w_up, w_down, layer):
    n = q.shape[0]
    grid_rows = n // GRID_W
    assert grid_rows % NA_ROWS_PER_BLOCK == 0 and grid_rows >= 2 * NA_ROWS_PER_BLOCK
    nblk = n // NA_BLOCK
    steps = HEAD_PAIRS * nblk
    depth = w_gate.shape[0]
    up_rows = N_EXPERTS * D_MODEL
    down_rows = N_EXPERTS * D_EXPERT
    assert up_rows % steps == 0 and down_rows % steps == 0
    wg2 = w_gate.reshape(depth * up_rows, D_EXPERT)
    wu2 = w_up.reshape(depth * up_rows, D_EXPERT)
    wd2 = w_down.reshape(depth * down_rows, D_MODEL)
    up_in = pl.BlockSpec((up_rows // steps, D_EXPERT), lambda p, b: (layer * steps + p * nblk + b, 0))
    down_in = pl.BlockSpec((down_rows // steps, D_MODEL), lambda p, b: (layer * steps + p * nblk + b, 0))
    up_out = pl.BlockSpec((up_rows // steps, D_EXPERT), lambda p, b: (p * nblk + b, 0))
    down_out = pl.BlockSpec((down_rows // steps, D_MODEL), lambda p, b: (p * nblk + b, 0))
    nhalo = n // NA_HALO
    hb = NA_BLOCK // NA_HALO
    cur = pl.BlockSpec((NA_BLOCK, LANES), lambda p, b: (b, p))
    prev = pl.BlockSpec((NA_HALO, LANES), lambda p, b: (jnp.maximum(b * hb - 1, 0), p))
    nxt = pl.BlockSpec((NA_HALO, LANES), lambda p, b: (jnp.minimum((b + 1) * hb, nhalo - 1), p))
    ctx = pl.BlockSpec((k_ctx.shape[0], LANES), lambda p, b: (0, p))
    att, wg_bf, wu_bf, wd_bf = pl.pallas_call(
        functools.partial(_natten_kernel, grid_rows=grid_rows),
        grid=(HEAD_PAIRS, nblk),
        in_specs=[cur, prev, cur, nxt, prev, cur, nxt, ctx, ctx,
                  pl.BlockSpec((None, 3, 2 * NA_GROUP, NA_WINDOW), lambda p, b: (p, 0, 0, 0)),
                  up_in, up_in, down_in],
        out_specs=[cur, up_out, up_out, down_out],
        out_shape=[jax.ShapeDtypeStruct((n, NA_DIM), jnp.bfloat16),
                   jax.ShapeDtypeStruct((up_rows, D_EXPERT), jnp.bfloat16),
                   jax.ShapeDtypeStruct((up_rows, D_EXPERT), jnp.bfloat16),
                   jax.ShapeDtypeStruct((down_rows, D_MODEL), jnp.bfloat16)],
        scratch_shapes=[pltpu.VMEM((NA_BLOCK + 2 * NA_HALO, LANES), jnp.bfloat16),
                        pltpu.VMEM((NA_BLOCK + 2 * NA_HALO, 2 * LANES), jnp.bfloat16),
                        pltpu.VMEM((k_ctx.shape[0], 2 * LANES), jnp.bfloat16)],
        compiler_params=_cparams(("arbitrary", "arbitrary")),
        name="natten",
    )(q, k, k, k, v, v, v, k_ctx, v_ctx, bias, wg2, wu2, wd2)
    return (att, wg_bf.reshape(N_EXPERTS, D_MODEL, D_EXPERT), wu_bf.reshape(N_EXPERTS, D_MODEL, D_EXPERT),
            wd_bf.reshape(N_EXPERTS, D_EXPERT, D_MODEL))


def _natten_bias(rpb):
    cols = np.arange(GRID_W)
    col_start = np.clip(cols - NA_WIN_COLS // 2, 0, GRID_W - NA_WIN_COLS)
    kc = np.arange(GRID_W)
    in_win = (kc[None, :] >= col_start[:, None]) & (kc[None, :] < col_start[:, None] + NA_WIN_COLS)
    dc = kc[None, :] - cols[:, None] + NA_WIN_COLS - 1
    sel = (np.arange(2 * NA_WIN_COLS - 1)[:, None, None] == dc[None]) & in_win[None]
    t2 = jnp.einsum("hdj,jqk->hdqk", rpb, jnp.asarray(sel, jnp.float32), precision=lax.Precision.HIGHEST)
    t2 = jnp.where(in_win[None, None], t2 * LOG2_E, NEG_BIG)
    neg = jnp.full((NA_HEADS, 1, GRID_W, GRID_W), NEG_BIG, jnp.float32)
    t2e = jnp.concatenate([neg, t2, neg], axis=1)
    u = jnp.concatenate([t2e[:, :-1], t2e[:, 1:]], axis=-1)
    u = u.reshape(HEAD_PAIRS, 2, 2 * NA_WIN_ROWS, GRID_W, LANES)
    return pl.pallas_call(
        _bias_expand_kernel,
        grid=(HEAD_PAIRS,),
        in_specs=[pl.BlockSpec((None, 2, 2 * NA_WIN_ROWS, GRID_W, LANES), lambda p: (p, 0, 0, 0, 0))],
        out_specs=pl.BlockSpec((None, 3, 2 * NA_GROUP, NA_WINDOW), lambda p: (p, 0, 0, 0)),
        out_shape=jax.ShapeDtypeStruct((HEAD_PAIRS, 3, 2 * NA_GROUP, NA_WINDOW), jnp.bfloat16),
        compiler_params=_cparams(("arbitrary",)),
        name="bias_expand",
    )(u)


def _bias_expand_kernel(u_ref, o_ref):
    spare = NA_WINDOW_ROWS - NA_WIN_ROWS
    half = NA_WIN_ROWS // 2
    low = lax.broadcasted_iota(jnp.int32, (GRID_W, LANES), 1) < GRID_W
    neg = jnp.full((GRID_W, LANES), NEG_BIG, jnp.float32)
    for edge in (NA_EDGE_FIRST, NA_EDGE_NONE, NA_EDGE_LAST):
        for j in range(NA_GROUP_ROWS):
            if edge == NA_EDGE_FIRST:
                lo, base = 0, NA_WIN_ROWS - 1 - j
            elif edge == NA_EDGE_NONE:
                lo, base = j, half - 1
            else:
                lo, base = spare, half - 1 - j
            for hd in range(2):
                rows = slice(hd * NA_GROUP + j * GRID_W, hd * NA_GROUP + (j + 1) * GRID_W)
                for i in range(NA_WINDOW_ROWS // 2):
                    a0, a1 = 2 * i, 2 * i + 1
                    ok0 = lo <= a0 < lo + NA_WIN_ROWS
                    ok1 = lo <= a1 < lo + NA_WIN_ROWS
                    if not (ok0 or ok1):
                        tile = neg
                    else:
                        tile = u_ref[hd, base + a1 - lo]
                        if not ok0:
                            tile = jnp.where(low, neg, tile)
                        if not ok1:
                            tile = jnp.where(low, tile, neg)
                    o_ref[edge, rows, i * LANES:(i + 1) * LANES] = tile.astype(o_ref.dtype)


def _ctxatt_kernel(q_ref, k_ref, v_ref, o_ref):
    lc = q_ref.shape[0]
    low = lax.broadcasted_iota(jnp.int32, (lc, LANES), 1) < HEAD_DIM
    lhs = _stack_heads(q_ref[...], low)
    s = _dot_nt(lhs, k_ref[...])
    m = jnp.max(s, axis=-1, keepdims=True)
    p = jnp.exp2(s - m)
    denom = jnp.sum(p, axis=-1, keepdims=True)
    o = _dot(p.astype(jnp.bfloat16), v_ref[...]) * (1.0 / denom)
    o_ref[...] = jnp.where(low, o[:lc], o[lc:]).astype(jnp.bfloat16)


def _ctxatt_call(q, k, v):
    lc = q.shape[0]
    spec = pl.BlockSpec((lc, LANES), lambda p: (0, p))
    return pl.pallas_call(
        _ctxatt_kernel,
        grid=(HEAD_PAIRS,),
        in_specs=[spec, spec, spec],
        out_specs=spec,
        out_shape=jax.ShapeDtypeStruct((lc, NA_DIM), jnp.bfloat16),
        compiler_params=_cparams(("arbitrary",)),
        name="ctxatt",
    )(q, k, v)


OUTPROJ_CHAIN = 256


def _outproj_kernel(h_ref, mp_ref, att_ref, sg_ref, wo_ref, g1_ref, n2_ref, sh_ref, sc_ref,
                    wr_ref, br_ref, h1_ref, pay_ref, route_ref):
    chains = [slice(c * OUTPROJ_CHAIN, (c + 1) * OUTPROJ_CHAIN) for c in range(h_ref.shape[0] // OUTPROJ_CHAIN)]
    h1s = []
    for rows in chains:
        mix = (_dot(mp_ref[rows, :], wo_ref[0:POOL_DIM, :])
               + _dot(att_ref[rows, :], wo_ref[POOL_DIM:POOL_DIM + NA_DIM, :])
               + _dot(sg_ref[rows, :], wo_ref[POOL_DIM + NA_DIM:, :]))
        h1 = h_ref[rows, :] + g1_ref[...] * mix
        h1_ref[rows, :] = h1
        h1s.append(h1)
    for rows, h1 in zip(chains, h1s):
        _outproj_route(rows, h1, n2_ref, sh_ref, sc_ref, wr_ref, br_ref, pay_ref, route_ref)


def _outproj_route(rows, h1, n2_ref, sh_ref, sc_ref, wr_ref, br_ref, pay_ref, route_ref):
    tm = OUTPROJ_CHAIN
    ms = jnp.mean(h1 * h1, axis=-1, keepdims=True)
    hm = (h1 * lax.rsqrt(ms + EPS)) * (n2_ref[...] * (1.0 + sc_ref[...])) + sh_ref[...]
    pay_ref[rows, 0:HALF_D] = _pack_halves(hm)

    hm_hi = hm.astype(jnp.bfloat16)
    lt = _dot_nt(wr_ref[...], hm_hi)
    logits = lt[:N_EXPERTS] + lt[N_EXPERTS:] + br_ref[...]
    e = jnp.exp(logits - jnp.max(logits, axis=0, keepdims=True))

    best = ga = gb = e1 = e2 = cls = None
    for c in range(N_CLASSES):
        a, b2 = int(CLASS_E1[c]), int(CLASS_E2[c])
        ea, eb = e[a:a + 1, :], e[b2:b2 + 1, :]
        s = ea + eb
        if best is None:
            best, ga, gb = s, ea, eb
            e1 = jnp.full_like(s, float(a))
            e2 = jnp.full_like(s, float(b2))
            cls = jnp.zeros_like(s)
        else:
            better = s > best
            best = jnp.where(better, s, best)
            ga = jnp.where(better, ea, ga)
            gb = jnp.where(better, eb, gb)
            e1 = jnp.where(better, float(a), e1)
            e2 = jnp.where(better, float(b2), e2)
            cls = jnp.where(better, float(c), cls)
    inv = 1.0 / best
    row = lax.broadcasted_iota(jnp.int32, (ROUTE_ROWS, tm), 0)
    rec = jnp.where(row == 0, ga * inv,
          jnp.where(row == 1, gb * inv,
          jnp.where(row == 2, e1,
          jnp.where(row == 3, e2,
          jnp.where(row == 4, cls, 0.0)))))
    route_ref[:, rows] = rec
    wide = jnp.concatenate([rec, jnp.zeros((LANES - ROUTE_ROWS, tm), jnp.float32)], axis=0)
    pay_ref[rows, HALF_D:] = pltpu.bitcast(wide.T, jnp.uint32)


def _outproj_call(h, mp, att, sg, w_out, layer, g1, n2, sh2, sc2, wr, br, tm):
    n = h.shape[0]
    row = lambda i: (i, 0)
    fixed = lambda i: (0, 0)
    vec = pl.BlockSpec((1, D_MODEL), fixed)
    return pl.pallas_call(
        _outproj_kernel,
        grid=(n // tm,),
        in_specs=[
            pl.BlockSpec((tm, D_MODEL), row),
            pl.BlockSpec((tm, POOL_DIM), row),
            pl.BlockSpec((tm, NA_DIM), row),
            pl.BlockSpec((tm, SG_DIM), row),
            pl.BlockSpec((None, D_MODEL, D_MODEL), lambda i: (layer, 0, 0)),
            vec, vec, vec, vec,
            pl.BlockSpec((2 * N_EXPERTS, D_MODEL), fixed),
            pl.BlockSpec((N_EXPERTS, 1), fixed),
        ],
        out_specs=[
            pl.BlockSpec((tm, D_MODEL), row),
            pl.BlockSpec((tm, PAYLOAD_W), row),
            pl.BlockSpec((ROUTE_ROWS, tm), lambda i: (0, i)),
        ],
        out_shape=[
            jax.ShapeDtypeStruct((n, D_MODEL), jnp.float32),
            jax.ShapeDtypeStruct((n, PAYLOAD_W), jnp.uint32),
            jax.ShapeDtypeStruct((ROUTE_ROWS, n), jnp.float32),
        ],
        compiler_params=_cparams(("arbitrary",)),
        name="outproj",
    )(h, mp, att, sg, w_out, g1, n2, sh2, sc2, wr, br)


def _payload_parts(pay_ref):
    lo, hi = _unpack_halves(pay_ref[:, 0:HALF_D])
    x = jnp.concatenate([lo, hi], axis=-1).astype(jnp.bfloat16)
    return x, pltpu.bitcast(pay_ref[:, HALF_D:], jnp.float32)


def _expert_pair(x, ga, gb, wga, wua, wda, wgb, wub, wdb):
    ha = (_silu(_dot(x, wga)) * _dot(x, wua) * ga).astype(jnp.bfloat16)
    hb = (_silu(_dot(x, wgb)) * _dot(x, wub) * gb).astype(jnp.bfloat16)
    return _dot(ha, wda) + _dot(hb, wdb)


def _moe_sorted_kernel(blk_ref, e1_ref, e2_ref, nact_ref, pay_ref,
                       wga_ref, wua_ref, wda_ref, wgb_ref, wub_ref, wdb_ref, o_ref):
    i = pl.program_id(0)

    @pl.when(i < nact_ref[0])
    def _():
        x, route = _payload_parts(pay_ref)
        y = _expert_pair(x, route[:, 0:1], route[:, 1:2], wga_ref[...], wua_ref[...], wda_ref[...],
                         wgb_ref[...], wub_ref[...], wdb_ref[...])
        o_ref[...] = _pack_halves(y)


def _moe_sorted_call(blk, e1, e2, nact, pay_sorted, wg, wu, wd):
    n_tiles = blk.shape[0]
    rows = lambda i, blk, e1, e2, na: (blk[i], 0)
    wa = lambda i, blk, e1, e2, na: (e1[i], 0, 0)
    wb = lambda i, blk, e1, e2, na: (e2[i], 0, 0)
    up = lambda m: pl.BlockSpec((None, D_MODEL, D_EXPERT), m)
    down = lambda m: pl.BlockSpec((None, D_EXPERT, D_MODEL), m)
    return pl.pallas_call(
        _moe_sorted_kernel,
        grid_spec=pltpu.PrefetchScalarGridSpec(
            num_scalar_prefetch=4,
            grid=(n_tiles,),
            in_specs=[pl.BlockSpec((MOE_TM, PAYLOAD_W), rows),
                      up(wa), up(wa), down(wa), up(wb), up(wb), down(wb)],
            out_specs=pl.BlockSpec((MOE_TM, HALF_D), rows),
        ),
        out_shape=jax.ShapeDtypeStruct((n_tiles * MOE_TM, HALF_D), jnp.uint32),
        compiler_params=_cparams(("arbitrary",)),
        name="moe_sorted",
    )(blk, e1, e2, nact, pay_sorted, wg, wu, wd, wg, wu, wd)


def _moe_dense_kernel(pay_ref, wg_ref, wu_ref, wd_ref, o_ref):
    e = pl.program_id(0)

    @pl.when(e == 0)
    def _():
        o_ref[...] = jnp.zeros_like(o_ref)

    x, route = _payload_parts(pay_ref)
    ef = e.astype(jnp.float32)
    gate = (jnp.where(route[:, 2:3] == ef, route[:, 0:1], 0.0)
            + jnp.where(route[:, 3:4] == ef, route[:, 1:2], 0.0))
    he = (_silu(_dot(x, wg_ref[...])) * _dot(x, wu_ref[...]) * gate).astype(jnp.bfloat16)
    o_ref[...] += _dot(he, wd_ref[...])


def _moe_dense_call(pay, wg, wu, wd):
    n = pay.shape[0]
    return pl.pallas_call(
        _moe_dense_kernel,
        grid=(N_EXPERTS,),
        in_specs=[pl.BlockSpec((n, PAYLOAD_W), lambda e: (0, 0)),
                  pl.BlockSpec((None, D_MODEL, D_EXPERT), lambda e: (e, 0, 0)),
                  pl.BlockSpec((None, D_MODEL, D_EXPERT), lambda e: (e, 0, 0)),
                  pl.BlockSpec((None, D_EXPERT, D_MODEL), lambda e: (e, 0, 0))],
        out_specs=pl.BlockSpec((n, D_MODEL), lambda e: (0, 0)),
        out_shape=jax.ShapeDtypeStruct((n, D_MODEL), jnp.float32),
        compiler_params=_cparams(("arbitrary",)),
        name="moe_dense",
    )(pay, wg, wu, wd)


def _moe_out(y_ref):
    if y_ref.dtype == jnp.uint32:
        return jnp.concatenate(_unpack_halves(y_ref[...]), axis=-1)
    return y_ref[...]


def _residual_kernel(h_ref, y_ref, g_ref, o_ref):
    o_ref[...] = h_ref[...] + g_ref[...] * _moe_out(y_ref)


def _residual_call(h, y, g, tm):
    n = h.shape[0]
    row = pl.BlockSpec((tm, D_MODEL), lambda i: (i, 0))
    return pl.pallas_call(
        _residual_kernel,
        grid=(n // tm,),
        in_specs=[row, pl.BlockSpec((tm, y.shape[1]), lambda i: (i, 0)),
                  pl.BlockSpec((1, D_MODEL), lambda i: (0, 0))],
        out_specs=row,
        out_shape=jax.ShapeDtypeStruct((n, D_MODEL), jnp.float32),
        compiler_params=_cparams(("arbitrary",)),
        name="residual",
    )(h, y, g)


SC_ROWS = 32


SC_CORES = 2
SC_SUBCORES = 16
SC_WORKERS = SC_CORES * SC_SUBCORES


def _sc_mesh():
    return plsc.VectorSubcoreMesh(core_axis_name="core", subcore_axis_name="subcore")


def _sc_worker():
    return lax.axis_index("subcore") * SC_CORES + lax.axis_index("core")


def _scatter_rows(x, dest, n_out):
    n, w = x.shape

    per_worker = n // SC_WORKERS
    assert per_worker % SC_ROWS == 0

    @functools.partial(pl.kernel, out_type=jax.ShapeDtypeStruct((n_out, w), x.dtype), mesh=_sc_mesh(),
                       scratch_types=[pltpu.VMEM((SC_ROWS,), jnp.int32), pltpu.VMEM((SC_ROWS, w), x.dtype)])
    def scatter(x_hbm, i_hbm, o_hbm, idx_v, rows_v):
        first = _sc_worker() * per_worker

        @pl.loop(0, per_worker // SC_ROWS)
        def _(i):
            base = pl.multiple_of(first + i * SC_ROWS, SC_ROWS)
            pltpu.sync_copy(i_hbm.at[pl.ds(base, SC_ROWS)], idx_v)
            pltpu.sync_copy(x_hbm.at[pl.ds(base, SC_ROWS)], rows_v)
            pltpu.sync_copy(rows_v, o_hbm.at[idx_v])

    return scatter(x, dest)


def _gather_rows(x, src):
    n = src.shape[0]
    w = x.shape[1]

    per_worker = n // SC_WORKERS
    assert per_worker % SC_ROWS == 0

    @functools.partial(pl.kernel, out_type=jax.ShapeDtypeStruct((n, w), x.dtype), mesh=_sc_mesh(),
                       scratch_types=[pltpu.VMEM((SC_ROWS,), jnp.int32), pltpu.VMEM((SC_ROWS, w), x.dtype)])
    def gather(x_hbm, i_hbm, o_hbm, idx_v, rows_v):
        first = _sc_worker() * per_worker

        @pl.loop(0, per_worker // SC_ROWS)
        def _(i):
            base = pl.multiple_of(first + i * SC_ROWS, SC_ROWS)
            pltpu.sync_copy(i_hbm.at[pl.ds(base, SC_ROWS)], idx_v)
            pltpu.sync_copy(x_hbm.at[idx_v], rows_v)
            pltpu.sync_copy(rows_v, o_hbm.at[pl.ds(base, SC_ROWS)])

    return gather(x, src)


def _routing_plan(cls, n_tiles):
    onehot = (cls[:, None] == jnp.arange(N_CLASSES, dtype=jnp.int32)[None, :]).astype(jnp.int32)
    counts = jnp.sum(onehot, axis=0)
    rank = jnp.sum((jnp.cumsum(onehot, axis=0) - onehot) * onehot, axis=1)
    tiles = (counts + MOE_TM - 1) // MOE_TM
    tile_end = jnp.cumsum(tiles)
    tile_start = tile_end - tiles
    dest = jnp.sum(onehot * tile_start[None, :], axis=1) * MOE_TM + rank
    nact = tile_end[-1]
    blk = jnp.minimum(jnp.arange(n_tiles, dtype=jnp.int32), nact - 1)
    tile_cls = jnp.sum((blk[:, None] >= tile_end[None, :]).astype(jnp.int32), axis=1)
    e1 = jnp.asarray(CLASS_E1)[tile_cls]
    e2 = jnp.asarray(CLASS_E2)[tile_cls]
    return dest.astype(jnp.int32), blk, e1, e2, nact.reshape(1).astype(jnp.int32)


def _row_tile(n):
    return 512 if n % 512 == 0 else 256


def kernel(x, c, ctx, c_ctx, w_ada, b_ada, norm1, w_in, pool_w, pool_scale, q_norm, k_norm, rpb,
           sg_w, sg_b, sg_norm, w_out, norm2, w_router, b_router, w_gate, w_up, w_down):
    depth = w_ada.shape[0]
    n = x.shape[1]
    lc = ctx.shape[1]
    bf = jnp.bfloat16
    lat_stream = (x[0],)
    h_ctx = ctx[0]

    cond = jnp.zeros((SUBLANES, D_MODEL), jnp.float32).at[0].set(c[0]).at[1].set(c_ctx)
    mod = _ada_call(cond, w_ada, b_ada)

    wr_t = w_router.T
    wr_hi = wr_t.astype(bf)
    wr_lo = (wr_t - wr_hi.astype(jnp.float32)).astype(bf)
    wr = jnp.concatenate([wr_hi, wr_lo], axis=0)
    br = b_router.reshape(N_EXPERTS, 1)
    n_tiles = n // MOE_TM + N_CLASSES
    w_in_bf = w_in.astype(bf)
    w_out_bf = w_out.astype(bf)

    for l in range(depth):
        last = l == depth - 1
        qg =(q_norm[l] * (HEAD_DIM ** -0.5 * LOG2_E)).reshape(1, NA_DIM)
        kg = k_norm[l].reshape(1, NA_DIM)
        sgn = sg_norm[l].reshape(1, SG_DIM)
        sgw = sg_w[l].astype(bf).reshape(SG_DIM // LANES, 2 * SG_CHUNK, SG_CHUNK)
        sgb = jnp.broadcast_to(sg_b[l].reshape(SG_DIM // LANES, 2 * SG_CHUNK, 1),
                               (SG_DIM // LANES, 2 * SG_CHUNK, LANES))
        pool_bd = jax.scipy.linalg.block_diag(*[pool_w[l, g] for g in range(len(POOL_WINDOWS))]).astype(bf)
        pscale = pool_scale[l].reshape(1, POOL_DIM)
        bias = _natten_bias(rpb[l])
        n1 = norm1[l].reshape(1, D_MODEL)
        n2 = norm2[l].reshape(1, D_MODEL)

        def mods(row):
            return [mod[l, row:row + 1, i * D_MODEL:(i + 1) * D_MODEL] for i in range(6)]

        sh1, sc1, g1, sh2, sc2, g2 = mods(0)
        csh1, csc1, cg1, csh2, csc2, cg2 = mods(1)

        tc = _row_tile(lc)
        mix_pool_c, q_c, k_c, v_c, sg_c = _inproj_call((h_ctx,), n1, csh1, csc1, w_in_bf, l, qg, kg,
                                                       sgn, sgw, sgb, pool_bd, pscale, tc)

        tm = _row_tile(n)
        outs = _inproj_call(lat_stream, n1, sh1, sc1, w_in_bf, l, qg, kg, sgn, sgw, sgb,
                            pool_bd, pscale, tm)
        mix_pool, q, k, v, sg = outs[:5]
        h_lat = outs[5] if len(lat_stream) == 3 else lat_stream[0]
        att, wg_l, wu_l, wd_l = _natten_call(q, k, v, k_c, v_c, bias, w_gate, w_up, w_down, l)
        h1, pay, route = _outproj_call(h_lat, mix_pool, att, sg, w_out_bf, l, g1, n2, sh2, sc2, wr, br, tm)
        cls = route[4].astype(jnp.int32)
        dest, blk, e1, e2, nact = _routing_plan(cls, n_tiles)
        pay_sorted = _scatter_rows(pay, dest, n_tiles * MOE_TM)
        y_sorted = _moe_sorted_call(blk, e1, e2, nact, pay_sorted, wg_l, wu_l, wd_l)
        y = _gather_rows(y_sorted, dest)
        lat_stream = (h1, y, g2)

        if not last:
            att_c = _ctxatt_call(q_c, k_c, v_c)
            h1_c, pay_c, _ = _outproj_call(h_ctx, mix_pool_c, att_c, sg_c, w_out_bf, l, cg1, n2, csh2, csc2,
                                           wr, br, tc)
            y_c = _moe_dense_call(pay_c, wg_l, wu_l, wd_l)
            h_ctx = _residual_call(h1_c, y_c, cg2, tc)

    return _residual_call(*lat_stream, _row_tile(n))[None]
```

```python
import functools

import jax
import jax.numpy as jnp
import numpy as np
from jax import lax
from jax.experimental import pallas as pl
from jax.experimental.pallas import tpu as pltpu
from jax.experimental.pallas import tpu_sc as plsc

D_MODEL = 1024
GRID_W = 64
HEAD_DIM = 64
POOL_WINDOWS = (2, 4, 8, 16)
POOL_DIM = 256
NA_HEADS = 8
NA_DIM = 512
NA_WIN_ROWS = 8
NA_WIN_COLS = 16
SG_DIM = 256
SG_CHUNK = 128
Q_OFF = POOL_DIM
K_OFF = Q_OFF + NA_DIM
V_OFF = K_OFF + NA_DIM
U_OFF = V_OFF + NA_DIM
G_OFF = U_OFF + SG_DIM
IN_DIM = G_OFF + SG_DIM
N_EXPERTS = 16
GROUP_SIZE = 4
D_EXPERT = 512
EPS = 1e-6

LANES = 128
SUBLANES = 8
HEAD_PAIRS = NA_DIM // LANES
VMEM_LIMIT = 48 * 1024 * 1024

PAIRS = ((0, 1), (0, 2), (0, 3), (1, 2), (1, 3), (2, 3))
N_CLASSES = (N_EXPERTS // GROUP_SIZE) * len(PAIRS)
CLASS_E1 = np.array([4 * g + i for g in range(4) for (i, j) in PAIRS], np.int32)
CLASS_E2 = np.array([4 * g + j for g in range(4) for (i, j) in PAIRS], np.int32)

ROUTE_ROWS = 8
HALF_D = D_MODEL // 2
PAYLOAD_W = HALF_D + LANES
MOE_TM = 256
NEG_BIG = -1e30
LOG2_E = 1.4426950408889634


def _cparams(sem):
    return pltpu.CompilerParams(dimension_semantics=sem, vmem_limit_bytes=VMEM_LIMIT)


def _dot(a, b):
    return jnp.dot(a, b, preferred_element_type=jnp.float32)


def _dot_nt(a, b):
    return lax.dot_general(a, b, (((1,), (1,)), ((), ())), preferred_element_type=jnp.float32)


def _gelu_tanh(x):
    return 0.5 * x * (1.0 + jnp.tanh(0.7978845608028654 * (x + 0.044715 * (x * x * x))))


def _silu(x):
    return x * (1.0 / (1.0 + jnp.exp(-x)))


def _pack_halves(x):
    w = x.shape[1] // 2
    lo = pltpu.bitcast(x[:, :w].astype(jnp.bfloat16).astype(jnp.float32), jnp.uint32) >> 16
    hi = pltpu.bitcast(x[:, w:].astype(jnp.bfloat16).astype(jnp.float32), jnp.uint32) & jnp.uint32(0xFFFF0000)
    return lo | hi


def _unpack_halves(words):
    lo = pltpu.bitcast(words << 16, jnp.float32)
    hi = pltpu.bitcast(words & jnp.uint32(0xFFFF0000), jnp.float32)
    return lo, hi


def _ada_kernel(cond_ref, w_ref, b_ref, o_ref):
    cond = _silu(cond_ref[...])
    o_ref[...] = jnp.dot(cond, w_ref[...], preferred_element_type=jnp.float32,
                         precision=lax.Precision.HIGHEST) + b_ref[...]


def _ada_call(cond, w_ada, b_ada):
    depth = w_ada.shape[0]
    tn = 1536
    return pl.pallas_call(
        _ada_kernel,
        grid=(depth, 6 * D_MODEL // tn),
        in_specs=[
            pl.BlockSpec((SUBLANES, D_MODEL), lambda l, j: (0, 0)),
            pl.BlockSpec((None, D_MODEL, tn), lambda l, j: (l, 0, j)),
            pl.BlockSpec((None, 1, tn), lambda l, j: (l, 0, j)),
        ],
        out_specs=pl.BlockSpec((None, SUBLANES, tn), lambda l, j: (l, 0, j)),
        out_shape=jax.ShapeDtypeStruct((depth, SUBLANES, 6 * D_MODEL), jnp.float32),
        compiler_params=_cparams(("arbitrary", "arbitrary")),
        name="adaln",
    )(cond, w_ada, b_ada.reshape(depth, 1, 6 * D_MODEL))


def _norm_modulate(x, n_ref, sh_ref, sc_ref):
    ms = jnp.mean(x * x, axis=-1, keepdims=True)
    return ((x * lax.rsqrt(ms + EPS)) * (n_ref[...] * (1.0 + sc_ref[...])) + sh_ref[...]).astype(jnp.bfloat16)


def _head_rms_scale(a):
    low = lax.broadcasted_iota(jnp.int32, (a.shape[0], LANES), 1) < HEAD_DIM
    blocks = []
    for p in range(a.shape[1] // LANES):
        sq = jnp.square(a[:, p * LANES:(p + 1) * LANES])
        s_lo = jnp.sum(jnp.where(low, sq, 0.0), axis=-1, keepdims=True)
        s_hi = jnp.sum(jnp.where(low, 0.0, sq), axis=-1, keepdims=True)
        blocks.append(jnp.where(low, lax.rsqrt(s_lo * (1.0 / HEAD_DIM) + EPS),
                                lax.rsqrt(s_hi * (1.0 / HEAD_DIM) + EPS)))
    return jnp.concatenate(blocks, axis=-1)


STRIP_W = 8
N_STRIPS = GRID_W // STRIP_W


def _store_keys(ref, x):
    if len(ref.shape) == 2:
        ref[...] = x.astype(jnp.bfloat16)
        return
    pair = 2 * STRIP_W
    for s in range(N_STRIPS):
        for rp in range(x.shape[0] // (2 * GRID_W)):
            top = 2 * rp * GRID_W + s * STRIP_W
            rows = jnp.concatenate([x[top:top + STRIP_W], x[top + GRID_W:top + GRID_W + STRIP_W]], axis=0)
            ref[s, rp * pair:(rp + 1) * pair, :] = rows.astype(jnp.bfloat16)


POOL_HALO = 8


POOL_EDGE = 16


def _pool_mix(xe_ref, w_ref, scale_ref, tm, seq_len):
    low = lax.broadcasted_iota(jnp.int32, (tm, LANES), 1) < HEAD_DIM
    t_edge = pl.program_id(0) * tm + lax.broadcasted_iota(jnp.int32, (POOL_EDGE, LANES), 0)

    def window_mean(s, half):
        mean = s * (1.0 / (2 * half))

        def clip_fix(t):
            count = (jnp.minimum(t + half, seq_len) - jnp.maximum(t - half, 0)).astype(jnp.float32)
            return (2.0 * half) / count

        return jnp.concatenate([mean[:POOL_EDGE] * clip_fix(t_edge),
                                mean[POOL_EDGE:tm - POOL_EDGE],
                                mean[tm - POOL_EDGE:] * clip_fix(t_edge + (tm - POOL_EDGE))], axis=0)

    def window_sums(xs, n_levels):
        sums = []
        s = xs
        for k in range(n_levels):
            step = 1 << k
            s = s[:-step] + s[step:]
            sums.append(s)
        return sums

    outs = []
    for half_block, windows in enumerate(((2, 4), (8, 16))):
        xs = xe_ref[:, half_block * LANES:(half_block + 1) * LANES]
        sums = window_sums(xs, int(np.log2(windows[1])))
        parts = []
        for w in windows:
            half = w // 2
            s = sums[int(np.log2(w)) - 1][POOL_HALO - half:POOL_HALO - half + tm]
            parts.append(window_mean(s, half))
        mean = jnp.where(low, parts[0], parts[1])
        outs.append(mean - xs[POOL_HALO:POOL_HALO + tm])
    d = jnp.concatenate(outs, axis=-1).astype(jnp.bfloat16)
    return (_dot(d, w_ref[...]) * scale_ref[...]).astype(jnp.bfloat16)


def _inproj_kernel(*refs, pending, seq_len):
    n_stream = 7 if pending else 3
    stream, refs = refs[:n_stream], refs[n_stream:]
    if pending:
        h_ref, y_ref, g_ref, hp_ref, hn_ref, yp_ref, yn_ref = stream
        hres_ref, refs = refs[-2], refs[:-2] + refs[-1:]
        x = h_ref[...] + g_ref[...] * _moe_out(y_ref)
        hres_ref[...] = x
        x_halo = jnp.concatenate([hp_ref[...] + g_ref[...] * _moe_out(yp_ref),
                                  hn_ref[...] + g_ref[...] * _moe_out(yn_ref)], axis=0)
    else:
        h_ref, hp_ref, hn_ref = stream
        x = h_ref[...]
        x_halo = jnp.concatenate([hp_ref[...], hn_ref[...]], axis=0)
    (n1_ref, sh_ref, sc_ref, w_ref, qg_ref, kg_ref, sgn_ref, sgw_ref, sgb_ref,
     pw_ref, ps_ref, pool_ref, q_ref, k_ref, v_ref, sg_ref, xe_ref) = refs
    tm = h_ref.shape[0]
    i = pl.program_id(0)
    hn = _norm_modulate(x, n1_ref, sh_ref, sc_ref)
    hn_halo = _norm_modulate(x_halo, n1_ref, sh_ref, sc_ref)

    a_halo = _dot(hn_halo, w_ref[:, 0:Q_OFF])
    xe_ref[0:POOL_HALO, :] = jnp.where(i > 0, a_halo[:POOL_HALO], 0.0)
    xe_ref[POOL_HALO:POOL_HALO + tm, :] = _dot(hn, w_ref[:, 0:Q_OFF])
    xe_ref[POOL_HALO + tm:, :] = jnp.where(i < pl.num_programs(0) - 1, a_halo[POOL_HALO:], 0.0)

    a_g = _dot(hn, w_ref[:, G_OFF:IN_DIM])
    a_u = _dot(hn, w_ref[:, U_OFF:G_OFF])
    a_q = _dot(hn, w_ref[:, Q_OFF:K_OFF])
    a_k = _dot(hn, w_ref[:, K_OFF:V_OFF])
    _store_keys(v_ref, _dot(hn, w_ref[:, V_OFF:U_OFF]))

    gv = _gelu_tanh(a_g)
    q_ref[...] = (a_q * _head_rms_scale(a_q) * qg_ref[...]).astype(jnp.bfloat16)
    _store_keys(k_ref, a_k * _head_rms_scale(a_k) * kg_ref[...])

    u = _gelu_tanh(a_u)
    vn = (gv * _head_rms_scale(gv) * sgn_ref[...]).astype(jnp.bfloat16)
    low = lax.broadcasted_iota(jnp.int32, (SG_CHUNK, LANES), 1) < HEAD_DIM
    for c in range(tm // SG_CHUNK):
        rows = slice(c * SG_CHUNK, (c + 1) * SG_CHUNK)
        for s in range(SG_DIM // LANES):
            cols = slice(s * LANES, (s + 1) * LANES)
            m = _dot(sgw_ref[s], vn[rows, cols]) + sgb_ref[s]
            mixed = jnp.where(low, m[:SG_CHUNK], m[SG_CHUNK:])
            sg_ref[rows, cols] = (u[rows, cols] * mixed).astype(jnp.bfloat16)

    pool_ref[...] = _pool_mix(xe_ref, pw_ref, ps_ref, tm, seq_len)


def _inproj_call(stream, n1, sh, sc, w_in, layer, qg, kg, sgn, sgw, sgb, pool_w, pool_scale, tm, strips):
    pending = len(stream) == 3
    n = stream[0].shape[0]
    if strips:
        kv_spec = pl.BlockSpec((N_STRIPS, tm // N_STRIPS, NA_DIM), lambda i: (0, i, 0))
        kv_shape = jax.ShapeDtypeStruct((N_STRIPS, n // N_STRIPS, NA_DIM), jnp.bfloat16)
    else:
        kv_spec = pl.BlockSpec((tm, NA_DIM), lambda i: (i, 0))
        kv_shape = jax.ShapeDtypeStruct((n, NA_DIM), jnp.bfloat16)
    row = lambda i: (i, 0)
    fixed2 = lambda i: (0, 0)
    fixed3 = lambda i: (0, 0, 0)
    vec = lambda w: pl.BlockSpec((1, w), fixed2)
    rows = pl.BlockSpec((tm, D_MODEL), row)
    r8 = tm // POOL_HALO
    before = lambda i: (jnp.maximum(i * r8 - 1, 0), 0)
    after = lambda i: (jnp.minimum((i + 1) * r8, n // POOL_HALO - 1), 0)
    halo = lambda w, m: pl.BlockSpec((POOL_HALO, w), m)
    h = stream[0]
    if pending:
        y, g = stream[1], stream[2]
        yw = y.shape[1]
        args = [h, y, g, h, h, y, y]
        stream_specs = [rows, pl.BlockSpec((tm, yw), row), vec(D_MODEL),
                        halo(D_MODEL, before), halo(D_MODEL, after), halo(yw, before), halo(yw, after)]
    else:
        args = [h, h, h]
        stream_specs = [rows, halo(D_MODEL, before), halo(D_MODEL, after)]
    extra_out_specs = [rows] if pending else []
    extra_out_shape = [jax.ShapeDtypeStruct((n, D_MODEL), jnp.float32)] if pending else []
    return pl.pallas_call(
        functools.partial(_inproj_kernel, pending=pending, seq_len=n),
        grid=(n // tm,),
        in_specs=stream_specs + [
            vec(D_MODEL), vec(D_MODEL), vec(D_MODEL),
            pl.BlockSpec((None, D_MODEL, IN_DIM), lambda i: (layer, 0, 0)),
            vec(NA_DIM), vec(NA_DIM),
            vec(SG_DIM),
            pl.BlockSpec((SG_DIM // LANES, 2 * SG_CHUNK, SG_CHUNK), fixed3),
            pl.BlockSpec((SG_DIM // LANES, 2 * SG_CHUNK, LANES), fixed3),
            pl.BlockSpec((POOL_DIM, POOL_DIM), fixed2),
            vec(POOL_DIM),
        ],
        out_specs=[
            pl.BlockSpec((tm, POOL_DIM), row),
            pl.BlockSpec((tm, NA_DIM), row),
            kv_spec,
            kv_spec,
            pl.BlockSpec((tm, SG_DIM), row),
        ] + extra_out_specs,
        out_shape=[
            jax.ShapeDtypeStruct((n, POOL_DIM), jnp.bfloat16),
            jax.ShapeDtypeStruct((n, NA_DIM), jnp.bfloat16),
            kv_shape,
            kv_shape,
            jax.ShapeDtypeStruct((n, SG_DIM), jnp.bfloat16),
        ] + extra_out_shape,
        scratch_shapes=[pltpu.VMEM((tm + 2 * POOL_HALO, POOL_DIM), jnp.float32)],
        compiler_params=_cparams(("arbitrary",)),
        name="inproj",
    )(*args, n1, sh, sc, w_in, qg, kg, sgn, sgw, sgb, pool_w, pool_scale)


NA_ROWS_PER_BLOCK = 32
NA_GROUP_ROWS = 4
NA_WINDOW_ROWS = NA_GROUP_ROWS + NA_WIN_ROWS
NA_BLOCK = NA_ROWS_PER_BLOCK * GRID_W
NA_GROUP = NA_GROUP_ROWS * GRID_W
NA_HALF_COLS = GRID_W // 2
NA_CHAIN = NA_GROUP_ROWS * NA_HALF_COLS
NA_HALF_STRIPS = 5
NA_HALF_COL0 = (0, GRID_W - NA_HALF_STRIPS * STRIP_W)
NA_RUN = NA_WINDOW_ROWS * STRIP_W
NA_LOCAL = 512
NA_STRIP_BLOCK = NA_ROWS_PER_BLOCK * STRIP_W
NA_STRIP_HALO = (NA_WIN_ROWS // 2) * STRIP_W
NA_EDGE_FIRST, NA_EDGE_NONE, NA_EDGE_LAST = 0, 1, 2


def _stack_heads(x, low):
    zero = jnp.zeros_like(x)
    return jnp.concatenate([jnp.where(low, x, zero), jnp.where(low, zero, x)], axis=0)


def _natten_kernel(q_ref, kp_ref, kc_ref, kn_ref, vp_ref, vc_ref, vn_ref, kx_ref, vx_ref, bias_ref,
                   wg_ref, wu_ref, wd_ref,
                   o_ref, wg_bf_ref, wu_bf_ref, wd_bf_ref, kwin_ref, vwin_ref, vxe_ref, *, grid_rows):
    b = pl.program_id(1)
    wg_bf_ref[...] = wg_ref[...].astype(jnp.bfloat16)
    wu_bf_ref[...] = wu_ref[...].astype(jnp.bfloat16)
    wd_bf_ref[...] = wd_ref[...].astype(jnp.bfloat16)
    top, bottom = NA_STRIP_HALO, NA_STRIP_HALO + NA_STRIP_BLOCK
    kwin_ref[:, 0:top, :] = kp_ref[...]
    kwin_ref[:, top:bottom, :] = kc_ref[...]
    kwin_ref[:, bottom:, :] = kn_ref[...]
    vwin_ref[:, 0:top, 0:LANES] = vp_ref[...]
    vwin_ref[:, top:bottom, 0:LANES] = vc_ref[...]
    vwin_ref[:, bottom:, 0:LANES] = vn_ref[...]
    vwin_ref[:, :, LANES:] = jnp.ones(vwin_ref.shape[:2] + (LANES,), jnp.bfloat16)
    vxe_ref[:, 0:LANES] = vx_ref[...]
    vxe_ref[:, LANES:] = jnp.ones((vxe_ref.shape[0], LANES), jnp.bfloat16)
    low_q = lax.broadcasted_iota(jnp.int32, (NA_CHAIN, LANES), 1) < HEAD_DIM
    n_pad = NA_LOCAL - NA_HALF_STRIPS * NA_RUN

    def window_start(g):
        r0 = b * NA_ROWS_PER_BLOCK + g * NA_GROUP_ROWS
        ws = jnp.clip(r0 - NA_WIN_ROWS // 2, 0, grid_rows - NA_WINDOW_ROWS)
        edge = jnp.where(r0 == 0, NA_EDGE_FIRST,
                         jnp.where(r0 == grid_rows - NA_GROUP_ROWS, NA_EDGE_LAST, NA_EDGE_NONE))
        start = pl.multiple_of((ws - b * NA_ROWS_PER_BLOCK + NA_WIN_ROWS // 2) * STRIP_W, NA_STRIP_HALO)
        return start, edge

    def local_window(win_ref, g, half):
        start, _ = window_start(g)
        s0 = NA_HALF_COL0[half] // STRIP_W
        runs = [win_ref[s, pl.ds(start, NA_RUN), :] for s in range(s0, s0 + NA_HALF_STRIPS)]
        return jnp.concatenate(runs + [jnp.zeros((n_pad, win_ref.shape[2]), jnp.bfloat16)], axis=0)

    def query_rows(g, half, j):
        first = g * NA_GROUP + j * GRID_W + half * NA_HALF_COLS
        return slice(first, first + NA_HALF_COLS)

    def scores(c):
        g, half = divmod(c, 2)
        _, edge = window_start(g)
        qh = jnp.concatenate([q_ref[query_rows(g, half, j), :] for j in range(NA_GROUP_ROWS)], axis=0)
        lhs = _stack_heads(qh, low_q)
        kl = local_window(kwin_ref, g, half)
        return jnp.concatenate([_dot_nt(lhs, kl).astype(jnp.bfloat16) + bias_ref[edge, half],
                                _dot_nt(lhs, kx_ref[...]).astype(jnp.bfloat16)], axis=-1)

    n_chains = 2 * (NA_ROWS_PER_BLOCK // NA_GROUP_ROWS)
    s_next = scores(0)
    for c in range(n_chains):
        s = s_next
        if c + 1 < n_chains:
            s_next = scores(c + 1)
        g, half = divmod(c, 2)
        vl = local_window(vwin_ref, g, half)
        m = jnp.max(s, axis=-1, keepdims=True)
        pb = jnp.exp2(s - m)
        o = _dot(pb[:, :NA_LOCAL], vl) + _dot(pb[:, NA_LOCAL:], vxe_ref[...])
        o = o[:, :LANES] * (1.0 / o[:, LANES:])
        o = jnp.where(low_q, o[:NA_CHAIN], o[NA_CHAIN:]).astype(jnp.bfloat16)
        for j in range(NA_GROUP_ROWS):
            o_ref[query_rows(g, half, j), :] = o[j * NA_HALF_COLS:(j + 1) * NA_HALF_COLS]


def _natten_call(q, k, v, k_ctx, v_ctx, bias, w_gate, w_up, w_down, layer):
    n = q.shape[0]
    grid_rows = n // GRID_W
    assert grid_rows % NA_ROWS_PER_BLOCK == 0 and grid_rows >= 2 * NA_ROWS_PER_BLOCK
    nblk = n // NA_BLOCK
    steps = HEAD_PAIRS * nblk
    depth = w_gate.shape[0]
    up_rows = N_EXPERTS * D_MODEL
    down_rows = N_EXPERTS * D_EXPERT
    assert up_rows % steps == 0 and down_rows % steps == 0
    wg2 = w_gate.reshape(depth * up_rows, D_EXPERT)
    wu2 = w_up.reshape(depth * up_rows, D_EXPERT)
    wd2 = w_down.reshape(depth * down_rows, D_MODEL)
    up_in = pl.BlockSpec((up_rows // steps, D_EXPERT), lambda p, b: (layer * steps + p * nblk + b, 0))
    down_in = pl.BlockSpec((down_rows // steps, D_MODEL), lambda p, b: (layer * steps + p * nblk + b, 0))
    up_out = pl.BlockSpec((up_rows // steps, D_EXPERT), lambda p, b: (p * nblk + b, 0))
    down_out = pl.BlockSpec((down_rows // steps, D_MODEL), lambda p, b: (p * nblk + b, 0))
    n_halo = n // N_STRIPS // NA_STRIP_HALO
    hb = NA_STRIP_BLOCK // NA_STRIP_HALO
    rows = pl.BlockSpec((NA_BLOCK, LANES), lambda p, b: (b, p))
    cur = pl.BlockSpec((N_STRIPS, NA_STRIP_BLOCK, LANES), lambda p, b: (0, b, p))
    prev = pl.BlockSpec((N_STRIPS, NA_STRIP_HALO, LANES), lambda p, b: (0, jnp.maximum(b * hb - 1, 0), p))
    nxt = pl.BlockSpec((N_STRIPS, NA_STRIP_HALO, LANES),
                       lambda p, b: (0, jnp.minimum((b + 1) * hb, n_halo - 1), p))
    ctx = pl.BlockSpec((k_ctx.shape[0], LANES), lambda p, b: (0, p))
    win_rows = NA_STRIP_BLOCK + 2 * NA_STRIP_HALO
    att, wg_bf, wu_bf, wd_bf = pl.pallas_call(
        functools.partial(_natten_kernel, grid_rows=grid_rows),
        grid=(HEAD_PAIRS, nblk),
        in_specs=[rows, prev, cur, nxt, prev, cur, nxt, ctx, ctx,
                  pl.BlockSpec((None, 3, 2, 2 * NA_CHAIN, NA_LOCAL), lambda p, b: (p, 0, 0, 0, 0)),
                  up_in, up_in, down_in],
        out_specs=[rows, up_out, up_out, down_out],
        out_shape=[jax.ShapeDtypeStruct((n, NA_DIM), jnp.bfloat16),
                   jax.ShapeDtypeStruct((up_rows, D_EXPERT), jnp.bfloat16),
                   jax.ShapeDtypeStruct((up_rows, D_EXPERT), jnp.bfloat16),
                   jax.ShapeDtypeStruct((down_rows, D_MODEL), jnp.bfloat16)],
        scratch_shapes=[pltpu.VMEM((N_STRIPS, win_rows, LANES), jnp.bfloat16),
                        pltpu.VMEM((N_STRIPS, win_rows, 2 * LANES), jnp.bfloat16),
                        pltpu.VMEM((k_ctx.shape[0], 2 * LANES), jnp.bfloat16)],
        compiler_params=_cparams(("arbitrary", "arbitrary")),
        name="natten",
    )(q, k, k, k, v, v, v, k_ctx, v_ctx, bias, wg2, wu2, wd2)
    return (att, wg_bf.reshape(N_EXPERTS, D_MODEL, D_EXPERT), wu_bf.reshape(N_EXPERTS, D_MODEL, D_EXPERT),
            wd_bf.reshape(N_EXPERTS, D_EXPERT, D_MODEL))


def _natten_bias(rpb):
    cols = np.arange(GRID_W)
    col_start = np.clip(cols - NA_WIN_COLS // 2, 0, GRID_W - NA_WIN_COLS)
    kc = np.arange(GRID_W)
    in_win = (kc[None, :] >= col_start[:, None]) & (kc[None, :] < col_start[:, None] + NA_WIN_COLS)
    dc = kc[None, :] - cols[:, None] + NA_WIN_COLS - 1
    sel = (np.arange(2 * NA_WIN_COLS - 1)[:, None, None] == dc[None]) & in_win[None]
    t2 = jnp.einsum("hdj,jqk->hdqk", rpb, jnp.asarray(sel, jnp.float32), precision=lax.Precision.HIGHEST)
    t2 = jnp.where(in_win[None, None], t2 * LOG2_E, NEG_BIG)
    neg = jnp.full((NA_HEADS, 1, GRID_W, GRID_W), NEG_BIG, jnp.float32)
    t2e = jnp.concatenate([neg, t2, neg], axis=1)
    u = jnp.concatenate([t2e[:, :-1], t2e[:, 1:]], axis=-1)
    u = u.reshape(HEAD_PAIRS, 2, 2 * NA_WIN_ROWS, GRID_W, LANES)

    place = np.zeros((2, NA_WINDOW_ROWS * GRID_W, NA_LOCAL), np.float32)
    for half, c0 in enumerate(NA_HALF_COL0):
        for a in range(NA_WINDOW_ROWS):
            for kcol in range(c0, c0 + NA_HALF_STRIPS * STRIP_W):
                s, c8 = divmod(kcol - c0, STRIP_W)
                place[half, a * GRID_W + kcol, s * NA_RUN + a * STRIP_W + c8] = 1.0
    outside = np.full((3, NA_GROUP_ROWS, NA_LOCAL), NEG_BIG, np.float32)
    for edge in (NA_EDGE_FIRST, NA_EDGE_NONE, NA_EDGE_LAST):
        for j in range(NA_GROUP_ROWS):
            lo, _ = _window_rows(edge, j)
            for s in range(NA_HALF_STRIPS):
                outside[edge, j, s * NA_RUN + lo * STRIP_W:s * NA_RUN + (lo + NA_WIN_ROWS) * STRIP_W] = 0.0
    return pl.pallas_call(
        _bias_expand_kernel,
        grid=(HEAD_PAIRS,),
        in_specs=[pl.BlockSpec((None, 2, 2 * NA_WIN_ROWS, GRID_W, LANES), lambda p: (p, 0, 0, 0, 0)),
                  pl.BlockSpec(place.shape, lambda p: (0, 0, 0)),
                  pl.BlockSpec(outside.shape, lambda p: (0, 0, 0))],
        out_specs=pl.BlockSpec((None, 3, 2, 2 * NA_CHAIN, NA_LOCAL), lambda p: (p, 0, 0, 0, 0)),
        out_shape=jax.ShapeDtypeStruct((HEAD_PAIRS, 3, 2, 2 * NA_CHAIN, NA_LOCAL), jnp.bfloat16),
        compiler_params=_cparams(("arbitrary",)),
        name="bias_expand",
    )(u, jnp.asarray(place, jnp.bfloat16), jnp.asarray(outside))


def _window_rows(edge, j):
    if edge == NA_EDGE_FIRST:
        return 0, NA_WIN_ROWS - 1 - j
    if edge == NA_EDGE_NONE:
        return j, NA_WIN_ROWS // 2 - 1
    return NA_WINDOW_ROWS - NA_WIN_ROWS, NA_WIN_ROWS // 2 - 1 - j


def _bias_expand_kernel(u_ref, place_ref, outside_ref, o_ref):
    low = lax.broadcasted_iota(jnp.int32, (NA_HALF_COLS, LANES), 1) < GRID_W
    zero = jnp.zeros((NA_HALF_COLS, LANES), jnp.float32)
    for edge in (NA_EDGE_FIRST, NA_EDGE_NONE, NA_EDGE_LAST):
        for half in range(2):
            q0 = half * NA_HALF_COLS
            blocks, masks = [], []
            for hd in range(2):
                for j in range(NA_GROUP_ROWS):
                    lo, base = _window_rows(edge, j)
                    tiles = []
                    for i in range(NA_WINDOW_ROWS // 2):
                        a0, a1 = 2 * i, 2 * i + 1
                        ok0 = lo <= a0 < lo + NA_WIN_ROWS
                        ok1 = lo <= a1 < lo + NA_WIN_ROWS
                        if not (ok0 or ok1):
                            tile = zero
                        else:
                            tile = u_ref[hd, base + a1 - lo, q0:q0 + NA_HALF_COLS, :]
                            if not ok0:
                                tile = jnp.where(low, zero, tile)
                            if not ok1:
                                tile = jnp.where(low, tile, zero)
                        tiles.append(tile)
                    blocks.append(jnp.concatenate(tiles, axis=-1))
                    masks.append(jnp.broadcast_to(outside_ref[edge, j:j + 1, :], (NA_HALF_COLS, NA_LOCAL)))
            lhs = jnp.concatenate(blocks, axis=0).astype(jnp.bfloat16)
            placed = _dot(lhs, place_ref[half]) + jnp.concatenate(masks, axis=0)
            o_ref[edge, half] = placed.astype(o_ref.dtype)


def _ctxatt_kernel(q_ref, k_ref, v_ref, o_ref):
    lc = q_ref.shape[0]
    low = lax.broadcasted_iota(jnp.int32, (lc, LANES), 1) < HEAD_DIM
    lhs = _stack_heads(q_ref[...], low)
    s = _dot_nt(lhs, k_ref[...])
    m = jnp.max(s, axis=-1, keepdims=True)
    p = jnp.exp2(s - m)
    denom = jnp.sum(p, axis=-1, keepdims=True)
    o = _dot(p.astype(jnp.bfloat16), v_ref[...]) * (1.0 / denom)
    o_ref[...] = jnp.where(low, o[:lc], o[lc:]).astype(jnp.bfloat16)


def _ctxatt_call(q, k, v):
    lc = q.shape[0]
    spec = pl.BlockSpec((lc, LANES), lambda p: (0, p))
    return pl.pallas_call(
        _ctxatt_kernel,
        grid=(HEAD_PAIRS,),
        in_specs=[spec, spec, spec],
        out_specs=spec,
        out_shape=jax.ShapeDtypeStruct((lc, NA_DIM), jnp.bfloat16),
        compiler_params=_cparams(("arbitrary",)),
        name="ctxatt",
    )(q, k, v)


OUTPROJ_CHAIN = 256


def _outproj_kernel(h_ref, mp_ref, att_ref, sg_ref, wo_ref, g1_ref, n2_ref, sh_ref, sc_ref,
                    wr_ref, br_ref, h1_ref, pay_ref, route_ref):
    chains = [slice(c * OUTPROJ_CHAIN, (c + 1) * OUTPROJ_CHAIN) for c in range(h_ref.shape[0] // OUTPROJ_CHAIN)]
    h1s = []
    for rows in chains:
        mix = (_dot(mp_ref[rows, :], wo_ref[0:POOL_DIM, :])
               + _dot(att_ref[rows, :], wo_ref[POOL_DIM:POOL_DIM + NA_DIM, :])
               + _dot(sg_ref[rows, :], wo_ref[POOL_DIM + NA_DIM:, :]))
        h1 = h_ref[rows, :] + g1_ref[...] * mix
        h1_ref[rows, :] = h1
        h1s.append(h1)
    for rows, h1 in zip(chains, h1s):
        _outproj_route(rows, h1, n2_ref, sh_ref, sc_ref, wr_ref, br_ref, pay_ref, route_ref)


def _outproj_route(rows, h1, n2_ref, sh_ref, sc_ref, wr_ref, br_ref, pay_ref, route_ref):
    tm = OUTPROJ_CHAIN
    ms = jnp.mean(h1 * h1, axis=-1, keepdims=True)
    hm = (h1 * lax.rsqrt(ms + EPS)) * (n2_ref[...] * (1.0 + sc_ref[...])) + sh_ref[...]
    pay_ref[rows, 0:HALF_D] = _pack_halves(hm)

    hm_hi = hm.astype(jnp.bfloat16)
    lt = _dot_nt(wr_ref[...], hm_hi)
    logits = lt[:N_EXPERTS] + lt[N_EXPERTS:] + br_ref[...]
    e = jnp.exp(logits - jnp.max(logits, axis=0, keepdims=True))

    best = ga = gb = e1 = e2 = cls = None
    for c in range(N_CLASSES):
        a, b2 = int(CLASS_E1[c]), int(CLASS_E2[c])
        ea, eb = e[a:a + 1, :], e[b2:b2 + 1, :]
        s = ea + eb
        if best is None:
            best, ga, gb = s, ea, eb
            e1 = jnp.full_like(s, float(a))
            e2 = jnp.full_like(s, float(b2))
            cls = jnp.zeros_like(s)
        else:
            better = s > best
            best = jnp.where(better, s, best)
            ga = jnp.where(better, ea, ga)
            gb = jnp.where(better, eb, gb)
            e1 = jnp.where(better, float(a), e1)
            e2 = jnp.where(better, float(b2), e2)
            cls = jnp.where(better, float(c), cls)
    inv = 1.0 / best
    row = lax.broadcasted_iota(jnp.int32, (ROUTE_ROWS, tm), 0)
    rec = jnp.where(row == 0, ga * inv,
          jnp.where(row == 1, gb * inv,
          jnp.where(row == 2, e1,
          jnp.where(row == 3, e2,
          jnp.where(row == 4, cls, 0.0)))))
    route_ref[:, rows] = rec
    wide = jnp.concatenate([rec, jnp.zeros((LANES - ROUTE_ROWS, tm), jnp.float32)], axis=0)
    pay_ref[rows, HALF_D:] = pltpu.bitcast(wide.T, jnp.uint32)


def _outproj_call(h, mp, att, sg, w_out, layer, g1, n2, sh2, sc2, wr, br, tm):
    n = h.shape[0]
    row = lambda i: (i, 0)
    fixed = lambda i: (0, 0)
    vec = pl.BlockSpec((1, D_MODEL), fixed)
    return pl.pallas_call(
        _outproj_kernel,
        grid=(n // tm,),
        in_specs=[
            pl.BlockSpec((tm, D_MODEL), row),
            pl.BlockSpec((tm, POOL_DIM), row),
            pl.BlockSpec((tm, NA_DIM), row),
            pl.BlockSpec((tm, SG_DIM), row),
            pl.BlockSpec((None, D_MODEL, D_MODEL), lambda i: (layer, 0, 0)),
            vec, vec, vec, vec,
            pl.BlockSpec((2 * N_EXPERTS, D_MODEL), fixed),
            pl.BlockSpec((N_EXPERTS, 1), fixed),
        ],
        out_specs=[
            pl.BlockSpec((tm, D_MODEL), row),
            pl.BlockSpec((tm, PAYLOAD_W), row),
            pl.BlockSpec((ROUTE_ROWS, tm), lambda i: (0, i)),
        ],
        out_shape=[
            jax.ShapeDtypeStruct((n, D_MODEL), jnp.float32),
            jax.ShapeDtypeStruct((n, PAYLOAD_W), jnp.uint32),
            jax.ShapeDtypeStruct((ROUTE_ROWS, n), jnp.float32),
        ],
        compiler_params=_cparams(("arbitrary",)),
        name="outproj",
    )(h, mp, att, sg, w_out, g1, n2, sh2, sc2, wr, br)


def _payload_parts(pay_ref):
    lo, hi = _unpack_halves(pay_ref[:, 0:HALF_D])
    x = jnp.concatenate([lo, hi], axis=-1).astype(jnp.bfloat16)
    return x, pltpu.bitcast(pay_ref[:, HALF_D:], jnp.float32)


def _expert_pair(x, ga, gb, wga, wua, wda, wgb, wub, wdb):
    ha = (_silu(_dot(x, wga)) * _dot(x, wua) * ga).astype(jnp.bfloat16)
    hb = (_silu(_dot(x, wgb)) * _dot(x, wub) * gb).astype(jnp.bfloat16)
    return _dot(ha, wda) + _dot(hb, wdb)


def _moe_sorted_kernel(blk_ref, e1_ref, e2_ref, nact_ref, pay_ref,
                       wga_ref, wua_ref, wda_ref, wgb_ref, wub_ref, wdb_ref, o_ref):
    i = pl.program_id(0)

    @pl.when(i < nact_ref[0])
    def _():
        x, route = _payload_parts(pay_ref)
        y = _expert_pair(x, route[:, 0:1], route[:, 1:2], wga_ref[...], wua_ref[...], wda_ref[...],
                         wgb_ref[...], wub_ref[...], wdb_ref[...])
        o_ref[...] = _pack_halves(y)


def _moe_sorted_call(blk, e1, e2, nact, pay_sorted, wg, wu, wd):
    n_tiles = blk.shape[0]
    rows = lambda i, blk, e1, e2, na: (blk[i], 0)
    wa = lambda i, blk, e1, e2, na: (e1[i], 0, 0)
    wb = lambda i, blk, e1, e2, na: (e2[i], 0, 0)
    up = lambda m: pl.BlockSpec((None, D_MODEL, D_EXPERT), m)
    down = lambda m: pl.BlockSpec((None, D_EXPERT, D_MODEL), m)
    return pl.pallas_call(
        _moe_sorted_kernel,
        grid_spec=pltpu.PrefetchScalarGridSpec(
            num_scalar_prefetch=4,
            grid=(n_tiles,),
            in_specs=[pl.BlockSpec((MOE_TM, PAYLOAD_W), rows),
                      up(wa), up(wa), down(wa), up(wb), up(wb), down(wb)],
            out_specs=pl.BlockSpec((MOE_TM, HALF_D), rows),
        ),
        out_shape=jax.ShapeDtypeStruct((n_tiles * MOE_TM, HALF_D), jnp.uint32),
        compiler_params=_cparams(("arbitrary",)),
        name="moe_sorted",
    )(blk, e1, e2, nact, pay_sorted, wg, wu, wd, wg, wu, wd)


def _moe_dense_kernel(pay_ref, wg_ref, wu_ref, wd_ref, o_ref):
    e = pl.program_id(0)

    @pl.when(e == 0)
    def _():
        o_ref[...] = jnp.zeros_like(o_ref)

    x, route = _payload_parts(pay_ref)
    ef = e.astype(jnp.float32)
    gate = (jnp.where(route[:, 2:3] == ef, route[:, 0:1], 0.0)
            + jnp.where(route[:, 3:4] == ef, route[:, 1:2], 0.0))
    he = (_silu(_dot(x, wg_ref[...])) * _dot(x, wu_ref[...]) * gate).astype(jnp.bfloat16)
    o_ref[...] += _dot(he, wd_ref[...])


def _moe_dense_call(pay, wg, wu, wd):
    n = pay.shape[0]
    return pl.pallas_call(
        _moe_dense_kernel,
        grid=(N_EXPERTS,),
        in_specs=[pl.BlockSpec((n, PAYLOAD_W), lambda e: (0, 0)),
                  pl.BlockSpec((None, D_MODEL, D_EXPERT), lambda e: (e, 0, 0)),
                  pl.BlockSpec((None, D_MODEL, D_EXPERT), lambda e: (e, 0, 0)),
                  pl.BlockSpec((None, D_EXPERT, D_MODEL), lambda e: (e, 0, 0))],
        out_specs=pl.BlockSpec((n, D_MODEL), lambda e: (0, 0)),
        out_shape=jax.ShapeDtypeStruct((n, D_MODEL), jnp.float32),
        compiler_params=_cparams(("arbitrary",)),
        name="moe_dense",
    )(pay, wg, wu, wd)


def _moe_out(y_ref):
    if y_ref.dtype == jnp.uint32:
        return jnp.concatenate(_unpack_halves(y_ref[...]), axis=-1)
    return y_ref[...]


def _residual_kernel(h_ref, y_ref, g_ref, o_ref):
    o_ref[...] = h_ref[...] + g_ref[...] * _moe_out(y_ref)


def _residual_call(h, y, g, tm):
    n = h.shape[0]
    row = pl.BlockSpec((tm, D_MODEL), lambda i: (i, 0))
    return pl.pallas_call(
        _residual_kernel,
        grid=(n // tm,),
        in_specs=[row, pl.BlockSpec((tm, y.shape[1]), lambda i: (i, 0)),
                  pl.BlockSpec((1, D_MODEL), lambda i: (0, 0))],
        out_specs=row,
        out_shape=jax.ShapeDtypeStruct((n, D_MODEL), jnp.float32),
        compiler_params=_cparams(("arbitrary",)),
        name="residual",
    )(h, y, g)


SC_ROWS = 32


SC_CORES = 2
SC_SUBCORES = 16
SC_WORKERS = SC_CORES * SC_SUBCORES


def _sc_mesh():
    return plsc.VectorSubcoreMesh(core_axis_name="core", subcore_axis_name="subcore")


def _sc_worker():
    return lax.axis_index("subcore") * SC_CORES + lax.axis_index("core")


def _scatter_rows(x, dest, n_out):
    n, w = x.shape

    per_worker = n // SC_WORKERS
    assert per_worker % SC_ROWS == 0

    @functools.partial(pl.kernel, out_type=jax.ShapeDtypeStruct((n_out, w), x.dtype), mesh=_sc_mesh(),
                       scratch_types=[pltpu.VMEM((SC_ROWS,), jnp.int32), pltpu.VMEM((SC_ROWS, w), x.dtype)])
    def scatter(x_hbm, i_hbm, o_hbm, idx_v, rows_v):
        first = _sc_worker() * per_worker

        @pl.loop(0, per_worker // SC_ROWS)
        def _(i):
            base = pl.multiple_of(first + i * SC_ROWS, SC_ROWS)
            pltpu.sync_copy(i_hbm.at[pl.ds(base, SC_ROWS)], idx_v)
            pltpu.sync_copy(x_hbm.at[pl.ds(base, SC_ROWS)], rows_v)
            pltpu.sync_copy(rows_v, o_hbm.at[idx_v])

    return scatter(x, dest)


def _gather_rows(x, src):
    n = src.shape[0]
    w = x.shape[1]

    per_worker = n // SC_WORKERS
    assert per_worker % SC_ROWS == 0

    @functools.partial(pl.kernel, out_type=jax.ShapeDtypeStruct((n, w), x.dtype), mesh=_sc_mesh(),
                       scratch_types=[pltpu.VMEM((SC_ROWS,), jnp.int32), pltpu.VMEM((SC_ROWS, w), x.dtype)])
    def gather(x_hbm, i_hbm, o_hbm, idx_v, rows_v):
        first = _sc_worker() * per_worker

        @pl.loop(0, per_worker // SC_ROWS)
        def _(i):
            base = pl.multiple_of(first + i * SC_ROWS, SC_ROWS)
            pltpu.sync_copy(i_hbm.at[pl.ds(base, SC_ROWS)], idx_v)
            pltpu.sync_copy(x_hbm.at[idx_v], rows_v)
            pltpu.sync_copy(rows_v, o_hbm.at[pl.ds(base, SC_ROWS)])

    return gather(x, src)


def _routing_plan(cls, n_tiles):
    onehot = (cls[:, None] == jnp.arange(N_CLASSES, dtype=jnp.int32)[None, :]).astype(jnp.int32)
    counts = jnp.sum(onehot, axis=0)
    rank = jnp.sum((jnp.cumsum(onehot, axis=0) - onehot) * onehot, axis=1)
    tiles = (counts + MOE_TM - 1) // MOE_TM
    tile_end = jnp.cumsum(tiles)
    tile_start = tile_end - tiles
    dest = jnp.sum(onehot * tile_start[None, :], axis=1) * MOE_TM + rank
    nact = tile_end[-1]
    blk = jnp.minimum(jnp.arange(n_tiles, dtype=jnp.int32), nact - 1)
    tile_cls = jnp.sum((blk[:, None] >= tile_end[None, :]).astype(jnp.int32), axis=1)
    e1 = jnp.asarray(CLASS_E1)[tile_cls]
    e2 = jnp.asarray(CLASS_E2)[tile_cls]
    return dest.astype(jnp.int32), blk, e1, e2, nact.reshape(1).astype(jnp.int32)


def _row_tile(n):
    return 512 if n % 512 == 0 else 256


def kernel(x, c, ctx, c_ctx, w_ada, b_ada, norm1, w_in, pool_w, pool_scale, q_norm, k_norm, rpb,
           sg_w, sg_b, sg_norm, w_out, norm2, w_router, b_router, w_gate, w_up, w_down):
    depth = w_ada.shape[0]
    n = x.shape[1]
    lc = ctx.shape[1]
    bf = jnp.bfloat16
    lat_stream = (x[0],)
    h_ctx = ctx[0]

    cond = jnp.zeros((SUBLANES, D_MODEL), jnp.float32).at[0].set(c[0]).at[1].set(c_ctx)
    mod = _ada_call(cond, w_ada, b_ada)

    wr_t = w_router.T
    wr_hi = wr_t.astype(bf)
    wr_lo = (wr_t - wr_hi.astype(jnp.float32)).astype(bf)
    wr = jnp.concatenate([wr_hi, wr_lo], axis=0)
    br = b_router.reshape(N_EXPERTS, 1)
    n_tiles = n // MOE_TM + N_CLASSES
    w_in_bf = w_in.astype(bf)
    w_out_bf = w_out.astype(bf)

    for l in range(depth):
        last = l == depth - 1
        qg =(q_norm[l] * (HEAD_DIM ** -0.5 * LOG2_E)).reshape(1, NA_DIM)
        kg = k_norm[l].reshape(1, NA_DIM)
        sgn = sg_norm[l].reshape(1, SG_DIM)
        sgw = sg_w[l].astype(bf).reshape(SG_DIM // LANES, 2 * SG_CHUNK, SG_CHUNK)
        sgb = jnp.broadcast_to(sg_b[l].reshape(SG_DIM // LANES, 2 * SG_CHUNK, 1),
                               (SG_DIM // LANES, 2 * SG_CHUNK, LANES))
        pool_bd = jax.scipy.linalg.block_diag(*[pool_w[l, g] for g in range(len(POOL_WINDOWS))]).astype(bf)
        pscale = pool_scale[l].reshape(1, POOL_DIM)
        bias = _natten_bias(rpb[l])
        n1 = norm1[l].reshape(1, D_MODEL)
        n2 = norm2[l].reshape(1, D_MODEL)

        def mods(row):
            return [mod[l, row:row + 1, i * D_MODEL:(i + 1) * D_MODEL] for i in range(6)]

        sh1, sc1, g1, sh2, sc2, g2 = mods(0)
        csh1, csc1, cg1, csh2, csc2, cg2 = mods(1)

        tc = _row_tile(lc)
        mix_pool_c, q_c, k_c, v_c, sg_c = _inproj_call((h_ctx,), n1, csh1, csc1, w_in_bf, l, qg, kg,
                                                       sgn, sgw, sgb, pool_bd, pscale, tc, strips=False)

        tm = _row_tile(n)
        outs = _inproj_call(lat_stream, n1, sh1, sc1, w_in_bf, l, qg, kg, sgn, sgw, sgb,
                            pool_bd, pscale, tm, strips=True)
        mix_pool, q, k, v, sg = outs[:5]
        h_lat = outs[5] if len(lat_stream) == 3 else lat_stream[0]
        att, wg_l, wu_l, wd_l = _natten_call(q, k, v, k_c, v_c, bias, w_gate, w_up, w_down, l)
        h1, pay, route = _outproj_call(h_lat, mix_pool, att, sg, w_out_bf, l, g1, n2, sh2, sc2, wr, br, tm)
        cls = route[4].astype(jnp.int32)
        dest, blk, e1, e2, nact = _routing_plan(cls, n_tiles)
        pay_sorted = _scatter_rows(pay, dest, n_tiles * MOE_TM)
        y_sorted = _moe_sorted_call(blk, e1, e2, nact, pay_sorted, wg_l, wu_l, wd_l)
        y = _gather_rows(y_sorted, dest)
        lat_stream = (h1, y, g2)

        if not last:
            att_c = _ctxatt_call(q_c, k_c, v_c)
            h1_c, pay_c, _ = _outproj_call(h_ctx, mix_pool_c, att_c, sg_c, w_out_bf, l, cg1, n2, csh2, csc2,
                                           wr, br, tc)
            y_c = _moe_dense_call(pay_c, wg_l, wu_l, wd_l)
            h_ctx = _residual_call(h1_c, y_c, cg2, tc)

    return _residual_call(*lat_stream, _row_tile(n))[None]
```

```python
import functools

import jax
import jax.numpy as jnp
import numpy as np
from jax import lax
from jax.experimental import pallas as pl
from jax.experimental.pallas import tpu as pltpu
from jax.experimental.pallas import tpu_sc as plsc

D_MODEL = 1024
GRID_W = 64
HEAD_DIM = 64
POOL_WINDOWS = (2, 4, 8, 16)
POOL_DIM = 256
NA_HEADS = 8
NA_DIM = 512
NA_WIN_ROWS = 8
NA_WIN_COLS = 16
SG_DIM = 256
SG_CHUNK = 128
Q_OFF = POOL_DIM
K_OFF = Q_OFF + NA_DIM
V_OFF = K_OFF + NA_DIM
U_OFF = V_OFF + NA_DIM
G_OFF = U_OFF + SG_DIM
IN_DIM = G_OFF + SG_DIM
N_EXPERTS = 16
GROUP_SIZE = 4
D_EXPERT = 512
EPS = 1e-6

LANES = 128
SUBLANES = 8
HEAD_PAIRS = NA_DIM // LANES
VMEM_LIMIT = 48 * 1024 * 1024

PAIRS = ((0, 1), (0, 2), (1, 2), (1, 3), (0, 3), (2, 3))
N_CLASSES = (N_EXPERTS // GROUP_SIZE) * len(PAIRS)
CLASS_E1 = np.array([4 * g + i for g in range(4) for (i, j) in PAIRS], np.int32)
CLASS_E2 = np.array([4 * g + j for g in range(4) for (i, j) in PAIRS], np.int32)

ROUTE_ROWS = 8
HALF_D = D_MODEL // 2
PAYLOAD_W = HALF_D + LANES
MOE_TM = 256
NEG_BIG = -1e30
LOG2_E = 1.4426950408889634


def _cparams(sem):
    return pltpu.CompilerParams(dimension_semantics=sem, vmem_limit_bytes=VMEM_LIMIT)


def _dot(a, b):
    return jnp.dot(a, b, preferred_element_type=jnp.float32)


def _dot_nt(a, b):
    return lax.dot_general(a, b, (((1,), (1,)), ((), ())), preferred_element_type=jnp.float32)


def _gelu_tanh(x):
    return 0.5 * x * (1.0 + jnp.tanh(0.7978845608028654 * (x + 0.044715 * (x * x * x))))


def _silu(x):
    return x * (1.0 / (1.0 + jnp.exp(-x)))


def _pack_halves(x):
    w = x.shape[1] // 2
    lo = pltpu.bitcast(x[:, :w].astype(jnp.bfloat16).astype(jnp.float32), jnp.uint32) >> 16
    hi = pltpu.bitcast(x[:, w:].astype(jnp.bfloat16).astype(jnp.float32), jnp.uint32) & jnp.uint32(0xFFFF0000)
    return lo | hi


def _unpack_halves(words):
    lo = pltpu.bitcast(words << 16, jnp.float32)
    hi = pltpu.bitcast(words & jnp.uint32(0xFFFF0000), jnp.float32)
    return lo, hi


def _ada_kernel(cond_ref, w_ref, b_ref, o_ref):
    cond = _silu(cond_ref[...])
    o_ref[...] = jnp.dot(cond, w_ref[...], preferred_element_type=jnp.float32,
                         precision=lax.Precision.HIGHEST) + b_ref[...]


def _ada_call(cond, w_ada, b_ada):
    depth = w_ada.shape[0]
    tn = 1536
    return pl.pallas_call(
        _ada_kernel,
        grid=(depth, 6 * D_MODEL // tn),
        in_specs=[
            pl.BlockSpec((SUBLANES, D_MODEL), lambda l, j: (0, 0)),
            pl.BlockSpec((None, D_MODEL, tn), lambda l, j: (l, 0, j)),
            pl.BlockSpec((None, 1, tn), lambda l, j: (l, 0, j)),
        ],
        out_specs=pl.BlockSpec((None, SUBLANES, tn), lambda l, j: (l, 0, j)),
        out_shape=jax.ShapeDtypeStruct((depth, SUBLANES, 6 * D_MODEL), jnp.float32),
        compiler_params=_cparams(("arbitrary", "arbitrary")),
        name="adaln",
    )(cond, w_ada, b_ada.reshape(depth, 1, 6 * D_MODEL))


def _norm_modulate(x, n_ref, sh_ref, sc_ref):
    ms = jnp.mean(x * x, axis=-1, keepdims=True)
    return ((x * lax.rsqrt(ms + EPS)) * (n_ref[...] * (1.0 + sc_ref[...])) + sh_ref[...]).astype(jnp.bfloat16)


def _head_rms_scale(a):
    low = lax.broadcasted_iota(jnp.int32, (a.shape[0], LANES), 1) < HEAD_DIM
    blocks = []
    for p in range(a.shape[1] // LANES):
        sq = jnp.square(a[:, p * LANES:(p + 1) * LANES])
        s_lo = jnp.sum(jnp.where(low, sq, 0.0), axis=-1, keepdims=True)
        s_hi = jnp.sum(jnp.where(low, 0.0, sq), axis=-1, keepdims=True)
        blocks.append(jnp.where(low, lax.rsqrt(s_lo * (1.0 / HEAD_DIM) + EPS),
                                lax.rsqrt(s_hi * (1.0 / HEAD_DIM) + EPS)))
    return jnp.concatenate(blocks, axis=-1)


STRIP_W = 8
N_STRIPS = GRID_W // STRIP_W


def _store_keys(ref, x):
    if len(ref.shape) == 2:
        ref[...] = x.astype(jnp.bfloat16)
        return
    pair = 2 * STRIP_W
    for s in range(N_STRIPS):
        for rp in range(x.shape[0] // (2 * GRID_W)):
            top = 2 * rp * GRID_W + s * STRIP_W
            rows = jnp.concatenate([x[top:top + STRIP_W], x[top + GRID_W:top + GRID_W + STRIP_W]], axis=0)
            ref[s, rp * pair:(rp + 1) * pair, :] = rows.astype(jnp.bfloat16)


POOL_HALO = 8


POOL_EDGE = 16


def _pool_mix(xe_ref, w_ref, scale_ref, tm, seq_len):
    low = lax.broadcasted_iota(jnp.int32, (tm, LANES), 1) < HEAD_DIM
    t_edge = pl.program_id(0) * tm + lax.broadcasted_iota(jnp.int32, (POOL_EDGE, LANES), 0)

    def window_mean(s, half):
        mean = s * (1.0 / (2 * half))

        def clip_fix(t):
            count = (jnp.minimum(t + half, seq_len) - jnp.maximum(t - half, 0)).astype(jnp.float32)
            return (2.0 * half) / count

        return jnp.concatenate([mean[:POOL_EDGE] * clip_fix(t_edge),
                                mean[POOL_EDGE:tm - POOL_EDGE],
                                mean[tm - POOL_EDGE:] * clip_fix(t_edge + (tm - POOL_EDGE))], axis=0)

    def window_sums(xs, n_levels):
        sums = []
        s = xs
        for k in range(n_levels):
            step = 1 << k
            s = s[:-step] + s[step:]
            sums.append(s)
        return sums

    outs = []
    for half_block, windows in enumerate(((2, 4), (8, 16))):
        xs = xe_ref[:, half_block * LANES:(half_block + 1) * LANES]
        sums = window_sums(xs, int(np.log2(windows[1])))
        parts = []
        for w in windows:
            half = w // 2
            s = sums[int(np.log2(w)) - 1][POOL_HALO - half:POOL_HALO - half + tm]
            parts.append(window_mean(s, half))
        mean = jnp.where(low, parts[0], parts[1])
        outs.append(mean - xs[POOL_HALO:POOL_HALO + tm])
    d = jnp.concatenate(outs, axis=-1).astype(jnp.bfloat16)
    return (_dot(d, w_ref[...]) * scale_ref[...]).astype(jnp.bfloat16)


def _inproj_kernel(*refs, pending, seq_len):
    n_stream = 7 if pending else 3
    stream, refs = refs[:n_stream], refs[n_stream:]
    if pending:
        h_ref, y_ref, g_ref, hp_ref, hn_ref, yp_ref, yn_ref = stream
        hres_ref, refs = refs[-2], refs[:-2] + refs[-1:]
        x = h_ref[...] + g_ref[...] * _moe_out(y_ref)
        hres_ref[...] = x
        x_halo = jnp.concatenate([hp_ref[...] + g_ref[...] * _moe_out(yp_ref),
                                  hn_ref[...] + g_ref[...] * _moe_out(yn_ref)], axis=0)
    else:
        h_ref, hp_ref, hn_ref = stream
        x = h_ref[...]
        x_halo = jnp.concatenate([hp_ref[...], hn_ref[...]], axis=0)
    (n1_ref, sh_ref, sc_ref, w_ref, qg_ref, kg_ref, sgn_ref, sgw_ref, sgb_ref,
     pw_ref, ps_ref, pool_ref, q_ref, k_ref, v_ref, sg_ref, xe_ref) = refs
    tm = h_ref.shape[0]
    i = pl.program_id(0)
    hn = _norm_modulate(x, n1_ref, sh_ref, sc_ref)
    hn_halo = _norm_modulate(x_halo, n1_ref, sh_ref, sc_ref)

    a_halo = _dot(hn_halo, w_ref[:, 0:Q_OFF])
    xe_ref[0:POOL_HALO, :] = jnp.where(i > 0, a_halo[:POOL_HALO], 0.0)
    xe_ref[POOL_HALO:POOL_HALO + tm, :] = _dot(hn, w_ref[:, 0:Q_OFF])
    xe_ref[POOL_HALO + tm:, :] = jnp.where(i < pl.num_programs(0) - 1, a_halo[POOL_HALO:], 0.0)

    a_g = _dot(hn, w_ref[:, G_OFF:IN_DIM])
    a_u = _dot(hn, w_ref[:, U_OFF:G_OFF])
    a_q = _dot(hn, w_ref[:, Q_OFF:K_OFF])
    a_k = _dot(hn, w_ref[:, K_OFF:V_OFF])
    _store_keys(v_ref, _dot(hn, w_ref[:, V_OFF:U_OFF]))

    gv = _gelu_tanh(a_g)
    q_ref[...] = (a_q * _head_rms_scale(a_q) * qg_ref[...]).astype(jnp.bfloat16)
    _store_keys(k_ref, a_k * _head_rms_scale(a_k) * kg_ref[...])

    u = _gelu_tanh(a_u)
    vn = (gv * _head_rms_scale(gv) * sgn_ref[...]).astype(jnp.bfloat16)
    low = lax.broadcasted_iota(jnp.int32, (SG_CHUNK, LANES), 1) < HEAD_DIM
    for c in range(tm // SG_CHUNK):
        rows = slice(c * SG_CHUNK, (c + 1) * SG_CHUNK)
        for s in range(SG_DIM // LANES):
            cols = slice(s * LANES, (s + 1) * LANES)
            m = _dot(sgw_ref[s], vn[rows, cols]) + sgb_ref[s]
            mixed = jnp.where(low, m[:SG_CHUNK], m[SG_CHUNK:])
            sg_ref[rows, cols] = (u[rows, cols] * mixed).astype(jnp.bfloat16)

    pool_ref[...] = _pool_mix(xe_ref, pw_ref, ps_ref, tm, seq_len)


def _inproj_call(stream, n1, sh, sc, w_in, layer, qg, kg, sgn, sgw, sgb, pool_w, pool_scale, tm, strips):
    pending = len(stream) == 3
    n = stream[0].shape[0]
    if strips:
        kv_spec = pl.BlockSpec((N_STRIPS, tm // N_STRIPS, NA_DIM), lambda i: (0, i, 0))
        kv_shape = jax.ShapeDtypeStruct((N_STRIPS, n // N_STRIPS, NA_DIM), jnp.bfloat16)
    else:
        kv_spec = pl.BlockSpec((tm, NA_DIM), lambda i: (i, 0))
        kv_shape = jax.ShapeDtypeStruct((n, NA_DIM), jnp.bfloat16)
    row = lambda i: (i, 0)
    fixed2 = lambda i: (0, 0)
    fixed3 = lambda i: (0, 0, 0)
    vec = lambda w: pl.BlockSpec((1, w), fixed2)
    rows = pl.BlockSpec((tm, D_MODEL), row)
    r8 = tm // POOL_HALO
    before = lambda i: (jnp.maximum(i * r8 - 1, 0), 0)
    after = lambda i: (jnp.minimum((i + 1) * r8, n // POOL_HALO - 1), 0)
    halo = lambda w, m: pl.BlockSpec((POOL_HALO, w), m)
    h = stream[0]
    if pending:
        y, g = stream[1], stream[2]
        yw = y.shape[1]
        args = [h, y, g, h, h, y, y]
        stream_specs = [rows, pl.BlockSpec((tm, yw), row), vec(D_MODEL),
                        halo(D_MODEL, before), halo(D_MODEL, after), halo(yw, before), halo(yw, after)]
    else:
        args = [h, h, h]
        stream_specs = [rows, halo(D_MODEL, before), halo(D_MODEL, after)]
    extra_out_specs = [rows] if pending else []
    extra_out_shape = [jax.ShapeDtypeStruct((n, D_MODEL), jnp.float32)] if pending else []
    return pl.pallas_call(
        functools.partial(_inproj_kernel, pending=pending, seq_len=n),
        grid=(n // tm,),
        in_specs=stream_specs + [
            vec(D_MODEL), vec(D_MODEL), vec(D_MODEL),
            pl.BlockSpec((None, D_MODEL, IN_DIM), lambda i: (layer, 0, 0)),
            vec(NA_DIM), vec(NA_DIM),
            vec(SG_DIM),
            pl.BlockSpec((SG_DIM // LANES, 2 * SG_CHUNK, SG_CHUNK), fixed3),
            pl.BlockSpec((SG_DIM // LANES, 2 * SG_CHUNK, LANES), fixed3),
            pl.BlockSpec((POOL_DIM, POOL_DIM), fixed2),
            vec(POOL_DIM),
        ],
        out_specs=[
            pl.BlockSpec((tm, POOL_DIM), row),
            pl.BlockSpec((tm, NA_DIM), row),
            kv_spec,
            kv_spec,
            pl.BlockSpec((tm, SG_DIM), row),
        ] + extra_out_specs,
        out_shape=[
            jax.ShapeDtypeStruct((n, POOL_DIM), jnp.bfloat16),
            jax.ShapeDtypeStruct((n, NA_DIM), jnp.bfloat16),
            kv_shape,
            kv_shape,
            jax.ShapeDtypeStruct((n, SG_DIM), jnp.bfloat16),
        ] + extra_out_shape,
        scratch_shapes=[pltpu.VMEM((tm + 2 * POOL_HALO, POOL_DIM), jnp.float32)],
        compiler_params=_cparams(("arbitrary",)),
        name="inproj",
    )(*args, n1, sh, sc, w_in, qg, kg, sgn, sgw, sgb, pool_w, pool_scale)


NA_ROWS_PER_BLOCK = 64
NA_GROUP_ROWS = 4
NA_WINDOW_ROWS = NA_GROUP_ROWS + NA_WIN_ROWS
NA_BLOCK = NA_ROWS_PER_BLOCK * GRID_W
NA_GROUP = NA_GROUP_ROWS * GRID_W
NA_HALF_COLS = GRID_W // 2
NA_CHAIN = NA_GROUP_ROWS * NA_HALF_COLS
NA_HALF_STRIPS = 5
NA_HALF_COL0 = (0, GRID_W - NA_HALF_STRIPS * STRIP_W)
NA_RUN = NA_WINDOW_ROWS * STRIP_W
NA_LOCAL = 512
NA_STRIP_BLOCK = NA_ROWS_PER_BLOCK * STRIP_W
NA_STRIP_HALO = (NA_WIN_ROWS // 2) * STRIP_W
NA_EDGE_FIRST, NA_EDGE_NONE, NA_EDGE_LAST = 0, 1, 2


def _stack_heads(x, low):
    zero = jnp.zeros_like(x)
    return jnp.concatenate([jnp.where(low, x, zero), jnp.where(low, zero, x)], axis=0)


def _natten_kernel(q_ref, kp_ref, kc_ref, kn_ref, vp_ref, vc_ref, vn_ref, kx_ref, vx_ref, bias_ref,
                   wg_ref, wu_ref, wd_ref,
                   o_ref, wg_bf_ref, wu_bf_ref, wd_bf_ref, kwin_ref, vwin_ref, vxe_ref, *, grid_rows):
    b = pl.program_id(1)
    wg_bf_ref[...] = wg_ref[...].astype(jnp.bfloat16)
    wu_bf_ref[...] = wu_ref[...].astype(jnp.bfloat16)
    wd_bf_ref[...] = wd_ref[...].astype(jnp.bfloat16)
    top, bottom = NA_STRIP_HALO, NA_STRIP_HALO + NA_STRIP_BLOCK
    kwin_ref[:, 0:top, :] = kp_ref[...]
    kwin_ref[:, top:bottom, :] = kc_ref[...]
    kwin_ref[:, bottom:, :] = kn_ref[...]
    vwin_ref[:, 0:top, 0:LANES] = vp_ref[...]
    vwin_ref[:, top:bottom, 0:LANES] = vc_ref[...]
    vwin_ref[:, bottom:, 0:LANES] = vn_ref[...]
    vwin_ref[:, :, LANES:] = jnp.ones(vwin_ref.shape[:2] + (LANES,), jnp.bfloat16)
    vxe_ref[:, 0:LANES] = vx_ref[...]
    vxe_ref[:, LANES:] = jnp.ones((vxe_ref.shape[0], LANES), jnp.bfloat16)
    low_q = lax.broadcasted_iota(jnp.int32, (NA_CHAIN, LANES), 1) < HEAD_DIM
    n_pad = NA_LOCAL - NA_HALF_STRIPS * NA_RUN

    def window_start(g):
        r0 = b * NA_ROWS_PER_BLOCK + g * NA_GROUP_ROWS
        ws = jnp.clip(r0 - NA_WIN_ROWS // 2, 0, grid_rows - NA_WINDOW_ROWS)
        edge = jnp.where(r0 == 0, NA_EDGE_FIRST,
                         jnp.where(r0 == grid_rows - NA_GROUP_ROWS, NA_EDGE_LAST, NA_EDGE_NONE))
        start = pl.multiple_of((ws - b * NA_ROWS_PER_BLOCK + NA_WIN_ROWS // 2) * STRIP_W, NA_STRIP_HALO)
        return start, edge

    def local_window(win_ref, g, half):
        start, _ = window_start(g)
        s0 = NA_HALF_COL0[half] // STRIP_W
        runs = [win_ref[s, pl.ds(start, NA_RUN), :] for s in range(s0, s0 + NA_HALF_STRIPS)]
        return jnp.concatenate(runs + [jnp.zeros((n_pad, win_ref.shape[2]), jnp.bfloat16)], axis=0)

    def query_rows(g, half, j):
        first = g * NA_GROUP + j * GRID_W + half * NA_HALF_COLS
        return slice(first, first + NA_HALF_COLS)

    def scores(c):
        g, half = divmod(c, 2)
        _, edge = window_start(g)
        qh = jnp.concatenate([q_ref[query_rows(g, half, j), :] for j in range(NA_GROUP_ROWS)], axis=0)
        lhs = _stack_heads(qh, low_q)
        kl = local_window(kwin_ref, g, half)
        return jnp.concatenate([_dot_nt(lhs, kl).astype(jnp.bfloat16) + bias_ref[edge, half],
                                _dot_nt(lhs, kx_ref[...]).astype(jnp.bfloat16)], axis=-1)

    n_chains = 2 * (NA_ROWS_PER_BLOCK // NA_GROUP_ROWS)
    s_next = scores(0)
    for c in range(n_chains):
        s = s_next
        if c + 1 < n_chains:
            s_next = scores(c + 1)
        g, half = divmod(c, 2)
        vl = local_window(vwin_ref, g, half)
        m = jnp.max(s, axis=-1, keepdims=True)
        pb = jnp.exp2(s - m)
        o = _dot(pb[:, :NA_LOCAL], vl) + _dot(pb[:, NA_LOCAL:], vxe_ref[...])
        o = o[:, :LANES] * (1.0 / o[:, LANES:])
        o = jnp.where(low_q, o[:NA_CHAIN], o[NA_CHAIN:]).astype(jnp.bfloat16)
        for j in range(NA_GROUP_ROWS):
            o_ref[query_rows(g, half, j), :] = o[j * NA_HALF_COLS:(j + 1) * NA_HALF_COLS]


def _natten_call(q, k, v, k_ctx, v_ctx, bias, w_gate, w_up, w_down, layer):
    n = q.shape[0]
    grid_rows = n // GRID_W
    assert grid_rows % NA_ROWS_PER_BLOCK == 0 and grid_rows >= 2 * NA_ROWS_PER_BLOCK
    nblk = n // NA_BLOCK
    steps = HEAD_PAIRS * nblk
    depth = w_gate.shape[0]
    up_rows = N_EXPERTS * D_MODEL
    down_rows = N_EXPERTS * D_EXPERT
    assert up_rows % steps == 0 and down_rows % steps == 0
    wg2 = w_gate.reshape(depth * up_rows, D_EXPERT)
    wu2 = w_up.reshape(depth * up_rows, D_EXPERT)
    wd2 = w_down.reshape(depth * down_rows, D_MODEL)
    up_in = pl.BlockSpec((up_rows // steps, D_EXPERT), lambda p, b: (layer * steps + p * nblk + b, 0))
    down_in = pl.BlockSpec((down_rows // steps, D_MODEL), lambda p, b: (layer * steps + p * nblk + b, 0))
    up_out = pl.BlockSpec((up_rows // steps, D_EXPERT), lambda p, b: (p * nblk + b, 0))
    down_out = pl.BlockSpec((down_rows // steps, D_MODEL), lambda p, b: (p * nblk + b, 0))
    n_halo = n // N_STRIPS // NA_STRIP_HALO
    hb = NA_STRIP_BLOCK // NA_STRIP_HALO
    rows = pl.BlockSpec((NA_BLOCK, LANES), lambda p, b: (b, p))
    cur = pl.BlockSpec((N_STRIPS, NA_STRIP_BLOCK, LANES), lambda p, b: (0, b, p))
    prev = pl.BlockSpec((N_STRIPS, NA_STRIP_HALO, LANES), lambda p, b: (0, jnp.maximum(b * hb - 1, 0), p))
    nxt = pl.BlockSpec((N_STRIPS, NA_STRIP_HALO, LANES),
                       lambda p, b: (0, jnp.minimum((b + 1) * hb, n_halo - 1), p))
    ctx = pl.BlockSpec((k_ctx.shape[0], LANES), lambda p, b: (0, p))
    win_rows = NA_STRIP_BLOCK + 2 * NA_STRIP_HALO
    att, wg_bf, wu_bf, wd_bf = pl.pallas_call(
        functools.partial(_natten_kernel, grid_rows=grid_rows),
        grid=(HEAD_PAIRS, nblk),
        in_specs=[rows, prev, cur, nxt, prev, cur, nxt, ctx, ctx,
                  pl.BlockSpec((None, 3, 2, 2 * NA_CHAIN, NA_LOCAL), lambda p, b: (p, 0, 0, 0, 0)),
                  up_in, up_in, down_in],
        out_specs=[rows, up_out, up_out, down_out],
        out_shape=[jax.ShapeDtypeStruct((n, NA_DIM), jnp.bfloat16),
                   jax.ShapeDtypeStruct((up_rows, D_EXPERT), jnp.bfloat16),
                   jax.ShapeDtypeStruct((up_rows, D_EXPERT), jnp.bfloat16),
                   jax.ShapeDtypeStruct((down_rows, D_MODEL), jnp.bfloat16)],
        scratch_shapes=[pltpu.VMEM((N_STRIPS, win_rows, LANES), jnp.bfloat16),
                        pltpu.VMEM((N_STRIPS, win_rows, 2 * LANES), jnp.bfloat16),
                        pltpu.VMEM((k_ctx.shape[0], 2 * LANES), jnp.bfloat16)],
        compiler_params=_cparams(("arbitrary", "arbitrary")),
        name="natten",
    )(q, k, k, k, v, v, v, k_ctx, v_ctx, bias, wg2, wu2, wd2)
    return (att, wg_bf.reshape(N_EXPERTS, D_MODEL, D_EXPERT), wu_bf.reshape(N_EXPERTS, D_MODEL, D_EXPERT),
            wd_bf.reshape(N_EXPERTS, D_EXPERT, D_MODEL))


def _natten_bias(rpb):
    cols = np.arange(GRID_W)
    col_start = np.clip(cols - NA_WIN_COLS // 2, 0, GRID_W - NA_WIN_COLS)
    kc = np.arange(GRID_W)
    in_win = (kc[None, :] >= col_start[:, None]) & (kc[None, :] < col_start[:, None] + NA_WIN_COLS)
    dc = kc[None, :] - cols[:, None] + NA_WIN_COLS - 1
    sel = (np.arange(2 * NA_WIN_COLS - 1)[:, None, None] == dc[None]) & in_win[None]
    t2 = jnp.einsum("hdj,jqk->hdqk", rpb, jnp.asarray(sel, jnp.float32), precision=lax.Precision.HIGHEST)
    t2 = jnp.where(in_win[None, None], t2 * LOG2_E, NEG_BIG)
    neg = jnp.full((NA_HEADS, 1, GRID_W, GRID_W), NEG_BIG, jnp.float32)
    t2e = jnp.concatenate([neg, t2, neg], axis=1)
    u = jnp.concatenate([t2e[:, :-1], t2e[:, 1:]], axis=-1)
    u = u.reshape(HEAD_PAIRS, 2, 2 * NA_WIN_ROWS, GRID_W, LANES)

    place = np.zeros((2, NA_WINDOW_ROWS * GRID_W, NA_LOCAL), np.float32)
    for half, c0 in enumerate(NA_HALF_COL0):
        for a in range(NA_WINDOW_ROWS):
            for kcol in range(c0, c0 + NA_HALF_STRIPS * STRIP_W):
                s, c8 = divmod(kcol - c0, STRIP_W)
                place[half, a * GRID_W + kcol, s * NA_RUN + a * STRIP_W + c8] = 1.0
    outside = np.full((3, NA_GROUP_ROWS, NA_LOCAL), NEG_BIG, np.float32)
    for edge in (NA_EDGE_FIRST, NA_EDGE_NONE, NA_EDGE_LAST):
        for j in range(NA_GROUP_ROWS):
            lo, _ = _window_rows(edge, j)
            for s in range(NA_HALF_STRIPS):
                outside[edge, j, s * NA_RUN + lo * STRIP_W:s * NA_RUN + (lo + NA_WIN_ROWS) * STRIP_W] = 0.0
    return pl.pallas_call(
        _bias_expand_kernel,
        grid=(HEAD_PAIRS,),
        in_specs=[pl.BlockSpec((None, 2, 2 * NA_WIN_ROWS, GRID_W, LANES), lambda p: (p, 0, 0, 0, 0)),
                  pl.BlockSpec(place.shape, lambda p: (0, 0, 0)),
                  pl.BlockSpec(outside.shape, lambda p: (0, 0, 0))],
        out_specs=pl.BlockSpec((None, 3, 2, 2 * NA_CHAIN, NA_LOCAL), lambda p: (p, 0, 0, 0, 0)),
        out_shape=jax.ShapeDtypeStruct((HEAD_PAIRS, 3, 2, 2 * NA_CHAIN, NA_LOCAL), jnp.bfloat16),
        compiler_params=_cparams(("arbitrary",)),
        name="bias_expand",
    )(u, jnp.asarray(place, jnp.bfloat16), jnp.asarray(outside))


def _window_rows(edge, j):
    if edge == NA_EDGE_FIRST:
        return 0, NA_WIN_ROWS - 1 - j
    if edge == NA_EDGE_NONE:
        return j, NA_WIN_ROWS // 2 - 1
    return NA_WINDOW_ROWS - NA_WIN_ROWS, NA_WIN_ROWS // 2 - 1 - j


def _bias_expand_kernel(u_ref, place_ref, outside_ref, o_ref):
    low = lax.broadcasted_iota(jnp.int32, (NA_HALF_COLS, LANES), 1) < GRID_W
    zero = jnp.zeros((NA_HALF_COLS, LANES), jnp.float32)
    for edge in (NA_EDGE_FIRST, NA_EDGE_NONE, NA_EDGE_LAST):
        for half in range(2):
            q0 = half * NA_HALF_COLS
            blocks, masks = [], []
            for hd in range(2):
                for j in range(NA_GROUP_ROWS):
                    lo, base = _window_rows(edge, j)
                    tiles = []
                    for i in range(NA_WINDOW_ROWS // 2):
                        a0, a1 = 2 * i, 2 * i + 1
                        ok0 = lo <= a0 < lo + NA_WIN_ROWS
                        ok1 = lo <= a1 < lo + NA_WIN_ROWS
                        if not (ok0 or ok1):
                            tile = zero
                        else:
                            tile = u_ref[hd, base + a1 - lo, q0:q0 + NA_HALF_COLS, :]
                            if not ok0:
                                tile = jnp.where(low, zero, tile)
                            if not ok1:
                                tile = jnp.where(low, tile, zero)
                        tiles.append(tile)
                    blocks.append(jnp.concatenate(tiles, axis=-1))
                    masks.append(jnp.broadcast_to(outside_ref[edge, j:j + 1, :], (NA_HALF_COLS, NA_LOCAL)))
            lhs = jnp.concatenate(blocks, axis=0).astype(jnp.bfloat16)
            placed = _dot(lhs, place_ref[half]) + jnp.concatenate(masks, axis=0)
            o_ref[edge, half] = placed.astype(o_ref.dtype)


def _ctxatt_kernel(q_ref, k_ref, v_ref, o_ref):
    lc = q_ref.shape[0]
    low = lax.broadcasted_iota(jnp.int32, (lc, LANES), 1) < HEAD_DIM
    lhs = _stack_heads(q_ref[...], low)
    s = _dot_nt(lhs, k_ref[...])
    m = jnp.max(s, axis=-1, keepdims=True)
    p = jnp.exp2(s - m)
    denom = jnp.sum(p, axis=-1, keepdims=True)
    o = _dot(p.astype(jnp.bfloat16), v_ref[...]) * (1.0 / denom)
    o_ref[...] = jnp.where(low, o[:lc], o[lc:]).astype(jnp.bfloat16)


def _ctxatt_call(q, k, v):
    lc = q.shape[0]
    spec = pl.BlockSpec((lc, LANES), lambda p: (0, p))
    return pl.pallas_call(
        _ctxatt_kernel,
        grid=(HEAD_PAIRS,),
        in_specs=[spec, spec, spec],
        out_specs=spec,
        out_shape=jax.ShapeDtypeStruct((lc, NA_DIM), jnp.bfloat16),
        compiler_params=_cparams(("arbitrary",)),
        name="ctxatt",
    )(q, k, v)


OUTPROJ_CHAIN = 256
OUTPROJ_TILE = 1024


def _outproj_kernel(h_ref, mp_ref, att_ref, sg_ref, wo_ref, g1_ref, n2_ref, sh_ref, sc_ref,
                    wr_ref, br_ref, h1_ref, pay_ref, route_ref):
    chains = [slice(c * OUTPROJ_CHAIN, (c + 1) * OUTPROJ_CHAIN) for c in range(h_ref.shape[0] // OUTPROJ_CHAIN)]
    h1s = []
    for rows in chains:
        mix = (_dot(mp_ref[rows, :], wo_ref[0:POOL_DIM, :])
               + _dot(att_ref[rows, :], wo_ref[POOL_DIM:POOL_DIM + NA_DIM, :])
               + _dot(sg_ref[rows, :], wo_ref[POOL_DIM + NA_DIM:, :]))
        h1 = h_ref[rows, :] + g1_ref[...] * mix
        h1_ref[rows, :] = h1
        h1s.append(h1)
    for rows, h1 in zip(chains, h1s):
        _outproj_route(rows, h1, n2_ref, sh_ref, sc_ref, wr_ref, br_ref, pay_ref, route_ref)


def _outproj_route(rows, h1, n2_ref, sh_ref, sc_ref, wr_ref, br_ref, pay_ref, route_ref):
    tm = OUTPROJ_CHAIN
    ms = jnp.mean(h1 * h1, axis=-1, keepdims=True)
    hm = (h1 * lax.rsqrt(ms + EPS)) * (n2_ref[...] * (1.0 + sc_ref[...])) + sh_ref[...]
    pay_ref[rows, 0:HALF_D] = _pack_halves(hm)

    hm_hi = hm.astype(jnp.bfloat16)
    lt = _dot_nt(wr_ref[...], hm_hi)
    logits = lt[:N_EXPERTS] + lt[N_EXPERTS:] + br_ref[...]
    e = jnp.exp(logits - jnp.max(logits, axis=0, keepdims=True))

    best = ga = gb = e1 = e2 = cls = None
    for c in range(N_CLASSES):
        a, b2 = int(CLASS_E1[c]), int(CLASS_E2[c])
        ea, eb = e[a:a + 1, :], e[b2:b2 + 1, :]
        s = ea + eb
        if best is None:
            best, ga, gb = s, ea, eb
            e1 = jnp.full_like(s, float(a))
            e2 = jnp.full_like(s, float(b2))
            cls = jnp.zeros_like(s)
        else:
            better = s > best
            best = jnp.where(better, s, best)
            ga = jnp.where(better, ea, ga)
            gb = jnp.where(better, eb, gb)
            e1 = jnp.where(better, float(a), e1)
            e2 = jnp.where(better, float(b2), e2)
            cls = jnp.where(better, float(c), cls)
    inv = 1.0 / best
    row = lax.broadcasted_iota(jnp.int32, (ROUTE_ROWS, tm), 0)
    rec = jnp.where(row == 0, ga * inv,
          jnp.where(row == 1, gb * inv,
          jnp.where(row == 2, e1,
          jnp.where(row == 3, e2,
          jnp.where(row == 4, cls, 0.0)))))
    route_ref[:, rows] = rec
    wide = jnp.concatenate([rec, jnp.zeros((LANES - ROUTE_ROWS, tm), jnp.float32)], axis=0)
    pay_ref[rows, HALF_D:] = pltpu.bitcast(wide.T, jnp.uint32)


def _outproj_call(h, mp, att, sg, w_out, layer, g1, n2, sh2, sc2, wr, br, tm):
    n = h.shape[0]
    row = lambda i: (i, 0)
    fixed = lambda i: (0, 0)
    vec = pl.BlockSpec((1, D_MODEL), fixed)
    return pl.pallas_call(
        _outproj_kernel,
        grid=(n // tm,),
        in_specs=[
            pl.BlockSpec((tm, D_MODEL), row),
            pl.BlockSpec((tm, POOL_DIM), row),
            pl.BlockSpec((tm, NA_DIM), row),
            pl.BlockSpec((tm, SG_DIM), row),
            pl.BlockSpec((None, D_MODEL, D_MODEL), lambda i: (layer, 0, 0)),
            vec, vec, vec, vec,
            pl.BlockSpec((2 * N_EXPERTS, D_MODEL), fixed),
            pl.BlockSpec((N_EXPERTS, 1), fixed),
        ],
        out_specs=[
            pl.BlockSpec((tm, D_MODEL), row),
            pl.BlockSpec((tm, PAYLOAD_W), row),
            pl.BlockSpec((ROUTE_ROWS, tm), lambda i: (0, i)),
        ],
        out_shape=[
            jax.ShapeDtypeStruct((n, D_MODEL), jnp.float32),
            jax.ShapeDtypeStruct((n, PAYLOAD_W), jnp.uint32),
            jax.ShapeDtypeStruct((ROUTE_ROWS, n), jnp.float32),
        ],
        compiler_params=_cparams(("arbitrary",)),
        name="outproj",
    )(h, mp, att, sg, w_out, g1, n2, sh2, sc2, wr, br)


def _payload_parts(pay_ref):
    lo, hi = _unpack_halves(pay_ref[:, 0:HALF_D])
    x = jnp.concatenate([lo, hi], axis=-1).astype(jnp.bfloat16)
    return x, pltpu.bitcast(pay_ref[:, HALF_D:], jnp.float32)


def _expert_pair(x, ga, gb, wga, wua, wda, wgb, wub, wdb):
    ha = (_silu(_dot(x, wga)) * _dot(x, wua) * ga).astype(jnp.bfloat16)
    hb = (_silu(_dot(x, wgb)) * _dot(x, wub) * gb).astype(jnp.bfloat16)
    return _dot(ha, wda) + _dot(hb, wdb)


def _moe_sorted_kernel(blk_ref, e1_ref, e2_ref, nact_ref, pay_ref,
                       wga_ref, wua_ref, wda_ref, wgb_ref, wub_ref, wdb_ref, o_ref):
    i = pl.program_id(0)

    @pl.when(i < nact_ref[0])
    def _():
        x, route = _payload_parts(pay_ref)
        y = _expert_pair(x, route[:, 0:1], route[:, 1:2], wga_ref[...], wua_ref[...], wda_ref[...],
                         wgb_ref[...], wub_ref[...], wdb_ref[...])
        o_ref[...] = _pack_halves(y)


def _moe_sorted_call(blk, e1, e2, nact, pay_sorted, wg, wu, wd):
    n_tiles = blk.shape[0]
    rows = lambda i, blk, e1, e2, na: (blk[i], 0)
    wa = lambda i, blk, e1, e2, na: (e1[i], 0, 0)
    wb = lambda i, blk, e1, e2, na: (e2[i], 0, 0)
    up = lambda m: pl.BlockSpec((None, D_MODEL, D_EXPERT), m)
    down = lambda m: pl.BlockSpec((None, D_EXPERT, D_MODEL), m)
    return pl.pallas_call(
        _moe_sorted_kernel,
        grid_spec=pltpu.PrefetchScalarGridSpec(
            num_scalar_prefetch=4,
            grid=(n_tiles,),
            in_specs=[pl.BlockSpec((MOE_TM, PAYLOAD_W), rows),
                      up(wa), up(wa), down(wa), up(wb), up(wb), down(wb)],
            out_specs=pl.BlockSpec((MOE_TM, HALF_D), rows),
        ),
        out_shape=jax.ShapeDtypeStruct((n_tiles * MOE_TM, HALF_D), jnp.uint32),
        compiler_params=_cparams(("arbitrary",)),
        name="moe_sorted",
    )(blk, e1, e2, nact, pay_sorted, wg, wu, wd, wg, wu, wd)


def _moe_dense_kernel(pay_ref, wg_ref, wu_ref, wd_ref, o_ref):
    e = pl.program_id(0)

    @pl.when(e == 0)
    def _():
        o_ref[...] = jnp.zeros_like(o_ref)

    x, route = _payload_parts(pay_ref)
    ef = e.astype(jnp.float32)
    gate = (jnp.where(route[:, 2:3] == ef, route[:, 0:1], 0.0)
            + jnp.where(route[:, 3:4] == ef, route[:, 1:2], 0.0))
    he = (_silu(_dot(x, wg_ref[...])) * _dot(x, wu_ref[...]) * gate).astype(jnp.bfloat16)
    o_ref[...] += _dot(he, wd_ref[...])


def _moe_dense_call(pay, wg, wu, wd):
    n = pay.shape[0]
    return pl.pallas_call(
        _moe_dense_kernel,
        grid=(N_EXPERTS,),
        in_specs=[pl.BlockSpec((n, PAYLOAD_W), lambda e: (0, 0)),
                  pl.BlockSpec((None, D_MODEL, D_EXPERT), lambda e: (e, 0, 0)),
                  pl.BlockSpec((None, D_MODEL, D_EXPERT), lambda e: (e, 0, 0)),
                  pl.BlockSpec((None, D_EXPERT, D_MODEL), lambda e: (e, 0, 0))],
        out_specs=pl.BlockSpec((n, D_MODEL), lambda e: (0, 0)),
        out_shape=jax.ShapeDtypeStruct((n, D_MODEL), jnp.float32),
        compiler_params=_cparams(("arbitrary",)),
        name="moe_dense",
    )(pay, wg, wu, wd)


def _moe_out(y_ref):
    if y_ref.dtype == jnp.uint32:
        return jnp.concatenate(_unpack_halves(y_ref[...]), axis=-1)
    return y_ref[...]


def _residual_kernel(h_ref, y_ref, g_ref, o_ref):
    o_ref[...] = h_ref[...] + g_ref[...] * _moe_out(y_ref)


def _residual_call(h, y, g, tm):
    n = h.shape[0]
    row = pl.BlockSpec((tm, D_MODEL), lambda i: (i, 0))
    return pl.pallas_call(
        _residual_kernel,
        grid=(n // tm,),
        in_specs=[row, pl.BlockSpec((tm, y.shape[1]), lambda i: (i, 0)),
                  pl.BlockSpec((1, D_MODEL), lambda i: (0, 0))],
        out_specs=row,
        out_shape=jax.ShapeDtypeStruct((n, D_MODEL), jnp.float32),
        compiler_params=_cparams(("arbitrary",)),
        name="residual",
    )(h, y, g)


SC_ROWS = 32


SC_CORES = 2
SC_SUBCORES = 16
SC_WORKERS = SC_CORES * SC_SUBCORES


def _sc_mesh():
    return plsc.VectorSubcoreMesh(core_axis_name="core", subcore_axis_name="subcore")


def _sc_worker():
    return lax.axis_index("subcore") * SC_CORES + lax.axis_index("core")


def _scatter_rows(x, dest, n_out):
    n, w = x.shape

    per_worker = n // SC_WORKERS
    assert per_worker % SC_ROWS == 0

    @functools.partial(pl.kernel, out_type=jax.ShapeDtypeStruct((n_out, w), x.dtype), mesh=_sc_mesh(),
                       scratch_types=[pltpu.VMEM((SC_ROWS,), jnp.int32), pltpu.VMEM((SC_ROWS, w), x.dtype)])
    def scatter(x_hbm, i_hbm, o_hbm, idx_v, rows_v):
        first = _sc_worker() * per_worker

        @pl.loop(0, per_worker // SC_ROWS)
        def _(i):
            base = pl.multiple_of(first + i * SC_ROWS, SC_ROWS)
            pltpu.sync_copy(i_hbm.at[pl.ds(base, SC_ROWS)], idx_v)
            pltpu.sync_copy(x_hbm.at[pl.ds(base, SC_ROWS)], rows_v)
            pltpu.sync_copy(rows_v, o_hbm.at[idx_v])

    return scatter(x, dest)


def _gather_rows(x, src):
    n = src.shape[0]
    w = x.shape[1]

    per_worker = n // SC_WORKERS
    assert per_worker % SC_ROWS == 0

    @functools.partial(pl.kernel, out_type=jax.ShapeDtypeStruct((n, w), x.dtype), mesh=_sc_mesh(),
                       scratch_types=[pltpu.VMEM((SC_ROWS,), jnp.int32), pltpu.VMEM((SC_ROWS, w), x.dtype)])
    def gather(x_hbm, i_hbm, o_hbm, idx_v, rows_v):
        first = _sc_worker() * per_worker

        @pl.loop(0, per_worker // SC_ROWS)
        def _(i):
            base = pl.multiple_of(first + i * SC_ROWS, SC_ROWS)
            pltpu.sync_copy(i_hbm.at[pl.ds(base, SC_ROWS)], idx_v)
            pltpu.sync_copy(x_hbm.at[idx_v], rows_v)
            pltpu.sync_copy(rows_v, o_hbm.at[pl.ds(base, SC_ROWS)])

    return gather(x, src)


def _routing_plan(cls, n_tiles):
    onehot = (cls[:, None] == jnp.arange(N_CLASSES, dtype=jnp.int32)[None, :]).astype(jnp.int32)
    counts = jnp.sum(onehot, axis=0)
    rank = jnp.sum((jnp.cumsum(onehot, axis=0) - onehot) * onehot, axis=1)
    tiles = (counts + MOE_TM - 1) // MOE_TM
    tile_end = jnp.cumsum(tiles)
    tile_start = tile_end - tiles
    dest = jnp.sum(onehot * tile_start[None, :], axis=1) * MOE_TM + rank
    nact = tile_end[-1]
    blk = jnp.minimum(jnp.arange(n_tiles, dtype=jnp.int32), nact - 1)
    tile_cls = jnp.sum((blk[:, None] >= tile_end[None, :]).astype(jnp.int32), axis=1)
    e1 = jnp.asarray(CLASS_E1)[tile_cls]
    e2 = jnp.asarray(CLASS_E2)[tile_cls]
    return dest.astype(jnp.int32), blk, e1, e2, nact.reshape(1).astype(jnp.int32)


def _row_tile(n):
    return 512 if n % 512 == 0 else 256


def kernel(x, c, ctx, c_ctx, w_ada, b_ada, norm1, w_in, pool_w, pool_scale, q_norm, k_norm, rpb,
           sg_w, sg_b, sg_norm, w_out, norm2, w_router, b_router, w_gate, w_up, w_down):
    depth = w_ada.shape[0]
    n = x.shape[1]
    lc = ctx.shape[1]
    bf = jnp.bfloat16
    lat_stream = (x[0],)
    h_ctx = ctx[0]

    cond = jnp.zeros((SUBLANES, D_MODEL), jnp.float32).at[0].set(c[0]).at[1].set(c_ctx)
    mod = _ada_call(cond, w_ada, b_ada)

    wr_t = w_router.T
    wr_hi = wr_t.astype(bf)
    wr_lo = (wr_t - wr_hi.astype(jnp.float32)).astype(bf)
    wr = jnp.concatenate([wr_hi, wr_lo], axis=0)
    br = b_router.reshape(N_EXPERTS, 1)
    n_tiles = n // MOE_TM + N_CLASSES
    w_in_bf = w_in.astype(bf)
    w_out_bf = w_out.astype(bf)

    for l in range(depth):
        last = l == depth - 1
        qg =(q_norm[l] * (HEAD_DIM ** -0.5 * LOG2_E)).reshape(1, NA_DIM)
        kg = k_norm[l].reshape(1, NA_DIM)
        sgn = sg_norm[l].reshape(1, SG_DIM)
        sgw = sg_w[l].astype(bf).reshape(SG_DIM // LANES, 2 * SG_CHUNK, SG_CHUNK)
        sgb = jnp.broadcast_to(sg_b[l].reshape(SG_DIM // LANES, 2 * SG_CHUNK, 1),
                               (SG_DIM // LANES, 2 * SG_CHUNK, LANES))
        pool_bd = jax.scipy.linalg.block_diag(*[pool_w[l, g] for g in range(len(POOL_WINDOWS))]).astype(bf)
        pscale = pool_scale[l].reshape(1, POOL_DIM)
        bias = _natten_bias(rpb[l])
        n1 = norm1[l].reshape(1, D_MODEL)
        n2 = norm2[l].reshape(1, D_MODEL)

        def mods(row):
            return [mod[l, row:row + 1, i * D_MODEL:(i + 1) * D_MODEL] for i in range(6)]

        sh1, sc1, g1, sh2, sc2, g2 = mods(0)
        csh1, csc1, cg1, csh2, csc2, cg2 = mods(1)

        tc = _row_tile(lc)
        mix_pool_c, q_c, k_c, v_c, sg_c = _inproj_call((h_ctx,), n1, csh1, csc1, w_in_bf, l, qg, kg,
                                                       sgn, sgw, sgb, pool_bd, pscale, tc, strips=False)

        tm = _row_tile(n)
        outs = _inproj_call(lat_stream, n1, sh1, sc1, w_in_bf, l, qg, kg, sgn, sgw, sgb,
                            pool_bd, pscale, tm, strips=True)
        mix_pool, q, k, v, sg = outs[:5]
        h_lat = outs[5] if len(lat_stream) == 3 else lat_stream[0]
        att, wg_l, wu_l, wd_l = _natten_call(q, k, v, k_c, v_c, bias, w_gate, w_up, w_down, l)
        h1, pay, route = _outproj_call(h_lat, mix_pool, att, sg, w_out_bf, l, g1, n2, sh2, sc2, wr, br,
                                       OUTPROJ_TILE if n % OUTPROJ_TILE == 0 else tm)
        cls = route[4].astype(jnp.int32)
        dest, blk, e1, e2, nact = _routing_plan(cls, n_tiles)
        pay_sorted = _scatter_rows(pay, dest, n_tiles * MOE_TM)
        y_sorted = _moe_sorted_call(blk, e1, e2, nact, pay_sorted, wg_l, wu_l, wd_l)
        y = _gather_rows(y_sorted, dest)
        lat_stream = (h1, y, g2)

        if not last:
            att_c = _ctxatt_call(q_c, k_c, v_c)
            h1_c, pay_c, _ = _outproj_call(h_ctx, mix_pool_c, att_c, sg_c, w_out_bf, l, cg1, n2, csh2, csc2,
                                           wr, br, tc)
            y_c = _moe_dense_call(pay_c, wg_l, wu_l, wd_l)
            h_ctx = _residual_call(h1_c, y_c, cg2, tc)

    return _residual_call(*lat_stream, _row_tile(n))[None]
```

```python
import functools

import jax
import jax.numpy as jnp
import numpy as np
from jax import lax
from jax.experimental import pallas as pl
from jax.experimental.pallas import tpu as pltpu
from jax.experimental.pallas import tpu_sc as plsc

D_MODEL = 1024
GRID_W = 64
HEAD_DIM = 64
POOL_WINDOWS = (2, 4, 8, 16)
POOL_DIM = 256
NA_HEADS = 8
NA_DIM = 512
NA_WIN_ROWS = 8
NA_WIN_COLS = 16
SG_DIM = 256
SG_CHUNK = 128
Q_OFF = POOL_DIM
K_OFF = Q_OFF + NA_DIM
V_OFF = K_OFF + NA_DIM
U_OFF = V_OFF + NA_DIM
G_OFF = U_OFF + SG_DIM
IN_DIM = G_OFF + SG_DIM
N_EXPERTS = 16
GROUP_SIZE = 4
D_EXPERT = 512
EPS = 1e-6

LANES = 128
SUBLANES = 8
HEAD_PAIRS = NA_DIM // LANES
VMEM_LIMIT = 48 * 1024 * 1024

PAIRS = ((0, 1), (0, 2), (1, 2), (1, 3), (0, 3), (2, 3))
N_CLASSES = (N_EXPERTS // GROUP_SIZE) * len(PAIRS)
CLASS_E1 = np.array([4 * g + i for g in range(4) for (i, j) in PAIRS], np.int32)
CLASS_E2 = np.array([4 * g + j for g in range(4) for (i, j) in PAIRS], np.int32)

ROUTE_ROWS = 8
HALF_D = D_MODEL // 2
PAYLOAD_W = HALF_D + LANES
MOE_TM = 512
NEG_BIG = -1e30
LOG2_E = 1.4426950408889634


def _cparams(sem):
    return pltpu.CompilerParams(dimension_semantics=sem, vmem_limit_bytes=VMEM_LIMIT)


def _dot(a, b):
    return jnp.dot(a, b, preferred_element_type=jnp.float32)


def _dot_nt(a, b):
    return lax.dot_general(a, b, (((1,), (1,)), ((), ())), preferred_element_type=jnp.float32)


def _gelu_tanh(x):
    return 0.5 * x * (1.0 + jnp.tanh(0.7978845608028654 * (x + 0.044715 * (x * x * x))))


def _silu(x):
    return x * (1.0 / (1.0 + jnp.exp(-x)))


def _pack_halves(x):
    w = x.shape[1] // 2
    lo = pltpu.bitcast(x[:, :w].astype(jnp.bfloat16).astype(jnp.float32), jnp.uint32) >> 16
    hi = pltpu.bitcast(x[:, w:].astype(jnp.bfloat16).astype(jnp.float32), jnp.uint32) & jnp.uint32(0xFFFF0000)
    return lo | hi


def _unpack_halves(words):
    lo = pltpu.bitcast(words << 16, jnp.float32)
    hi = pltpu.bitcast(words & jnp.uint32(0xFFFF0000), jnp.float32)
    return lo, hi


def _ada_kernel(cond_ref, w_ref, b_ref, o_ref):
    cond = _silu(cond_ref[...])
    o_ref[...] = jnp.dot(cond, w_ref[...], preferred_element_type=jnp.float32,
                         precision=lax.Precision.HIGHEST) + b_ref[...]


def _ada_call(cond, w_ada, b_ada):
    depth = w_ada.shape[0]
    tn = 1536
    return pl.pallas_call(
        _ada_kernel,
        grid=(depth, 6 * D_MODEL // tn),
        in_specs=[
            pl.BlockSpec((SUBLANES, D_MODEL), lambda l, j: (0, 0)),
            pl.BlockSpec((None, D_MODEL, tn), lambda l, j: (l, 0, j)),
            pl.BlockSpec((None, 1, tn), lambda l, j: (l, 0, j)),
        ],
        out_specs=pl.BlockSpec((None, SUBLANES, tn), lambda l, j: (l, 0, j)),
        out_shape=jax.ShapeDtypeStruct((depth, SUBLANES, 6 * D_MODEL), jnp.float32),
        compiler_params=_cparams(("arbitrary", "arbitrary")),
        name="adaln",
    )(cond, w_ada, b_ada.reshape(depth, 1, 6 * D_MODEL))


def _norm_modulate(x, n_ref, sh_ref, sc_ref):
    ms = jnp.mean(x * x, axis=-1, keepdims=True)
    return ((x * lax.rsqrt(ms + EPS)) * (n_ref[...] * (1.0 + sc_ref[...])) + sh_ref[...]).astype(jnp.bfloat16)


def _head_rms_scale(a):
    low = lax.broadcasted_iota(jnp.int32, (a.shape[0], LANES), 1) < HEAD_DIM
    blocks = []
    for p in range(a.shape[1] // LANES):
        sq = jnp.square(a[:, p * LANES:(p + 1) * LANES])
        s_lo = jnp.sum(jnp.where(low, sq, 0.0), axis=-1, keepdims=True)
        s_hi = jnp.sum(jnp.where(low, 0.0, sq), axis=-1, keepdims=True)
        blocks.append(jnp.where(low, lax.rsqrt(s_lo * (1.0 / HEAD_DIM) + EPS),
                                lax.rsqrt(s_hi * (1.0 / HEAD_DIM) + EPS)))
    return jnp.concatenate(blocks, axis=-1)


STRIP_W = 8
N_STRIPS = GRID_W // STRIP_W


def _store_keys(ref, x):
    if len(ref.shape) == 2:
        ref[...] = x.astype(jnp.bfloat16)
        return
    pair = 2 * STRIP_W
    for s in range(N_STRIPS):
        for rp in range(x.shape[0] // (2 * GRID_W)):
            top = 2 * rp * GRID_W + s * STRIP_W
            rows = jnp.concatenate([x[top:top + STRIP_W], x[top + GRID_W:top + GRID_W + STRIP_W]], axis=0)
            ref[s, rp * pair:(rp + 1) * pair, :] = rows.astype(jnp.bfloat16)


POOL_HALO = 8


POOL_EDGE = 16


def _pool_mix(xe_ref, w_ref, scale_ref, tm, seq_len):
    low = lax.broadcasted_iota(jnp.int32, (tm, LANES), 1) < HEAD_DIM
    t_edge = pl.program_id(0) * tm + lax.broadcasted_iota(jnp.int32, (POOL_EDGE, LANES), 0)

    def window_mean(s, half):
        mean = s * (1.0 / (2 * half))

        def clip_fix(t):
            count = (jnp.minimum(t + half, seq_len) - jnp.maximum(t - half, 0)).astype(jnp.float32)
            return (2.0 * half) / count

        return jnp.concatenate([mean[:POOL_EDGE] * clip_fix(t_edge),
                                mean[POOL_EDGE:tm - POOL_EDGE],
                                mean[tm - POOL_EDGE:] * clip_fix(t_edge + (tm - POOL_EDGE))], axis=0)

    def window_sums(xs, n_levels):
        sums = []
        s = xs
        for k in range(n_levels):
            step = 1 << k
            s = s[:-step] + s[step:]
            sums.append(s)
        return sums

    outs = []
    for half_block, windows in enumerate(((2, 4), (8, 16))):
        xs = xe_ref[:, half_block * LANES:(half_block + 1) * LANES]
        sums = window_sums(xs, int(np.log2(windows[1])))
        parts = []
        for w in windows:
            half = w // 2
            s = sums[int(np.log2(w)) - 1][POOL_HALO - half:POOL_HALO - half + tm]
            parts.append(window_mean(s, half))
        mean = jnp.where(low, parts[0], parts[1])
        outs.append(mean - xs[POOL_HALO:POOL_HALO + tm])
    d = jnp.concatenate(outs, axis=-1).astype(jnp.bfloat16)
    return (_dot(d, w_ref[...]) * scale_ref[...]).astype(jnp.bfloat16)


def _inproj_kernel(*refs, pending, seq_len):
    n_stream = 7 if pending else 3
    stream, refs = refs[:n_stream], refs[n_stream:]
    if pending:
        h_ref, y_ref, g_ref, hp_ref, hn_ref, yp_ref, yn_ref = stream
        hres_ref, refs = refs[-2], refs[:-2] + refs[-1:]
        x = h_ref[...] + g_ref[...] * _moe_out(y_ref)
        hres_ref[...] = x
        x_halo = jnp.concatenate([hp_ref[...] + g_ref[...] * _moe_out(yp_ref),
                                  hn_ref[...] + g_ref[...] * _moe_out(yn_ref)], axis=0)
    else:
        h_ref, hp_ref, hn_ref = stream
        x = h_ref[...]
        x_halo = jnp.concatenate([hp_ref[...], hn_ref[...]], axis=0)
    (n1_ref, sh_ref, sc_ref, w_ref, qg_ref, kg_ref, sgn_ref, sgw_ref, sgb_ref,
     pw_ref, ps_ref, pool_ref, q_ref, k_ref, v_ref, sg_ref, xe_ref) = refs
    tm = h_ref.shape[0]
    i = pl.program_id(0)
    hn = _norm_modulate(x, n1_ref, sh_ref, sc_ref)
    hn_halo = _norm_modulate(x_halo, n1_ref, sh_ref, sc_ref)

    a_halo = _dot(hn_halo, w_ref[:, 0:Q_OFF])
    xe_ref[0:POOL_HALO, :] = jnp.where(i > 0, a_halo[:POOL_HALO], 0.0)
    xe_ref[POOL_HALO:POOL_HALO + tm, :] = _dot(hn, w_ref[:, 0:Q_OFF])
    xe_ref[POOL_HALO + tm:, :] = jnp.where(i < pl.num_programs(0) - 1, a_halo[POOL_HALO:], 0.0)

    a_g = _dot(hn, w_ref[:, G_OFF:IN_DIM])
    a_u = _dot(hn, w_ref[:, U_OFF:G_OFF])
    a_q = _dot(hn, w_ref[:, Q_OFF:K_OFF])
    a_k = _dot(hn, w_ref[:, K_OFF:V_OFF])
    _store_keys(v_ref, _dot(hn, w_ref[:, V_OFF:U_OFF]))

    gv = _gelu_tanh(a_g)
    q_ref[...] = (a_q * _head_rms_scale(a_q) * qg_ref[...]).astype(jnp.bfloat16)
    _store_keys(k_ref, a_k * _head_rms_scale(a_k) * kg_ref[...])

    u = _gelu_tanh(a_u)
    vn = (gv * _head_rms_scale(gv) * sgn_ref[...]).astype(jnp.bfloat16)
    low = lax.broadcasted_iota(jnp.int32, (SG_CHUNK, LANES), 1) < HEAD_DIM
    for c in range(tm // SG_CHUNK):
        rows = slice(c * SG_CHUNK, (c + 1) * SG_CHUNK)
        for s in range(SG_DIM // LANES):
            cols = slice(s * LANES, (s + 1) * LANES)
            m = _dot(sgw_ref[s], vn[rows, cols]) + sgb_ref[s]
            mixed = jnp.where(low, m[:SG_CHUNK], m[SG_CHUNK:])
            sg_ref[rows, cols] = (u[rows, cols] * mixed).astype(jnp.bfloat16)

    pool_ref[...] = _pool_mix(xe_ref, pw_ref, ps_ref, tm, seq_len)


def _inproj_call(stream, n1, sh, sc, w_in, layer, qg, kg, sgn, sgw, sgb, pool_w, pool_scale, tm, strips):
    pending = len(stream) == 3
    n = stream[0].shape[0]
    if strips:
        kv_spec = pl.BlockSpec((N_STRIPS, tm // N_STRIPS, NA_DIM), lambda i: (0, i, 0))
        kv_shape = jax.ShapeDtypeStruct((N_STRIPS, n // N_STRIPS, NA_DIM), jnp.bfloat16)
    else:
        kv_spec = pl.BlockSpec((tm, NA_DIM), lambda i: (i, 0))
        kv_shape = jax.ShapeDtypeStruct((n, NA_DIM), jnp.bfloat16)
    row = lambda i: (i, 0)
    fixed2 = lambda i: (0, 0)
    fixed3 = lambda i: (0, 0, 0)
    vec = lambda w: pl.BlockSpec((1, w), fixed2)
    rows = pl.BlockSpec((tm, D_MODEL), row)
    r8 = tm // POOL_HALO
    before = lambda i: (jnp.maximum(i * r8 - 1, 0), 0)
    after = lambda i: (jnp.minimum((i + 1) * r8, n // POOL_HALO - 1), 0)
    halo = lambda w, m: pl.BlockSpec((POOL_HALO, w), m)
    h = stream[0]
    if pending:
        y, g = stream[1], stream[2]
        yw = y.shape[1]
        args = [h, y, g, h, h, y, y]
        stream_specs = [rows, pl.BlockSpec((tm, yw), row), vec(D_MODEL),
                        halo(D_MODEL, before), halo(D_MODEL, after), halo(yw, before), halo(yw, after)]
    else:
        args = [h, h, h]
        stream_specs = [rows, halo(D_MODEL, before), halo(D_MODEL, after)]
    extra_out_specs = [rows] if pending else []
    extra_out_shape = [jax.ShapeDtypeStruct((n, D_MODEL), jnp.float32)] if pending else []
    return pl.pallas_call(
        functools.partial(_inproj_kernel, pending=pending, seq_len=n),
        grid=(n // tm,),
        in_specs=stream_specs + [
            vec(D_MODEL), vec(D_MODEL), vec(D_MODEL),
            pl.BlockSpec((None, D_MODEL, IN_DIM), lambda i: (layer, 0, 0)),
            vec(NA_DIM), vec(NA_DIM),
            vec(SG_DIM),
            pl.BlockSpec((SG_DIM // LANES, 2 * SG_CHUNK, SG_CHUNK), fixed3),
            pl.BlockSpec((SG_DIM // LANES, 2 * SG_CHUNK, LANES), fixed3),
            pl.BlockSpec((POOL_DIM, POOL_DIM), fixed2),
            vec(POOL_DIM),
        ],
        out_specs=[
            pl.BlockSpec((tm, POOL_DIM), row),
            pl.BlockSpec((tm, NA_DIM), row),
            kv_spec,
            kv_spec,
            pl.BlockSpec((tm, SG_DIM), row),
        ] + extra_out_specs,
        out_shape=[
            jax.ShapeDtypeStruct((n, POOL_DIM), jnp.bfloat16),
            jax.ShapeDtypeStruct((n, NA_DIM), jnp.bfloat16),
            kv_shape,
            kv_shape,
            jax.ShapeDtypeStruct((n, SG_DIM), jnp.bfloat16),
        ] + extra_out_shape,
        scratch_shapes=[pltpu.VMEM((tm + 2 * POOL_HALO, POOL_DIM), jnp.float32)],
        compiler_params=_cparams(("arbitrary",)),
        name="inproj",
    )(*args, n1, sh, sc, w_in, qg, kg, sgn, sgw, sgb, pool_w, pool_scale)


NA_ROWS_PER_BLOCK = 64
NA_GROUP_ROWS = 4
NA_WINDOW_ROWS = NA_GROUP_ROWS + NA_WIN_ROWS
NA_BLOCK = NA_ROWS_PER_BLOCK * GRID_W
NA_GROUP = NA_GROUP_ROWS * GRID_W
NA_HALF_COLS = GRID_W // 2
NA_CHAIN = NA_GROUP_ROWS * NA_HALF_COLS
NA_HALF_STRIPS = 5
NA_HALF_COL0 = (0, GRID_W - NA_HALF_STRIPS * STRIP_W)
NA_RUN = NA_WINDOW_ROWS * STRIP_W
NA_LOCAL = 512
NA_STRIP_BLOCK = NA_ROWS_PER_BLOCK * STRIP_W
NA_STRIP_HALO = (NA_WIN_ROWS // 2) * STRIP_W
NA_EDGE_FIRST, NA_EDGE_NONE, NA_EDGE_LAST = 0, 1, 2


def _stack_heads(x, low):
    zero = jnp.zeros_like(x)
    return jnp.concatenate([jnp.where(low, x, zero), jnp.where(low, zero, x)], axis=0)


def _natten_kernel(q_ref, kp_ref, kc_ref, kn_ref, vp_ref, vc_ref, vn_ref, kx_ref, vx_ref, bias_ref,
                   wg_ref, wu_ref, wd_ref,
                   o_ref, wg_bf_ref, wu_bf_ref, wd_bf_ref, kwin_ref, vwin_ref, vxe_ref, *, grid_rows):
    b = pl.program_id(1)
    wg_bf_ref[...] = wg_ref[...].astype(jnp.bfloat16)
    wu_bf_ref[...] = wu_ref[...].astype(jnp.bfloat16)
    wd_bf_ref[...] = wd_ref[...].astype(jnp.bfloat16)
    top, bottom = NA_STRIP_HALO, NA_STRIP_HALO + NA_STRIP_BLOCK
    kwin_ref[:, 0:top, :] = kp_ref[...]
    kwin_ref[:, top:bottom, :] = kc_ref[...]
    kwin_ref[:, bottom:, :] = kn_ref[...]
    vwin_ref[:, 0:top, 0:LANES] = vp_ref[...]
    vwin_ref[:, top:bottom, 0:LANES] = vc_ref[...]
    vwin_ref[:, bottom:, 0:LANES] = vn_ref[...]
    vwin_ref[:, :, LANES:] = jnp.ones(vwin_ref.shape[:2] + (LANES,), jnp.bfloat16)
    vxe_ref[:, 0:LANES] = vx_ref[...]
    vxe_ref[:, LANES:] = jnp.ones((vxe_ref.shape[0], LANES), jnp.bfloat16)
    low_q = lax.broadcasted_iota(jnp.int32, (NA_CHAIN, LANES), 1) < HEAD_DIM
    n_pad = NA_LOCAL - NA_HALF_STRIPS * NA_RUN

    def window_start(g):
        r0 = b * NA_ROWS_PER_BLOCK + g * NA_GROUP_ROWS
        ws = jnp.clip(r0 - NA_WIN_ROWS // 2, 0, grid_rows - NA_WINDOW_ROWS)
        edge = jnp.where(r0 == 0, NA_EDGE_FIRST,
                         jnp.where(r0 == grid_rows - NA_GROUP_ROWS, NA_EDGE_LAST, NA_EDGE_NONE))
        start = pl.multiple_of((ws - b * NA_ROWS_PER_BLOCK + NA_WIN_ROWS // 2) * STRIP_W, NA_STRIP_HALO)
        return start, edge

    def local_window(win_ref, g, half):
        start, _ = window_start(g)
        s0 = NA_HALF_COL0[half] // STRIP_W
        runs = [win_ref[s, pl.ds(start, NA_RUN), :] for s in range(s0, s0 + NA_HALF_STRIPS)]
        return jnp.concatenate(runs + [jnp.zeros((n_pad, win_ref.shape[2]), jnp.bfloat16)], axis=0)

    def query_rows(g, half, j):
        first = g * NA_GROUP + j * GRID_W + half * NA_HALF_COLS
        return slice(first, first + NA_HALF_COLS)

    def scores(c):
        g, half = divmod(c, 2)
        _, edge = window_start(g)
        qh = jnp.concatenate([q_ref[query_rows(g, half, j), :] for j in range(NA_GROUP_ROWS)], axis=0)
        lhs = _stack_heads(qh, low_q)
        kl = local_window(kwin_ref, g, half)
        return jnp.concatenate([_dot_nt(lhs, kl).astype(jnp.bfloat16) + bias_ref[edge, half],
                                _dot_nt(lhs, kx_ref[...]).astype(jnp.bfloat16)], axis=-1)

    n_chains = 2 * (NA_ROWS_PER_BLOCK // NA_GROUP_ROWS)
    s_next = scores(0)
    for c in range(n_chains):
        s = s_next
        if c + 1 < n_chains:
            s_next = scores(c + 1)
        g, half = divmod(c, 2)
        vl = local_window(vwin_ref, g, half)
        m = jnp.max(s, axis=-1, keepdims=True)
        pb = jnp.exp2(s - m)
        o = _dot(pb[:, :NA_LOCAL], vl) + _dot(pb[:, NA_LOCAL:], vxe_ref[...])
        o = o[:, :LANES] * (1.0 / o[:, LANES:])
        o = jnp.where(low_q, o[:NA_CHAIN], o[NA_CHAIN:]).astype(jnp.bfloat16)
        for j in range(NA_GROUP_ROWS):
            o_ref[query_rows(g, half, j), :] = o[j * NA_HALF_COLS:(j + 1) * NA_HALF_COLS]


def _natten_call(q, k, v, k_ctx, v_ctx, bias, w_gate, w_up, w_down, layer):
    n = q.shape[0]
    grid_rows = n // GRID_W
    assert grid_rows % NA_ROWS_PER_BLOCK == 0 and grid_rows >= 2 * NA_ROWS_PER_BLOCK
    nblk = n // NA_BLOCK
    steps = HEAD_PAIRS * nblk
    depth = w_gate.shape[0]
    up_rows = N_EXPERTS * D_MODEL
    down_rows = N_EXPERTS * D_EXPERT
    assert up_rows % steps == 0 and down_rows % steps == 0
    wg2 = w_gate.reshape(depth * up_rows, D_EXPERT)
    wu2 = w_up.reshape(depth * up_rows, D_EXPERT)
    wd2 = w_down.reshape(depth * down_rows, D_MODEL)
    up_in = pl.BlockSpec((up_rows // steps, D_EXPERT), lambda p, b: (layer * steps + p * nblk + b, 0))
    down_in = pl.BlockSpec((down_rows // steps, D_MODEL), lambda p, b: (layer * steps + p * nblk + b, 0))
    up_out = pl.BlockSpec((up_rows // steps, D_EXPERT), lambda p, b: (p * nblk + b, 0))
    down_out = pl.BlockSpec((down_rows // steps, D_MODEL), lambda p, b: (p * nblk + b, 0))
    n_halo = n // N_STRIPS // NA_STRIP_HALO
    hb = NA_STRIP_BLOCK // NA_STRIP_HALO
    rows = pl.BlockSpec((NA_BLOCK, LANES), lambda p, b: (b, p))
    cur = pl.BlockSpec((N_STRIPS, NA_STRIP_BLOCK, LANES), lambda p, b: (0, b, p))
    prev = pl.BlockSpec((N_STRIPS, NA_STRIP_HALO, LANES), lambda p, b: (0, jnp.maximum(b * hb - 1, 0), p))
    nxt = pl.BlockSpec((N_STRIPS, NA_STRIP_HALO, LANES),
                       lambda p, b: (0, jnp.minimum((b + 1) * hb, n_halo - 1), p))
    ctx = pl.BlockSpec((k_ctx.shape[0], LANES), lambda p, b: (0, p))
    win_rows = NA_STRIP_BLOCK + 2 * NA_STRIP_HALO
    att, wg_bf, wu_bf, wd_bf = pl.pallas_call(
        functools.partial(_natten_kernel, grid_rows=grid_rows),
        grid=(HEAD_PAIRS, nblk),
        in_specs=[rows, prev, cur, nxt, prev, cur, nxt, ctx, ctx,
                  pl.BlockSpec((None, 3, 2, 2 * NA_CHAIN, NA_LOCAL), lambda p, b: (p, 0, 0, 0, 0)),
                  up_in, up_in, down_in],
        out_specs=[rows, up_out, up_out, down_out],
        out_shape=[jax.ShapeDtypeStruct((n, NA_DIM), jnp.bfloat16),
                   jax.ShapeDtypeStruct((up_rows, D_EXPERT), jnp.bfloat16),
                   jax.ShapeDtypeStruct((up_rows, D_EXPERT), jnp.bfloat16),
                   jax.ShapeDtypeStruct((down_rows, D_MODEL), jnp.bfloat16)],
        scratch_shapes=[pltpu.VMEM((N_STRIPS, win_rows, LANES), jnp.bfloat16),
                        pltpu.VMEM((N_STRIPS, win_rows, 2 * LANES), jnp.bfloat16),
                        pltpu.VMEM((k_ctx.shape[0], 2 * LANES), jnp.bfloat16)],
        compiler_params=_cparams(("arbitrary", "arbitrary")),
        name="natten",
    )(q, k, k, k, v, v, v, k_ctx, v_ctx, bias, wg2, wu2, wd2)
    return (att, wg_bf.reshape(N_EXPERTS, D_MODEL, D_EXPERT), wu_bf.reshape(N_EXPERTS, D_MODEL, D_EXPERT),
            wd_bf.reshape(N_EXPERTS, D_EXPERT, D_MODEL))


def _natten_bias(rpb):
    cols = np.arange(GRID_W)
    col_start = np.clip(cols - NA_WIN_COLS // 2, 0, GRID_W - NA_WIN_COLS)
    kc = np.arange(GRID_W)
    in_win = (kc[None, :] >= col_start[:, None]) & (kc[None, :] < col_start[:, None] + NA_WIN_COLS)
    dc = kc[None, :] - cols[:, None] + NA_WIN_COLS - 1
    sel = (np.arange(2 * NA_WIN_COLS - 1)[:, None, None] == dc[None]) & in_win[None]
    t2 = jnp.einsum("hdj,jqk->hdqk", rpb, jnp.asarray(sel, jnp.float32), precision=lax.Precision.HIGHEST)
    t2 = jnp.where(in_win[None, None], t2 * LOG2_E, NEG_BIG)
    neg = jnp.full((NA_HEADS, 1, GRID_W, GRID_W), NEG_BIG, jnp.float32)
    t2e = jnp.concatenate([neg, t2, neg], axis=1)
    u = jnp.concatenate([t2e[:, :-1], t2e[:, 1:]], axis=-1)
    u = u.reshape(HEAD_PAIRS, 2, 2 * NA_WIN_ROWS, GRID_W, LANES)

    place = np.zeros((2, NA_WINDOW_ROWS * GRID_W, NA_LOCAL), np.float32)
    for half, c0 in enumerate(NA_HALF_COL0):
        for a in range(NA_WINDOW_ROWS):
            for kcol in range(c0, c0 + NA_HALF_STRIPS * STRIP_W):
                s, c8 = divmod(kcol - c0, STRIP_W)
                place[half, a * GRID_W + kcol, s * NA_RUN + a * STRIP_W + c8] = 1.0
    outside = np.full((3, NA_GROUP_ROWS, NA_LOCAL), NEG_BIG, np.float32)
    for edge in (NA_EDGE_FIRST, NA_EDGE_NONE, NA_EDGE_LAST):
        for j in range(NA_GROUP_ROWS):
            lo, _ = _window_rows(edge, j)
            for s in range(NA_HALF_STRIPS):
                outside[edge, j, s * NA_RUN + lo * STRIP_W:s * NA_RUN + (lo + NA_WIN_ROWS) * STRIP_W] = 0.0
    return pl.pallas_call(
        _bias_expand_kernel,
        grid=(HEAD_PAIRS,),
        in_specs=[pl.BlockSpec((None, 2, 2 * NA_WIN_ROWS, GRID_W, LANES), lambda p: (p, 0, 0, 0, 0)),
                  pl.BlockSpec(place.shape, lambda p: (0, 0, 0)),
                  pl.BlockSpec(outside.shape, lambda p: (0, 0, 0))],
        out_specs=pl.BlockSpec((None, 3, 2, 2 * NA_CHAIN, NA_LOCAL), lambda p: (p, 0, 0, 0, 0)),
        out_shape=jax.ShapeDtypeStruct((HEAD_PAIRS, 3, 2, 2 * NA_CHAIN, NA_LOCAL), jnp.bfloat16),
        compiler_params=_cparams(("arbitrary",)),
        name="bias_expand",
    )(u, jnp.asarray(place, jnp.bfloat16), jnp.asarray(outside))


def _window_rows(edge, j):
    if edge == NA_EDGE_FIRST:
        return 0, NA_WIN_ROWS - 1 - j
    if edge == NA_EDGE_NONE:
        return j, NA_WIN_ROWS // 2 - 1
    return NA_WINDOW_ROWS - NA_WIN_ROWS, NA_WIN_ROWS // 2 - 1 - j


def _bias_expand_kernel(u_ref, place_ref, outside_ref, o_ref):
    low = lax.broadcasted_iota(jnp.int32, (NA_HALF_COLS, LANES), 1) < GRID_W
    zero = jnp.zeros((NA_HALF_COLS, LANES), jnp.float32)
    for edge in (NA_EDGE_FIRST, NA_EDGE_NONE, NA_EDGE_LAST):
        for half in range(2):
            q0 = half * NA_HALF_COLS
            blocks, masks = [], []
            for hd in range(2):
                for j in range(NA_GROUP_ROWS):
                    lo, base = _window_rows(edge, j)
                    tiles = []
                    for i in range(NA_WINDOW_ROWS // 2):
                        a0, a1 = 2 * i, 2 * i + 1
                        ok0 = lo <= a0 < lo + NA_WIN_ROWS
                        ok1 = lo <= a1 < lo + NA_WIN_ROWS
                        if not (ok0 or ok1):
                            tile = zero
                        else:
                            tile = u_ref[hd, base + a1 - lo, q0:q0 + NA_HALF_COLS, :]
                            if not ok0:
                                tile = jnp.where(low, zero, tile)
                            if not ok1:
                                tile = jnp.where(low, tile, zero)
                        tiles.append(tile)
                    blocks.append(jnp.concatenate(tiles, axis=-1))
                    masks.append(jnp.broadcast_to(outside_ref[edge, j:j + 1, :], (NA_HALF_COLS, NA_LOCAL)))
            lhs = jnp.concatenate(blocks, axis=0).astype(jnp.bfloat16)
            placed = _dot(lhs, place_ref[half]) + jnp.concatenate(masks, axis=0)
            o_ref[edge, half] = placed.astype(o_ref.dtype)


def _ctxatt_kernel(q_ref, k_ref, v_ref, o_ref):
    lc = q_ref.shape[0]
    low = lax.broadcasted_iota(jnp.int32, (lc, LANES), 1) < HEAD_DIM
    lhs = _stack_heads(q_ref[...], low)
    s = _dot_nt(lhs, k_ref[...])
    m = jnp.max(s, axis=-1, keepdims=True)
    p = jnp.exp2(s - m)
    denom = jnp.sum(p, axis=-1, keepdims=True)
    o = _dot(p.astype(jnp.bfloat16), v_ref[...]) * (1.0 / denom)
    o_ref[...] = jnp.where(low, o[:lc], o[lc:]).astype(jnp.bfloat16)


def _ctxatt_call(q, k, v):
    lc = q.shape[0]
    spec = pl.BlockSpec((lc, LANES), lambda p: (0, p))
    return pl.pallas_call(
        _ctxatt_kernel,
        grid=(HEAD_PAIRS,),
        in_specs=[spec, spec, spec],
        out_specs=spec,
        out_shape=jax.ShapeDtypeStruct((lc, NA_DIM), jnp.bfloat16),
        compiler_params=_cparams(("arbitrary",)),
        name="ctxatt",
    )(q, k, v)


OUTPROJ_CHAIN = 256
OUTPROJ_TILE = 1024


def _outproj_kernel(h_ref, mp_ref, att_ref, sg_ref, wo_ref, g1_ref, n2_ref, sh_ref, sc_ref,
                    wr_ref, br_ref, h1_ref, pay_ref, route_ref):
    chains = [slice(c * OUTPROJ_CHAIN, (c + 1) * OUTPROJ_CHAIN) for c in range(h_ref.shape[0] // OUTPROJ_CHAIN)]
    h1s = []
    for rows in chains:
        mix = (_dot(mp_ref[rows, :], wo_ref[0:POOL_DIM, :])
               + _dot(att_ref[rows, :], wo_ref[POOL_DIM:POOL_DIM + NA_DIM, :])
               + _dot(sg_ref[rows, :], wo_ref[POOL_DIM + NA_DIM:, :]))
        h1 = h_ref[rows, :] + g1_ref[...] * mix
        h1_ref[rows, :] = h1
        h1s.append(h1)
    for rows, h1 in zip(chains, h1s):
        _outproj_route(rows, h1, n2_ref, sh_ref, sc_ref, wr_ref, br_ref, pay_ref, route_ref)


def _outproj_route(rows, h1, n2_ref, sh_ref, sc_ref, wr_ref, br_ref, pay_ref, route_ref):
    tm = OUTPROJ_CHAIN
    ms = jnp.mean(h1 * h1, axis=-1, keepdims=True)
    hm = (h1 * lax.rsqrt(ms + EPS)) * (n2_ref[...] * (1.0 + sc_ref[...])) + sh_ref[...]
    pay_ref[rows, 0:HALF_D] = _pack_halves(hm)

    hm_hi = hm.astype(jnp.bfloat16)
    lt = _dot_nt(wr_ref[...], hm_hi)
    logits = lt[:N_EXPERTS] + lt[N_EXPERTS:] + br_ref[...]
    e = jnp.exp(logits - jnp.max(logits, axis=0, keepdims=True))

    best = ga = gb = e1 = e2 = cls = None
    for c in range(N_CLASSES):
        a, b2 = int(CLASS_E1[c]), int(CLASS_E2[c])
        ea, eb = e[a:a + 1, :], e[b2:b2 + 1, :]
        s = ea + eb
        if best is None:
            best, ga, gb = s, ea, eb
            e1 = jnp.full_like(s, float(a))
            e2 = jnp.full_like(s, float(b2))
            cls = jnp.zeros_like(s)
        else:
            better = s > best
            best = jnp.where(better, s, best)
            ga = jnp.where(better, ea, ga)
            gb = jnp.where(better, eb, gb)
            e1 = jnp.where(better, float(a), e1)
            e2 = jnp.where(better, float(b2), e2)
            cls = jnp.where(better, float(c), cls)
    inv = 1.0 / best
    row = lax.broadcasted_iota(jnp.int32, (ROUTE_ROWS, tm), 0)
    rec = jnp.where(row == 0, ga * inv,
          jnp.where(row == 1, gb * inv,
          jnp.where(row == 2, e1,
          jnp.where(row == 3, e2,
          jnp.where(row == 4, cls, 0.0)))))
    route_ref[:, rows] = rec
    wide = jnp.concatenate([rec, jnp.zeros((LANES - ROUTE_ROWS, tm), jnp.float32)], axis=0)
    pay_ref[rows, HALF_D:] = pltpu.bitcast(wide.T, jnp.uint32)


def _outproj_call(h, mp, att, sg, w_out, layer, g1, n2, sh2, sc2, wr, br, tm):
    n = h.shape[0]
    row = lambda i: (i, 0)
    fixed = lambda i: (0, 0)
    vec = pl.BlockSpec((1, D_MODEL), fixed)
    return pl.pallas_call(
        _outproj_kernel,
        grid=(n // tm,),
        in_specs=[
            pl.BlockSpec((tm, D_MODEL), row),
            pl.BlockSpec((tm, POOL_DIM), row),
            pl.BlockSpec((tm, NA_DIM), row),
            pl.BlockSpec((tm, SG_DIM), row),
            pl.BlockSpec((None, D_MODEL, D_MODEL), lambda i: (layer, 0, 0)),
            vec, vec, vec, vec,
            pl.BlockSpec((2 * N_EXPERTS, D_MODEL), fixed),
            pl.BlockSpec((N_EXPERTS, 1), fixed),
        ],
        out_specs=[
            pl.BlockSpec((tm, D_MODEL), row),
            pl.BlockSpec((tm, PAYLOAD_W), row),
            pl.BlockSpec((ROUTE_ROWS, tm), lambda i: (0, i)),
        ],
        out_shape=[
            jax.ShapeDtypeStruct((n, D_MODEL), jnp.float32),
            jax.ShapeDtypeStruct((n, PAYLOAD_W), jnp.uint32),
            jax.ShapeDtypeStruct((ROUTE_ROWS, n), jnp.float32),
        ],
        compiler_params=_cparams(("arbitrary",)),
        name="outproj",
    )(h, mp, att, sg, w_out, g1, n2, sh2, sc2, wr, br)


def _payload_parts(pay_ref, rows=slice(None)):
    lo, hi = _unpack_halves(pay_ref[rows, 0:HALF_D])
    x = jnp.concatenate([lo, hi], axis=-1).astype(jnp.bfloat16)
    return x, pltpu.bitcast(pay_ref[rows, HALF_D:], jnp.float32)


def _expert_pair(x, ga, gb, wga, wua, wda, wgb, wub, wdb):
    ha = (_silu(_dot(x, wga)) * _dot(x, wua) * ga).astype(jnp.bfloat16)
    hb = (_silu(_dot(x, wgb)) * _dot(x, wub) * gb).astype(jnp.bfloat16)
    return _dot(ha, wda) + _dot(hb, wdb)


MOE_SKIP, MOE_HALF_TILE, MOE_FULL_TILE = 0, 1, 2


def _moe_sorted_kernel(blk_ref, e1_ref, e2_ref, mode_ref, pay_ref,
                       wga_ref, wua_ref, wda_ref, wgb_ref, wub_ref, wdb_ref, o_ref):
    mode = mode_ref[pl.program_id(0)]

    def run(rows):
        x, route = _payload_parts(pay_ref, rows)
        y = _expert_pair(x, route[:, 0:1], route[:, 1:2], wga_ref[...], wua_ref[...], wda_ref[...],
                         wgb_ref[...], wub_ref[...], wdb_ref[...])
        o_ref[rows, :] = _pack_halves(y)

    @pl.when(mode == MOE_FULL_TILE)
    def _():
        run(slice(0, MOE_TM))

    @pl.when(mode == MOE_HALF_TILE)
    def _():
        run(slice(0, MOE_TM // 2))


def _moe_sorted_call(blk, e1, e2, mode, pay_sorted, wg, wu, wd):
    n_tiles = blk.shape[0]
    rows = lambda i, blk, e1, e2, mode: (blk[i], 0)
    wa = lambda i, blk, e1, e2, mode: (e1[i], 0, 0)
    wb = lambda i, blk, e1, e2, mode: (e2[i], 0, 0)
    up = lambda m: pl.BlockSpec((None, D_MODEL, D_EXPERT), m)
    down = lambda m: pl.BlockSpec((None, D_EXPERT, D_MODEL), m)
    return pl.pallas_call(
        _moe_sorted_kernel,
        grid_spec=pltpu.PrefetchScalarGridSpec(
            num_scalar_prefetch=4,
            grid=(n_tiles,),
            in_specs=[pl.BlockSpec((MOE_TM, PAYLOAD_W), rows),
                      up(wa), up(wa), down(wa), up(wb), up(wb), down(wb)],
            out_specs=pl.BlockSpec((MOE_TM, HALF_D), rows),
        ),
        out_shape=jax.ShapeDtypeStruct((n_tiles * MOE_TM, HALF_D), jnp.uint32),
        compiler_params=_cparams(("arbitrary",)),
        name="moe_sorted",
    )(blk, e1, e2, mode, pay_sorted, wg, wu, wd, wg, wu, wd)


def _moe_dense_kernel(pay_ref, wg_ref, wu_ref, wd_ref, o_ref):
    e = pl.program_id(0)

    @pl.when(e == 0)
    def _():
        o_ref[...] = jnp.zeros_like(o_ref)

    x, route = _payload_parts(pay_ref)
    ef = e.astype(jnp.float32)
    gate = (jnp.where(route[:, 2:3] == ef, route[:, 0:1], 0.0)
            + jnp.where(route[:, 3:4] == ef, route[:, 1:2], 0.0))
    he = (_silu(_dot(x, wg_ref[...])) * _dot(x, wu_ref[...]) * gate).astype(jnp.bfloat16)
    o_ref[...] += _dot(he, wd_ref[...])


def _moe_dense_call(pay, wg, wu, wd):
    n = pay.shape[0]
    return pl.pallas_call(
        _moe_dense_kernel,
        grid=(N_EXPERTS,),
        in_specs=[pl.BlockSpec((n, PAYLOAD_W), lambda e: (0, 0)),
                  pl.BlockSpec((None, D_MODEL, D_EXPERT), lambda e: (e, 0, 0)),
                  pl.BlockSpec((None, D_MODEL, D_EXPERT), lambda e: (e, 0, 0)),
                  pl.BlockSpec((None, D_EXPERT, D_MODEL), lambda e: (e, 0, 0))],
        out_specs=pl.BlockSpec((n, D_MODEL), lambda e: (0, 0)),
        out_shape=jax.ShapeDtypeStruct((n, D_MODEL), jnp.float32),
        compiler_params=_cparams(("arbitrary",)),
        name="moe_dense",
    )(pay, wg, wu, wd)


def _moe_out(y_ref):
    if y_ref.dtype == jnp.uint32:
        return jnp.concatenate(_unpack_halves(y_ref[...]), axis=-1)
    return y_ref[...]


def _residual_kernel(h_ref, y_ref, g_ref, o_ref):
    o_ref[...] = h_ref[...] + g_ref[...] * _moe_out(y_ref)


def _residual_call(h, y, g, tm):
    n = h.shape[0]
    row = pl.BlockSpec((tm, D_MODEL), lambda i: (i, 0))
    return pl.pallas_call(
        _residual_kernel,
        grid=(n // tm,),
        in_specs=[row, pl.BlockSpec((tm, y.shape[1]), lambda i: (i, 0)),
                  pl.BlockSpec((1, D_MODEL), lambda i: (0, 0))],
        out_specs=row,
        out_shape=jax.ShapeDtypeStruct((n, D_MODEL), jnp.float32),
        compiler_params=_cparams(("arbitrary",)),
        name="residual",
    )(h, y, g)


SC_ROWS = 32


SC_CORES = 2
SC_SUBCORES = 16
SC_WORKERS = SC_CORES * SC_SUBCORES


def _sc_mesh():
    return plsc.VectorSubcoreMesh(core_axis_name="core", subcore_axis_name="subcore")


def _sc_worker():
    return lax.axis_index("subcore") * SC_CORES + lax.axis_index("core")


def _scatter_rows(x, dest, n_out):
    n, w = x.shape

    per_worker = n // SC_WORKERS
    assert per_worker % SC_ROWS == 0

    @functools.partial(pl.kernel, out_type=jax.ShapeDtypeStruct((n_out, w), x.dtype), mesh=_sc_mesh(),
                       scratch_types=[pltpu.VMEM((SC_ROWS,), jnp.int32), pltpu.VMEM((SC_ROWS, w), x.dtype)])
    def scatter(x_hbm, i_hbm, o_hbm, idx_v, rows_v):
        first = _sc_worker() * per_worker

        @pl.loop(0, per_worker // SC_ROWS)
        def _(i):
            base = pl.multiple_of(first + i * SC_ROWS, SC_ROWS)
            pltpu.sync_copy(i_hbm.at[pl.ds(base, SC_ROWS)], idx_v)
            pltpu.sync_copy(x_hbm.at[pl.ds(base, SC_ROWS)], rows_v)
            pltpu.sync_copy(rows_v, o_hbm.at[idx_v])

    return scatter(x, dest)


def _gather_rows(x, src):
    n = src.shape[0]
    w = x.shape[1]

    per_worker = n // SC_WORKERS
    assert per_worker % SC_ROWS == 0

    @functools.partial(pl.kernel, out_type=jax.ShapeDtypeStruct((n, w), x.dtype), mesh=_sc_mesh(),
                       scratch_types=[pltpu.VMEM((SC_ROWS,), jnp.int32), pltpu.VMEM((SC_ROWS, w), x.dtype)])
    def gather(x_hbm, i_hbm, o_hbm, idx_v, rows_v):
        first = _sc_worker() * per_worker

        @pl.loop(0, per_worker // SC_ROWS)
        def _(i):
            base = pl.multiple_of(first + i * SC_ROWS, SC_ROWS)
            pltpu.sync_copy(i_hbm.at[pl.ds(base, SC_ROWS)], idx_v)
            pltpu.sync_copy(x_hbm.at[idx_v], rows_v)
            pltpu.sync_copy(rows_v, o_hbm.at[pl.ds(base, SC_ROWS)])

    return gather(x, src)


def _routing_plan(cls, n_tiles):
    onehot = (cls[:, None] == jnp.arange(N_CLASSES, dtype=jnp.int32)[None, :]).astype(jnp.int32)
    counts = jnp.sum(onehot, axis=0)
    rank = jnp.sum((jnp.cumsum(onehot, axis=0) - onehot) * onehot, axis=1)
    tiles = (counts + MOE_TM - 1) // MOE_TM
    tile_end = jnp.cumsum(tiles)
    tile_start = tile_end - tiles
    dest = jnp.sum(onehot * tile_start[None, :], axis=1) * MOE_TM + rank
    nact = tile_end[-1]
    tile = jnp.arange(n_tiles, dtype=jnp.int32)
    blk = jnp.minimum(tile, nact - 1)
    tile_cls = jnp.sum((blk[:, None] >= tile_end[None, :]).astype(jnp.int32), axis=1)
    e1 = jnp.asarray(CLASS_E1)[tile_cls]
    e2 = jnp.asarray(CLASS_E2)[tile_cls]
    valid = counts[tile_cls] - (blk - tile_start[tile_cls]) * MOE_TM
    mode = jnp.where(tile >= nact, MOE_SKIP, jnp.where(valid > MOE_TM // 2, MOE_FULL_TILE, MOE_HALF_TILE))
    return dest.astype(jnp.int32), blk, e1, e2, mode.astype(jnp.int32)


def _row_tile(n):
    return 512 if n % 512 == 0 else 256


def kernel(x, c, ctx, c_ctx, w_ada, b_ada, norm1, w_in, pool_w, pool_scale, q_norm, k_norm, rpb,
           sg_w, sg_b, sg_norm, w_out, norm2, w_router, b_router, w_gate, w_up, w_down):
    depth = w_ada.shape[0]
    n = x.shape[1]
    lc = ctx.shape[1]
    bf = jnp.bfloat16
    lat_stream = (x[0],)
    h_ctx = ctx[0]

    cond = jnp.zeros((SUBLANES, D_MODEL), jnp.float32).at[0].set(c[0]).at[1].set(c_ctx)
    mod = _ada_call(cond, w_ada, b_ada)

    wr_t = w_router.T
    wr_hi = wr_t.astype(bf)
    wr_lo = (wr_t - wr_hi.astype(jnp.float32)).astype(bf)
    wr = jnp.concatenate([wr_hi, wr_lo], axis=0)
    br = b_router.reshape(N_EXPERTS, 1)
    n_tiles = n // MOE_TM + N_CLASSES
    w_in_bf = w_in.astype(bf)
    w_out_bf = w_out.astype(bf)

    for l in range(depth):
        last = l == depth - 1
        qg =(q_norm[l] * (HEAD_DIM ** -0.5 * LOG2_E)).reshape(1, NA_DIM)
        kg = k_norm[l].reshape(1, NA_DIM)
        sgn = sg_norm[l].reshape(1, SG_DIM)
        sgw = sg_w[l].astype(bf).reshape(SG_DIM // LANES, 2 * SG_CHUNK, SG_CHUNK)
        sgb = jnp.broadcast_to(sg_b[l].reshape(SG_DIM // LANES, 2 * SG_CHUNK, 1),
                               (SG_DIM // LANES, 2 * SG_CHUNK, LANES))
        pool_bd = jax.scipy.linalg.block_diag(*[pool_w[l, g] for g in range(len(POOL_WINDOWS))]).astype(bf)
        pscale = pool_scale[l].reshape(1, POOL_DIM)
        bias = _natten_bias(rpb[l])
        n1 = norm1[l].reshape(1, D_MODEL)
        n2 = norm2[l].reshape(1, D_MODEL)

        def mods(row):
            return [mod[l, row:row + 1, i * D_MODEL:(i + 1) * D_MODEL] for i in range(6)]

        sh1, sc1, g1, sh2, sc2, g2 = mods(0)
        csh1, csc1, cg1, csh2, csc2, cg2 = mods(1)

        tc = _row_tile(lc)
        mix_pool_c, q_c, k_c, v_c, sg_c = _inproj_call((h_ctx,), n1, csh1, csc1, w_in_bf, l, qg, kg,
                                                       sgn, sgw, sgb, pool_bd, pscale, tc, strips=False)

        tm = _row_tile(n)
        outs = _inproj_call(lat_stream, n1, sh1, sc1, w_in_bf, l, qg, kg, sgn, sgw, sgb,
                            pool_bd, pscale, tm, strips=True)
        mix_pool, q, k, v, sg = outs[:5]
        h_lat = outs[5] if len(lat_stream) == 3 else lat_stream[0]
        att, wg_l, wu_l, wd_l = _natten_call(q, k, v, k_c, v_c, bias, w_gate, w_up, w_down, l)
        h1, pay, route = _outproj_call(h_lat, mix_pool, att, sg, w_out_bf, l, g1, n2, sh2, sc2, wr, br,
                                       OUTPROJ_TILE if n % OUTPROJ_TILE == 0 else tm)
        cls = route[4].astype(jnp.int32)
        dest, blk, e1, e2, mode = _routing_plan(cls, n_tiles)
        pay_sorted = _scatter_rows(pay, dest, n_tiles * MOE_TM)
        y_sorted = _moe_sorted_call(blk, e1, e2, mode, pay_sorted, wg_l, wu_l, wd_l)
        y = _gather_rows(y_sorted, dest)
        lat_stream = (h1, y, g2)

        if not last:
            att_c = _ctxatt_call(q_c, k_c, v_c)
            h1_c, pay_c, _ = _outproj_call(h_ctx, mix_pool_c, att_c, sg_c, w_out_bf, l, cg1, n2, csh2, csc2,
                                           wr, br, tc)
            y_c = _moe_dense_call(pay_c, wg_l, wu_l, wd_l)
            h_ctx = _residual_call(h1_c, y_c, cg2, tc)

    return _residual_call(*lat_stream, _row_tile(n))[None]
```

```python
import functools

import jax
import jax.numpy as jnp
import numpy as np
from jax import lax
from jax.experimental import pallas as pl
from jax.experimental.pallas import tpu as pltpu
from jax.experimental.pallas import tpu_sc as plsc

D_MODEL = 1024
GRID_W = 64
HEAD_DIM = 64
POOL_WINDOWS = (2, 4, 8, 16)
POOL_DIM = 256
NA_HEADS = 8
NA_DIM = 512
NA_WIN_ROWS = 8
NA_WIN_COLS = 16
SG_DIM = 256
SG_CHUNK = 128
Q_OFF = POOL_DIM
K_OFF = Q_OFF + NA_DIM
V_OFF = K_OFF + NA_DIM
U_OFF = V_OFF + NA_DIM
G_OFF = U_OFF + SG_DIM
IN_DIM = G_OFF + SG_DIM
N_EXPERTS = 16
GROUP_SIZE = 4
D_EXPERT = 512
EPS = 1e-6

LANES = 128
SUBLANES = 8
HEAD_PAIRS = NA_DIM // LANES
VMEM_LIMIT = 48 * 1024 * 1024

PAIRS = ((0, 1), (0, 2), (1, 2), (1, 3), (0, 3), (2, 3))
N_CLASSES = (N_EXPERTS // GROUP_SIZE) * len(PAIRS)
CLASS_E1 = np.array([4 * g + i for g in range(4) for (i, j) in PAIRS], np.int32)
CLASS_E2 = np.array([4 * g + j for g in range(4) for (i, j) in PAIRS], np.int32)

ROUTE_ROWS = 8
HALF_D = D_MODEL // 2
PAYLOAD_W = HALF_D + LANES
MOE_TM = 512
NEG_BIG = -1e30
LOG2_E = 1.4426950408889634


def _cparams(sem):
    return pltpu.CompilerParams(dimension_semantics=sem, vmem_limit_bytes=VMEM_LIMIT)


def _dot(a, b):
    return jnp.dot(a, b, preferred_element_type=jnp.float32)


def _dot_nt(a, b):
    return lax.dot_general(a, b, (((1,), (1,)), ((), ())), preferred_element_type=jnp.float32)


def _gelu_tanh(x):
    return 0.5 * x * (1.0 + jnp.tanh(0.7978845608028654 * (x + 0.044715 * (x * x * x))))


def _silu(x):
    return x * (1.0 / (1.0 + jnp.exp(-x)))


def _pack_halves(x):
    w = x.shape[1] // 2
    lo = pltpu.bitcast(x[:, :w].astype(jnp.bfloat16).astype(jnp.float32), jnp.uint32) >> 16
    hi = pltpu.bitcast(x[:, w:].astype(jnp.bfloat16).astype(jnp.float32), jnp.uint32) & jnp.uint32(0xFFFF0000)
    return lo | hi


def _unpack_halves(words):
    lo = pltpu.bitcast(words << 16, jnp.float32)
    hi = pltpu.bitcast(words & jnp.uint32(0xFFFF0000), jnp.float32)
    return lo, hi


def _ada_kernel(cond_ref, w_ref, b_ref, o_ref):
    cond = _silu(cond_ref[...])
    o_ref[...] = jnp.dot(cond, w_ref[...], preferred_element_type=jnp.float32,
                         precision=lax.Precision.HIGHEST) + b_ref[...]


def _ada_call(cond, w_ada, b_ada):
    depth = w_ada.shape[0]
    tn = 1536
    return pl.pallas_call(
        _ada_kernel,
        grid=(depth, 6 * D_MODEL // tn),
        in_specs=[
            pl.BlockSpec((SUBLANES, D_MODEL), lambda l, j: (0, 0)),
            pl.BlockSpec((None, D_MODEL, tn), lambda l, j: (l, 0, j)),
            pl.BlockSpec((None, 1, tn), lambda l, j: (l, 0, j)),
        ],
        out_specs=pl.BlockSpec((None, SUBLANES, tn), lambda l, j: (l, 0, j)),
        out_shape=jax.ShapeDtypeStruct((depth, SUBLANES, 6 * D_MODEL), jnp.float32),
        compiler_params=_cparams(("arbitrary", "arbitrary")),
        name="adaln",
    )(cond, w_ada, b_ada.reshape(depth, 1, 6 * D_MODEL))


def _norm_modulate(x, n_ref, sh_ref, sc_ref):
    ms = jnp.mean(x * x, axis=-1, keepdims=True)
    return ((x * lax.rsqrt(ms + EPS)) * (n_ref[...] * (1.0 + sc_ref[...])) + sh_ref[...]).astype(jnp.bfloat16)


def _head_rms_scale(a):
    low = lax.broadcasted_iota(jnp.int32, (a.shape[0], LANES), 1) < HEAD_DIM
    blocks = []
    for p in range(a.shape[1] // LANES):
        sq = jnp.square(a[:, p * LANES:(p + 1) * LANES])
        s_lo = jnp.sum(jnp.where(low, sq, 0.0), axis=-1, keepdims=True)
        s_hi = jnp.sum(jnp.where(low, 0.0, sq), axis=-1, keepdims=True)
        blocks.append(jnp.where(low, lax.rsqrt(s_lo * (1.0 / HEAD_DIM) + EPS),
                                lax.rsqrt(s_hi * (1.0 / HEAD_DIM) + EPS)))
    return jnp.concatenate(blocks, axis=-1)


STRIP_W = 8
N_STRIPS = GRID_W // STRIP_W


def _store_keys(ref, x):
    if len(ref.shape) == 2:
        ref[...] = x.astype(jnp.bfloat16)
        return
    pair = 2 * STRIP_W
    for s in range(N_STRIPS):
        for rp in range(x.shape[0] // (2 * GRID_W)):
            top = 2 * rp * GRID_W + s * STRIP_W
            rows = jnp.concatenate([x[top:top + STRIP_W], x[top + GRID_W:top + GRID_W + STRIP_W]], axis=0)
            ref[s, rp * pair:(rp + 1) * pair, :] = rows.astype(jnp.bfloat16)


POOL_HALO = 8
HALO_BLOCK = 16
STREAM_DTYPE = jnp.bfloat16


POOL_EDGE = 16


def _pool_mix(xe_ref, w_ref, scale_ref, tm, seq_len):
    low = lax.broadcasted_iota(jnp.int32, (tm, LANES), 1) < HEAD_DIM
    t_edge = pl.program_id(0) * tm + lax.broadcasted_iota(jnp.int32, (POOL_EDGE, LANES), 0)

    def window_mean(s, half):
        mean = s * (1.0 / (2 * half))

        def clip_fix(t):
            count = (jnp.minimum(t + half, seq_len) - jnp.maximum(t - half, 0)).astype(jnp.float32)
            return (2.0 * half) / count

        return jnp.concatenate([mean[:POOL_EDGE] * clip_fix(t_edge),
                                mean[POOL_EDGE:tm - POOL_EDGE],
                                mean[tm - POOL_EDGE:] * clip_fix(t_edge + (tm - POOL_EDGE))], axis=0)

    def window_sums(xs, n_levels):
        sums = []
        s = xs
        for k in range(n_levels):
            step = 1 << k
            s = s[:-step] + s[step:]
            sums.append(s)
        return sums

    outs = []
    for half_block, windows in enumerate(((2, 4), (8, 16))):
        xs = xe_ref[:, half_block * LANES:(half_block + 1) * LANES]
        sums = window_sums(xs, int(np.log2(windows[1])))
        parts = []
        for w in windows:
            half = w // 2
            s = sums[int(np.log2(w)) - 1][POOL_HALO - half:POOL_HALO - half + tm]
            parts.append(window_mean(s, half))
        mean = jnp.where(low, parts[0], parts[1])
        outs.append(mean - xs[POOL_HALO:POOL_HALO + tm])
    d = jnp.concatenate(outs, axis=-1).astype(jnp.bfloat16)
    return (_dot(d, w_ref[...]) * scale_ref[...]).astype(jnp.bfloat16)


def _inproj_kernel(*refs, pending, seq_len):
    n_stream = 7 if pending else 3
    stream, refs = refs[:n_stream], refs[n_stream:]
    if pending:
        h_ref, y_ref, g_ref, hp_ref, hn_ref, yp_ref, yn_ref = stream
        hres_ref, refs = refs[-2], refs[:-2] + refs[-1:]
        x = h_ref[...].astype(jnp.float32) + g_ref[...] * _moe_out(y_ref)
        hres_ref[...] = x.astype(hres_ref.dtype)
        before = hp_ref[...].astype(jnp.float32) + g_ref[...] * _moe_out(yp_ref)
        after = hn_ref[...].astype(jnp.float32) + g_ref[...] * _moe_out(yn_ref)
    else:
        h_ref, hp_ref, hn_ref = stream
        x = h_ref[...].astype(jnp.float32)
        before = hp_ref[...].astype(jnp.float32)
        after = hn_ref[...].astype(jnp.float32)
    x_halo = jnp.concatenate([before[HALO_BLOCK - POOL_HALO:], after[:POOL_HALO]], axis=0)
    (n1_ref, sh_ref, sc_ref, w_ref, qg_ref, kg_ref, sgn_ref, sgw_ref, sgb_ref,
     pw_ref, ps_ref, pool_ref, q_ref, k_ref, v_ref, sg_ref, xe_ref) = refs
    tm = h_ref.shape[0]
    i = pl.program_id(0)
    hn = _norm_modulate(x, n1_ref, sh_ref, sc_ref)
    hn_halo = _norm_modulate(x_halo, n1_ref, sh_ref, sc_ref)

    a_halo = _dot(hn_halo, w_ref[:, 0:Q_OFF])
    xe_ref[0:POOL_HALO, :] = jnp.where(i > 0, a_halo[:POOL_HALO], 0.0)
    xe_ref[POOL_HALO:POOL_HALO + tm, :] = _dot(hn, w_ref[:, 0:Q_OFF])
    xe_ref[POOL_HALO + tm:, :] = jnp.where(i < pl.num_programs(0) - 1, a_halo[POOL_HALO:], 0.0)

    a_g = _dot(hn, w_ref[:, G_OFF:IN_DIM])
    a_u = _dot(hn, w_ref[:, U_OFF:G_OFF])
    a_q = _dot(hn, w_ref[:, Q_OFF:K_OFF])
    a_k = _dot(hn, w_ref[:, K_OFF:V_OFF])
    _store_keys(v_ref, _dot(hn, w_ref[:, V_OFF:U_OFF]))

    gv = _gelu_tanh(a_g)
    q_ref[...] = (a_q * _head_rms_scale(a_q) * qg_ref[...]).astype(jnp.bfloat16)
    _store_keys(k_ref, a_k * _head_rms_scale(a_k) * kg_ref[...])

    u = _gelu_tanh(a_u)
    vn = (gv * _head_rms_scale(gv) * sgn_ref[...]).astype(jnp.bfloat16)
    low = lax.broadcasted_iota(jnp.int32, (SG_CHUNK, LANES), 1) < HEAD_DIM
    for c in range(tm // SG_CHUNK):
        rows = slice(c * SG_CHUNK, (c + 1) * SG_CHUNK)
        for s in range(SG_DIM // LANES):
            cols = slice(s * LANES, (s + 1) * LANES)
            m = _dot(sgw_ref[s], vn[rows, cols]) + sgb_ref[s]
            mixed = jnp.where(low, m[:SG_CHUNK], m[SG_CHUNK:])
            sg_ref[rows, cols] = (u[rows, cols] * mixed).astype(jnp.bfloat16)

    pool_ref[...] = _pool_mix(xe_ref, pw_ref, ps_ref, tm, seq_len)


def _inproj_call(stream, n1, sh, sc, w_in, layer, qg, kg, sgn, sgw, sgb, pool_w, pool_scale, tm, strips):
    pending = len(stream) == 3
    n = stream[0].shape[0]
    if strips:
        kv_spec = pl.BlockSpec((N_STRIPS, tm // N_STRIPS, NA_DIM), lambda i: (0, i, 0))
        kv_shape = jax.ShapeDtypeStruct((N_STRIPS, n // N_STRIPS, NA_DIM), jnp.bfloat16)
    else:
        kv_spec = pl.BlockSpec((tm, NA_DIM), lambda i: (i, 0))
        kv_shape = jax.ShapeDtypeStruct((n, NA_DIM), jnp.bfloat16)
    row = lambda i: (i, 0)
    fixed2 = lambda i: (0, 0)
    fixed3 = lambda i: (0, 0, 0)
    vec = lambda w: pl.BlockSpec((1, w), fixed2)
    rows = pl.BlockSpec((tm, D_MODEL), row)
    per_tile = tm // HALO_BLOCK
    before = lambda i: (jnp.maximum(i * per_tile - 1, 0), 0)
    after = lambda i: (jnp.minimum((i + 1) * per_tile, n // HALO_BLOCK - 1), 0)
    halo = lambda w, m: pl.BlockSpec((HALO_BLOCK, w), m)
    h = stream[0]
    if pending:
        y, g = stream[1], stream[2]
        yw = y.shape[1]
        args = [h, y, g, h, h, y, y]
        stream_specs = [rows, pl.BlockSpec((tm, yw), row), vec(D_MODEL),
                        halo(D_MODEL, before), halo(D_MODEL, after), halo(yw, before), halo(yw, after)]
    else:
        args = [h, h, h]
        stream_specs = [rows, halo(D_MODEL, before), halo(D_MODEL, after)]
    extra_out_specs = [rows] if pending else []
    extra_out_shape = [jax.ShapeDtypeStruct((n, D_MODEL), STREAM_DTYPE)] if pending else []
    return pl.pallas_call(
        functools.partial(_inproj_kernel, pending=pending, seq_len=n),
        grid=(n // tm,),
        in_specs=stream_specs + [
            vec(D_MODEL), vec(D_MODEL), vec(D_MODEL),
            pl.BlockSpec((None, D_MODEL, IN_DIM), lambda i: (layer, 0, 0)),
            vec(NA_DIM), vec(NA_DIM),
            vec(SG_DIM),
            pl.BlockSpec((SG_DIM // LANES, 2 * SG_CHUNK, SG_CHUNK), fixed3),
            pl.BlockSpec((SG_DIM // LANES, 2 * SG_CHUNK, LANES), fixed3),
            pl.BlockSpec((POOL_DIM, POOL_DIM), fixed2),
            vec(POOL_DIM),
        ],
        out_specs=[
            pl.BlockSpec((tm, POOL_DIM), row),
            pl.BlockSpec((tm, NA_DIM), row),
            kv_spec,
            kv_spec,
            pl.BlockSpec((tm, SG_DIM), row),
        ] + extra_out_specs,
        out_shape=[
            jax.ShapeDtypeStruct((n, POOL_DIM), jnp.bfloat16),
            jax.ShapeDtypeStruct((n, NA_DIM), jnp.bfloat16),
            kv_shape,
            kv_shape,
            jax.ShapeDtypeStruct((n, SG_DIM), jnp.bfloat16),
        ] + extra_out_shape,
        scratch_shapes=[pltpu.VMEM((tm + 2 * POOL_HALO, POOL_DIM), jnp.float32)],
        compiler_params=_cparams(("arbitrary",)),
        name="inproj",
    )(*args, n1, sh, sc, w_in, qg, kg, sgn, sgw, sgb, pool_w, pool_scale)


NA_ROWS_PER_BLOCK = 64
NA_GROUP_ROWS = 4
NA_WINDOW_ROWS = NA_GROUP_ROWS + NA_WIN_ROWS
NA_BLOCK = NA_ROWS_PER_BLOCK * GRID_W
NA_GROUP = NA_GROUP_ROWS * GRID_W
NA_HALF_COLS = GRID_W // 2
NA_CHAIN = NA_GROUP_ROWS * NA_HALF_COLS
NA_HALF_STRIPS = 5
NA_HALF_COL0 = (0, GRID_W - NA_HALF_STRIPS * STRIP_W)
NA_RUN = NA_WINDOW_ROWS * STRIP_W
NA_LOCAL = 512
NA_STRIP_BLOCK = NA_ROWS_PER_BLOCK * STRIP_W
NA_STRIP_HALO = (NA_WIN_ROWS // 2) * STRIP_W
NA_EDGE_FIRST, NA_EDGE_NONE, NA_EDGE_LAST = 0, 1, 2


def _stack_heads(x, low):
    zero = jnp.zeros_like(x)
    return jnp.concatenate([jnp.where(low, x, zero), jnp.where(low, zero, x)], axis=0)


def _natten_kernel(q_ref, kp_ref, kc_ref, kn_ref, vp_ref, vc_ref, vn_ref, kx_ref, vx_ref, bias_ref,
                   wg_ref, wu_ref, wd_ref,
                   o_ref, wg_bf_ref, wu_bf_ref, wd_bf_ref, kwin_ref, vwin_ref, vxe_ref, *, grid_rows):
    b = pl.program_id(1)
    wg_bf_ref[...] = wg_ref[...].astype(jnp.bfloat16)
    wu_bf_ref[...] = wu_ref[...].astype(jnp.bfloat16)
    wd_bf_ref[...] = wd_ref[...].astype(jnp.bfloat16)
    top, bottom = NA_STRIP_HALO, NA_STRIP_HALO + NA_STRIP_BLOCK
    kwin_ref[:, 0:top, :] = kp_ref[...]
    kwin_ref[:, top:bottom, :] = kc_ref[...]
    kwin_ref[:, bottom:, :] = kn_ref[...]
    vwin_ref[:, 0:top, 0:LANES] = vp_ref[...]
    vwin_ref[:, top:bottom, 0:LANES] = vc_ref[...]
    vwin_ref[:, bottom:, 0:LANES] = vn_ref[...]
    vwin_ref[:, :, LANES:] = jnp.ones(vwin_ref.shape[:2] + (LANES,), jnp.bfloat16)
    vxe_ref[:, 0:LANES] = vx_ref[...]
    vxe_ref[:, LANES:] = jnp.ones((vxe_ref.shape[0], LANES), jnp.bfloat16)
    low_q = lax.broadcasted_iota(jnp.int32, (NA_CHAIN, LANES), 1) < HEAD_DIM
    n_pad = NA_LOCAL - NA_HALF_STRIPS * NA_RUN

    def window_start(g):
        r0 = b * NA_ROWS_PER_BLOCK + g * NA_GROUP_ROWS
        ws = jnp.clip(r0 - NA_WIN_ROWS // 2, 0, grid_rows - NA_WINDOW_ROWS)
        edge = jnp.where(r0 == 0, NA_EDGE_FIRST,
                         jnp.where(r0 == grid_rows - NA_GROUP_ROWS, NA_EDGE_LAST, NA_EDGE_NONE))
        start = pl.multiple_of((ws - b * NA_ROWS_PER_BLOCK + NA_WIN_ROWS // 2) * STRIP_W, NA_STRIP_HALO)
        return start, edge

    def local_window(win_ref, g, half):
        start, _ = window_start(g)
        s0 = NA_HALF_COL0[half] // STRIP_W
        runs = [win_ref[s, pl.ds(start, NA_RUN), :] for s in range(s0, s0 + NA_HALF_STRIPS)]
        return jnp.concatenate(runs + [jnp.zeros((n_pad, win_ref.shape[2]), jnp.bfloat16)], axis=0)

    def query_rows(g, half, j):
        first = g * NA_GROUP + j * GRID_W + half * NA_HALF_COLS
        return slice(first, first + NA_HALF_COLS)

    def scores(c):
        g, half = divmod(c, 2)
        _, edge = window_start(g)
        qh = jnp.concatenate([q_ref[query_rows(g, half, j), :] for j in range(NA_GROUP_ROWS)], axis=0)
        lhs = _stack_heads(qh, low_q)
        kl = local_window(kwin_ref, g, half)
        return jnp.concatenate([_dot_nt(lhs, kl).astype(jnp.bfloat16) + bias_ref[edge, half],
                                _dot_nt(lhs, kx_ref[...]).astype(jnp.bfloat16)], axis=-1)

    n_chains = 2 * (NA_ROWS_PER_BLOCK // NA_GROUP_ROWS)
    s_next = scores(0)
    for c in range(n_chains):
        s = s_next
        if c + 1 < n_chains:
            s_next = scores(c + 1)
        g, half = divmod(c, 2)
        vl = local_window(vwin_ref, g, half)
        m = jnp.max(s, axis=-1, keepdims=True)
        pb = jnp.exp2(s - m)
        o = _dot(pb[:, :NA_LOCAL], vl) + _dot(pb[:, NA_LOCAL:], vxe_ref[...])
        o = o[:, :LANES] * (1.0 / o[:, LANES:])
        o = jnp.where(low_q, o[:NA_CHAIN], o[NA_CHAIN:]).astype(jnp.bfloat16)
        for j in range(NA_GROUP_ROWS):
            o_ref[query_rows(g, half, j), :] = o[j * NA_HALF_COLS:(j + 1) * NA_HALF_COLS]


def _natten_call(q, k, v, k_ctx, v_ctx, bias, w_gate, w_up, w_down, layer):
    n = q.shape[0]
    grid_rows = n // GRID_W
    assert grid_rows % NA_ROWS_PER_BLOCK == 0 and grid_rows >= 2 * NA_ROWS_PER_BLOCK
    nblk = n // NA_BLOCK
    steps = HEAD_PAIRS * nblk
    depth = w_gate.shape[0]
    up_rows = N_EXPERTS * D_MODEL
    down_rows = N_EXPERTS * D_EXPERT
    assert up_rows % steps == 0 and down_rows % steps == 0
    wg2 = w_gate.reshape(depth * up_rows, D_EXPERT)
    wu2 = w_up.reshape(depth * up_rows, D_EXPERT)
    wd2 = w_down.reshape(depth * down_rows, D_MODEL)
    up_in = pl.BlockSpec((up_rows // steps, D_EXPERT), lambda p, b: (layer * steps + p * nblk + b, 0))
    down_in = pl.BlockSpec((down_rows // steps, D_MODEL), lambda p, b: (layer * steps + p * nblk + b, 0))
    up_out = pl.BlockSpec((up_rows // steps, D_EXPERT), lambda p, b: (p * nblk + b, 0))
    down_out = pl.BlockSpec((down_rows // steps, D_MODEL), lambda p, b: (p * nblk + b, 0))
    n_halo = n // N_STRIPS // NA_STRIP_HALO
    hb = NA_STRIP_BLOCK // NA_STRIP_HALO
    rows = pl.BlockSpec((NA_BLOCK, LANES), lambda p, b: (b, p))
    cur = pl.BlockSpec((N_STRIPS, NA_STRIP_BLOCK, LANES), lambda p, b: (0, b, p))
    prev = pl.BlockSpec((N_STRIPS, NA_STRIP_HALO, LANES), lambda p, b: (0, jnp.maximum(b * hb - 1, 0), p))
    nxt = pl.BlockSpec((N_STRIPS, NA_STRIP_HALO, LANES),
                       lambda p, b: (0, jnp.minimum((b + 1) * hb, n_halo - 1), p))
    ctx = pl.BlockSpec((k_ctx.shape[0], LANES), lambda p, b: (0, p))
    win_rows = NA_STRIP_BLOCK + 2 * NA_STRIP_HALO
    att, wg_bf, wu_bf, wd_bf = pl.pallas_call(
        functools.partial(_natten_kernel, grid_rows=grid_rows),
        grid=(HEAD_PAIRS, nblk),
        in_specs=[rows, prev, cur, nxt, prev, cur, nxt, ctx, ctx,
                  pl.BlockSpec((None, 3, 2, 2 * NA_CHAIN, NA_LOCAL), lambda p, b: (p, 0, 0, 0, 0)),
                  up_in, up_in, down_in],
        out_specs=[rows, up_out, up_out, down_out],
        out_shape=[jax.ShapeDtypeStruct((n, NA_DIM), jnp.bfloat16),
                   jax.ShapeDtypeStruct((up_rows, D_EXPERT), jnp.bfloat16),
                   jax.ShapeDtypeStruct((up_rows, D_EXPERT), jnp.bfloat16),
                   jax.ShapeDtypeStruct((down_rows, D_MODEL), jnp.bfloat16)],
        scratch_shapes=[pltpu.VMEM((N_STRIPS, win_rows, LANES), jnp.bfloat16),
                        pltpu.VMEM((N_STRIPS, win_rows, 2 * LANES), jnp.bfloat16),
                        pltpu.VMEM((k_ctx.shape[0], 2 * LANES), jnp.bfloat16)],
        compiler_params=_cparams(("arbitrary", "arbitrary")),
        name="natten",
    )(q, k, k, k, v, v, v, k_ctx, v_ctx, bias, wg2, wu2, wd2)
    return (att, wg_bf.reshape(N_EXPERTS, D_MODEL, D_EXPERT), wu_bf.reshape(N_EXPERTS, D_MODEL, D_EXPERT),
            wd_bf.reshape(N_EXPERTS, D_EXPERT, D_MODEL))


def _natten_bias(rpb):
    cols = np.arange(GRID_W)
    col_start = np.clip(cols - NA_WIN_COLS // 2, 0, GRID_W - NA_WIN_COLS)
    kc = np.arange(GRID_W)
    in_win = (kc[None, :] >= col_start[:, None]) & (kc[None, :] < col_start[:, None] + NA_WIN_COLS)
    dc = kc[None, :] - cols[:, None] + NA_WIN_COLS - 1
    sel = (np.arange(2 * NA_WIN_COLS - 1)[:, None, None] == dc[None]) & in_win[None]
    t2 = jnp.einsum("hdj,jqk->hdqk", rpb, jnp.asarray(sel, jnp.float32), precision=lax.Precision.HIGHEST)
    t2 = jnp.where(in_win[None, None], t2 * LOG2_E, NEG_BIG)
    neg = jnp.full((NA_HEADS, 1, GRID_W, GRID_W), NEG_BIG, jnp.float32)
    t2e = jnp.concatenate([neg, t2, neg], axis=1)
    u = jnp.concatenate([t2e[:, :-1], t2e[:, 1:]], axis=-1)
    u = u.reshape(HEAD_PAIRS, 2, 2 * NA_WIN_ROWS, GRID_W, LANES)

    place = np.zeros((2, NA_WINDOW_ROWS * GRID_W, NA_LOCAL), np.float32)
    for half, c0 in enumerate(NA_HALF_COL0):
        for a in range(NA_WINDOW_ROWS):
            for kcol in range(c0, c0 + NA_HALF_STRIPS * STRIP_W):
                s, c8 = divmod(kcol - c0, STRIP_W)
                place[half, a * GRID_W + kcol, s * NA_RUN + a * STRIP_W + c8] = 1.0
    outside = np.full((3, NA_GROUP_ROWS, NA_LOCAL), NEG_BIG, np.float32)
    for edge in (NA_EDGE_FIRST, NA_EDGE_NONE, NA_EDGE_LAST):
        for j in range(NA_GROUP_ROWS):
            lo, _ = _window_rows(edge, j)
            for s in range(NA_HALF_STRIPS):
                outside[edge, j, s * NA_RUN + lo * STRIP_W:s * NA_RUN + (lo + NA_WIN_ROWS) * STRIP_W] = 0.0
    return pl.pallas_call(
        _bias_expand_kernel,
        grid=(HEAD_PAIRS,),
        in_specs=[pl.BlockSpec((None, 2, 2 * NA_WIN_ROWS, GRID_W, LANES), lambda p: (p, 0, 0, 0, 0)),
                  pl.BlockSpec(place.shape, lambda p: (0, 0, 0)),
                  pl.BlockSpec(outside.shape, lambda p: (0, 0, 0))],
        out_specs=pl.BlockSpec((None, 3, 2, 2 * NA_CHAIN, NA_LOCAL), lambda p: (p, 0, 0, 0, 0)),
        out_shape=jax.ShapeDtypeStruct((HEAD_PAIRS, 3, 2, 2 * NA_CHAIN, NA_LOCAL), jnp.bfloat16),
        compiler_params=_cparams(("arbitrary",)),
        name="bias_expand",
    )(u, jnp.asarray(place, jnp.bfloat16), jnp.asarray(outside))


def _window_rows(edge, j):
    if edge == NA_EDGE_FIRST:
        return 0, NA_WIN_ROWS - 1 - j
    if edge == NA_EDGE_NONE:
        return j, NA_WIN_ROWS // 2 - 1
    return NA_WINDOW_ROWS - NA_WIN_ROWS, NA_WIN_ROWS // 2 - 1 - j


def _bias_expand_kernel(u_ref, place_ref, outside_ref, o_ref):
    low = lax.broadcasted_iota(jnp.int32, (NA_HALF_COLS, LANES), 1) < GRID_W
    zero = jnp.zeros((NA_HALF_COLS, LANES), jnp.float32)
    for edge in (NA_EDGE_FIRST, NA_EDGE_NONE, NA_EDGE_LAST):
        for half in range(2):
            q0 = half * NA_HALF_COLS
            blocks, masks = [], []
            for hd in range(2):
                for j in range(NA_GROUP_ROWS):
                    lo, base = _window_rows(edge, j)
                    tiles = []
                    for i in range(NA_WINDOW_ROWS // 2):
                        a0, a1 = 2 * i, 2 * i + 1
                        ok0 = lo <= a0 < lo + NA_WIN_ROWS
                        ok1 = lo <= a1 < lo + NA_WIN_ROWS
                        if not (ok0 or ok1):
                            tile = zero
                        else:
                            tile = u_ref[hd, base + a1 - lo, q0:q0 + NA_HALF_COLS, :]
                            if not ok0:
                                tile = jnp.where(low, zero, tile)
                            if not ok1:
                                tile = jnp.where(low, tile, zero)
                        tiles.append(tile)
                    blocks.append(jnp.concatenate(tiles, axis=-1))
                    masks.append(jnp.broadcast_to(outside_ref[edge, j:j + 1, :], (NA_HALF_COLS, NA_LOCAL)))
            lhs = jnp.concatenate(blocks, axis=0).astype(jnp.bfloat16)
            placed = _dot(lhs, place_ref[half]) + jnp.concatenate(masks, axis=0)
            o_ref[edge, half] = placed.astype(o_ref.dtype)


def _ctxatt_kernel(q_ref, k_ref, v_ref, o_ref):
    lc = q_ref.shape[0]
    low = lax.broadcasted_iota(jnp.int32, (lc, LANES), 1) < HEAD_DIM
    lhs = _stack_heads(q_ref[...], low)
    s = _dot_nt(lhs, k_ref[...])
    m = jnp.max(s, axis=-1, keepdims=True)
    p = jnp.exp2(s - m)
    denom = jnp.sum(p, axis=-1, keepdims=True)
    o = _dot(p.astype(jnp.bfloat16), v_ref[...]) * (1.0 / denom)
    o_ref[...] = jnp.where(low, o[:lc], o[lc:]).astype(jnp.bfloat16)


def _ctxatt_call(q, k, v):
    lc = q.shape[0]
    spec = pl.BlockSpec((lc, LANES), lambda p: (0, p))
    return pl.pallas_call(
        _ctxatt_kernel,
        grid=(HEAD_PAIRS,),
        in_specs=[spec, spec, spec],
        out_specs=spec,
        out_shape=jax.ShapeDtypeStruct((lc, NA_DIM), jnp.bfloat16),
        compiler_params=_cparams(("arbitrary",)),
        name="ctxatt",
    )(q, k, v)


OUTPROJ_CHAIN = 256
OUTPROJ_TILE = 1024


def _outproj_kernel(h_ref, mp_ref, att_ref, sg_ref, wo_ref, g1_ref, n2_ref, sh_ref, sc_ref,
                    wr_ref, br_ref, h1_ref, pay_ref, route_ref):
    chains = [slice(c * OUTPROJ_CHAIN, (c + 1) * OUTPROJ_CHAIN) for c in range(h_ref.shape[0] // OUTPROJ_CHAIN)]
    h1s = []
    for rows in chains:
        mix = (_dot(mp_ref[rows, :], wo_ref[0:POOL_DIM, :])
               + _dot(att_ref[rows, :], wo_ref[POOL_DIM:POOL_DIM + NA_DIM, :])
               + _dot(sg_ref[rows, :], wo_ref[POOL_DIM + NA_DIM:, :]))
        h1 = h_ref[rows, :].astype(jnp.float32) + g1_ref[...] * mix
        h1_ref[rows, :] = h1.astype(h1_ref.dtype)
        h1s.append(h1)
    for rows, h1 in zip(chains, h1s):
        _outproj_route(rows, h1, n2_ref, sh_ref, sc_ref, wr_ref, br_ref, pay_ref, route_ref)


def _outproj_route(rows, h1, n2_ref, sh_ref, sc_ref, wr_ref, br_ref, pay_ref, route_ref):
    tm = OUTPROJ_CHAIN
    ms = jnp.mean(h1 * h1, axis=-1, keepdims=True)
    hm = (h1 * lax.rsqrt(ms + EPS)) * (n2_ref[...] * (1.0 + sc_ref[...])) + sh_ref[...]
    pay_ref[rows, 0:HALF_D] = _pack_halves(hm)

    hm_hi = hm.astype(jnp.bfloat16)
    lt = _dot_nt(wr_ref[...], hm_hi)
    logits = lt[:N_EXPERTS] + lt[N_EXPERTS:] + br_ref[...]
    e = jnp.exp(logits - jnp.max(logits, axis=0, keepdims=True))

    best = ga = gb = e1 = e2 = cls = None
    for c in range(N_CLASSES):
        a, b2 = int(CLASS_E1[c]), int(CLASS_E2[c])
        ea, eb = e[a:a + 1, :], e[b2:b2 + 1, :]
        s = ea + eb
        if best is None:
            best, ga, gb = s, ea, eb
            e1 = jnp.full_like(s, float(a))
            e2 = jnp.full_like(s, float(b2))
            cls = jnp.zeros_like(s)
        else:
            better = s > best
            best = jnp.where(better, s, best)
            ga = jnp.where(better, ea, ga)
            gb = jnp.where(better, eb, gb)
            e1 = jnp.where(better, float(a), e1)
            e2 = jnp.where(better, float(b2), e2)
            cls = jnp.where(better, float(c), cls)
    inv = 1.0 / best
    row = lax.broadcasted_iota(jnp.int32, (ROUTE_ROWS, tm), 0)
    rec = jnp.where(row == 0, ga * inv,
          jnp.where(row == 1, gb * inv,
          jnp.where(row == 2, e1,
          jnp.where(row == 3, e2,
          jnp.where(row == 4, cls, 0.0)))))
    route_ref[:, rows] = rec
    wide = jnp.concatenate([rec, jnp.zeros((LANES - ROUTE_ROWS, tm), jnp.float32)], axis=0)
    pay_ref[rows, HALF_D:] = pltpu.bitcast(wide.T, jnp.uint32)


def _outproj_call(h, mp, att, sg, w_out, layer, g1, n2, sh2, sc2, wr, br, tm):
    n = h.shape[0]
    row = lambda i: (i, 0)
    fixed = lambda i: (0, 0)
    vec = pl.BlockSpec((1, D_MODEL), fixed)
    return pl.pallas_call(
        _outproj_kernel,
        grid=(n // tm,),
        in_specs=[
            pl.BlockSpec((tm, D_MODEL), row),
            pl.BlockSpec((tm, POOL_DIM), row),
            pl.BlockSpec((tm, NA_DIM), row),
            pl.BlockSpec((tm, SG_DIM), row),
            pl.BlockSpec((None, D_MODEL, D_MODEL), lambda i: (layer, 0, 0)),
            vec, vec, vec, vec,
            pl.BlockSpec((2 * N_EXPERTS, D_MODEL), fixed),
            pl.BlockSpec((N_EXPERTS, 1), fixed),
        ],
        out_specs=[
            pl.BlockSpec((tm, D_MODEL), row),
            pl.BlockSpec((tm, PAYLOAD_W), row),
            pl.BlockSpec((ROUTE_ROWS, tm), lambda i: (0, i)),
        ],
        out_shape=[
            jax.ShapeDtypeStruct((n, D_MODEL), STREAM_DTYPE),
            jax.ShapeDtypeStruct((n, PAYLOAD_W), jnp.uint32),
            jax.ShapeDtypeStruct((ROUTE_ROWS, n), jnp.float32),
        ],
        compiler_params=_cparams(("arbitrary",)),
        name="outproj",
    )(h, mp, att, sg, w_out, g1, n2, sh2, sc2, wr, br)


def _payload_parts(pay_ref, rows=slice(None)):
    lo, hi = _unpack_halves(pay_ref[rows, 0:HALF_D])
    x = jnp.concatenate([lo, hi], axis=-1).astype(jnp.bfloat16)
    return x, pltpu.bitcast(pay_ref[rows, HALF_D:], jnp.float32)


def _expert_pair(x, ga, gb, wga, wua, wda, wgb, wub, wdb):
    ha = (_silu(_dot(x, wga)) * _dot(x, wua) * ga).astype(jnp.bfloat16)
    hb = (_silu(_dot(x, wgb)) * _dot(x, wub) * gb).astype(jnp.bfloat16)
    return _dot(ha, wda) + _dot(hb, wdb)


MOE_QUARTER = MOE_TM // 4


def _moe_sorted_kernel(blk_ref, e1_ref, e2_ref, quarters_ref, pay_ref,
                       wga_ref, wua_ref, wda_ref, wgb_ref, wub_ref, wdb_ref, o_ref):
    quarters = quarters_ref[pl.program_id(0)]

    def run(rows):
        x, route = _payload_parts(pay_ref, rows)
        y = _expert_pair(x, route[:, 0:1], route[:, 1:2], wga_ref[...], wua_ref[...], wda_ref[...],
                         wgb_ref[...], wub_ref[...], wdb_ref[...])
        o_ref[rows, :] = _pack_halves(y)

    for used in range(1, MOE_TM // MOE_QUARTER + 1):
        @pl.when(quarters == used)
        def _(used=used):
            run(slice(0, used * MOE_QUARTER))


def _moe_sorted_call(blk, e1, e2, quarters, pay_sorted, wg, wu, wd):
    n_tiles = blk.shape[0]
    rows = lambda i, blk, e1, e2, quarters: (blk[i], 0)
    wa = lambda i, blk, e1, e2, quarters: (e1[i], 0, 0)
    wb = lambda i, blk, e1, e2, quarters: (e2[i], 0, 0)
    up = lambda m: pl.BlockSpec((None, D_MODEL, D_EXPERT), m)
    down = lambda m: pl.BlockSpec((None, D_EXPERT, D_MODEL), m)
    return pl.pallas_call(
        _moe_sorted_kernel,
        grid_spec=pltpu.PrefetchScalarGridSpec(
            num_scalar_prefetch=4,
            grid=(n_tiles,),
            in_specs=[pl.BlockSpec((MOE_TM, PAYLOAD_W), rows),
                      up(wa), up(wa), down(wa), up(wb), up(wb), down(wb)],
            out_specs=pl.BlockSpec((MOE_TM, HALF_D), rows),
        ),
        out_shape=jax.ShapeDtypeStruct((n_tiles * MOE_TM, HALF_D), jnp.uint32),
        compiler_params=_cparams(("arbitrary",)),
        name="moe_sorted",
    )(blk, e1, e2, quarters, pay_sorted, wg, wu, wd, wg, wu, wd)


def _moe_dense_kernel(pay_ref, wg_ref, wu_ref, wd_ref, o_ref):
    e = pl.program_id(0)

    @pl.when(e == 0)
    def _():
        o_ref[...] = jnp.zeros_like(o_ref)

    x, route = _payload_parts(pay_ref)
    ef = e.astype(jnp.float32)
    gate = (jnp.where(route[:, 2:3] == ef, route[:, 0:1], 0.0)
            + jnp.where(route[:, 3:4] == ef, route[:, 1:2], 0.0))
    he = (_silu(_dot(x, wg_ref[...])) * _dot(x, wu_ref[...]) * gate).astype(jnp.bfloat16)
    o_ref[...] += _dot(he, wd_ref[...])


def _moe_dense_call(pay, wg, wu, wd):
    n = pay.shape[0]
    return pl.pallas_call(
        _moe_dense_kernel,
        grid=(N_EXPERTS,),
        in_specs=[pl.BlockSpec((n, PAYLOAD_W), lambda e: (0, 0)),
                  pl.BlockSpec((None, D_MODEL, D_EXPERT), lambda e: (e, 0, 0)),
                  pl.BlockSpec((None, D_MODEL, D_EXPERT), lambda e: (e, 0, 0)),
                  pl.BlockSpec((None, D_EXPERT, D_MODEL), lambda e: (e, 0, 0))],
        out_specs=pl.BlockSpec((n, D_MODEL), lambda e: (0, 0)),
        out_shape=jax.ShapeDtypeStruct((n, D_MODEL), jnp.float32),
        compiler_params=_cparams(("arbitrary",)),
        name="moe_dense",
    )(pay, wg, wu, wd)


def _moe_out(y_ref):
    if y_ref.dtype == jnp.uint32:
        return jnp.concatenate(_unpack_halves(y_ref[...]), axis=-1)
    return y_ref[...]


def _residual_kernel(h_ref, y_ref, g_ref, o_ref):
    o_ref[...] = h_ref[...].astype(jnp.float32) + g_ref[...] * _moe_out(y_ref)


def _residual_call(h, y, g, tm):
    n = h.shape[0]
    row = pl.BlockSpec((tm, D_MODEL), lambda i: (i, 0))
    return pl.pallas_call(
        _residual_kernel,
        grid=(n // tm,),
        in_specs=[row, pl.BlockSpec((tm, y.shape[1]), lambda i: (i, 0)),
                  pl.BlockSpec((1, D_MODEL), lambda i: (0, 0))],
        out_specs=row,
        out_shape=jax.ShapeDtypeStruct((n, D_MODEL), jnp.float32),
        compiler_params=_cparams(("arbitrary",)),
        name="residual",
    )(h, y, g)


SC_ROWS = 32


SC_CORES = 2
SC_SUBCORES = 16
SC_WORKERS = SC_CORES * SC_SUBCORES


def _sc_mesh():
    return plsc.VectorSubcoreMesh(core_axis_name="core", subcore_axis_name="subcore")


def _sc_worker():
    return lax.axis_index("subcore") * SC_CORES + lax.axis_index("core")


def _scatter_rows(x, dest, n_out):
    n, w = x.shape

    per_worker = n // SC_WORKERS
    assert per_worker % SC_ROWS == 0

    @functools.partial(pl.kernel, out_type=jax.ShapeDtypeStruct((n_out, w), x.dtype), mesh=_sc_mesh(),
                       scratch_types=[pltpu.VMEM((SC_ROWS,), jnp.int32), pltpu.VMEM((SC_ROWS, w), x.dtype)])
    def scatter(x_hbm, i_hbm, o_hbm, idx_v, rows_v):
        first = _sc_worker() * per_worker

        @pl.loop(0, per_worker // SC_ROWS)
        def _(i):
            base = pl.multiple_of(first + i * SC_ROWS, SC_ROWS)
            pltpu.sync_copy(i_hbm.at[pl.ds(base, SC_ROWS)], idx_v)
            pltpu.sync_copy(x_hbm.at[pl.ds(base, SC_ROWS)], rows_v)
            pltpu.sync_copy(rows_v, o_hbm.at[idx_v])

    return scatter(x, dest)


def _gather_rows(x, src):
    n = src.shape[0]
    w = x.shape[1]

    per_worker = n // SC_WORKERS
    assert per_worker % SC_ROWS == 0

    @functools.partial(pl.kernel, out_type=jax.ShapeDtypeStruct((n, w), x.dtype), mesh=_sc_mesh(),
                       scratch_types=[pltpu.VMEM((SC_ROWS,), jnp.int32), pltpu.VMEM((SC_ROWS, w), x.dtype)])
    def gather(x_hbm, i_hbm, o_hbm, idx_v, rows_v):
        first = _sc_worker() * per_worker

        @pl.loop(0, per_worker // SC_ROWS)
        def _(i):
            base = pl.multiple_of(first + i * SC_ROWS, SC_ROWS)
            pltpu.sync_copy(i_hbm.at[pl.ds(base, SC_ROWS)], idx_v)
            pltpu.sync_copy(x_hbm.at[idx_v], rows_v)
            pltpu.sync_copy(rows_v, o_hbm.at[pl.ds(base, SC_ROWS)])

    return gather(x, src)


def _routing_plan(cls, n_tiles):
    onehot = (cls[:, None] == jnp.arange(N_CLASSES, dtype=jnp.int32)[None, :]).astype(jnp.int32)
    counts = jnp.sum(onehot, axis=0)
    rank = jnp.sum((jnp.cumsum(onehot, axis=0) - onehot) * onehot, axis=1)
    tiles = (counts + MOE_TM - 1) // MOE_TM
    tile_end = jnp.cumsum(tiles)
    tile_start = tile_end - tiles
    dest = jnp.sum(onehot * tile_start[None, :], axis=1) * MOE_TM + rank
    nact = tile_end[-1]
    tile = jnp.arange(n_tiles, dtype=jnp.int32)
    blk = jnp.minimum(tile, nact - 1)
    tile_cls = jnp.sum((blk[:, None] >= tile_end[None, :]).astype(jnp.int32), axis=1)
    e1 = jnp.asarray(CLASS_E1)[tile_cls]
    e2 = jnp.asarray(CLASS_E2)[tile_cls]
    valid = counts[tile_cls] - (blk - tile_start[tile_cls]) * MOE_TM
    quarters = jnp.where(tile >= nact, 0, (jnp.minimum(valid, MOE_TM) + MOE_QUARTER - 1) // MOE_QUARTER)
    return dest.astype(jnp.int32), blk, e1, e2, quarters.astype(jnp.int32)


def _row_tile(n):
    return 512 if n % 512 == 0 else 256


def kernel(x, c, ctx, c_ctx, w_ada, b_ada, norm1, w_in, pool_w, pool_scale, q_norm, k_norm, rpb,
           sg_w, sg_b, sg_norm, w_out, norm2, w_router, b_router, w_gate, w_up, w_down):
    depth = w_ada.shape[0]
    n = x.shape[1]
    lc = ctx.shape[1]
    bf = jnp.bfloat16
    lat_stream = (x[0],)
    h_ctx = ctx[0]

    cond = jnp.zeros((SUBLANES, D_MODEL), jnp.float32).at[0].set(c[0]).at[1].set(c_ctx)
    mod = _ada_call(cond, w_ada, b_ada)

    wr_t = w_router.T
    wr_hi = wr_t.astype(bf)
    wr_lo = (wr_t - wr_hi.astype(jnp.float32)).astype(bf)
    wr = jnp.concatenate([wr_hi, wr_lo], axis=0)
    br = b_router.reshape(N_EXPERTS, 1)
    n_tiles = n // MOE_TM + N_CLASSES
    w_in_bf = w_in.astype(bf)
    w_out_bf = w_out.astype(bf)

    for l in range(depth):
        last = l == depth - 1
        qg =(q_norm[l] * (HEAD_DIM ** -0.5 * LOG2_E)).reshape(1, NA_DIM)
        kg = k_norm[l].reshape(1, NA_DIM)
        sgn = sg_norm[l].reshape(1, SG_DIM)
        sgw = sg_w[l].astype(bf).reshape(SG_DIM // LANES, 2 * SG_CHUNK, SG_CHUNK)
        sgb = jnp.broadcast_to(sg_b[l].reshape(SG_DIM // LANES, 2 * SG_CHUNK, 1),
                               (SG_DIM // LANES, 2 * SG_CHUNK, LANES))
        pool_bd = jax.scipy.linalg.block_diag(*[pool_w[l, g] for g in range(len(POOL_WINDOWS))]).astype(bf)
        pscale = pool_scale[l].reshape(1, POOL_DIM)
        bias = _natten_bias(rpb[l])
        n1 = norm1[l].reshape(1, D_MODEL)
        n2 = norm2[l].reshape(1, D_MODEL)

        def mods(row):
            return [mod[l, row:row + 1, i * D_MODEL:(i + 1) * D_MODEL] for i in range(6)]

        sh1, sc1, g1, sh2, sc2, g2 = mods(0)
        csh1, csc1, cg1, csh2, csc2, cg2 = mods(1)

        tc = _row_tile(lc)
        mix_pool_c, q_c, k_c, v_c, sg_c = _inproj_call((h_ctx,), n1, csh1, csc1, w_in_bf, l, qg, kg,
                                                       sgn, sgw, sgb, pool_bd, pscale, tc, strips=False)

        tm = _row_tile(n)
        outs = _inproj_call(lat_stream, n1, sh1, sc1, w_in_bf, l, qg, kg, sgn, sgw, sgb,
                            pool_bd, pscale, tm, strips=True)
        mix_pool, q, k, v, sg = outs[:5]
        h_lat = outs[5] if len(lat_stream) == 3 else lat_stream[0]
        att, wg_l, wu_l, wd_l = _natten_call(q, k, v, k_c, v_c, bias, w_gate, w_up, w_down, l)
        h1, pay, route = _outproj_call(h_lat, mix_pool, att, sg, w_out_bf, l, g1, n2, sh2, sc2, wr, br,
                                       OUTPROJ_TILE if n % OUTPROJ_TILE == 0 else tm)
        cls = route[4].astype(jnp.int32)
        dest, blk, e1, e2, quarters = _routing_plan(cls, n_tiles)
        pay_sorted = _scatter_rows(pay, dest, n_tiles * MOE_TM)
        y_sorted = _moe_sorted_call(blk, e1, e2, quarters, pay_sorted, wg_l, wu_l, wd_l)
        y = _gather_rows(y_sorted, dest)
        lat_stream = (h1, y, g2)

        if not last:
            att_c = _ctxatt_call(q_c, k_c, v_c)
            h1_c, pay_c, _ = _outproj_call(h_ctx, mix_pool_c, att_c, sg_c, w_out_bf, l, cg1, n2, csh2, csc2,
                                           wr, br, tc)
            y_c = _moe_dense_call(pay_c, wg_l, wu_l, wd_l)
            h_ctx = _residual_call(h1_c, y_c, cg2, tc)

    return _residual_call(*lat_stream, _row_tile(n))[None]
```

```python
import functools

import jax
import jax.numpy as jnp
import numpy as np
from jax import lax
from jax.experimental import pallas as pl
from jax.experimental.pallas import tpu as pltpu
from jax.experimental.pallas import tpu_sc as plsc

D_MODEL = 1024
GRID_W = 64
HEAD_DIM = 64
POOL_WINDOWS = (2, 4, 8, 16)
POOL_DIM = 256
NA_HEADS = 8
NA_DIM = 512
NA_WIN_ROWS = 8
NA_WIN_COLS = 16
SG_DIM = 256
SG_CHUNK = 128
Q_OFF = POOL_DIM
K_OFF = Q_OFF + NA_DIM
V_OFF = K_OFF + NA_DIM
U_OFF = V_OFF + NA_DIM
G_OFF = U_OFF + SG_DIM
IN_DIM = G_OFF + SG_DIM
N_EXPERTS = 16
GROUP_SIZE = 4
D_EXPERT = 512
EPS = 1e-6

LANES = 128
SUBLANES = 8
HEAD_PAIRS = NA_DIM // LANES
VMEM_LIMIT = 48 * 1024 * 1024

PAIRS = ((0, 1), (0, 2), (1, 2), (1, 3), (0, 3), (2, 3))
N_CLASSES = (N_EXPERTS // GROUP_SIZE) * len(PAIRS)
CLASS_E1 = np.array([4 * g + i for g in range(4) for (i, j) in PAIRS], np.int32)
CLASS_E2 = np.array([4 * g + j for g in range(4) for (i, j) in PAIRS], np.int32)

ROUTE_ROWS = 8
HALF_D = D_MODEL // 2
PAYLOAD_W = HALF_D + LANES
MOE_TM = 512
NEG_BIG = -1e30
LOG2_E = 1.4426950408889634


def _cparams(sem):
    return pltpu.CompilerParams(dimension_semantics=sem, vmem_limit_bytes=VMEM_LIMIT)


def _dot(a, b):
    return jnp.dot(a, b, preferred_element_type=jnp.float32)


def _dot_nt(a, b):
    return lax.dot_general(a, b, (((1,), (1,)), ((), ())), preferred_element_type=jnp.float32)


def _gelu_tanh(x):
    return 0.5 * x * (1.0 + jnp.tanh(0.7978845608028654 * (x + 0.044715 * (x * x * x))))


def _silu(x):
    return x * (1.0 / (1.0 + jnp.exp(-x)))


def _pack_halves(x):
    w = x.shape[1] // 2
    lo = pltpu.bitcast(x[:, :w].astype(jnp.bfloat16).astype(jnp.float32), jnp.uint32) >> 16
    hi = pltpu.bitcast(x[:, w:].astype(jnp.bfloat16).astype(jnp.float32), jnp.uint32) & jnp.uint32(0xFFFF0000)
    return lo | hi


def _unpack_halves(words):
    lo = pltpu.bitcast(words << 16, jnp.float32)
    hi = pltpu.bitcast(words & jnp.uint32(0xFFFF0000), jnp.float32)
    return lo, hi


def _ada_kernel(cond_ref, w_ref, b_ref, o_ref):
    cond = _silu(cond_ref[...])
    o_ref[...] = jnp.dot(cond, w_ref[...], preferred_element_type=jnp.float32,
                         precision=lax.Precision.HIGHEST) + b_ref[...]


def _ada_call(cond, w_ada, b_ada):
    depth = w_ada.shape[0]
    tn = 1536
    return pl.pallas_call(
        _ada_kernel,
        grid=(depth, 6 * D_MODEL // tn),
        in_specs=[
            pl.BlockSpec((SUBLANES, D_MODEL), lambda l, j: (0, 0)),
            pl.BlockSpec((None, D_MODEL, tn), lambda l, j: (l, 0, j)),
            pl.BlockSpec((None, 1, tn), lambda l, j: (l, 0, j)),
        ],
        out_specs=pl.BlockSpec((None, SUBLANES, tn), lambda l, j: (l, 0, j)),
        out_shape=jax.ShapeDtypeStruct((depth, SUBLANES, 6 * D_MODEL), jnp.float32),
        compiler_params=_cparams(("arbitrary", "arbitrary")),
        name="adaln",
    )(cond, w_ada, b_ada.reshape(depth, 1, 6 * D_MODEL))


def _norm_modulate(x, n_ref, sh_ref, sc_ref):
    ms = jnp.mean(x * x, axis=-1, keepdims=True)
    return ((x * lax.rsqrt(ms + EPS)) * (n_ref[...] * (1.0 + sc_ref[...])) + sh_ref[...]).astype(jnp.bfloat16)


def _head_rms_scale(a):
    low = lax.broadcasted_iota(jnp.int32, (a.shape[0], LANES), 1) < HEAD_DIM
    blocks = []
    for p in range(a.shape[1] // LANES):
        sq = jnp.square(a[:, p * LANES:(p + 1) * LANES])
        s_lo = jnp.sum(jnp.where(low, sq, 0.0), axis=-1, keepdims=True)
        s_hi = jnp.sum(jnp.where(low, 0.0, sq), axis=-1, keepdims=True)
        blocks.append(jnp.where(low, lax.rsqrt(s_lo * (1.0 / HEAD_DIM) + EPS),
                                lax.rsqrt(s_hi * (1.0 / HEAD_DIM) + EPS)))
    return jnp.concatenate(blocks, axis=-1)


STRIP_W = 8
N_STRIPS = GRID_W // STRIP_W


def _store_keys(ref, x):
    if len(ref.shape) == 2:
        ref[...] = x.astype(jnp.bfloat16)
        return
    pair = 2 * STRIP_W
    for s in range(N_STRIPS):
        for rp in range(x.shape[0] // (2 * GRID_W)):
            top = 2 * rp * GRID_W + s * STRIP_W
            rows = jnp.concatenate([x[top:top + STRIP_W], x[top + GRID_W:top + GRID_W + STRIP_W]], axis=0)
            ref[s, rp * pair:(rp + 1) * pair, :] = rows.astype(jnp.bfloat16)


POOL_HALO = 8
HALO_BLOCK = 16
STREAM_DTYPE = jnp.bfloat16


POOL_EDGE = 16


def _pool_mix(xe_ref, w_ref, scale_ref, tm, seq_len):
    low = lax.broadcasted_iota(jnp.int32, (tm, LANES), 1) < HEAD_DIM
    t_edge = pl.program_id(0) * tm + lax.broadcasted_iota(jnp.int32, (POOL_EDGE, LANES), 0)

    def window_mean(s, half):
        mean = s * (1.0 / (2 * half))

        def clip_fix(t):
            count = (jnp.minimum(t + half, seq_len) - jnp.maximum(t - half, 0)).astype(jnp.float32)
            return (2.0 * half) / count

        return jnp.concatenate([mean[:POOL_EDGE] * clip_fix(t_edge),
                                mean[POOL_EDGE:tm - POOL_EDGE],
                                mean[tm - POOL_EDGE:] * clip_fix(t_edge + (tm - POOL_EDGE))], axis=0)

    def window_sums(xs, n_levels):
        sums = []
        s = xs
        for k in range(n_levels):
            step = 1 << k
            s = s[:-step] + s[step:]
            sums.append(s)
        return sums

    outs = []
    for half_block, windows in enumerate(((2, 4), (8, 16))):
        xs = xe_ref[:, half_block * LANES:(half_block + 1) * LANES]
        sums = window_sums(xs, int(np.log2(windows[1])))
        parts = []
        for w in windows:
            half = w // 2
            s = sums[int(np.log2(w)) - 1][POOL_HALO - half:POOL_HALO - half + tm]
            parts.append(window_mean(s, half))
        mean = jnp.where(low, parts[0], parts[1])
        outs.append(mean - xs[POOL_HALO:POOL_HALO + tm])
    d = jnp.concatenate(outs, axis=-1).astype(jnp.bfloat16)
    return (_dot(d, w_ref[...]) * scale_ref[...]).astype(jnp.bfloat16)


def _inproj_kernel(*refs, pending, seq_len):
    n_stream = 7 if pending else 3
    stream, refs = refs[:n_stream], refs[n_stream:]
    if pending:
        h_ref, y_ref, g_ref, hp_ref, hn_ref, yp_ref, yn_ref = stream
        hres_ref, refs = refs[-2], refs[:-2] + refs[-1:]
        x = h_ref[...].astype(jnp.float32) + g_ref[...] * _moe_out(y_ref)
        hres_ref[...] = x.astype(hres_ref.dtype)
        before = hp_ref[...].astype(jnp.float32) + g_ref[...] * _moe_out(yp_ref)
        after = hn_ref[...].astype(jnp.float32) + g_ref[...] * _moe_out(yn_ref)
    else:
        h_ref, hp_ref, hn_ref = stream
        x = h_ref[...].astype(jnp.float32)
        before = hp_ref[...].astype(jnp.float32)
        after = hn_ref[...].astype(jnp.float32)
    x_halo = jnp.concatenate([before[HALO_BLOCK - POOL_HALO:], after[:POOL_HALO]], axis=0)
    (n1_ref, sh_ref, sc_ref, w_ref, qg_ref, kg_ref, sgn_ref, sgw_ref, sgb_ref,
     pw_ref, ps_ref, pool_ref, q_ref, k_ref, v_ref, sg_ref, xe_ref) = refs
    tm = h_ref.shape[0]
    i = pl.program_id(0)
    hn = _norm_modulate(x, n1_ref, sh_ref, sc_ref)
    hn_halo = _norm_modulate(x_halo, n1_ref, sh_ref, sc_ref)

    a_halo = _dot(hn_halo, w_ref[:, 0:Q_OFF])
    xe_ref[0:POOL_HALO, :] = jnp.where(i > 0, a_halo[:POOL_HALO], 0.0)
    xe_ref[POOL_HALO:POOL_HALO + tm, :] = _dot(hn, w_ref[:, 0:Q_OFF])
    xe_ref[POOL_HALO + tm:, :] = jnp.where(i < pl.num_programs(0) - 1, a_halo[POOL_HALO:], 0.0)

    a_g = _dot(hn, w_ref[:, G_OFF:IN_DIM])
    a_u = _dot(hn, w_ref[:, U_OFF:G_OFF])
    a_q = _dot(hn, w_ref[:, Q_OFF:K_OFF])
    a_k = _dot(hn, w_ref[:, K_OFF:V_OFF])
    _store_keys(v_ref, _dot(hn, w_ref[:, V_OFF:U_OFF]))

    gv = _gelu_tanh(a_g)
    q_ref[...] = (a_q * _head_rms_scale(a_q) * qg_ref[...]).astype(jnp.bfloat16)
    _store_keys(k_ref, a_k * _head_rms_scale(a_k) * kg_ref[...])

    u = _gelu_tanh(a_u)
    vn = (gv * _head_rms_scale(gv) * sgn_ref[...]).astype(jnp.bfloat16)
    low = lax.broadcasted_iota(jnp.int32, (SG_CHUNK, LANES), 1) < HEAD_DIM
    for c in range(tm // SG_CHUNK):
        rows = slice(c * SG_CHUNK, (c + 1) * SG_CHUNK)
        for s in range(SG_DIM // LANES):
            cols = slice(s * LANES, (s + 1) * LANES)
            m = _dot(sgw_ref[s], vn[rows, cols]) + sgb_ref[s]
            mixed = jnp.where(low, m[:SG_CHUNK], m[SG_CHUNK:])
            sg_ref[rows, cols] = (u[rows, cols] * mixed).astype(jnp.bfloat16)

    pool_ref[...] = _pool_mix(xe_ref, pw_ref, ps_ref, tm, seq_len)


def _inproj_call(stream, n1, sh, sc, w_in, layer, qg, kg, sgn, sgw, sgb, pool_w, pool_scale, tm, strips):
    pending = len(stream) == 3
    n = stream[0].shape[0]
    if strips:
        kv_spec = pl.BlockSpec((N_STRIPS, tm // N_STRIPS, NA_DIM), lambda i: (0, i, 0))
        kv_shape = jax.ShapeDtypeStruct((N_STRIPS, n // N_STRIPS, NA_DIM), jnp.bfloat16)
    else:
        kv_spec = pl.BlockSpec((tm, NA_DIM), lambda i: (i, 0))
        kv_shape = jax.ShapeDtypeStruct((n, NA_DIM), jnp.bfloat16)
    row = lambda i: (i, 0)
    fixed2 = lambda i: (0, 0)
    fixed3 = lambda i: (0, 0, 0)
    vec = lambda w: pl.BlockSpec((1, w), fixed2)
    rows = pl.BlockSpec((tm, D_MODEL), row)
    per_tile = tm // HALO_BLOCK
    before = lambda i: (jnp.maximum(i * per_tile - 1, 0), 0)
    after = lambda i: (jnp.minimum((i + 1) * per_tile, n // HALO_BLOCK - 1), 0)
    halo = lambda w, m: pl.BlockSpec((HALO_BLOCK, w), m)
    h = stream[0]
    if pending:
        y, g = stream[1], stream[2]
        yw = y.shape[1]
        args = [h, y, g, h, h, y, y]
        stream_specs = [rows, pl.BlockSpec((tm, yw), row), vec(D_MODEL),
                        halo(D_MODEL, before), halo(D_MODEL, after), halo(yw, before), halo(yw, after)]
    else:
        args = [h, h, h]
        stream_specs = [rows, halo(D_MODEL, before), halo(D_MODEL, after)]
    extra_out_specs = [rows] if pending else []
    extra_out_shape = [jax.ShapeDtypeStruct((n, D_MODEL), STREAM_DTYPE)] if pending else []
    return pl.pallas_call(
        functools.partial(_inproj_kernel, pending=pending, seq_len=n),
        grid=(n // tm,),
        in_specs=stream_specs + [
            vec(D_MODEL), vec(D_MODEL), vec(D_MODEL),
            pl.BlockSpec((None, D_MODEL, IN_DIM), lambda i: (layer, 0, 0)),
            vec(NA_DIM), vec(NA_DIM),
            vec(SG_DIM),
            pl.BlockSpec((SG_DIM // LANES, 2 * SG_CHUNK, SG_CHUNK), fixed3),
            pl.BlockSpec((SG_DIM // LANES, 2 * SG_CHUNK, LANES), fixed3),
            pl.BlockSpec((POOL_DIM, POOL_DIM), fixed2),
            vec(POOL_DIM),
        ],
        out_specs=[
            pl.BlockSpec((tm, POOL_DIM), row),
            pl.BlockSpec((tm, NA_DIM), row),
            kv_spec,
            kv_spec,
            pl.BlockSpec((tm, SG_DIM), row),
        ] + extra_out_specs,
        out_shape=[
            jax.ShapeDtypeStruct((n, POOL_DIM), jnp.bfloat16),
            jax.ShapeDtypeStruct((n, NA_DIM), jnp.bfloat16),
            kv_shape,
            kv_shape,
            jax.ShapeDtypeStruct((n, SG_DIM), jnp.bfloat16),
        ] + extra_out_shape,
        scratch_shapes=[pltpu.VMEM((tm + 2 * POOL_HALO, POOL_DIM), jnp.float32)],
        compiler_params=_cparams(("arbitrary",)),
        name="inproj",
    )(*args, n1, sh, sc, w_in, qg, kg, sgn, sgw, sgb, pool_w, pool_scale)


NA_ROWS_PER_BLOCK = 64
NA_GROUP_ROWS = 4
NA_WINDOW_ROWS = NA_GROUP_ROWS + NA_WIN_ROWS
NA_BLOCK = NA_ROWS_PER_BLOCK * GRID_W
NA_GROUP = NA_GROUP_ROWS * GRID_W
NA_HALF_COLS = GRID_W // 2
NA_CHAIN = NA_GROUP_ROWS * NA_HALF_COLS
NA_HALF_STRIPS = 5
NA_HALF_COL0 = (0, GRID_W - NA_HALF_STRIPS * STRIP_W)
NA_RUN = NA_WINDOW_ROWS * STRIP_W
NA_LOCAL = 512
NA_STRIP_BLOCK = NA_ROWS_PER_BLOCK * STRIP_W
NA_STRIP_HALO = (NA_WIN_ROWS // 2) * STRIP_W
NA_EDGE_FIRST, NA_EDGE_NONE, NA_EDGE_LAST = 0, 1, 2


def _stack_heads(x, low):
    zero = jnp.zeros_like(x)
    return jnp.concatenate([jnp.where(low, x, zero), jnp.where(low, zero, x)], axis=0)


def _natten_kernel(q_ref, kp_ref, kc_ref, kn_ref, vp_ref, vc_ref, vn_ref, kx_ref, vx_ref, bias_ref,
                   wg_ref, wu_ref, wd_ref,
                   o_ref, wg_bf_ref, wu_bf_ref, wd_bf_ref, kwin_ref, vwin_ref, vxe_ref, *, grid_rows):
    b = pl.program_id(1)
    wg_bf_ref[...] = wg_ref[...].astype(jnp.bfloat16)
    wu_bf_ref[...] = wu_ref[...].astype(jnp.bfloat16)
    wd_bf_ref[...] = wd_ref[...].astype(jnp.bfloat16)
    top, bottom = NA_STRIP_HALO, NA_STRIP_HALO + NA_STRIP_BLOCK
    kwin_ref[:, 0:top, :] = kp_ref[...]
    kwin_ref[:, top:bottom, :] = kc_ref[...]
    kwin_ref[:, bottom:, :] = kn_ref[...]
    vwin_ref[:, 0:top, 0:LANES] = vp_ref[...]
    vwin_ref[:, top:bottom, 0:LANES] = vc_ref[...]
    vwin_ref[:, bottom:, 0:LANES] = vn_ref[...]
    vwin_ref[:, :, LANES:] = jnp.ones(vwin_ref.shape[:2] + (LANES,), jnp.bfloat16)
    vxe_ref[:, 0:LANES] = vx_ref[...]
    vxe_ref[:, LANES:] = jnp.ones((vxe_ref.shape[0], LANES), jnp.bfloat16)
    low_q = lax.broadcasted_iota(jnp.int32, (NA_CHAIN, LANES), 1) < HEAD_DIM
    n_pad = NA_LOCAL - NA_HALF_STRIPS * NA_RUN

    def window_start(g):
        r0 = b * NA_ROWS_PER_BLOCK + g * NA_GROUP_ROWS
        ws = jnp.clip(r0 - NA_WIN_ROWS // 2, 0, grid_rows - NA_WINDOW_ROWS)
        edge = jnp.where(r0 == 0, NA_EDGE_FIRST,
                         jnp.where(r0 == grid_rows - NA_GROUP_ROWS, NA_EDGE_LAST, NA_EDGE_NONE))
        start = pl.multiple_of((ws - b * NA_ROWS_PER_BLOCK + NA_WIN_ROWS // 2) * STRIP_W, NA_STRIP_HALO)
        return start, edge

    def local_window(win_ref, g, half):
        start, _ = window_start(g)
        s0 = NA_HALF_COL0[half] // STRIP_W
        runs = [win_ref[s, pl.ds(start, NA_RUN), :] for s in range(s0, s0 + NA_HALF_STRIPS)]
        return jnp.concatenate(runs + [jnp.zeros((n_pad, win_ref.shape[2]), jnp.bfloat16)], axis=0)

    def query_rows(g, half, j):
        first = g * NA_GROUP + j * GRID_W + half * NA_HALF_COLS
        return slice(first, first + NA_HALF_COLS)

    def scores(c):
        g, half = divmod(c, 2)
        _, edge = window_start(g)
        qh = jnp.concatenate([q_ref[query_rows(g, half, j), :] for j in range(NA_GROUP_ROWS)], axis=0)
        lhs = _stack_heads(qh, low_q)
        kl = local_window(kwin_ref, g, half)
        return jnp.concatenate([_dot_nt(lhs, kl).astype(jnp.bfloat16) + bias_ref[edge, half],
                                _dot_nt(lhs, kx_ref[...]).astype(jnp.bfloat16)], axis=-1)

    n_chains = 2 * (NA_ROWS_PER_BLOCK // NA_GROUP_ROWS)
    s_next = scores(0)
    for c in range(n_chains):
        s = s_next
        if c + 1 < n_chains:
            s_next = scores(c + 1)
        g, half = divmod(c, 2)
        vl = local_window(vwin_ref, g, half)
        m = jnp.max(s, axis=-1, keepdims=True)
        pb = jnp.exp2(s - m)
        o = _dot(pb[:, :NA_LOCAL], vl) + _dot(pb[:, NA_LOCAL:], vxe_ref[...])
        o = o[:, :LANES] * (1.0 / o[:, LANES:])
        o = jnp.where(low_q, o[:NA_CHAIN], o[NA_CHAIN:]).astype(jnp.bfloat16)
        for j in range(NA_GROUP_ROWS):
            o_ref[query_rows(g, half, j), :] = o[j * NA_HALF_COLS:(j + 1) * NA_HALF_COLS]


def _natten_call(q, k, v, k_ctx, v_ctx, bias, w_gate, w_up, w_down, layer):
    n = q.shape[0]
    grid_rows = n // GRID_W
    assert grid_rows % NA_ROWS_PER_BLOCK == 0 and grid_rows >= 2 * NA_ROWS_PER_BLOCK
    nblk = n // NA_BLOCK
    steps = HEAD_PAIRS * nblk
    depth = w_gate.shape[0]
    up_rows = N_EXPERTS * D_MODEL
    down_rows = N_EXPERTS * D_EXPERT
    assert up_rows % steps == 0 and down_rows % steps == 0
    wg2 = w_gate.reshape(depth * up_rows, D_EXPERT)
    wu2 = w_up.reshape(depth * up_rows, D_EXPERT)
    wd2 = w_down.reshape(depth * down_rows, D_MODEL)
    up_in = pl.BlockSpec((up_rows // steps, D_EXPERT), lambda p, b: (layer * steps + p * nblk + b, 0))
    down_in = pl.BlockSpec((down_rows // steps, D_MODEL), lambda p, b: (layer * steps + p * nblk + b, 0))
    up_out = pl.BlockSpec((up_rows // steps, D_EXPERT), lambda p, b: (p * nblk + b, 0))
    down_out = pl.BlockSpec((down_rows // steps, D_MODEL), lambda p, b: (p * nblk + b, 0))
    n_halo = n // N_STRIPS // NA_STRIP_HALO
    hb = NA_STRIP_BLOCK // NA_STRIP_HALO
    rows = pl.BlockSpec((NA_BLOCK, LANES), lambda p, b: (b, p))
    cur = pl.BlockSpec((N_STRIPS, NA_STRIP_BLOCK, LANES), lambda p, b: (0, b, p))
    prev = pl.BlockSpec((N_STRIPS, NA_STRIP_HALO, LANES), lambda p, b: (0, jnp.maximum(b * hb - 1, 0), p))
    nxt = pl.BlockSpec((N_STRIPS, NA_STRIP_HALO, LANES),
                       lambda p, b: (0, jnp.minimum((b + 1) * hb, n_halo - 1), p))
    ctx = pl.BlockSpec((k_ctx.shape[0], LANES), lambda p, b: (0, p))
    win_rows = NA_STRIP_BLOCK + 2 * NA_STRIP_HALO
    att, wg_bf, wu_bf, wd_bf = pl.pallas_call(
        functools.partial(_natten_kernel, grid_rows=grid_rows),
        grid=(HEAD_PAIRS, nblk),
        in_specs=[rows, prev, cur, nxt, prev, cur, nxt, ctx, ctx,
                  pl.BlockSpec((None, 3, 2, 2 * NA_CHAIN, NA_LOCAL), lambda p, b: (p, 0, 0, 0, 0)),
                  up_in, up_in, down_in],
        out_specs=[rows, up_out, up_out, down_out],
        out_shape=[jax.ShapeDtypeStruct((n, NA_DIM), jnp.bfloat16),
                   jax.ShapeDtypeStruct((up_rows, D_EXPERT), jnp.bfloat16),
                   jax.ShapeDtypeStruct((up_rows, D_EXPERT), jnp.bfloat16),
                   jax.ShapeDtypeStruct((down_rows, D_MODEL), jnp.bfloat16)],
        scratch_shapes=[pltpu.VMEM((N_STRIPS, win_rows, LANES), jnp.bfloat16),
                        pltpu.VMEM((N_STRIPS, win_rows, 2 * LANES), jnp.bfloat16),
                        pltpu.VMEM((k_ctx.shape[0], 2 * LANES), jnp.bfloat16)],
        compiler_params=_cparams(("arbitrary", "arbitrary")),
        name="natten",
    )(q, k, k, k, v, v, v, k_ctx, v_ctx, bias, wg2, wu2, wd2)
    return (att, wg_bf.reshape(N_EXPERTS, D_MODEL, D_EXPERT), wu_bf.reshape(N_EXPERTS, D_MODEL, D_EXPERT),
            wd_bf.reshape(N_EXPERTS, D_EXPERT, D_MODEL))


def _natten_bias(rpb):
    cols = np.arange(GRID_W)
    col_start = np.clip(cols - NA_WIN_COLS // 2, 0, GRID_W - NA_WIN_COLS)
    kc = np.arange(GRID_W)
    in_win = (kc[None, :] >= col_start[:, None]) & (kc[None, :] < col_start[:, None] + NA_WIN_COLS)
    dc = kc[None, :] - cols[:, None] + NA_WIN_COLS - 1
    sel = (np.arange(2 * NA_WIN_COLS - 1)[:, None, None] == dc[None]) & in_win[None]
    t2 = jnp.einsum("hdj,jqk->hdqk", rpb, jnp.asarray(sel, jnp.float32), precision=lax.Precision.HIGHEST)
    t2 = jnp.where(in_win[None, None], t2 * LOG2_E, NEG_BIG)
    neg = jnp.full((NA_HEADS, 1, GRID_W, GRID_W), NEG_BIG, jnp.float32)
    t2e = jnp.concatenate([neg, t2, neg], axis=1)
    u = jnp.concatenate([t2e[:, :-1], t2e[:, 1:]], axis=-1)
    u = u.reshape(HEAD_PAIRS, 2, 2 * NA_WIN_ROWS, GRID_W, LANES)

    place = np.zeros((2, NA_WINDOW_ROWS * GRID_W, NA_LOCAL), np.float32)
    for half, c0 in enumerate(NA_HALF_COL0):
        for a in range(NA_WINDOW_ROWS):
            for kcol in range(c0, c0 + NA_HALF_STRIPS * STRIP_W):
                s, c8 = divmod(kcol - c0, STRIP_W)
                place[half, a * GRID_W + kcol, s * NA_RUN + a * STRIP_W + c8] = 1.0
    outside = np.full((3, NA_GROUP_ROWS, NA_LOCAL), NEG_BIG, np.float32)
    for edge in (NA_EDGE_FIRST, NA_EDGE_NONE, NA_EDGE_LAST):
        for j in range(NA_GROUP_ROWS):
            lo, _ = _window_rows(edge, j)
            for s in range(NA_HALF_STRIPS):
                outside[edge, j, s * NA_RUN + lo * STRIP_W:s * NA_RUN + (lo + NA_WIN_ROWS) * STRIP_W] = 0.0
    return pl.pallas_call(
        _bias_expand_kernel,
        grid=(HEAD_PAIRS,),
        in_specs=[pl.BlockSpec((None, 2, 2 * NA_WIN_ROWS, GRID_W, LANES), lambda p: (p, 0, 0, 0, 0)),
                  pl.BlockSpec(place.shape, lambda p: (0, 0, 0)),
                  pl.BlockSpec(outside.shape, lambda p: (0, 0, 0))],
        out_specs=pl.BlockSpec((None, 3, 2, 2 * NA_CHAIN, NA_LOCAL), lambda p: (p, 0, 0, 0, 0)),
        out_shape=jax.ShapeDtypeStruct((HEAD_PAIRS, 3, 2, 2 * NA_CHAIN, NA_LOCAL), jnp.bfloat16),
        compiler_params=_cparams(("arbitrary",)),
        name="bias_expand",
    )(u, jnp.asarray(place, jnp.bfloat16), jnp.asarray(outside))


def _window_rows(edge, j):
    if edge == NA_EDGE_FIRST:
        return 0, NA_WIN_ROWS - 1 - j
    if edge == NA_EDGE_NONE:
        return j, NA_WIN_ROWS // 2 - 1
    return NA_WINDOW_ROWS - NA_WIN_ROWS, NA_WIN_ROWS // 2 - 1 - j


def _bias_expand_kernel(u_ref, place_ref, outside_ref, o_ref):
    low = lax.broadcasted_iota(jnp.int32, (NA_HALF_COLS, LANES), 1) < GRID_W
    zero = jnp.zeros((NA_HALF_COLS, LANES), jnp.float32)
    for edge in (NA_EDGE_FIRST, NA_EDGE_NONE, NA_EDGE_LAST):
        for half in range(2):
            q0 = half * NA_HALF_COLS
            blocks, masks = [], []
            for hd in range(2):
                for j in range(NA_GROUP_ROWS):
                    lo, base = _window_rows(edge, j)
                    tiles = []
                    for i in range(NA_WINDOW_ROWS // 2):
                        a0, a1 = 2 * i, 2 * i + 1
                        ok0 = lo <= a0 < lo + NA_WIN_ROWS
                        ok1 = lo <= a1 < lo + NA_WIN_ROWS
                        if not (ok0 or ok1):
                            tile = zero
                        else:
                            tile = u_ref[hd, base + a1 - lo, q0:q0 + NA_HALF_COLS, :]
                            if not ok0:
                                tile = jnp.where(low, zero, tile)
                            if not ok1:
                                tile = jnp.where(low, tile, zero)
                        tiles.append(tile)
                    blocks.append(jnp.concatenate(tiles, axis=-1))
                    masks.append(jnp.broadcast_to(outside_ref[edge, j:j + 1, :], (NA_HALF_COLS, NA_LOCAL)))
            lhs = jnp.concatenate(blocks, axis=0).astype(jnp.bfloat16)
            placed = _dot(lhs, place_ref[half]) + jnp.concatenate(masks, axis=0)
            o_ref[edge, half] = placed.astype(o_ref.dtype)


def _ctxatt_kernel(q_ref, k_ref, v_ref, o_ref):
    lc = q_ref.shape[0]
    low = lax.broadcasted_iota(jnp.int32, (lc, LANES), 1) < HEAD_DIM
    lhs = _stack_heads(q_ref[...], low)
    s = _dot_nt(lhs, k_ref[...])
    m = jnp.max(s, axis=-1, keepdims=True)
    p = jnp.exp2(s - m)
    denom = jnp.sum(p, axis=-1, keepdims=True)
    o = _dot(p.astype(jnp.bfloat16), v_ref[...]) * (1.0 / denom)
    o_ref[...] = jnp.where(low, o[:lc], o[lc:]).astype(jnp.bfloat16)


def _ctxatt_call(q, k, v):
    lc = q.shape[0]
    spec = pl.BlockSpec((lc, LANES), lambda p: (0, p))
    return pl.pallas_call(
        _ctxatt_kernel,
        grid=(HEAD_PAIRS,),
        in_specs=[spec, spec, spec],
        out_specs=spec,
        out_shape=jax.ShapeDtypeStruct((lc, NA_DIM), jnp.bfloat16),
        compiler_params=_cparams(("arbitrary",)),
        name="ctxatt",
    )(q, k, v)


OUTPROJ_CHAIN = 256
OUTPROJ_TILE = 1024


def _outproj_kernel(h_ref, mp_ref, att_ref, sg_ref, wo_ref, g1_ref, n2_ref, sh_ref, sc_ref,
                    wr_ref, br_ref, h1_ref, pay_ref, route_ref):
    chains = [slice(c * OUTPROJ_CHAIN, (c + 1) * OUTPROJ_CHAIN) for c in range(h_ref.shape[0] // OUTPROJ_CHAIN)]
    h1s = []
    for rows in chains:
        mix = (_dot(mp_ref[rows, :], wo_ref[0:POOL_DIM, :])
               + _dot(att_ref[rows, :], wo_ref[POOL_DIM:POOL_DIM + NA_DIM, :])
               + _dot(sg_ref[rows, :], wo_ref[POOL_DIM + NA_DIM:, :]))
        h1 = h_ref[rows, :].astype(jnp.float32) + g1_ref[...] * mix
        h1_ref[rows, :] = h1.astype(h1_ref.dtype)
        h1s.append(h1)
    for rows, h1 in zip(chains, h1s):
        _outproj_route(rows, h1, n2_ref, sh_ref, sc_ref, wr_ref, br_ref, pay_ref, route_ref)


def _outproj_route(rows, h1, n2_ref, sh_ref, sc_ref, wr_ref, br_ref, pay_ref, route_ref):
    tm = OUTPROJ_CHAIN
    ms = jnp.mean(h1 * h1, axis=-1, keepdims=True)
    hm = (h1 * lax.rsqrt(ms + EPS)) * (n2_ref[...] * (1.0 + sc_ref[...])) + sh_ref[...]
    pay_ref[rows, 0:HALF_D] = _pack_halves(hm)

    hm_hi = hm.astype(jnp.bfloat16)
    lt = _dot_nt(wr_ref[...], hm_hi)
    logits = lt[:N_EXPERTS] + lt[N_EXPERTS:] + br_ref[...]
    e = jnp.exp(logits - jnp.max(logits, axis=0, keepdims=True))

    best = ga = gb = e1 = e2 = cls = None
    for c in range(N_CLASSES):
        a, b2 = int(CLASS_E1[c]), int(CLASS_E2[c])
        ea, eb = e[a:a + 1, :], e[b2:b2 + 1, :]
        s = ea + eb
        if best is None:
            best, ga, gb = s, ea, eb
            e1 = jnp.full_like(s, float(a))
            e2 = jnp.full_like(s, float(b2))
            cls = jnp.zeros_like(s)
        else:
            better = s > best
            best = jnp.where(better, s, best)
            ga = jnp.where(better, ea, ga)
            gb = jnp.where(better, eb, gb)
            e1 = jnp.where(better, float(a), e1)
            e2 = jnp.where(better, float(b2), e2)
            cls = jnp.where(better, float(c), cls)
    inv = 1.0 / best
    row = lax.broadcasted_iota(jnp.int32, (ROUTE_ROWS, tm), 0)
    rec = jnp.where(row == 0, ga * inv,
          jnp.where(row == 1, gb * inv,
          jnp.where(row == 2, e1,
          jnp.where(row == 3, e2,
          jnp.where(row == 4, cls, 0.0)))))
    route_ref[:, rows] = rec
    wide = jnp.concatenate([rec, jnp.zeros((LANES - ROUTE_ROWS, tm), jnp.float32)], axis=0)
    pay_ref[rows, HALF_D:] = pltpu.bitcast(wide.T, jnp.uint32)


def _outproj_call(h, mp, att, sg, w_out, layer, g1, n2, sh2, sc2, wr, br, tm):
    n = h.shape[0]
    row = lambda i: (i, 0)
    fixed = lambda i: (0, 0)
    vec = pl.BlockSpec((1, D_MODEL), fixed)
    return pl.pallas_call(
        _outproj_kernel,
        grid=(n // tm,),
        in_specs=[
            pl.BlockSpec((tm, D_MODEL), row),
            pl.BlockSpec((tm, POOL_DIM), row),
            pl.BlockSpec((tm, NA_DIM), row),
            pl.BlockSpec((tm, SG_DIM), row),
            pl.BlockSpec((None, D_MODEL, D_MODEL), lambda i: (layer, 0, 0)),
            vec, vec, vec, vec,
            pl.BlockSpec((2 * N_EXPERTS, D_MODEL), fixed),
            pl.BlockSpec((N_EXPERTS, 1), fixed),
        ],
        out_specs=[
            pl.BlockSpec((tm, D_MODEL), row),
            pl.BlockSpec((tm, PAYLOAD_W), row),
            pl.BlockSpec((ROUTE_ROWS, tm), lambda i: (0, i)),
        ],
        out_shape=[
            jax.ShapeDtypeStruct((n, D_MODEL), STREAM_DTYPE),
            jax.ShapeDtypeStruct((n, PAYLOAD_W), jnp.uint32),
            jax.ShapeDtypeStruct((ROUTE_ROWS, n), jnp.float32),
        ],
        compiler_params=_cparams(("arbitrary",)),
        name="outproj",
    )(h, mp, att, sg, w_out, g1, n2, sh2, sc2, wr, br)


def _payload_parts(pay_ref, rows=slice(None)):
    lo, hi = _unpack_halves(pay_ref[rows, 0:HALF_D])
    x = jnp.concatenate([lo, hi], axis=-1).astype(jnp.bfloat16)
    return x, pltpu.bitcast(pay_ref[rows, HALF_D:], jnp.float32)


def _expert_pair(x, ga, gb, wga, wua, wda, wgb, wub, wdb):
    ha = (_silu(_dot(x, wga)) * _dot(x, wua) * ga).astype(jnp.bfloat16)
    hb = (_silu(_dot(x, wgb)) * _dot(x, wub) * gb).astype(jnp.bfloat16)
    return _dot(ha, wda) + _dot(hb, wdb)


MOE_QUARTER = MOE_TM // 4


def _moe_sorted_kernel(blk_ref, e1_ref, e2_ref, quarters_ref, pay_ref,
                       wga_ref, wua_ref, wda_ref, wgb_ref, wub_ref, wdb_ref, o_ref):
    quarters = quarters_ref[pl.program_id(0)]

    def run(rows):
        x, route = _payload_parts(pay_ref, rows)
        y = _expert_pair(x, route[:, 0:1], route[:, 1:2], wga_ref[...], wua_ref[...], wda_ref[...],
                         wgb_ref[...], wub_ref[...], wdb_ref[...])
        o_ref[rows, :] = _pack_halves(y)

    for used in range(1, MOE_TM // MOE_QUARTER + 1):
        @pl.when(quarters == used)
        def _(used=used):
            run(slice(0, used * MOE_QUARTER))


def _moe_sorted_call(blk, e1, e2, quarters, pay_sorted, wg, wu, wd):
    n_tiles = blk.shape[0]
    rows = lambda i, blk, e1, e2, quarters: (blk[i], 0)
    wa = lambda i, blk, e1, e2, quarters: (e1[i], 0, 0)
    wb = lambda i, blk, e1, e2, quarters: (e2[i], 0, 0)
    up = lambda m: pl.BlockSpec((None, D_MODEL, D_EXPERT), m)
    down = lambda m: pl.BlockSpec((None, D_EXPERT, D_MODEL), m)
    return pl.pallas_call(
        _moe_sorted_kernel,
        grid_spec=pltpu.PrefetchScalarGridSpec(
            num_scalar_prefetch=4,
            grid=(n_tiles,),
            in_specs=[pl.BlockSpec((MOE_TM, PAYLOAD_W), rows),
                      up(wa), up(wa), down(wa), up(wb), up(wb), down(wb)],
            out_specs=pl.BlockSpec((MOE_TM, HALF_D), rows),
        ),
        out_shape=jax.ShapeDtypeStruct((n_tiles * MOE_TM, HALF_D), jnp.uint32),
        compiler_params=_cparams(("arbitrary",)),
        name="moe_sorted",
    )(blk, e1, e2, quarters, pay_sorted, wg, wu, wd, wg, wu, wd)


def _moe_dense_kernel(pay_ref, wg_ref, wu_ref, wd_ref, o_ref):
    e = pl.program_id(0)

    @pl.when(e == 0)
    def _():
        o_ref[...] = jnp.zeros_like(o_ref)

    x, route = _payload_parts(pay_ref)
    ef = e.astype(jnp.float32)
    gate = (jnp.where(route[:, 2:3] == ef, route[:, 0:1], 0.0)
            + jnp.where(route[:, 3:4] == ef, route[:, 1:2], 0.0))
    he = (_silu(_dot(x, wg_ref[...])) * _dot(x, wu_ref[...]) * gate).astype(jnp.bfloat16)
    o_ref[...] += _dot(he, wd_ref[...])


def _moe_dense_call(pay, wg, wu, wd):
    n = pay.shape[0]
    return pl.pallas_call(
        _moe_dense_kernel,
        grid=(N_EXPERTS,),
        in_specs=[pl.BlockSpec((n, PAYLOAD_W), lambda e: (0, 0)),
                  pl.BlockSpec((None, D_MODEL, D_EXPERT), lambda e: (e, 0, 0)),
                  pl.BlockSpec((None, D_MODEL, D_EXPERT), lambda e: (e, 0, 0)),
                  pl.BlockSpec((None, D_EXPERT, D_MODEL), lambda e: (e, 0, 0))],
        out_specs=pl.BlockSpec((n, D_MODEL), lambda e: (0, 0)),
        out_shape=jax.ShapeDtypeStruct((n, D_MODEL), jnp.float32),
        compiler_params=_cparams(("arbitrary",)),
        name="moe_dense",
    )(pay, wg, wu, wd)


def _moe_out(y_ref):
    if y_ref.dtype == jnp.uint32:
        return jnp.concatenate(_unpack_halves(y_ref[...]), axis=-1)
    return y_ref[...]


def _residual_kernel(h_ref, y_ref, g_ref, o_ref):
    o_ref[...] = h_ref[...].astype(jnp.float32) + g_ref[...] * _moe_out(y_ref)


def _residual_call(h, y, g, tm):
    n = h.shape[0]
    row = pl.BlockSpec((tm, D_MODEL), lambda i: (i, 0))
    return pl.pallas_call(
        _residual_kernel,
        grid=(n // tm,),
        in_specs=[row, pl.BlockSpec((tm, y.shape[1]), lambda i: (i, 0)),
                  pl.BlockSpec((1, D_MODEL), lambda i: (0, 0))],
        out_specs=row,
        out_shape=jax.ShapeDtypeStruct((n, D_MODEL), jnp.float32),
        compiler_params=_cparams(("arbitrary",)),
        name="residual",
    )(h, y, g)


SC_ROWS = 128


SC_CORES = 2
SC_SUBCORES = 16
SC_WORKERS = SC_CORES * SC_SUBCORES


def _sc_mesh():
    return plsc.VectorSubcoreMesh(core_axis_name="core", subcore_axis_name="subcore")


def _sc_worker():
    return lax.axis_index("subcore") * SC_CORES + lax.axis_index("core")


def _scatter_rows(x, dest, n_out):
    n, w = x.shape

    per_worker = n // SC_WORKERS
    assert per_worker % SC_ROWS == 0

    @functools.partial(pl.kernel, out_type=jax.ShapeDtypeStruct((n_out, w), x.dtype), mesh=_sc_mesh(),
                       scratch_types=[pltpu.VMEM((SC_ROWS,), jnp.int32), pltpu.VMEM((SC_ROWS, w), x.dtype)])
    def scatter(x_hbm, i_hbm, o_hbm, idx_v, rows_v):
        first = _sc_worker() * per_worker

        @pl.loop(0, per_worker // SC_ROWS)
        def _(i):
            base = pl.multiple_of(first + i * SC_ROWS, SC_ROWS)
            pltpu.sync_copy(i_hbm.at[pl.ds(base, SC_ROWS)], idx_v)
            pltpu.sync_copy(x_hbm.at[pl.ds(base, SC_ROWS)], rows_v)
            pltpu.sync_copy(rows_v, o_hbm.at[idx_v])

    return scatter(x, dest)


def _gather_rows(x, src):
    n = src.shape[0]
    w = x.shape[1]

    per_worker = n // SC_WORKERS
    assert per_worker % SC_ROWS == 0

    @functools.partial(pl.kernel, out_type=jax.ShapeDtypeStruct((n, w), x.dtype), mesh=_sc_mesh(),
                       scratch_types=[pltpu.VMEM((SC_ROWS,), jnp.int32), pltpu.VMEM((SC_ROWS, w), x.dtype)])
    def gather(x_hbm, i_hbm, o_hbm, idx_v, rows_v):
        first = _sc_worker() * per_worker

        @pl.loop(0, per_worker // SC_ROWS)
        def _(i):
            base = pl.multiple_of(first + i * SC_ROWS, SC_ROWS)
            pltpu.sync_copy(i_hbm.at[pl.ds(base, SC_ROWS)], idx_v)
            pltpu.sync_copy(x_hbm.at[idx_v], rows_v)
            pltpu.sync_copy(rows_v, o_hbm.at[pl.ds(base, SC_ROWS)])

    return gather(x, src)


def _routing_plan(cls, n_tiles):
    onehot = (cls[:, None] == jnp.arange(N_CLASSES, dtype=jnp.int32)[None, :]).astype(jnp.int32)
    counts = jnp.sum(onehot, axis=0)
    rank = jnp.sum((jnp.cumsum(onehot, axis=0) - onehot) * onehot, axis=1)
    tiles = (counts + MOE_TM - 1) // MOE_TM
    tile_end = jnp.cumsum(tiles)
    tile_start = tile_end - tiles
    dest = jnp.sum(onehot * tile_start[None, :], axis=1) * MOE_TM + rank
    nact = tile_end[-1]
    tile = jnp.arange(n_tiles, dtype=jnp.int32)
    blk = jnp.minimum(tile, nact - 1)
    tile_cls = jnp.sum((blk[:, None] >= tile_end[None, :]).astype(jnp.int32), axis=1)
    e1 = jnp.asarray(CLASS_E1)[tile_cls]
    e2 = jnp.asarray(CLASS_E2)[tile_cls]
    valid = counts[tile_cls] - (blk - tile_start[tile_cls]) * MOE_TM
    quarters = jnp.where(tile >= nact, 0, (jnp.minimum(valid, MOE_TM) + MOE_QUARTER - 1) // MOE_QUARTER)
    return dest.astype(jnp.int32), blk, e1, e2, quarters.astype(jnp.int32)


def _row_tile(n):
    return 512 if n % 512 == 0 else 256


def kernel(x, c, ctx, c_ctx, w_ada, b_ada, norm1, w_in, pool_w, pool_scale, q_norm, k_norm, rpb,
           sg_w, sg_b, sg_norm, w_out, norm2, w_router, b_router, w_gate, w_up, w_down):
    depth = w_ada.shape[0]
    n = x.shape[1]
    lc = ctx.shape[1]
    bf = jnp.bfloat16
    lat_stream = (x[0],)
    h_ctx = ctx[0]

    cond = jnp.zeros((SUBLANES, D_MODEL), jnp.float32).at[0].set(c[0]).at[1].set(c_ctx)
    mod = _ada_call(cond, w_ada, b_ada)

    wr_t = w_router.T
    wr_hi = wr_t.astype(bf)
    wr_lo = (wr_t - wr_hi.astype(jnp.float32)).astype(bf)
    wr = jnp.concatenate([wr_hi, wr_lo], axis=0)
    br = b_router.reshape(N_EXPERTS, 1)
    n_tiles = n // MOE_TM + N_CLASSES
    w_in_bf = w_in.astype(bf)
    w_out_bf = w_out.astype(bf)

    for l in range(depth):
        last = l == depth - 1
        qg =(q_norm[l] * (HEAD_DIM ** -0.5 * LOG2_E)).reshape(1, NA_DIM)
        kg = k_norm[l].reshape(1, NA_DIM)
        sgn = sg_norm[l].reshape(1, SG_DIM)
        sgw = sg_w[l].astype(bf).reshape(SG_DIM // LANES, 2 * SG_CHUNK, SG_CHUNK)
        sgb = jnp.broadcast_to(sg_b[l].reshape(SG_DIM // LANES, 2 * SG_CHUNK, 1),
                               (SG_DIM // LANES, 2 * SG_CHUNK, LANES))
        pool_bd = jax.scipy.linalg.block_diag(*[pool_w[l, g] for g in range(len(POOL_WINDOWS))]).astype(bf)
        pscale = pool_scale[l].reshape(1, POOL_DIM)
        bias = _natten_bias(rpb[l])
        n1 = norm1[l].reshape(1, D_MODEL)
        n2 = norm2[l].reshape(1, D_MODEL)

        def mods(row):
            return [mod[l, row:row + 1, i * D_MODEL:(i + 1) * D_MODEL] for i in range(6)]

        sh1, sc1, g1, sh2, sc2, g2 = mods(0)
        csh1, csc1, cg1, csh2, csc2, cg2 = mods(1)

        tc = _row_tile(lc)
        mix_pool_c, q_c, k_c, v_c, sg_c = _inproj_call((h_ctx,), n1, csh1, csc1, w_in_bf, l, qg, kg,
                                                       sgn, sgw, sgb, pool_bd, pscale, tc, strips=False)

        tm = _row_tile(n)
        outs = _inproj_call(lat_stream, n1, sh1, sc1, w_in_bf, l, qg, kg, sgn, sgw, sgb,
                            pool_bd, pscale, tm, strips=True)
        mix_pool, q, k, v, sg = outs[:5]
        h_lat = outs[5] if len(lat_stream) == 3 else lat_stream[0]
        att, wg_l, wu_l, wd_l = _natten_call(q, k, v, k_c, v_c, bias, w_gate, w_up, w_down, l)
        h1, pay, route = _outproj_call(h_lat, mix_pool, att, sg, w_out_bf, l, g1, n2, sh2, sc2, wr, br,
                                       OUTPROJ_TILE if n % OUTPROJ_TILE == 0 else tm)
        cls = route[4].astype(jnp.int32)
        dest, blk, e1, e2, quarters = _routing_plan(cls, n_tiles)
        pay_sorted = _scatter_rows(pay, dest, n_tiles * MOE_TM)
        y_sorted = _moe_sorted_call(blk, e1, e2, quarters, pay_sorted, wg_l, wu_l, wd_l)
        y = _gather_rows(y_sorted, dest)
        lat_stream = (h1, y, g2)

        if not last:
            att_c = _ctxatt_call(q_c, k_c, v_c)
            h1_c, pay_c, _ = _outproj_call(h_ctx, mix_pool_c, att_c, sg_c, w_out_bf, l, cg1, n2, csh2, csc2,
                                           wr, br, tc)
            y_c = _moe_dense_call(pay_c, wg_l, wu_l, wd_l)
            h_ctx = _residual_call(h1_c, y_c, cg2, tc)

    return _residual_call(*lat_stream, _row_tile(n))[None]
```

```python
import functools

import jax
import jax.numpy as jnp
import numpy as np
from jax import lax
from jax.experimental import pallas as pl
from jax.experimental.pallas import tpu as pltpu
from jax.experimental.pallas import tpu_sc as plsc

D_MODEL = 1024
GRID_W = 64
HEAD_DIM = 64
POOL_WINDOWS = (2, 4, 8, 16)
POOL_DIM = 256
NA_HEADS = 8
NA_DIM = 512
NA_WIN_ROWS = 8
NA_WIN_COLS = 16
SG_DIM = 256
SG_CHUNK = 128
Q_OFF = POOL_DIM
K_OFF = Q_OFF + NA_DIM
V_OFF = K_OFF + NA_DIM
U_OFF = V_OFF + NA_DIM
G_OFF = U_OFF + SG_DIM
IN_DIM = G_OFF + SG_DIM
N_EXPERTS = 16
GROUP_SIZE = 4
D_EXPERT = 512
EPS = 1e-6

LANES = 128
SUBLANES = 8
HEAD_PAIRS = NA_DIM // LANES
VMEM_LIMIT = 48 * 1024 * 1024

PAIRS = ((0, 1), (0, 2), (1, 2), (1, 3), (0, 3), (2, 3))
N_CLASSES = (N_EXPERTS // GROUP_SIZE) * len(PAIRS)
CLASS_E1 = np.array([4 * g + i for g in range(4) for (i, j) in PAIRS], np.int32)
CLASS_E2 = np.array([4 * g + j for g in range(4) for (i, j) in PAIRS], np.int32)

ROUTE_ROWS = 8
HALF_D = D_MODEL // 2
PAYLOAD_W = HALF_D + LANES
MOE_TM = 512
NEG_BIG = -1e30
LOG2_E = 1.4426950408889634


def _cparams(sem):
    return pltpu.CompilerParams(dimension_semantics=sem, vmem_limit_bytes=VMEM_LIMIT)


def _dot(a, b):
    return jnp.dot(a, b, preferred_element_type=jnp.float32)


def _dot_nt(a, b):
    return lax.dot_general(a, b, (((1,), (1,)), ((), ())), preferred_element_type=jnp.float32)


def _gelu_tanh(x):
    return 0.5 * x * (1.0 + jnp.tanh(0.7978845608028654 * (x + 0.044715 * (x * x * x))))


def _silu(x):
    return x * (1.0 / (1.0 + jnp.exp(-x)))


def _pack_halves(x):
    w = x.shape[1] // 2
    lo = pltpu.bitcast(x[:, :w].astype(jnp.bfloat16).astype(jnp.float32), jnp.uint32) >> 16
    hi = pltpu.bitcast(x[:, w:].astype(jnp.bfloat16).astype(jnp.float32), jnp.uint32) & jnp.uint32(0xFFFF0000)
    return lo | hi


def _unpack_halves(words):
    lo = pltpu.bitcast(words << 16, jnp.float32)
    hi = pltpu.bitcast(words & jnp.uint32(0xFFFF0000), jnp.float32)
    return lo, hi


def _ada_kernel(cond_ref, w_ref, b_ref, o_ref):
    cond = _silu(cond_ref[...])
    o_ref[...] = jnp.dot(cond, w_ref[...], preferred_element_type=jnp.float32,
                         precision=lax.Precision.HIGHEST) + b_ref[...]


def _ada_call(cond, w_ada, b_ada):
    depth = w_ada.shape[0]
    tn = 1536
    return pl.pallas_call(
        _ada_kernel,
        grid=(depth, 6 * D_MODEL // tn),
        in_specs=[
            pl.BlockSpec((SUBLANES, D_MODEL), lambda l, j: (0, 0)),
            pl.BlockSpec((None, D_MODEL, tn), lambda l, j: (l, 0, j)),
            pl.BlockSpec((None, 1, tn), lambda l, j: (l, 0, j)),
        ],
        out_specs=pl.BlockSpec((None, SUBLANES, tn), lambda l, j: (l, 0, j)),
        out_shape=jax.ShapeDtypeStruct((depth, SUBLANES, 6 * D_MODEL), jnp.float32),
        compiler_params=_cparams(("arbitrary", "arbitrary")),
        name="adaln",
    )(cond, w_ada, b_ada.reshape(depth, 1, 6 * D_MODEL))


def _norm_modulate(x, n_ref, sh_ref, sc_ref):
    ms = jnp.mean(x * x, axis=-1, keepdims=True)
    return ((x * lax.rsqrt(ms + EPS)) * (n_ref[...] * (1.0 + sc_ref[...])) + sh_ref[...]).astype(jnp.bfloat16)


def _head_rms_scale(a):
    low = lax.broadcasted_iota(jnp.int32, (a.shape[0], LANES), 1) < HEAD_DIM
    blocks = []
    for p in range(a.shape[1] // LANES):
        sq = jnp.square(a[:, p * LANES:(p + 1) * LANES])
        s_lo = jnp.sum(jnp.where(low, sq, 0.0), axis=-1, keepdims=True)
        s_hi = jnp.sum(jnp.where(low, 0.0, sq), axis=-1, keepdims=True)
        blocks.append(jnp.where(low, lax.rsqrt(s_lo * (1.0 / HEAD_DIM) + EPS),
                                lax.rsqrt(s_hi * (1.0 / HEAD_DIM) + EPS)))
    return jnp.concatenate(blocks, axis=-1)


STRIP_W = 8
N_STRIPS = GRID_W // STRIP_W


def _store_keys(ref, x):
    if len(ref.shape) == 2:
        ref[...] = x.astype(jnp.bfloat16)
        return
    pair = 2 * STRIP_W
    for s in range(N_STRIPS):
        for rp in range(x.shape[0] // (2 * GRID_W)):
            top = 2 * rp * GRID_W + s * STRIP_W
            rows = jnp.concatenate([x[top:top + STRIP_W], x[top + GRID_W:top + GRID_W + STRIP_W]], axis=0)
            ref[s, rp * pair:(rp + 1) * pair, :] = rows.astype(jnp.bfloat16)


POOL_HALO = 8
HALO_BLOCK = 16
STREAM_DTYPE = jnp.bfloat16


POOL_EDGE = 16


def _pool_mix(xe_ref, w_ref, scale_ref, tm, seq_len):
    low = lax.broadcasted_iota(jnp.int32, (tm, LANES), 1) < HEAD_DIM
    t_edge = pl.program_id(0) * tm + lax.broadcasted_iota(jnp.int32, (POOL_EDGE, LANES), 0)

    def window_mean(s, half):
        mean = s * (1.0 / (2 * half))

        def clip_fix(t):
            count = (jnp.minimum(t + half, seq_len) - jnp.maximum(t - half, 0)).astype(jnp.float32)
            return (2.0 * half) / count

        return jnp.concatenate([mean[:POOL_EDGE] * clip_fix(t_edge),
                                mean[POOL_EDGE:tm - POOL_EDGE],
                                mean[tm - POOL_EDGE:] * clip_fix(t_edge + (tm - POOL_EDGE))], axis=0)

    def window_sums(xs, n_levels):
        sums = []
        s = xs
        for k in range(n_levels):
            step = 1 << k
            s = s[:-step] + s[step:]
            sums.append(s)
        return sums

    outs = []
    for half_block, windows in enumerate(((2, 4), (8, 16))):
        xs = xe_ref[:, half_block * LANES:(half_block + 1) * LANES]
        sums = window_sums(xs, int(np.log2(windows[1])))
        parts = []
        for w in windows:
            half = w // 2
            s = sums[int(np.log2(w)) - 1][POOL_HALO - half:POOL_HALO - half + tm]
            parts.append(window_mean(s, half))
        mean = jnp.where(low, parts[0], parts[1])
        outs.append(mean - xs[POOL_HALO:POOL_HALO + tm])
    d = jnp.concatenate(outs, axis=-1).astype(jnp.bfloat16)
    return (_dot(d, w_ref[...]) * scale_ref[...]).astype(jnp.bfloat16)


def _inproj_kernel(*refs, pending, seq_len):
    n_stream = 7 if pending else 3
    stream, refs = refs[:n_stream], refs[n_stream:]
    if pending:
        h_ref, y_ref, g_ref, hp_ref, hn_ref, yp_ref, yn_ref = stream
        hres_ref, refs = refs[-2], refs[:-2] + refs[-1:]
        x = h_ref[...].astype(jnp.float32) + g_ref[...] * _moe_out(y_ref)
        hres_ref[...] = x.astype(hres_ref.dtype)
        before = hp_ref[...].astype(jnp.float32) + g_ref[...] * _moe_out(yp_ref)
        after = hn_ref[...].astype(jnp.float32) + g_ref[...] * _moe_out(yn_ref)
    else:
        h_ref, hp_ref, hn_ref = stream
        x = h_ref[...].astype(jnp.float32)
        before = hp_ref[...].astype(jnp.float32)
        after = hn_ref[...].astype(jnp.float32)
    x_halo = jnp.concatenate([before[HALO_BLOCK - POOL_HALO:], after[:POOL_HALO]], axis=0)
    (n1_ref, sh_ref, sc_ref, w_ref, qg_ref, kg_ref, sgn_ref, sgw_ref, sgb_ref,
     pw_ref, ps_ref, pool_ref, q_ref, k_ref, v_ref, sg_ref, xe_ref) = refs
    tm = h_ref.shape[0]
    i = pl.program_id(0)
    hn = _norm_modulate(x, n1_ref, sh_ref, sc_ref)
    hn_halo = _norm_modulate(x_halo, n1_ref, sh_ref, sc_ref)

    a_halo = _dot(hn_halo, w_ref[:, 0:Q_OFF])
    xe_ref[0:POOL_HALO, :] = jnp.where(i > 0, a_halo[:POOL_HALO], 0.0)
    xe_ref[POOL_HALO:POOL_HALO + tm, :] = _dot(hn, w_ref[:, 0:Q_OFF])
    xe_ref[POOL_HALO + tm:, :] = jnp.where(i < pl.num_programs(0) - 1, a_halo[POOL_HALO:], 0.0)

    a_g = _dot(hn, w_ref[:, G_OFF:IN_DIM])
    a_u = _dot(hn, w_ref[:, U_OFF:G_OFF])
    a_q = _dot(hn, w_ref[:, Q_OFF:K_OFF])
    a_k = _dot(hn, w_ref[:, K_OFF:V_OFF])
    _store_keys(v_ref, _dot(hn, w_ref[:, V_OFF:U_OFF]))

    gv = _gelu_tanh(a_g)
    q_ref[...] = (a_q * _head_rms_scale(a_q) * qg_ref[...]).astype(jnp.bfloat16)
    _store_keys(k_ref, a_k * _head_rms_scale(a_k) * kg_ref[...])

    u = _gelu_tanh(a_u)
    vn = (gv * _head_rms_scale(gv) * sgn_ref[...]).astype(jnp.bfloat16)
    low = lax.broadcasted_iota(jnp.int32, (SG_CHUNK, LANES), 1) < HEAD_DIM
    for c in range(tm // SG_CHUNK):
        rows = slice(c * SG_CHUNK, (c + 1) * SG_CHUNK)
        for s in range(SG_DIM // LANES):
            cols = slice(s * LANES, (s + 1) * LANES)
            m = _dot(sgw_ref[s], vn[rows, cols]) + sgb_ref[s]
            mixed = jnp.where(low, m[:SG_CHUNK], m[SG_CHUNK:])
            sg_ref[rows, cols] = (u[rows, cols] * mixed).astype(jnp.bfloat16)

    pool_ref[...] = _pool_mix(xe_ref, pw_ref, ps_ref, tm, seq_len)


def _inproj_call(stream, n1, sh, sc, w_in, layer, qg, kg, sgn, sgw, sgb, pool_w, pool_scale, tm, strips):
    pending = len(stream) == 3
    n = stream[0].shape[0]
    if strips:
        kv_spec = pl.BlockSpec((N_STRIPS, tm // N_STRIPS, NA_DIM), lambda i: (0, i, 0))
        kv_shape = jax.ShapeDtypeStruct((N_STRIPS, n // N_STRIPS, NA_DIM), jnp.bfloat16)
    else:
        kv_spec = pl.BlockSpec((tm, NA_DIM), lambda i: (i, 0))
        kv_shape = jax.ShapeDtypeStruct((n, NA_DIM), jnp.bfloat16)
    row = lambda i: (i, 0)
    fixed2 = lambda i: (0, 0)
    fixed3 = lambda i: (0, 0, 0)
    vec = lambda w: pl.BlockSpec((1, w), fixed2)
    rows = pl.BlockSpec((tm, D_MODEL), row)
    per_tile = tm // HALO_BLOCK
    before = lambda i: (jnp.maximum(i * per_tile - 1, 0), 0)
    after = lambda i: (jnp.minimum((i + 1) * per_tile, n // HALO_BLOCK - 1), 0)
    halo = lambda w, m: pl.BlockSpec((HALO_BLOCK, w), m)
    h = stream[0]
    if pending:
        y, g = stream[1], stream[2]
        yw = y.shape[1]
        args = [h, y, g, h, h, y, y]
        stream_specs = [rows, pl.BlockSpec((tm, yw), row), vec(D_MODEL),
                        halo(D_MODEL, before), halo(D_MODEL, after), halo(yw, before), halo(yw, after)]
    else:
        args = [h, h, h]
        stream_specs = [rows, halo(D_MODEL, before), halo(D_MODEL, after)]
    extra_out_specs = [rows] if pending else []
    extra_out_shape = [jax.ShapeDtypeStruct((n, D_MODEL), STREAM_DTYPE)] if pending else []
    return pl.pallas_call(
        functools.partial(_inproj_kernel, pending=pending, seq_len=n),
        grid=(n // tm,),
        in_specs=stream_specs + [
            vec(D_MODEL), vec(D_MODEL), vec(D_MODEL),
            pl.BlockSpec((None, D_MODEL, IN_DIM), lambda i: (layer, 0, 0)),
            vec(NA_DIM), vec(NA_DIM),
            vec(SG_DIM),
            pl.BlockSpec((SG_DIM // LANES, 2 * SG_CHUNK, SG_CHUNK), fixed3),
            pl.BlockSpec((SG_DIM // LANES, 2 * SG_CHUNK, LANES), fixed3),
            pl.BlockSpec((POOL_DIM, POOL_DIM), fixed2),
            vec(POOL_DIM),
        ],
        out_specs=[
            pl.BlockSpec((tm, POOL_DIM), row),
            pl.BlockSpec((tm, NA_DIM), row),
            kv_spec,
            kv_spec,
            pl.BlockSpec((tm, SG_DIM), row),
        ] + extra_out_specs,
        out_shape=[
            jax.ShapeDtypeStruct((n, POOL_DIM), jnp.bfloat16),
            jax.ShapeDtypeStruct((n, NA_DIM), jnp.bfloat16),
            kv_shape,
            kv_shape,
            jax.ShapeDtypeStruct((n, SG_DIM), jnp.bfloat16),
        ] + extra_out_shape,
        scratch_shapes=[pltpu.VMEM((tm + 2 * POOL_HALO, POOL_DIM), jnp.float32)],
        compiler_params=_cparams(("arbitrary",)),
        name="inproj",
    )(*args, n1, sh, sc, w_in, qg, kg, sgn, sgw, sgb, pool_w, pool_scale)


NA_ROWS_PER_BLOCK = 64
NA_GROUP_ROWS = 4
NA_WINDOW_ROWS = NA_GROUP_ROWS + NA_WIN_ROWS
NA_BLOCK = NA_ROWS_PER_BLOCK * GRID_W
NA_GROUP = NA_GROUP_ROWS * GRID_W
NA_HALF_COLS = GRID_W // 2
NA_CHAIN = NA_GROUP_ROWS * NA_HALF_COLS
NA_HALF_STRIPS = 5
NA_HALF_COL0 = (0, GRID_W - NA_HALF_STRIPS * STRIP_W)
NA_RUN = NA_WINDOW_ROWS * STRIP_W
NA_LOCAL = 512
NA_STRIP_BLOCK = NA_ROWS_PER_BLOCK * STRIP_W
NA_STRIP_HALO = (NA_WIN_ROWS // 2) * STRIP_W
NA_EDGE_FIRST, NA_EDGE_NONE, NA_EDGE_LAST = 0, 1, 2


def _stack_heads(x, low):
    zero = jnp.zeros_like(x)
    return jnp.concatenate([jnp.where(low, x, zero), jnp.where(low, zero, x)], axis=0)


def _natten_kernel(q_ref, kp_ref, kc_ref, kn_ref, vp_ref, vc_ref, vn_ref, kx_ref, vx_ref, bias_ref,
                   wg_ref, wu_ref, wd_ref,
                   o_ref, wg_bf_ref, wu_bf_ref, wd_bf_ref, kwin_ref, vwin_ref, vxe_ref, *, grid_rows):
    b = pl.program_id(1)
    wg_bf_ref[...] = wg_ref[...].astype(jnp.bfloat16)
    wu_bf_ref[...] = wu_ref[...].astype(jnp.bfloat16)
    wd_bf_ref[...] = wd_ref[...].astype(jnp.bfloat16)
    top, bottom = NA_STRIP_HALO, NA_STRIP_HALO + NA_STRIP_BLOCK
    kwin_ref[:, 0:top, :] = kp_ref[...]
    kwin_ref[:, top:bottom, :] = kc_ref[...]
    kwin_ref[:, bottom:, :] = kn_ref[...]
    vwin_ref[:, 0:top, 0:LANES] = vp_ref[...]
    vwin_ref[:, top:bottom, 0:LANES] = vc_ref[...]
    vwin_ref[:, bottom:, 0:LANES] = vn_ref[...]
    vwin_ref[:, :, LANES:] = jnp.ones(vwin_ref.shape[:2] + (LANES,), jnp.bfloat16)
    vxe_ref[:, 0:LANES] = vx_ref[...]
    vxe_ref[:, LANES:] = jnp.ones((vxe_ref.shape[0], LANES), jnp.bfloat16)
    low_q = lax.broadcasted_iota(jnp.int32, (NA_CHAIN, LANES), 1) < HEAD_DIM
    n_pad = NA_LOCAL - NA_HALF_STRIPS * NA_RUN

    def window_start(g):
        r0 = b * NA_ROWS_PER_BLOCK + g * NA_GROUP_ROWS
        ws = jnp.clip(r0 - NA_WIN_ROWS // 2, 0, grid_rows - NA_WINDOW_ROWS)
        edge = jnp.where(r0 == 0, NA_EDGE_FIRST,
                         jnp.where(r0 == grid_rows - NA_GROUP_ROWS, NA_EDGE_LAST, NA_EDGE_NONE))
        start = pl.multiple_of((ws - b * NA_ROWS_PER_BLOCK + NA_WIN_ROWS // 2) * STRIP_W, NA_STRIP_HALO)
        return start, edge

    def local_window(win_ref, g, half):
        start, _ = window_start(g)
        s0 = NA_HALF_COL0[half] // STRIP_W
        runs = [win_ref[s, pl.ds(start, NA_RUN), :] for s in range(s0, s0 + NA_HALF_STRIPS)]
        return jnp.concatenate(runs + [jnp.zeros((n_pad, win_ref.shape[2]), jnp.bfloat16)], axis=0)

    def query_rows(g, half, j):
        first = g * NA_GROUP + j * GRID_W + half * NA_HALF_COLS
        return slice(first, first + NA_HALF_COLS)

    def scores(c):
        g, half = divmod(c, 2)
        _, edge = window_start(g)
        qh = jnp.concatenate([q_ref[query_rows(g, half, j), :] for j in range(NA_GROUP_ROWS)], axis=0)
        lhs = _stack_heads(qh, low_q)
        kl = local_window(kwin_ref, g, half)
        return jnp.concatenate([_dot_nt(lhs, kl).astype(jnp.bfloat16) + bias_ref[edge, half],
                                _dot_nt(lhs, kx_ref[...]).astype(jnp.bfloat16)], axis=-1)

    n_chains = 2 * (NA_ROWS_PER_BLOCK // NA_GROUP_ROWS)
    s_next = scores(0)
    for c in range(n_chains):
        s = s_next
        if c + 1 < n_chains:
            s_next = scores(c + 1)
        g, half = divmod(c, 2)
        vl = local_window(vwin_ref, g, half)
        m = jnp.max(s, axis=-1, keepdims=True)
        pb = jnp.exp2(s - m)
        o = _dot(pb[:, :NA_LOCAL], vl) + _dot(pb[:, NA_LOCAL:], vxe_ref[...])
        o = o[:, :LANES] * (1.0 / o[:, LANES:])
        o = jnp.where(low_q, o[:NA_CHAIN], o[NA_CHAIN:]).astype(jnp.bfloat16)
        for j in range(NA_GROUP_ROWS):
            o_ref[query_rows(g, half, j), :] = o[j * NA_HALF_COLS:(j + 1) * NA_HALF_COLS]


def _natten_call(q, k, v, k_ctx, v_ctx, bias, w_gate, w_up, w_down, layer):
    n = q.shape[0]
    grid_rows = n // GRID_W
    assert grid_rows % NA_ROWS_PER_BLOCK == 0 and grid_rows >= 2 * NA_ROWS_PER_BLOCK
    nblk = n // NA_BLOCK
    steps = HEAD_PAIRS * nblk
    depth = w_gate.shape[0]
    up_rows = N_EXPERTS * D_MODEL
    down_rows = N_EXPERTS * D_EXPERT
    assert up_rows % steps == 0 and down_rows % steps == 0
    wg2 = w_gate.reshape(depth * up_rows, D_EXPERT)
    wu2 = w_up.reshape(depth * up_rows, D_EXPERT)
    wd2 = w_down.reshape(depth * down_rows, D_MODEL)
    up_in = pl.BlockSpec((up_rows // steps, D_EXPERT), lambda p, b: (layer * steps + p * nblk + b, 0))
    down_in = pl.BlockSpec((down_rows // steps, D_MODEL), lambda p, b: (layer * steps + p * nblk + b, 0))
    up_out = pl.BlockSpec((up_rows // steps, D_EXPERT), lambda p, b: (p * nblk + b, 0))
    down_out = pl.BlockSpec((down_rows // steps, D_MODEL), lambda p, b: (p * nblk + b, 0))
    n_halo = n // N_STRIPS // NA_STRIP_HALO
    hb = NA_STRIP_BLOCK // NA_STRIP_HALO
    rows = pl.BlockSpec((NA_BLOCK, LANES), lambda p, b: (b, p))
    cur = pl.BlockSpec((N_STRIPS, NA_STRIP_BLOCK, LANES), lambda p, b: (0, b, p))
    prev = pl.BlockSpec((N_STRIPS, NA_STRIP_HALO, LANES), lambda p, b: (0, jnp.maximum(b * hb - 1, 0), p))
    nxt = pl.BlockSpec((N_STRIPS, NA_STRIP_HALO, LANES),
                       lambda p, b: (0, jnp.minimum((b + 1) * hb, n_halo - 1), p))
    ctx = pl.BlockSpec((k_ctx.shape[0], LANES), lambda p, b: (0, p))
    win_rows = NA_STRIP_BLOCK + 2 * NA_STRIP_HALO
    att, wg_bf, wu_bf, wd_bf = pl.pallas_call(
        functools.partial(_natten_kernel, grid_rows=grid_rows),
        grid=(HEAD_PAIRS, nblk),
        in_specs=[rows, prev, cur, nxt, prev, cur, nxt, ctx, ctx,
                  pl.BlockSpec((None, 3, 2, 2 * NA_CHAIN, NA_LOCAL), lambda p, b: (p, 0, 0, 0, 0)),
                  up_in, up_in, down_in],
        out_specs=[rows, up_out, up_out, down_out],
        out_shape=[jax.ShapeDtypeStruct((n, NA_DIM), jnp.bfloat16),
                   jax.ShapeDtypeStruct((up_rows, D_EXPERT), jnp.bfloat16),
                   jax.ShapeDtypeStruct((up_rows, D_EXPERT), jnp.bfloat16),
                   jax.ShapeDtypeStruct((down_rows, D_MODEL), jnp.bfloat16)],
        scratch_shapes=[pltpu.VMEM((N_STRIPS, win_rows, LANES), jnp.bfloat16),
                        pltpu.VMEM((N_STRIPS, win_rows, 2 * LANES), jnp.bfloat16),
                        pltpu.VMEM((k_ctx.shape[0], 2 * LANES), jnp.bfloat16)],
        compiler_params=_cparams(("arbitrary", "arbitrary")),
        name="natten",
    )(q, k, k, k, v, v, v, k_ctx, v_ctx, bias, wg2, wu2, wd2)
    return (att, wg_bf.reshape(N_EXPERTS, D_MODEL, D_EXPERT), wu_bf.reshape(N_EXPERTS, D_MODEL, D_EXPERT),
            wd_bf.reshape(N_EXPERTS, D_EXPERT, D_MODEL))


def _natten_bias(rpb):
    cols = np.arange(GRID_W)
    col_start = np.clip(cols - NA_WIN_COLS // 2, 0, GRID_W - NA_WIN_COLS)
    kc = np.arange(GRID_W)
    in_win = (kc[None, :] >= col_start[:, None]) & (kc[None, :] < col_start[:, None] + NA_WIN_COLS)
    dc = kc[None, :] - cols[:, None] + NA_WIN_COLS - 1
    sel = (np.arange(2 * NA_WIN_COLS - 1)[:, None, None] == dc[None]) & in_win[None]
    t2 = jnp.einsum("hdj,jqk->hdqk", rpb, jnp.asarray(sel, jnp.float32), precision=lax.Precision.HIGHEST)
    t2 = jnp.where(in_win[None, None], t2 * LOG2_E, NEG_BIG)
    neg = jnp.full((NA_HEADS, 1, GRID_W, GRID_W), NEG_BIG, jnp.float32)
    t2e = jnp.concatenate([neg, t2, neg], axis=1)
    u = jnp.concatenate([t2e[:, :-1], t2e[:, 1:]], axis=-1)
    u = u.reshape(HEAD_PAIRS, 2, 2 * NA_WIN_ROWS, GRID_W, LANES)

    place = np.zeros((2, NA_WINDOW_ROWS * GRID_W, NA_LOCAL), np.float32)
    for half, c0 in enumerate(NA_HALF_COL0):
        for a in range(NA_WINDOW_ROWS):
            for kcol in range(c0, c0 + NA_HALF_STRIPS * STRIP_W):
                s, c8 = divmod(kcol - c0, STRIP_W)
                place[half, a * GRID_W + kcol, s * NA_RUN + a * STRIP_W + c8] = 1.0
    outside = np.full((3, NA_GROUP_ROWS, NA_LOCAL), NEG_BIG, np.float32)
    for edge in (NA_EDGE_FIRST, NA_EDGE_NONE, NA_EDGE_LAST):
        for j in range(NA_GROUP_ROWS):
            lo, _ = _window_rows(edge, j)
            for s in range(NA_HALF_STRIPS):
                outside[edge, j, s * NA_RUN + lo * STRIP_W:s * NA_RUN + (lo + NA_WIN_ROWS) * STRIP_W] = 0.0
    return pl.pallas_call(
        _bias_expand_kernel,
        grid=(HEAD_PAIRS,),
        in_specs=[pl.BlockSpec((None, 2, 2 * NA_WIN_ROWS, GRID_W, LANES), lambda p: (p, 0, 0, 0, 0)),
                  pl.BlockSpec(place.shape, lambda p: (0, 0, 0)),
                  pl.BlockSpec(outside.shape, lambda p: (0, 0, 0))],
        out_specs=pl.BlockSpec((None, 3, 2, 2 * NA_CHAIN, NA_LOCAL), lambda p: (p, 0, 0, 0, 0)),
        out_shape=jax.ShapeDtypeStruct((HEAD_PAIRS, 3, 2, 2 * NA_CHAIN, NA_LOCAL), jnp.bfloat16),
        compiler_params=_cparams(("arbitrary",)),
        name="bias_expand",
    )(u, jnp.asarray(place, jnp.bfloat16), jnp.asarray(outside))


def _window_rows(edge, j):
    if edge == NA_EDGE_FIRST:
        return 0, NA_WIN_ROWS - 1 - j
    if edge == NA_EDGE_NONE:
        return j, NA_WIN_ROWS // 2 - 1
    return NA_WINDOW_ROWS - NA_WIN_ROWS, NA_WIN_ROWS // 2 - 1 - j


def _bias_expand_kernel(u_ref, place_ref, outside_ref, o_ref):
    low = lax.broadcasted_iota(jnp.int32, (NA_HALF_COLS, LANES), 1) < GRID_W
    zero = jnp.zeros((NA_HALF_COLS, LANES), jnp.float32)
    for edge in (NA_EDGE_FIRST, NA_EDGE_NONE, NA_EDGE_LAST):
        for half in range(2):
            q0 = half * NA_HALF_COLS
            blocks, masks = [], []
            for hd in range(2):
                for j in range(NA_GROUP_ROWS):
                    lo, base = _window_rows(edge, j)
                    tiles = []
                    for i in range(NA_WINDOW_ROWS // 2):
                        a0, a1 = 2 * i, 2 * i + 1
                        ok0 = lo <= a0 < lo + NA_WIN_ROWS
                        ok1 = lo <= a1 < lo + NA_WIN_ROWS
                        if not (ok0 or ok1):
                            tile = zero
                        else:
                            tile = u_ref[hd, base + a1 - lo, q0:q0 + NA_HALF_COLS, :]
                            if not ok0:
                                tile = jnp.where(low, zero, tile)
                            if not ok1:
                                tile = jnp.where(low, tile, zero)
                        tiles.append(tile)
                    blocks.append(jnp.concatenate(tiles, axis=-1))
                    masks.append(jnp.broadcast_to(outside_ref[edge, j:j + 1, :], (NA_HALF_COLS, NA_LOCAL)))
            lhs = jnp.concatenate(blocks, axis=0).astype(jnp.bfloat16)
            placed = _dot(lhs, place_ref[half]) + jnp.concatenate(masks, axis=0)
            o_ref[edge, half] = placed.astype(o_ref.dtype)


def _ctxatt_kernel(q_ref, k_ref, v_ref, o_ref):
    lc = q_ref.shape[0]
    low = lax.broadcasted_iota(jnp.int32, (lc, LANES), 1) < HEAD_DIM
    lhs = _stack_heads(q_ref[...], low)
    s = _dot_nt(lhs, k_ref[...])
    m = jnp.max(s, axis=-1, keepdims=True)
    p = jnp.exp2(s - m)
    denom = jnp.sum(p, axis=-1, keepdims=True)
    o = _dot(p.astype(jnp.bfloat16), v_ref[...]) * (1.0 / denom)
    o_ref[...] = jnp.where(low, o[:lc], o[lc:]).astype(jnp.bfloat16)


def _ctxatt_call(q, k, v):
    lc = q.shape[0]
    spec = pl.BlockSpec((lc, LANES), lambda p: (0, p))
    return pl.pallas_call(
        _ctxatt_kernel,
        grid=(HEAD_PAIRS,),
        in_specs=[spec, spec, spec],
        out_specs=spec,
        out_shape=jax.ShapeDtypeStruct((lc, NA_DIM), jnp.bfloat16),
        compiler_params=_cparams(("arbitrary",)),
        name="ctxatt",
    )(q, k, v)


OUTPROJ_CHAIN = 256
OUTPROJ_TILE = 1024


def _outproj_kernel(h_ref, mp_ref, att_ref, sg_ref, wo_ref, g1_ref, n2_ref, sh_ref, sc_ref,
                    wr_ref, br_ref, h1_ref, pay_ref, route_ref):
    chains = [slice(c * OUTPROJ_CHAIN, (c + 1) * OUTPROJ_CHAIN) for c in range(h_ref.shape[0] // OUTPROJ_CHAIN)]
    h1s = []
    for rows in chains:
        mix = (_dot(mp_ref[rows, :], wo_ref[0:POOL_DIM, :])
               + _dot(att_ref[rows, :], wo_ref[POOL_DIM:POOL_DIM + NA_DIM, :])
               + _dot(sg_ref[rows, :], wo_ref[POOL_DIM + NA_DIM:, :]))
        h1 = h_ref[rows, :].astype(jnp.float32) + g1_ref[...] * mix
        h1_ref[rows, :] = h1.astype(h1_ref.dtype)
        h1s.append(h1)
    for rows, h1 in zip(chains, h1s):
        _outproj_route(rows, h1, n2_ref, sh_ref, sc_ref, wr_ref, br_ref, pay_ref, route_ref)


def _outproj_route(rows, h1, n2_ref, sh_ref, sc_ref, wr_ref, br_ref, pay_ref, route_ref):
    tm = OUTPROJ_CHAIN
    ms = jnp.mean(h1 * h1, axis=-1, keepdims=True)
    hm = (h1 * lax.rsqrt(ms + EPS)) * (n2_ref[...] * (1.0 + sc_ref[...])) + sh_ref[...]
    pay_ref[rows, 0:HALF_D] = _pack_halves(hm)

    hm_hi = hm.astype(jnp.bfloat16)
    lt = _dot_nt(wr_ref[...], hm_hi)
    logits = lt[:N_EXPERTS] + lt[N_EXPERTS:] + br_ref[...]
    e = jnp.exp(logits - jnp.max(logits, axis=0, keepdims=True))

    best = ga = gb = e1 = e2 = cls = None
    for c in range(N_CLASSES):
        a, b2 = int(CLASS_E1[c]), int(CLASS_E2[c])
        ea, eb = e[a:a + 1, :], e[b2:b2 + 1, :]
        s = ea + eb
        if best is None:
            best, ga, gb = s, ea, eb
            e1 = jnp.full_like(s, float(a))
            e2 = jnp.full_like(s, float(b2))
            cls = jnp.zeros_like(s)
        else:
            better = s > best
            best = jnp.where(better, s, best)
            ga = jnp.where(better, ea, ga)
            gb = jnp.where(better, eb, gb)
            e1 = jnp.where(better, float(a), e1)
            e2 = jnp.where(better, float(b2), e2)
            cls = jnp.where(better, float(c), cls)
    inv = 1.0 / best
    row = lax.broadcasted_iota(jnp.int32, (ROUTE_ROWS, tm), 0)
    rec = jnp.where(row == 0, ga * inv,
          jnp.where(row == 1, gb * inv,
          jnp.where(row == 2, e1,
          jnp.where(row == 3, e2,
          jnp.where(row == 4, cls, 0.0)))))
    route_ref[:, rows] = rec
    wide = jnp.concatenate([rec, jnp.zeros((LANES - ROUTE_ROWS, tm), jnp.float32)], axis=0)
    pay_ref[rows, HALF_D:] = pltpu.bitcast(wide.T, jnp.uint32)


def _outproj_call(h, mp, att, sg, w_out, layer, g1, n2, sh2, sc2, wr, br, tm):
    n = h.shape[0]
    row = lambda i: (i, 0)
    fixed = lambda i: (0, 0)
    vec = pl.BlockSpec((1, D_MODEL), fixed)
    return pl.pallas_call(
        _outproj_kernel,
        grid=(n // tm,),
        in_specs=[
            pl.BlockSpec((tm, D_MODEL), row),
            pl.BlockSpec((tm, POOL_DIM), row),
            pl.BlockSpec((tm, NA_DIM), row),
            pl.BlockSpec((tm, SG_DIM), row),
            pl.BlockSpec((None, D_MODEL, D_MODEL), lambda i: (layer, 0, 0)),
            vec, vec, vec, vec,
            pl.BlockSpec((2 * N_EXPERTS, D_MODEL), fixed),
            pl.BlockSpec((N_EXPERTS, 1), fixed),
        ],
        out_specs=[
            pl.BlockSpec((tm, D_MODEL), row),
            pl.BlockSpec((tm, PAYLOAD_W), row),
            pl.BlockSpec((ROUTE_ROWS, tm), lambda i: (0, i)),
        ],
        out_shape=[
            jax.ShapeDtypeStruct((n, D_MODEL), STREAM_DTYPE),
            jax.ShapeDtypeStruct((n, PAYLOAD_W), jnp.uint32),
            jax.ShapeDtypeStruct((ROUTE_ROWS, n), jnp.float32),
        ],
        compiler_params=_cparams(("arbitrary",)),
        name="outproj",
    )(h, mp, att, sg, w_out, g1, n2, sh2, sc2, wr, br)


def _payload_parts(pay_ref, rows=slice(None)):
    lo, hi = _unpack_halves(pay_ref[rows, 0:HALF_D])
    x = jnp.concatenate([lo, hi], axis=-1).astype(jnp.bfloat16)
    return x, pltpu.bitcast(pay_ref[rows, HALF_D:], jnp.float32)


def _expert_pair(x, ga, gb, wga, wua, wda, wgb, wub, wdb):
    ha = (_silu(_dot(x, wga)) * _dot(x, wua) * ga).astype(jnp.bfloat16)
    hb = (_silu(_dot(x, wgb)) * _dot(x, wub) * gb).astype(jnp.bfloat16)
    return _dot(ha, wda) + _dot(hb, wdb)


MOE_QUARTER = MOE_TM // 4


def _moe_sorted_kernel(blk_ref, e1_ref, e2_ref, quarters_ref, pay_ref,
                       wga_ref, wua_ref, wda_ref, wgb_ref, wub_ref, wdb_ref, o_ref):
    quarters = quarters_ref[pl.program_id(0)]

    def run(rows):
        x, route = _payload_parts(pay_ref, rows)
        y = _expert_pair(x, route[:, 0:1], route[:, 1:2], wga_ref[...], wua_ref[...], wda_ref[...],
                         wgb_ref[...], wub_ref[...], wdb_ref[...])
        o_ref[rows, :] = _pack_halves(y)

    for used in range(1, MOE_TM // MOE_QUARTER + 1):
        @pl.when(quarters == used)
        def _(used=used):
            run(slice(0, used * MOE_QUARTER))


def _moe_sorted_call(blk, e1, e2, quarters, pay_sorted, wg, wu, wd):
    n_tiles = blk.shape[0]
    rows = lambda i, blk, e1, e2, quarters: (blk[i], 0)
    wa = lambda i, blk, e1, e2, quarters: (e1[i], 0, 0)
    wb = lambda i, blk, e1, e2, quarters: (e2[i], 0, 0)
    up = lambda m: pl.BlockSpec((None, D_MODEL, D_EXPERT), m)
    down = lambda m: pl.BlockSpec((None, D_EXPERT, D_MODEL), m)
    return pl.pallas_call(
        _moe_sorted_kernel,
        grid_spec=pltpu.PrefetchScalarGridSpec(
            num_scalar_prefetch=4,
            grid=(n_tiles,),
            in_specs=[pl.BlockSpec((MOE_TM, PAYLOAD_W), rows),
                      up(wa), up(wa), down(wa), up(wb), up(wb), down(wb)],
            out_specs=pl.BlockSpec((MOE_TM, HALF_D), rows),
        ),
        out_shape=jax.ShapeDtypeStruct((n_tiles * MOE_TM, HALF_D), jnp.uint32),
        compiler_params=_cparams(("arbitrary",)),
        name="moe_sorted",
    )(blk, e1, e2, quarters, pay_sorted, wg, wu, wd, wg, wu, wd)


def _moe_dense_kernel(pay_ref, wg_ref, wu_ref, wd_ref, o_ref):
    e = pl.program_id(0)

    @pl.when(e == 0)
    def _():
        o_ref[...] = jnp.zeros_like(o_ref)

    x, route = _payload_parts(pay_ref)
    ef = e.astype(jnp.float32)
    gate = (jnp.where(route[:, 2:3] == ef, route[:, 0:1], 0.0)
            + jnp.where(route[:, 3:4] == ef, route[:, 1:2], 0.0))
    he = (_silu(_dot(x, wg_ref[...])) * _dot(x, wu_ref[...]) * gate).astype(jnp.bfloat16)
    o_ref[...] += _dot(he, wd_ref[...])


def _moe_dense_call(pay, wg, wu, wd):
    n = pay.shape[0]
    return pl.pallas_call(
        _moe_dense_kernel,
        grid=(N_EXPERTS,),
        in_specs=[pl.BlockSpec((n, PAYLOAD_W), lambda e: (0, 0)),
                  pl.BlockSpec((None, D_MODEL, D_EXPERT), lambda e: (e, 0, 0)),
                  pl.BlockSpec((None, D_MODEL, D_EXPERT), lambda e: (e, 0, 0)),
                  pl.BlockSpec((None, D_EXPERT, D_MODEL), lambda e: (e, 0, 0))],
        out_specs=pl.BlockSpec((n, D_MODEL), lambda e: (0, 0)),
        out_shape=jax.ShapeDtypeStruct((n, D_MODEL), jnp.float32),
        compiler_params=_cparams(("arbitrary",)),
        name="moe_dense",
    )(pay, wg, wu, wd)


def _moe_out(y_ref):
    if y_ref.dtype == jnp.uint32:
        return jnp.concatenate(_unpack_halves(y_ref[...]), axis=-1)
    return y_ref[...]


def _residual_kernel(h_ref, y_ref, g_ref, o_ref):
    o_ref[...] = h_ref[...].astype(jnp.float32) + g_ref[...] * _moe_out(y_ref)


def _residual_call(h, y, g, tm):
    n = h.shape[0]
    row = pl.BlockSpec((tm, D_MODEL), lambda i: (i, 0))
    return pl.pallas_call(
        _residual_kernel,
        grid=(n // tm,),
        in_specs=[row, pl.BlockSpec((tm, y.shape[1]), lambda i: (i, 0)),
                  pl.BlockSpec((1, D_MODEL), lambda i: (0, 0))],
        out_specs=row,
        out_shape=jax.ShapeDtypeStruct((n, D_MODEL), jnp.float32),
        compiler_params=_cparams(("arbitrary",)),
        name="residual",
    )(h, y, g)


SC_ROWS = 64


SC_CORES = 2
SC_SUBCORES = 16
SC_WORKERS = SC_CORES * SC_SUBCORES


def _sc_mesh():
    return plsc.VectorSubcoreMesh(core_axis_name="core", subcore_axis_name="subcore")


def _sc_worker():
    return lax.axis_index("subcore") * SC_CORES + lax.axis_index("core")


def _sc_double_buffers(w, dtype):
    return [pltpu.VMEM((SC_ROWS,), jnp.int32), pltpu.VMEM((SC_ROWS,), jnp.int32),
            pltpu.VMEM((SC_ROWS, w), dtype), pltpu.VMEM((SC_ROWS, w), dtype),
            pltpu.SemaphoreType.DMA((2,)), pltpu.SemaphoreType.DMA((2,)), pltpu.SemaphoreType.DMA((2,))]


def _scatter_rows(x, dest, n_out):
    n, w = x.shape

    per_worker = n // SC_WORKERS
    assert per_worker % SC_ROWS == 0

    n_chunks = per_worker // SC_ROWS

    @functools.partial(pl.kernel, out_type=jax.ShapeDtypeStruct((n_out, w), x.dtype), mesh=_sc_mesh(),
                       scratch_types=_sc_double_buffers(w, x.dtype))
    def scatter(x_hbm, i_hbm, o_hbm, idx0, idx1, rows0, rows1, sem_idx, sem_rows, sem_out):
        idx, rows = (idx0, idx1), (rows0, rows1)
        first = _sc_worker() * per_worker

        def load(c):
            slot = c % 2
            base = pl.multiple_of(first + c * SC_ROWS, SC_ROWS)
            return (pltpu.async_copy(i_hbm.at[pl.ds(base, SC_ROWS)], idx[slot], sem_idx.at[slot]),
                    pltpu.async_copy(x_hbm.at[pl.ds(base, SC_ROWS)], rows[slot], sem_rows.at[slot]))

        stores = [None, None]
        loads = load(0)
        for c in range(n_chunks):
            slot = c % 2
            following = None
            if c + 1 < n_chunks:
                if stores[1 - slot] is not None:
                    stores[1 - slot].wait()
                following = load(c + 1)
            for copy in loads:
                copy.wait()
            stores[slot] = pltpu.async_copy(rows[slot], o_hbm.at[idx[slot]], sem_out.at[slot])
            loads = following
        for c in range(max(n_chunks - 2, 0), n_chunks):
            stores[c % 2].wait()

    return scatter(x, dest)


def _gather_rows(x, src):
    n = src.shape[0]
    w = x.shape[1]

    per_worker = n // SC_WORKERS
    assert per_worker % SC_ROWS == 0

    n_chunks = per_worker // SC_ROWS

    @functools.partial(pl.kernel, out_type=jax.ShapeDtypeStruct((n, w), x.dtype), mesh=_sc_mesh(),
                       scratch_types=_sc_double_buffers(w, x.dtype))
    def gather(x_hbm, i_hbm, o_hbm, idx0, idx1, rows0, rows1, sem_idx, sem_rows, sem_out):
        idx, rows = (idx0, idx1), (rows0, rows1)
        first = _sc_worker() * per_worker

        def base(c):
            return pl.multiple_of(first + c * SC_ROWS, SC_ROWS)

        def load_idx(c):
            return pltpu.async_copy(i_hbm.at[pl.ds(base(c), SC_ROWS)], idx[c % 2], sem_idx.at[c % 2])

        stores = [None, None]
        gathers = [None, None]
        idx_load = load_idx(0)
        for c in range(n_chunks):
            slot = c % 2
            idx_load.wait()
            if stores[slot] is not None:
                stores[slot].wait()
                stores[slot] = None
            gathers[slot] =pltpu.async_copy(x_hbm.at[idx[slot]], rows[slot], sem_rows.at[slot])
            if c >= 1:
                gathers[1 - slot].wait()
                stores[1 - slot] = pltpu.async_copy(rows[1 - slot], o_hbm.at[pl.ds(base(c - 1), SC_ROWS)],
                                                    sem_out.at[1 - slot])
            if c + 1 < n_chunks:
                idx_load = load_idx(c + 1)
        last = (n_chunks - 1) % 2
        gathers[last].wait()
        pltpu.async_copy(rows[last], o_hbm.at[pl.ds(base(n_chunks - 1), SC_ROWS)], sem_out.at[last]).wait()
        if stores[1 - last] is not None:
            stores[1 - last].wait()

    return gather(x, src)


def _routing_plan(cls, n_tiles):
    onehot = (cls[:, None] == jnp.arange(N_CLASSES, dtype=jnp.int32)[None, :]).astype(jnp.int32)
    counts = jnp.sum(onehot, axis=0)
    rank = jnp.sum((jnp.cumsum(onehot, axis=0) - onehot) * onehot, axis=1)
    tiles = (counts + MOE_TM - 1) // MOE_TM
    tile_end = jnp.cumsum(tiles)
    tile_start = tile_end - tiles
    dest = jnp.sum(onehot * tile_start[None, :], axis=1) * MOE_TM + rank
    nact = tile_end[-1]
    tile = jnp.arange(n_tiles, dtype=jnp.int32)
    blk = jnp.minimum(tile, nact - 1)
    tile_cls = jnp.sum((blk[:, None] >= tile_end[None, :]).astype(jnp.int32), axis=1)
    e1 = jnp.asarray(CLASS_E1)[tile_cls]
    e2 = jnp.asarray(CLASS_E2)[tile_cls]
    valid = counts[tile_cls] - (blk - tile_start[tile_cls]) * MOE_TM
    quarters = jnp.where(tile >= nact, 0, (jnp.minimum(valid, MOE_TM) + MOE_QUARTER - 1) // MOE_QUARTER)
    return dest.astype(jnp.int32), blk, e1, e2, quarters.astype(jnp.int32)


def _row_tile(n):
    return 512 if n % 512 == 0 else 256


def kernel(x, c, ctx, c_ctx, w_ada, b_ada, norm1, w_in, pool_w, pool_scale, q_norm, k_norm, rpb,
           sg_w, sg_b, sg_norm, w_out, norm2, w_router, b_router, w_gate, w_up, w_down):
    depth = w_ada.shape[0]
    n = x.shape[1]
    lc = ctx.shape[1]
    bf = jnp.bfloat16
    lat_stream = (x[0],)
    h_ctx = ctx[0]

    cond = jnp.zeros((SUBLANES, D_MODEL), jnp.float32).at[0].set(c[0]).at[1].set(c_ctx)
    mod = _ada_call(cond, w_ada, b_ada)

    wr_t = w_router.T
    wr_hi = wr_t.astype(bf)
    wr_lo = (wr_t - wr_hi.astype(jnp.float32)).astype(bf)
    wr = jnp.concatenate([wr_hi, wr_lo], axis=0)
    br = b_router.reshape(N_EXPERTS, 1)
    n_tiles = n // MOE_TM + N_CLASSES
    w_in_bf = w_in.astype(bf)
    w_out_bf = w_out.astype(bf)

    for l in range(depth):
        last = l == depth - 1
        qg =(q_norm[l] * (HEAD_DIM ** -0.5 * LOG2_E)).reshape(1, NA_DIM)
        kg = k_norm[l].reshape(1, NA_DIM)
        sgn = sg_norm[l].reshape(1, SG_DIM)
        sgw = sg_w[l].astype(bf).reshape(SG_DIM // LANES, 2 * SG_CHUNK, SG_CHUNK)
        sgb = jnp.broadcast_to(sg_b[l].reshape(SG_DIM // LANES, 2 * SG_CHUNK, 1),
                               (SG_DIM // LANES, 2 * SG_CHUNK, LANES))
        pool_bd = jax.scipy.linalg.block_diag(*[pool_w[l, g] for g in range(len(POOL_WINDOWS))]).astype(bf)
        pscale = pool_scale[l].reshape(1, POOL_DIM)
        bias = _natten_bias(rpb[l])
        n1 = norm1[l].reshape(1, D_MODEL)
        n2 = norm2[l].reshape(1, D_MODEL)

        def mods(row):
            return [mod[l, row:row + 1, i * D_MODEL:(i + 1) * D_MODEL] for i in range(6)]

        sh1, sc1, g1, sh2, sc2, g2 = mods(0)
        csh1, csc1, cg1, csh2, csc2, cg2 = mods(1)

        tc = _row_tile(lc)
        mix_pool_c, q_c, k_c, v_c, sg_c = _inproj_call((h_ctx,), n1, csh1, csc1, w_in_bf, l, qg, kg,
                                                       sgn, sgw, sgb, pool_bd, pscale, tc, strips=False)

        tm = _row_tile(n)
        outs = _inproj_call(lat_stream, n1, sh1, sc1, w_in_bf, l, qg, kg, sgn, sgw, sgb,
                            pool_bd, pscale, tm, strips=True)
        mix_pool, q, k, v, sg = outs[:5]
        h_lat = outs[5] if len(lat_stream) == 3 else lat_stream[0]
        att, wg_l, wu_l, wd_l = _natten_call(q, k, v, k_c, v_c, bias, w_gate, w_up, w_down, l)
        h1, pay, route = _outproj_call(h_lat, mix_pool, att, sg, w_out_bf, l, g1, n2, sh2, sc2, wr, br,
                                       OUTPROJ_TILE if n % OUTPROJ_TILE == 0 else tm)
        cls = route[4].astype(jnp.int32)
        dest, blk, e1, e2, quarters = _routing_plan(cls, n_tiles)
        pay_sorted = _scatter_rows(pay, dest, n_tiles * MOE_TM)
        y_sorted = _moe_sorted_call(blk, e1, e2, quarters, pay_sorted, wg_l, wu_l, wd_l)
        y = _gather_rows(y_sorted, dest)
        lat_stream = (h1, y, g2)

        if not last:
            att_c = _ctxatt_call(q_c, k_c, v_c)
            h1_c, pay_c, _ = _outproj_call(h_ctx, mix_pool_c, att_c, sg_c, w_out_bf, l, cg1, n2, csh2, csc2,
                                           wr, br, tc)
            y_c = _moe_dense_call(pay_c, wg_l, wu_l, wd_l)
            h_ctx = _residual_call(h1_c, y_c, cg2, tc)

    return _residual_call(*lat_stream, _row_tile(n))[None]
```

```python
import functools

import jax
import jax.numpy as jnp
import numpy as np
from jax import lax
from jax.experimental import pallas as pl
from jax.experimental.pallas import tpu as pltpu
from jax.experimental.pallas import tpu_sc as plsc

D_MODEL = 1024
GRID_W = 64
HEAD_DIM = 64
POOL_WINDOWS = (2, 4, 8, 16)
POOL_DIM = 256
NA_HEADS = 8
NA_DIM = 512
NA_WIN_ROWS = 8
NA_WIN_COLS = 16
SG_DIM = 256
SG_CHUNK = 128
Q_OFF = POOL_DIM
K_OFF = Q_OFF + NA_DIM
V_OFF = K_OFF + NA_DIM
U_OFF = V_OFF + NA_DIM
G_OFF = U_OFF + SG_DIM
IN_DIM = G_OFF + SG_DIM
N_EXPERTS = 16
GROUP_SIZE = 4
D_EXPERT = 512
EPS = 1e-6

LANES = 128
SUBLANES = 8
HEAD_PAIRS = NA_DIM // LANES
VMEM_LIMIT = 48 * 1024 * 1024

PAIRS = ((0, 1), (0, 2), (1, 2), (1, 3), (0, 3), (2, 3))
N_CLASSES = (N_EXPERTS // GROUP_SIZE) * len(PAIRS)
CLASS_E1 = np.array([4 * g + i for g in range(4) for (i, j) in PAIRS], np.int32)
CLASS_E2 = np.array([4 * g + j for g in range(4) for (i, j) in PAIRS], np.int32)

ROUTE_ROWS = 8
HALF_D = D_MODEL // 2
PAYLOAD_W = HALF_D + LANES
MOE_TM = 512
NEG_BIG = -1e30
LOG2_E = 1.4426950408889634


def _cparams(sem):
    return pltpu.CompilerParams(dimension_semantics=sem, vmem_limit_bytes=VMEM_LIMIT)


def _dot(a, b):
    return jnp.dot(a, b, preferred_element_type=jnp.float32)


def _dot_nt(a, b):
    return lax.dot_general(a, b, (((1,), (1,)), ((), ())), preferred_element_type=jnp.float32)


def _gelu_tanh(x):
    return 0.5 * x * (1.0 + jnp.tanh(0.7978845608028654 * (x + 0.044715 * (x * x * x))))


def _silu(x):
    return x * (1.0 / (1.0 + jnp.exp(-x)))


def _pack_halves(x):
    w = x.shape[1] // 2
    lo = pltpu.bitcast(x[:, :w].astype(jnp.bfloat16).astype(jnp.float32), jnp.uint32) >> 16
    hi = pltpu.bitcast(x[:, w:].astype(jnp.bfloat16).astype(jnp.float32), jnp.uint32) & jnp.uint32(0xFFFF0000)
    return lo | hi


def _unpack_halves(words):
    lo = pltpu.bitcast(words << 16, jnp.float32)
    hi = pltpu.bitcast(words & jnp.uint32(0xFFFF0000), jnp.float32)
    return lo, hi


def _ada_kernel(cond_ref, w_ref, b_ref, o_ref):
    cond = _silu(cond_ref[...])
    o_ref[...] = jnp.dot(cond, w_ref[...], preferred_element_type=jnp.float32,
                         precision=lax.Precision.HIGHEST) + b_ref[...]


def _ada_call(cond, w_ada, b_ada):
    depth = w_ada.shape[0]
    tn = 1536
    return pl.pallas_call(
        _ada_kernel,
        grid=(depth, 6 * D_MODEL // tn),
        in_specs=[
            pl.BlockSpec((SUBLANES, D_MODEL), lambda l, j: (0, 0)),
            pl.BlockSpec((None, D_MODEL, tn), lambda l, j: (l, 0, j)),
            pl.BlockSpec((None, 1, tn), lambda l, j: (l, 0, j)),
        ],
        out_specs=pl.BlockSpec((None, SUBLANES, tn), lambda l, j: (l, 0, j)),
        out_shape=jax.ShapeDtypeStruct((depth, SUBLANES, 6 * D_MODEL), jnp.float32),
        compiler_params=_cparams(("arbitrary", "arbitrary")),
        name="adaln",
    )(cond, w_ada, b_ada.reshape(depth, 1, 6 * D_MODEL))


def _norm_modulate(x, n_ref, sh_ref, sc_ref):
    ms = jnp.mean(x * x, axis=-1, keepdims=True)
    return ((x * lax.rsqrt(ms + EPS)) * (n_ref[...] * (1.0 + sc_ref[...])) + sh_ref[...]).astype(jnp.bfloat16)


def _head_rms_scale(a):
    low = lax.broadcasted_iota(jnp.int32, (a.shape[0], LANES), 1) < HEAD_DIM
    blocks = []
    for p in range(a.shape[1] // LANES):
        sq = jnp.square(a[:, p * LANES:(p + 1) * LANES])
        s_lo = jnp.sum(jnp.where(low, sq, 0.0), axis=-1, keepdims=True)
        s_hi = jnp.sum(jnp.where(low, 0.0, sq), axis=-1, keepdims=True)
        blocks.append(jnp.where(low, lax.rsqrt(s_lo * (1.0 / HEAD_DIM) + EPS),
                                lax.rsqrt(s_hi * (1.0 / HEAD_DIM) + EPS)))
    return jnp.concatenate(blocks, axis=-1)


STRIP_W = 8
N_STRIPS = GRID_W // STRIP_W


def _store_keys(ref, x):
    if len(ref.shape) == 2:
        ref[...] = x.astype(jnp.bfloat16)
        return
    pair = 2 * STRIP_W
    for s in range(N_STRIPS):
        for rp in range(x.shape[0] // (2 * GRID_W)):
            top = 2 * rp * GRID_W + s * STRIP_W
            rows = jnp.concatenate([x[top:top + STRIP_W], x[top + GRID_W:top + GRID_W + STRIP_W]], axis=0)
            ref[s, rp * pair:(rp + 1) * pair, :] = rows.astype(jnp.bfloat16)


POOL_HALO = 8
HALO_BLOCK = 16
STREAM_DTYPE = jnp.bfloat16


POOL_EDGE = 16


def _pool_mix(xe_ref, w_ref, scale_ref, tm, seq_len):
    low = lax.broadcasted_iota(jnp.int32, (tm, LANES), 1) < HEAD_DIM
    t_edge = pl.program_id(0) * tm + lax.broadcasted_iota(jnp.int32, (POOL_EDGE, LANES), 0)

    def window_mean(s, half):
        mean = s * (1.0 / (2 * half))

        def clip_fix(t):
            count = (jnp.minimum(t + half, seq_len) - jnp.maximum(t - half, 0)).astype(jnp.float32)
            return (2.0 * half) / count

        return jnp.concatenate([mean[:POOL_EDGE] * clip_fix(t_edge),
                                mean[POOL_EDGE:tm - POOL_EDGE],
                                mean[tm - POOL_EDGE:] * clip_fix(t_edge + (tm - POOL_EDGE))], axis=0)

    def window_sums(xs, n_levels):
        sums = []
        s = xs
        for k in range(n_levels):
            step = 1 << k
            s = s[:-step] + s[step:]
            sums.append(s)
        return sums

    outs = []
    for half_block, windows in enumerate(((2, 4), (8, 16))):
        xs = xe_ref[:, half_block * LANES:(half_block + 1) * LANES]
        sums = window_sums(xs, int(np.log2(windows[1])))
        parts = []
        for w in windows:
            half = w // 2
            s = sums[int(np.log2(w)) - 1][POOL_HALO - half:POOL_HALO - half + tm]
            parts.append(window_mean(s, half))
        mean = jnp.where(low, parts[0], parts[1])
        outs.append(mean - xs[POOL_HALO:POOL_HALO + tm])
    d = jnp.concatenate(outs, axis=-1).astype(jnp.bfloat16)
    return (_dot(d, w_ref[...]) * scale_ref[...]).astype(jnp.bfloat16)


def _inproj_kernel(*refs, pending, seq_len):
    n_stream = 7 if pending else 3
    stream, refs = refs[:n_stream], refs[n_stream:]
    if pending:
        h_ref, y_ref, g_ref, hp_ref, hn_ref, yp_ref, yn_ref = stream
        hres_ref, refs = refs[-2], refs[:-2] + refs[-1:]
        x = h_ref[...].astype(jnp.float32) + g_ref[...] * _moe_out(y_ref)
        hres_ref[...] = x.astype(hres_ref.dtype)
        before = hp_ref[...].astype(jnp.float32) + g_ref[...] * _moe_out(yp_ref)
        after = hn_ref[...].astype(jnp.float32) + g_ref[...] * _moe_out(yn_ref)
    else:
        h_ref, hp_ref, hn_ref = stream
        x = h_ref[...].astype(jnp.float32)
        before = hp_ref[...].astype(jnp.float32)
        after = hn_ref[...].astype(jnp.float32)
    x_halo = jnp.concatenate([before[HALO_BLOCK - POOL_HALO:], after[:POOL_HALO]], axis=0)
    (n1_ref, sh_ref, sc_ref, w_ref, qg_ref, kg_ref, sgn_ref, sgw_ref, sgb_ref,
     pw_ref, ps_ref, pool_ref, q_ref, k_ref, v_ref, sg_ref, xe_ref) = refs
    tm = h_ref.shape[0]
    i = pl.program_id(0)
    hn = _norm_modulate(x, n1_ref, sh_ref, sc_ref)
    hn_halo = _norm_modulate(x_halo, n1_ref, sh_ref, sc_ref)

    a_halo = _dot(hn_halo, w_ref[:, 0:Q_OFF])
    xe_ref[0:POOL_HALO, :] = jnp.where(i > 0, a_halo[:POOL_HALO], 0.0)
    xe_ref[POOL_HALO:POOL_HALO + tm, :] = _dot(hn, w_ref[:, 0:Q_OFF])
    xe_ref[POOL_HALO + tm:, :] = jnp.where(i < pl.num_programs(0) - 1, a_halo[POOL_HALO:], 0.0)

    a_g = _dot(hn, w_ref[:, G_OFF:IN_DIM])
    a_u = _dot(hn, w_ref[:, U_OFF:G_OFF])
    a_q = _dot(hn, w_ref[:, Q_OFF:K_OFF])
    a_k = _dot(hn, w_ref[:, K_OFF:V_OFF])
    _store_keys(v_ref, _dot(hn, w_ref[:, V_OFF:U_OFF]))

    gv = _gelu_tanh(a_g)
    q_ref[...] = (a_q * _head_rms_scale(a_q) * qg_ref[...]).astype(jnp.bfloat16)
    _store_keys(k_ref, a_k * _head_rms_scale(a_k) * kg_ref[...])

    u = _gelu_tanh(a_u)
    vn = (gv * _head_rms_scale(gv) * sgn_ref[...]).astype(jnp.bfloat16)
    low = lax.broadcasted_iota(jnp.int32, (SG_CHUNK, LANES), 1) < HEAD_DIM
    for c in range(tm // SG_CHUNK):
        rows = slice(c * SG_CHUNK, (c + 1) * SG_CHUNK)
        for s in range(SG_DIM // LANES):
            cols = slice(s * LANES, (s + 1) * LANES)
            m = _dot(sgw_ref[s], vn[rows, cols]) + sgb_ref[s]
            mixed = jnp.where(low, m[:SG_CHUNK], m[SG_CHUNK:])
            sg_ref[rows, cols] = (u[rows, cols] * mixed).astype(jnp.bfloat16)

    pool_ref[...] = _pool_mix(xe_ref, pw_ref, ps_ref, tm, seq_len)


def _inproj_call(stream, n1, sh, sc, w_in, layer, qg, kg, sgn, sgw, sgb, pool_w, pool_scale, tm, strips):
    pending = len(stream) == 3
    n = stream[0].shape[0]
    if strips:
        kv_spec = pl.BlockSpec((N_STRIPS, tm // N_STRIPS, NA_DIM), lambda i: (0, i, 0))
        kv_shape = jax.ShapeDtypeStruct((N_STRIPS, n // N_STRIPS, NA_DIM), jnp.bfloat16)
    else:
        kv_spec = pl.BlockSpec((tm, NA_DIM), lambda i: (i, 0))
        kv_shape = jax.ShapeDtypeStruct((n, NA_DIM), jnp.bfloat16)
    row = lambda i: (i, 0)
    fixed2 = lambda i: (0, 0)
    fixed3 = lambda i: (0, 0, 0)
    vec = lambda w: pl.BlockSpec((1, w), fixed2)
    rows = pl.BlockSpec((tm, D_MODEL), row)
    per_tile = tm // HALO_BLOCK
    before = lambda i: (jnp.maximum(i * per_tile - 1, 0), 0)
    after = lambda i: (jnp.minimum((i + 1) * per_tile, n // HALO_BLOCK - 1), 0)
    halo = lambda w, m: pl.BlockSpec((HALO_BLOCK, w), m)
    h = stream[0]
    if pending:
        y, g = stream[1], stream[2]
        yw = y.shape[1]
        args = [h, y, g, h, h, y, y]
        stream_specs = [rows, pl.BlockSpec((tm, yw), row), vec(D_MODEL),
                        halo(D_MODEL, before), halo(D_MODEL, after), halo(yw, before), halo(yw, after)]
    else:
        args = [h, h, h]
        stream_specs = [rows, halo(D_MODEL, before), halo(D_MODEL, after)]
    extra_out_specs = [rows] if pending else []
    extra_out_shape = [jax.ShapeDtypeStruct((n, D_MODEL), STREAM_DTYPE)] if pending else []
    return pl.pallas_call(
        functools.partial(_inproj_kernel, pending=pending, seq_len=n),
        grid=(n // tm,),
        in_specs=stream_specs + [
            vec(D_MODEL), vec(D_MODEL), vec(D_MODEL),
            pl.BlockSpec((None, D_MODEL, IN_DIM), lambda i: (layer, 0, 0)),
            vec(NA_DIM), vec(NA_DIM),
            vec(SG_DIM),
            pl.BlockSpec((SG_DIM // LANES, 2 * SG_CHUNK, SG_CHUNK), fixed3),
            pl.BlockSpec((SG_DIM // LANES, 2 * SG_CHUNK, LANES), fixed3),
            pl.BlockSpec((POOL_DIM, POOL_DIM), fixed2),
            vec(POOL_DIM),
        ],
        out_specs=[
            pl.BlockSpec((tm, POOL_DIM), row),
            pl.BlockSpec((tm, NA_DIM), row),
            kv_spec,
            kv_spec,
            pl.BlockSpec((tm, SG_DIM), row),
        ] + extra_out_specs,
        out_shape=[
            jax.ShapeDtypeStruct((n, POOL_DIM), jnp.bfloat16),
            jax.ShapeDtypeStruct((n, NA_DIM), jnp.bfloat16),
            kv_shape,
            kv_shape,
            jax.ShapeDtypeStruct((n, SG_DIM), jnp.bfloat16),
        ] + extra_out_shape,
        scratch_shapes=[pltpu.VMEM((tm + 2 * POOL_HALO, POOL_DIM), jnp.float32)],
        compiler_params=_cparams(("arbitrary",)),
        name="inproj",
    )(*args, n1, sh, sc, w_in, qg, kg, sgn, sgw, sgb, pool_w, pool_scale)


NA_ROWS_PER_BLOCK = 64
NA_GROUP_ROWS = 4
NA_WINDOW_ROWS = NA_GROUP_ROWS + NA_WIN_ROWS
NA_BLOCK = NA_ROWS_PER_BLOCK * GRID_W
NA_GROUP = NA_GROUP_ROWS * GRID_W
NA_HALF_COLS = GRID_W // 2
NA_CHAIN = NA_GROUP_ROWS * NA_HALF_COLS
NA_HALF_STRIPS = 5
NA_HALF_COL0 = (0, GRID_W - NA_HALF_STRIPS * STRIP_W)
NA_RUN = NA_WINDOW_ROWS * STRIP_W
NA_LOCAL = 512
NA_STRIP_BLOCK = NA_ROWS_PER_BLOCK * STRIP_W
NA_STRIP_HALO = (NA_WIN_ROWS // 2) * STRIP_W
NA_EDGE_FIRST, NA_EDGE_NONE, NA_EDGE_LAST = 0, 1, 2


def _stack_heads(x, low):
    zero = jnp.zeros_like(x)
    return jnp.concatenate([jnp.where(low, x, zero), jnp.where(low, zero, x)], axis=0)


def _natten_kernel(q_ref, kp_ref, kc_ref, kn_ref, vp_ref, vc_ref, vn_ref, kx_ref, vx_ref, bias_ref,
                   wg_ref, wu_ref, wd_ref,
                   o_ref, wg_bf_ref, wu_bf_ref, wd_bf_ref, kwin_ref, vwin_ref, vxe_ref, *, grid_rows):
    b = pl.program_id(1)
    wg_bf_ref[...] = wg_ref[...].astype(jnp.bfloat16)
    wu_bf_ref[...] = wu_ref[...].astype(jnp.bfloat16)
    wd_bf_ref[...] = wd_ref[...].astype(jnp.bfloat16)
    top, bottom = NA_STRIP_HALO, NA_STRIP_HALO + NA_STRIP_BLOCK
    kwin_ref[:, 0:top, :] = kp_ref[...]
    kwin_ref[:, top:bottom, :] = kc_ref[...]
    kwin_ref[:, bottom:, :] = kn_ref[...]
    vwin_ref[:, 0:top, 0:LANES] = vp_ref[...]
    vwin_ref[:, top:bottom, 0:LANES] = vc_ref[...]
    vwin_ref[:, bottom:, 0:LANES] = vn_ref[...]
    vwin_ref[:, :, LANES:] = jnp.ones(vwin_ref.shape[:2] + (LANES,), jnp.bfloat16)
    vxe_ref[:, 0:LANES] = vx_ref[...]
    vxe_ref[:, LANES:] = jnp.ones((vxe_ref.shape[0], LANES), jnp.bfloat16)
    low_q = lax.broadcasted_iota(jnp.int32, (NA_CHAIN, LANES), 1) < HEAD_DIM
    n_pad = NA_LOCAL - NA_HALF_STRIPS * NA_RUN

    def window_start(g):
        r0 = b * NA_ROWS_PER_BLOCK + g * NA_GROUP_ROWS
        ws = jnp.clip(r0 - NA_WIN_ROWS // 2, 0, grid_rows - NA_WINDOW_ROWS)
        edge = jnp.where(r0 == 0, NA_EDGE_FIRST,
                         jnp.where(r0 == grid_rows - NA_GROUP_ROWS, NA_EDGE_LAST, NA_EDGE_NONE))
        start = pl.multiple_of((ws - b * NA_ROWS_PER_BLOCK + NA_WIN_ROWS // 2) * STRIP_W, NA_STRIP_HALO)
        return start, edge

    def local_window(win_ref, g, half):
        start, _ = window_start(g)
        s0 = NA_HALF_COL0[half] // STRIP_W
        runs = [win_ref[s, pl.ds(start, NA_RUN), :] for s in range(s0, s0 + NA_HALF_STRIPS)]
        return jnp.concatenate(runs + [jnp.zeros((n_pad, win_ref.shape[2]), jnp.bfloat16)], axis=0)

    def query_rows(g, half, j):
        first = g * NA_GROUP + j * GRID_W + half * NA_HALF_COLS
        return slice(first, first + NA_HALF_COLS)

    def scores(c):
        g, half = divmod(c, 2)
        _, edge = window_start(g)
        qh = jnp.concatenate([q_ref[query_rows(g, half, j), :] for j in range(NA_GROUP_ROWS)], axis=0)
        lhs = _stack_heads(qh, low_q)
        kl = local_window(kwin_ref, g, half)
        return jnp.concatenate([_dot_nt(lhs, kl).astype(jnp.bfloat16) + bias_ref[edge, half],
                                _dot_nt(lhs, kx_ref[...]).astype(jnp.bfloat16)], axis=-1)

    n_chains = 2 * (NA_ROWS_PER_BLOCK // NA_GROUP_ROWS)
    s_next = scores(0)
    for c in range(n_chains):
        s = s_next
        if c + 1 < n_chains:
            s_next = scores(c + 1)
        g, half = divmod(c, 2)
        vl = local_window(vwin_ref, g, half)
        m = jnp.max(s, axis=-1, keepdims=True)
        pb = jnp.exp2(s - m)
        o = _dot(pb[:, :NA_LOCAL], vl) + _dot(pb[:, NA_LOCAL:], vxe_ref[...])
        o = o[:, :LANES] * (1.0 / o[:, LANES:])
        o = jnp.where(low_q, o[:NA_CHAIN], o[NA_CHAIN:]).astype(jnp.bfloat16)
        for j in range(NA_GROUP_ROWS):
            o_ref[query_rows(g, half, j), :] = o[j * NA_HALF_COLS:(j + 1) * NA_HALF_COLS]


def _natten_call(q, k, v, k_ctx, v_ctx, bias, w_gate, w_up, w_down, layer):
    n = q.shape[0]
    grid_rows = n // GRID_W
    assert grid_rows % NA_ROWS_PER_BLOCK == 0 and grid_rows >= 2 * NA_ROWS_PER_BLOCK
    nblk = n // NA_BLOCK
    steps = HEAD_PAIRS * nblk
    depth = w_gate.shape[0]
    up_rows = N_EXPERTS * D_MODEL
    down_rows = N_EXPERTS * D_EXPERT
    assert up_rows % steps == 0 and down_rows % steps == 0
    wg2 = w_gate.reshape(depth * up_rows, D_EXPERT)
    wu2 = w_up.reshape(depth * up_rows, D_EXPERT)
    wd2 = w_down.reshape(depth * down_rows, D_MODEL)
    up_in = pl.BlockSpec((up_rows // steps, D_EXPERT), lambda p, b: (layer * steps + p * nblk + b, 0))
    down_in = pl.BlockSpec((down_rows // steps, D_MODEL), lambda p, b: (layer * steps + p * nblk + b, 0))
    up_out = pl.BlockSpec((up_rows // steps, D_EXPERT), lambda p, b: (p * nblk + b, 0))
    down_out = pl.BlockSpec((down_rows // steps, D_MODEL), lambda p, b: (p * nblk + b, 0))
    n_halo = n // N_STRIPS // NA_STRIP_HALO
    hb = NA_STRIP_BLOCK // NA_STRIP_HALO
    rows = pl.BlockSpec((NA_BLOCK, LANES), lambda p, b: (b, p))
    cur = pl.BlockSpec((N_STRIPS, NA_STRIP_BLOCK, LANES), lambda p, b: (0, b, p))
    prev = pl.BlockSpec((N_STRIPS, NA_STRIP_HALO, LANES), lambda p, b: (0, jnp.maximum(b * hb - 1, 0), p))
    nxt = pl.BlockSpec((N_STRIPS, NA_STRIP_HALO, LANES),
                       lambda p, b: (0, jnp.minimum((b + 1) * hb, n_halo - 1), p))
    ctx = pl.BlockSpec((k_ctx.shape[0], LANES), lambda p, b: (0, p))
    win_rows = NA_STRIP_BLOCK + 2 * NA_STRIP_HALO
    att, wg_bf, wu_bf, wd_bf = pl.pallas_call(
        functools.partial(_natten_kernel, grid_rows=grid_rows),
        grid=(HEAD_PAIRS, nblk),
        in_specs=[rows, prev, cur, nxt, prev, cur, nxt, ctx, ctx,
                  pl.BlockSpec((None, 3, 2, 2 * NA_CHAIN, NA_LOCAL), lambda p, b: (p, 0, 0, 0, 0)),
                  up_in, up_in, down_in],
        out_specs=[rows, up_out, up_out, down_out],
        out_shape=[jax.ShapeDtypeStruct((n, NA_DIM), jnp.bfloat16),
                   jax.ShapeDtypeStruct((up_rows, D_EXPERT), jnp.bfloat16),
                   jax.ShapeDtypeStruct((up_rows, D_EXPERT), jnp.bfloat16),
                   jax.ShapeDtypeStruct((down_rows, D_MODEL), jnp.bfloat16)],
        scratch_shapes=[pltpu.VMEM((N_STRIPS, win_rows, LANES), jnp.bfloat16),
                        pltpu.VMEM((N_STRIPS, win_rows, 2 * LANES), jnp.bfloat16),
                        pltpu.VMEM((k_ctx.shape[0], 2 * LANES), jnp.bfloat16)],
        compiler_params=_cparams(("arbitrary", "arbitrary")),
        name="natten",
    )(q, k, k, k, v, v, v, k_ctx, v_ctx, bias, wg2, wu2, wd2)
    return (att, wg_bf.reshape(N_EXPERTS, D_MODEL, D_EXPERT), wu_bf.reshape(N_EXPERTS, D_MODEL, D_EXPERT),
            wd_bf.reshape(N_EXPERTS, D_EXPERT, D_MODEL))


def _natten_bias(rpb):
    cols = np.arange(GRID_W)
    col_start = np.clip(cols - NA_WIN_COLS // 2, 0, GRID_W - NA_WIN_COLS)
    kc = np.arange(GRID_W)
    in_win = (kc[None, :] >= col_start[:, None]) & (kc[None, :] < col_start[:, None] + NA_WIN_COLS)
    dc = kc[None, :] - cols[:, None] + NA_WIN_COLS - 1
    sel = (np.arange(2 * NA_WIN_COLS - 1)[:, None, None] == dc[None]) & in_win[None]
    t2 = jnp.einsum("hdj,jqk->hdqk", rpb, jnp.asarray(sel, jnp.float32), precision=lax.Precision.HIGHEST)
    t2 = jnp.where(in_win[None, None], t2 * LOG2_E, NEG_BIG)
    neg = jnp.full((NA_HEADS, 1, GRID_W, GRID_W), NEG_BIG, jnp.float32)
    t2e = jnp.concatenate([neg, t2, neg], axis=1)
    u = jnp.concatenate([t2e[:, :-1], t2e[:, 1:]], axis=-1)
    u = u.reshape(HEAD_PAIRS, 2, 2 * NA_WIN_ROWS, GRID_W, LANES)

    place = np.zeros((2, NA_WINDOW_ROWS * GRID_W, NA_LOCAL), np.float32)
    for half, c0 in enumerate(NA_HALF_COL0):
        for a in range(NA_WINDOW_ROWS):
            for kcol in range(c0, c0 + NA_HALF_STRIPS * STRIP_W):
                s, c8 = divmod(kcol - c0, STRIP_W)
                place[half, a * GRID_W + kcol, s * NA_RUN + a * STRIP_W + c8] = 1.0
    outside = np.full((3, NA_GROUP_ROWS, NA_LOCAL), NEG_BIG, np.float32)
    for edge in (NA_EDGE_FIRST, NA_EDGE_NONE, NA_EDGE_LAST):
        for j in range(NA_GROUP_ROWS):
            lo, _ = _window_rows(edge, j)
            for s in range(NA_HALF_STRIPS):
                outside[edge, j, s * NA_RUN + lo * STRIP_W:s * NA_RUN + (lo + NA_WIN_ROWS) * STRIP_W] = 0.0
    return pl.pallas_call(
        _bias_expand_kernel,
        grid=(HEAD_PAIRS,),
        in_specs=[pl.BlockSpec((None, 2, 2 * NA_WIN_ROWS, GRID_W, LANES), lambda p: (p, 0, 0, 0, 0)),
                  pl.BlockSpec(place.shape, lambda p: (0, 0, 0)),
                  pl.BlockSpec(outside.shape, lambda p: (0, 0, 0))],
        out_specs=pl.BlockSpec((None, 3, 2, 2 * NA_CHAIN, NA_LOCAL), lambda p: (p, 0, 0, 0, 0)),
        out_shape=jax.ShapeDtypeStruct((HEAD_PAIRS, 3, 2, 2 * NA_CHAIN, NA_LOCAL), jnp.bfloat16),
        compiler_params=_cparams(("arbitrary",)),
        name="bias_expand",
    )(u, jnp.asarray(place, jnp.bfloat16), jnp.asarray(outside))


def _window_rows(edge, j):
    if edge == NA_EDGE_FIRST:
        return 0, NA_WIN_ROWS - 1 - j
    if edge == NA_EDGE_NONE:
        return j, NA_WIN_ROWS // 2 - 1
    return NA_WINDOW_ROWS - NA_WIN_ROWS, NA_WIN_ROWS // 2 - 1 - j


def _bias_expand_kernel(u_ref, place_ref, outside_ref, o_ref):
    low = lax.broadcasted_iota(jnp.int32, (NA_HALF_COLS, LANES), 1) < GRID_W
    zero = jnp.zeros((NA_HALF_COLS, LANES), jnp.float32)
    for edge in (NA_EDGE_FIRST, NA_EDGE_NONE, NA_EDGE_LAST):
        for half in range(2):
            q0 = half * NA_HALF_COLS
            blocks, masks = [], []
            for hd in range(2):
                for j in range(NA_GROUP_ROWS):
                    lo, base = _window_rows(edge, j)
                    tiles = []
                    for i in range(NA_WINDOW_ROWS // 2):
                        a0, a1 = 2 * i, 2 * i + 1
                        ok0 = lo <= a0 < lo + NA_WIN_ROWS
                        ok1 = lo <= a1 < lo + NA_WIN_ROWS
                        if not (ok0 or ok1):
                            tile = zero
                        else:
                            tile = u_ref[hd, base + a1 - lo, q0:q0 + NA_HALF_COLS, :]
                            if not ok0:
                                tile = jnp.where(low, zero, tile)
                            if not ok1:
                                tile = jnp.where(low, tile, zero)
                        tiles.append(tile)
                    blocks.append(jnp.concatenate(tiles, axis=-1))
                    masks.append(jnp.broadcast_to(outside_ref[edge, j:j + 1, :], (NA_HALF_COLS, NA_LOCAL)))
            lhs = jnp.concatenate(blocks, axis=0).astype(jnp.bfloat16)
            placed = _dot(lhs, place_ref[half]) + jnp.concatenate(masks, axis=0)
            o_ref[edge, half] = placed.astype(o_ref.dtype)


def _ctxatt_kernel(q_ref, k_ref, v_ref, o_ref):
    lc = q_ref.shape[0]
    low = lax.broadcasted_iota(jnp.int32, (lc, LANES), 1) < HEAD_DIM
    lhs = _stack_heads(q_ref[...], low)
    s = _dot_nt(lhs, k_ref[...])
    m = jnp.max(s, axis=-1, keepdims=True)
    p = jnp.exp2(s - m)
    denom = jnp.sum(p, axis=-1, keepdims=True)
    o = _dot(p.astype(jnp.bfloat16), v_ref[...]) * (1.0 / denom)
    o_ref[...] = jnp.where(low, o[:lc], o[lc:]).astype(jnp.bfloat16)


def _ctxatt_call(q, k, v):
    lc = q.shape[0]
    spec = pl.BlockSpec((lc, LANES), lambda p: (0, p))
    return pl.pallas_call(
        _ctxatt_kernel,
        grid=(HEAD_PAIRS,),
        in_specs=[spec, spec, spec],
        out_specs=spec,
        out_shape=jax.ShapeDtypeStruct((lc, NA_DIM), jnp.bfloat16),
        compiler_params=_cparams(("arbitrary",)),
        name="ctxatt",
    )(q, k, v)


OUTPROJ_CHAIN = 256
OUTPROJ_TILE = 1024
INPROJ_TILE = 1024
RESIDUAL_TILE = 2048


def _outproj_kernel(h_ref, mp_ref, att_ref, sg_ref, wo_ref, g1_ref, n2_ref, sh_ref, sc_ref,
                    wr_ref, br_ref, h1_ref, pay_ref, route_ref):
    chains = [slice(c * OUTPROJ_CHAIN, (c + 1) * OUTPROJ_CHAIN) for c in range(h_ref.shape[0] // OUTPROJ_CHAIN)]
    h1s = []
    for rows in chains:
        mix = (_dot(mp_ref[rows, :], wo_ref[0:POOL_DIM, :])
               + _dot(att_ref[rows, :], wo_ref[POOL_DIM:POOL_DIM + NA_DIM, :])
               + _dot(sg_ref[rows, :], wo_ref[POOL_DIM + NA_DIM:, :]))
        h1 = h_ref[rows, :].astype(jnp.float32) + g1_ref[...] * mix
        h1_ref[rows, :] = h1.astype(h1_ref.dtype)
        h1s.append(h1)
    for rows, h1 in zip(chains, h1s):
        _outproj_route(rows, h1, n2_ref, sh_ref, sc_ref, wr_ref, br_ref, pay_ref, route_ref)


def _outproj_route(rows, h1, n2_ref, sh_ref, sc_ref, wr_ref, br_ref, pay_ref, route_ref):
    tm = OUTPROJ_CHAIN
    ms = jnp.mean(h1 * h1, axis=-1, keepdims=True)
    hm = (h1 * lax.rsqrt(ms + EPS)) * (n2_ref[...] * (1.0 + sc_ref[...])) + sh_ref[...]
    pay_ref[rows, 0:HALF_D] = _pack_halves(hm)

    hm_hi = hm.astype(jnp.bfloat16)
    lt = _dot_nt(wr_ref[...], hm_hi)
    logits = lt[:N_EXPERTS] + lt[N_EXPERTS:] + br_ref[...]
    e = jnp.exp(logits - jnp.max(logits, axis=0, keepdims=True))

    best = ga = gb = e1 = e2 = cls = None
    for c in range(N_CLASSES):
        a, b2 = int(CLASS_E1[c]), int(CLASS_E2[c])
        ea, eb = e[a:a + 1, :], e[b2:b2 + 1, :]
        s = ea + eb
        if best is None:
            best, ga, gb = s, ea, eb
            e1 = jnp.full_like(s, float(a))
            e2 = jnp.full_like(s, float(b2))
            cls = jnp.zeros_like(s)
        else:
            better = s > best
            best = jnp.where(better, s, best)
            ga = jnp.where(better, ea, ga)
            gb = jnp.where(better, eb, gb)
            e1 = jnp.where(better, float(a), e1)
            e2 = jnp.where(better, float(b2), e2)
            cls = jnp.where(better, float(c), cls)
    inv = 1.0 / best
    row = lax.broadcasted_iota(jnp.int32, (ROUTE_ROWS, tm), 0)
    rec = jnp.where(row == 0, ga * inv,
          jnp.where(row == 1, gb * inv,
          jnp.where(row == 2, e1,
          jnp.where(row == 3, e2,
          jnp.where(row == 4, cls, 0.0)))))
    route_ref[:, rows] = rec
    wide = jnp.concatenate([rec, jnp.zeros((LANES - ROUTE_ROWS, tm), jnp.float32)], axis=0)
    pay_ref[rows, HALF_D:] = pltpu.bitcast(wide.T, jnp.uint32)


def _outproj_call(h, mp, att, sg, w_out, layer, g1, n2, sh2, sc2, wr, br, tm):
    n = h.shape[0]
    row = lambda i: (i, 0)
    fixed = lambda i: (0, 0)
    vec = pl.BlockSpec((1, D_MODEL), fixed)
    return pl.pallas_call(
        _outproj_kernel,
        grid=(n // tm,),
        in_specs=[
            pl.BlockSpec((tm, D_MODEL), row),
            pl.BlockSpec((tm, POOL_DIM), row),
            pl.BlockSpec((tm, NA_DIM), row),
            pl.BlockSpec((tm, SG_DIM), row),
            pl.BlockSpec((None, D_MODEL, D_MODEL), lambda i: (layer, 0, 0)),
            vec, vec, vec, vec,
            pl.BlockSpec((2 * N_EXPERTS, D_MODEL), fixed),
            pl.BlockSpec((N_EXPERTS, 1), fixed),
        ],
        out_specs=[
            pl.BlockSpec((tm, D_MODEL), row),
            pl.BlockSpec((tm, PAYLOAD_W), row),
            pl.BlockSpec((ROUTE_ROWS, tm), lambda i: (0, i)),
        ],
        out_shape=[
            jax.ShapeDtypeStruct((n, D_MODEL), STREAM_DTYPE),
            jax.ShapeDtypeStruct((n, PAYLOAD_W), jnp.uint32),
            jax.ShapeDtypeStruct((ROUTE_ROWS, n), jnp.float32),
        ],
        compiler_params=_cparams(("arbitrary",)),
        name="outproj",
    )(h, mp, att, sg, w_out, g1, n2, sh2, sc2, wr, br)


def _payload_parts(pay_ref, rows=slice(None)):
    lo, hi = _unpack_halves(pay_ref[rows, 0:HALF_D])
    x = jnp.concatenate([lo, hi], axis=-1).astype(jnp.bfloat16)
    return x, pltpu.bitcast(pay_ref[rows, HALF_D:], jnp.float32)


def _expert_pair(x, ga, gb, wga, wua, wda, wgb, wub, wdb):
    ha = (_silu(_dot(x, wga)) * _dot(x, wua) * ga).astype(jnp.bfloat16)
    hb = (_silu(_dot(x, wgb)) * _dot(x, wub) * gb).astype(jnp.bfloat16)
    return _dot(ha, wda) + _dot(hb, wdb)


MOE_QUARTER = MOE_TM // 4


def _moe_sorted_kernel(blk_ref, e1_ref, e2_ref, quarters_ref, pay_ref,
                       wga_ref, wua_ref, wda_ref, wgb_ref, wub_ref, wdb_ref, o_ref):
    quarters = quarters_ref[pl.program_id(0)]

    def run(rows):
        x, route = _payload_parts(pay_ref, rows)
        y = _expert_pair(x, route[:, 0:1], route[:, 1:2], wga_ref[...], wua_ref[...], wda_ref[...],
                         wgb_ref[...], wub_ref[...], wdb_ref[...])
        o_ref[rows, :] = _pack_halves(y)

    for used in range(1, MOE_TM // MOE_QUARTER + 1):
        @pl.when(quarters == used)
        def _(used=used):
            run(slice(0, used * MOE_QUARTER))


def _moe_sorted_call(blk, e1, e2, quarters, pay_sorted, wg, wu, wd):
    n_tiles = blk.shape[0]
    rows = lambda i, blk, e1, e2, quarters: (blk[i], 0)
    wa = lambda i, blk, e1, e2, quarters: (e1[i], 0, 0)
    wb = lambda i, blk, e1, e2, quarters: (e2[i], 0, 0)
    up = lambda m: pl.BlockSpec((None, D_MODEL, D_EXPERT), m)
    down = lambda m: pl.BlockSpec((None, D_EXPERT, D_MODEL), m)
    return pl.pallas_call(
        _moe_sorted_kernel,
        grid_spec=pltpu.PrefetchScalarGridSpec(
            num_scalar_prefetch=4,
            grid=(n_tiles,),
            in_specs=[pl.BlockSpec((MOE_TM, PAYLOAD_W), rows),
                      up(wa), up(wa), down(wa), up(wb), up(wb), down(wb)],
            out_specs=pl.BlockSpec((MOE_TM, HALF_D), rows),
        ),
        out_shape=jax.ShapeDtypeStruct((n_tiles * MOE_TM, HALF_D), jnp.uint32),
        compiler_params=_cparams(("arbitrary",)),
        name="moe_sorted",
    )(blk, e1, e2, quarters, pay_sorted, wg, wu, wd, wg, wu, wd)


DENSE_EXPERTS_PER_STEP = GROUP_SIZE


def _moe_dense_kernel(pay_ref, wg_ref, wu_ref, wd_ref, o_ref):
    step = pl.program_id(0)

    @pl.when(step == 0)
    def _():
        o_ref[...] = jnp.zeros_like(o_ref)

    x, route = _payload_parts(pay_ref)
    y = None
    for k in range(DENSE_EXPERTS_PER_STEP):
        ef = (step * DENSE_EXPERTS_PER_STEP + k).astype(jnp.float32)
        gate = (jnp.where(route[:, 2:3] == ef, route[:, 0:1], 0.0)
                + jnp.where(route[:, 3:4] == ef, route[:, 1:2], 0.0))
        he = (_silu(_dot(x, wg_ref[k])) * _dot(x, wu_ref[k]) * gate).astype(jnp.bfloat16)
        yk = _dot(he, wd_ref[k])
        y = yk if y is None else y + yk
    o_ref[...] += y


def _moe_dense_call(pay, wg, wu, wd):
    n = pay.shape[0]
    per = DENSE_EXPERTS_PER_STEP
    return pl.pallas_call(
        _moe_dense_kernel,
        grid=(N_EXPERTS // per,),
        in_specs=[pl.BlockSpec((n, PAYLOAD_W), lambda e: (0, 0)),
                  pl.BlockSpec((per, D_MODEL, D_EXPERT), lambda e: (e, 0, 0)),
                  pl.BlockSpec((per, D_MODEL, D_EXPERT), lambda e: (e, 0, 0)),
                  pl.BlockSpec((per, D_EXPERT, D_MODEL), lambda e: (e, 0, 0))],
        out_specs=pl.BlockSpec((n, D_MODEL), lambda e: (0, 0)),
        out_shape=jax.ShapeDtypeStruct((n, D_MODEL), jnp.float32),
        compiler_params=_cparams(("arbitrary",)),
        name="moe_dense",
    )(pay, wg, wu, wd)


def _moe_out(y_ref):
    if y_ref.dtype == jnp.uint32:
        return jnp.concatenate(_unpack_halves(y_ref[...]), axis=-1)
    return y_ref[...]


def _residual_kernel(h_ref, y_ref, g_ref, o_ref):
    o_ref[...] = h_ref[...].astype(jnp.float32) + g_ref[...] * _moe_out(y_ref)


def _residual_call(h, y, g, tm):
    n = h.shape[0]
    row = pl.BlockSpec((tm, D_MODEL), lambda i: (i, 0))
    return pl.pallas_call(
        _residual_kernel,
        grid=(n // tm,),
        in_specs=[row, pl.BlockSpec((tm, y.shape[1]), lambda i: (i, 0)),
                  pl.BlockSpec((1, D_MODEL), lambda i: (0, 0))],
        out_specs=row,
        out_shape=jax.ShapeDtypeStruct((n, D_MODEL), jnp.float32),
        compiler_params=_cparams(("arbitrary",)),
        name="residual",
    )(h, y, g)


SC_ROWS = 128


SC_CORES = 2
SC_SUBCORES = 16
SC_WORKERS = SC_CORES * SC_SUBCORES


def _sc_mesh():
    return plsc.VectorSubcoreMesh(core_axis_name="core", subcore_axis_name="subcore")


def _sc_worker():
    return lax.axis_index("subcore") * SC_CORES + lax.axis_index("core")


def _scatter_rows(x, dest, n_out):
    n, w = x.shape

    per_worker = n // SC_WORKERS
    assert per_worker % SC_ROWS == 0

    @functools.partial(pl.kernel, out_type=jax.ShapeDtypeStruct((n_out, w), x.dtype), mesh=_sc_mesh(),
                       scratch_types=[pltpu.VMEM((SC_ROWS,), jnp.int32), pltpu.VMEM((SC_ROWS, w), x.dtype)])
    def scatter(x_hbm, i_hbm, o_hbm, idx_v, rows_v):
        first = _sc_worker() * per_worker

        @pl.loop(0, per_worker // SC_ROWS)
        def _(i):
            base = pl.multiple_of(first + i * SC_ROWS, SC_ROWS)
            pltpu.sync_copy(i_hbm.at[pl.ds(base, SC_ROWS)], idx_v)
            pltpu.sync_copy(x_hbm.at[pl.ds(base, SC_ROWS)], rows_v)
            pltpu.sync_copy(rows_v, o_hbm.at[idx_v])

    return scatter(x, dest)


def _gather_rows(x, src):
    n = src.shape[0]
    w = x.shape[1]

    per_worker = n // SC_WORKERS
    assert per_worker % SC_ROWS == 0

    @functools.partial(pl.kernel, out_type=jax.ShapeDtypeStruct((n, w), x.dtype), mesh=_sc_mesh(),
                       scratch_types=[pltpu.VMEM((SC_ROWS,), jnp.int32), pltpu.VMEM((SC_ROWS, w), x.dtype)])
    def gather(x_hbm, i_hbm, o_hbm, idx_v, rows_v):
        first = _sc_worker() * per_worker

        @pl.loop(0, per_worker // SC_ROWS)
        def _(i):
            base = pl.multiple_of(first + i * SC_ROWS, SC_ROWS)
            pltpu.sync_copy(i_hbm.at[pl.ds(base, SC_ROWS)], idx_v)
            pltpu.sync_copy(x_hbm.at[idx_v], rows_v)
            pltpu.sync_copy(rows_v, o_hbm.at[pl.ds(base, SC_ROWS)])

    return gather(x, src)


def _routing_plan(cls, n_tiles):
    onehot = (cls[:, None] == jnp.arange(N_CLASSES, dtype=jnp.int32)[None, :]).astype(jnp.int32)
    counts = jnp.sum(onehot, axis=0)
    rank = jnp.sum((jnp.cumsum(onehot, axis=0) - onehot) * onehot, axis=1)
    tiles = (counts + MOE_TM - 1) // MOE_TM
    tile_end = jnp.cumsum(tiles)
    tile_start = tile_end - tiles
    dest = jnp.sum(onehot * tile_start[None, :], axis=1) * MOE_TM + rank
    nact = tile_end[-1]
    tile = jnp.arange(n_tiles, dtype=jnp.int32)
    blk = jnp.minimum(tile, nact - 1)
    tile_cls = jnp.sum((blk[:, None] >= tile_end[None, :]).astype(jnp.int32), axis=1)
    e1 = jnp.asarray(CLASS_E1)[tile_cls]
    e2 = jnp.asarray(CLASS_E2)[tile_cls]
    valid = counts[tile_cls] - (blk - tile_start[tile_cls]) * MOE_TM
    quarters = jnp.where(tile >= nact, 0, (jnp.minimum(valid, MOE_TM) + MOE_QUARTER - 1) // MOE_QUARTER)
    return dest.astype(jnp.int32), blk, e1, e2, quarters.astype(jnp.int32)


def _row_tile(n, prefer=512):
    return next(t for t in (prefer, 512, 256) if n % t == 0)


def kernel(x, c, ctx, c_ctx, w_ada, b_ada, norm1, w_in, pool_w, pool_scale, q_norm, k_norm, rpb,
           sg_w, sg_b, sg_norm, w_out, norm2, w_router, b_router, w_gate, w_up, w_down):
    depth = w_ada.shape[0]
    n = x.shape[1]
    lc = ctx.shape[1]
    bf = jnp.bfloat16
    lat_stream = (x[0],)
    h_ctx = ctx[0]

    cond = jnp.zeros((SUBLANES, D_MODEL), jnp.float32).at[0].set(c[0]).at[1].set(c_ctx)
    mod = _ada_call(cond, w_ada, b_ada)

    wr_t = w_router.T
    wr_hi = wr_t.astype(bf)
    wr_lo = (wr_t - wr_hi.astype(jnp.float32)).astype(bf)
    wr = jnp.concatenate([wr_hi, wr_lo], axis=0)
    br = b_router.reshape(N_EXPERTS, 1)
    n_tiles = n // MOE_TM + N_CLASSES
    w_in_bf = w_in.astype(bf)
    w_out_bf = w_out.astype(bf)

    for l in range(depth):
        last = l == depth - 1
        qg =(q_norm[l] * (HEAD_DIM ** -0.5 * LOG2_E)).reshape(1, NA_DIM)
        kg = k_norm[l].reshape(1, NA_DIM)
        sgn = sg_norm[l].reshape(1, SG_DIM)
        sgw = sg_w[l].astype(bf).reshape(SG_DIM // LANES, 2 * SG_CHUNK, SG_CHUNK)
        sgb = jnp.broadcast_to(sg_b[l].reshape(SG_DIM // LANES, 2 * SG_CHUNK, 1),
                               (SG_DIM // LANES, 2 * SG_CHUNK, LANES))
        pool_bd = jax.scipy.linalg.block_diag(*[pool_w[l, g] for g in range(len(POOL_WINDOWS))]).astype(bf)
        pscale = pool_scale[l].reshape(1, POOL_DIM)
        bias = _natten_bias(rpb[l])
        n1 = norm1[l].reshape(1, D_MODEL)
        n2 = norm2[l].reshape(1, D_MODEL)

        def mods(row):
            return [mod[l, row:row + 1, i * D_MODEL:(i + 1) * D_MODEL] for i in range(6)]

        sh1, sc1, g1, sh2, sc2, g2 = mods(0)
        csh1, csc1, cg1, csh2, csc2, cg2 = mods(1)

        tc = _row_tile(lc)
        mix_pool_c, q_c, k_c, v_c, sg_c = _inproj_call((h_ctx,), n1, csh1, csc1, w_in_bf, l, qg, kg,
                                                       sgn, sgw, sgb, pool_bd, pscale, tc, strips=False)

        outs = _inproj_call(lat_stream, n1, sh1, sc1, w_in_bf, l, qg, kg, sgn, sgw, sgb,
                            pool_bd, pscale, _row_tile(n, INPROJ_TILE), strips=True)
        mix_pool, q, k, v, sg = outs[:5]
        h_lat = outs[5] if len(lat_stream) == 3 else lat_stream[0]
        att, wg_l, wu_l, wd_l = _natten_call(q, k, v, k_c, v_c, bias, w_gate, w_up, w_down, l)
        h1, pay, route = _outproj_call(h_lat, mix_pool, att, sg, w_out_bf, l, g1, n2, sh2, sc2, wr, br,
                                       _row_tile(n, OUTPROJ_TILE))
        cls = route[4].astype(jnp.int32)
        dest, blk, e1, e2, quarters = _routing_plan(cls, n_tiles)
        pay_sorted = _scatter_rows(pay, dest, n_tiles * MOE_TM)
        y_sorted = _moe_sorted_call(blk, e1, e2, quarters, pay_sorted, wg_l, wu_l, wd_l)
        y = _gather_rows(y_sorted, dest)
        lat_stream = (h1, y, g2)

        if not last:
            att_c = _ctxatt_call(q_c, k_c, v_c)
            h1_c, pay_c, _ = _outproj_call(h_ctx, mix_pool_c, att_c, sg_c, w_out_bf, l, cg1, n2, csh2, csc2,
                                           wr, br, tc)
            y_c = _moe_dense_call(pay_c, wg_l, wu_l, wd_l)
            h_ctx = _residual_call(h1_c, y_c, cg2, tc)

    return _residual_call(*lat_stream, _row_tile(n, RESIDUAL_TILE))[None]
```

```python
import functools

import jax
import jax.numpy as jnp
import numpy as np
from jax import lax
from jax.experimental import pallas as pl
from jax.experimental.pallas import tpu as pltpu
from jax.experimental.pallas import tpu_sc as plsc

D_MODEL = 1024
GRID_W = 64
HEAD_DIM = 64
POOL_WINDOWS = (2, 4, 8, 16)
POOL_DIM = 256
NA_HEADS = 8
NA_DIM = 512
NA_WIN_ROWS = 8
NA_WIN_COLS = 16
SG_DIM = 256
SG_CHUNK = 128
Q_OFF = POOL_DIM
K_OFF = Q_OFF + NA_DIM
V_OFF = K_OFF + NA_DIM
U_OFF = V_OFF + NA_DIM
G_OFF = U_OFF + SG_DIM
IN_DIM = G_OFF + SG_DIM
N_EXPERTS = 16
GROUP_SIZE = 4
D_EXPERT = 512
EPS = 1e-6

LANES = 128
SUBLANES = 8
HEAD_PAIRS = NA_DIM // LANES
VMEM_LIMIT = 48 * 1024 * 1024

PAIRS = ((0, 1), (0, 2), (1, 2), (1, 3), (0, 3), (2, 3))
N_CLASSES = (N_EXPERTS // GROUP_SIZE) * len(PAIRS)
CLASS_E1 = np.array([4 * g + i for g in range(4) for (i, j) in PAIRS], np.int32)
CLASS_E2 = np.array([4 * g + j for g in range(4) for (i, j) in PAIRS], np.int32)

ROUTE_ROWS = 8
HALF_D = D_MODEL // 2
PAYLOAD_W = HALF_D + LANES
MOE_TM = 1024
NEG_BIG = -1e30
LOG2_E = 1.4426950408889634


def _cparams(sem):
    return pltpu.CompilerParams(dimension_semantics=sem, vmem_limit_bytes=VMEM_LIMIT)


def _dot(a, b):
    return jnp.dot(a, b, preferred_element_type=jnp.float32)


def _dot_nt(a, b):
    return lax.dot_general(a, b, (((1,), (1,)), ((), ())), preferred_element_type=jnp.float32)


def _gelu_tanh(x):
    return 0.5 * x * (1.0 + jnp.tanh(0.7978845608028654 * (x + 0.044715 * (x * x * x))))


def _silu(x):
    return x * (1.0 / (1.0 + jnp.exp(-x)))


def _pack_halves(x):
    w = x.shape[1] // 2
    lo = pltpu.bitcast(x[:, :w].astype(jnp.bfloat16).astype(jnp.float32), jnp.uint32) >> 16
    hi = pltpu.bitcast(x[:, w:].astype(jnp.bfloat16).astype(jnp.float32), jnp.uint32) & jnp.uint32(0xFFFF0000)
    return lo | hi


def _unpack_halves(words):
    lo = pltpu.bitcast(words << 16, jnp.float32)
    hi = pltpu.bitcast(words & jnp.uint32(0xFFFF0000), jnp.float32)
    return lo, hi


def _ada_kernel(cond_ref, w_ref, b_ref, o_ref):
    cond = _silu(cond_ref[...])
    o_ref[...] = jnp.dot(cond, w_ref[...], preferred_element_type=jnp.float32,
                         precision=lax.Precision.HIGHEST) + b_ref[...]


def _ada_call(cond, w_ada, b_ada):
    depth = w_ada.shape[0]
    tn = 1536
    return pl.pallas_call(
        _ada_kernel,
        grid=(depth, 6 * D_MODEL // tn),
        in_specs=[
            pl.BlockSpec((SUBLANES, D_MODEL), lambda l, j: (0, 0)),
            pl.BlockSpec((None, D_MODEL, tn), lambda l, j: (l, 0, j)),
            pl.BlockSpec((None, 1, tn), lambda l, j: (l, 0, j)),
        ],
        out_specs=pl.BlockSpec((None, SUBLANES, tn), lambda l, j: (l, 0, j)),
        out_shape=jax.ShapeDtypeStruct((depth, SUBLANES, 6 * D_MODEL), jnp.float32),
        compiler_params=_cparams(("arbitrary", "arbitrary")),
        name="adaln",
    )(cond, w_ada, b_ada.reshape(depth, 1, 6 * D_MODEL))


def _norm_modulate(x, n_ref, sh_ref, sc_ref):
    ms = jnp.mean(x * x, axis=-1, keepdims=True)
    return ((x * lax.rsqrt(ms + EPS)) * (n_ref[...] * (1.0 + sc_ref[...])) + sh_ref[...]).astype(jnp.bfloat16)


def _head_rms_scale(a):
    low = lax.broadcasted_iota(jnp.int32, (a.shape[0], LANES), 1) < HEAD_DIM
    blocks = []
    for p in range(a.shape[1] // LANES):
        sq = jnp.square(a[:, p * LANES:(p + 1) * LANES])
        s_lo = jnp.sum(jnp.where(low, sq, 0.0), axis=-1, keepdims=True)
        s_hi = jnp.sum(jnp.where(low, 0.0, sq), axis=-1, keepdims=True)
        blocks.append(jnp.where(low, lax.rsqrt(s_lo * (1.0 / HEAD_DIM) + EPS),
                                lax.rsqrt(s_hi * (1.0 / HEAD_DIM) + EPS)))
    return jnp.concatenate(blocks, axis=-1)


STRIP_W = 8
N_STRIPS = GRID_W // STRIP_W


def _store_keys(ref, x):
    if len(ref.shape) == 2:
        ref[...] = x.astype(jnp.bfloat16)
        return
    pair = 2 * STRIP_W
    for s in range(N_STRIPS):
        for rp in range(x.shape[0] // (2 * GRID_W)):
            top = 2 * rp * GRID_W + s * STRIP_W
            rows = jnp.concatenate([x[top:top + STRIP_W], x[top + GRID_W:top + GRID_W + STRIP_W]], axis=0)
            ref[s, rp * pair:(rp + 1) * pair, :] = rows.astype(jnp.bfloat16)


POOL_HALO = 8
HALO_BLOCK = 16
STREAM_DTYPE = jnp.bfloat16


POOL_EDGE = 16


def _pool_mix(xe_ref, w_ref, scale_ref, tm, seq_len):
    low = lax.broadcasted_iota(jnp.int32, (tm, LANES), 1) < HEAD_DIM
    t_edge = pl.program_id(0) * tm + lax.broadcasted_iota(jnp.int32, (POOL_EDGE, LANES), 0)

    def window_mean(s, half):
        mean = s * (1.0 / (2 * half))

        def clip_fix(t):
            count = (jnp.minimum(t + half, seq_len) - jnp.maximum(t - half, 0)).astype(jnp.float32)
            return (2.0 * half) / count

        return jnp.concatenate([mean[:POOL_EDGE] * clip_fix(t_edge),
                                mean[POOL_EDGE:tm - POOL_EDGE],
                                mean[tm - POOL_EDGE:] * clip_fix(t_edge + (tm - POOL_EDGE))], axis=0)

    def window_sums(xs, n_levels):
        sums = []
        s = xs
        for k in range(n_levels):
            step = 1 << k
            s = s[:-step] + s[step:]
            sums.append(s)
        return sums

    outs = []
    for half_block, windows in enumerate(((2, 4), (8, 16))):
        xs = xe_ref[:, half_block * LANES:(half_block + 1) * LANES]
        sums = window_sums(xs, int(np.log2(windows[1])))
        parts = []
        for w in windows:
            half = w // 2
            s = sums[int(np.log2(w)) - 1][POOL_HALO - half:POOL_HALO - half + tm]
            parts.append(window_mean(s, half))
        mean = jnp.where(low, parts[0], parts[1])
        outs.append(mean - xs[POOL_HALO:POOL_HALO + tm])
    d = jnp.concatenate(outs, axis=-1).astype(jnp.bfloat16)
    return (_dot(d, w_ref[...]) * scale_ref[...]).astype(jnp.bfloat16)


def _inproj_kernel(*refs, pending, seq_len):
    n_stream = 7 if pending else 3
    stream, refs = refs[:n_stream], refs[n_stream:]
    if pending:
        h_ref, y_ref, g_ref, hp_ref, hn_ref, yp_ref, yn_ref = stream
        hres_ref, refs = refs[-2], refs[:-2] + refs[-1:]
        x = h_ref[...].astype(jnp.float32) + g_ref[...] * _moe_out(y_ref)
        hres_ref[...] = x.astype(hres_ref.dtype)
        before = hp_ref[...].astype(jnp.float32) + g_ref[...] * _moe_out(yp_ref)
        after = hn_ref[...].astype(jnp.float32) + g_ref[...] * _moe_out(yn_ref)
    else:
        h_ref, hp_ref, hn_ref = stream
        x = h_ref[...].astype(jnp.float32)
        before = hp_ref[...].astype(jnp.float32)
        after = hn_ref[...].astype(jnp.float32)
    x_halo = jnp.concatenate([before[HALO_BLOCK - POOL_HALO:], after[:POOL_HALO]], axis=0)
    (n1_ref, sh_ref, sc_ref, w_ref, qg_ref, kg_ref, sgn_ref, sgw_ref, sgb_ref,
     pw_ref, ps_ref, pool_ref, q_ref, k_ref, v_ref, sg_ref, xe_ref) = refs
    tm = h_ref.shape[0]
    i = pl.program_id(0)
    hn = _norm_modulate(x, n1_ref, sh_ref, sc_ref)
    hn_halo = _norm_modulate(x_halo, n1_ref, sh_ref, sc_ref)

    a_halo = _dot(hn_halo, w_ref[:, 0:Q_OFF])
    xe_ref[0:POOL_HALO, :] = jnp.where(i > 0, a_halo[:POOL_HALO], 0.0)
    xe_ref[POOL_HALO:POOL_HALO + tm, :] = _dot(hn, w_ref[:, 0:Q_OFF])
    xe_ref[POOL_HALO + tm:, :] = jnp.where(i < pl.num_programs(0) - 1, a_halo[POOL_HALO:], 0.0)

    a_g = _dot(hn, w_ref[:, G_OFF:IN_DIM])
    a_u = _dot(hn, w_ref[:, U_OFF:G_OFF])
    a_q = _dot(hn, w_ref[:, Q_OFF:K_OFF])
    a_k = _dot(hn, w_ref[:, K_OFF:V_OFF])
    _store_keys(v_ref, _dot(hn, w_ref[:, V_OFF:U_OFF]))

    gv = _gelu_tanh(a_g)
    q_ref[...] = (a_q * _head_rms_scale(a_q) * qg_ref[...]).astype(jnp.bfloat16)
    _store_keys(k_ref, a_k * _head_rms_scale(a_k) * kg_ref[...])

    u = _gelu_tanh(a_u)
    vn = (gv * _head_rms_scale(gv) * sgn_ref[...]).astype(jnp.bfloat16)
    low = lax.broadcasted_iota(jnp.int32, (SG_CHUNK, LANES), 1) < HEAD_DIM
    for c in range(tm // SG_CHUNK):
        rows = slice(c * SG_CHUNK, (c + 1) * SG_CHUNK)
        for s in range(SG_DIM // LANES):
            cols = slice(s * LANES, (s + 1) * LANES)
            m = _dot(sgw_ref[s], vn[rows, cols]) + sgb_ref[s]
            mixed = jnp.where(low, m[:SG_CHUNK], m[SG_CHUNK:])
            sg_ref[rows, cols] = (u[rows, cols] * mixed).astype(jnp.bfloat16)

    pool_ref[...] = _pool_mix(xe_ref, pw_ref, ps_ref, tm, seq_len)


def _inproj_call(stream, n1, sh, sc, w_in, layer, qg, kg, sgn, sgw, sgb, pool_w, pool_scale, tm, strips):
    pending = len(stream) == 3
    n = stream[0].shape[0]
    if strips:
        kv_spec = pl.BlockSpec((N_STRIPS, tm // N_STRIPS, NA_DIM), lambda i: (0, i, 0))
        kv_shape = jax.ShapeDtypeStruct((N_STRIPS, n // N_STRIPS, NA_DIM), jnp.bfloat16)
    else:
        kv_spec = pl.BlockSpec((tm, NA_DIM), lambda i: (i, 0))
        kv_shape = jax.ShapeDtypeStruct((n, NA_DIM), jnp.bfloat16)
    row = lambda i: (i, 0)
    fixed2 = lambda i: (0, 0)
    fixed3 = lambda i: (0, 0, 0)
    vec = lambda w: pl.BlockSpec((1, w), fixed2)
    rows = pl.BlockSpec((tm, D_MODEL), row)
    per_tile = tm // HALO_BLOCK
    before = lambda i: (jnp.maximum(i * per_tile - 1, 0), 0)
    after = lambda i: (jnp.minimum((i + 1) * per_tile, n // HALO_BLOCK - 1), 0)
    halo = lambda w, m: pl.BlockSpec((HALO_BLOCK, w), m)
    h = stream[0]
    if pending:
        y, g = stream[1], stream[2]
        yw = y.shape[1]
        args = [h, y, g, h, h, y, y]
        stream_specs = [rows, pl.BlockSpec((tm, yw), row), vec(D_MODEL),
                        halo(D_MODEL, before), halo(D_MODEL, after), halo(yw, before), halo(yw, after)]
    else:
        args = [h, h, h]
        stream_specs = [rows, halo(D_MODEL, before), halo(D_MODEL, after)]
    extra_out_specs = [rows] if pending else []
    extra_out_shape = [jax.ShapeDtypeStruct((n, D_MODEL), STREAM_DTYPE)] if pending else []
    return pl.pallas_call(
        functools.partial(_inproj_kernel, pending=pending, seq_len=n),
        grid=(n // tm,),
        in_specs=stream_specs + [
            vec(D_MODEL), vec(D_MODEL), vec(D_MODEL),
            pl.BlockSpec((None, D_MODEL, IN_DIM), lambda i: (layer, 0, 0)),
            vec(NA_DIM), vec(NA_DIM),
            vec(SG_DIM),
            pl.BlockSpec((SG_DIM // LANES, 2 * SG_CHUNK, SG_CHUNK), fixed3),
            pl.BlockSpec((SG_DIM // LANES, 2 * SG_CHUNK, LANES), fixed3),
            pl.BlockSpec((POOL_DIM, POOL_DIM), fixed2),
            vec(POOL_DIM),
        ],
        out_specs=[
            pl.BlockSpec((tm, POOL_DIM), row),
            pl.BlockSpec((tm, NA_DIM), row),
            kv_spec,
            kv_spec,
            pl.BlockSpec((tm, SG_DIM), row),
        ] + extra_out_specs,
        out_shape=[
            jax.ShapeDtypeStruct((n, POOL_DIM), jnp.bfloat16),
            jax.ShapeDtypeStruct((n, NA_DIM), jnp.bfloat16),
            kv_shape,
            kv_shape,
            jax.ShapeDtypeStruct((n, SG_DIM), jnp.bfloat16),
        ] + extra_out_shape,
        scratch_shapes=[pltpu.VMEM((tm + 2 * POOL_HALO, POOL_DIM), jnp.float32)],
        compiler_params=_cparams(("arbitrary",)),
        name="inproj",
    )(*args, n1, sh, sc, w_in, qg, kg, sgn, sgw, sgb, pool_w, pool_scale)


NA_ROWS_PER_BLOCK = 64
NA_GROUP_ROWS = 4
NA_WINDOW_ROWS = NA_GROUP_ROWS + NA_WIN_ROWS
NA_BLOCK = NA_ROWS_PER_BLOCK * GRID_W
NA_GROUP = NA_GROUP_ROWS * GRID_W
NA_HALF_COLS = GRID_W // 2
NA_CHAIN = NA_GROUP_ROWS * NA_HALF_COLS
NA_HALF_STRIPS = 5
NA_HALF_COL0 = (0, GRID_W - NA_HALF_STRIPS * STRIP_W)
NA_RUN = NA_WINDOW_ROWS * STRIP_W
NA_LOCAL = 512
NA_STRIP_BLOCK = NA_ROWS_PER_BLOCK * STRIP_W
NA_STRIP_HALO = (NA_WIN_ROWS // 2) * STRIP_W
NA_EDGE_FIRST, NA_EDGE_NONE, NA_EDGE_LAST = 0, 1, 2


def _stack_heads(x, low):
    zero = jnp.zeros_like(x)
    return jnp.concatenate([jnp.where(low, x, zero), jnp.where(low, zero, x)], axis=0)


def _natten_kernel(q_ref, kp_ref, kc_ref, kn_ref, vp_ref, vc_ref, vn_ref, kx_ref, vx_ref, bias_ref,
                   wg_ref, wu_ref, wd_ref,
                   o_ref, wg_bf_ref, wu_bf_ref, wd_bf_ref, kwin_ref, vwin_ref, vxe_ref, *, grid_rows):
    b = pl.program_id(1)
    wg_bf_ref[...] = wg_ref[...].astype(jnp.bfloat16)
    wu_bf_ref[...] = wu_ref[...].astype(jnp.bfloat16)
    wd_bf_ref[...] = wd_ref[...].astype(jnp.bfloat16)
    top, bottom = NA_STRIP_HALO, NA_STRIP_HALO + NA_STRIP_BLOCK
    kwin_ref[:, 0:top, :] = kp_ref[...]
    kwin_ref[:, top:bottom, :] = kc_ref[...]
    kwin_ref[:, bottom:, :] = kn_ref[...]
    vwin_ref[:, 0:top, 0:LANES] = vp_ref[...]
    vwin_ref[:, top:bottom, 0:LANES] = vc_ref[...]
    vwin_ref[:, bottom:, 0:LANES] = vn_ref[...]
    vwin_ref[:, :, LANES:] = jnp.ones(vwin_ref.shape[:2] + (LANES,), jnp.bfloat16)
    vxe_ref[:, 0:LANES] = vx_ref[...]
    vxe_ref[:, LANES:] = jnp.ones((vxe_ref.shape[0], LANES), jnp.bfloat16)
    low_q = lax.broadcasted_iota(jnp.int32, (NA_CHAIN, LANES), 1) < HEAD_DIM
    n_pad = NA_LOCAL - NA_HALF_STRIPS * NA_RUN

    def window_start(g):
        r0 = b * NA_ROWS_PER_BLOCK + g * NA_GROUP_ROWS
        ws = jnp.clip(r0 - NA_WIN_ROWS // 2, 0, grid_rows - NA_WINDOW_ROWS)
        edge = jnp.where(r0 == 0, NA_EDGE_FIRST,
                         jnp.where(r0 == grid_rows - NA_GROUP_ROWS, NA_EDGE_LAST, NA_EDGE_NONE))
        start = pl.multiple_of((ws - b * NA_ROWS_PER_BLOCK + NA_WIN_ROWS // 2) * STRIP_W, NA_STRIP_HALO)
        return start, edge

    def local_window(win_ref, g, half):
        start, _ = window_start(g)
        s0 = NA_HALF_COL0[half] // STRIP_W
        runs = [win_ref[s, pl.ds(start, NA_RUN), :] for s in range(s0, s0 + NA_HALF_STRIPS)]
        return jnp.concatenate(runs + [jnp.zeros((n_pad, win_ref.shape[2]), jnp.bfloat16)], axis=0)

    def query_rows(g, half, j):
        first = g * NA_GROUP + j * GRID_W + half * NA_HALF_COLS
        return slice(first, first + NA_HALF_COLS)

    def scores(c):
        g, half = divmod(c, 2)
        _, edge = window_start(g)
        qh = jnp.concatenate([q_ref[query_rows(g, half, j), :] for j in range(NA_GROUP_ROWS)], axis=0)
        lhs = _stack_heads(qh, low_q)
        kl = local_window(kwin_ref, g, half)
        return jnp.concatenate([_dot_nt(lhs, kl).astype(jnp.bfloat16) + bias_ref[edge, half],
                                _dot_nt(lhs, kx_ref[...]).astype(jnp.bfloat16)], axis=-1)

    n_chains = 2 * (NA_ROWS_PER_BLOCK // NA_GROUP_ROWS)
    s_next = scores(0)
    for c in range(n_chains):
        s = s_next
        if c + 1 < n_chains:
            s_next = scores(c + 1)
        g, half = divmod(c, 2)
        vl = local_window(vwin_ref, g, half)
        m = jnp.max(s, axis=-1, keepdims=True)
        pb = jnp.exp2(s - m)
        o = _dot(pb[:, :NA_LOCAL], vl) + _dot(pb[:, NA_LOCAL:], vxe_ref[...])
        o = o[:, :LANES] * (1.0 / o[:, LANES:])
        o = jnp.where(low_q, o[:NA_CHAIN], o[NA_CHAIN:]).astype(jnp.bfloat16)
        for j in range(NA_GROUP_ROWS):
            o_ref[query_rows(g, half, j), :] = o[j * NA_HALF_COLS:(j + 1) * NA_HALF_COLS]


def _natten_call(q, k, v, k_ctx, v_ctx, bias, w_gate, w_up, w_down, layer):
    n = q.shape[0]
    grid_rows = n // GRID_W
    assert grid_rows % NA_ROWS_PER_BLOCK == 0 and grid_rows >= 2 * NA_ROWS_PER_BLOCK
    nblk = n // NA_BLOCK
    steps = HEAD_PAIRS * nblk
    depth = w_gate.shape[0]
    up_rows = N_EXPERTS * D_MODEL
    down_rows = N_EXPERTS * D_EXPERT
    assert up_rows % steps == 0 and down_rows % steps == 0
    wg2 = w_gate.reshape(depth * up_rows, D_EXPERT)
    wu2 = w_up.reshape(depth * up_rows, D_EXPERT)
    wd2 = w_down.reshape(depth * down_rows, D_MODEL)
    up_in = pl.BlockSpec((up_rows // steps, D_EXPERT), lambda p, b: (layer * steps + p * nblk + b, 0))
    down_in = pl.BlockSpec((down_rows // steps, D_MODEL), lambda p, b: (layer * steps + p * nblk + b, 0))
    up_out = pl.BlockSpec((up_rows // steps, D_EXPERT), lambda p, b: (p * nblk + b, 0))
    down_out = pl.BlockSpec((down_rows // steps, D_MODEL), lambda p, b: (p * nblk + b, 0))
    n_halo = n // N_STRIPS // NA_STRIP_HALO
    hb = NA_STRIP_BLOCK // NA_STRIP_HALO
    rows = pl.BlockSpec((NA_BLOCK, LANES), lambda p, b: (b, p))
    cur = pl.BlockSpec((N_STRIPS, NA_STRIP_BLOCK, LANES), lambda p, b: (0, b, p))
    prev = pl.BlockSpec((N_STRIPS, NA_STRIP_HALO, LANES), lambda p, b: (0, jnp.maximum(b * hb - 1, 0), p))
    nxt = pl.BlockSpec((N_STRIPS, NA_STRIP_HALO, LANES),
                       lambda p, b: (0, jnp.minimum((b + 1) * hb, n_halo - 1), p))
    ctx = pl.BlockSpec((k_ctx.shape[0], LANES), lambda p, b: (0, p))
    win_rows = NA_STRIP_BLOCK + 2 * NA_STRIP_HALO
    att, wg_bf, wu_bf, wd_bf = pl.pallas_call(
        functools.partial(_natten_kernel, grid_rows=grid_rows),
        grid=(HEAD_PAIRS, nblk),
        in_specs=[rows, prev, cur, nxt, prev, cur, nxt, ctx, ctx,
                  pl.BlockSpec((None, 3, 2, 2 * NA_CHAIN, NA_LOCAL), lambda p, b: (p, 0, 0, 0, 0)),
                  up_in, up_in, down_in],
        out_specs=[rows, up_out, up_out, down_out],
        out_shape=[jax.ShapeDtypeStruct((n, NA_DIM), jnp.bfloat16),
                   jax.ShapeDtypeStruct((up_rows, D_EXPERT), jnp.bfloat16),
                   jax.ShapeDtypeStruct((up_rows, D_EXPERT), jnp.bfloat16),
                   jax.ShapeDtypeStruct((down_rows, D_MODEL), jnp.bfloat16)],
        scratch_shapes=[pltpu.VMEM((N_STRIPS, win_rows, LANES), jnp.bfloat16),
                        pltpu.VMEM((N_STRIPS, win_rows, 2 * LANES), jnp.bfloat16),
                        pltpu.VMEM((k_ctx.shape[0], 2 * LANES), jnp.bfloat16)],
        compiler_params=_cparams(("arbitrary", "arbitrary")),
        name="natten",
    )(q, k, k, k, v, v, v, k_ctx, v_ctx, bias, wg2, wu2, wd2)
    return (att, wg_bf.reshape(N_EXPERTS, D_MODEL, D_EXPERT), wu_bf.reshape(N_EXPERTS, D_MODEL, D_EXPERT),
            wd_bf.reshape(N_EXPERTS, D_EXPERT, D_MODEL))


def _natten_bias(rpb):
    cols = np.arange(GRID_W)
    col_start = np.clip(cols - NA_WIN_COLS // 2, 0, GRID_W - NA_WIN_COLS)
    kc = np.arange(GRID_W)
    in_win = (kc[None, :] >= col_start[:, None]) & (kc[None, :] < col_start[:, None] + NA_WIN_COLS)
    dc = kc[None, :] - cols[:, None] + NA_WIN_COLS - 1
    sel = (np.arange(2 * NA_WIN_COLS - 1)[:, None, None] == dc[None]) & in_win[None]
    t2 = jnp.einsum("hdj,jqk->hdqk", rpb, jnp.asarray(sel, jnp.float32), precision=lax.Precision.HIGHEST)
    t2 = jnp.where(in_win[None, None], t2 * LOG2_E, NEG_BIG)
    neg = jnp.full((NA_HEADS, 1, GRID_W, GRID_W), NEG_BIG, jnp.float32)
    t2e = jnp.concatenate([neg, t2, neg], axis=1)
    u = jnp.concatenate([t2e[:, :-1], t2e[:, 1:]], axis=-1)
    u = u.reshape(HEAD_PAIRS, 2, 2 * NA_WIN_ROWS, GRID_W, LANES)

    place = np.zeros((2, NA_WINDOW_ROWS * GRID_W, NA_LOCAL), np.float32)
    for half, c0 in enumerate(NA_HALF_COL0):
        for a in range(NA_WINDOW_ROWS):
            for kcol in range(c0, c0 + NA_HALF_STRIPS * STRIP_W):
                s, c8 = divmod(kcol - c0, STRIP_W)
                place[half, a * GRID_W + kcol, s * NA_RUN + a * STRIP_W + c8] = 1.0
    outside = np.full((3, NA_GROUP_ROWS, NA_LOCAL), NEG_BIG, np.float32)
    for edge in (NA_EDGE_FIRST, NA_EDGE_NONE, NA_EDGE_LAST):
        for j in range(NA_GROUP_ROWS):
            lo, _ = _window_rows(edge, j)
            for s in range(NA_HALF_STRIPS):
                outside[edge, j, s * NA_RUN + lo * STRIP_W:s * NA_RUN + (lo + NA_WIN_ROWS) * STRIP_W] = 0.0
    return pl.pallas_call(
        _bias_expand_kernel,
        grid=(HEAD_PAIRS,),
        in_specs=[pl.BlockSpec((None, 2, 2 * NA_WIN_ROWS, GRID_W, LANES), lambda p: (p, 0, 0, 0, 0)),
                  pl.BlockSpec(place.shape, lambda p: (0, 0, 0)),
                  pl.BlockSpec(outside.shape, lambda p: (0, 0, 0))],
        out_specs=pl.BlockSpec((None, 3, 2, 2 * NA_CHAIN, NA_LOCAL), lambda p: (p, 0, 0, 0, 0)),
        out_shape=jax.ShapeDtypeStruct((HEAD_PAIRS, 3, 2, 2 * NA_CHAIN, NA_LOCAL), jnp.bfloat16),
        compiler_params=_cparams(("arbitrary",)),
        name="bias_expand",
    )(u, jnp.asarray(place, jnp.bfloat16), jnp.asarray(outside))


def _window_rows(edge, j):
    if edge == NA_EDGE_FIRST:
        return 0, NA_WIN_ROWS - 1 - j
    if edge == NA_EDGE_NONE:
        return j, NA_WIN_ROWS // 2 - 1
    return NA_WINDOW_ROWS - NA_WIN_ROWS, NA_WIN_ROWS // 2 - 1 - j


def _bias_expand_kernel(u_ref, place_ref, outside_ref, o_ref):
    low = lax.broadcasted_iota(jnp.int32, (NA_HALF_COLS, LANES), 1) < GRID_W
    zero = jnp.zeros((NA_HALF_COLS, LANES), jnp.float32)
    for edge in (NA_EDGE_FIRST, NA_EDGE_NONE, NA_EDGE_LAST):
        for half in range(2):
            q0 = half * NA_HALF_COLS
            blocks, masks = [], []
            for hd in range(2):
                for j in range(NA_GROUP_ROWS):
                    lo, base = _window_rows(edge, j)
                    tiles = []
                    for i in range(NA_WINDOW_ROWS // 2):
                        a0, a1 = 2 * i, 2 * i + 1
                        ok0 = lo <= a0 < lo + NA_WIN_ROWS
                        ok1 = lo <= a1 < lo + NA_WIN_ROWS
                        if not (ok0 or ok1):
                            tile = zero
                        else:
                            tile = u_ref[hd, base + a1 - lo, q0:q0 + NA_HALF_COLS, :]
                            if not ok0:
                                tile = jnp.where(low, zero, tile)
                            if not ok1:
                                tile = jnp.where(low, tile, zero)
                        tiles.append(tile)
                    blocks.append(jnp.concatenate(tiles, axis=-1))
                    masks.append(jnp.broadcast_to(outside_ref[edge, j:j + 1, :], (NA_HALF_COLS, NA_LOCAL)))
            lhs = jnp.concatenate(blocks, axis=0).astype(jnp.bfloat16)
            placed = _dot(lhs, place_ref[half]) + jnp.concatenate(masks, axis=0)
            o_ref[edge, half] = placed.astype(o_ref.dtype)


def _ctxatt_kernel(q_ref, k_ref, v_ref, o_ref):
    lc = q_ref.shape[0]
    low = lax.broadcasted_iota(jnp.int32, (lc, LANES), 1) < HEAD_DIM
    lhs = _stack_heads(q_ref[...], low)
    s = _dot_nt(lhs, k_ref[...])
    m = jnp.max(s, axis=-1, keepdims=True)
    p = jnp.exp2(s - m)
    denom = jnp.sum(p, axis=-1, keepdims=True)
    o = _dot(p.astype(jnp.bfloat16), v_ref[...]) * (1.0 / denom)
    o_ref[...] = jnp.where(low, o[:lc], o[lc:]).astype(jnp.bfloat16)


def _ctxatt_call(q, k, v):
    lc = q.shape[0]
    spec = pl.BlockSpec((lc, LANES), lambda p: (0, p))
    return pl.pallas_call(
        _ctxatt_kernel,
        grid=(HEAD_PAIRS,),
        in_specs=[spec, spec, spec],
        out_specs=spec,
        out_shape=jax.ShapeDtypeStruct((lc, NA_DIM), jnp.bfloat16),
        compiler_params=_cparams(("arbitrary",)),
        name="ctxatt",
    )(q, k, v)


OUTPROJ_CHAIN = 256
OUTPROJ_TILE = 1024
INPROJ_TILE = 1024
RESIDUAL_TILE = 2048


def _outproj_kernel(h_ref, mp_ref, att_ref, sg_ref, wo_ref, g1_ref, n2_ref, sh_ref, sc_ref,
                    wr_ref, br_ref, h1_ref, pay_ref, route_ref):
    chains = [slice(c * OUTPROJ_CHAIN, (c + 1) * OUTPROJ_CHAIN) for c in range(h_ref.shape[0] // OUTPROJ_CHAIN)]
    h1s = []
    for rows in chains:
        mix = (_dot(mp_ref[rows, :], wo_ref[0:POOL_DIM, :])
               + _dot(att_ref[rows, :], wo_ref[POOL_DIM:POOL_DIM + NA_DIM, :])
               + _dot(sg_ref[rows, :], wo_ref[POOL_DIM + NA_DIM:, :]))
        h1 = h_ref[rows, :].astype(jnp.float32) + g1_ref[...] * mix
        h1_ref[rows, :] = h1.astype(h1_ref.dtype)
        h1s.append(h1)
    for rows, h1 in zip(chains, h1s):
        _outproj_route(rows, h1, n2_ref, sh_ref, sc_ref, wr_ref, br_ref, pay_ref, route_ref)


def _outproj_route(rows, h1, n2_ref, sh_ref, sc_ref, wr_ref, br_ref, pay_ref, route_ref):
    tm = OUTPROJ_CHAIN
    ms = jnp.mean(h1 * h1, axis=-1, keepdims=True)
    hm = (h1 * lax.rsqrt(ms + EPS)) * (n2_ref[...] * (1.0 + sc_ref[...])) + sh_ref[...]
    pay_ref[rows, 0:HALF_D] = _pack_halves(hm)

    hm_hi = hm.astype(jnp.bfloat16)
    lt = _dot_nt(wr_ref[...], hm_hi)
    logits = lt[:N_EXPERTS] + lt[N_EXPERTS:] + br_ref[...]
    e = jnp.exp(logits - jnp.max(logits, axis=0, keepdims=True))

    best = ga = gb = e1 = e2 = cls = None
    for c in range(N_CLASSES):
        a, b2 = int(CLASS_E1[c]), int(CLASS_E2[c])
        ea, eb = e[a:a + 1, :], e[b2:b2 + 1, :]
        s = ea + eb
        if best is None:
            best, ga, gb = s, ea, eb
            e1 = jnp.full_like(s, float(a))
            e2 = jnp.full_like(s, float(b2))
            cls = jnp.zeros_like(s)
        else:
            better = s > best
            best = jnp.where(better, s, best)
            ga = jnp.where(better, ea, ga)
            gb = jnp.where(better, eb, gb)
            e1 = jnp.where(better, float(a), e1)
            e2 = jnp.where(better, float(b2), e2)
            cls = jnp.where(better, float(c), cls)
    inv = 1.0 / best
    row = lax.broadcasted_iota(jnp.int32, (ROUTE_ROWS, tm), 0)
    rec = jnp.where(row == 0, ga * inv,
          jnp.where(row == 1, gb * inv,
          jnp.where(row == 2, e1,
          jnp.where(row == 3, e2,
          jnp.where(row == 4, cls, 0.0)))))
    route_ref[:, rows] = rec
    wide = jnp.concatenate([rec, jnp.zeros((LANES - ROUTE_ROWS, tm), jnp.float32)], axis=0)
    pay_ref[rows, HALF_D:] = pltpu.bitcast(wide.T, jnp.uint32)


def _outproj_call(h, mp, att, sg, w_out, layer, g1, n2, sh2, sc2, wr, br, tm):
    n = h.shape[0]
    row = lambda i: (i, 0)
    fixed = lambda i: (0, 0)
    vec = pl.BlockSpec((1, D_MODEL), fixed)
    return pl.pallas_call(
        _outproj_kernel,
        grid=(n // tm,),
        in_specs=[
            pl.BlockSpec((tm, D_MODEL), row),
            pl.BlockSpec((tm, POOL_DIM), row),
            pl.BlockSpec((tm, NA_DIM), row),
            pl.BlockSpec((tm, SG_DIM), row),
            pl.BlockSpec((None, D_MODEL, D_MODEL), lambda i: (layer, 0, 0)),
            vec, vec, vec, vec,
            pl.BlockSpec((2 * N_EXPERTS, D_MODEL), fixed),
            pl.BlockSpec((N_EXPERTS, 1), fixed),
        ],
        out_specs=[
            pl.BlockSpec((tm, D_MODEL), row),
            pl.BlockSpec((tm, PAYLOAD_W), row),
            pl.BlockSpec((ROUTE_ROWS, tm), lambda i: (0, i)),
        ],
        out_shape=[
            jax.ShapeDtypeStruct((n, D_MODEL), STREAM_DTYPE),
            jax.ShapeDtypeStruct((n, PAYLOAD_W), jnp.uint32),
            jax.ShapeDtypeStruct((ROUTE_ROWS, n), jnp.float32),
        ],
        compiler_params=_cparams(("arbitrary",)),
        name="outproj",
    )(h, mp, att, sg, w_out, g1, n2, sh2, sc2, wr, br)


def _payload_parts(pay_ref, rows=slice(None)):
    lo, hi = _unpack_halves(pay_ref[rows, 0:HALF_D])
    x = jnp.concatenate([lo, hi], axis=-1).astype(jnp.bfloat16)
    return x, pltpu.bitcast(pay_ref[rows, HALF_D:], jnp.float32)


def _expert_pair(x, ga, gb, wga, wua, wda, wgb, wub, wdb):
    ha = (_silu(_dot(x, wga)) * _dot(x, wua) * ga).astype(jnp.bfloat16)
    hb = (_silu(_dot(x, wgb)) * _dot(x, wub) * gb).astype(jnp.bfloat16)
    return _dot(ha, wda) + _dot(hb, wdb)


MOE_QUARTER = MOE_TM // 4


def _moe_sorted_kernel(blk_ref, e1_ref, e2_ref, quarters_ref, pay_ref,
                       wga_ref, wua_ref, wda_ref, wgb_ref, wub_ref, wdb_ref, o_ref):
    quarters = quarters_ref[pl.program_id(0)]

    def run(rows):
        x, route = _payload_parts(pay_ref, rows)
        y = _expert_pair(x, route[:, 0:1], route[:, 1:2], wga_ref[...], wua_ref[...], wda_ref[...],
                         wgb_ref[...], wub_ref[...], wdb_ref[...])
        o_ref[rows, :] = _pack_halves(y)

    for used in range(1, MOE_TM // MOE_QUARTER + 1):
        @pl.when(quarters == used)
        def _(used=used):
            run(slice(0, used * MOE_QUARTER))


def _moe_sorted_call(blk, e1, e2, quarters, pay_sorted, wg, wu, wd):
    n_tiles = blk.shape[0]
    rows = lambda i, blk, e1, e2, quarters: (blk[i], 0)
    wa = lambda i, blk, e1, e2, quarters: (e1[i], 0, 0)
    wb = lambda i, blk, e1, e2, quarters: (e2[i], 0, 0)
    up = lambda m: pl.BlockSpec((None, D_MODEL, D_EXPERT), m)
    down = lambda m: pl.BlockSpec((None, D_EXPERT, D_MODEL), m)
    return pl.pallas_call(
        _moe_sorted_kernel,
        grid_spec=pltpu.PrefetchScalarGridSpec(
            num_scalar_prefetch=4,
            grid=(n_tiles,),
            in_specs=[pl.BlockSpec((MOE_TM, PAYLOAD_W), rows),
                      up(wa), up(wa), down(wa), up(wb), up(wb), down(wb)],
            out_specs=pl.BlockSpec((MOE_TM, HALF_D), rows),
        ),
        out_shape=jax.ShapeDtypeStruct((n_tiles * MOE_TM, HALF_D), jnp.uint32),
        compiler_params=_cparams(("arbitrary",)),
        name="moe_sorted",
    )(blk, e1, e2, quarters, pay_sorted, wg, wu, wd, wg, wu, wd)


DENSE_EXPERTS_PER_STEP = GROUP_SIZE


def _moe_dense_kernel(pay_ref, wg_ref, wu_ref, wd_ref, o_ref):
    step = pl.program_id(0)

    @pl.when(step == 0)
    def _():
        o_ref[...] = jnp.zeros_like(o_ref)

    x, route = _payload_parts(pay_ref)
    y = None
    for k in range(DENSE_EXPERTS_PER_STEP):
        ef = (step * DENSE_EXPERTS_PER_STEP + k).astype(jnp.float32)
        gate = (jnp.where(route[:, 2:3] == ef, route[:, 0:1], 0.0)
                + jnp.where(route[:, 3:4] == ef, route[:, 1:2], 0.0))
        he = (_silu(_dot(x, wg_ref[k])) * _dot(x, wu_ref[k]) * gate).astype(jnp.bfloat16)
        yk = _dot(he, wd_ref[k])
        y = yk if y is None else y + yk
    o_ref[...] += y


def _moe_dense_call(pay, wg, wu, wd):
    n = pay.shape[0]
    per = DENSE_EXPERTS_PER_STEP
    return pl.pallas_call(
        _moe_dense_kernel,
        grid=(N_EXPERTS // per,),
        in_specs=[pl.BlockSpec((n, PAYLOAD_W), lambda e: (0, 0)),
                  pl.BlockSpec((per, D_MODEL, D_EXPERT), lambda e: (e, 0, 0)),
                  pl.BlockSpec((per, D_MODEL, D_EXPERT), lambda e: (e, 0, 0)),
                  pl.BlockSpec((per, D_EXPERT, D_MODEL), lambda e: (e, 0, 0))],
        out_specs=pl.BlockSpec((n, D_MODEL), lambda e: (0, 0)),
        out_shape=jax.ShapeDtypeStruct((n, D_MODEL), jnp.float32),
        compiler_params=_cparams(("arbitrary",)),
        name="moe_dense",
    )(pay, wg, wu, wd)


def _moe_out(y_ref):
    if y_ref.dtype == jnp.uint32:
        return jnp.concatenate(_unpack_halves(y_ref[...]), axis=-1)
    return y_ref[...]


def _residual_kernel(h_ref, y_ref, g_ref, o_ref):
    o_ref[...] = h_ref[...].astype(jnp.float32) + g_ref[...] * _moe_out(y_ref)


def _residual_call(h, y, g, tm):
    n = h.shape[0]
    row = pl.BlockSpec((tm, D_MODEL), lambda i: (i, 0))
    return pl.pallas_call(
        _residual_kernel,
        grid=(n // tm,),
        in_specs=[row, pl.BlockSpec((tm, y.shape[1]), lambda i: (i, 0)),
                  pl.BlockSpec((1, D_MODEL), lambda i: (0, 0))],
        out_specs=row,
        out_shape=jax.ShapeDtypeStruct((n, D_MODEL), jnp.float32),
        compiler_params=_cparams(("arbitrary",)),
        name="residual",
    )(h, y, g)


SC_ROWS = 128


SC_CORES = 2
SC_SUBCORES = 16
SC_WORKERS = SC_CORES * SC_SUBCORES


def _sc_mesh():
    return plsc.VectorSubcoreMesh(core_axis_name="core", subcore_axis_name="subcore")


def _sc_worker():
    return lax.axis_index("subcore") * SC_CORES + lax.axis_index("core")


def _scatter_rows(x, dest, n_out):
    n, w = x.shape

    per_worker = n // SC_WORKERS
    assert per_worker % SC_ROWS == 0

    @functools.partial(pl.kernel, out_type=jax.ShapeDtypeStruct((n_out, w), x.dtype), mesh=_sc_mesh(),
                       scratch_types=[pltpu.VMEM((SC_ROWS,), jnp.int32), pltpu.VMEM((SC_ROWS, w), x.dtype)])
    def scatter(x_hbm, i_hbm, o_hbm, idx_v, rows_v):
        first = _sc_worker() * per_worker

        @pl.loop(0, per_worker // SC_ROWS)
        def _(i):
            base = pl.multiple_of(first + i * SC_ROWS, SC_ROWS)
            pltpu.sync_copy(i_hbm.at[pl.ds(base, SC_ROWS)], idx_v)
            pltpu.sync_copy(x_hbm.at[pl.ds(base, SC_ROWS)], rows_v)
            pltpu.sync_copy(rows_v, o_hbm.at[idx_v])

    return scatter(x, dest)


def _gather_rows(x, src):
    n = src.shape[0]
    w = x.shape[1]

    per_worker = n // SC_WORKERS
    assert per_worker % SC_ROWS == 0

    @functools.partial(pl.kernel, out_type=jax.ShapeDtypeStruct((n, w), x.dtype), mesh=_sc_mesh(),
                       scratch_types=[pltpu.VMEM((SC_ROWS,), jnp.int32), pltpu.VMEM((SC_ROWS, w), x.dtype)])
    def gather(x_hbm, i_hbm, o_hbm, idx_v, rows_v):
        first = _sc_worker() * per_worker

        @pl.loop(0, per_worker // SC_ROWS)
        def _(i):
            base = pl.multiple_of(first + i * SC_ROWS, SC_ROWS)
            pltpu.sync_copy(i_hbm.at[pl.ds(base, SC_ROWS)], idx_v)
            pltpu.sync_copy(x_hbm.at[idx_v], rows_v)
            pltpu.sync_copy(rows_v, o_hbm.at[pl.ds(base, SC_ROWS)])

    return gather(x, src)


def _routing_plan(cls, n_tiles):
    onehot = (cls[:, None] == jnp.arange(N_CLASSES, dtype=jnp.int32)[None, :]).astype(jnp.int32)
    counts = jnp.sum(onehot, axis=0)
    rank = jnp.sum((jnp.cumsum(onehot, axis=0) - onehot) * onehot, axis=1)
    tiles = (counts + MOE_TM - 1) // MOE_TM
    tile_end = jnp.cumsum(tiles)
    tile_start = tile_end - tiles
    dest = jnp.sum(onehot * tile_start[None, :], axis=1) * MOE_TM + rank
    nact = tile_end[-1]
    tile = jnp.arange(n_tiles, dtype=jnp.int32)
    blk = jnp.minimum(tile, nact - 1)
    tile_cls = jnp.sum((blk[:, None] >= tile_end[None, :]).astype(jnp.int32), axis=1)
    e1 = jnp.asarray(CLASS_E1)[tile_cls]
    e2 = jnp.asarray(CLASS_E2)[tile_cls]
    valid = counts[tile_cls] - (blk - tile_start[tile_cls]) * MOE_TM
    quarters = jnp.where(tile >= nact, 0, (jnp.minimum(valid, MOE_TM) + MOE_QUARTER - 1) // MOE_QUARTER)
    return dest.astype(jnp.int32), blk, e1, e2, quarters.astype(jnp.int32)


def _row_tile(n, prefer=512):
    return next(t for t in (prefer, 512, 256) if n % t == 0)


def kernel(x, c, ctx, c_ctx, w_ada, b_ada, norm1, w_in, pool_w, pool_scale, q_norm, k_norm, rpb,
           sg_w, sg_b, sg_norm, w_out, norm2, w_router, b_router, w_gate, w_up, w_down):
    depth = w_ada.shape[0]
    n = x.shape[1]
    lc = ctx.shape[1]
    bf = jnp.bfloat16
    lat_stream = (x[0],)
    h_ctx = ctx[0]

    cond = jnp.zeros((SUBLANES, D_MODEL), jnp.float32).at[0].set(c[0]).at[1].set(c_ctx)
    mod = _ada_call(cond, w_ada, b_ada)

    wr_t = w_router.T
    wr_hi = wr_t.astype(bf)
    wr_lo = (wr_t - wr_hi.astype(jnp.float32)).astype(bf)
    wr = jnp.concatenate([wr_hi, wr_lo], axis=0)
    br = b_router.reshape(N_EXPERTS, 1)
    n_tiles = n // MOE_TM + N_CLASSES
    w_in_bf = w_in.astype(bf)
    w_out_bf = w_out.astype(bf)

    for l in range(depth):
        last = l == depth - 1
        qg =(q_norm[l] * (HEAD_DIM ** -0.5 * LOG2_E)).reshape(1, NA_DIM)
        kg = k_norm[l].reshape(1, NA_DIM)
        sgn = sg_norm[l].reshape(1, SG_DIM)
        sgw = sg_w[l].astype(bf).reshape(SG_DIM // LANES, 2 * SG_CHUNK, SG_CHUNK)
        sgb = jnp.broadcast_to(sg_b[l].reshape(SG_DIM // LANES, 2 * SG_CHUNK, 1),
                               (SG_DIM // LANES, 2 * SG_CHUNK, LANES))
        pool_bd = jax.scipy.linalg.block_diag(*[pool_w[l, g] for g in range(len(POOL_WINDOWS))]).astype(bf)
        pscale = pool_scale[l].reshape(1, POOL_DIM)
        bias = _natten_bias(rpb[l])
        n1 = norm1[l].reshape(1, D_MODEL)
        n2 = norm2[l].reshape(1, D_MODEL)

        def mods(row):
            return [mod[l, row:row + 1, i * D_MODEL:(i + 1) * D_MODEL] for i in range(6)]

        sh1, sc1, g1, sh2, sc2, g2 = mods(0)
        csh1, csc1, cg1, csh2, csc2, cg2 = mods(1)

        tc = _row_tile(lc)
        mix_pool_c, q_c, k_c, v_c, sg_c = _inproj_call((h_ctx,), n1, csh1, csc1, w_in_bf, l, qg, kg,
                                                       sgn, sgw, sgb, pool_bd, pscale, tc, strips=False)

        outs = _inproj_call(lat_stream, n1, sh1, sc1, w_in_bf, l, qg, kg, sgn, sgw, sgb,
                            pool_bd, pscale, _row_tile(n, INPROJ_TILE), strips=True)
        mix_pool, q, k, v, sg = outs[:5]
        h_lat = outs[5] if len(lat_stream) == 3 else lat_stream[0]
        att, wg_l, wu_l, wd_l = _natten_call(q, k, v, k_c, v_c, bias, w_gate, w_up, w_down, l)
        h1, pay, route = _outproj_call(h_lat, mix_pool, att, sg, w_out_bf, l, g1, n2, sh2, sc2, wr, br,
                                       _row_tile(n, OUTPROJ_TILE))
        cls = route[4].astype(jnp.int32)
        dest, blk, e1, e2, quarters = _routing_plan(cls, n_tiles)
        pay_sorted = _scatter_rows(pay, dest, n_tiles * MOE_TM)
        y_sorted = _moe_sorted_call(blk, e1, e2, quarters, pay_sorted, wg_l, wu_l, wd_l)
        y = _gather_rows(y_sorted, dest)
        lat_stream = (h1, y, g2)

        if not last:
            att_c = _ctxatt_call(q_c, k_c, v_c)
            h1_c, pay_c, _ = _outproj_call(h_ctx, mix_pool_c, att_c, sg_c, w_out_bf, l, cg1, n2, csh2, csc2,
                                           wr, br, tc)
            y_c = _moe_dense_call(pay_c, wg_l, wu_l, wd_l)
            h_ctx = _residual_call(h1_c, y_c, cg2, tc)

    return _residual_call(*lat_stream, _row_tile(n, RESIDUAL_TILE))[None]
```

```python
import functools

import jax
import jax.numpy as jnp
import numpy as np
from jax import lax
from jax.experimental import pallas as pl
from jax.experimental.pallas import tpu as pltpu
from jax.experimental.pallas import tpu_sc as plsc

D_MODEL = 1024
GRID_W = 64
HEAD_DIM = 64
POOL_WINDOWS = (2, 4, 8, 16)
POOL_DIM = 256
NA_HEADS = 8
NA_DIM = 512
NA_WIN_ROWS = 8
NA_WIN_COLS = 16
SG_DIM = 256
SG_CHUNK = 128
Q_OFF = POOL_DIM
K_OFF = Q_OFF + NA_DIM
V_OFF = K_OFF + NA_DIM
U_OFF = V_OFF + NA_DIM
G_OFF = U_OFF + SG_DIM
IN_DIM = G_OFF + SG_DIM
N_EXPERTS = 16
GROUP_SIZE = 4
D_EXPERT = 512
EPS = 1e-6

LANES = 128
SUBLANES = 8
HEAD_PAIRS = NA_DIM // LANES
VMEM_LIMIT = 48 * 1024 * 1024

PAIRS = ((0, 1), (0, 2), (1, 2), (1, 3), (0, 3), (2, 3))
N_CLASSES = (N_EXPERTS // GROUP_SIZE) * len(PAIRS)
CLASS_E1 = np.array([4 * g + i for g in range(4) for (i, j) in PAIRS], np.int32)
CLASS_E2 = np.array([4 * g + j for g in range(4) for (i, j) in PAIRS], np.int32)

ROUTE_ROWS = 8
HALF_D = D_MODEL // 2
PAYLOAD_W = HALF_D + LANES
MOE_TM = 512
NEG_BIG = -1e30
LOG2_E = 1.4426950408889634


def _cparams(sem):
    return pltpu.CompilerParams(dimension_semantics=sem, vmem_limit_bytes=VMEM_LIMIT)


def _dot(a, b):
    return jnp.dot(a, b, preferred_element_type=jnp.float32)


def _dot_nt(a, b):
    return lax.dot_general(a, b, (((1,), (1,)), ((), ())), preferred_element_type=jnp.float32)


def _gelu_tanh(x):
    return 0.5 * x * (1.0 + jnp.tanh(0.7978845608028654 * (x + 0.044715 * (x * x * x))))


def _silu(x):
    return x * (1.0 / (1.0 + jnp.exp(-x)))


def _pack_halves(x):
    w = x.shape[1] // 2
    lo = pltpu.bitcast(x[:, :w].astype(jnp.bfloat16).astype(jnp.float32), jnp.uint32) >> 16
    hi = pltpu.bitcast(x[:, w:].astype(jnp.bfloat16).astype(jnp.float32), jnp.uint32) & jnp.uint32(0xFFFF0000)
    return lo | hi


def _unpack_halves(words):
    lo = pltpu.bitcast(words << 16, jnp.float32)
    hi = pltpu.bitcast(words & jnp.uint32(0xFFFF0000), jnp.float32)
    return lo, hi


def _ada_kernel(cond_ref, w_ref, b_ref, o_ref):
    cond = _silu(cond_ref[...])
    o_ref[...] = jnp.dot(cond, w_ref[...], preferred_element_type=jnp.float32,
                         precision=lax.Precision.HIGHEST) + b_ref[...]


def _ada_call(cond, w_ada, b_ada):
    depth = w_ada.shape[0]
    tn = 1536
    return pl.pallas_call(
        _ada_kernel,
        grid=(depth, 6 * D_MODEL // tn),
        in_specs=[
            pl.BlockSpec((SUBLANES, D_MODEL), lambda l, j: (0, 0)),
            pl.BlockSpec((None, D_MODEL, tn), lambda l, j: (l, 0, j)),
            pl.BlockSpec((None, 1, tn), lambda l, j: (l, 0, j)),
        ],
        out_specs=pl.BlockSpec((None, SUBLANES, tn), lambda l, j: (l, 0, j)),
        out_shape=jax.ShapeDtypeStruct((depth, SUBLANES, 6 * D_MODEL), jnp.float32),
        compiler_params=_cparams(("arbitrary", "arbitrary")),
        name="adaln",
    )(cond, w_ada, b_ada.reshape(depth, 1, 6 * D_MODEL))


def _norm_modulate(x, n_ref, sh_ref, sc_ref):
    ms = jnp.mean(x * x, axis=-1, keepdims=True)
    return ((x * lax.rsqrt(ms + EPS)) * (n_ref[...] * (1.0 + sc_ref[...])) + sh_ref[...]).astype(jnp.bfloat16)


def _head_rms_scale(a):
    low = lax.broadcasted_iota(jnp.int32, (a.shape[0], LANES), 1) < HEAD_DIM
    blocks = []
    for p in range(a.shape[1] // LANES):
        sq = jnp.square(a[:, p * LANES:(p + 1) * LANES])
        s_lo = jnp.sum(jnp.where(low, sq, 0.0), axis=-1, keepdims=True)
        s_hi = jnp.sum(jnp.where(low, 0.0, sq), axis=-1, keepdims=True)
        blocks.append(jnp.where(low, lax.rsqrt(s_lo * (1.0 / HEAD_DIM) + EPS),
                                lax.rsqrt(s_hi * (1.0 / HEAD_DIM) + EPS)))
    return jnp.concatenate(blocks, axis=-1)


STRIP_W = 8
N_STRIPS = GRID_W // STRIP_W


def _store_keys(ref, x):
    if len(ref.shape) == 2:
        ref[...] = x.astype(jnp.bfloat16)
        return
    pair = 2 * STRIP_W
    for s in range(N_STRIPS):
        for rp in range(x.shape[0] // (2 * GRID_W)):
            top = 2 * rp * GRID_W + s * STRIP_W
            rows = jnp.concatenate([x[top:top + STRIP_W], x[top + GRID_W:top + GRID_W + STRIP_W]], axis=0)
            ref[s, rp * pair:(rp + 1) * pair, :] = rows.astype(jnp.bfloat16)


POOL_HALO = 8
HALO_BLOCK = 16
STREAM_DTYPE = jnp.bfloat16


POOL_EDGE = 16


def _pool_mix(xe_ref, w_ref, scale_ref, tm, seq_len):
    low = lax.broadcasted_iota(jnp.int32, (tm, LANES), 1) < HEAD_DIM
    t_edge = pl.program_id(0) * tm + lax.broadcasted_iota(jnp.int32, (POOL_EDGE, LANES), 0)

    def window_mean(s, half):
        mean = s * (1.0 / (2 * half))

        def clip_fix(t):
            count = (jnp.minimum(t + half, seq_len) - jnp.maximum(t - half, 0)).astype(jnp.float32)
            return (2.0 * half) / count

        return jnp.concatenate([mean[:POOL_EDGE] * clip_fix(t_edge),
                                mean[POOL_EDGE:tm - POOL_EDGE],
                                mean[tm - POOL_EDGE:] * clip_fix(t_edge + (tm - POOL_EDGE))], axis=0)

    def window_sums(xs, n_levels):
        sums = []
        s = xs
        for k in range(n_levels):
            step = 1 << k
            s = s[:-step] + s[step:]
            sums.append(s)
        return sums

    outs = []
    for half_block, windows in enumerate(((2, 4), (8, 16))):
        xs = xe_ref[:, half_block * LANES:(half_block + 1) * LANES]
        sums = window_sums(xs, int(np.log2(windows[1])))
        parts = []
        for w in windows:
            half = w // 2
            s = sums[int(np.log2(w)) - 1][POOL_HALO - half:POOL_HALO - half + tm]
            parts.append(window_mean(s, half))
        mean = jnp.where(low, parts[0], parts[1])
        outs.append(mean - xs[POOL_HALO:POOL_HALO + tm])
    d = jnp.concatenate(outs, axis=-1).astype(jnp.bfloat16)
    return (_dot(d, w_ref[...]) * scale_ref[...]).astype(jnp.bfloat16)


def _inproj_kernel(*refs, pending, seq_len):
    n_stream = 7 if pending else 3
    stream, refs = refs[:n_stream], refs[n_stream:]
    if pending:
        h_ref, y_ref, g_ref, hp_ref, hn_ref, yp_ref, yn_ref = stream
        hres_ref, refs = refs[-2], refs[:-2] + refs[-1:]
        x = h_ref[...].astype(jnp.float32) + g_ref[...] * _moe_out(y_ref)
        hres_ref[...] = x.astype(hres_ref.dtype)
        before = hp_ref[...].astype(jnp.float32) + g_ref[...] * _moe_out(yp_ref)
        after = hn_ref[...].astype(jnp.float32) + g_ref[...] * _moe_out(yn_ref)
    else:
        h_ref, hp_ref, hn_ref = stream
        x = h_ref[...].astype(jnp.float32)
        before = hp_ref[...].astype(jnp.float32)
        after = hn_ref[...].astype(jnp.float32)
    x_halo = jnp.concatenate([before[HALO_BLOCK - POOL_HALO:], after[:POOL_HALO]], axis=0)
    (n1_ref, sh_ref, sc_ref, w_ref, qg_ref, kg_ref, sgn_ref, sgw_ref, sgb_ref,
     pw_ref, ps_ref, pool_ref, q_ref, k_ref, v_ref, sg_ref, xe_ref) = refs
    tm = h_ref.shape[0]
    i = pl.program_id(0)
    hn = _norm_modulate(x, n1_ref, sh_ref, sc_ref)
    hn_halo = _norm_modulate(x_halo, n1_ref, sh_ref, sc_ref)

    a_halo = _dot(hn_halo, w_ref[:, 0:Q_OFF])
    xe_ref[0:POOL_HALO, :] = jnp.where(i > 0, a_halo[:POOL_HALO], 0.0)
    xe_ref[POOL_HALO:POOL_HALO + tm, :] = _dot(hn, w_ref[:, 0:Q_OFF])
    xe_ref[POOL_HALO + tm:, :] = jnp.where(i < pl.num_programs(0) - 1, a_halo[POOL_HALO:], 0.0)

    a_g = _dot(hn, w_ref[:, G_OFF:IN_DIM])
    a_u = _dot(hn, w_ref[:, U_OFF:G_OFF])
    a_q = _dot(hn, w_ref[:, Q_OFF:K_OFF])
    a_k = _dot(hn, w_ref[:, K_OFF:V_OFF])
    _store_keys(v_ref, _dot(hn, w_ref[:, V_OFF:U_OFF]))

    gv = _gelu_tanh(a_g)
    q_ref[...] = (a_q * _head_rms_scale(a_q) * qg_ref[...]).astype(jnp.bfloat16)
    _store_keys(k_ref, a_k * _head_rms_scale(a_k) * kg_ref[...])

    u = _gelu_tanh(a_u)
    vn = (gv * _head_rms_scale(gv) * sgn_ref[...]).astype(jnp.bfloat16)
    low = lax.broadcasted_iota(jnp.int32, (SG_CHUNK, LANES), 1) < HEAD_DIM
    for c in range(tm // SG_CHUNK):
        rows = slice(c * SG_CHUNK, (c + 1) * SG_CHUNK)
        for s in range(SG_DIM // LANES):
            cols = slice(s * LANES, (s + 1) * LANES)
            m = _dot(sgw_ref[s], vn[rows, cols]) + sgb_ref[s]
            mixed = jnp.where(low, m[:SG_CHUNK], m[SG_CHUNK:])
            sg_ref[rows, cols] = (u[rows, cols] * mixed).astype(jnp.bfloat16)

    pool_ref[...] = _pool_mix(xe_ref, pw_ref, ps_ref, tm, seq_len)


def _inproj_call(stream, n1, sh, sc, w_in, layer, qg, kg, sgn, sgw, sgb, pool_w, pool_scale, tm, strips):
    pending = len(stream) == 3
    n = stream[0].shape[0]
    if strips:
        kv_spec = pl.BlockSpec((N_STRIPS, tm // N_STRIPS, NA_DIM), lambda i: (0, i, 0))
        kv_shape = jax.ShapeDtypeStruct((N_STRIPS, n // N_STRIPS, NA_DIM), jnp.bfloat16)
    else:
        kv_spec = pl.BlockSpec((tm, NA_DIM), lambda i: (i, 0))
        kv_shape = jax.ShapeDtypeStruct((n, NA_DIM), jnp.bfloat16)
    row = lambda i: (i, 0)
    fixed2 = lambda i: (0, 0)
    fixed3 = lambda i: (0, 0, 0)
    vec = lambda w: pl.BlockSpec((1, w), fixed2)
    rows = pl.BlockSpec((tm, D_MODEL), row)
    per_tile = tm // HALO_BLOCK
    before = lambda i: (jnp.maximum(i * per_tile - 1, 0), 0)
    after = lambda i: (jnp.minimum((i + 1) * per_tile, n // HALO_BLOCK - 1), 0)
    halo = lambda w, m: pl.BlockSpec((HALO_BLOCK, w), m)
    h = stream[0]
    if pending:
        y, g = stream[1], stream[2]
        yw = y.shape[1]
        args = [h, y, g, h, h, y, y]
        stream_specs = [rows, pl.BlockSpec((tm, yw), row), vec(D_MODEL),
                        halo(D_MODEL, before), halo(D_MODEL, after), halo(yw, before), halo(yw, after)]
    else:
        args = [h, h, h]
        stream_specs = [rows, halo(D_MODEL, before), halo(D_MODEL, after)]
    extra_out_specs = [rows] if pending else []
    extra_out_shape = [jax.ShapeDtypeStruct((n, D_MODEL), STREAM_DTYPE)] if pending else []
    return pl.pallas_call(
        functools.partial(_inproj_kernel, pending=pending, seq_len=n),
        grid=(n // tm,),
        in_specs=stream_specs + [
            vec(D_MODEL), vec(D_MODEL), vec(D_MODEL),
            pl.BlockSpec((None, D_MODEL, IN_DIM), lambda i: (layer, 0, 0)),
            vec(NA_DIM), vec(NA_DIM),
            vec(SG_DIM),
            pl.BlockSpec((SG_DIM // LANES, 2 * SG_CHUNK, SG_CHUNK), fixed3),
            pl.BlockSpec((SG_DIM // LANES, 2 * SG_CHUNK, LANES), fixed3),
            pl.BlockSpec((POOL_DIM, POOL_DIM), fixed2),
            vec(POOL_DIM),
        ],
        out_specs=[
            pl.BlockSpec((tm, POOL_DIM), row),
            pl.BlockSpec((tm, NA_DIM), row),
            kv_spec,
            kv_spec,
            pl.BlockSpec((tm, SG_DIM), row),
        ] + extra_out_specs,
        out_shape=[
            jax.ShapeDtypeStruct((n, POOL_DIM), jnp.bfloat16),
            jax.ShapeDtypeStruct((n, NA_DIM), jnp.bfloat16),
            kv_shape,
            kv_shape,
            jax.ShapeDtypeStruct((n, SG_DIM), jnp.bfloat16),
        ] + extra_out_shape,
        scratch_shapes=[pltpu.VMEM((tm + 2 * POOL_HALO, POOL_DIM), jnp.float32)],
        compiler_params=_cparams(("arbitrary",)),
        name="inproj",
    )(*args, n1, sh, sc, w_in, qg, kg, sgn, sgw, sgb, pool_w, pool_scale)


NA_ROWS_PER_BLOCK = 64
NA_GROUP_ROWS = 4
NA_WINDOW_ROWS = NA_GROUP_ROWS + NA_WIN_ROWS
NA_BLOCK = NA_ROWS_PER_BLOCK * GRID_W
NA_GROUP = NA_GROUP_ROWS * GRID_W
NA_HALF_COLS = GRID_W // 2
NA_CHAIN = NA_GROUP_ROWS * NA_HALF_COLS
NA_HALF_STRIPS = 5
NA_HALF_COL0 = (0, GRID_W - NA_HALF_STRIPS * STRIP_W)
NA_RUN = NA_WINDOW_ROWS * STRIP_W
NA_LOCAL = 512
NA_STRIP_BLOCK = NA_ROWS_PER_BLOCK * STRIP_W
NA_STRIP_HALO = (NA_WIN_ROWS // 2) * STRIP_W
NA_EDGE_FIRST, NA_EDGE_NONE, NA_EDGE_LAST = 0, 1, 2


def _stack_heads(x, low):
    zero = jnp.zeros_like(x)
    return jnp.concatenate([jnp.where(low, x, zero), jnp.where(low, zero, x)], axis=0)


def _natten_kernel(q_ref, kp_ref, kc_ref, kn_ref, vp_ref, vc_ref, vn_ref, kx_ref, vx_ref, bias_ref,
                   wg_ref, wu_ref, wd_ref,
                   o_ref, wg_bf_ref, wu_bf_ref, wd_bf_ref, kwin_ref, vwin_ref, vxe_ref, *, grid_rows):
    b = pl.program_id(1)
    wg_bf_ref[...] = wg_ref[...].astype(jnp.bfloat16)
    wu_bf_ref[...] = wu_ref[...].astype(jnp.bfloat16)
    wd_bf_ref[...] = wd_ref[...].astype(jnp.bfloat16)
    top, bottom = NA_STRIP_HALO, NA_STRIP_HALO + NA_STRIP_BLOCK
    kwin_ref[:, 0:top, :] = kp_ref[...]
    kwin_ref[:, top:bottom, :] = kc_ref[...]
    kwin_ref[:, bottom:, :] = kn_ref[...]
    vwin_ref[:, 0:top, 0:LANES] = vp_ref[...]
    vwin_ref[:, top:bottom, 0:LANES] = vc_ref[...]
    vwin_ref[:, bottom:, 0:LANES] = vn_ref[...]
    vwin_ref[:, :, LANES:] = jnp.ones(vwin_ref.shape[:2] + (LANES,), jnp.bfloat16)
    vxe_ref[:, 0:LANES] = vx_ref[...]
    vxe_ref[:, LANES:] = jnp.ones((vxe_ref.shape[0], LANES), jnp.bfloat16)
    low_q = lax.broadcasted_iota(jnp.int32, (NA_CHAIN, LANES), 1) < HEAD_DIM
    n_pad = NA_LOCAL - NA_HALF_STRIPS * NA_RUN

    def window_start(g):
        r0 = b * NA_ROWS_PER_BLOCK + g * NA_GROUP_ROWS
        ws = jnp.clip(r0 - NA_WIN_ROWS // 2, 0, grid_rows - NA_WINDOW_ROWS)
        edge = jnp.where(r0 == 0, NA_EDGE_FIRST,
                         jnp.where(r0 == grid_rows - NA_GROUP_ROWS, NA_EDGE_LAST, NA_EDGE_NONE))
        start = pl.multiple_of((ws - b * NA_ROWS_PER_BLOCK + NA_WIN_ROWS // 2) * STRIP_W, NA_STRIP_HALO)
        return start, edge

    def local_window(win_ref, g, half):
        start, _ = window_start(g)
        s0 = NA_HALF_COL0[half] // STRIP_W
        runs = [win_ref[s, pl.ds(start, NA_RUN), :] for s in range(s0, s0 + NA_HALF_STRIPS)]
        return jnp.concatenate(runs + [jnp.zeros((n_pad, win_ref.shape[2]), jnp.bfloat16)], axis=0)

    def query_rows(g, half, j):
        first = g * NA_GROUP + j * GRID_W + half * NA_HALF_COLS
        return slice(first, first + NA_HALF_COLS)

    def scores(c):
        g, half = divmod(c, 2)
        _, edge = window_start(g)
        qh = jnp.concatenate([q_ref[query_rows(g, half, j), :] for j in range(NA_GROUP_ROWS)], axis=0)
        lhs = _stack_heads(qh, low_q)
        kl = local_window(kwin_ref, g, half)
        return jnp.concatenate([_dot_nt(lhs, kl).astype(jnp.bfloat16) + bias_ref[edge, half],
                                _dot_nt(lhs, kx_ref[...]).astype(jnp.bfloat16)], axis=-1)

    n_chains = 2 * (NA_ROWS_PER_BLOCK // NA_GROUP_ROWS)
    s_next = scores(0)
    for c in range(n_chains):
        s = s_next
        if c + 1 < n_chains:
            s_next = scores(c + 1)
        g, half = divmod(c, 2)
        vl = local_window(vwin_ref, g, half)
        m = jnp.max(s, axis=-1, keepdims=True)
        pb = jnp.exp2(s - m)
        o = _dot(pb[:, :NA_LOCAL], vl) + _dot(pb[:, NA_LOCAL:], vxe_ref[...])
        o = o[:, :LANES] * (1.0 / o[:, LANES:])
        o = jnp.where(low_q, o[:NA_CHAIN], o[NA_CHAIN:]).astype(jnp.bfloat16)
        for j in range(NA_GROUP_ROWS):
            o_ref[query_rows(g, half, j), :] = o[j * NA_HALF_COLS:(j + 1) * NA_HALF_COLS]


def _natten_call(q, k, v, k_ctx, v_ctx, bias, w_gate, w_up, w_down, layer):
    n = q.shape[0]
    grid_rows = n // GRID_W
    assert grid_rows % NA_ROWS_PER_BLOCK == 0 and grid_rows >= 2 * NA_ROWS_PER_BLOCK
    nblk = n // NA_BLOCK
    steps = HEAD_PAIRS * nblk
    depth = w_gate.shape[0]
    up_rows = N_EXPERTS * D_MODEL
    down_rows = N_EXPERTS * D_EXPERT
    assert up_rows % steps == 0 and down_rows % steps == 0
    wg2 = w_gate.reshape(depth * up_rows, D_EXPERT)
    wu2 = w_up.reshape(depth * up_rows, D_EXPERT)
    wd2 = w_down.reshape(depth * down_rows, D_MODEL)
    up_in = pl.BlockSpec((up_rows // steps, D_EXPERT), lambda p, b: (layer * steps + p * nblk + b, 0))
    down_in = pl.BlockSpec((down_rows // steps, D_MODEL), lambda p, b: (layer * steps + p * nblk + b, 0))
    up_out = pl.BlockSpec((up_rows // steps, D_EXPERT), lambda p, b: (p * nblk + b, 0))
    down_out = pl.BlockSpec((down_rows // steps, D_MODEL), lambda p, b: (p * nblk + b, 0))
    n_halo = n // N_STRIPS // NA_STRIP_HALO
    hb = NA_STRIP_BLOCK // NA_STRIP_HALO
    rows = pl.BlockSpec((NA_BLOCK, LANES), lambda p, b: (b, p))
    cur = pl.BlockSpec((N_STRIPS, NA_STRIP_BLOCK, LANES), lambda p, b: (0, b, p))
    prev = pl.BlockSpec((N_STRIPS, NA_STRIP_HALO, LANES), lambda p, b: (0, jnp.maximum(b * hb - 1, 0), p))
    nxt = pl.BlockSpec((N_STRIPS, NA_STRIP_HALO, LANES),
                       lambda p, b: (0, jnp.minimum((b + 1) * hb, n_halo - 1), p))
    ctx = pl.BlockSpec((k_ctx.shape[0], LANES), lambda p, b: (0, p))
    win_rows = NA_STRIP_BLOCK + 2 * NA_STRIP_HALO
    att, wg_bf, wu_bf, wd_bf = pl.pallas_call(
        functools.partial(_natten_kernel, grid_rows=grid_rows),
        grid=(HEAD_PAIRS, nblk),
        in_specs=[rows, prev, cur, nxt, prev, cur, nxt, ctx, ctx,
                  pl.BlockSpec((None, 3, 2, 2 * NA_CHAIN, NA_LOCAL), lambda p, b: (p, 0, 0, 0, 0)),
                  up_in, up_in, down_in],
        out_specs=[rows, up_out, up_out, down_out],
        out_shape=[jax.ShapeDtypeStruct((n, NA_DIM), jnp.bfloat16),
                   jax.ShapeDtypeStruct((up_rows, D_EXPERT), jnp.bfloat16),
                   jax.ShapeDtypeStruct((up_rows, D_EXPERT), jnp.bfloat16),
                   jax.ShapeDtypeStruct((down_rows, D_MODEL), jnp.bfloat16)],
        scratch_shapes=[pltpu.VMEM((N_STRIPS, win_rows, LANES), jnp.bfloat16),
                        pltpu.VMEM((N_STRIPS, win_rows, 2 * LANES), jnp.bfloat16),
                        pltpu.VMEM((k_ctx.shape[0], 2 * LANES), jnp.bfloat16)],
        compiler_params=_cparams(("arbitrary", "arbitrary")),
        name="natten",
    )(q, k, k, k, v, v, v, k_ctx, v_ctx, bias, wg2, wu2, wd2)
    return (att, wg_bf.reshape(N_EXPERTS, D_MODEL, D_EXPERT), wu_bf.reshape(N_EXPERTS, D_MODEL, D_EXPERT),
            wd_bf.reshape(N_EXPERTS, D_EXPERT, D_MODEL))


def _natten_bias(rpb):
    cols = np.arange(GRID_W)
    col_start = np.clip(cols - NA_WIN_COLS // 2, 0, GRID_W - NA_WIN_COLS)
    kc = np.arange(GRID_W)
    in_win = (kc[None, :] >= col_start[:, None]) & (kc[None, :] < col_start[:, None] + NA_WIN_COLS)
    dc = kc[None, :] - cols[:, None] + NA_WIN_COLS - 1
    sel = (np.arange(2 * NA_WIN_COLS - 1)[:, None, None] == dc[None]) & in_win[None]
    t2 = jnp.einsum("hdj,jqk->hdqk", rpb, jnp.asarray(sel, jnp.float32), precision=lax.Precision.HIGHEST)
    t2 = jnp.where(in_win[None, None], t2 * LOG2_E, NEG_BIG)
    neg = jnp.full((NA_HEADS, 1, GRID_W, GRID_W), NEG_BIG, jnp.float32)
    t2e = jnp.concatenate([neg, t2, neg], axis=1)
    u = jnp.concatenate([t2e[:, :-1], t2e[:, 1:]], axis=-1)
    u = u.reshape(HEAD_PAIRS, 2, 2 * NA_WIN_ROWS, GRID_W, LANES)

    place = np.zeros((2, NA_WINDOW_ROWS * GRID_W, NA_LOCAL), np.float32)
    for half, c0 in enumerate(NA_HALF_COL0):
        for a in range(NA_WINDOW_ROWS):
            for kcol in range(c0, c0 + NA_HALF_STRIPS * STRIP_W):
                s, c8 = divmod(kcol - c0, STRIP_W)
                place[half, a * GRID_W + kcol, s * NA_RUN + a * STRIP_W + c8] = 1.0
    outside = np.full((3, NA_GROUP_ROWS, NA_LOCAL), NEG_BIG, np.float32)
    for edge in (NA_EDGE_FIRST, NA_EDGE_NONE, NA_EDGE_LAST):
        for j in range(NA_GROUP_ROWS):
            lo, _ = _window_rows(edge, j)
            for s in range(NA_HALF_STRIPS):
                outside[edge, j, s * NA_RUN + lo * STRIP_W:s * NA_RUN + (lo + NA_WIN_ROWS) * STRIP_W] = 0.0
    return pl.pallas_call(
        _bias_expand_kernel,
        grid=(HEAD_PAIRS,),
        in_specs=[pl.BlockSpec((None, 2, 2 * NA_WIN_ROWS, GRID_W, LANES), lambda p: (p, 0, 0, 0, 0)),
                  pl.BlockSpec(place.shape, lambda p: (0, 0, 0)),
                  pl.BlockSpec(outside.shape, lambda p: (0, 0, 0))],
        out_specs=pl.BlockSpec((None, 3, 2, 2 * NA_CHAIN, NA_LOCAL), lambda p: (p, 0, 0, 0, 0)),
        out_shape=jax.ShapeDtypeStruct((HEAD_PAIRS, 3, 2, 2 * NA_CHAIN, NA_LOCAL), jnp.bfloat16),
        compiler_params=_cparams(("arbitrary",)),
        name="bias_expand",
    )(u, jnp.asarray(place, jnp.bfloat16), jnp.asarray(outside))


def _window_rows(edge, j):
    if edge == NA_EDGE_FIRST:
        return 0, NA_WIN_ROWS - 1 - j
    if edge == NA_EDGE_NONE:
        return j, NA_WIN_ROWS // 2 - 1
    return NA_WINDOW_ROWS - NA_WIN_ROWS, NA_WIN_ROWS // 2 - 1 - j


def _bias_expand_kernel(u_ref, place_ref, outside_ref, o_ref):
    low = lax.broadcasted_iota(jnp.int32, (NA_HALF_COLS, LANES), 1) < GRID_W
    zero = jnp.zeros((NA_HALF_COLS, LANES), jnp.float32)
    for edge in (NA_EDGE_FIRST, NA_EDGE_NONE, NA_EDGE_LAST):
        for half in range(2):
            q0 = half * NA_HALF_COLS
            blocks, masks = [], []
            for hd in range(2):
                for j in range(NA_GROUP_ROWS):
                    lo, base = _window_rows(edge, j)
                    tiles = []
                    for i in range(NA_WINDOW_ROWS // 2):
                        a0, a1 = 2 * i, 2 * i + 1
                        ok0 = lo <= a0 < lo + NA_WIN_ROWS
                        ok1 = lo <= a1 < lo + NA_WIN_ROWS
                        if not (ok0 or ok1):
                            tile = zero
                        else:
                            tile = u_ref[hd, base + a1 - lo, q0:q0 + NA_HALF_COLS, :]
                            if not ok0:
                                tile = jnp.where(low, zero, tile)
                            if not ok1:
                                tile = jnp.where(low, tile, zero)
                        tiles.append(tile)
                    blocks.append(jnp.concatenate(tiles, axis=-1))
                    masks.append(jnp.broadcast_to(outside_ref[edge, j:j + 1, :], (NA_HALF_COLS, NA_LOCAL)))
            lhs = jnp.concatenate(blocks, axis=0).astype(jnp.bfloat16)
            placed = _dot(lhs, place_ref[half]) + jnp.concatenate(masks, axis=0)
            o_ref[edge, half] = placed.astype(o_ref.dtype)


def _ctxatt_kernel(q_ref, k_ref, v_ref, o_ref):
    lc = q_ref.shape[0]
    low = lax.broadcasted_iota(jnp.int32, (lc, LANES), 1) < HEAD_DIM
    lhs = _stack_heads(q_ref[...], low)
    s = _dot_nt(lhs, k_ref[...])
    m = jnp.max(s, axis=-1, keepdims=True)
    p = jnp.exp2(s - m)
    denom = jnp.sum(p, axis=-1, keepdims=True)
    o = _dot(p.astype(jnp.bfloat16), v_ref[...]) * (1.0 / denom)
    o_ref[...] = jnp.where(low, o[:lc], o[lc:]).astype(jnp.bfloat16)


def _ctxatt_call(q, k, v):
    lc = q.shape[0]
    spec = pl.BlockSpec((lc, LANES), lambda p: (0, p))
    return pl.pallas_call(
        _ctxatt_kernel,
        grid=(HEAD_PAIRS,),
        in_specs=[spec, spec, spec],
        out_specs=spec,
        out_shape=jax.ShapeDtypeStruct((lc, NA_DIM), jnp.bfloat16),
        compiler_params=_cparams(("arbitrary",)),
        name="ctxatt",
    )(q, k, v)


OUTPROJ_CHAIN = 256
OUTPROJ_TILE = 1024
INPROJ_TILE = 1024
RESIDUAL_TILE = 2048


def _outproj_kernel(h_ref, mp_ref, att_ref, sg_ref, wo_ref, g1_ref, n2_ref, sh_ref, sc_ref,
                    wr_ref, br_ref, h1_ref, pay_ref, route_ref):
    chains = [slice(c * OUTPROJ_CHAIN, (c + 1) * OUTPROJ_CHAIN) for c in range(h_ref.shape[0] // OUTPROJ_CHAIN)]
    h1s = []
    for rows in chains:
        mix = (_dot(mp_ref[rows, :], wo_ref[0:POOL_DIM, :])
               + _dot(att_ref[rows, :], wo_ref[POOL_DIM:POOL_DIM + NA_DIM, :])
               + _dot(sg_ref[rows, :], wo_ref[POOL_DIM + NA_DIM:, :]))
        h1 = h_ref[rows, :].astype(jnp.float32) + g1_ref[...] * mix
        h1_ref[rows, :] = h1.astype(h1_ref.dtype)
        h1s.append(h1)
    for rows, h1 in zip(chains, h1s):
        _outproj_route(rows, h1, n2_ref, sh_ref, sc_ref, wr_ref, br_ref, pay_ref, route_ref)


def _outproj_route(rows, h1, n2_ref, sh_ref, sc_ref, wr_ref, br_ref, pay_ref, route_ref):
    tm = OUTPROJ_CHAIN
    ms = jnp.mean(h1 * h1, axis=-1, keepdims=True)
    hm = (h1 * lax.rsqrt(ms + EPS)) * (n2_ref[...] * (1.0 + sc_ref[...])) + sh_ref[...]
    pay_ref[rows, 0:HALF_D] = _pack_halves(hm)

    hm_hi = hm.astype(jnp.bfloat16)
    lt = _dot_nt(wr_ref[...], hm_hi)
    logits = lt[:N_EXPERTS] + lt[N_EXPERTS:] + br_ref[...]
    e = jnp.exp(logits - jnp.max(logits, axis=0, keepdims=True))

    best = ga = gb = e1 = e2 = cls = None
    for c in range(N_CLASSES):
        a, b2 = int(CLASS_E1[c]), int(CLASS_E2[c])
        ea, eb = e[a:a + 1, :], e[b2:b2 + 1, :]
        s = ea + eb
        if best is None:
            best, ga, gb = s, ea, eb
            e1 = jnp.full_like(s, float(a))
            e2 = jnp.full_like(s, float(b2))
            cls = jnp.zeros_like(s)
        else:
            better = s > best
            best = jnp.where(better, s, best)
            ga = jnp.where(better, ea, ga)
            gb = jnp.where(better, eb, gb)
            e1 = jnp.where(better, float(a), e1)
            e2 = jnp.where(better, float(b2), e2)
            cls = jnp.where(better, float(c), cls)
    inv = 1.0 / best
    row = lax.broadcasted_iota(jnp.int32, (ROUTE_ROWS, tm), 0)
    rec = jnp.where(row == 0, ga * inv,
          jnp.where(row == 1, gb * inv,
          jnp.where(row == 2, e1,
          jnp.where(row == 3, e2,
          jnp.where(row == 4, cls, 0.0)))))
    route_ref[:, rows] = rec
    wide = jnp.concatenate([rec, jnp.zeros((LANES - ROUTE_ROWS, tm), jnp.float32)], axis=0)
    pay_ref[rows, HALF_D:] = pltpu.bitcast(wide.T, jnp.uint32)


def _outproj_call(h, mp, att, sg, w_out, layer, g1, n2, sh2, sc2, wr, br, tm):
    n = h.shape[0]
    row = lambda i: (i, 0)
    fixed = lambda i: (0, 0)
    vec = pl.BlockSpec((1, D_MODEL), fixed)
    return pl.pallas_call(
        _outproj_kernel,
        grid=(n // tm,),
        in_specs=[
            pl.BlockSpec((tm, D_MODEL), row),
            pl.BlockSpec((tm, POOL_DIM), row),
            pl.BlockSpec((tm, NA_DIM), row),
            pl.BlockSpec((tm, SG_DIM), row),
            pl.BlockSpec((None, D_MODEL, D_MODEL), lambda i: (layer, 0, 0)),
            vec, vec, vec, vec,
            pl.BlockSpec((2 * N_EXPERTS, D_MODEL), fixed),
            pl.BlockSpec((N_EXPERTS, 1), fixed),
        ],
        out_specs=[
            pl.BlockSpec((tm, D_MODEL), row),
            pl.BlockSpec((tm, PAYLOAD_W), row),
            pl.BlockSpec((ROUTE_ROWS, tm), lambda i: (0, i)),
        ],
        out_shape=[
            jax.ShapeDtypeStruct((n, D_MODEL), STREAM_DTYPE),
            jax.ShapeDtypeStruct((n, PAYLOAD_W), jnp.uint32),
            jax.ShapeDtypeStruct((ROUTE_ROWS, n), jnp.float32),
        ],
        compiler_params=_cparams(("arbitrary",)),
        name="outproj",
    )(h, mp, att, sg, w_out, g1, n2, sh2, sc2, wr, br)


def _payload_parts(pay_ref, rows=slice(None)):
    lo, hi = _unpack_halves(pay_ref[rows, 0:HALF_D])
    x = jnp.concatenate([lo, hi], axis=-1).astype(jnp.bfloat16)
    return x, pltpu.bitcast(pay_ref[rows, HALF_D:], jnp.float32)


def _expert_pair(x, ga, gb, wga, wua, wda, wgb, wub, wdb):
    ha = (_silu(_dot(x, wga)) * _dot(x, wua) * ga).astype(jnp.bfloat16)
    hb = (_silu(_dot(x, wgb)) * _dot(x, wub) * gb).astype(jnp.bfloat16)
    return _dot(ha, wda) + _dot(hb, wdb)


MOE_QUARTER = MOE_TM // 4


def _moe_sorted_kernel(blk_ref, e1_ref, e2_ref, quarters_ref, pay_ref,
                       wga_ref, wua_ref, wda_ref, wgb_ref, wub_ref, wdb_ref, o_ref):
    quarters = quarters_ref[pl.program_id(0)]

    def run(rows):
        x, route = _payload_parts(pay_ref, rows)
        y = _expert_pair(x, route[:, 0:1], route[:, 1:2], wga_ref[...], wua_ref[...], wda_ref[...],
                         wgb_ref[...], wub_ref[...], wdb_ref[...])
        o_ref[rows, :] = _pack_halves(y)

    for used in range(1, MOE_TM // MOE_QUARTER + 1):
        @pl.when(quarters == used)
        def _(used=used):
            run(slice(0, used * MOE_QUARTER))


def _moe_sorted_call(blk, e1, e2, quarters, pay_sorted, wg, wu, wd):
    n_tiles = blk.shape[0]
    rows = lambda i, blk, e1, e2, quarters: (blk[i], 0)
    wa = lambda i, blk, e1, e2, quarters: (e1[i], 0, 0)
    wb = lambda i, blk, e1, e2, quarters: (e2[i], 0, 0)
    up = lambda m: pl.BlockSpec((None, D_MODEL, D_EXPERT), m)
    down = lambda m: pl.BlockSpec((None, D_EXPERT, D_MODEL), m)
    return pl.pallas_call(
        _moe_sorted_kernel,
        grid_spec=pltpu.PrefetchScalarGridSpec(
            num_scalar_prefetch=4,
            grid=(n_tiles,),
            in_specs=[pl.BlockSpec((MOE_TM, PAYLOAD_W), rows),
                      up(wa), up(wa), down(wa), up(wb), up(wb), down(wb)],
            out_specs=pl.BlockSpec((MOE_TM, HALF_D), rows),
        ),
        out_shape=jax.ShapeDtypeStruct((n_tiles * MOE_TM, HALF_D), jnp.uint32),
        compiler_params=_cparams(("arbitrary",)),
        name="moe_sorted",
    )(blk, e1, e2, quarters, pay_sorted, wg, wu, wd, wg, wu, wd)


DENSE_EXPERTS_PER_STEP = GROUP_SIZE


def _moe_dense_kernel(*refs, first_step, resume):
    if resume:
        pay_ref, wg_ref, wu_ref, wd_ref, acc_ref, o_ref = refs
    else:
        pay_ref, wg_ref, wu_ref, wd_ref, o_ref = refs
    step = pl.program_id(0)

    @pl.when(step == 0)
    def _():
        o_ref[...] = acc_ref[...] if resume else jnp.zeros_like(o_ref)

    x, route = _payload_parts(pay_ref)
    y = None
    for k in range(DENSE_EXPERTS_PER_STEP):
        ef = ((first_step + step) * DENSE_EXPERTS_PER_STEP + k).astype(jnp.float32)
        gate = (jnp.where(route[:, 2:3] == ef, route[:, 0:1], 0.0)
                + jnp.where(route[:, 3:4] == ef, route[:, 1:2], 0.0))
        he = (_silu(_dot(x, wg_ref[k])) * _dot(x, wu_ref[k]) * gate).astype(jnp.bfloat16)
        yk = _dot(he, wd_ref[k])
        y = yk if y is None else y + yk
    o_ref[...] += y


def _moe_dense_call(pay, wg, wu, wd, first_step, n_steps, acc=None):
    n = pay.shape[0]
    per = DENSE_EXPERTS_PER_STEP
    whole = pl.BlockSpec((n, D_MODEL), lambda e: (0, 0))
    return pl.pallas_call(
        functools.partial(_moe_dense_kernel, first_step=first_step, resume=acc is not None),
        grid=(n_steps,),
        in_specs=[pl.BlockSpec((n, PAYLOAD_W), lambda e: (0, 0)),
                  pl.BlockSpec((per, D_MODEL, D_EXPERT), lambda e: (first_step + e, 0, 0)),
                  pl.BlockSpec((per, D_MODEL, D_EXPERT), lambda e: (first_step + e, 0, 0)),
                  pl.BlockSpec((per, D_EXPERT, D_MODEL), lambda e: (first_step + e, 0, 0))]
                 + ([whole] if acc is not None else []),
        out_specs=whole,
        out_shape=jax.ShapeDtypeStruct((n, D_MODEL), jnp.float32),
        compiler_params=_cparams(("arbitrary",)),
        name="moe_dense",
    )(pay, wg, wu, wd, *([acc] if acc is not None else []))


def _moe_out(y_ref):
    if y_ref.dtype == jnp.uint32:
        return jnp.concatenate(_unpack_halves(y_ref[...]), axis=-1)
    return y_ref[...]


def _residual_kernel(h_ref, y_ref, g_ref, o_ref):
    o_ref[...] = h_ref[...].astype(jnp.float32) + g_ref[...] * _moe_out(y_ref)


def _residual_call(h, y, g, tm):
    n = h.shape[0]
    row = pl.BlockSpec((tm, D_MODEL), lambda i: (i, 0))
    return pl.pallas_call(
        _residual_kernel,
        grid=(n // tm,),
        in_specs=[row, pl.BlockSpec((tm, y.shape[1]), lambda i: (i, 0)),
                  pl.BlockSpec((1, D_MODEL), lambda i: (0, 0))],
        out_specs=row,
        out_shape=jax.ShapeDtypeStruct((n, D_MODEL), jnp.float32),
        compiler_params=_cparams(("arbitrary",)),
        name="residual",
    )(h, y, g)


SC_ROWS = 128


SC_CORES = 2
SC_SUBCORES = 16
SC_WORKERS = SC_CORES * SC_SUBCORES


def _sc_mesh():
    return plsc.VectorSubcoreMesh(core_axis_name="core", subcore_axis_name="subcore")


def _sc_worker():
    return lax.axis_index("subcore") * SC_CORES + lax.axis_index("core")


def _scatter_rows(x, dest, n_out):
    n, w = x.shape

    per_worker = n // SC_WORKERS
    assert per_worker % SC_ROWS == 0

    @functools.partial(pl.kernel, out_type=jax.ShapeDtypeStruct((n_out, w), x.dtype), mesh=_sc_mesh(),
                       scratch_types=[pltpu.VMEM((SC_ROWS,), jnp.int32), pltpu.VMEM((SC_ROWS, w), x.dtype)])
    def scatter(x_hbm, i_hbm, o_hbm, idx_v, rows_v):
        first = _sc_worker() * per_worker

        @pl.loop(0, per_worker // SC_ROWS)
        def _(i):
            base = pl.multiple_of(first + i * SC_ROWS, SC_ROWS)
            pltpu.sync_copy(i_hbm.at[pl.ds(base, SC_ROWS)], idx_v)
            pltpu.sync_copy(x_hbm.at[pl.ds(base, SC_ROWS)], rows_v)
            pltpu.sync_copy(rows_v, o_hbm.at[idx_v])

    return scatter(x, dest)


def _gather_rows(x, src):
    n = src.shape[0]
    w = x.shape[1]

    per_worker = n // SC_WORKERS
    assert per_worker % SC_ROWS == 0

    @functools.partial(pl.kernel, out_type=jax.ShapeDtypeStruct((n, w), x.dtype), mesh=_sc_mesh(),
                       scratch_types=[pltpu.VMEM((SC_ROWS,), jnp.int32), pltpu.VMEM((SC_ROWS, w), x.dtype)])
    def gather(x_hbm, i_hbm, o_hbm, idx_v, rows_v):
        first = _sc_worker() * per_worker

        @pl.loop(0, per_worker // SC_ROWS)
        def _(i):
            base = pl.multiple_of(first + i * SC_ROWS, SC_ROWS)
            pltpu.sync_copy(i_hbm.at[pl.ds(base, SC_ROWS)], idx_v)
            pltpu.sync_copy(x_hbm.at[idx_v], rows_v)
            pltpu.sync_copy(rows_v, o_hbm.at[pl.ds(base, SC_ROWS)])

    return gather(x, src)


def _routing_plan(cls, n_tiles):
    onehot = (cls[:, None] == jnp.arange(N_CLASSES, dtype=jnp.int32)[None, :]).astype(jnp.int32)
    counts = jnp.sum(onehot, axis=0)
    rank = jnp.sum((jnp.cumsum(onehot, axis=0) - onehot) * onehot, axis=1)
    tiles = (counts + MOE_TM - 1) // MOE_TM
    tile_end = jnp.cumsum(tiles)
    tile_start = tile_end - tiles
    dest = jnp.sum(onehot * tile_start[None, :], axis=1) * MOE_TM + rank
    nact = tile_end[-1]
    tile = jnp.arange(n_tiles, dtype=jnp.int32)
    blk = jnp.minimum(tile, nact - 1)
    tile_cls = jnp.sum((blk[:, None] >= tile_end[None, :]).astype(jnp.int32), axis=1)
    e1 = jnp.asarray(CLASS_E1)[tile_cls]
    e2 = jnp.asarray(CLASS_E2)[tile_cls]
    valid = counts[tile_cls] - (blk - tile_start[tile_cls]) * MOE_TM
    quarters = jnp.where(tile >= nact, 0, (jnp.minimum(valid, MOE_TM) + MOE_QUARTER - 1) // MOE_QUARTER)
    return dest.astype(jnp.int32), blk, e1, e2, quarters.astype(jnp.int32)


def _row_tile(n, prefer=512):
    return next(t for t in (prefer, 512, 256) if n % t == 0)


def kernel(x, c, ctx, c_ctx, w_ada, b_ada, norm1, w_in, pool_w, pool_scale, q_norm, k_norm, rpb,
           sg_w, sg_b, sg_norm, w_out, norm2, w_router, b_router, w_gate, w_up, w_down):
    depth = w_ada.shape[0]
    n = x.shape[1]
    lc = ctx.shape[1]
    bf = jnp.bfloat16
    lat_stream = (x[0],)
    h_ctx = ctx[0]

    cond = jnp.zeros((SUBLANES, D_MODEL), jnp.float32).at[0].set(c[0]).at[1].set(c_ctx)
    mod = _ada_call(cond, w_ada, b_ada)

    wr_t = w_router.T
    wr_hi = wr_t.astype(bf)
    wr_lo = (wr_t - wr_hi.astype(jnp.float32)).astype(bf)
    wr = jnp.concatenate([wr_hi, wr_lo], axis=0)
    br = b_router.reshape(N_EXPERTS, 1)
    n_tiles = n // MOE_TM + N_CLASSES
    w_in_bf = w_in.astype(bf)
    w_out_bf = w_out.astype(bf)

    for l in range(depth):
        last = l == depth - 1
        qg =(q_norm[l] * (HEAD_DIM ** -0.5 * LOG2_E)).reshape(1, NA_DIM)
        kg = k_norm[l].reshape(1, NA_DIM)
        sgn = sg_norm[l].reshape(1, SG_DIM)
        sgw = sg_w[l].astype(bf).reshape(SG_DIM // LANES, 2 * SG_CHUNK, SG_CHUNK)
        sgb = jnp.broadcast_to(sg_b[l].reshape(SG_DIM // LANES, 2 * SG_CHUNK, 1),
                               (SG_DIM // LANES, 2 * SG_CHUNK, LANES))
        pool_bd = jax.scipy.linalg.block_diag(*[pool_w[l, g] for g in range(len(POOL_WINDOWS))]).astype(bf)
        pscale = pool_scale[l].reshape(1, POOL_DIM)
        bias = _natten_bias(rpb[l])
        n1 = norm1[l].reshape(1, D_MODEL)
        n2 = norm2[l].reshape(1, D_MODEL)

        def mods(row):
            return [mod[l, row:row + 1, i * D_MODEL:(i + 1) * D_MODEL] for i in range(6)]

        sh1, sc1, g1, sh2, sc2, g2 = mods(0)
        csh1, csc1, cg1, csh2, csc2, cg2 = mods(1)

        tc = _row_tile(lc)
        mix_pool_c, q_c, k_c, v_c, sg_c = _inproj_call((h_ctx,), n1, csh1, csc1, w_in_bf, l, qg, kg,
                                                       sgn, sgw, sgb, pool_bd, pscale, tc, strips=False)

        outs = _inproj_call(lat_stream, n1, sh1, sc1, w_in_bf, l, qg, kg, sgn, sgw, sgb,
                            pool_bd, pscale, _row_tile(n, INPROJ_TILE), strips=True)
        mix_pool, q, k, v, sg = outs[:5]
        h_lat = outs[5] if len(lat_stream) == 3 else lat_stream[0]
        att, wg_l, wu_l, wd_l = _natten_call(q, k, v, k_c, v_c, bias, w_gate, w_up, w_down, l)
        h1, pay, route = _outproj_call(h_lat, mix_pool, att, sg, w_out_bf, l, g1, n2, sh2, sc2, wr, br,
                                       _row_tile(n, OUTPROJ_TILE))
        cls = route[4].astype(jnp.int32)
        dest, blk, e1, e2, quarters = _routing_plan(cls, n_tiles)
        pay_sorted = _scatter_rows(pay, dest, n_tiles * MOE_TM)

        if not last:
            dense_steps = N_EXPERTS // DENSE_EXPERTS_PER_STEP
            att_c = _ctxatt_call(q_c, k_c, v_c)
            h1_c, pay_c, _ = _outproj_call(h_ctx, mix_pool_c, att_c, sg_c, w_out_bf, l, cg1, n2, csh2, csc2,
                                           wr, br, tc)
            y_c = _moe_dense_call(pay_c, wg_l, wu_l, wd_l, 0, dense_steps // 2)
            quarters, y_c = lax.optimization_barrier((quarters, y_c))

        y_sorted = _moe_sorted_call(blk, e1, e2, quarters, pay_sorted, wg_l, wu_l, wd_l)
        y = _gather_rows(y_sorted, dest)
        lat_stream = (h1, y, g2)

        if not last:
            y_c = _moe_dense_call(pay_c, wg_l, wu_l, wd_l, dense_steps // 2, dense_steps - dense_steps // 2,
                                  acc=y_c)
            h_ctx = _residual_call(h1_c, y_c, cg2, tc)

    return _residual_call(*lat_stream, _row_tile(n, RESIDUAL_TILE))[None]
```

```python
import functools

import jax
import jax.numpy as jnp
import numpy as np
from jax import lax
from jax.experimental import pallas as pl
from jax.experimental.pallas import tpu as pltpu
from jax.experimental.pallas import tpu_sc as plsc

D_MODEL = 1024
GRID_W = 64
HEAD_DIM = 64
POOL_WINDOWS = (2, 4, 8, 16)
POOL_DIM = 256
NA_HEADS = 8
NA_DIM = 512
NA_WIN_ROWS = 8
NA_WIN_COLS = 16
SG_DIM = 256
SG_CHUNK = 128
Q_OFF = POOL_DIM
K_OFF = Q_OFF + NA_DIM
V_OFF = K_OFF + NA_DIM
U_OFF = V_OFF + NA_DIM
G_OFF = U_OFF + SG_DIM
IN_DIM = G_OFF + SG_DIM
N_EXPERTS = 16
GROUP_SIZE = 4
D_EXPERT = 512
EPS = 1e-6

LANES = 128
SUBLANES = 8
HEAD_PAIRS = NA_DIM // LANES
VMEM_LIMIT = 48 * 1024 * 1024

PAIRS = ((0, 1), (0, 2), (1, 2), (1, 3), (0, 3), (2, 3))
N_CLASSES = (N_EXPERTS // GROUP_SIZE) * len(PAIRS)
CLASS_E1 = np.array([4 * g + i for g in range(4) for (i, j) in PAIRS], np.int32)
CLASS_E2 = np.array([4 * g + j for g in range(4) for (i, j) in PAIRS], np.int32)

ROUTE_ROWS = 8
HALF_D = D_MODEL // 2
PAYLOAD_W = HALF_D + LANES
MOE_TM = 512
NEG_BIG = -1e30
LOG2_E = 1.4426950408889634


def _cparams(sem):
    return pltpu.CompilerParams(dimension_semantics=sem, vmem_limit_bytes=VMEM_LIMIT)


def _dot(a, b):
    return jnp.dot(a, b, preferred_element_type=jnp.float32)


def _dot_nt(a, b):
    return lax.dot_general(a, b, (((1,), (1,)), ((), ())), preferred_element_type=jnp.float32)


def _gelu_tanh(x):
    return 0.5 * x * (1.0 + jnp.tanh(0.7978845608028654 * (x + 0.044715 * (x * x * x))))


def _silu(x):
    return x * (1.0 / (1.0 + jnp.exp(-x)))


def _pack_halves(x):
    w = x.shape[1] // 2
    lo = pltpu.bitcast(x[:, :w].astype(jnp.bfloat16).astype(jnp.float32), jnp.uint32) >> 16
    hi = pltpu.bitcast(x[:, w:].astype(jnp.bfloat16).astype(jnp.float32), jnp.uint32) & jnp.uint32(0xFFFF0000)
    return lo | hi


def _unpack_halves(words):
    lo = pltpu.bitcast(words << 16, jnp.float32)
    hi = pltpu.bitcast(words & jnp.uint32(0xFFFF0000), jnp.float32)
    return lo, hi


def _ada_kernel(cond_ref, w_ref, b_ref, o_ref):
    cond = _silu(cond_ref[...])
    w = w_ref[...]
    rows = [jnp.sum(w * cond[:, r:r + 1], axis=0, keepdims=True) + b_ref[...] for r in range(2)]
    o_ref[...] = jnp.concatenate(rows + [jnp.zeros((SUBLANES - 2, w.shape[1]), jnp.float32)], axis=0)


def _ada_call(cond, w_ada, b_ada):
    depth = w_ada.shape[0]
    tn = 1536
    return pl.pallas_call(
        _ada_kernel,
        grid=(depth, 6 * D_MODEL // tn),
        in_specs=[
            pl.BlockSpec((D_MODEL, 2), lambda l, j: (0, 0)),
            pl.BlockSpec((None, D_MODEL, tn), lambda l, j: (l, 0, j)),
            pl.BlockSpec((None, 1, tn), lambda l, j: (l, 0, j)),
        ],
        out_specs=pl.BlockSpec((None, SUBLANES, tn), lambda l, j: (l, 0, j)),
        out_shape=jax.ShapeDtypeStruct((depth, SUBLANES, 6 * D_MODEL), jnp.float32),
        compiler_params=_cparams(("arbitrary", "arbitrary")),
        name="adaln",
    )(cond, w_ada, b_ada.reshape(depth, 1, 6 * D_MODEL))


def _norm_modulate(x, n_ref, sh_ref, sc_ref):
    ms = jnp.mean(x * x, axis=-1, keepdims=True)
    return ((x * lax.rsqrt(ms + EPS)) * (n_ref[...] * (1.0 + sc_ref[...])) + sh_ref[...]).astype(jnp.bfloat16)


def _head_rms_scale(a):
    low = lax.broadcasted_iota(jnp.int32, (a.shape[0], LANES), 1) < HEAD_DIM
    blocks = []
    for p in range(a.shape[1] // LANES):
        sq = jnp.square(a[:, p * LANES:(p + 1) * LANES])
        s_lo = jnp.sum(jnp.where(low, sq, 0.0), axis=-1, keepdims=True)
        s_hi = jnp.sum(jnp.where(low, 0.0, sq), axis=-1, keepdims=True)
        blocks.append(jnp.where(low, lax.rsqrt(s_lo * (1.0 / HEAD_DIM) + EPS),
                                lax.rsqrt(s_hi * (1.0 / HEAD_DIM) + EPS)))
    return jnp.concatenate(blocks, axis=-1)


STRIP_W = 8
N_STRIPS = GRID_W // STRIP_W


def _store_keys(ref, x):
    if len(ref.shape) == 2:
        ref[...] = x.astype(jnp.bfloat16)
        return
    pair = 2 * STRIP_W
    for s in range(N_STRIPS):
        for rp in range(x.shape[0] // (2 * GRID_W)):
            top = 2 * rp * GRID_W + s * STRIP_W
            rows = jnp.concatenate([x[top:top + STRIP_W], x[top + GRID_W:top + GRID_W + STRIP_W]], axis=0)
            ref[s, rp * pair:(rp + 1) * pair, :] = rows.astype(jnp.bfloat16)


POOL_HALO = 8
HALO_BLOCK = 16
STREAM_DTYPE = jnp.bfloat16


POOL_EDGE = 16


def _pool_mix(xe_ref, w_ref, scale_ref, tm, seq_len):
    low = lax.broadcasted_iota(jnp.int32, (tm, LANES), 1) < HEAD_DIM
    t_edge = pl.program_id(0) * tm + lax.broadcasted_iota(jnp.int32, (POOL_EDGE, LANES), 0)

    def window_mean(s, half):
        mean = s * (1.0 / (2 * half))

        def clip_fix(t):
            count = (jnp.minimum(t + half, seq_len) - jnp.maximum(t - half, 0)).astype(jnp.float32)
            return (2.0 * half) / count

        return jnp.concatenate([mean[:POOL_EDGE] * clip_fix(t_edge),
                                mean[POOL_EDGE:tm - POOL_EDGE],
                                mean[tm - POOL_EDGE:] * clip_fix(t_edge + (tm - POOL_EDGE))], axis=0)

    def window_sums(xs, n_levels):
        sums = []
        s = xs
        for k in range(n_levels):
            step = 1 << k
            s = s[:-step] + s[step:]
            sums.append(s)
        return sums

    outs = []
    for half_block, windows in enumerate(((2, 4), (8, 16))):
        xs = xe_ref[:, half_block * LANES:(half_block + 1) * LANES]
        sums = window_sums(xs, int(np.log2(windows[1])))
        parts = []
        for w in windows:
            half = w // 2
            s = sums[int(np.log2(w)) - 1][POOL_HALO - half:POOL_HALO - half + tm]
            parts.append(window_mean(s, half))
        mean = jnp.where(low, parts[0], parts[1])
        outs.append(mean - xs[POOL_HALO:POOL_HALO + tm])
    d = jnp.concatenate(outs, axis=-1).astype(jnp.bfloat16)
    return (_dot(d, w_ref[...]) * scale_ref[...]).astype(jnp.bfloat16)


def _inproj_kernel(*refs, pending, seq_len):
    n_stream = 7 if pending else 3
    stream, refs = refs[:n_stream], refs[n_stream:]
    if pending:
        h_ref, y_ref, g_ref, hp_ref, hn_ref, yp_ref, yn_ref = stream
        hres_ref, refs = refs[-2], refs[:-2] + refs[-1:]
        x = h_ref[...].astype(jnp.float32) + g_ref[...] * _moe_out(y_ref)
        hres_ref[...] = x.astype(hres_ref.dtype)
        before = hp_ref[...].astype(jnp.float32) + g_ref[...] * _moe_out(yp_ref)
        after = hn_ref[...].astype(jnp.float32) + g_ref[...] * _moe_out(yn_ref)
    else:
        h_ref, hp_ref, hn_ref = stream
        x = h_ref[...].astype(jnp.float32)
        before = hp_ref[...].astype(jnp.float32)
        after = hn_ref[...].astype(jnp.float32)
    x_halo = jnp.concatenate([before[HALO_BLOCK - POOL_HALO:], after[:POOL_HALO]], axis=0)
    (n1_ref, sh_ref, sc_ref, w_ref, qg_ref, kg_ref, sgn_ref, sgw_ref, sgb_ref,
     pw_ref, ps_ref, pool_ref, q_ref, k_ref, v_ref, sg_ref, xe_ref) = refs
    tm = h_ref.shape[0]
    i = pl.program_id(0)
    hn = _norm_modulate(x, n1_ref, sh_ref, sc_ref)
    hn_halo = _norm_modulate(x_halo, n1_ref, sh_ref, sc_ref)

    a_halo = _dot(hn_halo, w_ref[:, 0:Q_OFF])
    xe_ref[0:POOL_HALO, :] = jnp.where(i > 0, a_halo[:POOL_HALO], 0.0)
    xe_ref[POOL_HALO:POOL_HALO + tm, :] = _dot(hn, w_ref[:, 0:Q_OFF])
    xe_ref[POOL_HALO + tm:, :] = jnp.where(i < pl.num_programs(0) - 1, a_halo[POOL_HALO:], 0.0)

    a_g = _dot(hn, w_ref[:, G_OFF:IN_DIM])
    a_u = _dot(hn, w_ref[:, U_OFF:G_OFF])
    a_q = _dot(hn, w_ref[:, Q_OFF:K_OFF])
    a_k = _dot(hn, w_ref[:, K_OFF:V_OFF])
    _store_keys(v_ref, _dot(hn, w_ref[:, V_OFF:U_OFF]))

    gv = _gelu_tanh(a_g)
    q_ref[...] = (a_q * _head_rms_scale(a_q) * qg_ref[...]).astype(jnp.bfloat16)
    _store_keys(k_ref, a_k * _head_rms_scale(a_k) * kg_ref[...])

    u = _gelu_tanh(a_u)
    vn = (gv * _head_rms_scale(gv) * sgn_ref[...]).astype(jnp.bfloat16)
    low = lax.broadcasted_iota(jnp.int32, (SG_CHUNK, LANES), 1) < HEAD_DIM
    for c in range(tm // SG_CHUNK):
        rows = slice(c * SG_CHUNK, (c + 1) * SG_CHUNK)
        for s in range(SG_DIM // LANES):
            cols = slice(s * LANES, (s + 1) * LANES)
            m = _dot(sgw_ref[s], vn[rows, cols]) + sgb_ref[s]
            mixed = jnp.where(low, m[:SG_CHUNK], m[SG_CHUNK:])
            sg_ref[rows, cols] = (u[rows, cols] * mixed).astype(jnp.bfloat16)

    pool_ref[...] = _pool_mix(xe_ref, pw_ref, ps_ref, tm, seq_len)


def _inproj_call(stream, n1, sh, sc, w_in, layer, qg, kg, sgn, sgw, sgb, pool_w, pool_scale, tm, strips):
    pending = len(stream) == 3
    n = stream[0].shape[0]
    if strips:
        kv_spec = pl.BlockSpec((N_STRIPS, tm // N_STRIPS, NA_DIM), lambda i: (0, i, 0))
        kv_shape = jax.ShapeDtypeStruct((N_STRIPS, n // N_STRIPS, NA_DIM), jnp.bfloat16)
    else:
        kv_spec = pl.BlockSpec((tm, NA_DIM), lambda i: (i, 0))
        kv_shape = jax.ShapeDtypeStruct((n, NA_DIM), jnp.bfloat16)
    row = lambda i: (i, 0)
    fixed2 = lambda i: (0, 0)
    fixed3 = lambda i: (0, 0, 0)
    vec = lambda w: pl.BlockSpec((1, w), fixed2)
    rows = pl.BlockSpec((tm, D_MODEL), row)
    per_tile = tm // HALO_BLOCK
    before = lambda i: (jnp.maximum(i * per_tile - 1, 0), 0)
    after = lambda i: (jnp.minimum((i + 1) * per_tile, n // HALO_BLOCK - 1), 0)
    halo = lambda w, m: pl.BlockSpec((HALO_BLOCK, w), m)
    h = stream[0]
    if pending:
        y, g = stream[1], stream[2]
        yw = y.shape[1]
        args = [h, y, g, h, h, y, y]
        stream_specs = [rows, pl.BlockSpec((tm, yw), row), vec(D_MODEL),
                        halo(D_MODEL, before), halo(D_MODEL, after), halo(yw, before), halo(yw, after)]
    else:
        args = [h, h, h]
        stream_specs = [rows, halo(D_MODEL, before), halo(D_MODEL, after)]
    extra_out_specs = [rows] if pending else []
    extra_out_shape = [jax.ShapeDtypeStruct((n, D_MODEL), STREAM_DTYPE)] if pending else []
    return pl.pallas_call(
        functools.partial(_inproj_kernel, pending=pending, seq_len=n),
        grid=(n // tm,),
        in_specs=stream_specs + [
            vec(D_MODEL), vec(D_MODEL), vec(D_MODEL),
            pl.BlockSpec((None, D_MODEL, IN_DIM), lambda i: (layer, 0, 0)),
            vec(NA_DIM), vec(NA_DIM),
            vec(SG_DIM),
            pl.BlockSpec((SG_DIM // LANES, 2 * SG_CHUNK, SG_CHUNK), fixed3),
            pl.BlockSpec((SG_DIM // LANES, 2 * SG_CHUNK, LANES), fixed3),
            pl.BlockSpec((POOL_DIM, POOL_DIM), fixed2),
            vec(POOL_DIM),
        ],
        out_specs=[
            pl.BlockSpec((tm, POOL_DIM), row),
            pl.BlockSpec((tm, NA_DIM), row),
            kv_spec,
            kv_spec,
            pl.BlockSpec((tm, SG_DIM), row),
        ] + extra_out_specs,
        out_shape=[
            jax.ShapeDtypeStruct((n, POOL_DIM), jnp.bfloat16),
            jax.ShapeDtypeStruct((n, NA_DIM), jnp.bfloat16),
            kv_shape,
            kv_shape,
            jax.ShapeDtypeStruct((n, SG_DIM), jnp.bfloat16),
        ] + extra_out_shape,
        scratch_shapes=[pltpu.VMEM((tm + 2 * POOL_HALO, POOL_DIM), jnp.float32)],
        compiler_params=_cparams(("arbitrary",)),
        name="inproj",
    )(*args, n1, sh, sc, w_in, qg, kg, sgn, sgw, sgb, pool_w, pool_scale)


NA_ROWS_PER_BLOCK = 64
NA_GROUP_ROWS = 4
NA_WINDOW_ROWS = NA_GROUP_ROWS + NA_WIN_ROWS
NA_BLOCK = NA_ROWS_PER_BLOCK * GRID_W
NA_GROUP = NA_GROUP_ROWS * GRID_W
NA_HALF_COLS = GRID_W // 2
NA_CHAIN = NA_GROUP_ROWS * NA_HALF_COLS
NA_HALF_STRIPS = 5
NA_HALF_COL0 = (0, GRID_W - NA_HALF_STRIPS * STRIP_W)
NA_RUN = NA_WINDOW_ROWS * STRIP_W
NA_LOCAL = 512
NA_STRIP_BLOCK = NA_ROWS_PER_BLOCK * STRIP_W
NA_STRIP_HALO = (NA_WIN_ROWS // 2) * STRIP_W
NA_EDGE_FIRST, NA_EDGE_NONE, NA_EDGE_LAST = 0, 1, 2


def _stack_heads(x, low):
    zero = jnp.zeros_like(x)
    return jnp.concatenate([jnp.where(low, x, zero), jnp.where(low, zero, x)], axis=0)


def _natten_kernel(q_ref, kp_ref, kc_ref, kn_ref, vp_ref, vc_ref, vn_ref, kx_ref, vx_ref, bias_ref,
                   wg_ref, wu_ref, wd_ref,
                   o_ref, wg_bf_ref, wu_bf_ref, wd_bf_ref, kwin_ref, vwin_ref, vxe_ref, *, grid_rows):
    b = pl.program_id(1)
    wg_bf_ref[...] = wg_ref[...].astype(jnp.bfloat16)
    wu_bf_ref[...] = wu_ref[...].astype(jnp.bfloat16)
    wd_bf_ref[...] = wd_ref[...].astype(jnp.bfloat16)
    top, bottom = NA_STRIP_HALO, NA_STRIP_HALO + NA_STRIP_BLOCK
    kwin_ref[:, 0:top, :] = kp_ref[...]
    kwin_ref[:, top:bottom, :] = kc_ref[...]
    kwin_ref[:, bottom:, :] = kn_ref[...]
    vwin_ref[:, 0:top, 0:LANES] = vp_ref[...]
    vwin_ref[:, top:bottom, 0:LANES] = vc_ref[...]
    vwin_ref[:, bottom:, 0:LANES] = vn_ref[...]
    vwin_ref[:, :, LANES:] = jnp.ones(vwin_ref.shape[:2] + (LANES,), jnp.bfloat16)
    vxe_ref[:, 0:LANES] = vx_ref[...]
    vxe_ref[:, LANES:] = jnp.ones((vxe_ref.shape[0], LANES), jnp.bfloat16)
    low_q = lax.broadcasted_iota(jnp.int32, (NA_CHAIN, LANES), 1) < HEAD_DIM
    n_pad = NA_LOCAL - NA_HALF_STRIPS * NA_RUN

    def window_start(g):
        r0 = b * NA_ROWS_PER_BLOCK + g * NA_GROUP_ROWS
        ws = jnp.clip(r0 - NA_WIN_ROWS // 2, 0, grid_rows - NA_WINDOW_ROWS)
        edge = jnp.where(r0 == 0, NA_EDGE_FIRST,
                         jnp.where(r0 == grid_rows - NA_GROUP_ROWS, NA_EDGE_LAST, NA_EDGE_NONE))
        start = pl.multiple_of((ws - b * NA_ROWS_PER_BLOCK + NA_WIN_ROWS // 2) * STRIP_W, NA_STRIP_HALO)
        return start, edge

    def local_window(win_ref, g, half):
        start, _ = window_start(g)
        s0 = NA_HALF_COL0[half] // STRIP_W
        runs = [win_ref[s, pl.ds(start, NA_RUN), :] for s in range(s0, s0 + NA_HALF_STRIPS)]
        return jnp.concatenate(runs + [jnp.zeros((n_pad, win_ref.shape[2]), jnp.bfloat16)], axis=0)

    def query_rows(g, half, j):
        first = g * NA_GROUP + j * GRID_W + half * NA_HALF_COLS
        return slice(first, first + NA_HALF_COLS)

    def scores(c):
        g, half = divmod(c, 2)
        _, edge = window_start(g)
        qh = jnp.concatenate([q_ref[query_rows(g, half, j), :] for j in range(NA_GROUP_ROWS)], axis=0)
        lhs = _stack_heads(qh, low_q)
        kl = local_window(kwin_ref, g, half)
        return jnp.concatenate([_dot_nt(lhs, kl).astype(jnp.bfloat16) + bias_ref[edge, half],
                                _dot_nt(lhs, kx_ref[...]).astype(jnp.bfloat16)], axis=-1)

    n_chains = 2 * (NA_ROWS_PER_BLOCK // NA_GROUP_ROWS)
    s_next = scores(0)
    for c in range(n_chains):
        s = s_next
        if c + 1 < n_chains:
            s_next = scores(c + 1)
        g, half = divmod(c, 2)
        vl = local_window(vwin_ref, g, half)
        m = jnp.max(s, axis=-1, keepdims=True)
        pb = jnp.exp2(s - m)
        o = _dot(pb[:, :NA_LOCAL], vl) + _dot(pb[:, NA_LOCAL:], vxe_ref[...])
        o = o[:, :LANES] * (1.0 / o[:, LANES:])
        o = jnp.where(low_q, o[:NA_CHAIN], o[NA_CHAIN:]).astype(jnp.bfloat16)
        for j in range(NA_GROUP_ROWS):
            o_ref[query_rows(g, half, j), :] = o[j * NA_HALF_COLS:(j + 1) * NA_HALF_COLS]


def _natten_call(q, k, v, k_ctx, v_ctx, bias, w_gate, w_up, w_down, layer):
    n = q.shape[0]
    grid_rows = n // GRID_W
    assert grid_rows % NA_ROWS_PER_BLOCK == 0 and grid_rows >= 2 * NA_ROWS_PER_BLOCK
    nblk = n // NA_BLOCK
    steps = HEAD_PAIRS * nblk
    depth = w_gate.shape[0]
    up_rows = N_EXPERTS * D_MODEL
    down_rows = N_EXPERTS * D_EXPERT
    assert up_rows % steps == 0 and down_rows % steps == 0
    wg2 = w_gate.reshape(depth * up_rows, D_EXPERT)
    wu2 = w_up.reshape(depth * up_rows, D_EXPERT)
    wd2 = w_down.reshape(depth * down_rows, D_MODEL)
    up_in = pl.BlockSpec((up_rows // steps, D_EXPERT), lambda p, b: (layer * steps + p * nblk + b, 0))
    down_in = pl.BlockSpec((down_rows // steps, D_MODEL), lambda p, b: (layer * steps + p * nblk + b, 0))
    up_out = pl.BlockSpec((up_rows // steps, D_EXPERT), lambda p, b: (p * nblk + b, 0))
    down_out = pl.BlockSpec((down_rows // steps, D_MODEL), lambda p, b: (p * nblk + b, 0))
    n_halo = n // N_STRIPS // NA_STRIP_HALO
    hb = NA_STRIP_BLOCK // NA_STRIP_HALO
    rows = pl.BlockSpec((NA_BLOCK, LANES), lambda p, b: (b, p))
    cur = pl.BlockSpec((N_STRIPS, NA_STRIP_BLOCK, LANES), lambda p, b: (0, b, p))
    prev = pl.BlockSpec((N_STRIPS, NA_STRIP_HALO, LANES), lambda p, b: (0, jnp.maximum(b * hb - 1, 0), p))
    nxt = pl.BlockSpec((N_STRIPS, NA_STRIP_HALO, LANES),
                       lambda p, b: (0, jnp.minimum((b + 1) * hb, n_halo - 1), p))
    ctx = pl.BlockSpec((k_ctx.shape[0], LANES), lambda p, b: (0, p))
    win_rows = NA_STRIP_BLOCK + 2 * NA_STRIP_HALO
    att, wg_bf, wu_bf, wd_bf = pl.pallas_call(
        functools.partial(_natten_kernel, grid_rows=grid_rows),
        grid=(HEAD_PAIRS, nblk),
        in_specs=[rows, prev, cur, nxt, prev, cur, nxt, ctx, ctx,
                  pl.BlockSpec((None, 3, 2, 2 * NA_CHAIN, NA_LOCAL), lambda p, b: (p, 0, 0, 0, 0)),
                  up_in, up_in, down_in],
        out_specs=[rows, up_out, up_out, down_out],
        out_shape=[jax.ShapeDtypeStruct((n, NA_DIM), jnp.bfloat16),
                   jax.ShapeDtypeStruct((up_rows, D_EXPERT), jnp.bfloat16),
                   jax.ShapeDtypeStruct((up_rows, D_EXPERT), jnp.bfloat16),
                   jax.ShapeDtypeStruct((down_rows, D_MODEL), jnp.bfloat16)],
        scratch_shapes=[pltpu.VMEM((N_STRIPS, win_rows, LANES), jnp.bfloat16),
                        pltpu.VMEM((N_STRIPS, win_rows, 2 * LANES), jnp.bfloat16),
                        pltpu.VMEM((k_ctx.shape[0], 2 * LANES), jnp.bfloat16)],
        compiler_params=_cparams(("arbitrary", "arbitrary")),
        name="natten",
    )(q, k, k, k, v, v, v, k_ctx, v_ctx, bias, wg2, wu2, wd2)
    return (att, wg_bf.reshape(N_EXPERTS, D_MODEL, D_EXPERT), wu_bf.reshape(N_EXPERTS, D_MODEL, D_EXPERT),
            wd_bf.reshape(N_EXPERTS, D_EXPERT, D_MODEL))


def _natten_bias(rpb):
    cols = np.arange(GRID_W)
    col_start = np.clip(cols - NA_WIN_COLS // 2, 0, GRID_W - NA_WIN_COLS)
    kc = np.arange(GRID_W)
    in_win = (kc[None, :] >= col_start[:, None]) & (kc[None, :] < col_start[:, None] + NA_WIN_COLS)
    dc = kc[None, :] - cols[:, None] + NA_WIN_COLS - 1
    sel = (np.arange(2 * NA_WIN_COLS - 1)[:, None, None] == dc[None]) & in_win[None]
    t2 = jnp.einsum("hdj,jqk->hdqk", rpb, jnp.asarray(sel, jnp.float32), precision=lax.Precision.HIGHEST)
    t2 = jnp.where(in_win[None, None], t2 * LOG2_E, NEG_BIG)
    neg = jnp.full((NA_HEADS, 1, GRID_W, GRID_W), NEG_BIG, jnp.float32)
    t2e = jnp.concatenate([neg, t2, neg], axis=1)
    u = jnp.concatenate([t2e[:, :-1], t2e[:, 1:]], axis=-1)
    u = u.reshape(HEAD_PAIRS, 2, 2 * NA_WIN_ROWS, GRID_W, LANES)

    place = np.zeros((2, NA_WINDOW_ROWS * GRID_W, NA_LOCAL), np.float32)
    for half, c0 in enumerate(NA_HALF_COL0):
        for a in range(NA_WINDOW_ROWS):
            for kcol in range(c0, c0 + NA_HALF_STRIPS * STRIP_W):
                s, c8 = divmod(kcol - c0, STRIP_W)
                place[half, a * GRID_W + kcol, s * NA_RUN + a * STRIP_W + c8] = 1.0
    outside = np.full((3, NA_GROUP_ROWS, NA_LOCAL), NEG_BIG, np.float32)
    for edge in (NA_EDGE_FIRST, NA_EDGE_NONE, NA_EDGE_LAST):
        for j in range(NA_GROUP_ROWS):
            lo, _ = _window_rows(edge, j)
            for s in range(NA_HALF_STRIPS):
                outside[edge, j, s * NA_RUN + lo * STRIP_W:s * NA_RUN + (lo + NA_WIN_ROWS) * STRIP_W] = 0.0
    return pl.pallas_call(
        _bias_expand_kernel,
        grid=(HEAD_PAIRS,),
        in_specs=[pl.BlockSpec((None, 2, 2 * NA_WIN_ROWS, GRID_W, LANES), lambda p: (p, 0, 0, 0, 0)),
                  pl.BlockSpec(place.shape, lambda p: (0, 0, 0)),
                  pl.BlockSpec(outside.shape, lambda p: (0, 0, 0))],
        out_specs=pl.BlockSpec((None, 3, 2, 2 * NA_CHAIN, NA_LOCAL), lambda p: (p, 0, 0, 0, 0)),
        out_shape=jax.ShapeDtypeStruct((HEAD_PAIRS, 3, 2, 2 * NA_CHAIN, NA_LOCAL), jnp.bfloat16),
        compiler_params=_cparams(("arbitrary",)),
        name="bias_expand",
    )(u, jnp.asarray(place, jnp.bfloat16), jnp.asarray(outside))


def _window_rows(edge, j):
    if edge == NA_EDGE_FIRST:
        return 0, NA_WIN_ROWS - 1 - j
    if edge == NA_EDGE_NONE:
        return j, NA_WIN_ROWS // 2 - 1
    return NA_WINDOW_ROWS - NA_WIN_ROWS, NA_WIN_ROWS // 2 - 1 - j


def _bias_expand_kernel(u_ref, place_ref, outside_ref, o_ref):
    low = lax.broadcasted_iota(jnp.int32, (NA_HALF_COLS, LANES), 1) < GRID_W
    zero = jnp.zeros((NA_HALF_COLS, LANES), jnp.float32)
    for edge in (NA_EDGE_FIRST, NA_EDGE_NONE, NA_EDGE_LAST):
        for half in range(2):
            q0 = half * NA_HALF_COLS
            blocks, masks = [], []
            for hd in range(2):
                for j in range(NA_GROUP_ROWS):
                    lo, base = _window_rows(edge, j)
                    tiles = []
                    for i in range(NA_WINDOW_ROWS // 2):
                        a0, a1 = 2 * i, 2 * i + 1
                        ok0 = lo <= a0 < lo + NA_WIN_ROWS
                        ok1 = lo <= a1 < lo + NA_WIN_ROWS
                        if not (ok0 or ok1):
                            tile = zero
                        else:
                            tile = u_ref[hd, base + a1 - lo, q0:q0 + NA_HALF_COLS, :]
                            if not ok0:
                                tile = jnp.where(low, zero, tile)
                            if not ok1:
                                tile = jnp.where(low, tile, zero)
                        tiles.append(tile)
                    blocks.append(jnp.concatenate(tiles, axis=-1))
                    masks.append(jnp.broadcast_to(outside_ref[edge, j:j + 1, :], (NA_HALF_COLS, NA_LOCAL)))
            lhs = jnp.concatenate(blocks, axis=0).astype(jnp.bfloat16)
            placed = _dot(lhs, place_ref[half]) + jnp.concatenate(masks, axis=0)
            o_ref[edge, half] = placed.astype(o_ref.dtype)


def _ctxatt_kernel(q_ref, k_ref, v_ref, o_ref):
    lc = q_ref.shape[0]
    low = lax.broadcasted_iota(jnp.int32, (lc, LANES), 1) < HEAD_DIM
    lhs = _stack_heads(q_ref[...], low)
    s = _dot_nt(lhs, k_ref[...])
    m = jnp.max(s, axis=-1, keepdims=True)
    p = jnp.exp2(s - m)
    denom = jnp.sum(p, axis=-1, keepdims=True)
    o = _dot(p.astype(jnp.bfloat16), v_ref[...]) * (1.0 / denom)
    o_ref[...] = jnp.where(low, o[:lc], o[lc:]).astype(jnp.bfloat16)


def _ctxatt_call(q, k, v):
    lc = q.shape[0]
    spec = pl.BlockSpec((lc, LANES), lambda p: (0, p))
    return pl.pallas_call(
        _ctxatt_kernel,
        grid=(HEAD_PAIRS,),
        in_specs=[spec, spec, spec],
        out_specs=spec,
        out_shape=jax.ShapeDtypeStruct((lc, NA_DIM), jnp.bfloat16),
        compiler_params=_cparams(("arbitrary",)),
        name="ctxatt",
    )(q, k, v)


OUTPROJ_CHAIN = 256
OUTPROJ_TILE = 1024
INPROJ_TILE = 1024
RESIDUAL_TILE = 2048


def _outproj_kernel(h_ref, mp_ref, att_ref, sg_ref, wo_ref, g1_ref, n2_ref, sh_ref, sc_ref,
                    wr_ref, br_ref, h1_ref, pay_ref, route_ref):
    chains = [slice(c * OUTPROJ_CHAIN, (c + 1) * OUTPROJ_CHAIN) for c in range(h_ref.shape[0] // OUTPROJ_CHAIN)]
    h1s = []
    for rows in chains:
        mix = (_dot(mp_ref[rows, :], wo_ref[0:POOL_DIM, :])
               + _dot(att_ref[rows, :], wo_ref[POOL_DIM:POOL_DIM + NA_DIM, :])
               + _dot(sg_ref[rows, :], wo_ref[POOL_DIM + NA_DIM:, :]))
        h1 = h_ref[rows, :].astype(jnp.float32) + g1_ref[...] * mix
        h1_ref[rows, :] = h1.astype(h1_ref.dtype)
        h1s.append(h1)
    for rows, h1 in zip(chains, h1s):
        _outproj_route(rows, h1, n2_ref, sh_ref, sc_ref, wr_ref, br_ref, pay_ref, route_ref)


def _outproj_route(rows, h1, n2_ref, sh_ref, sc_ref, wr_ref, br_ref, pay_ref, route_ref):
    tm = OUTPROJ_CHAIN
    ms = jnp.mean(h1 * h1, axis=-1, keepdims=True)
    hm = (h1 * lax.rsqrt(ms + EPS)) * (n2_ref[...] * (1.0 + sc_ref[...])) + sh_ref[...]
    pay_ref[rows, 0:HALF_D] = _pack_halves(hm)

    hm_hi = hm.astype(jnp.bfloat16)
    lt = _dot_nt(wr_ref[...], hm_hi)
    logits = lt[:N_EXPERTS] + lt[N_EXPERTS:] + br_ref[...]
    e = jnp.exp(logits - jnp.max(logits, axis=0, keepdims=True))

    best = ga = gb = e1 = e2 = cls = None
    for c in range(N_CLASSES):
        a, b2 = int(CLASS_E1[c]), int(CLASS_E2[c])
        ea, eb = e[a:a + 1, :], e[b2:b2 + 1, :]
        s = ea + eb
        if best is None:
            best, ga, gb = s, ea, eb
            e1 = jnp.full_like(s, float(a))
            e2 = jnp.full_like(s, float(b2))
            cls = jnp.zeros_like(s)
        else:
            better = s > best
            best = jnp.where(better, s, best)
            ga = jnp.where(better, ea, ga)
            gb = jnp.where(better, eb, gb)
            e1 = jnp.where(better, float(a), e1)
            e2 = jnp.where(better, float(b2), e2)
            cls = jnp.where(better, float(c), cls)
    inv = 1.0 / best
    row = lax.broadcasted_iota(jnp.int32, (ROUTE_ROWS, tm), 0)
    rec = jnp.where(row == 0, ga * inv,
          jnp.where(row == 1, gb * inv,
          jnp.where(row == 2, e1,
          jnp.where(row == 3, e2,
          jnp.where(row == 4, cls, 0.0)))))
    route_ref[:, rows] = rec
    wide = jnp.concatenate([rec, jnp.zeros((LANES - ROUTE_ROWS, tm), jnp.float32)], axis=0)
    pay_ref[rows, HALF_D:] = pltpu.bitcast(wide.T, jnp.uint32)


def _outproj_call(h, mp, att, sg, w_out, layer, g1, n2, sh2, sc2, wr, br, tm):
    n = h.shape[0]
    row = lambda i: (i, 0)
    fixed = lambda i: (0, 0)
    vec = pl.BlockSpec((1, D_MODEL), fixed)
    return pl.pallas_call(
        _outproj_kernel,
        grid=(n // tm,),
        in_specs=[
            pl.BlockSpec((tm, D_MODEL), row),
            pl.BlockSpec((tm, POOL_DIM), row),
            pl.BlockSpec((tm, NA_DIM), row),
            pl.BlockSpec((tm, SG_DIM), row),
            pl.BlockSpec((None, D_MODEL, D_MODEL), lambda i: (layer, 0, 0)),
            vec, vec, vec, vec,
            pl.BlockSpec((2 * N_EXPERTS, D_MODEL), fixed),
            pl.BlockSpec((N_EXPERTS, 1), fixed),
        ],
        out_specs=[
            pl.BlockSpec((tm, D_MODEL), row),
            pl.BlockSpec((tm, PAYLOAD_W), row),
            pl.BlockSpec((ROUTE_ROWS, tm), lambda i: (0, i)),
        ],
        out_shape=[
            jax.ShapeDtypeStruct((n, D_MODEL), STREAM_DTYPE),
            jax.ShapeDtypeStruct((n, PAYLOAD_W), jnp.uint32),
            jax.ShapeDtypeStruct((ROUTE_ROWS, n), jnp.float32),
        ],
        compiler_params=_cparams(("arbitrary",)),
        name="outproj",
    )(h, mp, att, sg, w_out, g1, n2, sh2, sc2, wr, br)


def _payload_parts(pay_ref, rows=slice(None)):
    lo, hi = _unpack_halves(pay_ref[rows, 0:HALF_D])
    x = jnp.concatenate([lo, hi], axis=-1).astype(jnp.bfloat16)
    return x, pltpu.bitcast(pay_ref[rows, HALF_D:], jnp.float32)


def _expert_pair(x, ga, gb, wga, wua, wda, wgb, wub, wdb):
    ha = (_silu(_dot(x, wga)) * _dot(x, wua) * ga).astype(jnp.bfloat16)
    hb = (_silu(_dot(x, wgb)) * _dot(x, wub) * gb).astype(jnp.bfloat16)
    return _dot(ha, wda) + _dot(hb, wdb)


MOE_QUARTER = MOE_TM // 4


def _moe_sorted_kernel(blk_ref, e1_ref, e2_ref, quarters_ref, pay_ref,
                       wga_ref, wua_ref, wda_ref, wgb_ref, wub_ref, wdb_ref, o_ref):
    quarters = quarters_ref[pl.program_id(0)]

    def run(rows):
        x, route = _payload_parts(pay_ref, rows)
        y = _expert_pair(x, route[:, 0:1], route[:, 1:2], wga_ref[...], wua_ref[...], wda_ref[...],
                         wgb_ref[...], wub_ref[...], wdb_ref[...])
        o_ref[rows, :] = _pack_halves(y)

    for used in range(1, MOE_TM // MOE_QUARTER + 1):
        @pl.when(quarters == used)
        def _(used=used):
            run(slice(0, used * MOE_QUARTER))


def _moe_sorted_call(blk, e1, e2, quarters, pay_sorted, wg, wu, wd):
    n_tiles = blk.shape[0]
    rows = lambda i, blk, e1, e2, quarters: (blk[i], 0)
    wa = lambda i, blk, e1, e2, quarters: (e1[i], 0, 0)
    wb = lambda i, blk, e1, e2, quarters: (e2[i], 0, 0)
    up = lambda m: pl.BlockSpec((None, D_MODEL, D_EXPERT), m)
    down = lambda m: pl.BlockSpec((None, D_EXPERT, D_MODEL), m)
    return pl.pallas_call(
        _moe_sorted_kernel,
        grid_spec=pltpu.PrefetchScalarGridSpec(
            num_scalar_prefetch=4,
            grid=(n_tiles,),
            in_specs=[pl.BlockSpec((MOE_TM, PAYLOAD_W), rows),
                      up(wa), up(wa), down(wa), up(wb), up(wb), down(wb)],
            out_specs=pl.BlockSpec((MOE_TM, HALF_D), rows),
        ),
        out_shape=jax.ShapeDtypeStruct((n_tiles * MOE_TM, HALF_D), jnp.uint32),
        compiler_params=_cparams(("arbitrary",)),
        name="moe_sorted",
    )(blk, e1, e2, quarters, pay_sorted, wg, wu, wd, wg, wu, wd)


DENSE_EXPERTS_PER_STEP = GROUP_SIZE


def _moe_dense_kernel(*refs, first_step, resume):
    if resume:
        pay_ref, wg_ref, wu_ref, wd_ref, acc_ref, o_ref = refs
    else:
        pay_ref, wg_ref, wu_ref, wd_ref, o_ref = refs
    step = pl.program_id(0)

    @pl.when(step == 0)
    def _():
        o_ref[...] = acc_ref[...] if resume else jnp.zeros_like(o_ref)

    x, route = _payload_parts(pay_ref)
    y = None
    for k in range(DENSE_EXPERTS_PER_STEP):
        ef = ((first_step + step) * DENSE_EXPERTS_PER_STEP + k).astype(jnp.float32)
        gate = (jnp.where(route[:, 2:3] == ef, route[:, 0:1], 0.0)
                + jnp.where(route[:, 3:4] == ef, route[:, 1:2], 0.0))
        he = (_silu(_dot(x, wg_ref[k])) * _dot(x, wu_ref[k]) * gate).astype(jnp.bfloat16)
        yk = _dot(he, wd_ref[k])
        y = yk if y is None else y + yk
    o_ref[...] += y


def _moe_dense_call(pay, wg, wu, wd, first_step, n_steps, acc=None):
    n = pay.shape[0]
    per = DENSE_EXPERTS_PER_STEP
    whole = pl.BlockSpec((n, D_MODEL), lambda e: (0, 0))
    return pl.pallas_call(
        functools.partial(_moe_dense_kernel, first_step=first_step, resume=acc is not None),
        grid=(n_steps,),
        in_specs=[pl.BlockSpec((n, PAYLOAD_W), lambda e: (0, 0)),
                  pl.BlockSpec((per, D_MODEL, D_EXPERT), lambda e: (first_step + e, 0, 0)),
                  pl.BlockSpec((per, D_MODEL, D_EXPERT), lambda e: (first_step + e, 0, 0)),
                  pl.BlockSpec((per, D_EXPERT, D_MODEL), lambda e: (first_step + e, 0, 0))]
                 + ([whole] if acc is not None else []),
        out_specs=whole,
        out_shape=jax.ShapeDtypeStruct((n, D_MODEL), jnp.float32),
        compiler_params=_cparams(("arbitrary",)),
        name="moe_dense",
    )(pay, wg, wu, wd, *([acc] if acc is not None else []))


def _moe_out(y_ref):
    if y_ref.dtype == jnp.uint32:
        return jnp.concatenate(_unpack_halves(y_ref[...]), axis=-1)
    return y_ref[...]


def _residual_kernel(h_ref, y_ref, g_ref, o_ref):
    o_ref[...] = h_ref[...].astype(jnp.float32) + g_ref[...] * _moe_out(y_ref)


def _residual_call(h, y, g, tm):
    n = h.shape[0]
    row = pl.BlockSpec((tm, D_MODEL), lambda i: (i, 0))
    return pl.pallas_call(
        _residual_kernel,
        grid=(n // tm,),
        in_specs=[row, pl.BlockSpec((tm, y.shape[1]), lambda i: (i, 0)),
                  pl.BlockSpec((1, D_MODEL), lambda i: (0, 0))],
        out_specs=row,
        out_shape=jax.ShapeDtypeStruct((n, D_MODEL), jnp.float32),
        compiler_params=_cparams(("arbitrary",)),
        name="residual",
    )(h, y, g)


SC_ROWS = 128


SC_CORES = 2
SC_SUBCORES = 16
SC_WORKERS = SC_CORES * SC_SUBCORES


def _sc_mesh():
    return plsc.VectorSubcoreMesh(core_axis_name="core", subcore_axis_name="subcore")


def _sc_worker():
    return lax.axis_index("subcore") * SC_CORES + lax.axis_index("core")


def _scatter_rows(x, dest, n_out):
    n, w = x.shape

    per_worker = n // SC_WORKERS
    assert per_worker % SC_ROWS == 0

    @functools.partial(pl.kernel, out_type=jax.ShapeDtypeStruct((n_out, w), x.dtype), mesh=_sc_mesh(),
                       scratch_types=[pltpu.VMEM((SC_ROWS,), jnp.int32), pltpu.VMEM((SC_ROWS, w), x.dtype)])
    def scatter(x_hbm, i_hbm, o_hbm, idx_v, rows_v):
        first = _sc_worker() * per_worker

        @pl.loop(0, per_worker // SC_ROWS)
        def _(i):
            base = pl.multiple_of(first + i * SC_ROWS, SC_ROWS)
            pltpu.sync_copy(i_hbm.at[pl.ds(base, SC_ROWS)], idx_v)
            pltpu.sync_copy(x_hbm.at[pl.ds(base, SC_ROWS)], rows_v)
            pltpu.sync_copy(rows_v, o_hbm.at[idx_v])

    return scatter(x, dest)


def _gather_rows(x, src):
    n = src.shape[0]
    w = x.shape[1]

    per_worker = n // SC_WORKERS
    assert per_worker % SC_ROWS == 0

    @functools.partial(pl.kernel, out_type=jax.ShapeDtypeStruct((n, w), x.dtype), mesh=_sc_mesh(),
                       scratch_types=[pltpu.VMEM((SC_ROWS,), jnp.int32), pltpu.VMEM((SC_ROWS, w), x.dtype)])
    def gather(x_hbm, i_hbm, o_hbm, idx_v, rows_v):
        first = _sc_worker() * per_worker

        @pl.loop(0, per_worker // SC_ROWS)
        def _(i):
            base = pl.multiple_of(first + i * SC_ROWS, SC_ROWS)
            pltpu.sync_copy(i_hbm.at[pl.ds(base, SC_ROWS)], idx_v)
            pltpu.sync_copy(x_hbm.at[idx_v], rows_v)
            pltpu.sync_copy(rows_v, o_hbm.at[pl.ds(base, SC_ROWS)])

    return gather(x, src)


def _routing_plan(cls, n_tiles):
    onehot = (cls[:, None] == jnp.arange(N_CLASSES, dtype=jnp.int32)[None, :]).astype(jnp.int32)
    counts = jnp.sum(onehot, axis=0)
    rank = jnp.sum((jnp.cumsum(onehot, axis=0) - onehot) * onehot, axis=1)
    tiles = (counts + MOE_TM - 1) // MOE_TM
    tile_end = jnp.cumsum(tiles)
    tile_start = tile_end - tiles
    dest = jnp.sum(onehot * tile_start[None, :], axis=1) * MOE_TM + rank
    nact = tile_end[-1]
    tile = jnp.arange(n_tiles, dtype=jnp.int32)
    blk = jnp.minimum(tile, nact - 1)
    tile_cls = jnp.sum((blk[:, None] >= tile_end[None, :]).astype(jnp.int32), axis=1)
    e1 = jnp.asarray(CLASS_E1)[tile_cls]
    e2 = jnp.asarray(CLASS_E2)[tile_cls]
    valid = counts[tile_cls] - (blk - tile_start[tile_cls]) * MOE_TM
    quarters = jnp.where(tile >= nact, 0, (jnp.minimum(valid, MOE_TM) + MOE_QUARTER - 1) // MOE_QUARTER)
    return dest.astype(jnp.int32), blk, e1, e2, quarters.astype(jnp.int32)


def _row_tile(n, prefer=512):
    return next(t for t in (prefer, 512, 256) if n % t == 0)


def kernel(x, c, ctx, c_ctx, w_ada, b_ada, norm1, w_in, pool_w, pool_scale, q_norm, k_norm, rpb,
           sg_w, sg_b, sg_norm, w_out, norm2, w_router, b_router, w_gate, w_up, w_down):
    depth = w_ada.shape[0]
    n = x.shape[1]
    lc = ctx.shape[1]
    bf = jnp.bfloat16
    lat_stream = (x[0],)
    h_ctx = ctx[0]

    cond = jnp.stack([c[0], c_ctx], axis=1)
    mod = _ada_call(cond, w_ada, b_ada)

    wr_t = w_router.T
    wr_hi = wr_t.astype(bf)
    wr_lo = (wr_t - wr_hi.astype(jnp.float32)).astype(bf)
    wr = jnp.concatenate([wr_hi, wr_lo], axis=0)
    br = b_router.reshape(N_EXPERTS, 1)
    n_tiles = n // MOE_TM + N_CLASSES
    w_in_bf = w_in.astype(bf)
    w_out_bf = w_out.astype(bf)

    for l in range(depth):
        last = l == depth - 1
        qg =(q_norm[l] * (HEAD_DIM ** -0.5 * LOG2_E)).reshape(1, NA_DIM)
        kg = k_norm[l].reshape(1, NA_DIM)
        sgn = sg_norm[l].reshape(1, SG_DIM)
        sgw = sg_w[l].astype(bf).reshape(SG_DIM // LANES, 2 * SG_CHUNK, SG_CHUNK)
        sgb = jnp.broadcast_to(sg_b[l].reshape(SG_DIM // LANES, 2 * SG_CHUNK, 1),
                               (SG_DIM // LANES, 2 * SG_CHUNK, LANES))
        pool_bd = jax.scipy.linalg.block_diag(*[pool_w[l, g] for g in range(len(POOL_WINDOWS))]).astype(bf)
        pscale = pool_scale[l].reshape(1, POOL_DIM)
        bias = _natten_bias(rpb[l])
        n1 = norm1[l].reshape(1, D_MODEL)
        n2 = norm2[l].reshape(1, D_MODEL)

        def mods(row):
            return [mod[l, row:row + 1, i * D_MODEL:(i + 1) * D_MODEL] for i in range(6)]

        sh1, sc1, g1, sh2, sc2, g2 = mods(0)
        csh1, csc1, cg1, csh2, csc2, cg2 = mods(1)

        tc = _row_tile(lc)
        mix_pool_c, q_c, k_c, v_c, sg_c = _inproj_call((h_ctx,), n1, csh1, csc1, w_in_bf, l, qg, kg,
                                                       sgn, sgw, sgb, pool_bd, pscale, tc, strips=False)

        outs = _inproj_call(lat_stream, n1, sh1, sc1, w_in_bf, l, qg, kg, sgn, sgw, sgb,
                            pool_bd, pscale, _row_tile(n, INPROJ_TILE), strips=True)
        mix_pool, q, k, v, sg = outs[:5]
        h_lat = outs[5] if len(lat_stream) == 3 else lat_stream[0]
        att, wg_l, wu_l, wd_l = _natten_call(q, k, v, k_c, v_c, bias, w_gate, w_up, w_down, l)
        h1, pay, route = _outproj_call(h_lat, mix_pool, att, sg, w_out_bf, l, g1, n2, sh2, sc2, wr, br,
                                       _row_tile(n, OUTPROJ_TILE))
        cls = route[4].astype(jnp.int32)
        dest, blk, e1, e2, quarters = _routing_plan(cls, n_tiles)
        pay_sorted = _scatter_rows(pay, dest, n_tiles * MOE_TM)

        if not last:
            dense_steps = N_EXPERTS // DENSE_EXPERTS_PER_STEP
            att_c = _ctxatt_call(q_c, k_c, v_c)
            h1_c, pay_c, _ = _outproj_call(h_ctx, mix_pool_c, att_c, sg_c, w_out_bf, l, cg1, n2, csh2, csc2,
                                           wr, br, tc)
            y_c = _moe_dense_call(pay_c, wg_l, wu_l, wd_l, 0, dense_steps // 2)
            quarters, y_c = lax.optimization_barrier((quarters, y_c))

        y_sorted = _moe_sorted_call(blk, e1, e2, quarters, pay_sorted, wg_l, wu_l, wd_l)
        y = _gather_rows(y_sorted, dest)
        lat_stream = (h1, y, g2)

        if not last:
            y_c = _moe_dense_call(pay_c, wg_l, wu_l, wd_l, dense_steps // 2, dense_steps - dense_steps // 2,
                                  acc=y_c)
            h_ctx = _residual_call(h1_c, y_c, cg2, tc)

    return _residual_call(*lat_stream, _row_tile(n, RESIDUAL_TILE))[None]
```

```python
import functools

import jax
import jax.numpy as jnp
import numpy as np
from jax import lax
from jax.experimental import pallas as pl
from jax.experimental.pallas import tpu as pltpu
from jax.experimental.pallas import tpu_sc as plsc

D_MODEL = 1024
GRID_W = 64
HEAD_DIM = 64
POOL_WINDOWS = (2, 4, 8, 16)
POOL_DIM = 256
NA_HEADS = 8
NA_DIM = 512
NA_WIN_ROWS = 8
NA_WIN_COLS = 16
SG_DIM = 256
SG_CHUNK = 128
Q_OFF = POOL_DIM
K_OFF = Q_OFF + NA_DIM
V_OFF = K_OFF + NA_DIM
U_OFF = V_OFF + NA_DIM
G_OFF = U_OFF + SG_DIM
IN_DIM = G_OFF + SG_DIM
N_EXPERTS = 16
GROUP_SIZE = 4
D_EXPERT = 512
EPS = 1e-6

LANES = 128
SUBLANES = 8
HEAD_PAIRS = NA_DIM // LANES
VMEM_LIMIT = 48 * 1024 * 1024

PAIRS = ((0, 1), (0, 2), (1, 2), (1, 3), (0, 3), (2, 3))
N_CLASSES = (N_EXPERTS // GROUP_SIZE) * len(PAIRS)
CLASS_E1 = np.array([4 * g + i for g in range(4) for (i, j) in PAIRS], np.int32)
CLASS_E2 = np.array([4 * g + j for g in range(4) for (i, j) in PAIRS], np.int32)

ROUTE_ROWS = 8
HALF_D = D_MODEL // 2
PAYLOAD_W = HALF_D + LANES
MOE_TM = 512
NEG_BIG = -1e30
LOG2_E = 1.4426950408889634


def _cparams(sem):
    return pltpu.CompilerParams(dimension_semantics=sem, vmem_limit_bytes=VMEM_LIMIT)


def _dot(a, b):
    return jnp.dot(a, b, preferred_element_type=jnp.float32)


def _dot_nt(a, b):
    return lax.dot_general(a, b, (((1,), (1,)), ((), ())), preferred_element_type=jnp.float32)


def _gelu_tanh(x):
    return 0.5 * x * (1.0 + jnp.tanh(0.7978845608028654 * (x + 0.044715 * (x * x * x))))


def _silu(x):
    return x * (1.0 / (1.0 + jnp.exp(-x)))


def _pack_halves(x):
    w = x.shape[1] // 2
    lo = pltpu.bitcast(x[:, :w].astype(jnp.bfloat16).astype(jnp.float32), jnp.uint32) >> 16
    hi = pltpu.bitcast(x[:, w:].astype(jnp.bfloat16).astype(jnp.float32), jnp.uint32) & jnp.uint32(0xFFFF0000)
    return lo | hi


def _unpack_halves(words):
    lo = pltpu.bitcast(words << 16, jnp.float32)
    hi = pltpu.bitcast(words & jnp.uint32(0xFFFF0000), jnp.float32)
    return lo, hi


def _ada_kernel(cond_ref, w_ref, b_ref, o_ref):
    cond = _silu(cond_ref[...])
    w = w_ref[...]
    rows = [jnp.sum(w * cond[:, r:r + 1], axis=0, keepdims=True) + b_ref[...] for r in range(2)]
    o_ref[...] = jnp.concatenate(rows + [jnp.zeros((SUBLANES - 2, w.shape[1]), jnp.float32)], axis=0)


def _prep_kernel(cond_ref, w_ref, b_ref, u_ref, place_ref, outside_ref, mod_ref, bias_ref):
    _ada_kernel(cond_ref, w_ref, b_ref, mod_ref)
    _bias_expand_kernel(u_ref, place_ref, outside_ref, bias_ref)


def _prep_call(cond, w_ada, b_ada, rpb):
    depth = w_ada.shape[0]
    tn = 6 * D_MODEL // HEAD_PAIRS
    u, place, outside = _natten_bias_inputs(rpb)
    return pl.pallas_call(
        _prep_kernel,
        grid=(depth, HEAD_PAIRS),
        in_specs=[
            pl.BlockSpec((D_MODEL, 2), lambda l, j: (0, 0)),
            pl.BlockSpec((None, D_MODEL, tn), lambda l, j: (l, 0, j)),
            pl.BlockSpec((None, 1, tn), lambda l, j: (l, 0, j)),
            pl.BlockSpec((None, None, 2, 2 * NA_WIN_ROWS, GRID_W, LANES), lambda l, j: (l, j, 0, 0, 0, 0)),
            pl.BlockSpec(place.shape, lambda l, j: (0, 0, 0)),
            pl.BlockSpec(outside.shape, lambda l, j: (0, 0, 0)),
        ],
        out_specs=[pl.BlockSpec((None, SUBLANES, tn), lambda l, j: (l, 0, j)),
                   pl.BlockSpec((None, None, 3, 2, 2 * NA_CHAIN, NA_LOCAL), lambda l, j: (l, j, 0, 0, 0, 0))],
        out_shape=[jax.ShapeDtypeStruct((depth, SUBLANES, 6 * D_MODEL), jnp.float32),
                   jax.ShapeDtypeStruct((depth, HEAD_PAIRS, 3, 2, 2 * NA_CHAIN, NA_LOCAL), jnp.bfloat16)],
        compiler_params=_cparams(("arbitrary", "arbitrary")),
        name="prep",
    )(cond, w_ada, b_ada.reshape(depth, 1, 6 * D_MODEL), u, place, outside)


def _norm_modulate(x, n_ref, sh_ref, sc_ref):
    ms = jnp.mean(x * x, axis=-1, keepdims=True)
    return ((x * lax.rsqrt(ms + EPS)) * (n_ref[...] * (1.0 + sc_ref[...])) + sh_ref[...]).astype(jnp.bfloat16)


def _head_rms_scale(a):
    low = lax.broadcasted_iota(jnp.int32, (a.shape[0], LANES), 1) < HEAD_DIM
    blocks = []
    for p in range(a.shape[1] // LANES):
        sq = jnp.square(a[:, p * LANES:(p + 1) * LANES])
        s_lo = jnp.sum(jnp.where(low, sq, 0.0), axis=-1, keepdims=True)
        s_hi = jnp.sum(jnp.where(low, 0.0, sq), axis=-1, keepdims=True)
        blocks.append(jnp.where(low, lax.rsqrt(s_lo * (1.0 / HEAD_DIM) + EPS),
                                lax.rsqrt(s_hi * (1.0 / HEAD_DIM) + EPS)))
    return jnp.concatenate(blocks, axis=-1)


STRIP_W = 8
N_STRIPS = GRID_W // STRIP_W


def _store_keys(ref, x):
    if len(ref.shape) == 2:
        ref[...] = x.astype(jnp.bfloat16)
        return
    pair = 2 * STRIP_W
    for s in range(N_STRIPS):
        for rp in range(x.shape[0] // (2 * GRID_W)):
            top = 2 * rp * GRID_W + s * STRIP_W
            rows = jnp.concatenate([x[top:top + STRIP_W], x[top + GRID_W:top + GRID_W + STRIP_W]], axis=0)
            ref[s, rp * pair:(rp + 1) * pair, :] = rows.astype(jnp.bfloat16)


POOL_HALO = 8
HALO_BLOCK = 16
STREAM_DTYPE = jnp.bfloat16


POOL_EDGE = 16


def _pool_mix(xe_ref, w_ref, scale_ref, tm, seq_len):
    low = lax.broadcasted_iota(jnp.int32, (tm, LANES), 1) < HEAD_DIM
    t_edge = pl.program_id(0) * tm + lax.broadcasted_iota(jnp.int32, (POOL_EDGE, LANES), 0)

    def window_mean(s, half):
        mean = s * (1.0 / (2 * half))

        def clip_fix(t):
            count = (jnp.minimum(t + half, seq_len) - jnp.maximum(t - half, 0)).astype(jnp.float32)
            return (2.0 * half) / count

        return jnp.concatenate([mean[:POOL_EDGE] * clip_fix(t_edge),
                                mean[POOL_EDGE:tm - POOL_EDGE],
                                mean[tm - POOL_EDGE:] * clip_fix(t_edge + (tm - POOL_EDGE))], axis=0)

    def window_sums(xs, n_levels):
        sums = []
        s = xs
        for k in range(n_levels):
            step = 1 << k
            s = s[:-step] + s[step:]
            sums.append(s)
        return sums

    outs = []
    for half_block, windows in enumerate(((2, 4), (8, 16))):
        xs = xe_ref[:, half_block * LANES:(half_block + 1) * LANES]
        sums = window_sums(xs, int(np.log2(windows[1])))
        parts = []
        for w in windows:
            half = w // 2
            s = sums[int(np.log2(w)) - 1][POOL_HALO - half:POOL_HALO - half + tm]
            parts.append(window_mean(s, half))
        mean = jnp.where(low, parts[0], parts[1])
        outs.append(mean - xs[POOL_HALO:POOL_HALO + tm])
    d = jnp.concatenate(outs, axis=-1).astype(jnp.bfloat16)
    return (_dot(d, w_ref[...]) * scale_ref[...]).astype(jnp.bfloat16)


def _inproj_kernel(*refs, pending, seq_len):
    n_stream = 7 if pending else 3
    stream, refs = refs[:n_stream], refs[n_stream:]
    if pending:
        h_ref, y_ref, g_ref, hp_ref, hn_ref, yp_ref, yn_ref = stream
        hres_ref, refs = refs[-2], refs[:-2] + refs[-1:]
        x = h_ref[...].astype(jnp.float32) + g_ref[...] * _moe_out(y_ref)
        hres_ref[...] = x.astype(hres_ref.dtype)
        before = hp_ref[...].astype(jnp.float32) + g_ref[...] * _moe_out(yp_ref)
        after = hn_ref[...].astype(jnp.float32) + g_ref[...] * _moe_out(yn_ref)
    else:
        h_ref, hp_ref, hn_ref = stream
        x = h_ref[...].astype(jnp.float32)
        before = hp_ref[...].astype(jnp.float32)
        after = hn_ref[...].astype(jnp.float32)
    x_halo = jnp.concatenate([before[HALO_BLOCK - POOL_HALO:], after[:POOL_HALO]], axis=0)
    (n1_ref, sh_ref, sc_ref, w_ref, qg_ref, kg_ref, sgn_ref, sgw_ref, sgb_ref,
     pw_ref, ps_ref, pool_ref, q_ref, k_ref, v_ref, sg_ref, xe_ref) = refs
    tm = h_ref.shape[0]
    i = pl.program_id(0)
    hn = _norm_modulate(x, n1_ref, sh_ref, sc_ref)
    hn_halo = _norm_modulate(x_halo, n1_ref, sh_ref, sc_ref)

    a_halo = _dot(hn_halo, w_ref[:, 0:Q_OFF])
    xe_ref[0:POOL_HALO, :] = jnp.where(i > 0, a_halo[:POOL_HALO], 0.0)
    xe_ref[POOL_HALO:POOL_HALO + tm, :] = _dot(hn, w_ref[:, 0:Q_OFF])
    xe_ref[POOL_HALO + tm:, :] = jnp.where(i < pl.num_programs(0) - 1, a_halo[POOL_HALO:], 0.0)

    a_g = _dot(hn, w_ref[:, G_OFF:IN_DIM])
    a_u = _dot(hn, w_ref[:, U_OFF:G_OFF])
    a_q = _dot(hn, w_ref[:, Q_OFF:K_OFF])
    a_k = _dot(hn, w_ref[:, K_OFF:V_OFF])
    _store_keys(v_ref, _dot(hn, w_ref[:, V_OFF:U_OFF]))

    gv = _gelu_tanh(a_g)
    q_ref[...] = (a_q * _head_rms_scale(a_q) * qg_ref[...]).astype(jnp.bfloat16)
    _store_keys(k_ref, a_k * _head_rms_scale(a_k) * kg_ref[...])

    u = _gelu_tanh(a_u)
    vn = (gv * _head_rms_scale(gv) * sgn_ref[...]).astype(jnp.bfloat16)
    low = lax.broadcasted_iota(jnp.int32, (SG_CHUNK, LANES), 1) < HEAD_DIM
    for c in range(tm // SG_CHUNK):
        rows = slice(c * SG_CHUNK, (c + 1) * SG_CHUNK)
        for s in range(SG_DIM // LANES):
            cols = slice(s * LANES, (s + 1) * LANES)
            m = _dot(sgw_ref[s], vn[rows, cols]) + sgb_ref[s]
            mixed = jnp.where(low, m[:SG_CHUNK], m[SG_CHUNK:])
            sg_ref[rows, cols] = (u[rows, cols] * mixed).astype(jnp.bfloat16)

    pool_ref[...] = _pool_mix(xe_ref, pw_ref, ps_ref, tm, seq_len)


def _inproj_call(stream, n1, sh, sc, w_in, layer, qg, kg, sgn, sgw, sgb, pool_w, pool_scale, tm, strips):
    pending = len(stream) == 3
    n = stream[0].shape[0]
    if strips:
        kv_spec = pl.BlockSpec((N_STRIPS, tm // N_STRIPS, NA_DIM), lambda i: (0, i, 0))
        kv_shape = jax.ShapeDtypeStruct((N_STRIPS, n // N_STRIPS, NA_DIM), jnp.bfloat16)
    else:
        kv_spec = pl.BlockSpec((tm, NA_DIM), lambda i: (i, 0))
        kv_shape = jax.ShapeDtypeStruct((n, NA_DIM), jnp.bfloat16)
    row = lambda i: (i, 0)
    fixed2 = lambda i: (0, 0)
    fixed3 = lambda i: (0, 0, 0)
    vec = lambda w: pl.BlockSpec((1, w), fixed2)
    rows = pl.BlockSpec((tm, D_MODEL), row)
    per_tile = tm // HALO_BLOCK
    before = lambda i: (jnp.maximum(i * per_tile - 1, 0), 0)
    after = lambda i: (jnp.minimum((i + 1) * per_tile, n // HALO_BLOCK - 1), 0)
    halo = lambda w, m: pl.BlockSpec((HALO_BLOCK, w), m)
    h = stream[0]
    if pending:
        y, g = stream[1], stream[2]
        yw = y.shape[1]
        args = [h, y, g, h, h, y, y]
        stream_specs = [rows, pl.BlockSpec((tm, yw), row), vec(D_MODEL),
                        halo(D_MODEL, before), halo(D_MODEL, after), halo(yw, before), halo(yw, after)]
    else:
        args = [h, h, h]
        stream_specs = [rows, halo(D_MODEL, before), halo(D_MODEL, after)]
    extra_out_specs = [rows] if pending else []
    extra_out_shape = [jax.ShapeDtypeStruct((n, D_MODEL), STREAM_DTYPE)] if pending else []
    return pl.pallas_call(
        functools.partial(_inproj_kernel, pending=pending, seq_len=n),
        grid=(n // tm,),
        in_specs=stream_specs + [
            vec(D_MODEL), vec(D_MODEL), vec(D_MODEL),
            pl.BlockSpec((None, D_MODEL, IN_DIM), lambda i: (layer, 0, 0)),
            vec(NA_DIM), vec(NA_DIM),
            vec(SG_DIM),
            pl.BlockSpec((SG_DIM // LANES, 2 * SG_CHUNK, SG_CHUNK), fixed3),
            pl.BlockSpec((SG_DIM // LANES, 2 * SG_CHUNK, LANES), fixed3),
            pl.BlockSpec((POOL_DIM, POOL_DIM), fixed2),
            vec(POOL_DIM),
        ],
        out_specs=[
            pl.BlockSpec((tm, POOL_DIM), row),
            pl.BlockSpec((tm, NA_DIM), row),
            kv_spec,
            kv_spec,
            pl.BlockSpec((tm, SG_DIM), row),
        ] + extra_out_specs,
        out_shape=[
            jax.ShapeDtypeStruct((n, POOL_DIM), jnp.bfloat16),
            jax.ShapeDtypeStruct((n, NA_DIM), jnp.bfloat16),
            kv_shape,
            kv_shape,
            jax.ShapeDtypeStruct((n, SG_DIM), jnp.bfloat16),
        ] + extra_out_shape,
        scratch_shapes=[pltpu.VMEM((tm + 2 * POOL_HALO, POOL_DIM), jnp.float32)],
        compiler_params=_cparams(("arbitrary",)),
        name="inproj",
    )(*args, n1, sh, sc, w_in, qg, kg, sgn, sgw, sgb, pool_w, pool_scale)


NA_ROWS_PER_BLOCK = 64
NA_GROUP_ROWS = 4
NA_WINDOW_ROWS = NA_GROUP_ROWS + NA_WIN_ROWS
NA_BLOCK = NA_ROWS_PER_BLOCK * GRID_W
NA_GROUP = NA_GROUP_ROWS * GRID_W
NA_HALF_COLS = GRID_W // 2
NA_CHAIN = NA_GROUP_ROWS * NA_HALF_COLS
NA_HALF_STRIPS = 5
NA_HALF_COL0 = (0, GRID_W - NA_HALF_STRIPS * STRIP_W)
NA_RUN = NA_WINDOW_ROWS * STRIP_W
NA_LOCAL = 512
NA_STRIP_BLOCK = NA_ROWS_PER_BLOCK * STRIP_W
NA_STRIP_HALO = (NA_WIN_ROWS // 2) * STRIP_W
NA_EDGE_FIRST, NA_EDGE_NONE, NA_EDGE_LAST = 0, 1, 2


def _stack_heads(x, low):
    zero = jnp.zeros_like(x)
    return jnp.concatenate([jnp.where(low, x, zero), jnp.where(low, zero, x)], axis=0)


def _natten_kernel(q_ref, kp_ref, kc_ref, kn_ref, vp_ref, vc_ref, vn_ref, kx_ref, vx_ref, bias_ref,
                   wg_ref, wu_ref, wd_ref,
                   o_ref, wg_bf_ref, wu_bf_ref, wd_bf_ref, kwin_ref, vwin_ref, vxe_ref, *, grid_rows):
    b = pl.program_id(1)
    wg_bf_ref[...] = wg_ref[...].astype(jnp.bfloat16)
    wu_bf_ref[...] = wu_ref[...].astype(jnp.bfloat16)
    wd_bf_ref[...] = wd_ref[...].astype(jnp.bfloat16)
    top, bottom = NA_STRIP_HALO, NA_STRIP_HALO + NA_STRIP_BLOCK
    kwin_ref[:, 0:top, :] = kp_ref[...]
    kwin_ref[:, top:bottom, :] = kc_ref[...]
    kwin_ref[:, bottom:, :] = kn_ref[...]
    vwin_ref[:, 0:top, 0:LANES] = vp_ref[...]
    vwin_ref[:, top:bottom, 0:LANES] = vc_ref[...]
    vwin_ref[:, bottom:, 0:LANES] = vn_ref[...]
    vwin_ref[:, :, LANES:] = jnp.ones(vwin_ref.shape[:2] + (LANES,), jnp.bfloat16)
    vxe_ref[:, 0:LANES] = vx_ref[...]
    vxe_ref[:, LANES:] = jnp.ones((vxe_ref.shape[0], LANES), jnp.bfloat16)
    low_q = lax.broadcasted_iota(jnp.int32, (NA_CHAIN, LANES), 1) < HEAD_DIM
    n_pad = NA_LOCAL - NA_HALF_STRIPS * NA_RUN

    def window_start(g):
        r0 = b * NA_ROWS_PER_BLOCK + g * NA_GROUP_ROWS
        ws = jnp.clip(r0 - NA_WIN_ROWS // 2, 0, grid_rows - NA_WINDOW_ROWS)
        edge = jnp.where(r0 == 0, NA_EDGE_FIRST,
                         jnp.where(r0 == grid_rows - NA_GROUP_ROWS, NA_EDGE_LAST, NA_EDGE_NONE))
        start = pl.multiple_of((ws - b * NA_ROWS_PER_BLOCK + NA_WIN_ROWS // 2) * STRIP_W, NA_STRIP_HALO)
        return start, edge

    def local_window(win_ref, g, half):
        start, _ = window_start(g)
        s0 = NA_HALF_COL0[half] // STRIP_W
        runs = [win_ref[s, pl.ds(start, NA_RUN), :] for s in range(s0, s0 + NA_HALF_STRIPS)]
        return jnp.concatenate(runs + [jnp.zeros((n_pad, win_ref.shape[2]), jnp.bfloat16)], axis=0)

    def query_rows(g, half, j):
        first = g * NA_GROUP + j * GRID_W + half * NA_HALF_COLS
        return slice(first, first + NA_HALF_COLS)

    def scores(c):
        g, half = divmod(c, 2)
        _, edge = window_start(g)
        qh = jnp.concatenate([q_ref[query_rows(g, half, j), :] for j in range(NA_GROUP_ROWS)], axis=0)
        lhs = _stack_heads(qh, low_q)
        kl = local_window(kwin_ref, g, half)
        return jnp.concatenate([_dot_nt(lhs, kl).astype(jnp.bfloat16) + bias_ref[edge, half],
                                _dot_nt(lhs, kx_ref[...]).astype(jnp.bfloat16)], axis=-1)

    n_chains = 2 * (NA_ROWS_PER_BLOCK // NA_GROUP_ROWS)
    s_next = scores(0)
    for c in range(n_chains):
        s = s_next
        if c + 1 < n_chains:
            s_next = scores(c + 1)
        g, half = divmod(c, 2)
        vl = local_window(vwin_ref, g, half)
        m = jnp.max(s, axis=-1, keepdims=True)
        pb = jnp.exp2(s - m)
        o = _dot(pb[:, :NA_LOCAL], vl) + _dot(pb[:, NA_LOCAL:], vxe_ref[...])
        o = o[:, :LANES] * (1.0 / o[:, LANES:])
        o = jnp.where(low_q, o[:NA_CHAIN], o[NA_CHAIN:]).astype(jnp.bfloat16)
        for j in range(NA_GROUP_ROWS):
            o_ref[query_rows(g, half, j), :] = o[j * NA_HALF_COLS:(j + 1) * NA_HALF_COLS]


def _natten_call(q, k, v, k_ctx, v_ctx, bias, w_gate, w_up, w_down, layer):
    n = q.shape[0]
    grid_rows = n // GRID_W
    assert grid_rows % NA_ROWS_PER_BLOCK == 0 and grid_rows >= 2 * NA_ROWS_PER_BLOCK
    nblk = n // NA_BLOCK
    steps = HEAD_PAIRS * nblk
    depth = w_gate.shape[0]
    up_rows = N_EXPERTS * D_MODEL
    down_rows = N_EXPERTS * D_EXPERT
    assert up_rows % steps == 0 and down_rows % steps == 0
    wg2 = w_gate.reshape(depth * up_rows, D_EXPERT)
    wu2 = w_up.reshape(depth * up_rows, D_EXPERT)
    wd2 = w_down.reshape(depth * down_rows, D_MODEL)
    up_in = pl.BlockSpec((up_rows // steps, D_EXPERT), lambda p, b: (layer * steps + p * nblk + b, 0))
    down_in = pl.BlockSpec((down_rows // steps, D_MODEL), lambda p, b: (layer * steps + p * nblk + b, 0))
    up_out = pl.BlockSpec((up_rows // steps, D_EXPERT), lambda p, b: (p * nblk + b, 0))
    down_out = pl.BlockSpec((down_rows // steps, D_MODEL), lambda p, b: (p * nblk + b, 0))
    n_halo = n // N_STRIPS // NA_STRIP_HALO
    hb = NA_STRIP_BLOCK // NA_STRIP_HALO
    rows = pl.BlockSpec((NA_BLOCK, LANES), lambda p, b: (b, p))
    cur = pl.BlockSpec((N_STRIPS, NA_STRIP_BLOCK, LANES), lambda p, b: (0, b, p))
    prev = pl.BlockSpec((N_STRIPS, NA_STRIP_HALO, LANES), lambda p, b: (0, jnp.maximum(b * hb - 1, 0), p))
    nxt = pl.BlockSpec((N_STRIPS, NA_STRIP_HALO, LANES),
                       lambda p, b: (0, jnp.minimum((b + 1) * hb, n_halo - 1), p))
    ctx = pl.BlockSpec((k_ctx.shape[0], LANES), lambda p, b: (0, p))
    win_rows = NA_STRIP_BLOCK + 2 * NA_STRIP_HALO
    att, wg_bf, wu_bf, wd_bf = pl.pallas_call(
        functools.partial(_natten_kernel, grid_rows=grid_rows),
        grid=(HEAD_PAIRS, nblk),
        in_specs=[rows, prev, cur, nxt, prev, cur, nxt, ctx, ctx,
                  pl.BlockSpec((None, None, 3, 2, 2 * NA_CHAIN, NA_LOCAL), lambda p, b: (layer, p, 0, 0, 0, 0)),
                  up_in, up_in, down_in],
        out_specs=[rows, up_out, up_out, down_out],
        out_shape=[jax.ShapeDtypeStruct((n, NA_DIM), jnp.bfloat16),
                   jax.ShapeDtypeStruct((up_rows, D_EXPERT), jnp.bfloat16),
                   jax.ShapeDtypeStruct((up_rows, D_EXPERT), jnp.bfloat16),
                   jax.ShapeDtypeStruct((down_rows, D_MODEL), jnp.bfloat16)],
        scratch_shapes=[pltpu.VMEM((N_STRIPS, win_rows, LANES), jnp.bfloat16),
                        pltpu.VMEM((N_STRIPS, win_rows, 2 * LANES), jnp.bfloat16),
                        pltpu.VMEM((k_ctx.shape[0], 2 * LANES), jnp.bfloat16)],
        compiler_params=_cparams(("arbitrary", "arbitrary")),
        name="natten",
    )(q, k, k, k, v, v, v, k_ctx, v_ctx, bias, wg2, wu2, wd2)
    return (att, wg_bf.reshape(N_EXPERTS, D_MODEL, D_EXPERT), wu_bf.reshape(N_EXPERTS, D_MODEL, D_EXPERT),
            wd_bf.reshape(N_EXPERTS, D_EXPERT, D_MODEL))


def _natten_bias_inputs(rpb):
    depth = rpb.shape[0]
    cols = np.arange(GRID_W)
    col_start = np.clip(cols - NA_WIN_COLS // 2, 0, GRID_W - NA_WIN_COLS)
    kc = np.arange(GRID_W)
    in_win = (kc[None, :] >= col_start[:, None]) & (kc[None, :] < col_start[:, None] + NA_WIN_COLS)
    dc = kc[None, :] - cols[:, None] + NA_WIN_COLS - 1
    sel = (np.arange(2 * NA_WIN_COLS - 1)[:, None, None] == dc[None]) & in_win[None]
    t2 = jnp.einsum("lhdj,jqk->lhdqk", rpb, jnp.asarray(sel, jnp.float32), precision=lax.Precision.HIGHEST)
    t2 = jnp.where(in_win[None, None, None], t2 * LOG2_E, NEG_BIG)
    neg = jnp.full((depth, NA_HEADS, 1, GRID_W, GRID_W), NEG_BIG, jnp.float32)
    t2e = jnp.concatenate([neg, t2, neg], axis=2)
    u = jnp.concatenate([t2e[:, :, :-1], t2e[:, :, 1:]], axis=-1)
    u = u.reshape(depth, HEAD_PAIRS, 2, 2 * NA_WIN_ROWS, GRID_W, LANES)

    place = np.zeros((2, NA_WINDOW_ROWS * GRID_W, NA_LOCAL), np.float32)
    for half, c0 in enumerate(NA_HALF_COL0):
        for a in range(NA_WINDOW_ROWS):
            for kcol in range(c0, c0 + NA_HALF_STRIPS * STRIP_W):
                s, c8 = divmod(kcol - c0, STRIP_W)
                place[half, a * GRID_W + kcol, s * NA_RUN + a * STRIP_W + c8] = 1.0
    outside = np.full((3, NA_GROUP_ROWS, NA_LOCAL), NEG_BIG, np.float32)
    for edge in (NA_EDGE_FIRST, NA_EDGE_NONE, NA_EDGE_LAST):
        for j in range(NA_GROUP_ROWS):
            lo, _ = _window_rows(edge, j)
            for s in range(NA_HALF_STRIPS):
                outside[edge, j, s * NA_RUN + lo * STRIP_W:s * NA_RUN + (lo + NA_WIN_ROWS) * STRIP_W] = 0.0
    return u, jnp.asarray(place, jnp.bfloat16), jnp.asarray(outside)


def _window_rows(edge, j):
    if edge == NA_EDGE_FIRST:
        return 0, NA_WIN_ROWS - 1 - j
    if edge == NA_EDGE_NONE:
        return j, NA_WIN_ROWS // 2 - 1
    return NA_WINDOW_ROWS - NA_WIN_ROWS, NA_WIN_ROWS // 2 - 1 - j


def _bias_expand_kernel(u_ref, place_ref, outside_ref, o_ref):
    low = lax.broadcasted_iota(jnp.int32, (NA_HALF_COLS, LANES), 1) < GRID_W
    zero = jnp.zeros((NA_HALF_COLS, LANES), jnp.float32)
    for edge in (NA_EDGE_FIRST, NA_EDGE_NONE, NA_EDGE_LAST):
        for half in range(2):
            q0 = half * NA_HALF_COLS
            blocks, masks = [], []
            for hd in range(2):
                for j in range(NA_GROUP_ROWS):
                    lo, base = _window_rows(edge, j)
                    tiles = []
                    for i in range(NA_WINDOW_ROWS // 2):
                        a0, a1 = 2 * i, 2 * i + 1
                        ok0 = lo <= a0 < lo + NA_WIN_ROWS
                        ok1 = lo <= a1 < lo + NA_WIN_ROWS
                        if not (ok0 or ok1):
                            tile = zero
                        else:
                            tile = u_ref[hd, base + a1 - lo, q0:q0 + NA_HALF_COLS, :]
                            if not ok0:
                                tile = jnp.where(low, zero, tile)
                            if not ok1:
                                tile = jnp.where(low, tile, zero)
                        tiles.append(tile)
                    blocks.append(jnp.concatenate(tiles, axis=-1))
                    masks.append(jnp.broadcast_to(outside_ref[edge, j:j + 1, :], (NA_HALF_COLS, NA_LOCAL)))
            lhs = jnp.concatenate(blocks, axis=0).astype(jnp.bfloat16)
            placed = _dot(lhs, place_ref[half]) + jnp.concatenate(masks, axis=0)
            o_ref[edge, half] = placed.astype(o_ref.dtype)


def _ctxatt_kernel(q_ref, k_ref, v_ref, o_ref):
    lc = q_ref.shape[0]
    low = lax.broadcasted_iota(jnp.int32, (lc, LANES), 1) < HEAD_DIM
    lhs = _stack_heads(q_ref[...], low)
    s = _dot_nt(lhs, k_ref[...])
    m = jnp.max(s, axis=-1, keepdims=True)
    p = jnp.exp2(s - m)
    denom = jnp.sum(p, axis=-1, keepdims=True)
    o = _dot(p.astype(jnp.bfloat16), v_ref[...]) * (1.0 / denom)
    o_ref[...] = jnp.where(low, o[:lc], o[lc:]).astype(jnp.bfloat16)


def _ctxatt_call(q, k, v):
    lc = q.shape[0]
    spec = pl.BlockSpec((lc, LANES), lambda p: (0, p))
    return pl.pallas_call(
        _ctxatt_kernel,
        grid=(HEAD_PAIRS,),
        in_specs=[spec, spec, spec],
        out_specs=spec,
        out_shape=jax.ShapeDtypeStruct((lc, NA_DIM), jnp.bfloat16),
        compiler_params=_cparams(("arbitrary",)),
        name="ctxatt",
    )(q, k, v)


OUTPROJ_CHAIN = 256
OUTPROJ_TILE = 1024
INPROJ_TILE = 1024
RESIDUAL_TILE = 2048


def _outproj_kernel(h_ref, mp_ref, att_ref, sg_ref, wo_ref, g1_ref, n2_ref, sh_ref, sc_ref,
                    wr_ref, br_ref, h1_ref, pay_ref, route_ref):
    chains = [slice(c * OUTPROJ_CHAIN, (c + 1) * OUTPROJ_CHAIN) for c in range(h_ref.shape[0] // OUTPROJ_CHAIN)]
    h1s = []
    for rows in chains:
        mix = (_dot(mp_ref[rows, :], wo_ref[0:POOL_DIM, :])
               + _dot(att_ref[rows, :], wo_ref[POOL_DIM:POOL_DIM + NA_DIM, :])
               + _dot(sg_ref[rows, :], wo_ref[POOL_DIM + NA_DIM:, :]))
        h1 = h_ref[rows, :].astype(jnp.float32) + g1_ref[...] * mix
        h1_ref[rows, :] = h1.astype(h1_ref.dtype)
        h1s.append(h1)
    for rows, h1 in zip(chains, h1s):
        _outproj_route(rows, h1, n2_ref, sh_ref, sc_ref, wr_ref, br_ref, pay_ref, route_ref)


def _outproj_route(rows, h1, n2_ref, sh_ref, sc_ref, wr_ref, br_ref, pay_ref, route_ref):
    tm = OUTPROJ_CHAIN
    ms = jnp.mean(h1 * h1, axis=-1, keepdims=True)
    hm = (h1 * lax.rsqrt(ms + EPS)) * (n2_ref[...] * (1.0 + sc_ref[...])) + sh_ref[...]
    pay_ref[rows, 0:HALF_D] = _pack_halves(hm)

    hm_hi = hm.astype(jnp.bfloat16)
    lt = _dot_nt(wr_ref[...], hm_hi)
    logits = lt[:N_EXPERTS] + lt[N_EXPERTS:] + br_ref[...]
    e = jnp.exp(logits - jnp.max(logits, axis=0, keepdims=True))

    best = ga = gb = e1 = e2 = cls = None
    for c in range(N_CLASSES):
        a, b2 = int(CLASS_E1[c]), int(CLASS_E2[c])
        ea, eb = e[a:a + 1, :], e[b2:b2 + 1, :]
        s = ea + eb
        if best is None:
            best, ga, gb = s, ea, eb
            e1 = jnp.full_like(s, float(a))
            e2 = jnp.full_like(s, float(b2))
            cls = jnp.zeros_like(s)
        else:
            better = s > best
            best = jnp.where(better, s, best)
            ga = jnp.where(better, ea, ga)
            gb = jnp.where(better, eb, gb)
            e1 = jnp.where(better, float(a), e1)
            e2 = jnp.where(better, float(b2), e2)
            cls = jnp.where(better, float(c), cls)
    inv = 1.0 / best
    row = lax.broadcasted_iota(jnp.int32, (ROUTE_ROWS, tm), 0)
    rec = jnp.where(row == 0, ga * inv,
          jnp.where(row == 1, gb * inv,
          jnp.where(row == 2, e1,
          jnp.where(row == 3, e2,
          jnp.where(row == 4, cls, 0.0)))))
    route_ref[:, rows] = rec
    wide = jnp.concatenate([rec, jnp.zeros((LANES - ROUTE_ROWS, tm), jnp.float32)], axis=0)
    pay_ref[rows, HALF_D:] = pltpu.bitcast(wide.T, jnp.uint32)


def _outproj_call(h, mp, att, sg, w_out, layer, g1, n2, sh2, sc2, wr, br, tm):
    n = h.shape[0]
    row = lambda i: (i, 0)
    fixed = lambda i: (0, 0)
    vec = pl.BlockSpec((1, D_MODEL), fixed)
    return pl.pallas_call(
        _outproj_kernel,
        grid=(n // tm,),
        in_specs=[
            pl.BlockSpec((tm, D_MODEL), row),
            pl.BlockSpec((tm, POOL_DIM), row),
            pl.BlockSpec((tm, NA_DIM), row),
            pl.BlockSpec((tm, SG_DIM), row),
            pl.BlockSpec((None, D_MODEL, D_MODEL), lambda i: (layer, 0, 0)),
            vec, vec, vec, vec,
            pl.BlockSpec((2 * N_EXPERTS, D_MODEL), fixed),
            pl.BlockSpec((N_EXPERTS, 1), fixed),
        ],
        out_specs=[
            pl.BlockSpec((tm, D_MODEL), row),
            pl.BlockSpec((tm, PAYLOAD_W), row),
            pl.BlockSpec((ROUTE_ROWS, tm), lambda i: (0, i)),
        ],
        out_shape=[
            jax.ShapeDtypeStruct((n, D_MODEL), STREAM_DTYPE),
            jax.ShapeDtypeStruct((n, PAYLOAD_W), jnp.uint32),
            jax.ShapeDtypeStruct((ROUTE_ROWS, n), jnp.float32),
        ],
        compiler_params=_cparams(("arbitrary",)),
        name="outproj",
    )(h, mp, att, sg, w_out, g1, n2, sh2, sc2, wr, br)


def _payload_parts(pay_ref, rows=slice(None)):
    lo, hi = _unpack_halves(pay_ref[rows, 0:HALF_D])
    x = jnp.concatenate([lo, hi], axis=-1).astype(jnp.bfloat16)
    return x, pltpu.bitcast(pay_ref[rows, HALF_D:], jnp.float32)


def _expert_pair(x, ga, gb, wga, wua, wda, wgb, wub, wdb):
    ha = (_silu(_dot(x, wga)) * _dot(x, wua) * ga).astype(jnp.bfloat16)
    hb = (_silu(_dot(x, wgb)) * _dot(x, wub) * gb).astype(jnp.bfloat16)
    return _dot(ha, wda) + _dot(hb, wdb)


MOE_QUARTER = MOE_TM // 4


def _moe_sorted_kernel(blk_ref, e1_ref, e2_ref, quarters_ref, pay_ref,
                       wga_ref, wua_ref, wda_ref, wgb_ref, wub_ref, wdb_ref, o_ref):
    quarters = quarters_ref[pl.program_id(0)]

    def run(rows):
        x, route = _payload_parts(pay_ref, rows)
        y = _expert_pair(x, route[:, 0:1], route[:, 1:2], wga_ref[...], wua_ref[...], wda_ref[...],
                         wgb_ref[...], wub_ref[...], wdb_ref[...])
        o_ref[rows, :] = _pack_halves(y)

    for used in range(1, MOE_TM // MOE_QUARTER + 1):
        @pl.when(quarters == used)
        def _(used=used):
            run(slice(0, used * MOE_QUARTER))


def _moe_sorted_call(blk, e1, e2, quarters, pay_sorted, wg, wu, wd):
    n_tiles = blk.shape[0]
    rows = lambda i, blk, e1, e2, quarters: (blk[i], 0)
    wa = lambda i, blk, e1, e2, quarters: (e1[i], 0, 0)
    wb = lambda i, blk, e1, e2, quarters: (e2[i], 0, 0)
    up = lambda m: pl.BlockSpec((None, D_MODEL, D_EXPERT), m)
    down = lambda m: pl.BlockSpec((None, D_EXPERT, D_MODEL), m)
    return pl.pallas_call(
        _moe_sorted_kernel,
        grid_spec=pltpu.PrefetchScalarGridSpec(
            num_scalar_prefetch=4,
            grid=(n_tiles,),
            in_specs=[pl.BlockSpec((MOE_TM, PAYLOAD_W), rows),
                      up(wa), up(wa), down(wa), up(wb), up(wb), down(wb)],
            out_specs=pl.BlockSpec((MOE_TM, HALF_D), rows),
        ),
        out_shape=jax.ShapeDtypeStruct((n_tiles * MOE_TM, HALF_D), jnp.uint32),
        compiler_params=_cparams(("arbitrary",)),
        name="moe_sorted",
    )(blk, e1, e2, quarters, pay_sorted, wg, wu, wd, wg, wu, wd)


DENSE_EXPERTS_PER_STEP = GROUP_SIZE


def _moe_dense_kernel(*refs, first_step, resume):
    if resume:
        pay_ref, wg_ref, wu_ref, wd_ref, acc_ref, o_ref = refs
    else:
        pay_ref, wg_ref, wu_ref, wd_ref, o_ref = refs
    step = pl.program_id(0)

    @pl.when(step == 0)
    def _():
        o_ref[...] = acc_ref[...] if resume else jnp.zeros_like(o_ref)

    x, route = _payload_parts(pay_ref)
    y = None
    for k in range(DENSE_EXPERTS_PER_STEP):
        ef = ((first_step + step) * DENSE_EXPERTS_PER_STEP + k).astype(jnp.float32)
        gate = (jnp.where(route[:, 2:3] == ef, route[:, 0:1], 0.0)
                + jnp.where(route[:, 3:4] == ef, route[:, 1:2], 0.0))
        he = (_silu(_dot(x, wg_ref[k])) * _dot(x, wu_ref[k]) * gate).astype(jnp.bfloat16)
        yk = _dot(he, wd_ref[k])
        y = yk if y is None else y + yk
    o_ref[...] += y


def _moe_dense_call(pay, wg, wu, wd, first_step, n_steps, acc=None):
    n = pay.shape[0]
    per = DENSE_EXPERTS_PER_STEP
    whole = pl.BlockSpec((n, D_MODEL), lambda e: (0, 0))
    return pl.pallas_call(
        functools.partial(_moe_dense_kernel, first_step=first_step, resume=acc is not None),
        grid=(n_steps,),
        in_specs=[pl.BlockSpec((n, PAYLOAD_W), lambda e: (0, 0)),
                  pl.BlockSpec((per, D_MODEL, D_EXPERT), lambda e: (first_step + e, 0, 0)),
                  pl.BlockSpec((per, D_MODEL, D_EXPERT), lambda e: (first_step + e, 0, 0)),
                  pl.BlockSpec((per, D_EXPERT, D_MODEL), lambda e: (first_step + e, 0, 0))]
                 + ([whole] if acc is not None else []),
        out_specs=whole,
        out_shape=jax.ShapeDtypeStruct((n, D_MODEL), jnp.float32),
        compiler_params=_cparams(("arbitrary",)),
        name="moe_dense",
    )(pay, wg, wu, wd, *([acc] if acc is not None else []))


def _moe_out(y_ref):
    if y_ref.dtype == jnp.uint32:
        return jnp.concatenate(_unpack_halves(y_ref[...]), axis=-1)
    return y_ref[...]


def _residual_kernel(h_ref, y_ref, g_ref, o_ref):
    o_ref[...] = h_ref[...].astype(jnp.float32) + g_ref[...] * _moe_out(y_ref)


def _residual_call(h, y, g, tm):
    n = h.shape[0]
    row = pl.BlockSpec((tm, D_MODEL), lambda i: (i, 0))
    return pl.pallas_call(
        _residual_kernel,
        grid=(n // tm,),
        in_specs=[row, pl.BlockSpec((tm, y.shape[1]), lambda i: (i, 0)),
                  pl.BlockSpec((1, D_MODEL), lambda i: (0, 0))],
        out_specs=row,
        out_shape=jax.ShapeDtypeStruct((n, D_MODEL), jnp.float32),
        compiler_params=_cparams(("arbitrary",)),
        name="residual",
    )(h, y, g)


SC_ROWS = 128


SC_CORES = 2
SC_SUBCORES = 16
SC_WORKERS = SC_CORES * SC_SUBCORES


def _sc_mesh():
    return plsc.VectorSubcoreMesh(core_axis_name="core", subcore_axis_name="subcore")


def _sc_worker():
    return lax.axis_index("subcore") * SC_CORES + lax.axis_index("core")


def _scatter_rows(x, dest, n_out):
    n, w = x.shape

    per_worker = n // SC_WORKERS
    assert per_worker % SC_ROWS == 0

    @functools.partial(pl.kernel, out_type=jax.ShapeDtypeStruct((n_out, w), x.dtype), mesh=_sc_mesh(),
                       scratch_types=[pltpu.VMEM((SC_ROWS,), jnp.int32), pltpu.VMEM((SC_ROWS, w), x.dtype)])
    def scatter(x_hbm, i_hbm, o_hbm, idx_v, rows_v):
        first = _sc_worker() * per_worker

        @pl.loop(0, per_worker // SC_ROWS)
        def _(i):
            base = pl.multiple_of(first + i * SC_ROWS, SC_ROWS)
            pltpu.sync_copy(i_hbm.at[pl.ds(base, SC_ROWS)], idx_v)
            pltpu.sync_copy(x_hbm.at[pl.ds(base, SC_ROWS)], rows_v)
            pltpu.sync_copy(rows_v, o_hbm.at[idx_v])

    return scatter(x, dest)


def _gather_rows(x, src):
    n = src.shape[0]
    w = x.shape[1]

    per_worker = n // SC_WORKERS
    assert per_worker % SC_ROWS == 0

    @functools.partial(pl.kernel, out_type=jax.ShapeDtypeStruct((n, w), x.dtype), mesh=_sc_mesh(),
                       scratch_types=[pltpu.VMEM((SC_ROWS,), jnp.int32), pltpu.VMEM((SC_ROWS, w), x.dtype)])
    def gather(x_hbm, i_hbm, o_hbm, idx_v, rows_v):
        first = _sc_worker() * per_worker

        @pl.loop(0, per_worker // SC_ROWS)
        def _(i):
            base = pl.multiple_of(first + i * SC_ROWS, SC_ROWS)
            pltpu.sync_copy(i_hbm.at[pl.ds(base, SC_ROWS)], idx_v)
            pltpu.sync_copy(x_hbm.at[idx_v], rows_v)
            pltpu.sync_copy(rows_v, o_hbm.at[pl.ds(base, SC_ROWS)])

    return gather(x, src)


def _routing_plan(cls, n_tiles):
    onehot = (cls[:, None] == jnp.arange(N_CLASSES, dtype=jnp.int32)[None, :]).astype(jnp.int32)
    counts = jnp.sum(onehot, axis=0)
    rank = jnp.sum((jnp.cumsum(onehot, axis=0) - onehot) * onehot, axis=1)
    tiles = (counts + MOE_TM - 1) // MOE_TM
    tile_end = jnp.cumsum(tiles)
    tile_start = tile_end - tiles
    dest = jnp.sum(onehot * tile_start[None, :], axis=1) * MOE_TM + rank
    nact = tile_end[-1]
    tile = jnp.arange(n_tiles, dtype=jnp.int32)
    blk = jnp.minimum(tile, nact - 1)
    tile_cls = jnp.sum((blk[:, None] >= tile_end[None, :]).astype(jnp.int32), axis=1)
    e1 = jnp.asarray(CLASS_E1)[tile_cls]
    e2 = jnp.asarray(CLASS_E2)[tile_cls]
    valid = counts[tile_cls] - (blk - tile_start[tile_cls]) * MOE_TM
    quarters = jnp.where(tile >= nact, 0, (jnp.minimum(valid, MOE_TM) + MOE_QUARTER - 1) // MOE_QUARTER)
    return dest.astype(jnp.int32), blk, e1, e2, quarters.astype(jnp.int32)


def _row_tile(n, prefer=512):
    return next(t for t in (prefer, 512, 256) if n % t == 0)


def kernel(x, c, ctx, c_ctx, w_ada, b_ada, norm1, w_in, pool_w, pool_scale, q_norm, k_norm, rpb,
           sg_w, sg_b, sg_norm, w_out, norm2, w_router, b_router, w_gate, w_up, w_down):
    depth = w_ada.shape[0]
    n = x.shape[1]
    lc = ctx.shape[1]
    bf = jnp.bfloat16
    lat_stream = (x[0],)
    h_ctx = ctx[0]

    cond = jnp.stack([c[0], c_ctx], axis=1)
    mod, bias_tables = _prep_call(cond, w_ada, b_ada, rpb)

    wr_t = w_router.T
    wr_hi = wr_t.astype(bf)
    wr_lo = (wr_t - wr_hi.astype(jnp.float32)).astype(bf)
    wr = jnp.concatenate([wr_hi, wr_lo], axis=0)
    br = b_router.reshape(N_EXPERTS, 1)
    n_tiles = n // MOE_TM + N_CLASSES
    w_in_bf = w_in.astype(bf)
    w_out_bf = w_out.astype(bf)

    for l in range(depth):
        last = l == depth - 1
        qg =(q_norm[l] * (HEAD_DIM ** -0.5 * LOG2_E)).reshape(1, NA_DIM)
        kg = k_norm[l].reshape(1, NA_DIM)
        sgn = sg_norm[l].reshape(1, SG_DIM)
        sgw = sg_w[l].astype(bf).reshape(SG_DIM // LANES, 2 * SG_CHUNK, SG_CHUNK)
        sgb = jnp.broadcast_to(sg_b[l].reshape(SG_DIM // LANES, 2 * SG_CHUNK, 1),
                               (SG_DIM // LANES, 2 * SG_CHUNK, LANES))
        pool_bd = jax.scipy.linalg.block_diag(*[pool_w[l, g] for g in range(len(POOL_WINDOWS))]).astype(bf)
        pscale = pool_scale[l].reshape(1, POOL_DIM)
        n1 = norm1[l].reshape(1, D_MODEL)
        n2 = norm2[l].reshape(1, D_MODEL)

        def mods(row):
            return [mod[l, row:row + 1, i * D_MODEL:(i + 1) * D_MODEL] for i in range(6)]

        sh1, sc1, g1, sh2, sc2, g2 = mods(0)
        csh1, csc1, cg1, csh2, csc2, cg2 = mods(1)

        tc = _row_tile(lc)
        mix_pool_c, q_c, k_c, v_c, sg_c = _inproj_call((h_ctx,), n1, csh1, csc1, w_in_bf, l, qg, kg,
                                                       sgn, sgw, sgb, pool_bd, pscale, tc, strips=False)

        outs = _inproj_call(lat_stream, n1, sh1, sc1, w_in_bf, l, qg, kg, sgn, sgw, sgb,
                            pool_bd, pscale, _row_tile(n, INPROJ_TILE), strips=True)
        mix_pool, q, k, v, sg = outs[:5]
        h_lat = outs[5] if len(lat_stream) == 3 else lat_stream[0]
        att, wg_l, wu_l, wd_l = _natten_call(q, k, v, k_c, v_c, bias_tables, w_gate, w_up, w_down, l)
        h1, pay, route = _outproj_call(h_lat, mix_pool, att, sg, w_out_bf, l, g1, n2, sh2, sc2, wr, br,
                                       _row_tile(n, OUTPROJ_TILE))
        cls = route[4].astype(jnp.int32)
        dest, blk, e1, e2, quarters = _routing_plan(cls, n_tiles)
        pay_sorted = _scatter_rows(pay, dest, n_tiles * MOE_TM)

        if not last:
            dense_steps = N_EXPERTS // DENSE_EXPERTS_PER_STEP
            att_c = _ctxatt_call(q_c, k_c, v_c)
            h1_c, pay_c, _ = _outproj_call(h_ctx, mix_pool_c, att_c, sg_c, w_out_bf, l, cg1, n2, csh2, csc2,
                                           wr, br, tc)
            y_c = _moe_dense_call(pay_c, wg_l, wu_l, wd_l, 0, dense_steps // 2)
            quarters, y_c = lax.optimization_barrier((quarters, y_c))

        y_sorted = _moe_sorted_call(blk, e1, e2, quarters, pay_sorted, wg_l, wu_l, wd_l)
        y = _gather_rows(y_sorted, dest)
        lat_stream = (h1, y, g2)

        if not last:
            y_c = _moe_dense_call(pay_c, wg_l, wu_l, wd_l, dense_steps // 2, dense_steps - dense_steps // 2,
                                  acc=y_c)
            h_ctx = _residual_call(h1_c, y_c, cg2, tc)

    return _residual_call(*lat_stream, _row_tile(n, RESIDUAL_TILE))[None]
```

```python
import functools

import jax
import jax.numpy as jnp
import numpy as np
from jax import lax
from jax.experimental import pallas as pl
from jax.experimental.pallas import tpu as pltpu
from jax.experimental.pallas import tpu_sc as plsc

D_MODEL = 1024
GRID_W = 64
HEAD_DIM = 64
POOL_WINDOWS = (2, 4, 8, 16)
POOL_DIM = 256
NA_HEADS = 8
NA_DIM = 512
NA_WIN_ROWS = 8
NA_WIN_COLS = 16
SG_DIM = 256
SG_CHUNK = 128
Q_OFF = POOL_DIM
K_OFF = Q_OFF + NA_DIM
V_OFF = K_OFF + NA_DIM
U_OFF = V_OFF + NA_DIM
G_OFF = U_OFF + SG_DIM
IN_DIM = G_OFF + SG_DIM
N_EXPERTS = 16
GROUP_SIZE = 4
D_EXPERT = 512
EPS = 1e-6

LANES = 128
SUBLANES = 8
HEAD_PAIRS = NA_DIM // LANES
VMEM_LIMIT = 48 * 1024 * 1024

PAIRS = ((0, 1), (0, 2), (1, 2), (1, 3), (0, 3), (2, 3))
N_CLASSES = (N_EXPERTS // GROUP_SIZE) * len(PAIRS)
CLASS_E1 = np.array([4 * g + i for g in range(4) for (i, j) in PAIRS], np.int32)
CLASS_E2 = np.array([4 * g + j for g in range(4) for (i, j) in PAIRS], np.int32)

ROUTE_ROWS = 8
HALF_D = D_MODEL // 2
PAYLOAD_W = HALF_D + LANES
MOE_TM = 512
NEG_BIG = -1e30
LOG2_E = 1.4426950408889634


def _cparams(sem):
    return pltpu.CompilerParams(dimension_semantics=sem, vmem_limit_bytes=VMEM_LIMIT)


def _dot(a, b):
    return jnp.dot(a, b, preferred_element_type=jnp.float32)


def _dot_nt(a, b):
    return lax.dot_general(a, b, (((1,), (1,)), ((), ())), preferred_element_type=jnp.float32)


def _gelu_tanh(x):
    return 0.5 * x * (1.0 + jnp.tanh(0.7978845608028654 * (x + 0.044715 * (x * x * x))))


def _silu(x):
    return x * (1.0 / (1.0 + jnp.exp(-x)))


def _pack_halves(x):
    w = x.shape[1] // 2
    lo = pltpu.bitcast(x[:, :w].astype(jnp.bfloat16).astype(jnp.float32), jnp.uint32) >> 16
    hi = pltpu.bitcast(x[:, w:].astype(jnp.bfloat16).astype(jnp.float32), jnp.uint32) & jnp.uint32(0xFFFF0000)
    return lo | hi


def _unpack_halves(words):
    lo = pltpu.bitcast(words << 16, jnp.float32)
    hi = pltpu.bitcast(words & jnp.uint32(0xFFFF0000), jnp.float32)
    return lo, hi


def _ada_kernel(cond_ref, w_ref, b_ref, o_ref):
    cond = _silu(cond_ref[...])
    w = w_ref[...]
    rows = [jnp.sum(w * cond[:, r:r + 1], axis=0, keepdims=True) + b_ref[...] for r in range(2)]
    o_ref[...] = jnp.concatenate(rows + [jnp.zeros((SUBLANES - 2, w.shape[1]), jnp.float32)], axis=0)


def _ada_call(cond, w_ada, b_ada):
    depth = w_ada.shape[0]
    tn = 1536
    return pl.pallas_call(
        _ada_kernel,
        grid=(depth, 6 * D_MODEL // tn),
        in_specs=[
            pl.BlockSpec((D_MODEL, 2), lambda l, j: (0, 0)),
            pl.BlockSpec((None, D_MODEL, tn), lambda l, j: (l, 0, j)),
            pl.BlockSpec((None, 1, tn), lambda l, j: (l, 0, j)),
        ],
        out_specs=pl.BlockSpec((None, SUBLANES, tn), lambda l, j: (l, 0, j)),
        out_shape=jax.ShapeDtypeStruct((depth, SUBLANES, 6 * D_MODEL), jnp.float32),
        compiler_params=_cparams(("arbitrary", "arbitrary")),
        name="adaln",
    )(cond, w_ada, b_ada.reshape(depth, 1, 6 * D_MODEL))


def _norm_modulate(x, n_ref, sh_ref, sc_ref):
    ms = jnp.mean(x * x, axis=-1, keepdims=True)
    return ((x * lax.rsqrt(ms + EPS)) * (n_ref[...] * (1.0 + sc_ref[...])) + sh_ref[...]).astype(jnp.bfloat16)


def _head_rms_scale(a):
    low = lax.broadcasted_iota(jnp.int32, (a.shape[0], LANES), 1) < HEAD_DIM
    blocks = []
    for p in range(a.shape[1] // LANES):
        sq = jnp.square(a[:, p * LANES:(p + 1) * LANES])
        s_lo = jnp.sum(jnp.where(low, sq, 0.0), axis=-1, keepdims=True)
        s_hi = jnp.sum(jnp.where(low, 0.0, sq), axis=-1, keepdims=True)
        blocks.append(jnp.where(low, lax.rsqrt(s_lo * (1.0 / HEAD_DIM) + EPS),
                                lax.rsqrt(s_hi * (1.0 / HEAD_DIM) + EPS)))
    return jnp.concatenate(blocks, axis=-1)


STRIP_W = 8
N_STRIPS = GRID_W // STRIP_W


def _store_keys(ref, x):
    if len(ref.shape) == 2:
        ref[...] = x.astype(jnp.bfloat16)
        return
    pair = 2 * STRIP_W
    for s in range(N_STRIPS):
        for rp in range(x.shape[0] // (2 * GRID_W)):
            top = 2 * rp * GRID_W + s * STRIP_W
            rows = jnp.concatenate([x[top:top + STRIP_W], x[top + GRID_W:top + GRID_W + STRIP_W]], axis=0)
            ref[s, rp * pair:(rp + 1) * pair, :] = rows.astype(jnp.bfloat16)


POOL_HALO = 8
HALO_BLOCK = 16
STREAM_DTYPE = jnp.bfloat16


POOL_EDGE = 16


def _pool_mix(xe_ref, w_ref, scale_ref, tm, seq_len):
    low = lax.broadcasted_iota(jnp.int32, (tm, LANES), 1) < HEAD_DIM
    t_edge = pl.program_id(0) * tm + lax.broadcasted_iota(jnp.int32, (POOL_EDGE, LANES), 0)

    def window_mean(s, half):
        mean = s * (1.0 / (2 * half))

        def clip_fix(t):
            count = (jnp.minimum(t + half, seq_len) - jnp.maximum(t - half, 0)).astype(jnp.float32)
            return (2.0 * half) / count

        return jnp.concatenate([mean[:POOL_EDGE] * clip_fix(t_edge),
                                mean[POOL_EDGE:tm - POOL_EDGE],
                                mean[tm - POOL_EDGE:] * clip_fix(t_edge + (tm - POOL_EDGE))], axis=0)

    def window_sums(xs, n_levels):
        sums = []
        s = xs
        for k in range(n_levels):
            step = 1 << k
            s = s[:-step] + s[step:]
            sums.append(s)
        return sums

    outs = []
    for half_block, windows in enumerate(((2, 4), (8, 16))):
        xs = xe_ref[:, half_block * LANES:(half_block + 1) * LANES]
        sums = window_sums(xs, int(np.log2(windows[1])))
        parts = []
        for w in windows:
            half = w // 2
            s = sums[int(np.log2(w)) - 1][POOL_HALO - half:POOL_HALO - half + tm]
            parts.append(window_mean(s, half))
        mean = jnp.where(low, parts[0], parts[1])
        outs.append(mean - xs[POOL_HALO:POOL_HALO + tm])
    d = jnp.concatenate(outs, axis=-1).astype(jnp.bfloat16)
    return (_dot(d, w_ref[...]) * scale_ref[...]).astype(jnp.bfloat16)


def _inproj_kernel(*refs, pending, seq_len):
    n_stream = 7 if pending else 3
    stream, refs = refs[:n_stream], refs[n_stream:]
    if pending:
        h_ref, y_ref, g_ref, hp_ref, hn_ref, yp_ref, yn_ref = stream
        hres_ref, refs = refs[-2], refs[:-2] + refs[-1:]
        x = h_ref[...].astype(jnp.float32) + g_ref[...] * _moe_out(y_ref)
        hres_ref[...] = x.astype(hres_ref.dtype)
        before = hp_ref[...].astype(jnp.float32) + g_ref[...] * _moe_out(yp_ref)
        after = hn_ref[...].astype(jnp.float32) + g_ref[...] * _moe_out(yn_ref)
    else:
        h_ref, hp_ref, hn_ref = stream
        x = h_ref[...].astype(jnp.float32)
        before = hp_ref[...].astype(jnp.float32)
        after = hn_ref[...].astype(jnp.float32)
    x_halo = jnp.concatenate([before[HALO_BLOCK - POOL_HALO:], after[:POOL_HALO]], axis=0)
    (n1_ref, sh_ref, sc_ref, w_ref, qg_ref, kg_ref, sgn_ref, sgw_ref, sgb_ref,
     pw_ref, ps_ref, pool_ref, q_ref, k_ref, v_ref, sg_ref, xe_ref) = refs
    tm = h_ref.shape[0]
    i = pl.program_id(0)
    hn = _norm_modulate(x, n1_ref, sh_ref, sc_ref)
    hn_halo = _norm_modulate(x_halo, n1_ref, sh_ref, sc_ref)

    a_halo = _dot(hn_halo, w_ref[:, 0:Q_OFF])
    xe_ref[0:POOL_HALO, :] = jnp.where(i > 0, a_halo[:POOL_HALO], 0.0)
    xe_ref[POOL_HALO:POOL_HALO + tm, :] = _dot(hn, w_ref[:, 0:Q_OFF])
    xe_ref[POOL_HALO + tm:, :] = jnp.where(i < pl.num_programs(0) - 1, a_halo[POOL_HALO:], 0.0)

    a_g = _dot(hn, w_ref[:, G_OFF:IN_DIM])
    a_u = _dot(hn, w_ref[:, U_OFF:G_OFF])
    a_q = _dot(hn, w_ref[:, Q_OFF:K_OFF])
    a_k = _dot(hn, w_ref[:, K_OFF:V_OFF])
    _store_keys(v_ref, _dot(hn, w_ref[:, V_OFF:U_OFF]))

    gv = _gelu_tanh(a_g)
    q_ref[...] = (a_q * _head_rms_scale(a_q) * qg_ref[...]).astype(jnp.bfloat16)
    _store_keys(k_ref, a_k * _head_rms_scale(a_k) * kg_ref[...])

    u = _gelu_tanh(a_u)
    vn = (gv * _head_rms_scale(gv) * sgn_ref[...]).astype(jnp.bfloat16)
    low = lax.broadcasted_iota(jnp.int32, (SG_CHUNK, LANES), 1) < HEAD_DIM
    for c in range(tm // SG_CHUNK):
        rows = slice(c * SG_CHUNK, (c + 1) * SG_CHUNK)
        for s in range(SG_DIM // LANES):
            cols = slice(s * LANES, (s + 1) * LANES)
            m = _dot(sgw_ref[s], vn[rows, cols]) + sgb_ref[s]
            mixed = jnp.where(low, m[:SG_CHUNK], m[SG_CHUNK:])
            sg_ref[rows, cols] = (u[rows, cols] * mixed).astype(jnp.bfloat16)

    pool_ref[...] = _pool_mix(xe_ref, pw_ref, ps_ref, tm, seq_len)


def _inproj_call(stream, n1, sh, sc, w_in, layer, qg, kg, sgn, sgw, sgb, pool_w, pool_scale, tm, strips):
    pending = len(stream) == 3
    n = stream[0].shape[0]
    if strips:
        kv_spec = pl.BlockSpec((N_STRIPS, tm // N_STRIPS, NA_DIM), lambda i: (0, i, 0))
        kv_shape = jax.ShapeDtypeStruct((N_STRIPS, n // N_STRIPS, NA_DIM), jnp.bfloat16)
    else:
        kv_spec = pl.BlockSpec((tm, NA_DIM), lambda i: (i, 0))
        kv_shape = jax.ShapeDtypeStruct((n, NA_DIM), jnp.bfloat16)
    row = lambda i: (i, 0)
    fixed2 = lambda i: (0, 0)
    fixed3 = lambda i: (0, 0, 0)
    vec = lambda w: pl.BlockSpec((1, w), fixed2)
    rows = pl.BlockSpec((tm, D_MODEL), row)
    per_tile = tm // HALO_BLOCK
    before = lambda i: (jnp.maximum(i * per_tile - 1, 0), 0)
    after = lambda i: (jnp.minimum((i + 1) * per_tile, n // HALO_BLOCK - 1), 0)
    halo = lambda w, m: pl.BlockSpec((HALO_BLOCK, w), m)
    h = stream[0]
    if pending:
        y, g = stream[1], stream[2]
        yw = y.shape[1]
        args = [h, y, g, h, h, y, y]
        stream_specs = [rows, pl.BlockSpec((tm, yw), row), vec(D_MODEL),
                        halo(D_MODEL, before), halo(D_MODEL, after), halo(yw, before), halo(yw, after)]
    else:
        args = [h, h, h]
        stream_specs = [rows, halo(D_MODEL, before), halo(D_MODEL, after)]
    extra_out_specs = [rows] if pending else []
    extra_out_shape = [jax.ShapeDtypeStruct((n, D_MODEL), STREAM_DTYPE)] if pending else []
    return pl.pallas_call(
        functools.partial(_inproj_kernel, pending=pending, seq_len=n),
        grid=(n // tm,),
        in_specs=stream_specs + [
            vec(D_MODEL), vec(D_MODEL), vec(D_MODEL),
            pl.BlockSpec((None, D_MODEL, IN_DIM), lambda i: (layer, 0, 0)),
            vec(NA_DIM), vec(NA_DIM),
            vec(SG_DIM),
            pl.BlockSpec((SG_DIM // LANES, 2 * SG_CHUNK, SG_CHUNK), fixed3),
            pl.BlockSpec((SG_DIM // LANES, 2 * SG_CHUNK, LANES), fixed3),
            pl.BlockSpec((POOL_DIM, POOL_DIM), fixed2),
            vec(POOL_DIM),
        ],
        out_specs=[
            pl.BlockSpec((tm, POOL_DIM), row),
            pl.BlockSpec((tm, NA_DIM), row),
            kv_spec,
            kv_spec,
            pl.BlockSpec((tm, SG_DIM), row),
        ] + extra_out_specs,
        out_shape=[
            jax.ShapeDtypeStruct((n, POOL_DIM), jnp.bfloat16),
            jax.ShapeDtypeStruct((n, NA_DIM), jnp.bfloat16),
            kv_shape,
            kv_shape,
            jax.ShapeDtypeStruct((n, SG_DIM), jnp.bfloat16),
        ] + extra_out_shape,
        scratch_shapes=[pltpu.VMEM((tm + 2 * POOL_HALO, POOL_DIM), jnp.float32)],
        compiler_params=_cparams(("arbitrary",)),
        name="inproj",
    )(*args, n1, sh, sc, w_in, qg, kg, sgn, sgw, sgb, pool_w, pool_scale)


NA_ROWS_PER_BLOCK = 64
NA_GROUP_ROWS = 4
NA_WINDOW_ROWS = NA_GROUP_ROWS + NA_WIN_ROWS
NA_BLOCK = NA_ROWS_PER_BLOCK * GRID_W
NA_GROUP = NA_GROUP_ROWS * GRID_W
NA_HALF_COLS = GRID_W // 2
NA_CHAIN = NA_GROUP_ROWS * NA_HALF_COLS
NA_HALF_STRIPS = 5
NA_HALF_COL0 = (0, GRID_W - NA_HALF_STRIPS * STRIP_W)
NA_RUN = NA_WINDOW_ROWS * STRIP_W
NA_LOCAL = 512
NA_STRIP_BLOCK = NA_ROWS_PER_BLOCK * STRIP_W
NA_STRIP_HALO = (NA_WIN_ROWS // 2) * STRIP_W
NA_EDGE_FIRST, NA_EDGE_NONE, NA_EDGE_LAST = 0, 1, 2


def _stack_heads(x, low):
    zero = jnp.zeros_like(x)
    return jnp.concatenate([jnp.where(low, x, zero), jnp.where(low, zero, x)], axis=0)


def _natten_kernel(q_ref, kp_ref, kc_ref, kn_ref, vp_ref, vc_ref, vn_ref, kx_ref, vx_ref, bias_ref,
                   wg_ref, wu_ref, wd_ref,
                   o_ref, wg_bf_ref, wu_bf_ref, wd_bf_ref, kwin_ref, vwin_ref, vxe_ref, *, grid_rows):
    b = pl.program_id(1)
    wg_bf_ref[...] = wg_ref[...].astype(jnp.bfloat16)
    wu_bf_ref[...] = wu_ref[...].astype(jnp.bfloat16)
    wd_bf_ref[...] = wd_ref[...].astype(jnp.bfloat16)
    top, bottom = NA_STRIP_HALO, NA_STRIP_HALO + NA_STRIP_BLOCK
    kwin_ref[:, 0:top, :] = kp_ref[...]
    kwin_ref[:, top:bottom, :] = kc_ref[...]
    kwin_ref[:, bottom:, :] = kn_ref[...]
    vwin_ref[:, 0:top, 0:LANES] = vp_ref[...]
    vwin_ref[:, top:bottom, 0:LANES] = vc_ref[...]
    vwin_ref[:, bottom:, 0:LANES] = vn_ref[...]
    vwin_ref[:, :, LANES:] = jnp.ones(vwin_ref.shape[:2] + (LANES,), jnp.bfloat16)
    vxe_ref[:, 0:LANES] = vx_ref[...]
    vxe_ref[:, LANES:] = jnp.ones((vxe_ref.shape[0], LANES), jnp.bfloat16)
    low_q = lax.broadcasted_iota(jnp.int32, (NA_CHAIN, LANES), 1) < HEAD_DIM
    n_pad = NA_LOCAL - NA_HALF_STRIPS * NA_RUN

    def window_start(g):
        r0 = b * NA_ROWS_PER_BLOCK + g * NA_GROUP_ROWS
        ws = jnp.clip(r0 - NA_WIN_ROWS // 2, 0, grid_rows - NA_WINDOW_ROWS)
        edge = jnp.where(r0 == 0, NA_EDGE_FIRST,
                         jnp.where(r0 == grid_rows - NA_GROUP_ROWS, NA_EDGE_LAST, NA_EDGE_NONE))
        start = pl.multiple_of((ws - b * NA_ROWS_PER_BLOCK + NA_WIN_ROWS // 2) * STRIP_W, NA_STRIP_HALO)
        return start, edge

    def local_window(win_ref, g, half):
        start, _ = window_start(g)
        s0 = NA_HALF_COL0[half] // STRIP_W
        runs = [win_ref[s, pl.ds(start, NA_RUN), :] for s in range(s0, s0 + NA_HALF_STRIPS)]
        return jnp.concatenate(runs + [jnp.zeros((n_pad, win_ref.shape[2]), jnp.bfloat16)], axis=0)

    def query_rows(g, half, j):
        first = g * NA_GROUP + j * GRID_W + half * NA_HALF_COLS
        return slice(first, first + NA_HALF_COLS)

    def scores(c):
        g, half = divmod(c, 2)
        _, edge = window_start(g)
        qh = jnp.concatenate([q_ref[query_rows(g, half, j), :] for j in range(NA_GROUP_ROWS)], axis=0)
        lhs = _stack_heads(qh, low_q)
        kl = local_window(kwin_ref, g, half)
        return jnp.concatenate([_dot_nt(lhs, kl).astype(jnp.bfloat16) + bias_ref[edge, half],
                                _dot_nt(lhs, kx_ref[...]).astype(jnp.bfloat16)], axis=-1)

    n_chains = 2 * (NA_ROWS_PER_BLOCK // NA_GROUP_ROWS)
    s_next = scores(0)
    for c in range(n_chains):
        s = s_next
        if c + 1 < n_chains:
            s_next = scores(c + 1)
        g, half = divmod(c, 2)
        vl = local_window(vwin_ref, g, half)
        m = jnp.max(s, axis=-1, keepdims=True)
        pb = jnp.exp2(s - m)
        o = _dot(pb[:, :NA_LOCAL], vl) + _dot(pb[:, NA_LOCAL:], vxe_ref[...])
        o = o[:, :LANES] * (1.0 / o[:, LANES:])
        o = jnp.where(low_q, o[:NA_CHAIN], o[NA_CHAIN:]).astype(jnp.bfloat16)
        for j in range(NA_GROUP_ROWS):
            o_ref[query_rows(g, half, j), :] = o[j * NA_HALF_COLS:(j + 1) * NA_HALF_COLS]


def _natten_call(q, k, v, k_ctx, v_ctx, bias, w_gate, w_up, w_down, layer):
    n = q.shape[0]
    grid_rows = n // GRID_W
    assert grid_rows % NA_ROWS_PER_BLOCK == 0 and grid_rows >= 2 * NA_ROWS_PER_BLOCK
    nblk = n // NA_BLOCK
    steps = HEAD_PAIRS * nblk
    depth = w_gate.shape[0]
    up_rows = N_EXPERTS * D_MODEL
    down_rows = N_EXPERTS * D_EXPERT
    assert up_rows % steps == 0 and down_rows % steps == 0
    wg2 = w_gate.reshape(depth * up_rows, D_EXPERT)
    wu2 = w_up.reshape(depth * up_rows, D_EXPERT)
    wd2 = w_down.reshape(depth * down_rows, D_MODEL)
    up_in = pl.BlockSpec((up_rows // steps, D_EXPERT), lambda p, b: (layer * steps + p * nblk + b, 0))
    down_in = pl.BlockSpec((down_rows // steps, D_MODEL), lambda p, b: (layer * steps + p * nblk + b, 0))
    up_out = pl.BlockSpec((up_rows // steps, D_EXPERT), lambda p, b: (p * nblk + b, 0))
    down_out = pl.BlockSpec((down_rows // steps, D_MODEL), lambda p, b: (p * nblk + b, 0))
    n_halo = n // N_STRIPS // NA_STRIP_HALO
    hb = NA_STRIP_BLOCK // NA_STRIP_HALO
    rows = pl.BlockSpec((NA_BLOCK, LANES), lambda p, b: (b, p))
    cur = pl.BlockSpec((N_STRIPS, NA_STRIP_BLOCK, LANES), lambda p, b: (0, b, p))
    prev = pl.BlockSpec((N_STRIPS, NA_STRIP_HALO, LANES), lambda p, b: (0, jnp.maximum(b * hb - 1, 0), p))
    nxt = pl.BlockSpec((N_STRIPS, NA_STRIP_HALO, LANES),
                       lambda p, b: (0, jnp.minimum((b + 1) * hb, n_halo - 1), p))
    ctx = pl.BlockSpec((k_ctx.shape[0], LANES), lambda p, b: (0, p))
    win_rows = NA_STRIP_BLOCK + 2 * NA_STRIP_HALO
    att, wg_bf, wu_bf, wd_bf = pl.pallas_call(
        functools.partial(_natten_kernel, grid_rows=grid_rows),
        grid=(HEAD_PAIRS, nblk),
        in_specs=[rows, prev, cur, nxt, prev, cur, nxt, ctx, ctx,
                  pl.BlockSpec((None, 3, 2, 2 * NA_CHAIN, NA_LOCAL), lambda p, b: (p, 0, 0, 0, 0)),
                  up_in, up_in, down_in],
        out_specs=[rows, up_out, up_out, down_out],
        out_shape=[jax.ShapeDtypeStruct((n, NA_DIM), jnp.bfloat16),
                   jax.ShapeDtypeStruct((up_rows, D_EXPERT), jnp.bfloat16),
                   jax.ShapeDtypeStruct((up_rows, D_EXPERT), jnp.bfloat16),
                   jax.ShapeDtypeStruct((down_rows, D_MODEL), jnp.bfloat16)],
        scratch_shapes=[pltpu.VMEM((N_STRIPS, win_rows, LANES), jnp.bfloat16),
                        pltpu.VMEM((N_STRIPS, win_rows, 2 * LANES), jnp.bfloat16),
                        pltpu.VMEM((k_ctx.shape[0], 2 * LANES), jnp.bfloat16)],
        compiler_params=_cparams(("arbitrary", "arbitrary")),
        name="natten",
    )(q, k, k, k, v, v, v, k_ctx, v_ctx, bias, wg2, wu2, wd2)
    return (att, wg_bf.reshape(N_EXPERTS, D_MODEL, D_EXPERT), wu_bf.reshape(N_EXPERTS, D_MODEL, D_EXPERT),
            wd_bf.reshape(N_EXPERTS, D_EXPERT, D_MODEL))


def _natten_bias(rpb):
    cols = np.arange(GRID_W)
    col_start = np.clip(cols - NA_WIN_COLS // 2, 0, GRID_W - NA_WIN_COLS)
    kc = np.arange(GRID_W)
    in_win = (kc[None, :] >= col_start[:, None]) & (kc[None, :] < col_start[:, None] + NA_WIN_COLS)
    dc = kc[None, :] - cols[:, None] + NA_WIN_COLS - 1
    sel = (np.arange(2 * NA_WIN_COLS - 1)[:, None, None] == dc[None]) & in_win[None]
    t2 = jnp.einsum("hdj,jqk->hdqk", rpb, jnp.asarray(sel, jnp.float32), precision=lax.Precision.HIGHEST)
    t2 = jnp.where(in_win[None, None], t2 * LOG2_E, NEG_BIG)
    neg = jnp.full((NA_HEADS, 1, GRID_W, GRID_W), NEG_BIG, jnp.float32)
    t2e = jnp.concatenate([neg, t2, neg], axis=1)
    u = jnp.concatenate([t2e[:, :-1], t2e[:, 1:]], axis=-1)
    u = u.reshape(HEAD_PAIRS, 2, 2 * NA_WIN_ROWS, GRID_W, LANES)

    place = np.zeros((2, NA_WINDOW_ROWS * GRID_W, NA_LOCAL), np.float32)
    for half, c0 in enumerate(NA_HALF_COL0):
        for a in range(NA_WINDOW_ROWS):
            for kcol in range(c0, c0 + NA_HALF_STRIPS * STRIP_W):
                s, c8 = divmod(kcol - c0, STRIP_W)
                place[half, a * GRID_W + kcol, s * NA_RUN + a * STRIP_W + c8] = 1.0
    outside = np.full((3, NA_GROUP_ROWS, NA_LOCAL), NEG_BIG, np.float32)
    for edge in (NA_EDGE_FIRST, NA_EDGE_NONE, NA_EDGE_LAST):
        for j in range(NA_GROUP_ROWS):
            lo, _ = _window_rows(edge, j)
            for s in range(NA_HALF_STRIPS):
                outside[edge, j, s * NA_RUN + lo * STRIP_W:s * NA_RUN + (lo + NA_WIN_ROWS) * STRIP_W] = 0.0
    return pl.pallas_call(
        _bias_expand_kernel,
        grid=(HEAD_PAIRS,),
        in_specs=[pl.BlockSpec((None, 2, 2 * NA_WIN_ROWS, GRID_W, LANES), lambda p: (p, 0, 0, 0, 0)),
                  pl.BlockSpec(place.shape, lambda p: (0, 0, 0)),
                  pl.BlockSpec(outside.shape, lambda p: (0, 0, 0))],
        out_specs=pl.BlockSpec((None, 3, 2, 2 * NA_CHAIN, NA_LOCAL), lambda p: (p, 0, 0, 0, 0)),
        out_shape=jax.ShapeDtypeStruct((HEAD_PAIRS, 3, 2, 2 * NA_CHAIN, NA_LOCAL), jnp.bfloat16),
        compiler_params=_cparams(("arbitrary",)),
        name="bias_expand",
    )(u, jnp.asarray(place, jnp.bfloat16), jnp.asarray(outside))


def _window_rows(edge, j):
    if edge == NA_EDGE_FIRST:
        return 0, NA_WIN_ROWS - 1 - j
    if edge == NA_EDGE_NONE:
        return j, NA_WIN_ROWS // 2 - 1
    return NA_WINDOW_ROWS - NA_WIN_ROWS, NA_WIN_ROWS // 2 - 1 - j


def _bias_expand_kernel(u_ref, place_ref, outside_ref, o_ref):
    low = lax.broadcasted_iota(jnp.int32, (NA_HALF_COLS, LANES), 1) < GRID_W
    zero = jnp.zeros((NA_HALF_COLS, LANES), jnp.float32)
    for edge in (NA_EDGE_FIRST, NA_EDGE_NONE, NA_EDGE_LAST):
        for half in range(2):
            q0 = half * NA_HALF_COLS
            blocks, masks = [], []
            for hd in range(2):
                for j in range(NA_GROUP_ROWS):
                    lo, base = _window_rows(edge, j)
                    tiles = []
                    for i in range(NA_WINDOW_ROWS // 2):
                        a0, a1 = 2 * i, 2 * i + 1
                        ok0 = lo <= a0 < lo + NA_WIN_ROWS
                        ok1 = lo <= a1 < lo + NA_WIN_ROWS
                        if not (ok0 or ok1):
                            tile = zero
                        else:
                            tile = u_ref[hd, base + a1 - lo, q0:q0 + NA_HALF_COLS, :]
                            if not ok0:
                                tile = jnp.where(low, zero, tile)
                            if not ok1:
                                tile = jnp.where(low, tile, zero)
                        tiles.append(tile)
                    blocks.append(jnp.concatenate(tiles, axis=-1))
                    masks.append(jnp.broadcast_to(outside_ref[edge, j:j + 1, :], (NA_HALF_COLS, NA_LOCAL)))
            lhs = jnp.concatenate(blocks, axis=0).astype(jnp.bfloat16)
            placed = _dot(lhs, place_ref[half]) + jnp.concatenate(masks, axis=0)
            o_ref[edge, half] = placed.astype(o_ref.dtype)


def _ctxatt_kernel(q_ref, k_ref, v_ref, o_ref):
    lc = q_ref.shape[0]
    low = lax.broadcasted_iota(jnp.int32, (lc, LANES), 1) < HEAD_DIM
    lhs = _stack_heads(q_ref[...], low)
    s = _dot_nt(lhs, k_ref[...])
    m = jnp.max(s, axis=-1, keepdims=True)
    p = jnp.exp2(s - m)
    denom = jnp.sum(p, axis=-1, keepdims=True)
    o = _dot(p.astype(jnp.bfloat16), v_ref[...]) * (1.0 / denom)
    o_ref[...] = jnp.where(low, o[:lc], o[lc:]).astype(jnp.bfloat16)


def _ctxatt_call(q, k, v):
    lc = q.shape[0]
    spec = pl.BlockSpec((lc, LANES), lambda p: (0, p))
    return pl.pallas_call(
        _ctxatt_kernel,
        grid=(HEAD_PAIRS,),
        in_specs=[spec, spec, spec],
        out_specs=spec,
        out_shape=jax.ShapeDtypeStruct((lc, NA_DIM), jnp.bfloat16),
        compiler_params=_cparams(("arbitrary",)),
        name="ctxatt",
    )(q, k, v)


OUTPROJ_CHAIN = 256
OUTPROJ_TILE = 1024
INPROJ_TILE = 1024
RESIDUAL_TILE = 2048


def _outproj_kernel(h_ref, mp_ref, att_ref, sg_ref, wo_ref, g1_ref, n2_ref, sh_ref, sc_ref,
                    wr_ref, br_ref, h1_ref, pay_ref, route_ref):
    chains = [slice(c * OUTPROJ_CHAIN, (c + 1) * OUTPROJ_CHAIN) for c in range(h_ref.shape[0] // OUTPROJ_CHAIN)]
    h1s = []
    for rows in chains:
        mix = (_dot(mp_ref[rows, :], wo_ref[0:POOL_DIM, :])
               + _dot(att_ref[rows, :], wo_ref[POOL_DIM:POOL_DIM + NA_DIM, :])
               + _dot(sg_ref[rows, :], wo_ref[POOL_DIM + NA_DIM:, :]))
        h1 = h_ref[rows, :].astype(jnp.float32) + g1_ref[...] * mix
        h1_ref[rows, :] = h1.astype(h1_ref.dtype)
        h1s.append(h1)
    for rows, h1 in zip(chains, h1s):
        _outproj_route(rows, h1, n2_ref, sh_ref, sc_ref, wr_ref, br_ref, pay_ref, route_ref)


def _outproj_route(rows, h1, n2_ref, sh_ref, sc_ref, wr_ref, br_ref, pay_ref, route_ref):
    tm = OUTPROJ_CHAIN
    ms = jnp.mean(h1 * h1, axis=-1, keepdims=True)
    hm = (h1 * lax.rsqrt(ms + EPS)) * (n2_ref[...] * (1.0 + sc_ref[...])) + sh_ref[...]
    pay_ref[rows, 0:HALF_D] = _pack_halves(hm)

    hm_hi = hm.astype(jnp.bfloat16)
    lt = _dot_nt(wr_ref[...], hm_hi)
    logits = lt[:N_EXPERTS] + lt[N_EXPERTS:] + br_ref[...]
    e = jnp.exp(logits - jnp.max(logits, axis=0, keepdims=True))

    best = ga = gb = e1 = e2 = cls = None
    for c in range(N_CLASSES):
        a, b2 = int(CLASS_E1[c]), int(CLASS_E2[c])
        ea, eb = e[a:a + 1, :], e[b2:b2 + 1, :]
        s = ea + eb
        if best is None:
            best, ga, gb = s, ea, eb
            e1 = jnp.full_like(s, float(a))
            e2 = jnp.full_like(s, float(b2))
            cls = jnp.zeros_like(s)
        else:
            better = s > best
            best = jnp.where(better, s, best)
            ga = jnp.where(better, ea, ga)
            gb = jnp.where(better, eb, gb)
            e1 = jnp.where(better, float(a), e1)
            e2 = jnp.where(better, float(b2), e2)
            cls = jnp.where(better, float(c), cls)
    inv = 1.0 / best
    row = lax.broadcasted_iota(jnp.int32, (ROUTE_ROWS, tm), 0)
    rec = jnp.where(row == 0, ga * inv,
          jnp.where(row == 1, gb * inv,
          jnp.where(row == 2, e1,
          jnp.where(row == 3, e2,
          jnp.where(row == 4, cls, 0.0)))))
    route_ref[:, rows] = rec
    wide = jnp.concatenate([rec, jnp.zeros((LANES - ROUTE_ROWS, tm), jnp.float32)], axis=0)
    pay_ref[rows, HALF_D:] = pltpu.bitcast(wide.T, jnp.uint32)


def _outproj_call(h, mp, att, sg, w_out, layer, g1, n2, sh2, sc2, wr, br, tm):
    n = h.shape[0]
    row = lambda i: (i, 0)
    fixed = lambda i: (0, 0)
    vec = pl.BlockSpec((1, D_MODEL), fixed)
    return pl.pallas_call(
        _outproj_kernel,
        grid=(n // tm,),
        in_specs=[
            pl.BlockSpec((tm, D_MODEL), row),
            pl.BlockSpec((tm, POOL_DIM), row),
            pl.BlockSpec((tm, NA_DIM), row),
            pl.BlockSpec((tm, SG_DIM), row),
            pl.BlockSpec((None, D_MODEL, D_MODEL), lambda i: (layer, 0, 0)),
            vec, vec, vec, vec,
            pl.BlockSpec((2 * N_EXPERTS, D_MODEL), fixed),
            pl.BlockSpec((N_EXPERTS, 1), fixed),
        ],
        out_specs=[
            pl.BlockSpec((tm, D_MODEL), row),
            pl.BlockSpec((tm, PAYLOAD_W), row),
            pl.BlockSpec((ROUTE_ROWS, tm), lambda i: (0, i)),
        ],
        out_shape=[
            jax.ShapeDtypeStruct((n, D_MODEL), STREAM_DTYPE),
            jax.ShapeDtypeStruct((n, PAYLOAD_W), jnp.uint32),
            jax.ShapeDtypeStruct((ROUTE_ROWS, n), jnp.float32),
        ],
        compiler_params=_cparams(("arbitrary",)),
        name="outproj",
    )(h, mp, att, sg, w_out, g1, n2, sh2, sc2, wr, br)


def _payload_parts(pay_ref, rows=slice(None)):
    lo, hi = _unpack_halves(pay_ref[rows, 0:HALF_D])
    x = jnp.concatenate([lo, hi], axis=-1).astype(jnp.bfloat16)
    return x, pltpu.bitcast(pay_ref[rows, HALF_D:], jnp.float32)


def _expert_pair(x, ga, gb, wga, wua, wda, wgb, wub, wdb):
    ha = (_silu(_dot(x, wga)) * _dot(x, wua) * ga).astype(jnp.bfloat16)
    hb = (_silu(_dot(x, wgb)) * _dot(x, wub) * gb).astype(jnp.bfloat16)
    return _dot(ha, wda) + _dot(hb, wdb)


MOE_QUARTER = MOE_TM // 8


def _moe_sorted_kernel(blk_ref, e1_ref, e2_ref, quarters_ref, pay_ref,
                       wga_ref, wua_ref, wda_ref, wgb_ref, wub_ref, wdb_ref, o_ref):
    quarters = quarters_ref[pl.program_id(0)]

    def run(rows):
        x, route = _payload_parts(pay_ref, rows)
        y = _expert_pair(x, route[:, 0:1], route[:, 1:2], wga_ref[...], wua_ref[...], wda_ref[...],
                         wgb_ref[...], wub_ref[...], wdb_ref[...])
        o_ref[rows, :] = _pack_halves(y)

    for used in range(1, MOE_TM // MOE_QUARTER + 1):
        @pl.when(quarters == used)
        def _(used=used):
            run(slice(0, used * MOE_QUARTER))


def _moe_sorted_call(blk, e1, e2, quarters, pay_sorted, wg, wu, wd):
    n_tiles = blk.shape[0]
    rows = lambda i, blk, e1, e2, quarters: (blk[i], 0)
    wa = lambda i, blk, e1, e2, quarters: (e1[i], 0, 0)
    wb = lambda i, blk, e1, e2, quarters: (e2[i], 0, 0)
    up = lambda m: pl.BlockSpec((None, D_MODEL, D_EXPERT), m)
    down = lambda m: pl.BlockSpec((None, D_EXPERT, D_MODEL), m)
    return pl.pallas_call(
        _moe_sorted_kernel,
        grid_spec=pltpu.PrefetchScalarGridSpec(
            num_scalar_prefetch=4,
            grid=(n_tiles,),
            in_specs=[pl.BlockSpec((MOE_TM, PAYLOAD_W), rows),
                      up(wa), up(wa), down(wa), up(wb), up(wb), down(wb)],
            out_specs=pl.BlockSpec((MOE_TM, HALF_D), rows),
        ),
        out_shape=jax.ShapeDtypeStruct((n_tiles * MOE_TM, HALF_D), jnp.uint32),
        compiler_params=_cparams(("arbitrary",)),
        name="moe_sorted",
    )(blk, e1, e2, quarters, pay_sorted, wg, wu, wd, wg, wu, wd)


DENSE_EXPERTS_PER_STEP = GROUP_SIZE


def _moe_dense_kernel(*refs, first_step, resume):
    if resume:
        pay_ref, wg_ref, wu_ref, wd_ref, acc_ref, o_ref = refs
    else:
        pay_ref, wg_ref, wu_ref, wd_ref, o_ref = refs
    step = pl.program_id(0)

    @pl.when(step == 0)
    def _():
        o_ref[...] = acc_ref[...] if resume else jnp.zeros_like(o_ref)

    x, route = _payload_parts(pay_ref)
    y = None
    for k in range(DENSE_EXPERTS_PER_STEP):
        ef = ((first_step + step) * DENSE_EXPERTS_PER_STEP + k).astype(jnp.float32)
        gate = (jnp.where(route[:, 2:3] == ef, route[:, 0:1], 0.0)
                + jnp.where(route[:, 3:4] == ef, route[:, 1:2], 0.0))
        he = (_silu(_dot(x, wg_ref[k])) * _dot(x, wu_ref[k]) * gate).astype(jnp.bfloat16)
        yk = _dot(he, wd_ref[k])
        y = yk if y is None else y + yk
    o_ref[...] += y


def _moe_dense_call(pay, wg, wu, wd, first_step, n_steps, acc=None):
    n = pay.shape[0]
    per = DENSE_EXPERTS_PER_STEP
    whole = pl.BlockSpec((n, D_MODEL), lambda e: (0, 0))
    return pl.pallas_call(
        functools.partial(_moe_dense_kernel, first_step=first_step, resume=acc is not None),
        grid=(n_steps,),
        in_specs=[pl.BlockSpec((n, PAYLOAD_W), lambda e: (0, 0)),
                  pl.BlockSpec((per, D_MODEL, D_EXPERT), lambda e: (first_step + e, 0, 0)),
                  pl.BlockSpec((per, D_MODEL, D_EXPERT), lambda e: (first_step + e, 0, 0)),
                  pl.BlockSpec((per, D_EXPERT, D_MODEL), lambda e: (first_step + e, 0, 0))]
                 + ([whole] if acc is not None else []),
        out_specs=whole,
        out_shape=jax.ShapeDtypeStruct((n, D_MODEL), jnp.float32),
        compiler_params=_cparams(("arbitrary",)),
        name="moe_dense",
    )(pay, wg, wu, wd, *([acc] if acc is not None else []))


def _moe_out(y_ref):
    if y_ref.dtype == jnp.uint32:
        return jnp.concatenate(_unpack_halves(y_ref[...]), axis=-1)
    return y_ref[...]


def _residual_kernel(h_ref, y_ref, g_ref, o_ref):
    o_ref[...] = h_ref[...].astype(jnp.float32) + g_ref[...] * _moe_out(y_ref)


def _residual_call(h, y, g, tm):
    n = h.shape[0]
    row = pl.BlockSpec((tm, D_MODEL), lambda i: (i, 0))
    return pl.pallas_call(
        _residual_kernel,
        grid=(n // tm,),
        in_specs=[row, pl.BlockSpec((tm, y.shape[1]), lambda i: (i, 0)),
                  pl.BlockSpec((1, D_MODEL), lambda i: (0, 0))],
        out_specs=row,
        out_shape=jax.ShapeDtypeStruct((n, D_MODEL), jnp.float32),
        compiler_params=_cparams(("arbitrary",)),
        name="residual",
    )(h, y, g)


SC_ROWS = 128


SC_CORES = 2
SC_SUBCORES = 16
SC_WORKERS = SC_CORES * SC_SUBCORES


def _sc_mesh():
    return plsc.VectorSubcoreMesh(core_axis_name="core", subcore_axis_name="subcore")


def _sc_worker():
    return lax.axis_index("subcore") * SC_CORES + lax.axis_index("core")


def _scatter_rows(x, dest, n_out):
    n, w = x.shape

    per_worker = n // SC_WORKERS
    assert per_worker % SC_ROWS == 0

    @functools.partial(pl.kernel, out_type=jax.ShapeDtypeStruct((n_out, w), x.dtype), mesh=_sc_mesh(),
                       scratch_types=[pltpu.VMEM((SC_ROWS,), jnp.int32), pltpu.VMEM((SC_ROWS, w), x.dtype)])
    def scatter(x_hbm, i_hbm, o_hbm, idx_v, rows_v):
        first = _sc_worker() * per_worker

        @pl.loop(0, per_worker // SC_ROWS)
        def _(i):
            base = pl.multiple_of(first + i * SC_ROWS, SC_ROWS)
            pltpu.sync_copy(i_hbm.at[pl.ds(base, SC_ROWS)], idx_v)
            pltpu.sync_copy(x_hbm.at[pl.ds(base, SC_ROWS)], rows_v)
            pltpu.sync_copy(rows_v, o_hbm.at[idx_v])

    return scatter(x, dest)


def _gather_rows(x, src):
    n = src.shape[0]
    w = x.shape[1]

    per_worker = n // SC_WORKERS
    assert per_worker % SC_ROWS == 0

    @functools.partial(pl.kernel, out_type=jax.ShapeDtypeStruct((n, w), x.dtype), mesh=_sc_mesh(),
                       scratch_types=[pltpu.VMEM((SC_ROWS,), jnp.int32), pltpu.VMEM((SC_ROWS, w), x.dtype)])
    def gather(x_hbm, i_hbm, o_hbm, idx_v, rows_v):
        first = _sc_worker() * per_worker

        @pl.loop(0, per_worker // SC_ROWS)
        def _(i):
            base = pl.multiple_of(first + i * SC_ROWS, SC_ROWS)
            pltpu.sync_copy(i_hbm.at[pl.ds(base, SC_ROWS)], idx_v)
            pltpu.sync_copy(x_hbm.at[idx_v], rows_v)
            pltpu.sync_copy(rows_v, o_hbm.at[pl.ds(base, SC_ROWS)])

    return gather(x, src)


def _routing_plan(cls, n_tiles):
    onehot = (cls[:, None] == jnp.arange(N_CLASSES, dtype=jnp.int32)[None, :]).astype(jnp.int32)
    counts = jnp.sum(onehot, axis=0)
    rank = jnp.sum((jnp.cumsum(onehot, axis=0) - onehot) * onehot, axis=1)
    tiles = (counts + MOE_TM - 1) // MOE_TM
    tile_end = jnp.cumsum(tiles)
    tile_start = tile_end - tiles
    dest = jnp.sum(onehot * tile_start[None, :], axis=1) * MOE_TM + rank
    nact = tile_end[-1]
    tile = jnp.arange(n_tiles, dtype=jnp.int32)
    blk = jnp.minimum(tile, nact - 1)
    tile_cls = jnp.sum((blk[:, None] >= tile_end[None, :]).astype(jnp.int32), axis=1)
    e1 = jnp.asarray(CLASS_E1)[tile_cls]
    e2 = jnp.asarray(CLASS_E2)[tile_cls]
    valid = counts[tile_cls] - (blk - tile_start[tile_cls]) * MOE_TM
    quarters = jnp.where(tile >= nact, 0, (jnp.minimum(valid, MOE_TM) + MOE_QUARTER - 1) // MOE_QUARTER)
    return dest.astype(jnp.int32), blk, e1, e2, quarters.astype(jnp.int32)


def _row_tile(n, prefer=512):
    return next(t for t in (prefer, 512, 256) if n % t == 0)


def kernel(x, c, ctx, c_ctx, w_ada, b_ada, norm1, w_in, pool_w, pool_scale, q_norm, k_norm, rpb,
           sg_w, sg_b, sg_norm, w_out, norm2, w_router, b_router, w_gate, w_up, w_down):
    depth = w_ada.shape[0]
    n = x.shape[1]
    lc = ctx.shape[1]
    bf = jnp.bfloat16
    lat_stream = (x[0],)
    h_ctx = ctx[0]

    cond = jnp.stack([c[0], c_ctx], axis=1)
    mod = _ada_call(cond, w_ada, b_ada)

    wr_t = w_router.T
    wr_hi = wr_t.astype(bf)
    wr_lo = (wr_t - wr_hi.astype(jnp.float32)).astype(bf)
    wr = jnp.concatenate([wr_hi, wr_lo], axis=0)
    br = b_router.reshape(N_EXPERTS, 1)
    n_tiles = n // MOE_TM + N_CLASSES
    w_in_bf = w_in.astype(bf)
    w_out_bf = w_out.astype(bf)

    for l in range(depth):
        last = l == depth - 1
        qg =(q_norm[l] * (HEAD_DIM ** -0.5 * LOG2_E)).reshape(1, NA_DIM)
        kg = k_norm[l].reshape(1, NA_DIM)
        sgn = sg_norm[l].reshape(1, SG_DIM)
        sgw = sg_w[l].astype(bf).reshape(SG_DIM // LANES, 2 * SG_CHUNK, SG_CHUNK)
        sgb = jnp.broadcast_to(sg_b[l].reshape(SG_DIM // LANES, 2 * SG_CHUNK, 1),
                               (SG_DIM // LANES, 2 * SG_CHUNK, LANES))
        pool_bd = jax.scipy.linalg.block_diag(*[pool_w[l, g] for g in range(len(POOL_WINDOWS))]).astype(bf)
        pscale = pool_scale[l].reshape(1, POOL_DIM)
        bias = _natten_bias(rpb[l])
        n1 = norm1[l].reshape(1, D_MODEL)
        n2 = norm2[l].reshape(1, D_MODEL)

        def mods(row):
            return [mod[l, row:row + 1, i * D_MODEL:(i + 1) * D_MODEL] for i in range(6)]

        sh1, sc1, g1, sh2, sc2, g2 = mods(0)
        csh1, csc1, cg1, csh2, csc2, cg2 = mods(1)

        tc = _row_tile(lc)
        mix_pool_c, q_c, k_c, v_c, sg_c = _inproj_call((h_ctx,), n1, csh1, csc1, w_in_bf, l, qg, kg,
                                                       sgn, sgw, sgb, pool_bd, pscale, tc, strips=False)

        outs = _inproj_call(lat_stream, n1, sh1, sc1, w_in_bf, l, qg, kg, sgn, sgw, sgb,
                            pool_bd, pscale, _row_tile(n, INPROJ_TILE), strips=True)
        mix_pool, q, k, v, sg = outs[:5]
        h_lat = outs[5] if len(lat_stream) == 3 else lat_stream[0]
        att, wg_l, wu_l, wd_l = _natten_call(q, k, v, k_c, v_c, bias, w_gate, w_up, w_down, l)
        h1, pay, route = _outproj_call(h_lat, mix_pool, att, sg, w_out_bf, l, g1, n2, sh2, sc2, wr, br,
                                       _row_tile(n, OUTPROJ_TILE))
        cls = route[4].astype(jnp.int32)
        dest, blk, e1, e2, quarters = _routing_plan(cls, n_tiles)
        pay_sorted = _scatter_rows(pay, dest, n_tiles * MOE_TM)

        if not last:
            dense_steps = N_EXPERTS // DENSE_EXPERTS_PER_STEP
            att_c = _ctxatt_call(q_c, k_c, v_c)
            h1_c, pay_c, _ = _outproj_call(h_ctx, mix_pool_c, att_c, sg_c, w_out_bf, l, cg1, n2, csh2, csc2,
                                           wr, br, tc)
            y_c = _moe_dense_call(pay_c, wg_l, wu_l, wd_l, 0, dense_steps // 2)
            quarters, y_c = lax.optimization_barrier((quarters, y_c))

        y_sorted = _moe_sorted_call(blk, e1, e2, quarters, pay_sorted, wg_l, wu_l, wd_l)
        y = _gather_rows(y_sorted, dest)
        lat_stream = (h1, y, g2)

        if not last:
            y_c = _moe_dense_call(pay_c, wg_l, wu_l, wd_l, dense_steps // 2, dense_steps - dense_steps // 2,
                                  acc=y_c)
            h_ctx = _residual_call(h1_c, y_c, cg2, tc)

    return _residual_call(*lat_stream, _row_tile(n, RESIDUAL_TILE))[None]
```

```python
import functools

import jax
import jax.numpy as jnp
import numpy as np
from jax import lax
from jax.experimental import pallas as pl
from jax.experimental.pallas import tpu as pltpu
from jax.experimental.pallas import tpu_sc as plsc

D_MODEL = 1024
GRID_W = 64
HEAD_DIM = 64
POOL_WINDOWS = (2, 4, 8, 16)
POOL_DIM = 256
NA_HEADS = 8
NA_DIM = 512
NA_WIN_ROWS = 8
NA_WIN_COLS = 16
SG_DIM = 256
SG_CHUNK = 128
Q_OFF = POOL_DIM
K_OFF = Q_OFF + NA_DIM
V_OFF = K_OFF + NA_DIM
U_OFF = V_OFF + NA_DIM
G_OFF = U_OFF + SG_DIM
IN_DIM = G_OFF + SG_DIM
N_EXPERTS = 16
GROUP_SIZE = 4
D_EXPERT = 512
EPS = 1e-6

LANES = 128
SUBLANES = 8
HEAD_PAIRS = NA_DIM // LANES
VMEM_LIMIT = 48 * 1024 * 1024

PAIRS = ((0, 1), (0, 2), (1, 2), (1, 3), (0, 3), (2, 3))
N_CLASSES = (N_EXPERTS // GROUP_SIZE) * len(PAIRS)
CLASS_E1 = np.array([4 * g + i for g in range(4) for (i, j) in PAIRS], np.int32)
CLASS_E2 = np.array([4 * g + j for g in range(4) for (i, j) in PAIRS], np.int32)

ROUTE_ROWS = 8
HALF_D = D_MODEL // 2
PAYLOAD_W = HALF_D + LANES
MOE_TM = 512
NEG_BIG = -1e30
LOG2_E = 1.4426950408889634


def _cparams(sem):
    return pltpu.CompilerParams(dimension_semantics=sem, vmem_limit_bytes=VMEM_LIMIT)


def _dot(a, b):
    return jnp.dot(a, b, preferred_element_type=jnp.float32)


def _dot_nt(a, b):
    return lax.dot_general(a, b, (((1,), (1,)), ((), ())), preferred_element_type=jnp.float32)


def _gelu_tanh(x):
    return 0.5 * x * (1.0 + jnp.tanh(0.7978845608028654 * (x + 0.044715 * (x * x * x))))


def _silu(x):
    return x * (1.0 / (1.0 + jnp.exp(-x)))


def _pack_halves(x):
    w = x.shape[1] // 2
    lo = pltpu.bitcast(x[:, :w].astype(jnp.bfloat16).astype(jnp.float32), jnp.uint32) >> 16
    hi = pltpu.bitcast(x[:, w:].astype(jnp.bfloat16).astype(jnp.float32), jnp.uint32) & jnp.uint32(0xFFFF0000)
    return lo | hi


def _unpack_halves(words):
    lo = pltpu.bitcast(words << 16, jnp.float32)
    hi = pltpu.bitcast(words & jnp.uint32(0xFFFF0000), jnp.float32)
    return lo, hi


def _ada_kernel(cond_ref, w_ref, b_ref, o_ref):
    cond = _silu(cond_ref[...])
    w = w_ref[...]
    rows = [jnp.sum(w * cond[:, r:r + 1], axis=0, keepdims=True) + b_ref[...] for r in range(2)]
    o_ref[...] = jnp.concatenate(rows + [jnp.zeros((SUBLANES - 2, w.shape[1]), jnp.float32)], axis=0)


def _ada_call(cond, w_ada, b_ada):
    depth = w_ada.shape[0]
    tn = 1536
    return pl.pallas_call(
        _ada_kernel,
        grid=(depth, 6 * D_MODEL // tn),
        in_specs=[
            pl.BlockSpec((D_MODEL, 2), lambda l, j: (0, 0)),
            pl.BlockSpec((None, D_MODEL, tn), lambda l, j: (l, 0, j)),
            pl.BlockSpec((None, 1, tn), lambda l, j: (l, 0, j)),
        ],
        out_specs=pl.BlockSpec((None, SUBLANES, tn), lambda l, j: (l, 0, j)),
        out_shape=jax.ShapeDtypeStruct((depth, SUBLANES, 6 * D_MODEL), jnp.float32),
        compiler_params=_cparams(("arbitrary", "arbitrary")),
        name="adaln",
    )(cond, w_ada, b_ada.reshape(depth, 1, 6 * D_MODEL))


def _norm_modulate(x, n_ref, sh_ref, sc_ref):
    ms = jnp.mean(x * x, axis=-1, keepdims=True)
    return ((x * lax.rsqrt(ms + EPS)) * (n_ref[...] * (1.0 + sc_ref[...])) + sh_ref[...]).astype(jnp.bfloat16)


def _head_rms_scale(a):
    low = lax.broadcasted_iota(jnp.int32, (a.shape[0], LANES), 1) < HEAD_DIM
    blocks = []
    for p in range(a.shape[1] // LANES):
        sq = jnp.square(a[:, p * LANES:(p + 1) * LANES])
        s_lo = jnp.sum(jnp.where(low, sq, 0.0), axis=-1, keepdims=True)
        s_hi = jnp.sum(jnp.where(low, 0.0, sq), axis=-1, keepdims=True)
        blocks.append(jnp.where(low, lax.rsqrt(s_lo * (1.0 / HEAD_DIM) + EPS),
                                lax.rsqrt(s_hi * (1.0 / HEAD_DIM) + EPS)))
    return jnp.concatenate(blocks, axis=-1)


STRIP_W = 8
N_STRIPS = GRID_W // STRIP_W


def _store_keys(ref, x):
    if len(ref.shape) == 2:
        ref[...] = x.astype(jnp.bfloat16)
        return
    pair = 2 * STRIP_W
    for s in range(N_STRIPS):
        for rp in range(x.shape[0] // (2 * GRID_W)):
            top = 2 * rp * GRID_W + s * STRIP_W
            rows = jnp.concatenate([x[top:top + STRIP_W], x[top + GRID_W:top + GRID_W + STRIP_W]], axis=0)
            ref[s, rp * pair:(rp + 1) * pair, :] = rows.astype(jnp.bfloat16)


POOL_HALO = 8
HALO_BLOCK = 16
STREAM_DTYPE = jnp.bfloat16


POOL_EDGE = 16


def _pool_mix(xe_ref, w_ref, scale_ref, tm, seq_len):
    low = lax.broadcasted_iota(jnp.int32, (tm, LANES), 1) < HEAD_DIM
    t_edge = pl.program_id(0) * tm + lax.broadcasted_iota(jnp.int32, (POOL_EDGE, LANES), 0)

    def window_mean(s, half):
        mean = s * (1.0 / (2 * half))

        def clip_fix(t):
            count = (jnp.minimum(t + half, seq_len) - jnp.maximum(t - half, 0)).astype(jnp.float32)
            return (2.0 * half) / count

        return jnp.concatenate([mean[:POOL_EDGE] * clip_fix(t_edge),
                                mean[POOL_EDGE:tm - POOL_EDGE],
                                mean[tm - POOL_EDGE:] * clip_fix(t_edge + (tm - POOL_EDGE))], axis=0)

    def window_sums(xs, n_levels):
        sums = []
        s = xs
        for k in range(n_levels):
            step = 1 << k
            s = s[:-step] + s[step:]
            sums.append(s)
        return sums

    outs = []
    for half_block, windows in enumerate(((2, 4), (8, 16))):
        xs = xe_ref[:, half_block * LANES:(half_block + 1) * LANES]
        sums = window_sums(xs, int(np.log2(windows[1])))
        parts = []
        for w in windows:
            half = w // 2
            s = sums[int(np.log2(w)) - 1][POOL_HALO - half:POOL_HALO - half + tm]
            parts.append(window_mean(s, half))
        mean = jnp.where(low, parts[0], parts[1])
        outs.append(mean - xs[POOL_HALO:POOL_HALO + tm])
    d = jnp.concatenate(outs, axis=-1).astype(jnp.bfloat16)
    return (_dot(d, w_ref[...]) * scale_ref[...]).astype(jnp.bfloat16)


def _inproj_kernel(*refs, pending, seq_len):
    n_stream = 7 if pending else 3
    stream, refs = refs[:n_stream], refs[n_stream:]
    if pending:
        h_ref, y_ref, g_ref, hp_ref, hn_ref, yp_ref, yn_ref = stream
        hres_ref, refs = refs[-2], refs[:-2] + refs[-1:]
        x = h_ref[...].astype(jnp.float32) + g_ref[...] * _moe_out(y_ref)
        hres_ref[...] = x.astype(hres_ref.dtype)
        before = hp_ref[...].astype(jnp.float32) + g_ref[...] * _moe_out(yp_ref)
        after = hn_ref[...].astype(jnp.float32) + g_ref[...] * _moe_out(yn_ref)
    else:
        h_ref, hp_ref, hn_ref = stream
        x = h_ref[...].astype(jnp.float32)
        before = hp_ref[...].astype(jnp.float32)
        after = hn_ref[...].astype(jnp.float32)
    x_halo = jnp.concatenate([before[HALO_BLOCK - POOL_HALO:], after[:POOL_HALO]], axis=0)
    (n1_ref, sh_ref, sc_ref, w_ref, qg_ref, kg_ref, sgn_ref, sgw_ref, sgb_ref,
     pw_ref, ps_ref, pool_ref, q_ref, k_ref, v_ref, sg_ref, xe_ref) = refs
    tm = h_ref.shape[0]
    i = pl.program_id(0)
    hn = _norm_modulate(x, n1_ref, sh_ref, sc_ref)
    hn_halo = _norm_modulate(x_halo, n1_ref, sh_ref, sc_ref)

    a_halo = _dot(hn_halo, w_ref[:, 0:Q_OFF])
    xe_ref[0:POOL_HALO, :] = jnp.where(i > 0, a_halo[:POOL_HALO], 0.0)
    xe_ref[POOL_HALO:POOL_HALO + tm, :] = _dot(hn, w_ref[:, 0:Q_OFF])
    xe_ref[POOL_HALO + tm:, :] = jnp.where(i < pl.num_programs(0) - 1, a_halo[POOL_HALO:], 0.0)

    a_g = _dot(hn, w_ref[:, G_OFF:IN_DIM])
    a_u = _dot(hn, w_ref[:, U_OFF:G_OFF])
    a_q = _dot(hn, w_ref[:, Q_OFF:K_OFF])
    a_k = _dot(hn, w_ref[:, K_OFF:V_OFF])
    _store_keys(v_ref, _dot(hn, w_ref[:, V_OFF:U_OFF]))

    gv = _gelu_tanh(a_g)
    q_ref[...] = (a_q * _head_rms_scale(a_q) * qg_ref[...]).astype(jnp.bfloat16)
    _store_keys(k_ref, a_k * _head_rms_scale(a_k) * kg_ref[...])

    u = _gelu_tanh(a_u)
    vn = (gv * _head_rms_scale(gv) * sgn_ref[...]).astype(jnp.bfloat16)
    low = lax.broadcasted_iota(jnp.int32, (SG_CHUNK, LANES), 1) < HEAD_DIM
    for c in range(tm // SG_CHUNK):
        rows = slice(c * SG_CHUNK, (c + 1) * SG_CHUNK)
        for s in range(SG_DIM // LANES):
            cols = slice(s * LANES, (s + 1) * LANES)
            m = _dot(sgw_ref[s], vn[rows, cols]) + sgb_ref[s]
            mixed = jnp.where(low, m[:SG_CHUNK], m[SG_CHUNK:])
            sg_ref[rows, cols] = (u[rows, cols] * mixed).astype(jnp.bfloat16)

    pool_ref[...] = _pool_mix(xe_ref, pw_ref, ps_ref, tm, seq_len)


def _inproj_call(stream, n1, sh, sc, w_in, layer, qg, kg, sgn, sgw, sgb, pool_w, pool_scale, tm, strips):
    pending = len(stream) == 3
    n = stream[0].shape[0]
    if strips:
        kv_spec = pl.BlockSpec((N_STRIPS, tm // N_STRIPS, NA_DIM), lambda i: (0, i, 0))
        kv_shape = jax.ShapeDtypeStruct((N_STRIPS, n // N_STRIPS, NA_DIM), jnp.bfloat16)
    else:
        kv_spec = pl.BlockSpec((tm, NA_DIM), lambda i: (i, 0))
        kv_shape = jax.ShapeDtypeStruct((n, NA_DIM), jnp.bfloat16)
    row = lambda i: (i, 0)
    fixed2 = lambda i: (0, 0)
    fixed3 = lambda i: (0, 0, 0)
    vec = lambda w: pl.BlockSpec((1, w), fixed2)
    rows = pl.BlockSpec((tm, D_MODEL), row)
    per_tile = tm // HALO_BLOCK
    before = lambda i: (jnp.maximum(i * per_tile - 1, 0), 0)
    after = lambda i: (jnp.minimum((i + 1) * per_tile, n // HALO_BLOCK - 1), 0)
    halo = lambda w, m: pl.BlockSpec((HALO_BLOCK, w), m)
    h = stream[0]
    if pending:
        y, g = stream[1], stream[2]
        yw = y.shape[1]
        args = [h, y, g, h, h, y, y]
        stream_specs = [rows, pl.BlockSpec((tm, yw), row), vec(D_MODEL),
                        halo(D_MODEL, before), halo(D_MODEL, after), halo(yw, before), halo(yw, after)]
    else:
        args = [h, h, h]
        stream_specs = [rows, halo(D_MODEL, before), halo(D_MODEL, after)]
    extra_out_specs = [rows] if pending else []
    extra_out_shape = [jax.ShapeDtypeStruct((n, D_MODEL), STREAM_DTYPE)] if pending else []
    return pl.pallas_call(
        functools.partial(_inproj_kernel, pending=pending, seq_len=n),
        grid=(n // tm,),
        in_specs=stream_specs + [
            vec(D_MODEL), vec(D_MODEL), vec(D_MODEL),
            pl.BlockSpec((None, D_MODEL, IN_DIM), lambda i: (layer, 0, 0)),
            vec(NA_DIM), vec(NA_DIM),
            vec(SG_DIM),
            pl.BlockSpec((SG_DIM // LANES, 2 * SG_CHUNK, SG_CHUNK), fixed3),
            pl.BlockSpec((SG_DIM // LANES, 2 * SG_CHUNK, LANES), fixed3),
            pl.BlockSpec((POOL_DIM, POOL_DIM), fixed2),
            vec(POOL_DIM),
        ],
        out_specs=[
            pl.BlockSpec((tm, POOL_DIM), row),
            pl.BlockSpec((tm, NA_DIM), row),
            kv_spec,
            kv_spec,
            pl.BlockSpec((tm, SG_DIM), row),
        ] + extra_out_specs,
        out_shape=[
            jax.ShapeDtypeStruct((n, POOL_DIM), jnp.bfloat16),
            jax.ShapeDtypeStruct((n, NA_DIM), jnp.bfloat16),
            kv_shape,
            kv_shape,
            jax.ShapeDtypeStruct((n, SG_DIM), jnp.bfloat16),
        ] + extra_out_shape,
        scratch_shapes=[pltpu.VMEM((tm + 2 * POOL_HALO, POOL_DIM), jnp.float32)],
        compiler_params=_cparams(("arbitrary",)),
        name="inproj",
    )(*args, n1, sh, sc, w_in, qg, kg, sgn, sgw, sgb, pool_w, pool_scale)


NA_ROWS_PER_BLOCK = 64
NA_GROUP_ROWS = 4
NA_WINDOW_ROWS = NA_GROUP_ROWS + NA_WIN_ROWS
NA_BLOCK = NA_ROWS_PER_BLOCK * GRID_W
NA_GROUP = NA_GROUP_ROWS * GRID_W
NA_HALF_COLS = GRID_W // 2
NA_CHAIN = NA_GROUP_ROWS * NA_HALF_COLS
NA_HALF_STRIPS = 5
NA_HALF_COL0 = (0, GRID_W - NA_HALF_STRIPS * STRIP_W)
NA_RUN = NA_WINDOW_ROWS * STRIP_W
NA_LOCAL = 512
NA_STRIP_BLOCK = NA_ROWS_PER_BLOCK * STRIP_W
NA_STRIP_HALO = (NA_WIN_ROWS // 2) * STRIP_W
NA_EDGE_FIRST, NA_EDGE_NONE, NA_EDGE_LAST = 0, 1, 2


def _stack_heads(x, low):
    zero = jnp.zeros_like(x)
    return jnp.concatenate([jnp.where(low, x, zero), jnp.where(low, zero, x)], axis=0)


def _natten_kernel(q_ref, kp_ref, kc_ref, kn_ref, vp_ref, vc_ref, vn_ref, kx_ref, vx_ref, bias_ref,
                   wg_ref, wu_ref, wd_ref,
                   o_ref, wg_bf_ref, wu_bf_ref, wd_bf_ref, kwin_ref, vwin_ref, vxe_ref, *, grid_rows):
    b = pl.program_id(1)
    wg_bf_ref[...] = wg_ref[...].astype(jnp.bfloat16)
    wu_bf_ref[...] = wu_ref[...].astype(jnp.bfloat16)
    wd_bf_ref[...] = wd_ref[...].astype(jnp.bfloat16)
    top, bottom = NA_STRIP_HALO, NA_STRIP_HALO + NA_STRIP_BLOCK
    kwin_ref[:, 0:top, :] = kp_ref[...]
    kwin_ref[:, top:bottom, :] = kc_ref[...]
    kwin_ref[:, bottom:, :] = kn_ref[...]
    vwin_ref[:, 0:top, 0:LANES] = vp_ref[...]
    vwin_ref[:, top:bottom, 0:LANES] = vc_ref[...]
    vwin_ref[:, bottom:, 0:LANES] = vn_ref[...]
    vwin_ref[:, :, LANES:] = jnp.ones(vwin_ref.shape[:2] + (LANES,), jnp.bfloat16)
    vxe_ref[:, 0:LANES] = vx_ref[...]
    vxe_ref[:, LANES:] = jnp.ones((vxe_ref.shape[0], LANES), jnp.bfloat16)
    low_q = lax.broadcasted_iota(jnp.int32, (NA_CHAIN, LANES), 1) < HEAD_DIM
    n_pad = NA_LOCAL - NA_HALF_STRIPS * NA_RUN

    def window_start(g):
        r0 = b * NA_ROWS_PER_BLOCK + g * NA_GROUP_ROWS
        ws = jnp.clip(r0 - NA_WIN_ROWS // 2, 0, grid_rows - NA_WINDOW_ROWS)
        edge = jnp.where(r0 == 0, NA_EDGE_FIRST,
                         jnp.where(r0 == grid_rows - NA_GROUP_ROWS, NA_EDGE_LAST, NA_EDGE_NONE))
        start = pl.multiple_of((ws - b * NA_ROWS_PER_BLOCK + NA_WIN_ROWS // 2) * STRIP_W, NA_STRIP_HALO)
        return start, edge

    def local_window(win_ref, g, half):
        start, _ = window_start(g)
        s0 = NA_HALF_COL0[half] // STRIP_W
        runs = [win_ref[s, pl.ds(start, NA_RUN), :] for s in range(s0, s0 + NA_HALF_STRIPS)]
        return jnp.concatenate(runs + [jnp.zeros((n_pad, win_ref.shape[2]), jnp.bfloat16)], axis=0)

    def query_rows(g, half, j):
        first = g * NA_GROUP + j * GRID_W + half * NA_HALF_COLS
        return slice(first, first + NA_HALF_COLS)

    def scores(c):
        g, half = divmod(c, 2)
        _, edge = window_start(g)
        qh = jnp.concatenate([q_ref[query_rows(g, half, j), :] for j in range(NA_GROUP_ROWS)], axis=0)
        lhs = _stack_heads(qh, low_q)
        kl = local_window(kwin_ref, g, half)
        return jnp.concatenate([_dot_nt(lhs, kl).astype(jnp.bfloat16) + bias_ref[edge, half],
                                _dot_nt(lhs, kx_ref[...]).astype(jnp.bfloat16)], axis=-1)

    n_chains = 2 * (NA_ROWS_PER_BLOCK // NA_GROUP_ROWS)
    s_next = scores(0)
    for c in range(n_chains):
        s = s_next
        if c + 1 < n_chains:
            s_next = scores(c + 1)
        g, half = divmod(c, 2)
        vl = local_window(vwin_ref, g, half)
        m = jnp.max(s, axis=-1, keepdims=True)
        pb = jnp.exp2(s - m)
        o = _dot(pb[:, :NA_LOCAL], vl) + _dot(pb[:, NA_LOCAL:], vxe_ref[...])
        o = o[:, :LANES] * (1.0 / o[:, LANES:])
        o = jnp.where(low_q, o[:NA_CHAIN], o[NA_CHAIN:]).astype(jnp.bfloat16)
        for j in range(NA_GROUP_ROWS):
            o_ref[query_rows(g, half, j), :] = o[j * NA_HALF_COLS:(j + 1) * NA_HALF_COLS]


def _natten_call(q, k, v, k_ctx, v_ctx, bias, w_gate, w_up, w_down, layer):
    n = q.shape[0]
    grid_rows = n // GRID_W
    assert grid_rows % NA_ROWS_PER_BLOCK == 0 and grid_rows >= 2 * NA_ROWS_PER_BLOCK
    nblk = n // NA_BLOCK
    steps = HEAD_PAIRS * nblk
    depth = w_gate.shape[0]
    up_rows = N_EXPERTS * D_MODEL
    down_rows = N_EXPERTS * D_EXPERT
    assert up_rows % steps == 0 and down_rows % steps == 0
    wg2 = w_gate.reshape(depth * up_rows, D_EXPERT)
    wu2 = w_up.reshape(depth * up_rows, D_EXPERT)
    wd2 = w_down.reshape(depth * down_rows, D_MODEL)
    up_in = pl.BlockSpec((up_rows // steps, D_EXPERT), lambda p, b: (layer * steps + p * nblk + b, 0))
    down_in = pl.BlockSpec((down_rows // steps, D_MODEL), lambda p, b: (layer * steps + p * nblk + b, 0))
    up_out = pl.BlockSpec((up_rows // steps, D_EXPERT), lambda p, b: (p * nblk + b, 0))
    down_out = pl.BlockSpec((down_rows // steps, D_MODEL), lambda p, b: (p * nblk + b, 0))
    n_halo = n // N_STRIPS // NA_STRIP_HALO
    hb = NA_STRIP_BLOCK // NA_STRIP_HALO
    rows = pl.BlockSpec((NA_BLOCK, LANES), lambda p, b: (b, p))
    cur = pl.BlockSpec((N_STRIPS, NA_STRIP_BLOCK, LANES), lambda p, b: (0, b, p))
    prev = pl.BlockSpec((N_STRIPS, NA_STRIP_HALO, LANES), lambda p, b: (0, jnp.maximum(b * hb - 1, 0), p))
    nxt = pl.BlockSpec((N_STRIPS, NA_STRIP_HALO, LANES),
                       lambda p, b: (0, jnp.minimum((b + 1) * hb, n_halo - 1), p))
    ctx = pl.BlockSpec((k_ctx.shape[0], LANES), lambda p, b: (0, p))
    win_rows = NA_STRIP_BLOCK + 2 * NA_STRIP_HALO
    att, wg_bf, wu_bf, wd_bf = pl.pallas_call(
        functools.partial(_natten_kernel, grid_rows=grid_rows),
        grid=(HEAD_PAIRS, nblk),
        in_specs=[rows, prev, cur, nxt, prev, cur, nxt, ctx, ctx,
                  pl.BlockSpec((None, 3, 2, 2 * NA_CHAIN, NA_LOCAL), lambda p, b: (p, 0, 0, 0, 0)),
                  up_in, up_in, down_in],
        out_specs=[rows, up_out, up_out, down_out],
        out_shape=[jax.ShapeDtypeStruct((n, NA_DIM), jnp.bfloat16),
                   jax.ShapeDtypeStruct((up_rows, D_EXPERT), jnp.bfloat16),
                   jax.ShapeDtypeStruct((up_rows, D_EXPERT), jnp.bfloat16),
                   jax.ShapeDtypeStruct((down_rows, D_MODEL), jnp.bfloat16)],
        scratch_shapes=[pltpu.VMEM((N_STRIPS, win_rows, LANES), jnp.bfloat16),
                        pltpu.VMEM((N_STRIPS, win_rows, 2 * LANES), jnp.bfloat16),
                        pltpu.VMEM((k_ctx.shape[0], 2 * LANES), jnp.bfloat16)],
        compiler_params=_cparams(("arbitrary", "arbitrary")),
        name="natten",
    )(q, k, k, k, v, v, v, k_ctx, v_ctx, bias, wg2, wu2, wd2)
    return (att, wg_bf.reshape(N_EXPERTS, D_MODEL, D_EXPERT), wu_bf.reshape(N_EXPERTS, D_MODEL, D_EXPERT),
            wd_bf.reshape(N_EXPERTS, D_EXPERT, D_MODEL))


def _natten_bias(rpb):
    cols = np.arange(GRID_W)
    col_start = np.clip(cols - NA_WIN_COLS // 2, 0, GRID_W - NA_WIN_COLS)
    kc = np.arange(GRID_W)
    in_win = (kc[None, :] >= col_start[:, None]) & (kc[None, :] < col_start[:, None] + NA_WIN_COLS)
    dc = kc[None, :] - cols[:, None] + NA_WIN_COLS - 1
    sel = (np.arange(2 * NA_WIN_COLS - 1)[:, None, None] == dc[None]) & in_win[None]
    t2 = jnp.einsum("hdj,jqk->hdqk", rpb, jnp.asarray(sel, jnp.float32), precision=lax.Precision.HIGHEST)
    t2 = jnp.where(in_win[None, None], t2 * LOG2_E, NEG_BIG)
    neg = jnp.full((NA_HEADS, 1, GRID_W, GRID_W), NEG_BIG, jnp.float32)
    t2e = jnp.concatenate([neg, t2, neg], axis=1)
    u = jnp.concatenate([t2e[:, :-1], t2e[:, 1:]], axis=-1)
    u = u.reshape(HEAD_PAIRS, 2, 2 * NA_WIN_ROWS, GRID_W, LANES)

    place = np.zeros((2, NA_WINDOW_ROWS * GRID_W, NA_LOCAL), np.float32)
    for half, c0 in enumerate(NA_HALF_COL0):
        for a in range(NA_WINDOW_ROWS):
            for kcol in range(c0, c0 + NA_HALF_STRIPS * STRIP_W):
                s, c8 = divmod(kcol - c0, STRIP_W)
                place[half, a * GRID_W + kcol, s * NA_RUN + a * STRIP_W + c8] = 1.0
    outside = np.full((3, NA_GROUP_ROWS, NA_LOCAL), NEG_BIG, np.float32)
    for edge in (NA_EDGE_FIRST, NA_EDGE_NONE, NA_EDGE_LAST):
        for j in range(NA_GROUP_ROWS):
            lo, _ = _window_rows(edge, j)
            for s in range(NA_HALF_STRIPS):
                outside[edge, j, s * NA_RUN + lo * STRIP_W:s * NA_RUN + (lo + NA_WIN_ROWS) * STRIP_W] = 0.0
    return pl.pallas_call(
        _bias_expand_kernel,
        grid=(HEAD_PAIRS,),
        in_specs=[pl.BlockSpec((None, 2, 2 * NA_WIN_ROWS, GRID_W, LANES), lambda p: (p, 0, 0, 0, 0)),
                  pl.BlockSpec(place.shape, lambda p: (0, 0, 0)),
                  pl.BlockSpec(outside.shape, lambda p: (0, 0, 0))],
        out_specs=pl.BlockSpec((None, 3, 2, 2 * NA_CHAIN, NA_LOCAL), lambda p: (p, 0, 0, 0, 0)),
        out_shape=jax.ShapeDtypeStruct((HEAD_PAIRS, 3, 2, 2 * NA_CHAIN, NA_LOCAL), jnp.bfloat16),
        compiler_params=_cparams(("arbitrary",)),
        name="bias_expand",
    )(u, jnp.asarray(place, jnp.bfloat16), jnp.asarray(outside))


def _window_rows(edge, j):
    if edge == NA_EDGE_FIRST:
        return 0, NA_WIN_ROWS - 1 - j
    if edge == NA_EDGE_NONE:
        return j, NA_WIN_ROWS // 2 - 1
    return NA_WINDOW_ROWS - NA_WIN_ROWS, NA_WIN_ROWS // 2 - 1 - j


def _bias_expand_kernel(u_ref, place_ref, outside_ref, o_ref):
    low = lax.broadcasted_iota(jnp.int32, (NA_HALF_COLS, LANES), 1) < GRID_W
    zero = jnp.zeros((NA_HALF_COLS, LANES), jnp.float32)
    for edge in (NA_EDGE_FIRST, NA_EDGE_NONE, NA_EDGE_LAST):
        for half in range(2):
            q0 = half * NA_HALF_COLS
            blocks, masks = [], []
            for hd in range(2):
                for j in range(NA_GROUP_ROWS):
                    lo, base = _window_rows(edge, j)
                    tiles = []
                    for i in range(NA_WINDOW_ROWS // 2):
                        a0, a1 = 2 * i, 2 * i + 1
                        ok0 = lo <= a0 < lo + NA_WIN_ROWS
                        ok1 = lo <= a1 < lo + NA_WIN_ROWS
                        if not (ok0 or ok1):
                            tile = zero
                        else:
                            tile = u_ref[hd, base + a1 - lo, q0:q0 + NA_HALF_COLS, :]
                            if not ok0:
                                tile = jnp.where(low, zero, tile)
                            if not ok1:
                                tile = jnp.where(low, tile, zero)
                        tiles.append(tile)
                    blocks.append(jnp.concatenate(tiles, axis=-1))
                    masks.append(jnp.broadcast_to(outside_ref[edge, j:j + 1, :], (NA_HALF_COLS, NA_LOCAL)))
            lhs = jnp.concatenate(blocks, axis=0).astype(jnp.bfloat16)
            placed = _dot(lhs, place_ref[half]) + jnp.concatenate(masks, axis=0)
            o_ref[edge, half] = placed.astype(o_ref.dtype)


def _ctxatt_kernel(q_ref, k_ref, v_ref, o_ref):
    lc = q_ref.shape[0]
    low = lax.broadcasted_iota(jnp.int32, (lc, LANES), 1) < HEAD_DIM
    lhs = _stack_heads(q_ref[...], low)
    s = _dot_nt(lhs, k_ref[...])
    m = jnp.max(s, axis=-1, keepdims=True)
    p = jnp.exp2(s - m)
    denom = jnp.sum(p, axis=-1, keepdims=True)
    o = _dot(p.astype(jnp.bfloat16), v_ref[...]) * (1.0 / denom)
    o_ref[...] = jnp.where(low, o[:lc], o[lc:]).astype(jnp.bfloat16)


def _ctxatt_call(q, k, v):
    lc = q.shape[0]
    spec = pl.BlockSpec((lc, LANES), lambda p: (0, p))
    return pl.pallas_call(
        _ctxatt_kernel,
        grid=(HEAD_PAIRS,),
        in_specs=[spec, spec, spec],
        out_specs=spec,
        out_shape=jax.ShapeDtypeStruct((lc, NA_DIM), jnp.bfloat16),
        compiler_params=_cparams(("arbitrary",)),
        name="ctxatt",
    )(q, k, v)


OUTPROJ_CHAIN = 256
OUTPROJ_TILE = 1024
INPROJ_TILE = 1024
RESIDUAL_TILE = 2048


def _outproj_kernel(h_ref, mp_ref, att_ref, sg_ref, wo_ref, g1_ref, n2_ref, sh_ref, sc_ref,
                    wr_ref, br_ref, h1_ref, pay_ref, route_ref):
    chains = [slice(c * OUTPROJ_CHAIN, (c + 1) * OUTPROJ_CHAIN) for c in range(h_ref.shape[0] // OUTPROJ_CHAIN)]
    h1s = []
    for rows in chains:
        mix = (_dot(mp_ref[rows, :], wo_ref[0:POOL_DIM, :])
               + _dot(att_ref[rows, :], wo_ref[POOL_DIM:POOL_DIM + NA_DIM, :])
               + _dot(sg_ref[rows, :], wo_ref[POOL_DIM + NA_DIM:, :]))
        h1 = h_ref[rows, :].astype(jnp.float32) + g1_ref[...] * mix
        h1_ref[rows, :] = h1.astype(h1_ref.dtype)
        h1s.append(h1)
    for rows, h1 in zip(chains, h1s):
        _outproj_route(rows, h1, n2_ref, sh_ref, sc_ref, wr_ref, br_ref, pay_ref, route_ref)


def _outproj_route(rows, h1, n2_ref, sh_ref, sc_ref, wr_ref, br_ref, pay_ref, route_ref):
    tm = OUTPROJ_CHAIN
    ms = jnp.mean(h1 * h1, axis=-1, keepdims=True)
    hm = (h1 * lax.rsqrt(ms + EPS)) * (n2_ref[...] * (1.0 + sc_ref[...])) + sh_ref[...]
    pay_ref[rows, 0:HALF_D] = _pack_halves(hm)

    hm_hi = hm.astype(jnp.bfloat16)
    lt = _dot_nt(wr_ref[...], hm_hi)
    logits = lt[:N_EXPERTS] + lt[N_EXPERTS:] + br_ref[...]
    e = jnp.exp(logits - jnp.max(logits, axis=0, keepdims=True))

    best = ga = gb = e1 = e2 = cls = None
    for c in range(N_CLASSES):
        a, b2 = int(CLASS_E1[c]), int(CLASS_E2[c])
        ea, eb = e[a:a + 1, :], e[b2:b2 + 1, :]
        s = ea + eb
        if best is None:
            best, ga, gb = s, ea, eb
            e1 = jnp.full_like(s, float(a))
            e2 = jnp.full_like(s, float(b2))
            cls = jnp.zeros_like(s)
        else:
            better = s > best
            best = jnp.where(better, s, best)
            ga = jnp.where(better, ea, ga)
            gb = jnp.where(better, eb, gb)
            e1 = jnp.where(better, float(a), e1)
            e2 = jnp.where(better, float(b2), e2)
            cls = jnp.where(better, float(c), cls)
    inv = 1.0 / best
    row = lax.broadcasted_iota(jnp.int32, (ROUTE_ROWS, tm), 0)
    rec = jnp.where(row == 0, ga * inv,
          jnp.where(row == 1, gb * inv,
          jnp.where(row == 2, e1,
          jnp.where(row == 3, e2,
          jnp.where(row == 4, cls, 0.0)))))
    route_ref[:, rows] = rec
    wide = jnp.concatenate([rec, jnp.zeros((LANES - ROUTE_ROWS, tm), jnp.float32)], axis=0)
    pay_ref[rows, HALF_D:] = pltpu.bitcast(wide.T, jnp.uint32)


def _outproj_call(h, mp, att, sg, w_out, layer, g1, n2, sh2, sc2, wr, br, tm):
    n = h.shape[0]
    row = lambda i: (i, 0)
    fixed = lambda i: (0, 0)
    vec = pl.BlockSpec((1, D_MODEL), fixed)
    return pl.pallas_call(
        _outproj_kernel,
        grid=(n // tm,),
        in_specs=[
            pl.BlockSpec((tm, D_MODEL), row),
            pl.BlockSpec((tm, POOL_DIM), row),
            pl.BlockSpec((tm, NA_DIM), row),
            pl.BlockSpec((tm, SG_DIM), row),
            pl.BlockSpec((None, D_MODEL, D_MODEL), lambda i: (layer, 0, 0)),
            vec, vec, vec, vec,
            pl.BlockSpec((2 * N_EXPERTS, D_MODEL), fixed),
            pl.BlockSpec((N_EXPERTS, 1), fixed),
        ],
        out_specs=[
            pl.BlockSpec((tm, D_MODEL), row),
            pl.BlockSpec((tm, PAYLOAD_W), row),
            pl.BlockSpec((ROUTE_ROWS, tm), lambda i: (0, i)),
        ],
        out_shape=[
            jax.ShapeDtypeStruct((n, D_MODEL), STREAM_DTYPE),
            jax.ShapeDtypeStruct((n, PAYLOAD_W), jnp.uint32),
            jax.ShapeDtypeStruct((ROUTE_ROWS, n), jnp.float32),
        ],
        compiler_params=_cparams(("arbitrary",)),
        name="outproj",
    )(h, mp, att, sg, w_out, g1, n2, sh2, sc2, wr, br)


def _payload_parts(pay_ref, rows=slice(None)):
    lo, hi = _unpack_halves(pay_ref[rows, 0:HALF_D])
    x = jnp.concatenate([lo, hi], axis=-1).astype(jnp.bfloat16)
    return x, pltpu.bitcast(pay_ref[rows, HALF_D:], jnp.float32)


def _expert_pair(x, ga, gb, wga, wua, wda, wgb, wub, wdb):
    ha = (_silu(_dot(x, wga)) * _dot(x, wua) * ga).astype(jnp.bfloat16)
    hb = (_silu(_dot(x, wgb)) * _dot(x, wub) * gb).astype(jnp.bfloat16)
    return _dot(ha, wda) + _dot(hb, wdb)


MOE_QUARTER = MOE_TM // 8


def _moe_sorted_kernel(blk_ref, e1_ref, e2_ref, quarters_ref, pay_ref,
                       wga_ref, wua_ref, wda_ref, wgb_ref, wub_ref, wdb_ref, o_ref):
    quarters = quarters_ref[pl.program_id(0)]

    def run(rows):
        x, route = _payload_parts(pay_ref, rows)
        y = _expert_pair(x, route[:, 0:1], route[:, 1:2], wga_ref[...], wua_ref[...], wda_ref[...],
                         wgb_ref[...], wub_ref[...], wdb_ref[...])
        o_ref[rows, :] = _pack_halves(y)

    for used in range(1, MOE_TM // MOE_QUARTER + 1):
        @pl.when(quarters == used)
        def _(used=used):
            run(slice(0, used * MOE_QUARTER))


def _moe_sorted_call(blk, e1, e2, quarters, pay_sorted, wg, wu, wd):
    n_tiles = blk.shape[0]
    rows = lambda i, blk, e1, e2, quarters: (blk[i], 0)
    wa = lambda i, blk, e1, e2, quarters: (e1[i], 0, 0)
    wb = lambda i, blk, e1, e2, quarters: (e2[i], 0, 0)
    up = lambda m: pl.BlockSpec((None, D_MODEL, D_EXPERT), m)
    down = lambda m: pl.BlockSpec((None, D_EXPERT, D_MODEL), m)
    return pl.pallas_call(
        _moe_sorted_kernel,
        grid_spec=pltpu.PrefetchScalarGridSpec(
            num_scalar_prefetch=4,
            grid=(n_tiles,),
            in_specs=[pl.BlockSpec((MOE_TM, PAYLOAD_W), rows),
                      up(wa), up(wa), down(wa), up(wb), up(wb), down(wb)],
            out_specs=pl.BlockSpec((MOE_TM, HALF_D), rows),
        ),
        out_shape=jax.ShapeDtypeStruct((n_tiles * MOE_TM, HALF_D), jnp.uint32),
        compiler_params=_cparams(("arbitrary",)),
        name="moe_sorted",
    )(blk, e1, e2, quarters, pay_sorted, wg, wu, wd, wg, wu, wd)


DENSE_EXPERTS_PER_STEP = GROUP_SIZE


def _moe_dense_kernel(*refs, first_step, resume):
    if resume:
        pay_ref, wg_ref, wu_ref, wd_ref, acc_ref, o_ref = refs
    else:
        pay_ref, wg_ref, wu_ref, wd_ref, o_ref = refs
    step = pl.program_id(0)

    @pl.when(step == 0)
    def _():
        o_ref[...] = acc_ref[...] if resume else jnp.zeros_like(o_ref)

    x, route = _payload_parts(pay_ref)
    y = None
    for k in range(DENSE_EXPERTS_PER_STEP):
        ef = ((first_step + step) * DENSE_EXPERTS_PER_STEP + k).astype(jnp.float32)
        gate = (jnp.where(route[:, 2:3] == ef, route[:, 0:1], 0.0)
                + jnp.where(route[:, 3:4] == ef, route[:, 1:2], 0.0))
        he = (_silu(_dot(x, wg_ref[k])) * _dot(x, wu_ref[k]) * gate).astype(jnp.bfloat16)
        yk = _dot(he, wd_ref[k])
        y = yk if y is None else y + yk
    o_ref[...] += y


def _moe_dense_call(pay, wg, wu, wd, first_step, n_steps, acc=None):
    n = pay.shape[0]
    per = DENSE_EXPERTS_PER_STEP
    whole = pl.BlockSpec((n, D_MODEL), lambda e: (0, 0))
    return pl.pallas_call(
        functools.partial(_moe_dense_kernel, first_step=first_step, resume=acc is not None),
        grid=(n_steps,),
        in_specs=[pl.BlockSpec((n, PAYLOAD_W), lambda e: (0, 0)),
                  pl.BlockSpec((per, D_MODEL, D_EXPERT), lambda e: (first_step + e, 0, 0)),
                  pl.BlockSpec((per, D_MODEL, D_EXPERT), lambda e: (first_step + e, 0, 0)),
                  pl.BlockSpec((per, D_EXPERT, D_MODEL), lambda e: (first_step + e, 0, 0))]
                 + ([whole] if acc is not None else []),
        out_specs=whole,
        out_shape=jax.ShapeDtypeStruct((n, D_MODEL), jnp.float32),
        compiler_params=_cparams(("arbitrary",)),
        name="moe_dense",
    )(pay, wg, wu, wd, *([acc] if acc is not None else []))


def _moe_out(y_ref):
    if y_ref.dtype == jnp.uint32:
        return jnp.concatenate(_unpack_halves(y_ref[...]), axis=-1)
    return y_ref[...]


def _residual_kernel(h_ref, y_ref, g_ref, o_ref):
    o_ref[...] = h_ref[...].astype(jnp.float32) + g_ref[...] * _moe_out(y_ref)


def _residual_call(h, y, g, tm):
    n = h.shape[0]
    row = pl.BlockSpec((tm, D_MODEL), lambda i: (i, 0))
    return pl.pallas_call(
        _residual_kernel,
        grid=(n // tm,),
        in_specs=[row, pl.BlockSpec((tm, y.shape[1]), lambda i: (i, 0)),
                  pl.BlockSpec((1, D_MODEL), lambda i: (0, 0))],
        out_specs=row,
        out_shape=jax.ShapeDtypeStruct((n, D_MODEL), jnp.float32),
        compiler_params=_cparams(("arbitrary",)),
        name="residual",
    )(h, y, g)


SC_ROWS = 128


SC_CORES = 2
SC_SUBCORES = 16
SC_WORKERS = SC_CORES * SC_SUBCORES


def _sc_mesh():
    return plsc.VectorSubcoreMesh(core_axis_name="core", subcore_axis_name="subcore")


def _sc_worker():
    return lax.axis_index("subcore") * SC_CORES + lax.axis_index("core")


def _scatter_rows(x, dest, n_out):
    n, w = x.shape

    per_worker = n // SC_WORKERS
    assert per_worker % SC_ROWS == 0

    @functools.partial(pl.kernel, out_type=jax.ShapeDtypeStruct((n_out, w), x.dtype), mesh=_sc_mesh(),
                       scratch_types=[pltpu.VMEM((SC_ROWS,), jnp.int32), pltpu.VMEM((SC_ROWS, w), x.dtype)])
    def scatter(x_hbm, i_hbm, o_hbm, idx_v, rows_v):
        first = _sc_worker() * per_worker

        @pl.loop(0, per_worker // SC_ROWS)
        def _(i):
            base = pl.multiple_of(first + i * SC_ROWS, SC_ROWS)
            pltpu.sync_copy(i_hbm.at[pl.ds(base, SC_ROWS)], idx_v)
            pltpu.sync_copy(x_hbm.at[pl.ds(base, SC_ROWS)], rows_v)
            pltpu.sync_copy(rows_v, o_hbm.at[idx_v])

    return scatter(x, dest)


def _gather_rows(x, src):
    n = src.shape[0]
    w = x.shape[1]

    per_worker = n // SC_WORKERS
    assert per_worker % SC_ROWS == 0

    @functools.partial(pl.kernel, out_type=jax.ShapeDtypeStruct((n, w), x.dtype), mesh=_sc_mesh(),
                       scratch_types=[pltpu.VMEM((SC_ROWS,), jnp.int32), pltpu.VMEM((SC_ROWS, w), x.dtype)])
    def gather(x_hbm, i_hbm, o_hbm, idx_v, rows_v):
        first = _sc_worker() * per_worker

        @pl.loop(0, per_worker // SC_ROWS)
        def _(i):
            base = pl.multiple_of(first + i * SC_ROWS, SC_ROWS)
            pltpu.sync_copy(i_hbm.at[pl.ds(base, SC_ROWS)], idx_v)
            pltpu.sync_copy(x_hbm.at[idx_v], rows_v)
            pltpu.sync_copy(rows_v, o_hbm.at[pl.ds(base, SC_ROWS)])

    return gather(x, src)


def _routing_plan(cls, n_tiles):
    onehot = (cls[:, None] == jnp.arange(N_CLASSES, dtype=jnp.int32)[None, :]).astype(jnp.int32)
    counts = jnp.sum(onehot, axis=0)
    rank = jnp.sum((jnp.cumsum(onehot, axis=0) - onehot) * onehot, axis=1)
    tiles = (counts + MOE_TM - 1) // MOE_TM
    tile_end = jnp.cumsum(tiles)
    tile_start = tile_end - tiles
    first_len = counts - (tiles - 1) * MOE_TM
    first_of = jnp.sum(onehot * first_len[None, :], axis=1)
    skip = jnp.where(rank >= first_of, MOE_TM - first_of, 0)
    dest = jnp.sum(onehot * tile_start[None, :], axis=1) * MOE_TM + rank + skip
    nact = tile_end[-1]
    tile = jnp.arange(n_tiles, dtype=jnp.int32)
    blk = jnp.minimum(tile, nact - 1)
    tile_cls = jnp.sum((blk[:, None] >= tile_end[None, :]).astype(jnp.int32), axis=1)
    e1 = jnp.asarray(CLASS_E1)[tile_cls]
    e2 = jnp.asarray(CLASS_E2)[tile_cls]
    valid = jnp.where(blk == tile_start[tile_cls], first_len[tile_cls], MOE_TM)
    quarters = jnp.where(tile >= nact, 0, (valid + MOE_QUARTER - 1) // MOE_QUARTER)
    return dest.astype(jnp.int32), blk, e1, e2, quarters.astype(jnp.int32)


def _row_tile(n, prefer=512):
    return next(t for t in (prefer, 512, 256) if n % t == 0)


def kernel(x, c, ctx, c_ctx, w_ada, b_ada, norm1, w_in, pool_w, pool_scale, q_norm, k_norm, rpb,
           sg_w, sg_b, sg_norm, w_out, norm2, w_router, b_router, w_gate, w_up, w_down):
    depth = w_ada.shape[0]
    n = x.shape[1]
    lc = ctx.shape[1]
    bf = jnp.bfloat16
    lat_stream = (x[0],)
    h_ctx = ctx[0]

    cond = jnp.stack([c[0], c_ctx], axis=1)
    mod = _ada_call(cond, w_ada, b_ada)

    wr_t = w_router.T
    wr_hi = wr_t.astype(bf)
    wr_lo = (wr_t - wr_hi.astype(jnp.float32)).astype(bf)
    wr = jnp.concatenate([wr_hi, wr_lo], axis=0)
    br = b_router.reshape(N_EXPERTS, 1)
    n_tiles = n // MOE_TM + N_CLASSES
    w_in_bf = w_in.astype(bf)
    w_out_bf = w_out.astype(bf)

    for l in range(depth):
        last = l == depth - 1
        qg =(q_norm[l] * (HEAD_DIM ** -0.5 * LOG2_E)).reshape(1, NA_DIM)
        kg = k_norm[l].reshape(1, NA_DIM)
        sgn = sg_norm[l].reshape(1, SG_DIM)
        sgw = sg_w[l].astype(bf).reshape(SG_DIM // LANES, 2 * SG_CHUNK, SG_CHUNK)
        sgb = jnp.broadcast_to(sg_b[l].reshape(SG_DIM // LANES, 2 * SG_CHUNK, 1),
                               (SG_DIM // LANES, 2 * SG_CHUNK, LANES))
        pool_bd = jax.scipy.linalg.block_diag(*[pool_w[l, g] for g in range(len(POOL_WINDOWS))]).astype(bf)
        pscale = pool_scale[l].reshape(1, POOL_DIM)
        bias = _natten_bias(rpb[l])
        n1 = norm1[l].reshape(1, D_MODEL)
        n2 = norm2[l].reshape(1, D_MODEL)

        def mods(row):
            return [mod[l, row:row + 1, i * D_MODEL:(i + 1) * D_MODEL] for i in range(6)]

        sh1, sc1, g1, sh2, sc2, g2 = mods(0)
        csh1, csc1, cg1, csh2, csc2, cg2 = mods(1)

        tc = _row_tile(lc)
        mix_pool_c, q_c, k_c, v_c, sg_c = _inproj_call((h_ctx,), n1, csh1, csc1, w_in_bf, l, qg, kg,
                                                       sgn, sgw, sgb, pool_bd, pscale, tc, strips=False)

        outs = _inproj_call(lat_stream, n1, sh1, sc1, w_in_bf, l, qg, kg, sgn, sgw, sgb,
                            pool_bd, pscale, _row_tile(n, INPROJ_TILE), strips=True)
        mix_pool, q, k, v, sg = outs[:5]
        h_lat = outs[5] if len(lat_stream) == 3 else lat_stream[0]
        att, wg_l, wu_l, wd_l = _natten_call(q, k, v, k_c, v_c, bias, w_gate, w_up, w_down, l)
        h1, pay, route = _outproj_call(h_lat, mix_pool, att, sg, w_out_bf, l, g1, n2, sh2, sc2, wr, br,
                                       _row_tile(n, OUTPROJ_TILE))
        cls = route[4].astype(jnp.int32)
        dest, blk, e1, e2, quarters = _routing_plan(cls, n_tiles)
        pay_sorted = _scatter_rows(pay, dest, n_tiles * MOE_TM)

        if not last:
            dense_steps = N_EXPERTS // DENSE_EXPERTS_PER_STEP
            att_c = _ctxatt_call(q_c, k_c, v_c)
            h1_c, pay_c, _ = _outproj_call(h_ctx, mix_pool_c, att_c, sg_c, w_out_bf, l, cg1, n2, csh2, csc2,
                                           wr, br, tc)
            y_c = _moe_dense_call(pay_c, wg_l, wu_l, wd_l, 0, dense_steps // 2)
            quarters, y_c = lax.optimization_barrier((quarters, y_c))

        y_sorted = _moe_sorted_call(blk, e1, e2, quarters, pay_sorted, wg_l, wu_l, wd_l)
        y = _gather_rows(y_sorted, dest)
        lat_stream = (h1, y, g2)

        if not last:
            y_c = _moe_dense_call(pay_c, wg_l, wu_l, wd_l, dense_steps // 2, dense_steps - dense_steps // 2,
                                  acc=y_c)
            h_ctx = _residual_call(h1_c, y_c, cg2, tc)

    return _residual_call(*lat_stream, _row_tile(n, RESIDUAL_TILE))[None]
```

```python
import functools

import jax
import jax.numpy as jnp
import numpy as np
from jax import lax
from jax.experimental import pallas as pl
from jax.experimental.pallas import tpu as pltpu
from jax.experimental.pallas import tpu_sc as plsc

D_MODEL = 1024
GRID_W = 64
HEAD_DIM = 64
POOL_WINDOWS = (2, 4, 8, 16)
POOL_DIM = 256
NA_HEADS = 8
NA_DIM = 512
NA_WIN_ROWS = 8
NA_WIN_COLS = 16
SG_DIM = 256
SG_CHUNK = 128
Q_OFF = POOL_DIM
K_OFF = Q_OFF + NA_DIM
V_OFF = K_OFF + NA_DIM
U_OFF = V_OFF + NA_DIM
G_OFF = U_OFF + SG_DIM
IN_DIM = G_OFF + SG_DIM
N_EXPERTS = 16
GROUP_SIZE = 4
D_EXPERT = 512
EPS = 1e-6

LANES = 128
SUBLANES = 8
HEAD_PAIRS = NA_DIM // LANES
VMEM_LIMIT = 48 * 1024 * 1024

PAIRS = ((0, 1), (0, 2), (1, 2), (1, 3), (0, 3), (2, 3))
N_GROUPS = N_EXPERTS // GROUP_SIZE
N_CLASSES = N_GROUPS * len(PAIRS)
CLASS_E1 = np.array([GROUP_SIZE * g + i for g in range(N_GROUPS) for (i, j) in PAIRS], np.int32)
CLASS_E2 = np.array([GROUP_SIZE * g + j for g in range(N_GROUPS) for (i, j) in PAIRS], np.int32)

ROUTE_ROWS = 8
HALF_D = D_MODEL // 2
PAYLOAD_W = HALF_D + LANES
MOE_TM = 512
NEG_BIG = -1e30
LOG2_E = 1.4426950408889634


def _cparams(sem):
    return pltpu.CompilerParams(dimension_semantics=sem, vmem_limit_bytes=VMEM_LIMIT)


def _dot(a, b):
    return jnp.dot(a, b, preferred_element_type=jnp.float32)


def _dot_nt(a, b):
    return lax.dot_general(a, b, (((1,), (1,)), ((), ())), preferred_element_type=jnp.float32)


def _gelu_tanh(x):
    return 0.5 * x * (1.0 + jnp.tanh(0.7978845608028654 * (x + 0.044715 * (x * x * x))))


def _silu(x):
    return x * (1.0 / (1.0 + jnp.exp(-x)))


def _pack_halves(x):
    w = x.shape[1] // 2
    lo = pltpu.bitcast(x[:, :w].astype(jnp.bfloat16).astype(jnp.float32), jnp.uint32) >> 16
    hi = pltpu.bitcast(x[:, w:].astype(jnp.bfloat16).astype(jnp.float32), jnp.uint32) & jnp.uint32(0xFFFF0000)
    return lo | hi


def _unpack_halves(words):
    lo = pltpu.bitcast(words << 16, jnp.float32)
    hi = pltpu.bitcast(words & jnp.uint32(0xFFFF0000), jnp.float32)
    return lo, hi


def _ada_kernel(cond_ref, w_ref, b_ref, o_ref):
    cond = _silu(cond_ref[...])
    w = w_ref[...]
    rows = [jnp.sum(w * cond[:, r:r + 1], axis=0, keepdims=True) + b_ref[...] for r in range(2)]
    o_ref[...] = jnp.concatenate(rows + [jnp.zeros((SUBLANES - 2, w.shape[1]), jnp.float32)], axis=0)


def _ada_call(cond, w_ada, b_ada):
    depth = w_ada.shape[0]
    tn = 1536
    return pl.pallas_call(
        _ada_kernel,
        grid=(depth, 6 * D_MODEL // tn),
        in_specs=[
            pl.BlockSpec((D_MODEL, 2), lambda l, j: (0, 0)),
            pl.BlockSpec((None, D_MODEL, tn), lambda l, j: (l, 0, j)),
            pl.BlockSpec((None, 1, tn), lambda l, j: (l, 0, j)),
        ],
        out_specs=pl.BlockSpec((None, SUBLANES, tn), lambda l, j: (l, 0, j)),
        out_shape=jax.ShapeDtypeStruct((depth, SUBLANES, 6 * D_MODEL), jnp.float32),
        compiler_params=_cparams(("arbitrary", "arbitrary")),
        name="adaln",
    )(cond, w_ada, b_ada.reshape(depth, 1, 6 * D_MODEL))


def _norm_modulate(x, n_ref, sh_ref, sc_ref):
    ms = jnp.mean(x * x, axis=-1, keepdims=True)
    return ((x * lax.rsqrt(ms + EPS)) * (n_ref[...] * (1.0 + sc_ref[...])) + sh_ref[...]).astype(jnp.bfloat16)


def _head_rms_scale(a):
    low = lax.broadcasted_iota(jnp.int32, (a.shape[0], LANES), 1) < HEAD_DIM
    blocks = []
    for p in range(a.shape[1] // LANES):
        sq = jnp.square(a[:, p * LANES:(p + 1) * LANES])
        s_lo = jnp.sum(jnp.where(low, sq, 0.0), axis=-1, keepdims=True)
        s_hi = jnp.sum(jnp.where(low, 0.0, sq), axis=-1, keepdims=True)
        blocks.append(jnp.where(low, lax.rsqrt(s_lo * (1.0 / HEAD_DIM) + EPS),
                                lax.rsqrt(s_hi * (1.0 / HEAD_DIM) + EPS)))
    return jnp.concatenate(blocks, axis=-1)


STRIP_W = 8
N_STRIPS = GRID_W // STRIP_W


def _store_keys(ref, x):
    if len(ref.shape) == 2:
        ref[...] = x.astype(jnp.bfloat16)
        return
    pair = 2 * STRIP_W
    for s in range(N_STRIPS):
        for rp in range(x.shape[0] // (2 * GRID_W)):
            top = 2 * rp * GRID_W + s * STRIP_W
            rows = jnp.concatenate([x[top:top + STRIP_W], x[top + GRID_W:top + GRID_W + STRIP_W]], axis=0)
            ref[s, rp * pair:(rp + 1) * pair, :] = rows.astype(jnp.bfloat16)


POOL_HALO = 8
HALO_BLOCK = 16
STREAM_DTYPE = jnp.bfloat16


POOL_EDGE = 16


def _pool_mix(xe_ref, w_ref, scale_ref, tm, seq_len):
    low = lax.broadcasted_iota(jnp.int32, (tm, LANES), 1) < HEAD_DIM
    t_edge = pl.program_id(0) * tm + lax.broadcasted_iota(jnp.int32, (POOL_EDGE, LANES), 0)

    def window_mean(s, half):
        mean = s * (1.0 / (2 * half))

        def clip_fix(t):
            count = (jnp.minimum(t + half, seq_len) - jnp.maximum(t - half, 0)).astype(jnp.float32)
            return (2.0 * half) / count

        return jnp.concatenate([mean[:POOL_EDGE] * clip_fix(t_edge),
                                mean[POOL_EDGE:tm - POOL_EDGE],
                                mean[tm - POOL_EDGE:] * clip_fix(t_edge + (tm - POOL_EDGE))], axis=0)

    def window_sums(xs, n_levels):
        sums = []
        s = xs
        for k in range(n_levels):
            step = 1 << k
            s = s[:-step] + s[step:]
            sums.append(s)
        return sums

    outs = []
    for half_block, windows in enumerate(((2, 4), (8, 16))):
        xs = xe_ref[:, half_block * LANES:(half_block + 1) * LANES]
        sums = window_sums(xs, int(np.log2(windows[1])))
        parts = []
        for w in windows:
            half = w // 2
            s = sums[int(np.log2(w)) - 1][POOL_HALO - half:POOL_HALO - half + tm]
            parts.append(window_mean(s, half))
        mean = jnp.where(low, parts[0], parts[1])
        outs.append(mean - xs[POOL_HALO:POOL_HALO + tm])
    d = jnp.concatenate(outs, axis=-1).astype(jnp.bfloat16)
    return (_dot(d, w_ref[...]) * scale_ref[...]).astype(jnp.bfloat16)


def _inproj_kernel(*refs, pending, seq_len):
    n_stream = 7 if pending else 3
    stream, refs = refs[:n_stream], refs[n_stream:]
    if pending:
        h_ref, y_ref, g_ref, hp_ref, hn_ref, yp_ref, yn_ref = stream
        hres_ref, refs = refs[-2], refs[:-2] + refs[-1:]
        x = h_ref[...].astype(jnp.float32) + g_ref[...] * _moe_out(y_ref)
        hres_ref[...] = x.astype(hres_ref.dtype)
        before = hp_ref[...].astype(jnp.float32) + g_ref[...] * _moe_out(yp_ref)
        after = hn_ref[...].astype(jnp.float32) + g_ref[...] * _moe_out(yn_ref)
    else:
        h_ref, hp_ref, hn_ref = stream
        x = h_ref[...].astype(jnp.float32)
        before = hp_ref[...].astype(jnp.float32)
        after = hn_ref[...].astype(jnp.float32)
    x_halo = jnp.concatenate([before[HALO_BLOCK - POOL_HALO:], after[:POOL_HALO]], axis=0)
    (n1_ref, sh_ref, sc_ref, w_ref, qg_ref, kg_ref, sgn_ref, sgw_ref, sgb_ref,
     pw_ref, ps_ref, pool_ref, q_ref, k_ref, v_ref, sg_ref, xe_ref) = refs
    tm = h_ref.shape[0]
    i = pl.program_id(0)
    hn = _norm_modulate(x, n1_ref, sh_ref, sc_ref)
    hn_halo = _norm_modulate(x_halo, n1_ref, sh_ref, sc_ref)

    a_halo = _dot(hn_halo, w_ref[:, 0:Q_OFF])
    xe_ref[0:POOL_HALO, :] = jnp.where(i > 0, a_halo[:POOL_HALO], 0.0)
    xe_ref[POOL_HALO:POOL_HALO + tm, :] = _dot(hn, w_ref[:, 0:Q_OFF])
    xe_ref[POOL_HALO + tm:, :] = jnp.where(i < pl.num_programs(0) - 1, a_halo[POOL_HALO:], 0.0)

    a_g = _dot(hn, w_ref[:, G_OFF:IN_DIM])
    a_u = _dot(hn, w_ref[:, U_OFF:G_OFF])
    a_q = _dot(hn, w_ref[:, Q_OFF:K_OFF])
    a_k = _dot(hn, w_ref[:, K_OFF:V_OFF])
    _store_keys(v_ref, _dot(hn, w_ref[:, V_OFF:U_OFF]))

    gv = _gelu_tanh(a_g)
    q_ref[...] = (a_q * _head_rms_scale(a_q) * qg_ref[...]).astype(jnp.bfloat16)
    _store_keys(k_ref, a_k * _head_rms_scale(a_k) * kg_ref[...])

    u = _gelu_tanh(a_u)
    vn = (gv * _head_rms_scale(gv) * sgn_ref[...]).astype(jnp.bfloat16)
    low = lax.broadcasted_iota(jnp.int32, (SG_CHUNK, LANES), 1) < HEAD_DIM
    for c in range(tm // SG_CHUNK):
        rows = slice(c * SG_CHUNK, (c + 1) * SG_CHUNK)
        for s in range(SG_DIM // LANES):
            cols = slice(s * LANES, (s + 1) * LANES)
            m = _dot(sgw_ref[s], vn[rows, cols]) + sgb_ref[s]
            mixed = jnp.where(low, m[:SG_CHUNK], m[SG_CHUNK:])
            sg_ref[rows, cols] = (u[rows, cols] * mixed).astype(jnp.bfloat16)

    pool_ref[...] = _pool_mix(xe_ref, pw_ref, ps_ref, tm, seq_len)


def _inproj_call(stream, n1, sh, sc, w_in, layer, qg, kg, sgn, sgw, sgb, pool_w, pool_scale, tm, strips):
    pending = len(stream) == 3
    n = stream[0].shape[0]
    assert n % tm == 0 and tm % SG_CHUNK == 0 and tm % (2 * GRID_W) == 0
    if strips:
        kv_spec = pl.BlockSpec((N_STRIPS, tm // N_STRIPS, NA_DIM), lambda i: (0, i, 0))
        kv_shape = jax.ShapeDtypeStruct((N_STRIPS, n // N_STRIPS, NA_DIM), jnp.bfloat16)
    else:
        kv_spec = pl.BlockSpec((tm, NA_DIM), lambda i: (i, 0))
        kv_shape = jax.ShapeDtypeStruct((n, NA_DIM), jnp.bfloat16)
    row = lambda i: (i, 0)
    fixed2 = lambda i: (0, 0)
    fixed3 = lambda i: (0, 0, 0)
    vec = lambda w: pl.BlockSpec((1, w), fixed2)
    rows = pl.BlockSpec((tm, D_MODEL), row)
    per_tile = tm // HALO_BLOCK
    before = lambda i: (jnp.maximum(i * per_tile - 1, 0), 0)
    after = lambda i: (jnp.minimum((i + 1) * per_tile, n // HALO_BLOCK - 1), 0)
    halo = lambda w, m: pl.BlockSpec((HALO_BLOCK, w), m)
    h = stream[0]
    if pending:
        y, g = stream[1], stream[2]
        yw = y.shape[1]
        args = [h, y, g, h, h, y, y]
        stream_specs = [rows, pl.BlockSpec((tm, yw), row), vec(D_MODEL),
                        halo(D_MODEL, before), halo(D_MODEL, after), halo(yw, before), halo(yw, after)]
    else:
        args = [h, h, h]
        stream_specs = [rows, halo(D_MODEL, before), halo(D_MODEL, after)]
    extra_out_specs = [rows] if pending else []
    extra_out_shape = [jax.ShapeDtypeStruct((n, D_MODEL), STREAM_DTYPE)] if pending else []
    return pl.pallas_call(
        functools.partial(_inproj_kernel, pending=pending, seq_len=n),
        grid=(n // tm,),
        in_specs=stream_specs + [
            vec(D_MODEL), vec(D_MODEL), vec(D_MODEL),
            pl.BlockSpec((None, D_MODEL, IN_DIM), lambda i: (layer, 0, 0)),
            vec(NA_DIM), vec(NA_DIM),
            vec(SG_DIM),
            pl.BlockSpec((SG_DIM // LANES, 2 * SG_CHUNK, SG_CHUNK), fixed3),
            pl.BlockSpec((SG_DIM // LANES, 2 * SG_CHUNK, LANES), fixed3),
            pl.BlockSpec((POOL_DIM, POOL_DIM), fixed2),
            vec(POOL_DIM),
        ],
        out_specs=[
            pl.BlockSpec((tm, POOL_DIM), row),
            pl.BlockSpec((tm, NA_DIM), row),
            kv_spec,
            kv_spec,
            pl.BlockSpec((tm, SG_DIM), row),
        ] + extra_out_specs,
        out_shape=[
            jax.ShapeDtypeStruct((n, POOL_DIM), jnp.bfloat16),
            jax.ShapeDtypeStruct((n, NA_DIM), jnp.bfloat16),
            kv_shape,
            kv_shape,
            jax.ShapeDtypeStruct((n, SG_DIM), jnp.bfloat16),
        ] + extra_out_shape,
        scratch_shapes=[pltpu.VMEM((tm + 2 * POOL_HALO, POOL_DIM), jnp.float32)],
        compiler_params=_cparams(("arbitrary",)),
        name="inproj",
    )(*args, n1, sh, sc, w_in, qg, kg, sgn, sgw, sgb, pool_w, pool_scale)


NA_ROWS_PER_BLOCK = 64
NA_GROUP_ROWS = 4
NA_WINDOW_ROWS = NA_GROUP_ROWS + NA_WIN_ROWS
NA_BLOCK = NA_ROWS_PER_BLOCK * GRID_W
NA_GROUP = NA_GROUP_ROWS * GRID_W
NA_HALF_COLS = GRID_W // 2
NA_CHAIN = NA_GROUP_ROWS * NA_HALF_COLS
NA_HALF_STRIPS = 5
NA_HALF_COL0 = (0, GRID_W - NA_HALF_STRIPS * STRIP_W)
NA_RUN = NA_WINDOW_ROWS * STRIP_W
NA_LOCAL = 512
NA_STRIP_BLOCK = NA_ROWS_PER_BLOCK * STRIP_W
NA_STRIP_HALO = (NA_WIN_ROWS // 2) * STRIP_W
NA_EDGE_FIRST, NA_EDGE_NONE, NA_EDGE_LAST = 0, 1, 2


def _stack_heads(x, low):
    zero = jnp.zeros_like(x)
    return jnp.concatenate([jnp.where(low, x, zero), jnp.where(low, zero, x)], axis=0)


def _natten_kernel(q_ref, kp_ref, kc_ref, kn_ref, vp_ref, vc_ref, vn_ref, kx_ref, vx_ref, bias_ref,
                   wg_ref, wu_ref, wd_ref,
                   o_ref, wg_bf_ref, wu_bf_ref, wd_bf_ref, kwin_ref, vwin_ref, vxe_ref, *, grid_rows):
    b = pl.program_id(1)
    wg_bf_ref[...] = wg_ref[...].astype(jnp.bfloat16)
    wu_bf_ref[...] = wu_ref[...].astype(jnp.bfloat16)
    wd_bf_ref[...] = wd_ref[...].astype(jnp.bfloat16)
    top, bottom = NA_STRIP_HALO, NA_STRIP_HALO + NA_STRIP_BLOCK
    kwin_ref[:, 0:top, :] = kp_ref[...]
    kwin_ref[:, top:bottom, :] = kc_ref[...]
    kwin_ref[:, bottom:, :] = kn_ref[...]
    vwin_ref[:, 0:top, 0:LANES] = vp_ref[...]
    vwin_ref[:, top:bottom, 0:LANES] = vc_ref[...]
    vwin_ref[:, bottom:, 0:LANES] = vn_ref[...]
    vwin_ref[:, :, LANES:] = jnp.ones(vwin_ref.shape[:2] + (LANES,), jnp.bfloat16)
    vxe_ref[:, 0:LANES] = vx_ref[...]
    vxe_ref[:, LANES:] = jnp.ones((vxe_ref.shape[0], LANES), jnp.bfloat16)
    low_q = lax.broadcasted_iota(jnp.int32, (NA_CHAIN, LANES), 1) < HEAD_DIM
    n_pad = NA_LOCAL - NA_HALF_STRIPS * NA_RUN

    def window_start(g):
        r0 = b * NA_ROWS_PER_BLOCK + g * NA_GROUP_ROWS
        ws = jnp.clip(r0 - NA_WIN_ROWS // 2, 0, grid_rows - NA_WINDOW_ROWS)
        edge = jnp.where(r0 == 0, NA_EDGE_FIRST,
                         jnp.where(r0 == grid_rows - NA_GROUP_ROWS, NA_EDGE_LAST, NA_EDGE_NONE))
        start = pl.multiple_of((ws - b * NA_ROWS_PER_BLOCK + NA_WIN_ROWS // 2) * STRIP_W, NA_STRIP_HALO)
        return start, edge

    def local_window(win_ref, g, half):
        start, _ = window_start(g)
        s0 = NA_HALF_COL0[half] // STRIP_W
        runs = [win_ref[s, pl.ds(start, NA_RUN), :] for s in range(s0, s0 + NA_HALF_STRIPS)]
        return jnp.concatenate(runs + [jnp.zeros((n_pad, win_ref.shape[2]), jnp.bfloat16)], axis=0)

    def query_rows(g, half, j):
        first = g * NA_GROUP + j * GRID_W + half * NA_HALF_COLS
        return slice(first, first + NA_HALF_COLS)

    def scores(c):
        g, half = divmod(c, 2)
        _, edge = window_start(g)
        qh = jnp.concatenate([q_ref[query_rows(g, half, j), :] for j in range(NA_GROUP_ROWS)], axis=0)
        lhs = _stack_heads(qh, low_q)
        kl = local_window(kwin_ref, g, half)
        return jnp.concatenate([_dot_nt(lhs, kl).astype(jnp.bfloat16) + bias_ref[edge, half],
                                _dot_nt(lhs, kx_ref[...]).astype(jnp.bfloat16)], axis=-1)

    n_chains = 2 * (NA_ROWS_PER_BLOCK // NA_GROUP_ROWS)
    s_next = scores(0)
    for c in range(n_chains):
        s = s_next
        if c + 1 < n_chains:
            s_next = scores(c + 1)
        g, half = divmod(c, 2)
        vl = local_window(vwin_ref, g, half)
        m = jnp.max(s, axis=-1, keepdims=True)
        pb = jnp.exp2(s - m)
        o = _dot(pb[:, :NA_LOCAL], vl) + _dot(pb[:, NA_LOCAL:], vxe_ref[...])
        o = o[:, :LANES] * (1.0 / o[:, LANES:])
        o = jnp.where(low_q, o[:NA_CHAIN], o[NA_CHAIN:]).astype(jnp.bfloat16)
        for j in range(NA_GROUP_ROWS):
            o_ref[query_rows(g, half, j), :] = o[j * NA_HALF_COLS:(j + 1) * NA_HALF_COLS]


def _natten_call(q, k, v, k_ctx, v_ctx, bias, w_gate, w_up, w_down, layer):
    n = q.shape[0]
    grid_rows = n // GRID_W
    assert grid_rows % NA_ROWS_PER_BLOCK == 0 and grid_rows >= 2 * NA_ROWS_PER_BLOCK
    nblk = n // NA_BLOCK
    steps = HEAD_PAIRS * nblk
    depth = w_gate.shape[0]
    up_rows = N_EXPERTS * D_MODEL
    down_rows = N_EXPERTS * D_EXPERT
    assert up_rows % steps == 0 and down_rows % steps == 0
    wg2 = w_gate.reshape(depth * up_rows, D_EXPERT)
    wu2 = w_up.reshape(depth * up_rows, D_EXPERT)
    wd2 = w_down.reshape(depth * down_rows, D_MODEL)
    up_in = pl.BlockSpec((up_rows // steps, D_EXPERT), lambda p, b: (layer * steps + p * nblk + b, 0))
    down_in = pl.BlockSpec((down_rows // steps, D_MODEL), lambda p, b: (layer * steps + p * nblk + b, 0))
    up_out = pl.BlockSpec((up_rows // steps, D_EXPERT), lambda p, b: (p * nblk + b, 0))
    down_out = pl.BlockSpec((down_rows // steps, D_MODEL), lambda p, b: (p * nblk + b, 0))
    n_halo = n // N_STRIPS // NA_STRIP_HALO
    hb = NA_STRIP_BLOCK // NA_STRIP_HALO
    rows = pl.BlockSpec((NA_BLOCK, LANES), lambda p, b: (b, p))
    cur = pl.BlockSpec((N_STRIPS, NA_STRIP_BLOCK, LANES), lambda p, b: (0, b, p))
    prev = pl.BlockSpec((N_STRIPS, NA_STRIP_HALO, LANES), lambda p, b: (0, jnp.maximum(b * hb - 1, 0), p))
    nxt = pl.BlockSpec((N_STRIPS, NA_STRIP_HALO, LANES),
                       lambda p, b: (0, jnp.minimum((b + 1) * hb, n_halo - 1), p))
    ctx = pl.BlockSpec((k_ctx.shape[0], LANES), lambda p, b: (0, p))
    win_rows = NA_STRIP_BLOCK + 2 * NA_STRIP_HALO
    att, wg_bf, wu_bf, wd_bf = pl.pallas_call(
        functools.partial(_natten_kernel, grid_rows=grid_rows),
        grid=(HEAD_PAIRS, nblk),
        in_specs=[rows, prev, cur, nxt, prev, cur, nxt, ctx, ctx,
                  pl.BlockSpec((None, 3, 2, 2 * NA_CHAIN, NA_LOCAL), lambda p, b: (p, 0, 0, 0, 0)),
                  up_in, up_in, down_in],
        out_specs=[rows, up_out, up_out, down_out],
        out_shape=[jax.ShapeDtypeStruct((n, NA_DIM), jnp.bfloat16),
                   jax.ShapeDtypeStruct((up_rows, D_EXPERT), jnp.bfloat16),
                   jax.ShapeDtypeStruct((up_rows, D_EXPERT), jnp.bfloat16),
                   jax.ShapeDtypeStruct((down_rows, D_MODEL), jnp.bfloat16)],
        scratch_shapes=[pltpu.VMEM((N_STRIPS, win_rows, LANES), jnp.bfloat16),
                        pltpu.VMEM((N_STRIPS, win_rows, 2 * LANES), jnp.bfloat16),
                        pltpu.VMEM((k_ctx.shape[0], 2 * LANES), jnp.bfloat16)],
        compiler_params=_cparams(("arbitrary", "arbitrary")),
        name="natten",
    )(q, k, k, k, v, v, v, k_ctx, v_ctx, bias, wg2, wu2, wd2)
    return (att, wg_bf.reshape(N_EXPERTS, D_MODEL, D_EXPERT), wu_bf.reshape(N_EXPERTS, D_MODEL, D_EXPERT),
            wd_bf.reshape(N_EXPERTS, D_EXPERT, D_MODEL))


def _natten_bias(rpb):
    cols = np.arange(GRID_W)
    col_start = np.clip(cols - NA_WIN_COLS // 2, 0, GRID_W - NA_WIN_COLS)
    kc = np.arange(GRID_W)
    in_win = (kc[None, :] >= col_start[:, None]) & (kc[None, :] < col_start[:, None] + NA_WIN_COLS)
    dc = kc[None, :] - cols[:, None] + NA_WIN_COLS - 1
    sel = (np.arange(2 * NA_WIN_COLS - 1)[:, None, None] == dc[None]) & in_win[None]
    t2 = jnp.einsum("hdj,jqk->hdqk", rpb, jnp.asarray(sel, jnp.float32), precision=lax.Precision.HIGHEST)
    t2 = jnp.where(in_win[None, None], t2 * LOG2_E, NEG_BIG)
    neg = jnp.full((NA_HEADS, 1, GRID_W, GRID_W), NEG_BIG, jnp.float32)
    t2e = jnp.concatenate([neg, t2, neg], axis=1)
    u = jnp.concatenate([t2e[:, :-1], t2e[:, 1:]], axis=-1)
    u = u.reshape(HEAD_PAIRS, 2, 2 * NA_WIN_ROWS, GRID_W, LANES)

    place = np.zeros((2, NA_WINDOW_ROWS * GRID_W, NA_LOCAL), np.float32)
    for half, c0 in enumerate(NA_HALF_COL0):
        for a in range(NA_WINDOW_ROWS):
            for kcol in range(c0, c0 + NA_HALF_STRIPS * STRIP_W):
                s, c8 = divmod(kcol - c0, STRIP_W)
                place[half, a * GRID_W + kcol, s * NA_RUN + a * STRIP_W + c8] = 1.0
    outside = np.full((3, NA_GROUP_ROWS, NA_LOCAL), NEG_BIG, np.float32)
    for edge in (NA_EDGE_FIRST, NA_EDGE_NONE, NA_EDGE_LAST):
        for j in range(NA_GROUP_ROWS):
            lo, _ = _window_rows(edge, j)
            for s in range(NA_HALF_STRIPS):
                outside[edge, j, s * NA_RUN + lo * STRIP_W:s * NA_RUN + (lo + NA_WIN_ROWS) * STRIP_W] = 0.0
    return pl.pallas_call(
        _bias_expand_kernel,
        grid=(HEAD_PAIRS,),
        in_specs=[pl.BlockSpec((None, 2, 2 * NA_WIN_ROWS, GRID_W, LANES), lambda p: (p, 0, 0, 0, 0)),
                  pl.BlockSpec(place.shape, lambda p: (0, 0, 0)),
                  pl.BlockSpec(outside.shape, lambda p: (0, 0, 0))],
        out_specs=pl.BlockSpec((None, 3, 2, 2 * NA_CHAIN, NA_LOCAL), lambda p: (p, 0, 0, 0, 0)),
        out_shape=jax.ShapeDtypeStruct((HEAD_PAIRS, 3, 2, 2 * NA_CHAIN, NA_LOCAL), jnp.bfloat16),
        compiler_params=_cparams(("arbitrary",)),
        name="bias_expand",
    )(u, jnp.asarray(place, jnp.bfloat16), jnp.asarray(outside))


def _window_rows(edge, j):
    if edge == NA_EDGE_FIRST:
        return 0, NA_WIN_ROWS - 1 - j
    if edge == NA_EDGE_NONE:
        return j, NA_WIN_ROWS // 2 - 1
    return NA_WINDOW_ROWS - NA_WIN_ROWS, NA_WIN_ROWS // 2 - 1 - j


def _bias_expand_kernel(u_ref, place_ref, outside_ref, o_ref):
    low = lax.broadcasted_iota(jnp.int32, (NA_HALF_COLS, LANES), 1) < GRID_W
    zero = jnp.zeros((NA_HALF_COLS, LANES), jnp.float32)
    for edge in (NA_EDGE_FIRST, NA_EDGE_NONE, NA_EDGE_LAST):
        for half in range(2):
            q0 = half * NA_HALF_COLS
            blocks, masks = [], []
            for hd in range(2):
                for j in range(NA_GROUP_ROWS):
                    lo, base = _window_rows(edge, j)
                    tiles = []
                    for i in range(NA_WINDOW_ROWS // 2):
                        a0, a1 = 2 * i, 2 * i + 1
                        ok0 = lo <= a0 < lo + NA_WIN_ROWS
                        ok1 = lo <= a1 < lo + NA_WIN_ROWS
                        if not (ok0 or ok1):
                            tile = zero
                        else:
                            tile = u_ref[hd, base + a1 - lo, q0:q0 + NA_HALF_COLS, :]
                            if not ok0:
                                tile = jnp.where(low, zero, tile)
                            if not ok1:
                                tile = jnp.where(low, tile, zero)
                        tiles.append(tile)
                    blocks.append(jnp.concatenate(tiles, axis=-1))
                    masks.append(jnp.broadcast_to(outside_ref[edge, j:j + 1, :], (NA_HALF_COLS, NA_LOCAL)))
            lhs = jnp.concatenate(blocks, axis=0).astype(jnp.bfloat16)
            placed = _dot(lhs, place_ref[half]) + jnp.concatenate(masks, axis=0)
            o_ref[edge, half] = placed.astype(o_ref.dtype)


def _ctxatt_kernel(q_ref, k_ref, v_ref, o_ref):
    lc = q_ref.shape[0]
    low = lax.broadcasted_iota(jnp.int32, (lc, LANES), 1) < HEAD_DIM
    lhs = _stack_heads(q_ref[...], low)
    s = _dot_nt(lhs, k_ref[...])
    m = jnp.max(s, axis=-1, keepdims=True)
    p = jnp.exp2(s - m)
    denom = jnp.sum(p, axis=-1, keepdims=True)
    o = _dot(p.astype(jnp.bfloat16), v_ref[...]) * (1.0 / denom)
    o_ref[...] = jnp.where(low, o[:lc], o[lc:]).astype(jnp.bfloat16)


def _ctxatt_call(q, k, v):
    lc = q.shape[0]
    spec = pl.BlockSpec((lc, LANES), lambda p: (0, p))
    return pl.pallas_call(
        _ctxatt_kernel,
        grid=(HEAD_PAIRS,),
        in_specs=[spec, spec, spec],
        out_specs=spec,
        out_shape=jax.ShapeDtypeStruct((lc, NA_DIM), jnp.bfloat16),
        compiler_params=_cparams(("arbitrary",)),
        name="ctxatt",
    )(q, k, v)


OUTPROJ_CHAIN = 256
OUTPROJ_TILE = 1024
INPROJ_TILE = 1024
RESIDUAL_TILE = 2048


def _outproj_kernel(h_ref, mp_ref, att_ref, sg_ref, wo_ref, g1_ref, n2_ref, sh_ref, sc_ref,
                    wr_ref, br_ref, h1_ref, pay_ref, route_ref):
    chains = [slice(c * OUTPROJ_CHAIN, (c + 1) * OUTPROJ_CHAIN) for c in range(h_ref.shape[0] // OUTPROJ_CHAIN)]
    h1s = []
    for rows in chains:
        mix = (_dot(mp_ref[rows, :], wo_ref[0:POOL_DIM, :])
               + _dot(att_ref[rows, :], wo_ref[POOL_DIM:POOL_DIM + NA_DIM, :])
               + _dot(sg_ref[rows, :], wo_ref[POOL_DIM + NA_DIM:, :]))
        h1 = h_ref[rows, :].astype(jnp.float32) + g1_ref[...] * mix
        h1_ref[rows, :] = h1.astype(h1_ref.dtype)
        h1s.append(h1)
    for rows, h1 in zip(chains, h1s):
        _outproj_route(rows, h1, n2_ref, sh_ref, sc_ref, wr_ref, br_ref, pay_ref, route_ref)


def _outproj_route(rows, h1, n2_ref, sh_ref, sc_ref, wr_ref, br_ref, pay_ref, route_ref):
    tm = OUTPROJ_CHAIN
    ms = jnp.mean(h1 * h1, axis=-1, keepdims=True)
    hm = (h1 * lax.rsqrt(ms + EPS)) * (n2_ref[...] * (1.0 + sc_ref[...])) + sh_ref[...]
    pay_ref[rows, 0:HALF_D] = _pack_halves(hm)

    hm_hi = hm.astype(jnp.bfloat16)
    lt = _dot_nt(wr_ref[...], hm_hi)
    logits = lt[:N_EXPERTS] + lt[N_EXPERTS:] + br_ref[...]
    e = jnp.exp(logits - jnp.max(logits, axis=0, keepdims=True))

    best = ga = gb = e1 = e2 = cls = None
    for c in range(N_CLASSES):
        a, b2 = int(CLASS_E1[c]), int(CLASS_E2[c])
        ea, eb = e[a:a + 1, :], e[b2:b2 + 1, :]
        s = ea + eb
        if best is None:
            best, ga, gb = s, ea, eb
            e1 = jnp.full_like(s, float(a))
            e2 = jnp.full_like(s, float(b2))
            cls = jnp.zeros_like(s)
        else:
            better = s > best
            best = jnp.where(better, s, best)
            ga = jnp.where(better, ea, ga)
            gb = jnp.where(better, eb, gb)
            e1 = jnp.where(better, float(a), e1)
            e2 = jnp.where(better, float(b2), e2)
            cls = jnp.where(better, float(c), cls)
    inv = 1.0 / best
    row = lax.broadcasted_iota(jnp.int32, (ROUTE_ROWS, tm), 0)
    rec = jnp.where(row == 0, ga * inv,
          jnp.where(row == 1, gb * inv,
          jnp.where(row == 2, e1,
          jnp.where(row == 3, e2,
          jnp.where(row == 4, cls, 0.0)))))
    route_ref[:, rows] = rec
    wide = jnp.concatenate([rec, jnp.zeros((LANES - ROUTE_ROWS, tm), jnp.float32)], axis=0)
    pay_ref[rows, HALF_D:] = pltpu.bitcast(wide.T, jnp.uint32)


def _outproj_call(h, mp, att, sg, w_out, layer, g1, n2, sh2, sc2, wr, br, tm):
    n = h.shape[0]
    assert n % tm == 0 and tm % OUTPROJ_CHAIN == 0
    row = lambda i: (i, 0)
    fixed = lambda i: (0, 0)
    vec = pl.BlockSpec((1, D_MODEL), fixed)
    return pl.pallas_call(
        _outproj_kernel,
        grid=(n // tm,),
        in_specs=[
            pl.BlockSpec((tm, D_MODEL), row),
            pl.BlockSpec((tm, POOL_DIM), row),
            pl.BlockSpec((tm, NA_DIM), row),
            pl.BlockSpec((tm, SG_DIM), row),
            pl.BlockSpec((None, D_MODEL, D_MODEL), lambda i: (layer, 0, 0)),
            vec, vec, vec, vec,
            pl.BlockSpec((2 * N_EXPERTS, D_MODEL), fixed),
            pl.BlockSpec((N_EXPERTS, 1), fixed),
        ],
        out_specs=[
            pl.BlockSpec((tm, D_MODEL), row),
            pl.BlockSpec((tm, PAYLOAD_W), row),
            pl.BlockSpec((ROUTE_ROWS, tm), lambda i: (0, i)),
        ],
        out_shape=[
            jax.ShapeDtypeStruct((n, D_MODEL), STREAM_DTYPE),
            jax.ShapeDtypeStruct((n, PAYLOAD_W), jnp.uint32),
            jax.ShapeDtypeStruct((ROUTE_ROWS, n), jnp.float32),
        ],
        compiler_params=_cparams(("arbitrary",)),
        name="outproj",
    )(h, mp, att, sg, w_out, g1, n2, sh2, sc2, wr, br)


def _payload_parts(pay_ref, rows=slice(None)):
    lo, hi = _unpack_halves(pay_ref[rows, 0:HALF_D])
    x = jnp.concatenate([lo, hi], axis=-1).astype(jnp.bfloat16)
    return x, pltpu.bitcast(pay_ref[rows, HALF_D:], jnp.float32)


def _expert_pair(x, ga, gb, wga, wua, wda, wgb, wub, wdb):
    ha = (_silu(_dot(x, wga)) * _dot(x, wua) * ga).astype(jnp.bfloat16)
    hb = (_silu(_dot(x, wgb)) * _dot(x, wub) * gb).astype(jnp.bfloat16)
    return _dot(ha, wda) + _dot(hb, wdb)


MOE_PART = MOE_TM // 8


def _moe_sorted_kernel(blk_ref, e1_ref, e2_ref, parts_ref, pay_ref,
                       wga_ref, wua_ref, wda_ref, wgb_ref, wub_ref, wdb_ref, o_ref):
    parts = parts_ref[pl.program_id(0)]

    def run(rows):
        x, route = _payload_parts(pay_ref, rows)
        y = _expert_pair(x, route[:, 0:1], route[:, 1:2], wga_ref[...], wua_ref[...], wda_ref[...],
                         wgb_ref[...], wub_ref[...], wdb_ref[...])
        o_ref[rows, :] = _pack_halves(y)

    for used in range(1, MOE_TM // MOE_PART + 1):
        @pl.when(parts == used)
        def _(used=used):
            run(slice(0, used * MOE_PART))


def _moe_sorted_call(blk, e1, e2, parts, pay_sorted, wg, wu, wd):
    n_tiles = blk.shape[0]
    rows = lambda i, blk, e1, e2, parts: (blk[i], 0)
    wa = lambda i, blk, e1, e2, parts: (e1[i], 0, 0)
    wb = lambda i, blk, e1, e2, parts: (e2[i], 0, 0)
    up = lambda m: pl.BlockSpec((None, D_MODEL, D_EXPERT), m)
    down = lambda m: pl.BlockSpec((None, D_EXPERT, D_MODEL), m)
    return pl.pallas_call(
        _moe_sorted_kernel,
        grid_spec=pltpu.PrefetchScalarGridSpec(
            num_scalar_prefetch=4,
            grid=(n_tiles,),
            in_specs=[pl.BlockSpec((MOE_TM, PAYLOAD_W), rows),
                      up(wa), up(wa), down(wa), up(wb), up(wb), down(wb)],
            out_specs=pl.BlockSpec((MOE_TM, HALF_D), rows),
        ),
        out_shape=jax.ShapeDtypeStruct((n_tiles * MOE_TM, HALF_D), jnp.uint32),
        compiler_params=_cparams(("arbitrary",)),
        name="moe_sorted",
    )(blk, e1, e2, parts, pay_sorted, wg, wu, wd, wg, wu, wd)


DENSE_EXPERTS_PER_STEP = 4


def _moe_dense_kernel(*refs, first_step, resume):
    if resume:
        pay_ref, wg_ref, wu_ref, wd_ref, acc_ref, o_ref = refs
    else:
        pay_ref, wg_ref, wu_ref, wd_ref, o_ref = refs
    step = pl.program_id(0)

    @pl.when(step == 0)
    def _():
        o_ref[...] = acc_ref[...] if resume else jnp.zeros_like(o_ref)

    x, route = _payload_parts(pay_ref)
    y = None
    for k in range(DENSE_EXPERTS_PER_STEP):
        ef = ((first_step + step) * DENSE_EXPERTS_PER_STEP + k).astype(jnp.float32)
        gate = (jnp.where(route[:, 2:3] == ef, route[:, 0:1], 0.0)
                + jnp.where(route[:, 3:4] == ef, route[:, 1:2], 0.0))
        he = (_silu(_dot(x, wg_ref[k])) * _dot(x, wu_ref[k]) * gate).astype(jnp.bfloat16)
        yk = _dot(he, wd_ref[k])
        y = yk if y is None else y + yk
    o_ref[...] += y


def _moe_dense_call(pay, wg, wu, wd, first_step, n_steps, acc=None):
    n = pay.shape[0]
    per = DENSE_EXPERTS_PER_STEP
    whole = pl.BlockSpec((n, D_MODEL), lambda e: (0, 0))
    return pl.pallas_call(
        functools.partial(_moe_dense_kernel, first_step=first_step, resume=acc is not None),
        grid=(n_steps,),
        in_specs=[pl.BlockSpec((n, PAYLOAD_W), lambda e: (0, 0)),
                  pl.BlockSpec((per, D_MODEL, D_EXPERT), lambda e: (first_step + e, 0, 0)),
                  pl.BlockSpec((per, D_MODEL, D_EXPERT), lambda e: (first_step + e, 0, 0)),
                  pl.BlockSpec((per, D_EXPERT, D_MODEL), lambda e: (first_step + e, 0, 0))]
                 + ([whole] if acc is not None else []),
        out_specs=whole,
        out_shape=jax.ShapeDtypeStruct((n, D_MODEL), jnp.float32),
        compiler_params=_cparams(("arbitrary",)),
        name="moe_dense",
    )(pay, wg, wu, wd, *([acc] if acc is not None else []))


def _moe_out(y_ref):
    if y_ref.dtype == jnp.uint32:
        return jnp.concatenate(_unpack_halves(y_ref[...]), axis=-1)
    return y_ref[...]


def _residual_kernel(h_ref, y_ref, g_ref, o_ref):
    o_ref[...] = h_ref[...].astype(jnp.float32) + g_ref[...] * _moe_out(y_ref)


def _residual_call(h, y, g, tm):
    n = h.shape[0]
    row = pl.BlockSpec((tm, D_MODEL), lambda i: (i, 0))
    return pl.pallas_call(
        _residual_kernel,
        grid=(n // tm,),
        in_specs=[row, pl.BlockSpec((tm, y.shape[1]), lambda i: (i, 0)),
                  pl.BlockSpec((1, D_MODEL), lambda i: (0, 0))],
        out_specs=row,
        out_shape=jax.ShapeDtypeStruct((n, D_MODEL), jnp.float32),
        compiler_params=_cparams(("arbitrary",)),
        name="residual",
    )(h, y, g)


SC_ROWS = 128


SC_CORES = 2
SC_SUBCORES = 16
SC_WORKERS = SC_CORES * SC_SUBCORES


def _sc_mesh():
    return plsc.VectorSubcoreMesh(core_axis_name="core", subcore_axis_name="subcore")


def _sc_worker():
    return lax.axis_index("subcore") * SC_CORES + lax.axis_index("core")


def _scatter_rows(x, dest, n_out):
    n, w = x.shape

    per_worker = n // SC_WORKERS
    assert per_worker % SC_ROWS == 0

    @functools.partial(pl.kernel, out_type=jax.ShapeDtypeStruct((n_out, w), x.dtype), mesh=_sc_mesh(),
                       scratch_types=[pltpu.VMEM((SC_ROWS,), jnp.int32), pltpu.VMEM((SC_ROWS, w), x.dtype)])
    def scatter(x_hbm, i_hbm, o_hbm, idx_v, rows_v):
        first = _sc_worker() * per_worker

        @pl.loop(0, per_worker // SC_ROWS)
        def _(i):
            base = pl.multiple_of(first + i * SC_ROWS, SC_ROWS)
            pltpu.sync_copy(i_hbm.at[pl.ds(base, SC_ROWS)], idx_v)
            pltpu.sync_copy(x_hbm.at[pl.ds(base, SC_ROWS)], rows_v)
            pltpu.sync_copy(rows_v, o_hbm.at[idx_v])

    return scatter(x, dest)


def _gather_rows(x, src):
    n = src.shape[0]
    w = x.shape[1]

    per_worker = n // SC_WORKERS
    assert per_worker % SC_ROWS == 0

    @functools.partial(pl.kernel, out_type=jax.ShapeDtypeStruct((n, w), x.dtype), mesh=_sc_mesh(),
                       scratch_types=[pltpu.VMEM((SC_ROWS,), jnp.int32), pltpu.VMEM((SC_ROWS, w), x.dtype)])
    def gather(x_hbm, i_hbm, o_hbm, idx_v, rows_v):
        first = _sc_worker() * per_worker

        @pl.loop(0, per_worker // SC_ROWS)
        def _(i):
            base = pl.multiple_of(first + i * SC_ROWS, SC_ROWS)
            pltpu.sync_copy(i_hbm.at[pl.ds(base, SC_ROWS)], idx_v)
            pltpu.sync_copy(x_hbm.at[idx_v], rows_v)
            pltpu.sync_copy(rows_v, o_hbm.at[pl.ds(base, SC_ROWS)])

    return gather(x, src)


def _routing_plan(cls, n_tiles):
    onehot = (cls[:, None] == jnp.arange(N_CLASSES, dtype=jnp.int32)[None, :]).astype(jnp.int32)
    counts = jnp.sum(onehot, axis=0)
    rank = jnp.sum((jnp.cumsum(onehot, axis=0) - onehot) * onehot, axis=1)
    tiles = (counts + MOE_TM - 1) // MOE_TM
    tile_end = jnp.cumsum(tiles)
    tile_start = tile_end - tiles
    first_len = counts - (tiles - 1) * MOE_TM
    first_of = jnp.sum(onehot * first_len[None, :], axis=1)
    skip = jnp.where(rank >= first_of, MOE_TM - first_of, 0)
    dest = jnp.sum(onehot * tile_start[None, :], axis=1) * MOE_TM + rank + skip
    nact = tile_end[-1]
    tile = jnp.arange(n_tiles, dtype=jnp.int32)
    blk = jnp.minimum(tile, nact - 1)
    tile_cls = jnp.sum((blk[:, None] >= tile_end[None, :]).astype(jnp.int32), axis=1)
    e1 = jnp.asarray(CLASS_E1)[tile_cls]
    e2 = jnp.asarray(CLASS_E2)[tile_cls]
    valid = jnp.where(blk == tile_start[tile_cls], first_len[tile_cls], MOE_TM)
    parts = jnp.where(tile >= nact, 0, (valid + MOE_PART - 1) // MOE_PART)
    return dest.astype(jnp.int32), blk, e1, e2, parts.astype(jnp.int32)


def _row_tile(n, prefer=512):
    return next(t for t in (prefer, 512, 256) if n % t == 0)


def kernel(x, c, ctx, c_ctx, w_ada, b_ada, norm1, w_in, pool_w, pool_scale, q_norm, k_norm, rpb,
           sg_w, sg_b, sg_norm, w_out, norm2, w_router, b_router, w_gate, w_up, w_down):
    depth = w_ada.shape[0]
    n = x.shape[1]
    lc = ctx.shape[1]
    assert x.shape[0] == 1 and x.shape[2] == D_MODEL, "one sample of (tokens, 1024) features"
    assert n % (GRID_W * NA_ROWS_PER_BLOCK) == 0 and n % (SC_WORKERS * SC_ROWS) == 0 and lc % 256 == 0
    bf = jnp.bfloat16
    lat_stream = (x[0],)
    h_ctx = ctx[0]

    cond = jnp.stack([c[0], c_ctx], axis=1)
    mod = _ada_call(cond, w_ada, b_ada)

    wr_t = w_router.T
    wr_hi = wr_t.astype(bf)
    wr_lo = (wr_t - wr_hi.astype(jnp.float32)).astype(bf)
    wr = jnp.concatenate([wr_hi, wr_lo], axis=0)
    br = b_router.reshape(N_EXPERTS, 1)
    n_tiles = n // MOE_TM + N_CLASSES
    w_in_bf = w_in.astype(bf)
    w_out_bf = w_out.astype(bf)

    for l in range(depth):
        last = l == depth - 1
        qg =(q_norm[l] * (HEAD_DIM ** -0.5 * LOG2_E)).reshape(1, NA_DIM)
        kg = k_norm[l].reshape(1, NA_DIM)
        sgn = sg_norm[l].reshape(1, SG_DIM)
        sgw = sg_w[l].astype(bf).reshape(SG_DIM // LANES, 2 * SG_CHUNK, SG_CHUNK)
        sgb = jnp.broadcast_to(sg_b[l].reshape(SG_DIM // LANES, 2 * SG_CHUNK, 1),
                               (SG_DIM // LANES, 2 * SG_CHUNK, LANES))
        pool_bd = jax.scipy.linalg.block_diag(*[pool_w[l, g] for g in range(len(POOL_WINDOWS))]).astype(bf)
        pscale = pool_scale[l].reshape(1, POOL_DIM)
        bias = _natten_bias(rpb[l])
        n1 = norm1[l].reshape(1, D_MODEL)
        n2 = norm2[l].reshape(1, D_MODEL)

        def mods(row):
            return [mod[l, row:row + 1, i * D_MODEL:(i + 1) * D_MODEL] for i in range(6)]

        sh1, sc1, g1, sh2, sc2, g2 = mods(0)
        csh1, csc1, cg1, csh2, csc2, cg2 = mods(1)

        tc = _row_tile(lc)
        mix_pool_c, q_c, k_c, v_c, sg_c = _inproj_call((h_ctx,), n1, csh1, csc1, w_in_bf, l, qg, kg,
                                                       sgn, sgw, sgb, pool_bd, pscale, tc, strips=False)

        outs = _inproj_call(lat_stream, n1, sh1, sc1, w_in_bf, l, qg, kg, sgn, sgw, sgb,
                            pool_bd, pscale, _row_tile(n, INPROJ_TILE), strips=True)
        mix_pool, q, k, v, sg = outs[:5]
        h_lat = outs[5] if len(lat_stream) == 3 else lat_stream[0]
        att, wg_l, wu_l, wd_l = _natten_call(q, k, v, k_c, v_c, bias, w_gate, w_up, w_down, l)
        h1, pay, route = _outproj_call(h_lat, mix_pool, att, sg, w_out_bf, l, g1, n2, sh2, sc2, wr, br,
                                       _row_tile(n, OUTPROJ_TILE))
        cls = route[4].astype(jnp.int32)
        dest, blk, e1, e2, parts = _routing_plan(cls, n_tiles)
        pay_sorted = _scatter_rows(pay, dest, n_tiles * MOE_TM)

        if not last:
            dense_steps = N_EXPERTS // DENSE_EXPERTS_PER_STEP
            att_c = _ctxatt_call(q_c, k_c, v_c)
            h1_c, pay_c, _ = _outproj_call(h_ctx, mix_pool_c, att_c, sg_c, w_out_bf, l, cg1, n2, csh2, csc2,
                                           wr, br, tc)
            y_c = _moe_dense_call(pay_c, wg_l, wu_l, wd_l, 0, dense_steps // 2)
            parts, y_c = lax.optimization_barrier((parts, y_c))

        y_sorted = _moe_sorted_call(blk, e1, e2, parts, pay_sorted, wg_l, wu_l, wd_l)
        y = _gather_rows(y_sorted, dest)
        lat_stream = (h1, y, g2)

        if not last:
            y_c = _moe_dense_call(pay_c, wg_l, wu_l, wd_l, dense_steps // 2, dense_steps - dense_steps // 2,
                                  acc=y_c)
            h_ctx = _residual_call(h1_c, y_c, cg2, tc)

    return _residual_call(*lat_stream, _row_tile(n, RESIDUAL_TILE))[None]
```

```python
import functools

import jax
import jax.numpy as jnp
import numpy as np
from jax import lax
from jax.experimental import pallas as pl
from jax.experimental.pallas import tpu as pltpu
from jax.experimental.pallas import tpu_sc as plsc

D_MODEL = 1024
GRID_W = 64
HEAD_DIM = 64
POOL_WINDOWS = (2, 4, 8, 16)
POOL_DIM = 256
NA_HEADS = 8
NA_DIM = 512
NA_WIN_ROWS = 8
NA_WIN_COLS = 16
SG_DIM = 256
SG_CHUNK = 128
Q_OFF = POOL_DIM
K_OFF = Q_OFF + NA_DIM
V_OFF = K_OFF + NA_DIM
U_OFF = V_OFF + NA_DIM
G_OFF = U_OFF + SG_DIM
IN_DIM = G_OFF + SG_DIM
N_EXPERTS = 16
GROUP_SIZE = 4
D_EXPERT = 512
EPS = 1e-6

LANES = 128
SUBLANES = 8
HEAD_PAIRS = NA_DIM // LANES
VMEM_LIMIT = 48 * 1024 * 1024

PAIRS = ((0, 1), (0, 2), (1, 2), (1, 3), (0, 3), (2, 3))
N_GROUPS = N_EXPERTS // GROUP_SIZE
N_CLASSES = N_GROUPS * len(PAIRS)
CLASS_E1 = np.array([GROUP_SIZE * g + i for g in range(N_GROUPS) for (i, j) in PAIRS], np.int32)
CLASS_E2 = np.array([GROUP_SIZE * g + j for g in range(N_GROUPS) for (i, j) in PAIRS], np.int32)

ROUTE_ROWS = 8
HALF_D = D_MODEL // 2
PAYLOAD_W = HALF_D + LANES
MOE_TM = 512
NEG_BIG = -1e30
LOG2_E = 1.4426950408889634


def _cparams(sem):
    return pltpu.CompilerParams(dimension_semantics=sem, vmem_limit_bytes=VMEM_LIMIT)


def _dot(a, b):
    return jnp.dot(a, b, preferred_element_type=jnp.float32)


def _dot_nt(a, b):
    return lax.dot_general(a, b, (((1,), (1,)), ((), ())), preferred_element_type=jnp.float32)


def _gelu_tanh(x):
    return 0.5 * x * (1.0 + jnp.tanh(0.7978845608028654 * (x + 0.044715 * (x * x * x))))


def _silu(x):
    return x * (1.0 / (1.0 + jnp.exp(-x)))


def _pack_halves(x):
    w = x.shape[1] // 2
    lo = pltpu.bitcast(x[:, :w].astype(jnp.bfloat16).astype(jnp.float32), jnp.uint32) >> 16
    hi = pltpu.bitcast(x[:, w:].astype(jnp.bfloat16).astype(jnp.float32), jnp.uint32) & jnp.uint32(0xFFFF0000)
    return lo | hi


def _unpack_halves(words):
    lo = pltpu.bitcast(words << 16, jnp.float32)
    hi = pltpu.bitcast(words & jnp.uint32(0xFFFF0000), jnp.float32)
    return lo, hi


def _ada_kernel(cond_ref, w_ref, b_ref, o_ref):
    cond = _silu(cond_ref[...])
    w = w_ref[...]
    rows = [jnp.sum(w * cond[:, r:r + 1], axis=0, keepdims=True) + b_ref[...] for r in range(2)]
    o_ref[...] = jnp.concatenate(rows + [jnp.zeros((SUBLANES - 2, w.shape[1]), jnp.float32)], axis=0)


def _ada_call(cond, w_ada, b_ada):
    depth = w_ada.shape[0]
    tn = 1536
    return pl.pallas_call(
        _ada_kernel,
        grid=(depth, 6 * D_MODEL // tn),
        in_specs=[
            pl.BlockSpec((D_MODEL, 2), lambda l, j: (0, 0)),
            pl.BlockSpec((None, D_MODEL, tn), lambda l, j: (l, 0, j)),
            pl.BlockSpec((None, 1, tn), lambda l, j: (l, 0, j)),
        ],
        out_specs=pl.BlockSpec((None, SUBLANES, tn), lambda l, j: (l, 0, j)),
        out_shape=jax.ShapeDtypeStruct((depth, SUBLANES, 6 * D_MODEL), jnp.float32),
        compiler_params=_cparams(("arbitrary", "arbitrary")),
        name="adaln",
    )(cond, w_ada, b_ada.reshape(depth, 1, 6 * D_MODEL))


def _norm_modulate(x, n_ref, sh_ref, sc_ref):
    ms = jnp.mean(x * x, axis=-1, keepdims=True)
    return ((x * lax.rsqrt(ms + EPS)) * (n_ref[...] * (1.0 + sc_ref[...])) + sh_ref[...]).astype(jnp.bfloat16)


def _head_rms_scale(a):
    low = lax.broadcasted_iota(jnp.int32, (a.shape[0], LANES), 1) < HEAD_DIM
    blocks = []
    for p in range(a.shape[1] // LANES):
        sq = jnp.square(a[:, p * LANES:(p + 1) * LANES])
        s_lo = jnp.sum(jnp.where(low, sq, 0.0), axis=-1, keepdims=True)
        s_hi = jnp.sum(jnp.where(low, 0.0, sq), axis=-1, keepdims=True)
        blocks.append(jnp.where(low, lax.rsqrt(s_lo * (1.0 / HEAD_DIM) + EPS),
                                lax.rsqrt(s_hi * (1.0 / HEAD_DIM) + EPS)))
    return jnp.concatenate(blocks, axis=-1)


STRIP_W = 8
N_STRIPS = GRID_W // STRIP_W


def _store_keys(ref, x):
    if len(ref.shape) == 2:
        ref[...] = x.astype(jnp.bfloat16)
        return
    pair = 2 * STRIP_W
    for s in range(N_STRIPS):
        for rp in range(x.shape[0] // (2 * GRID_W)):
            top = 2 * rp * GRID_W + s * STRIP_W
            rows = jnp.concatenate([x[top:top + STRIP_W], x[top + GRID_W:top + GRID_W + STRIP_W]], axis=0)
            ref[s, rp * pair:(rp + 1) * pair, :] = rows.astype(jnp.bfloat16)


POOL_HALO = 8
HALO_BLOCK = 16
STREAM_DTYPE = jnp.bfloat16


POOL_EDGE = 16


def _pool_mix(xe_ref, w_ref, scale_ref, tm, seq_len):
    low = lax.broadcasted_iota(jnp.int32, (tm, LANES), 1) < HEAD_DIM
    t_edge = pl.program_id(0) * tm + lax.broadcasted_iota(jnp.int32, (POOL_EDGE, LANES), 0)

    def window_mean(s, half):
        mean = s * (1.0 / (2 * half))

        def clip_fix(t):
            count = (jnp.minimum(t + half, seq_len) - jnp.maximum(t - half, 0)).astype(jnp.float32)
            return (2.0 * half) / count

        return jnp.concatenate([mean[:POOL_EDGE] * clip_fix(t_edge),
                                mean[POOL_EDGE:tm - POOL_EDGE],
                                mean[tm - POOL_EDGE:] * clip_fix(t_edge + (tm - POOL_EDGE))], axis=0)

    def window_sums(xs, n_levels):
        sums = []
        s = xs
        for k in range(n_levels):
            step = 1 << k
            s = s[:-step] + s[step:]
            sums.append(s)
        return sums

    outs = []
    for half_block, windows in enumerate(((2, 4), (8, 16))):
        xs = xe_ref[:, half_block * LANES:(half_block + 1) * LANES]
        sums = window_sums(xs, int(np.log2(windows[1])))
        parts = []
        for w in windows:
            half = w // 2
            s = sums[int(np.log2(w)) - 1][POOL_HALO - half:POOL_HALO - half + tm]
            parts.append(window_mean(s, half))
        mean = jnp.where(low, parts[0], parts[1])
        outs.append(mean - xs[POOL_HALO:POOL_HALO + tm])
    d = jnp.concatenate(outs, axis=-1).astype(jnp.bfloat16)
    return (_dot(d, w_ref[...]) * scale_ref[...]).astype(jnp.bfloat16)


def _inproj_kernel(*refs, pending, seq_len):
    n_stream = 7 if pending else 3
    stream, refs = refs[:n_stream], refs[n_stream:]
    if pending:
        h_ref, y_ref, g_ref, hp_ref, hn_ref, yp_ref, yn_ref = stream
        hres_ref, refs = refs[-2], refs[:-2] + refs[-1:]
        x = h_ref[...].astype(jnp.float32) + g_ref[...] * _moe_out(y_ref)
        hres_ref[...] = x.astype(hres_ref.dtype)
        before = hp_ref[...].astype(jnp.float32) + g_ref[...] * _moe_out(yp_ref)
        after = hn_ref[...].astype(jnp.float32) + g_ref[...] * _moe_out(yn_ref)
    else:
        h_ref, hp_ref, hn_ref = stream
        x = h_ref[...].astype(jnp.float32)
        before = hp_ref[...].astype(jnp.float32)
        after = hn_ref[...].astype(jnp.float32)
    x_halo = jnp.concatenate([before[HALO_BLOCK - POOL_HALO:], after[:POOL_HALO]], axis=0)
    (n1_ref, sh_ref, sc_ref, w_ref, qg_ref, kg_ref, sgn_ref, sgw_ref, sgb_ref,
     pw_ref, ps_ref, pool_ref, q_ref, k_ref, v_ref, sg_ref, xe_ref) = refs
    tm = h_ref.shape[0]
    i = pl.program_id(0)
    hn = _norm_modulate(x, n1_ref, sh_ref, sc_ref)
    hn_halo = _norm_modulate(x_halo, n1_ref, sh_ref, sc_ref)

    a_halo = _dot(hn_halo, w_ref[:, 0:Q_OFF])
    xe_ref[0:POOL_HALO, :] = jnp.where(i > 0, a_halo[:POOL_HALO], 0.0)
    xe_ref[POOL_HALO:POOL_HALO + tm, :] = _dot(hn, w_ref[:, 0:Q_OFF])
    xe_ref[POOL_HALO + tm:, :] = jnp.where(i < pl.num_programs(0) - 1, a_halo[POOL_HALO:], 0.0)

    a_g = _dot(hn, w_ref[:, G_OFF:IN_DIM])
    a_u = _dot(hn, w_ref[:, U_OFF:G_OFF])
    a_q = _dot(hn, w_ref[:, Q_OFF:K_OFF])
    a_k = _dot(hn, w_ref[:, K_OFF:V_OFF])
    _store_keys(v_ref, _dot(hn, w_ref[:, V_OFF:U_OFF]))

    gv = _gelu_tanh(a_g)
    q_ref[...] = (a_q * _head_rms_scale(a_q) * qg_ref[...]).astype(jnp.bfloat16)
    _store_keys(k_ref, a_k * _head_rms_scale(a_k) * kg_ref[...])

    u = _gelu_tanh(a_u)
    vn = (gv * _head_rms_scale(gv) * sgn_ref[...]).astype(jnp.bfloat16)
    low = lax.broadcasted_iota(jnp.int32, (SG_CHUNK, LANES), 1) < HEAD_DIM
    for c in range(tm // SG_CHUNK):
        rows = slice(c * SG_CHUNK, (c + 1) * SG_CHUNK)
        for s in range(SG_DIM // LANES):
            cols = slice(s * LANES, (s + 1) * LANES)
            m = _dot(sgw_ref[s], vn[rows, cols]) + sgb_ref[s]
            mixed = jnp.where(low, m[:SG_CHUNK], m[SG_CHUNK:])
            sg_ref[rows, cols] = (u[rows, cols] * mixed).astype(jnp.bfloat16)

    pool_ref[...] = _pool_mix(xe_ref, pw_ref, ps_ref, tm, seq_len)


def _inproj_call(stream, n1, sh, sc, w_in, layer, qg, kg, sgn, sgw, sgb, pool_w, pool_scale, tm, strips):
    pending = len(stream) == 3
    n = stream[0].shape[0]
    assert n % tm == 0 and tm % SG_CHUNK == 0 and tm % (2 * GRID_W) == 0
    if strips:
        kv_spec = pl.BlockSpec((N_STRIPS, tm // N_STRIPS, NA_DIM), lambda i: (0, i, 0))
        kv_shape = jax.ShapeDtypeStruct((N_STRIPS, n // N_STRIPS, NA_DIM), jnp.bfloat16)
    else:
        kv_spec = pl.BlockSpec((tm, NA_DIM), lambda i: (i, 0))
        kv_shape = jax.ShapeDtypeStruct((n, NA_DIM), jnp.bfloat16)
    row = lambda i: (i, 0)
    fixed2 = lambda i: (0, 0)
    fixed3 = lambda i: (0, 0, 0)
    vec = lambda w: pl.BlockSpec((1, w), fixed2)
    rows = pl.BlockSpec((tm, D_MODEL), row)
    per_tile = tm // HALO_BLOCK
    before = lambda i: (jnp.maximum(i * per_tile - 1, 0), 0)
    after = lambda i: (jnp.minimum((i + 1) * per_tile, n // HALO_BLOCK - 1), 0)
    halo = lambda w, m: pl.BlockSpec((HALO_BLOCK, w), m)
    h = stream[0]
    if pending:
        y, g = stream[1], stream[2]
        yw = y.shape[1]
        args = [h, y, g, h, h, y, y]
        stream_specs = [rows, pl.BlockSpec((tm, yw), row), vec(D_MODEL),
                        halo(D_MODEL, before), halo(D_MODEL, after), halo(yw, before), halo(yw, after)]
    else:
        args = [h, h, h]
        stream_specs = [rows, halo(D_MODEL, before), halo(D_MODEL, after)]
    extra_out_specs = [rows] if pending else []
    extra_out_shape = [jax.ShapeDtypeStruct((n, D_MODEL), STREAM_DTYPE)] if pending else []
    return pl.pallas_call(
        functools.partial(_inproj_kernel, pending=pending, seq_len=n),
        grid=(n // tm,),
        in_specs=stream_specs + [
            vec(D_MODEL), vec(D_MODEL), vec(D_MODEL),
            pl.BlockSpec((None, D_MODEL, IN_DIM), lambda i: (layer, 0, 0)),
            vec(NA_DIM), vec(NA_DIM),
            vec(SG_DIM),
            pl.BlockSpec((SG_DIM // LANES, 2 * SG_CHUNK, SG_CHUNK), fixed3),
            pl.BlockSpec((SG_DIM // LANES, 2 * SG_CHUNK, LANES), fixed3),
            pl.BlockSpec((POOL_DIM, POOL_DIM), fixed2),
            vec(POOL_DIM),
        ],
        out_specs=[
            pl.BlockSpec((tm, POOL_DIM), row),
            pl.BlockSpec((tm, NA_DIM), row),
            kv_spec,
            kv_spec,
            pl.BlockSpec((tm, SG_DIM), row),
        ] + extra_out_specs,
        out_shape=[
            jax.ShapeDtypeStruct((n, POOL_DIM), jnp.bfloat16),
            jax.ShapeDtypeStruct((n, NA_DIM), jnp.bfloat16),
            kv_shape,
            kv_shape,
            jax.ShapeDtypeStruct((n, SG_DIM), jnp.bfloat16),
        ] + extra_out_shape,
        scratch_shapes=[pltpu.VMEM((tm + 2 * POOL_HALO, POOL_DIM), jnp.float32)],
        compiler_params=_cparams(("arbitrary",)),
        name="inproj",
    )(*args, n1, sh, sc, w_in, qg, kg, sgn, sgw, sgb, pool_w, pool_scale)


NA_ROWS_PER_BLOCK = 64
NA_GROUP_ROWS = 4
NA_WINDOW_ROWS = NA_GROUP_ROWS + NA_WIN_ROWS
NA_BLOCK = NA_ROWS_PER_BLOCK * GRID_W
NA_GROUP = NA_GROUP_ROWS * GRID_W
NA_HALF_COLS = GRID_W // 2
NA_CHAIN = NA_GROUP_ROWS * NA_HALF_COLS
NA_HALF_STRIPS = 5
NA_HALF_COL0 = (0, GRID_W - NA_HALF_STRIPS * STRIP_W)
NA_RUN = NA_WINDOW_ROWS * STRIP_W
NA_LOCAL = 512
NA_STRIP_BLOCK = NA_ROWS_PER_BLOCK * STRIP_W
NA_STRIP_HALO = (NA_WIN_ROWS // 2) * STRIP_W
NA_EDGE_FIRST, NA_EDGE_NONE, NA_EDGE_LAST = 0, 1, 2


def _stack_heads(x, low):
    zero = jnp.zeros_like(x)
    return jnp.concatenate([jnp.where(low, x, zero), jnp.where(low, zero, x)], axis=0)


def _natten_kernel(q_ref, kp_ref, kc_ref, kn_ref, vp_ref, vc_ref, vn_ref, kx_ref, vx_ref, bias_ref,
                   wg_ref, wu_ref, wd_ref,
                   o_ref, wg_bf_ref, wu_bf_ref, wd_bf_ref, kwin_ref, vwin_ref, vxe_ref, *, grid_rows):
    b = pl.program_id(1)
    wg_bf_ref[...] = wg_ref[...].astype(jnp.bfloat16)
    wu_bf_ref[...] = wu_ref[...].astype(jnp.bfloat16)
    wd_bf_ref[...] = wd_ref[...].astype(jnp.bfloat16)
    top, bottom = NA_STRIP_HALO, NA_STRIP_HALO + NA_STRIP_BLOCK
    kwin_ref[:, 0:top, :] = kp_ref[...]
    kwin_ref[:, top:bottom, :] = kc_ref[...]
    kwin_ref[:, bottom:, :] = kn_ref[...]
    vwin_ref[:, 0:top, 0:LANES] = vp_ref[...]
    vwin_ref[:, top:bottom, 0:LANES] = vc_ref[...]
    vwin_ref[:, bottom:, 0:LANES] = vn_ref[...]
    vwin_ref[:, :, LANES:] = jnp.ones(vwin_ref.shape[:2] + (LANES,), jnp.bfloat16)
    vxe_ref[:, 0:LANES] = vx_ref[...]
    vxe_ref[:, LANES:] = jnp.ones((vxe_ref.shape[0], LANES), jnp.bfloat16)
    low_q = lax.broadcasted_iota(jnp.int32, (NA_CHAIN, LANES), 1) < HEAD_DIM
    n_pad = NA_LOCAL - NA_HALF_STRIPS * NA_RUN

    def window_start(g):
        r0 = b * NA_ROWS_PER_BLOCK + g * NA_GROUP_ROWS
        ws = jnp.clip(r0 - NA_WIN_ROWS // 2, 0, grid_rows - NA_WINDOW_ROWS)
        edge = jnp.where(r0 == 0, NA_EDGE_FIRST,
                         jnp.where(r0 == grid_rows - NA_GROUP_ROWS, NA_EDGE_LAST, NA_EDGE_NONE))
        start = pl.multiple_of((ws - b * NA_ROWS_PER_BLOCK + NA_WIN_ROWS // 2) * STRIP_W, NA_STRIP_HALO)
        return start, edge

    def local_window(win_ref, g, half):
        start, _ = window_start(g)
        s0 = NA_HALF_COL0[half] // STRIP_W
        runs = [win_ref[s, pl.ds(start, NA_RUN), :] for s in range(s0, s0 + NA_HALF_STRIPS)]
        return jnp.concatenate(runs + [jnp.zeros((n_pad, win_ref.shape[2]), jnp.bfloat16)], axis=0)

    def query_rows(g, half, j):
        first = g * NA_GROUP + j * GRID_W + half * NA_HALF_COLS
        return slice(first, first + NA_HALF_COLS)

    def scores(c):
        g, half = divmod(c, 2)
        _, edge = window_start(g)
        qh = jnp.concatenate([q_ref[query_rows(g, half, j), :] for j in range(NA_GROUP_ROWS)], axis=0)
        lhs = _stack_heads(qh, low_q)
        kl = local_window(kwin_ref, g, half)
        return jnp.concatenate([_dot_nt(lhs, kl).astype(jnp.bfloat16) + bias_ref[edge, half],
                                _dot_nt(lhs, kx_ref[...]).astype(jnp.bfloat16)], axis=-1)

    n_chains = 2 * (NA_ROWS_PER_BLOCK // NA_GROUP_ROWS)
    s_next = scores(0)
    for c in range(n_chains):
        s = s_next
        if c + 1 < n_chains:
            s_next = scores(c + 1)
        g, half = divmod(c, 2)
        vl = local_window(vwin_ref, g, half)
        m = jnp.max(s, axis=-1, keepdims=True)
        pb = jnp.exp2(s - m)
        o = _dot(pb[:, :NA_LOCAL], vl) + _dot(pb[:, NA_LOCAL:], vxe_ref[...])
        o = o[:, :LANES] * (1.0 / o[:, LANES:])
        o = jnp.where(low_q, o[:NA_CHAIN], o[NA_CHAIN:]).astype(jnp.bfloat16)
        for j in range(NA_GROUP_ROWS):
            o_ref[query_rows(g, half, j), :] = o[j * NA_HALF_COLS:(j + 1) * NA_HALF_COLS]


def _natten_call(q, k, v, k_ctx, v_ctx, bias, w_gate, w_up, w_down, layer):
    n = q.shape[0]
    grid_rows = n // GRID_W
    assert grid_rows % NA_ROWS_PER_BLOCK == 0 and grid_rows >= 2 * NA_ROWS_PER_BLOCK
    nblk = n // NA_BLOCK
    steps = HEAD_PAIRS * nblk
    depth = w_gate.shape[0]
    up_rows = N_EXPERTS * D_MODEL
    down_rows = N_EXPERTS * D_EXPERT
    assert up_rows % steps == 0 and down_rows % steps == 0
    wg2 = w_gate.reshape(depth * up_rows, D_EXPERT)
    wu2 = w_up.reshape(depth * up_rows, D_EXPERT)
    wd2 = w_down.reshape(depth * down_rows, D_MODEL)
    up_in = pl.BlockSpec((up_rows // steps, D_EXPERT), lambda p, b: (layer * steps + p * nblk + b, 0))
    down_in = pl.BlockSpec((down_rows // steps, D_MODEL), lambda p, b: (layer * steps + p * nblk + b, 0))
    up_out = pl.BlockSpec((up_rows // steps, D_EXPERT), lambda p, b: (p * nblk + b, 0))
    down_out = pl.BlockSpec((down_rows // steps, D_MODEL), lambda p, b: (p * nblk + b, 0))
    n_halo = n // N_STRIPS // NA_STRIP_HALO
    hb = NA_STRIP_BLOCK // NA_STRIP_HALO
    rows = pl.BlockSpec((NA_BLOCK, LANES), lambda p, b: (b, p))
    cur = pl.BlockSpec((N_STRIPS, NA_STRIP_BLOCK, LANES), lambda p, b: (0, b, p))
    prev = pl.BlockSpec((N_STRIPS, NA_STRIP_HALO, LANES), lambda p, b: (0, jnp.maximum(b * hb - 1, 0), p))
    nxt = pl.BlockSpec((N_STRIPS, NA_STRIP_HALO, LANES),
                       lambda p, b: (0, jnp.minimum((b + 1) * hb, n_halo - 1), p))
    ctx = pl.BlockSpec((k_ctx.shape[0], LANES), lambda p, b: (0, p))
    win_rows = NA_STRIP_BLOCK + 2 * NA_STRIP_HALO
    att, wg_bf, wu_bf, wd_bf = pl.pallas_call(
        functools.partial(_natten_kernel, grid_rows=grid_rows),
        grid=(HEAD_PAIRS, nblk),
        in_specs=[rows, prev, cur, nxt, prev, cur, nxt, ctx, ctx,
                  pl.BlockSpec((None, 3, 2, 2 * NA_CHAIN, NA_LOCAL), lambda p, b: (p, 0, 0, 0, 0)),
                  up_in, up_in, down_in],
        out_specs=[rows, up_out, up_out, down_out],
        out_shape=[jax.ShapeDtypeStruct((n, NA_DIM), jnp.bfloat16),
                   jax.ShapeDtypeStruct((up_rows, D_EXPERT), jnp.bfloat16),
                   jax.ShapeDtypeStruct((up_rows, D_EXPERT), jnp.bfloat16),
                   jax.ShapeDtypeStruct((down_rows, D_MODEL), jnp.bfloat16)],
        scratch_shapes=[pltpu.VMEM((N_STRIPS, win_rows, LANES), jnp.bfloat16),
                        pltpu.VMEM((N_STRIPS, win_rows, 2 * LANES), jnp.bfloat16),
                        pltpu.VMEM((k_ctx.shape[0], 2 * LANES), jnp.bfloat16)],
        compiler_params=_cparams(("arbitrary", "arbitrary")),
        name="natten",
    )(q, k, k, k, v, v, v, k_ctx, v_ctx, bias, wg2, wu2, wd2)
    return (att, wg_bf.reshape(N_EXPERTS, D_MODEL, D_EXPERT), wu_bf.reshape(N_EXPERTS, D_MODEL, D_EXPERT),
            wd_bf.reshape(N_EXPERTS, D_EXPERT, D_MODEL))


def _natten_bias(rpb):
    cols = np.arange(GRID_W)
    col_start = np.clip(cols - NA_WIN_COLS // 2, 0, GRID_W - NA_WIN_COLS)
    kc = np.arange(GRID_W)
    in_win = (kc[None, :] >= col_start[:, None]) & (kc[None, :] < col_start[:, None] + NA_WIN_COLS)
    dc = kc[None, :] - cols[:, None] + NA_WIN_COLS - 1
    sel = (np.arange(2 * NA_WIN_COLS - 1)[:, None, None] == dc[None]) & in_win[None]
    t2 = jnp.einsum("hdj,jqk->hdqk", rpb, jnp.asarray(sel, jnp.float32), precision=lax.Precision.HIGHEST)
    t2 = jnp.where(in_win[None, None], t2 * LOG2_E, NEG_BIG)
    neg = jnp.full((NA_HEADS, 1, GRID_W, GRID_W), NEG_BIG, jnp.float32)
    t2e = jnp.concatenate([neg, t2, neg], axis=1)
    u = jnp.concatenate([t2e[:, :-1], t2e[:, 1:]], axis=-1)
    u = u.reshape(HEAD_PAIRS, 2, 2 * NA_WIN_ROWS, GRID_W, LANES)

    place = np.zeros((2, NA_WINDOW_ROWS * GRID_W, NA_LOCAL), np.float32)
    for half, c0 in enumerate(NA_HALF_COL0):
        for a in range(NA_WINDOW_ROWS):
            for kcol in range(c0, c0 + NA_HALF_STRIPS * STRIP_W):
                s, c8 = divmod(kcol - c0, STRIP_W)
                place[half, a * GRID_W + kcol, s * NA_RUN + a * STRIP_W + c8] = 1.0
    outside = np.full((3, NA_GROUP_ROWS, NA_LOCAL), NEG_BIG, np.float32)
    for edge in (NA_EDGE_FIRST, NA_EDGE_NONE, NA_EDGE_LAST):
        for j in range(NA_GROUP_ROWS):
            lo, _ = _window_rows(edge, j)
            for s in range(NA_HALF_STRIPS):
                outside[edge, j, s * NA_RUN + lo * STRIP_W:s * NA_RUN + (lo + NA_WIN_ROWS) * STRIP_W] = 0.0
    return pl.pallas_call(
        _bias_expand_kernel,
        grid=(HEAD_PAIRS,),
        in_specs=[pl.BlockSpec((None, 2, 2 * NA_WIN_ROWS, GRID_W, LANES), lambda p: (p, 0, 0, 0, 0)),
                  pl.BlockSpec(place.shape, lambda p: (0, 0, 0)),
                  pl.BlockSpec(outside.shape, lambda p: (0, 0, 0))],
        out_specs=pl.BlockSpec((None, 3, 2, 2 * NA_CHAIN, NA_LOCAL), lambda p: (p, 0, 0, 0, 0)),
        out_shape=jax.ShapeDtypeStruct((HEAD_PAIRS, 3, 2, 2 * NA_CHAIN, NA_LOCAL), jnp.bfloat16),
        compiler_params=_cparams(("arbitrary",)),
        name="bias_expand",
    )(u, jnp.asarray(place, jnp.bfloat16), jnp.asarray(outside))


def _window_rows(edge, j):
    if edge == NA_EDGE_FIRST:
        return 0, NA_WIN_ROWS - 1 - j
    if edge == NA_EDGE_NONE:
        return j, NA_WIN_ROWS // 2 - 1
    return NA_WINDOW_ROWS - NA_WIN_ROWS, NA_WIN_ROWS // 2 - 1 - j


def _bias_expand_kernel(u_ref, place_ref, outside_ref, o_ref):
    low = lax.broadcasted_iota(jnp.int32, (NA_HALF_COLS, LANES), 1) < GRID_W
    zero = jnp.zeros((NA_HALF_COLS, LANES), jnp.float32)
    for edge in (NA_EDGE_FIRST, NA_EDGE_NONE, NA_EDGE_LAST):
        for half in range(2):
            q0 = half * NA_HALF_COLS
            blocks, masks = [], []
            for hd in range(2):
                for j in range(NA_GROUP_ROWS):
                    lo, base = _window_rows(edge, j)
                    tiles = []
                    for i in range(NA_WINDOW_ROWS // 2):
                        a0, a1 = 2 * i, 2 * i + 1
                        ok0 = lo <= a0 < lo + NA_WIN_ROWS
                        ok1 = lo <= a1 < lo + NA_WIN_ROWS
                        if not (ok0 or ok1):
                            tile = zero
                        else:
                            tile = u_ref[hd, base + a1 - lo, q0:q0 + NA_HALF_COLS, :]
                            if not ok0:
                                tile = jnp.where(low, zero, tile)
                            if not ok1:
                                tile = jnp.where(low, tile, zero)
                        tiles.append(tile)
                    blocks.append(jnp.concatenate(tiles, axis=-1))
                    masks.append(jnp.broadcast_to(outside_ref[edge, j:j + 1, :], (NA_HALF_COLS, NA_LOCAL)))
            lhs = jnp.concatenate(blocks, axis=0).astype(jnp.bfloat16)
            placed = _dot(lhs, place_ref[half]) + jnp.concatenate(masks, axis=0)
            o_ref[edge, half] = placed.astype(o_ref.dtype)


def _ctxatt_kernel(q_ref, k_ref, v_ref, o_ref):
    lc = q_ref.shape[0]
    low = lax.broadcasted_iota(jnp.int32, (lc, LANES), 1) < HEAD_DIM
    lhs = _stack_heads(q_ref[...], low)
    s = _dot_nt(lhs, k_ref[...])
    m = jnp.max(s, axis=-1, keepdims=True)
    p = jnp.exp2(s - m)
    denom = jnp.sum(p, axis=-1, keepdims=True)
    o = _dot(p.astype(jnp.bfloat16), v_ref[...]) * (1.0 / denom)
    o_ref[...] = jnp.where(low, o[:lc], o[lc:]).astype(jnp.bfloat16)


def _ctxatt_call(q, k, v):
    lc = q.shape[0]
    spec = pl.BlockSpec((lc, LANES), lambda p: (0, p))
    return pl.pallas_call(
        _ctxatt_kernel,
        grid=(HEAD_PAIRS,),
        in_specs=[spec, spec, spec],
        out_specs=spec,
        out_shape=jax.ShapeDtypeStruct((lc, NA_DIM), jnp.bfloat16),
        compiler_params=_cparams(("arbitrary",)),
        name="ctxatt",
    )(q, k, v)


OUTPROJ_CHAIN = 256
OUTPROJ_TILE = 1024
INPROJ_TILE = 1024
RESIDUAL_TILE = 2048


def _outproj_kernel(h_ref, mp_ref, att_ref, sg_ref, wo_ref, g1_ref, n2_ref, sh_ref, sc_ref,
                    wr_ref, br_ref, h1_ref, pay_ref, route_ref):
    chains = [slice(c * OUTPROJ_CHAIN, (c + 1) * OUTPROJ_CHAIN) for c in range(h_ref.shape[0] // OUTPROJ_CHAIN)]
    h1s = []
    for rows in chains:
        mix = (_dot(mp_ref[rows, :], wo_ref[0:POOL_DIM, :])
               + _dot(att_ref[rows, :], wo_ref[POOL_DIM:POOL_DIM + NA_DIM, :])
               + _dot(sg_ref[rows, :], wo_ref[POOL_DIM + NA_DIM:, :]))
        h1 = h_ref[rows, :].astype(jnp.float32) + g1_ref[...] * mix
        h1_ref[rows, :] = h1.astype(h1_ref.dtype)
        h1s.append(h1)
    for rows, h1 in zip(chains, h1s):
        _outproj_route(rows, h1, n2_ref, sh_ref, sc_ref, wr_ref, br_ref, pay_ref, route_ref)


def _outproj_route(rows, h1, n2_ref, sh_ref, sc_ref, wr_ref, br_ref, pay_ref, route_ref):
    tm = OUTPROJ_CHAIN
    ms = jnp.mean(h1 * h1, axis=-1, keepdims=True)
    hm = (h1 * lax.rsqrt(ms + EPS)) * (n2_ref[...] * (1.0 + sc_ref[...])) + sh_ref[...]
    pay_ref[rows, 0:HALF_D] = _pack_halves(hm)

    hm_hi = hm.astype(jnp.bfloat16)
    lt = _dot_nt(wr_ref[...], hm_hi)
    logits = lt[:N_EXPERTS] + lt[N_EXPERTS:] + br_ref[...]
    e = jnp.exp(logits - jnp.max(logits, axis=0, keepdims=True))

    best = ga = gb = e1 = e2 = cls = None
    for c in range(N_CLASSES):
        a, b2 = int(CLASS_E1[c]), int(CLASS_E2[c])
        ea, eb = e[a:a + 1, :], e[b2:b2 + 1, :]
        s = ea + eb
        if best is None:
            best, ga, gb = s, ea, eb
            e1 = jnp.full_like(s, float(a))
            e2 = jnp.full_like(s, float(b2))
            cls = jnp.zeros_like(s)
        else:
            better = s > best
            best = jnp.where(better, s, best)
            ga = jnp.where(better, ea, ga)
            gb = jnp.where(better, eb, gb)
            e1 = jnp.where(better, float(a), e1)
            e2 = jnp.where(better, float(b2), e2)
            cls = jnp.where(better, float(c), cls)
    inv = 1.0 / best
    row = lax.broadcasted_iota(jnp.int32, (ROUTE_ROWS, tm), 0)
    rec = jnp.where(row == 0, ga * inv,
          jnp.where(row == 1, gb * inv,
          jnp.where(row == 2, e1,
          jnp.where(row == 3, e2,
          jnp.where(row == 4, cls, 0.0)))))
    route_ref[:, rows] = rec
    wide = jnp.concatenate([rec, jnp.zeros((LANES - ROUTE_ROWS, tm), jnp.float32)], axis=0)
    pay_ref[rows, HALF_D:] = pltpu.bitcast(wide.T, jnp.uint32)


def _outproj_call(h, mp, att, sg, w_out, layer, g1, n2, sh2, sc2, wr, br, tm):
    n = h.shape[0]
    assert n % tm == 0 and tm % OUTPROJ_CHAIN == 0
    row = lambda i: (i, 0)
    fixed = lambda i: (0, 0)
    vec = pl.BlockSpec((1, D_MODEL), fixed)
    return pl.pallas_call(
        _outproj_kernel,
        grid=(n // tm,),
        in_specs=[
            pl.BlockSpec((tm, D_MODEL), row),
            pl.BlockSpec((tm, POOL_DIM), row),
            pl.BlockSpec((tm, NA_DIM), row),
            pl.BlockSpec((tm, SG_DIM), row),
            pl.BlockSpec((None, D_MODEL, D_MODEL), lambda i: (layer, 0, 0)),
            vec, vec, vec, vec,
            pl.BlockSpec((2 * N_EXPERTS, D_MODEL), fixed),
            pl.BlockSpec((N_EXPERTS, 1), fixed),
        ],
        out_specs=[
            pl.BlockSpec((tm, D_MODEL), row),
            pl.BlockSpec((tm, PAYLOAD_W), row),
            pl.BlockSpec((ROUTE_ROWS, tm), lambda i: (0, i)),
        ],
        out_shape=[
            jax.ShapeDtypeStruct((n, D_MODEL), STREAM_DTYPE),
            jax.ShapeDtypeStruct((n, PAYLOAD_W), jnp.uint32),
            jax.ShapeDtypeStruct((ROUTE_ROWS, n), jnp.float32),
        ],
        compiler_params=_cparams(("arbitrary",)),
        name="outproj",
    )(h, mp, att, sg, w_out, g1, n2, sh2, sc2, wr, br)


def _payload_parts(pay_ref, rows=slice(None)):
    lo, hi = _unpack_halves(pay_ref[rows, 0:HALF_D])
    x = jnp.concatenate([lo, hi], axis=-1).astype(jnp.bfloat16)
    return x, pltpu.bitcast(pay_ref[rows, HALF_D:], jnp.float32)


def _expert_pair(x, ga, gb, wga, wua, wda, wgb, wub, wdb):
    ha = (_silu(_dot(x, wga)) * _dot(x, wua) * ga).astype(jnp.bfloat16)
    hb = (_silu(_dot(x, wgb)) * _dot(x, wub) * gb).astype(jnp.bfloat16)
    return _dot(ha, wda) + _dot(hb, wdb)


MOE_PART = MOE_TM // 8


def _moe_sorted_kernel(blk_ref, e1_ref, e2_ref, parts_ref, pay_ref,
                       wga_ref, wua_ref, wda_ref, wgb_ref, wub_ref, wdb_ref, o_ref):
    parts = parts_ref[pl.program_id(0)]

    def run(rows):
        x, route = _payload_parts(pay_ref, rows)
        y = _expert_pair(x, route[:, 0:1], route[:, 1:2], wga_ref[...], wua_ref[...], wda_ref[...],
                         wgb_ref[...], wub_ref[...], wdb_ref[...])
        o_ref[rows, :] = _pack_halves(y)

    for used in range(1, MOE_TM // MOE_PART + 1):
        @pl.when(parts == used)
        def _(used=used):
            run(slice(0, used * MOE_PART))


def _moe_sorted_call(blk, e1, e2, parts, pay_sorted, wg, wu, wd):
    n_tiles = blk.shape[0]
    rows = lambda i, blk, e1, e2, parts: (blk[i], 0)
    wa = lambda i, blk, e1, e2, parts: (e1[i], 0, 0)
    wb = lambda i, blk, e1, e2, parts: (e2[i], 0, 0)
    up = lambda m: pl.BlockSpec((None, D_MODEL, D_EXPERT), m)
    down = lambda m: pl.BlockSpec((None, D_EXPERT, D_MODEL), m)
    return pl.pallas_call(
        _moe_sorted_kernel,
        grid_spec=pltpu.PrefetchScalarGridSpec(
            num_scalar_prefetch=4,
            grid=(n_tiles,),
            in_specs=[pl.BlockSpec((MOE_TM, PAYLOAD_W), rows),
                      up(wa), up(wa), down(wa), up(wb), up(wb), down(wb)],
            out_specs=pl.BlockSpec((MOE_TM, HALF_D), rows),
        ),
        out_shape=jax.ShapeDtypeStruct((n_tiles * MOE_TM, HALF_D), jnp.uint32),
        compiler_params=_cparams(("arbitrary",)),
        name="moe_sorted",
    )(blk, e1, e2, parts, pay_sorted, wg, wu, wd, wg, wu, wd)


DENSE_EXPERTS_PER_STEP = 4


def _moe_dense_kernel(*refs, first_step, resume):
    if resume:
        pay_ref, wg_ref, wu_ref, wd_ref, acc_ref, o_ref = refs
    else:
        pay_ref, wg_ref, wu_ref, wd_ref, o_ref = refs
    step = pl.program_id(0)

    @pl.when(step == 0)
    def _():
        o_ref[...] = acc_ref[...] if resume else jnp.zeros_like(o_ref)

    x, route = _payload_parts(pay_ref)
    y = None
    for k in range(DENSE_EXPERTS_PER_STEP):
        ef = ((first_step + step) * DENSE_EXPERTS_PER_STEP + k).astype(jnp.float32)
        gate = (jnp.where(route[:, 2:3] == ef, route[:, 0:1], 0.0)
                + jnp.where(route[:, 3:4] == ef, route[:, 1:2], 0.0))
        he = (_silu(_dot(x, wg_ref[k])) * _dot(x, wu_ref[k]) * gate).astype(jnp.bfloat16)
        yk = _dot(he, wd_ref[k])
        y = yk if y is None else y + yk
    o_ref[...] += y


def _moe_dense_call(pay, wg, wu, wd, first_step, n_steps, acc=None):
    n = pay.shape[0]
    per = DENSE_EXPERTS_PER_STEP
    whole = pl.BlockSpec((n, D_MODEL), lambda e: (0, 0))
    return pl.pallas_call(
        functools.partial(_moe_dense_kernel, first_step=first_step, resume=acc is not None),
        grid=(n_steps,),
        in_specs=[pl.BlockSpec((n, PAYLOAD_W), lambda e: (0, 0)),
                  pl.BlockSpec((per, D_MODEL, D_EXPERT), lambda e: (first_step + e, 0, 0)),
                  pl.BlockSpec((per, D_MODEL, D_EXPERT), lambda e: (first_step + e, 0, 0)),
                  pl.BlockSpec((per, D_EXPERT, D_MODEL), lambda e: (first_step + e, 0, 0))]
                 + ([whole] if acc is not None else []),
        out_specs=whole,
        out_shape=jax.ShapeDtypeStruct((n, D_MODEL), jnp.float32),
        compiler_params=_cparams(("arbitrary",)),
        name="moe_dense",
    )(pay, wg, wu, wd, *([acc] if acc is not None else []))


def _moe_out(y_ref):
    if y_ref.dtype == jnp.uint32:
        return jnp.concatenate(_unpack_halves(y_ref[...]), axis=-1)
    return y_ref[...]


def _residual_kernel(h_ref, y_ref, g_ref, *rest):
    o_ref = rest[-1]
    o_ref[...] = h_ref[...].astype(jnp.float32) + g_ref[...] * _moe_out(y_ref)


def _residual_call(h, y, g, tm, first_row=0, into=None):
    n = h.shape[0]
    first = first_row // tm
    row = pl.BlockSpec((tm, D_MODEL), lambda i: (first + i, 0))
    in_specs = [row, pl.BlockSpec((tm, y.shape[1]), lambda i: (i, 0)),
                pl.BlockSpec((1, D_MODEL), lambda i: (0, 0))]
    args = [h, y, g]
    aliases = {}
    if into is not None:
        in_specs.append(pl.BlockSpec(memory_space=pl.ANY))
        args.append(into)
        aliases = {len(args) - 1: 0}
    return pl.pallas_call(
        _residual_kernel,
        grid=(y.shape[0] // tm,),
        in_specs=in_specs,
        out_specs=row,
        out_shape=jax.ShapeDtypeStruct((n, D_MODEL), jnp.float32),
        input_output_aliases=aliases,
        compiler_params=_cparams(("arbitrary",)),
        name="residual",
    )(*args)


SC_ROWS = 128


SC_CORES = 2
SC_SUBCORES = 16
SC_WORKERS = SC_CORES * SC_SUBCORES


def _sc_mesh():
    return plsc.VectorSubcoreMesh(core_axis_name="core", subcore_axis_name="subcore")


def _sc_worker():
    return lax.axis_index("subcore") * SC_CORES + lax.axis_index("core")


def _scatter_rows(x, dest, n_out):
    n, w = x.shape

    per_worker = n // SC_WORKERS
    assert per_worker % SC_ROWS == 0

    @functools.partial(pl.kernel, out_type=jax.ShapeDtypeStruct((n_out, w), x.dtype), mesh=_sc_mesh(),
                       scratch_types=[pltpu.VMEM((SC_ROWS,), jnp.int32), pltpu.VMEM((SC_ROWS, w), x.dtype)])
    def scatter(x_hbm, i_hbm, o_hbm, idx_v, rows_v):
        first = _sc_worker() * per_worker

        @pl.loop(0, per_worker // SC_ROWS)
        def _(i):
            base = pl.multiple_of(first + i * SC_ROWS, SC_ROWS)
            pltpu.sync_copy(i_hbm.at[pl.ds(base, SC_ROWS)], idx_v)
            pltpu.sync_copy(x_hbm.at[pl.ds(base, SC_ROWS)], rows_v)
            pltpu.sync_copy(rows_v, o_hbm.at[idx_v])

    return scatter(x, dest)


def _gather_rows(x, src):
    n = src.shape[0]
    w = x.shape[1]

    per_worker = n // SC_WORKERS
    assert per_worker % SC_ROWS == 0

    @functools.partial(pl.kernel, out_type=jax.ShapeDtypeStruct((n, w), x.dtype), mesh=_sc_mesh(),
                       scratch_types=[pltpu.VMEM((SC_ROWS,), jnp.int32), pltpu.VMEM((SC_ROWS, w), x.dtype)])
    def gather(x_hbm, i_hbm, o_hbm, idx_v, rows_v):
        first = _sc_worker() * per_worker

        @pl.loop(0, per_worker // SC_ROWS)
        def _(i):
            base = pl.multiple_of(first + i * SC_ROWS, SC_ROWS)
            pltpu.sync_copy(i_hbm.at[pl.ds(base, SC_ROWS)], idx_v)
            pltpu.sync_copy(x_hbm.at[idx_v], rows_v)
            pltpu.sync_copy(rows_v, o_hbm.at[pl.ds(base, SC_ROWS)])

    return gather(x, src)


def _routing_plan(cls, n_tiles):
    onehot = (cls[:, None] == jnp.arange(N_CLASSES, dtype=jnp.int32)[None, :]).astype(jnp.int32)
    counts = jnp.sum(onehot, axis=0)
    rank = jnp.sum((jnp.cumsum(onehot, axis=0) - onehot) * onehot, axis=1)
    tiles = (counts + MOE_TM - 1) // MOE_TM
    tile_end = jnp.cumsum(tiles)
    tile_start = tile_end - tiles
    first_len = counts - (tiles - 1) * MOE_TM
    first_of = jnp.sum(onehot * first_len[None, :], axis=1)
    skip = jnp.where(rank >= first_of, MOE_TM - first_of, 0)
    dest = jnp.sum(onehot * tile_start[None, :], axis=1) * MOE_TM + rank + skip
    nact = tile_end[-1]
    tile = jnp.arange(n_tiles, dtype=jnp.int32)
    blk = jnp.minimum(tile, nact - 1)
    tile_cls = jnp.sum((blk[:, None] >= tile_end[None, :]).astype(jnp.int32), axis=1)
    e1 = jnp.asarray(CLASS_E1)[tile_cls]
    e2 = jnp.asarray(CLASS_E2)[tile_cls]
    valid = jnp.where(blk == tile_start[tile_cls], first_len[tile_cls], MOE_TM)
    parts = jnp.where(tile >= nact, 0, (valid + MOE_PART - 1) // MOE_PART)
    return dest.astype(jnp.int32), blk, e1, e2, parts.astype(jnp.int32)


def _row_tile(n, prefer=512):
    return next(t for t in (prefer, 512, 256) if n % t == 0)


def kernel(x, c, ctx, c_ctx, w_ada, b_ada, norm1, w_in, pool_w, pool_scale, q_norm, k_norm, rpb,
           sg_w, sg_b, sg_norm, w_out, norm2, w_router, b_router, w_gate, w_up, w_down):
    depth = w_ada.shape[0]
    n = x.shape[1]
    lc = ctx.shape[1]
    assert x.shape[0] == 1 and x.shape[2] == D_MODEL, "one sample of (tokens, 1024) features"
    assert n % (GRID_W * NA_ROWS_PER_BLOCK) == 0 and n % (2 * SC_WORKERS * SC_ROWS) == 0 and lc % 256 == 0
    bf = jnp.bfloat16
    lat_stream = (x[0],)
    h_ctx = ctx[0]

    cond = jnp.stack([c[0], c_ctx], axis=1)
    mod = _ada_call(cond, w_ada, b_ada)

    wr_t = w_router.T
    wr_hi = wr_t.astype(bf)
    wr_lo = (wr_t - wr_hi.astype(jnp.float32)).astype(bf)
    wr = jnp.concatenate([wr_hi, wr_lo], axis=0)
    br = b_router.reshape(N_EXPERTS, 1)
    n_tiles = n // MOE_TM + N_CLASSES
    w_in_bf = w_in.astype(bf)
    w_out_bf = w_out.astype(bf)

    for l in range(depth):
        last = l == depth - 1
        qg =(q_norm[l] * (HEAD_DIM ** -0.5 * LOG2_E)).reshape(1, NA_DIM)
        kg = k_norm[l].reshape(1, NA_DIM)
        sgn = sg_norm[l].reshape(1, SG_DIM)
        sgw = sg_w[l].astype(bf).reshape(SG_DIM // LANES, 2 * SG_CHUNK, SG_CHUNK)
        sgb = jnp.broadcast_to(sg_b[l].reshape(SG_DIM // LANES, 2 * SG_CHUNK, 1),
                               (SG_DIM // LANES, 2 * SG_CHUNK, LANES))
        pool_bd = jax.scipy.linalg.block_diag(*[pool_w[l, g] for g in range(len(POOL_WINDOWS))]).astype(bf)
        pscale = pool_scale[l].reshape(1, POOL_DIM)
        bias = _natten_bias(rpb[l])
        n1 = norm1[l].reshape(1, D_MODEL)
        n2 = norm2[l].reshape(1, D_MODEL)

        def mods(row):
            return [mod[l, row:row + 1, i * D_MODEL:(i + 1) * D_MODEL] for i in range(6)]

        sh1, sc1, g1, sh2, sc2, g2 = mods(0)
        csh1, csc1, cg1, csh2, csc2, cg2 = mods(1)

        tc = _row_tile(lc)
        mix_pool_c, q_c, k_c, v_c, sg_c = _inproj_call((h_ctx,), n1, csh1, csc1, w_in_bf, l, qg, kg,
                                                       sgn, sgw, sgb, pool_bd, pscale, tc, strips=False)

        outs = _inproj_call(lat_stream, n1, sh1, sc1, w_in_bf, l, qg, kg, sgn, sgw, sgb,
                            pool_bd, pscale, _row_tile(n, INPROJ_TILE), strips=True)
        mix_pool, q, k, v, sg = outs[:5]
        h_lat = outs[5] if len(lat_stream) == 3 else lat_stream[0]
        att, wg_l, wu_l, wd_l = _natten_call(q, k, v, k_c, v_c, bias, w_gate, w_up, w_down, l)
        h1, pay, route = _outproj_call(h_lat, mix_pool, att, sg, w_out_bf, l, g1, n2, sh2, sc2, wr, br,
                                       _row_tile(n, OUTPROJ_TILE))
        cls = route[4].astype(jnp.int32)
        dest, blk, e1, e2, parts = _routing_plan(cls, n_tiles)
        pay_sorted = _scatter_rows(pay, dest, n_tiles * MOE_TM)

        if not last:
            dense_steps = N_EXPERTS // DENSE_EXPERTS_PER_STEP
            att_c = _ctxatt_call(q_c, k_c, v_c)
            h1_c, pay_c, _ = _outproj_call(h_ctx, mix_pool_c, att_c, sg_c, w_out_bf, l, cg1, n2, csh2, csc2,
                                           wr, br, tc)
            y_c = _moe_dense_call(pay_c, wg_l, wu_l, wd_l, 0, dense_steps // 2)
            parts, y_c = lax.optimization_barrier((parts, y_c))

        y_sorted = _moe_sorted_call(blk, e1, e2, parts, pay_sorted, wg_l, wu_l, wd_l)

        if not last:
            lat_stream = (h1, _gather_rows(y_sorted, dest), g2)
            y_c = _moe_dense_call(pay_c, wg_l, wu_l, wd_l, dense_steps // 2, dense_steps - dense_steps // 2,
                                  acc=y_c)
            h_ctx = _residual_call(h1_c, y_c, cg2, tc)

    half = n // 2
    tr = _row_tile(half, RESIDUAL_TILE)
    out = _residual_call(h1, _gather_rows(y_sorted, dest[:half]), g2, tr)
    out = _residual_call(h1, _gather_rows(y_sorted, dest[half:]), g2, tr, first_row=half, into=out)
    return out[None]
```

```python
import functools

import jax
import jax.numpy as jnp
import numpy as np
from jax import lax
from jax.experimental import pallas as pl
from jax.experimental.pallas import tpu as pltpu
from jax.experimental.pallas import tpu_sc as plsc

D_MODEL = 1024
GRID_W = 64
HEAD_DIM = 64
POOL_WINDOWS = (2, 4, 8, 16)
POOL_DIM = 256
NA_HEADS = 8
NA_DIM = 512
NA_WIN_ROWS = 8
NA_WIN_COLS = 16
SG_DIM = 256
SG_CHUNK = 128
Q_OFF = POOL_DIM
K_OFF = Q_OFF + NA_DIM
V_OFF = K_OFF + NA_DIM
U_OFF = V_OFF + NA_DIM
G_OFF = U_OFF + SG_DIM
IN_DIM = G_OFF + SG_DIM
N_EXPERTS = 16
GROUP_SIZE = 4
D_EXPERT = 512
EPS = 1e-6

LANES = 128
SUBLANES = 8
HEAD_PAIRS = NA_DIM // LANES
VMEM_LIMIT = 48 * 1024 * 1024

PAIRS = ((0, 1), (0, 2), (1, 2), (1, 3), (0, 3), (2, 3))
N_GROUPS = N_EXPERTS // GROUP_SIZE
N_CLASSES = N_GROUPS * len(PAIRS)
CLASS_E1 = np.array([GROUP_SIZE * g + i for g in range(N_GROUPS) for (i, j) in PAIRS], np.int32)
CLASS_E2 = np.array([GROUP_SIZE * g + j for g in range(N_GROUPS) for (i, j) in PAIRS], np.int32)

ROUTE_ROWS = 8
HALF_D = D_MODEL // 2
PAYLOAD_W = HALF_D + LANES
MOE_TM = 512
NEG_BIG = -1e30
LOG2_E = 1.4426950408889634


def _cparams(sem):
    return pltpu.CompilerParams(dimension_semantics=sem, vmem_limit_bytes=VMEM_LIMIT)


def _dot(a, b):
    return jnp.dot(a, b, preferred_element_type=jnp.float32)


def _dot_nt(a, b):
    return lax.dot_general(a, b, (((1,), (1,)), ((), ())), preferred_element_type=jnp.float32)


def _gelu_tanh(x):
    return 0.5 * x * (1.0 + jnp.tanh(0.7978845608028654 * (x + 0.044715 * (x * x * x))))


def _silu(x):
    return x * (1.0 / (1.0 + jnp.exp(-x)))


def _pack_halves(x):
    w = x.shape[1] // 2
    lo = pltpu.bitcast(x[:, :w].astype(jnp.bfloat16).astype(jnp.float32), jnp.uint32) >> 16
    hi = pltpu.bitcast(x[:, w:].astype(jnp.bfloat16).astype(jnp.float32), jnp.uint32) & jnp.uint32(0xFFFF0000)
    return lo | hi


def _unpack_halves(words):
    lo = pltpu.bitcast(words << 16, jnp.float32)
    hi = pltpu.bitcast(words & jnp.uint32(0xFFFF0000), jnp.float32)
    return lo, hi


def _ada_kernel(cond_ref, w_ref, b_ref, o_ref):
    cond = _silu(cond_ref[...])
    w = w_ref[...]
    rows = [jnp.sum(w * cond[:, r:r + 1], axis=0, keepdims=True) + b_ref[...] for r in range(2)]
    o_ref[...] = jnp.concatenate(rows + [jnp.zeros((SUBLANES - 2, w.shape[1]), jnp.float32)], axis=0)


def _ada_call(cond, w_ada, b_ada):
    depth = w_ada.shape[0]
    tn = 1536
    return pl.pallas_call(
        _ada_kernel,
        grid=(depth, 6 * D_MODEL // tn),
        in_specs=[
            pl.BlockSpec((D_MODEL, 2), lambda l, j: (0, 0)),
            pl.BlockSpec((None, D_MODEL, tn), lambda l, j: (l, 0, j)),
            pl.BlockSpec((None, 1, tn), lambda l, j: (l, 0, j)),
        ],
        out_specs=pl.BlockSpec((None, SUBLANES, tn), lambda l, j: (l, 0, j)),
        out_shape=jax.ShapeDtypeStruct((depth, SUBLANES, 6 * D_MODEL), jnp.float32),
        compiler_params=_cparams(("arbitrary", "arbitrary")),
        name="adaln",
    )(cond, w_ada, b_ada.reshape(depth, 1, 6 * D_MODEL))


def _norm_modulate(x, n_ref, sh_ref, sc_ref):
    ms = jnp.mean(x * x, axis=-1, keepdims=True)
    return ((x * lax.rsqrt(ms + EPS)) * (n_ref[...] * (1.0 + sc_ref[...])) + sh_ref[...]).astype(jnp.bfloat16)


def _head_rms_scale(a):
    low = lax.broadcasted_iota(jnp.int32, (a.shape[0], LANES), 1) < HEAD_DIM
    blocks = []
    for p in range(a.shape[1] // LANES):
        sq = jnp.square(a[:, p * LANES:(p + 1) * LANES])
        s_lo = jnp.sum(jnp.where(low, sq, 0.0), axis=-1, keepdims=True)
        s_hi = jnp.sum(jnp.where(low, 0.0, sq), axis=-1, keepdims=True)
        blocks.append(lax.rsqrt(jnp.where(low, s_lo, s_hi) * (1.0 / HEAD_DIM) + EPS))
    return jnp.concatenate(blocks, axis=-1)


STRIP_W = 8
N_STRIPS = GRID_W // STRIP_W


def _store_keys(ref, x):
    if len(ref.shape) == 2:
        ref[...] = x.astype(jnp.bfloat16)
        return
    pair = 2 * STRIP_W
    for s in range(N_STRIPS):
        for rp in range(x.shape[0] // (2 * GRID_W)):
            top = 2 * rp * GRID_W + s * STRIP_W
            rows = jnp.concatenate([x[top:top + STRIP_W], x[top + GRID_W:top + GRID_W + STRIP_W]], axis=0)
            ref[s, rp * pair:(rp + 1) * pair, :] = rows.astype(jnp.bfloat16)


POOL_HALO = 8
HALO_BLOCK = 16
STREAM_DTYPE = jnp.bfloat16


POOL_EDGE = 16


def _pool_mix(xe_ref, w_ref, scale_ref, tm, seq_len):
    low = lax.broadcasted_iota(jnp.int32, (tm, LANES), 1) < HEAD_DIM
    t_edge = pl.program_id(0) * tm + lax.broadcasted_iota(jnp.int32, (POOL_EDGE, LANES), 0)

    def window_mean(s, half):
        mean = s * (1.0 / (2 * half))

        def clip_fix(t):
            count = (jnp.minimum(t + half, seq_len) - jnp.maximum(t - half, 0)).astype(jnp.float32)
            return (2.0 * half) / count

        return jnp.concatenate([mean[:POOL_EDGE] * clip_fix(t_edge),
                                mean[POOL_EDGE:tm - POOL_EDGE],
                                mean[tm - POOL_EDGE:] * clip_fix(t_edge + (tm - POOL_EDGE))], axis=0)

    def window_sums(xs, n_levels):
        sums = []
        s = xs
        for k in range(n_levels):
            step = 1 << k
            s = s[:-step] + s[step:]
            sums.append(s)
        return sums

    outs = []
    for half_block, windows in enumerate(((2, 4), (8, 16))):
        xs = xe_ref[:, half_block * LANES:(half_block + 1) * LANES]
        sums = window_sums(xs, int(np.log2(windows[1])))
        parts = []
        for w in windows:
            half = w // 2
            s = sums[int(np.log2(w)) - 1][POOL_HALO - half:POOL_HALO - half + tm]
            parts.append(window_mean(s, half))
        mean = jnp.where(low, parts[0], parts[1])
        outs.append(mean - xs[POOL_HALO:POOL_HALO + tm])
    d = jnp.concatenate(outs, axis=-1).astype(jnp.bfloat16)
    return (_dot(d, w_ref[...]) * scale_ref[...]).astype(jnp.bfloat16)


def _inproj_kernel(*refs, pending, seq_len):
    n_stream = 7 if pending else 3
    stream, refs = refs[:n_stream], refs[n_stream:]
    if pending:
        h_ref, y_ref, g_ref, hp_ref, hn_ref, yp_ref, yn_ref = stream
        hres_ref, refs = refs[-2], refs[:-2] + refs[-1:]
        x = h_ref[...].astype(jnp.float32) + g_ref[...] * _moe_out(y_ref)
        hres_ref[...] = x.astype(hres_ref.dtype)
        before = hp_ref[...].astype(jnp.float32) + g_ref[...] * _moe_out(yp_ref)
        after = hn_ref[...].astype(jnp.float32) + g_ref[...] * _moe_out(yn_ref)
    else:
        h_ref, hp_ref, hn_ref = stream
        x = h_ref[...].astype(jnp.float32)
        before = hp_ref[...].astype(jnp.float32)
        after = hn_ref[...].astype(jnp.float32)
    x_halo = jnp.concatenate([before[HALO_BLOCK - POOL_HALO:], after[:POOL_HALO]], axis=0)
    (n1_ref, sh_ref, sc_ref, w_ref, qg_ref, kg_ref, sgn_ref, sgw_ref, sgb_ref,
     pw_ref, ps_ref, pool_ref, q_ref, k_ref, v_ref, sg_ref, xe_ref) = refs
    tm = h_ref.shape[0]
    i = pl.program_id(0)
    hn = _norm_modulate(x, n1_ref, sh_ref, sc_ref)
    hn_halo = _norm_modulate(x_halo, n1_ref, sh_ref, sc_ref)

    a_halo = _dot(hn_halo, w_ref[:, 0:Q_OFF])
    xe_ref[0:POOL_HALO, :] = jnp.where(i > 0, a_halo[:POOL_HALO], 0.0)
    xe_ref[POOL_HALO:POOL_HALO + tm, :] = _dot(hn, w_ref[:, 0:Q_OFF])
    xe_ref[POOL_HALO + tm:, :] = jnp.where(i < pl.num_programs(0) - 1, a_halo[POOL_HALO:], 0.0)

    a_g = _dot(hn, w_ref[:, G_OFF:IN_DIM])
    a_u = _dot(hn, w_ref[:, U_OFF:G_OFF])
    a_q = _dot(hn, w_ref[:, Q_OFF:K_OFF])
    a_k = _dot(hn, w_ref[:, K_OFF:V_OFF])
    _store_keys(v_ref, _dot(hn, w_ref[:, V_OFF:U_OFF]))

    gv = _gelu_tanh(a_g)
    q_ref[...] = (a_q * _head_rms_scale(a_q) * qg_ref[...]).astype(jnp.bfloat16)
    _store_keys(k_ref, a_k * _head_rms_scale(a_k) * kg_ref[...])

    u = _gelu_tanh(a_u)
    vn = (gv * _head_rms_scale(gv) * sgn_ref[...]).astype(jnp.bfloat16)
    low = lax.broadcasted_iota(jnp.int32, (SG_CHUNK, LANES), 1) < HEAD_DIM
    for c in range(tm // SG_CHUNK):
        rows = slice(c * SG_CHUNK, (c + 1) * SG_CHUNK)
        for s in range(SG_DIM // LANES):
            cols = slice(s * LANES, (s + 1) * LANES)
            m = _dot(sgw_ref[s], vn[rows, cols]) + sgb_ref[s]
            mixed = jnp.where(low, m[:SG_CHUNK], m[SG_CHUNK:])
            sg_ref[rows, cols] = (u[rows, cols] * mixed).astype(jnp.bfloat16)

    pool_ref[...] = _pool_mix(xe_ref, pw_ref, ps_ref, tm, seq_len)


def _inproj_call(stream, n1, sh, sc, w_in, layer, qg, kg, sgn, sgw, sgb, pool_w, pool_scale, tm, strips):
    pending = len(stream) == 3
    n = stream[0].shape[0]
    assert n % tm == 0 and tm % SG_CHUNK == 0 and tm % (2 * GRID_W) == 0
    if strips:
        kv_spec = pl.BlockSpec((N_STRIPS, tm // N_STRIPS, NA_DIM), lambda i: (0, i, 0))
        kv_shape = jax.ShapeDtypeStruct((N_STRIPS, n // N_STRIPS, NA_DIM), jnp.bfloat16)
    else:
        kv_spec = pl.BlockSpec((tm, NA_DIM), lambda i: (i, 0))
        kv_shape = jax.ShapeDtypeStruct((n, NA_DIM), jnp.bfloat16)
    row = lambda i: (i, 0)
    fixed2 = lambda i: (0, 0)
    fixed3 = lambda i: (0, 0, 0)
    vec = lambda w: pl.BlockSpec((1, w), fixed2)
    rows = pl.BlockSpec((tm, D_MODEL), row)
    per_tile = tm // HALO_BLOCK
    before = lambda i: (jnp.maximum(i * per_tile - 1, 0), 0)
    after = lambda i: (jnp.minimum((i + 1) * per_tile, n // HALO_BLOCK - 1), 0)
    halo = lambda w, m: pl.BlockSpec((HALO_BLOCK, w), m)
    h = stream[0]
    if pending:
        y, g = stream[1], stream[2]
        yw = y.shape[1]
        args = [h, y, g, h, h, y, y]
        stream_specs = [rows, pl.BlockSpec((tm, yw), row), vec(D_MODEL),
                        halo(D_MODEL, before), halo(D_MODEL, after), halo(yw, before), halo(yw, after)]
    else:
        args = [h, h, h]
        stream_specs = [rows, halo(D_MODEL, before), halo(D_MODEL, after)]
    extra_out_specs = [rows] if pending else []
    extra_out_shape = [jax.ShapeDtypeStruct((n, D_MODEL), STREAM_DTYPE)] if pending else []
    return pl.pallas_call(
        functools.partial(_inproj_kernel, pending=pending, seq_len=n),
        grid=(n // tm,),
        in_specs=stream_specs + [
            vec(D_MODEL), vec(D_MODEL), vec(D_MODEL),
            pl.BlockSpec((None, D_MODEL, IN_DIM), lambda i: (layer, 0, 0)),
            vec(NA_DIM), vec(NA_DIM),
            vec(SG_DIM),
            pl.BlockSpec((SG_DIM // LANES, 2 * SG_CHUNK, SG_CHUNK), fixed3),
            pl.BlockSpec((SG_DIM // LANES, 2 * SG_CHUNK, LANES), fixed3),
            pl.BlockSpec((POOL_DIM, POOL_DIM), fixed2),
            vec(POOL_DIM),
        ],
        out_specs=[
            pl.BlockSpec((tm, POOL_DIM), row),
            pl.BlockSpec((tm, NA_DIM), row),
            kv_spec,
            kv_spec,
            pl.BlockSpec((tm, SG_DIM), row),
        ] + extra_out_specs,
        out_shape=[
            jax.ShapeDtypeStruct((n, POOL_DIM), jnp.bfloat16),
            jax.ShapeDtypeStruct((n, NA_DIM), jnp.bfloat16),
            kv_shape,
            kv_shape,
            jax.ShapeDtypeStruct((n, SG_DIM), jnp.bfloat16),
        ] + extra_out_shape,
        scratch_shapes=[pltpu.VMEM((tm + 2 * POOL_HALO, POOL_DIM), jnp.float32)],
        compiler_params=_cparams(("arbitrary",)),
        name="inproj",
    )(*args, n1, sh, sc, w_in, qg, kg, sgn, sgw, sgb, pool_w, pool_scale)


NA_ROWS_PER_BLOCK = 64
NA_GROUP_ROWS = 4
NA_WINDOW_ROWS = NA_GROUP_ROWS + NA_WIN_ROWS
NA_BLOCK = NA_ROWS_PER_BLOCK * GRID_W
NA_GROUP = NA_GROUP_ROWS * GRID_W
NA_HALF_COLS = GRID_W // 2
NA_CHAIN = NA_GROUP_ROWS * NA_HALF_COLS
NA_HALF_STRIPS = 5
NA_HALF_COL0 = (0, GRID_W - NA_HALF_STRIPS * STRIP_W)
NA_RUN = NA_WINDOW_ROWS * STRIP_W
NA_LOCAL = 512
NA_STRIP_BLOCK = NA_ROWS_PER_BLOCK * STRIP_W
NA_STRIP_HALO = (NA_WIN_ROWS // 2) * STRIP_W
NA_EDGE_FIRST, NA_EDGE_NONE, NA_EDGE_LAST = 0, 1, 2


def _stack_heads(x, low):
    zero = jnp.zeros_like(x)
    return jnp.concatenate([jnp.where(low, x, zero), jnp.where(low, zero, x)], axis=0)


def _natten_kernel(q_ref, kp_ref, kc_ref, kn_ref, vp_ref, vc_ref, vn_ref, kx_ref, vx_ref, bias_ref,
                   wg_ref, wu_ref, wd_ref,
                   o_ref, wg_bf_ref, wu_bf_ref, wd_bf_ref, kwin_ref, vwin_ref, vxe_ref, *, grid_rows):
    b = pl.program_id(1)
    wg_bf_ref[...] = wg_ref[...].astype(jnp.bfloat16)
    wu_bf_ref[...] = wu_ref[...].astype(jnp.bfloat16)
    wd_bf_ref[...] = wd_ref[...].astype(jnp.bfloat16)
    top, bottom = NA_STRIP_HALO, NA_STRIP_HALO + NA_STRIP_BLOCK
    kwin_ref[:, 0:top, :] = kp_ref[...]
    kwin_ref[:, top:bottom, :] = kc_ref[...]
    kwin_ref[:, bottom:, :] = kn_ref[...]
    vwin_ref[:, 0:top, 0:LANES] = vp_ref[...]
    vwin_ref[:, top:bottom, 0:LANES] = vc_ref[...]
    vwin_ref[:, bottom:, 0:LANES] = vn_ref[...]
    vwin_ref[:, :, LANES:] = jnp.ones(vwin_ref.shape[:2] + (LANES,), jnp.bfloat16)
    vxe_ref[:, 0:LANES] = vx_ref[...]
    vxe_ref[:, LANES:] = jnp.ones((vxe_ref.shape[0], LANES), jnp.bfloat16)
    low_q = lax.broadcasted_iota(jnp.int32, (NA_CHAIN, LANES), 1) < HEAD_DIM
    n_pad = NA_LOCAL - NA_HALF_STRIPS * NA_RUN

    def window_start(g):
        r0 = b * NA_ROWS_PER_BLOCK + g * NA_GROUP_ROWS
        ws = jnp.clip(r0 - NA_WIN_ROWS // 2, 0, grid_rows - NA_WINDOW_ROWS)
        edge = jnp.where(r0 == 0, NA_EDGE_FIRST,
                         jnp.where(r0 == grid_rows - NA_GROUP_ROWS, NA_EDGE_LAST, NA_EDGE_NONE))
        start = pl.multiple_of((ws - b * NA_ROWS_PER_BLOCK + NA_WIN_ROWS // 2) * STRIP_W, NA_STRIP_HALO)
        return start, edge

    def local_window(win_ref, g, half):
        start, _ = window_start(g)
        s0 = NA_HALF_COL0[half] // STRIP_W
        runs = [win_ref[s, pl.ds(start, NA_RUN), :] for s in range(s0, s0 + NA_HALF_STRIPS)]
        return jnp.concatenate(runs + [jnp.zeros((n_pad, win_ref.shape[2]), jnp.bfloat16)], axis=0)

    def query_rows(g, half, j):
        first = g * NA_GROUP + j * GRID_W + half * NA_HALF_COLS
        return slice(first, first + NA_HALF_COLS)

    def scores(c):
        g, half = divmod(c, 2)
        _, edge = window_start(g)
        qh = jnp.concatenate([q_ref[query_rows(g, half, j), :] for j in range(NA_GROUP_ROWS)], axis=0)
        lhs = _stack_heads(qh, low_q)
        kl = local_window(kwin_ref, g, half)
        return jnp.concatenate([_dot_nt(lhs, kl).astype(jnp.bfloat16) + bias_ref[edge, half],
                                _dot_nt(lhs, kx_ref[...]).astype(jnp.bfloat16)], axis=-1)

    n_chains = 2 * (NA_ROWS_PER_BLOCK // NA_GROUP_ROWS)
    s_next = scores(0)
    for c in range(n_chains):
        s = s_next
        if c + 1 < n_chains:
            s_next = scores(c + 1)
        g, half = divmod(c, 2)
        vl = local_window(vwin_ref, g, half)
        m = jnp.max(s, axis=-1, keepdims=True)
        pb = jnp.exp2(s - m)
        o = _dot(pb[:, :NA_LOCAL], vl) + _dot(pb[:, NA_LOCAL:], vxe_ref[...])
        o = o[:, :LANES] * (1.0 / o[:, LANES:])
        o = jnp.where(low_q, o[:NA_CHAIN], o[NA_CHAIN:]).astype(jnp.bfloat16)
        for j in range(NA_GROUP_ROWS):
            o_ref[query_rows(g, half, j), :] = o[j * NA_HALF_COLS:(j + 1) * NA_HALF_COLS]


def _natten_call(q, k, v, k_ctx, v_ctx, bias, w_gate, w_up, w_down, layer):
    n = q.shape[0]
    grid_rows = n // GRID_W
    assert grid_rows % NA_ROWS_PER_BLOCK == 0 and grid_rows >= 2 * NA_ROWS_PER_BLOCK
    nblk = n // NA_BLOCK
    steps = HEAD_PAIRS * nblk
    depth = w_gate.shape[0]
    up_rows = N_EXPERTS * D_MODEL
    down_rows = N_EXPERTS * D_EXPERT
    assert up_rows % steps == 0 and down_rows % steps == 0
    wg2 = w_gate.reshape(depth * up_rows, D_EXPERT)
    wu2 = w_up.reshape(depth * up_rows, D_EXPERT)
    wd2 = w_down.reshape(depth * down_rows, D_MODEL)
    up_in = pl.BlockSpec((up_rows // steps, D_EXPERT), lambda p, b: (layer * steps + p * nblk + b, 0))
    down_in = pl.BlockSpec((down_rows // steps, D_MODEL), lambda p, b: (layer * steps + p * nblk + b, 0))
    up_out = pl.BlockSpec((up_rows // steps, D_EXPERT), lambda p, b: (p * nblk + b, 0))
    down_out = pl.BlockSpec((down_rows // steps, D_MODEL), lambda p, b: (p * nblk + b, 0))
    n_halo = n // N_STRIPS // NA_STRIP_HALO
    hb = NA_STRIP_BLOCK // NA_STRIP_HALO
    rows = pl.BlockSpec((NA_BLOCK, LANES), lambda p, b: (b, p))
    cur = pl.BlockSpec((N_STRIPS, NA_STRIP_BLOCK, LANES), lambda p, b: (0, b, p))
    prev = pl.BlockSpec((N_STRIPS, NA_STRIP_HALO, LANES), lambda p, b: (0, jnp.maximum(b * hb - 1, 0), p))
    nxt = pl.BlockSpec((N_STRIPS, NA_STRIP_HALO, LANES),
                       lambda p, b: (0, jnp.minimum((b + 1) * hb, n_halo - 1), p))
    ctx = pl.BlockSpec((k_ctx.shape[0], LANES), lambda p, b: (0, p))
    win_rows = NA_STRIP_BLOCK + 2 * NA_STRIP_HALO
    att, wg_bf, wu_bf, wd_bf = pl.pallas_call(
        functools.partial(_natten_kernel, grid_rows=grid_rows),
        grid=(HEAD_PAIRS, nblk),
        in_specs=[rows, prev, cur, nxt, prev, cur, nxt, ctx, ctx,
                  pl.BlockSpec((None, 3, 2, 2 * NA_CHAIN, NA_LOCAL), lambda p, b: (p, 0, 0, 0, 0)),
                  up_in, up_in, down_in],
        out_specs=[rows, up_out, up_out, down_out],
        out_shape=[jax.ShapeDtypeStruct((n, NA_DIM), jnp.bfloat16),
                   jax.ShapeDtypeStruct((up_rows, D_EXPERT), jnp.bfloat16),
                   jax.ShapeDtypeStruct((up_rows, D_EXPERT), jnp.bfloat16),
                   jax.ShapeDtypeStruct((down_rows, D_MODEL), jnp.bfloat16)],
        scratch_shapes=[pltpu.VMEM((N_STRIPS, win_rows, LANES), jnp.bfloat16),
                        pltpu.VMEM((N_STRIPS, win_rows, 2 * LANES), jnp.bfloat16),
                        pltpu.VMEM((k_ctx.shape[0], 2 * LANES), jnp.bfloat16)],
        compiler_params=_cparams(("arbitrary", "arbitrary")),
        name="natten",
    )(q, k, k, k, v, v, v, k_ctx, v_ctx, bias, wg2, wu2, wd2)
    return (att, wg_bf.reshape(N_EXPERTS, D_MODEL, D_EXPERT), wu_bf.reshape(N_EXPERTS, D_MODEL, D_EXPERT),
            wd_bf.reshape(N_EXPERTS, D_EXPERT, D_MODEL))


def _natten_bias(rpb):
    cols = np.arange(GRID_W)
    col_start = np.clip(cols - NA_WIN_COLS // 2, 0, GRID_W - NA_WIN_COLS)
    kc = np.arange(GRID_W)
    in_win = (kc[None, :] >= col_start[:, None]) & (kc[None, :] < col_start[:, None] + NA_WIN_COLS)
    dc = kc[None, :] - cols[:, None] + NA_WIN_COLS - 1
    sel = (np.arange(2 * NA_WIN_COLS - 1)[:, None, None] == dc[None]) & in_win[None]
    t2 = jnp.einsum("hdj,jqk->hdqk", rpb, jnp.asarray(sel, jnp.float32), precision=lax.Precision.HIGHEST)
    t2 = jnp.where(in_win[None, None], t2 * LOG2_E, NEG_BIG)
    neg = jnp.full((NA_HEADS, 1, GRID_W, GRID_W), NEG_BIG, jnp.float32)
    t2e = jnp.concatenate([neg, t2, neg], axis=1)
    u = jnp.concatenate([t2e[:, :-1], t2e[:, 1:]], axis=-1)
    u = u.reshape(HEAD_PAIRS, 2, 2 * NA_WIN_ROWS, GRID_W, LANES)

    place = np.zeros((2, NA_WINDOW_ROWS * GRID_W, NA_LOCAL), np.float32)
    for half, c0 in enumerate(NA_HALF_COL0):
        for a in range(NA_WINDOW_ROWS):
            for kcol in range(c0, c0 + NA_HALF_STRIPS * STRIP_W):
                s, c8 = divmod(kcol - c0, STRIP_W)
                place[half, a * GRID_W + kcol, s * NA_RUN + a * STRIP_W + c8] = 1.0
    outside = np.full((3, NA_GROUP_ROWS, NA_LOCAL), NEG_BIG, np.float32)
    for edge in (NA_EDGE_FIRST, NA_EDGE_NONE, NA_EDGE_LAST):
        for j in range(NA_GROUP_ROWS):
            lo, _ = _window_rows(edge, j)
            for s in range(NA_HALF_STRIPS):
                outside[edge, j, s * NA_RUN + lo * STRIP_W:s * NA_RUN + (lo + NA_WIN_ROWS) * STRIP_W] = 0.0
    return pl.pallas_call(
        _bias_expand_kernel,
        grid=(HEAD_PAIRS,),
        in_specs=[pl.BlockSpec((None, 2, 2 * NA_WIN_ROWS, GRID_W, LANES), lambda p: (p, 0, 0, 0, 0)),
                  pl.BlockSpec(place.shape, lambda p: (0, 0, 0)),
                  pl.BlockSpec(outside.shape, lambda p: (0, 0, 0))],
        out_specs=pl.BlockSpec((None, 3, 2, 2 * NA_CHAIN, NA_LOCAL), lambda p: (p, 0, 0, 0, 0)),
        out_shape=jax.ShapeDtypeStruct((HEAD_PAIRS, 3, 2, 2 * NA_CHAIN, NA_LOCAL), jnp.bfloat16),
        compiler_params=_cparams(("arbitrary",)),
        name="bias_expand",
    )(u, jnp.asarray(place, jnp.bfloat16), jnp.asarray(outside))


def _window_rows(edge, j):
    if edge == NA_EDGE_FIRST:
        return 0, NA_WIN_ROWS - 1 - j
    if edge == NA_EDGE_NONE:
        return j, NA_WIN_ROWS // 2 - 1
    return NA_WINDOW_ROWS - NA_WIN_ROWS, NA_WIN_ROWS // 2 - 1 - j


def _bias_expand_kernel(u_ref, place_ref, outside_ref, o_ref):
    low = lax.broadcasted_iota(jnp.int32, (NA_HALF_COLS, LANES), 1) < GRID_W
    zero = jnp.zeros((NA_HALF_COLS, LANES), jnp.float32)
    for edge in (NA_EDGE_FIRST, NA_EDGE_NONE, NA_EDGE_LAST):
        for half in range(2):
            q0 = half * NA_HALF_COLS
            blocks, masks = [], []
            for hd in range(2):
                for j in range(NA_GROUP_ROWS):
                    lo, base = _window_rows(edge, j)
                    tiles = []
                    for i in range(NA_WINDOW_ROWS // 2):
                        a0, a1 = 2 * i, 2 * i + 1
                        ok0 = lo <= a0 < lo + NA_WIN_ROWS
                        ok1 = lo <= a1 < lo + NA_WIN_ROWS
                        if not (ok0 or ok1):
                            tile = zero
                        else:
                            tile = u_ref[hd, base + a1 - lo, q0:q0 + NA_HALF_COLS, :]
                            if not ok0:
                                tile = jnp.where(low, zero, tile)
                            if not ok1:
                                tile = jnp.where(low, tile, zero)
                        tiles.append(tile)
                    blocks.append(jnp.concatenate(tiles, axis=-1))
                    masks.append(jnp.broadcast_to(outside_ref[edge, j:j + 1, :], (NA_HALF_COLS, NA_LOCAL)))
            lhs = jnp.concatenate(blocks, axis=0).astype(jnp.bfloat16)
            placed = _dot(lhs, place_ref[half]) + jnp.concatenate(masks, axis=0)
            o_ref[edge, half] = placed.astype(o_ref.dtype)


def _ctxatt_kernel(q_ref, k_ref, v_ref, o_ref):
    lc = q_ref.shape[0]
    low = lax.broadcasted_iota(jnp.int32, (lc, LANES), 1) < HEAD_DIM
    lhs = _stack_heads(q_ref[...], low)
    s = _dot_nt(lhs, k_ref[...])
    m = jnp.max(s, axis=-1, keepdims=True)
    p = jnp.exp2(s - m)
    denom = jnp.sum(p, axis=-1, keepdims=True)
    o = _dot(p.astype(jnp.bfloat16), v_ref[...]) * (1.0 / denom)
    o_ref[...] = jnp.where(low, o[:lc], o[lc:]).astype(jnp.bfloat16)


def _ctxatt_call(q, k, v):
    lc = q.shape[0]
    spec = pl.BlockSpec((lc, LANES), lambda p: (0, p))
    return pl.pallas_call(
        _ctxatt_kernel,
        grid=(HEAD_PAIRS,),
        in_specs=[spec, spec, spec],
        out_specs=spec,
        out_shape=jax.ShapeDtypeStruct((lc, NA_DIM), jnp.bfloat16),
        compiler_params=_cparams(("arbitrary",)),
        name="ctxatt",
    )(q, k, v)


OUTPROJ_CHAIN = 256
OUTPROJ_TILE = 1024
INPROJ_TILE = 1024
RESIDUAL_TILE = 2048


def _outproj_kernel(h_ref, mp_ref, att_ref, sg_ref, wo_ref, g1_ref, n2_ref, sh_ref, sc_ref,
                    wr_ref, br_ref, h1_ref, pay_ref, route_ref):
    chains = [slice(c * OUTPROJ_CHAIN, (c + 1) * OUTPROJ_CHAIN) for c in range(h_ref.shape[0] // OUTPROJ_CHAIN)]
    h1s = []
    for rows in chains:
        mix = (_dot(mp_ref[rows, :], wo_ref[0:POOL_DIM, :])
               + _dot(att_ref[rows, :], wo_ref[POOL_DIM:POOL_DIM + NA_DIM, :])
               + _dot(sg_ref[rows, :], wo_ref[POOL_DIM + NA_DIM:, :]))
        h1 = h_ref[rows, :].astype(jnp.float32) + g1_ref[...] * mix
        h1_ref[rows, :] = h1.astype(h1_ref.dtype)
        h1s.append(h1)
    for rows, h1 in zip(chains, h1s):
        _outproj_route(rows, h1, n2_ref, sh_ref, sc_ref, wr_ref, br_ref, pay_ref, route_ref)


def _outproj_route(rows, h1, n2_ref, sh_ref, sc_ref, wr_ref, br_ref, pay_ref, route_ref):
    tm = OUTPROJ_CHAIN
    ms = jnp.mean(h1 * h1, axis=-1, keepdims=True)
    hm = (h1 * lax.rsqrt(ms + EPS)) * (n2_ref[...] * (1.0 + sc_ref[...])) + sh_ref[...]
    pay_ref[rows, 0:HALF_D] = _pack_halves(hm)

    hm_hi = hm.astype(jnp.bfloat16)
    lt = _dot_nt(wr_ref[...], hm_hi)
    logits = lt[:N_EXPERTS] + lt[N_EXPERTS:] + br_ref[...]
    e = jnp.exp(logits - jnp.max(logits, axis=0, keepdims=True))

    best = ga = gb = e1 = e2 = cls = None
    for c in range(N_CLASSES):
        a, b2 = int(CLASS_E1[c]), int(CLASS_E2[c])
        ea, eb = e[a:a + 1, :], e[b2:b2 + 1, :]
        s = ea + eb
        if best is None:
            best, ga, gb = s, ea, eb
            e1 = jnp.full_like(s, float(a))
            e2 = jnp.full_like(s, float(b2))
            cls = jnp.zeros_like(s)
        else:
            better = s > best
            best = jnp.where(better, s, best)
            ga = jnp.where(better, ea, ga)
            gb = jnp.where(better, eb, gb)
            e1 = jnp.where(better, float(a), e1)
            e2 = jnp.where(better, float(b2), e2)
            cls = jnp.where(better, float(c), cls)
    inv = 1.0 / best
    row = lax.broadcasted_iota(jnp.int32, (ROUTE_ROWS, tm), 0)
    rec = jnp.where(row == 0, ga * inv,
          jnp.where(row == 1, gb * inv,
          jnp.where(row == 2, e1,
          jnp.where(row == 3, e2,
          jnp.where(row == 4, cls, 0.0)))))
    route_ref[:, rows] = rec
    wide = jnp.concatenate([rec, jnp.zeros((LANES - ROUTE_ROWS, tm), jnp.float32)], axis=0)
    pay_ref[rows, HALF_D:] = pltpu.bitcast(wide.T, jnp.uint32)


def _outproj_call(h, mp, att, sg, w_out, layer, g1, n2, sh2, sc2, wr, br, tm):
    n = h.shape[0]
    assert n % tm == 0 and tm % OUTPROJ_CHAIN == 0
    row = lambda i: (i, 0)
    fixed = lambda i: (0, 0)
    vec = pl.BlockSpec((1, D_MODEL), fixed)
    return pl.pallas_call(
        _outproj_kernel,
        grid=(n // tm,),
        in_specs=[
            pl.BlockSpec((tm, D_MODEL), row),
            pl.BlockSpec((tm, POOL_DIM), row),
            pl.BlockSpec((tm, NA_DIM), row),
            pl.BlockSpec((tm, SG_DIM), row),
            pl.BlockSpec((None, D_MODEL, D_MODEL), lambda i: (layer, 0, 0)),
            vec, vec, vec, vec,
            pl.BlockSpec((2 * N_EXPERTS, D_MODEL), fixed),
            pl.BlockSpec((N_EXPERTS, 1), fixed),
        ],
        out_specs=[
            pl.BlockSpec((tm, D_MODEL), row),
            pl.BlockSpec((tm, PAYLOAD_W), row),
            pl.BlockSpec((ROUTE_ROWS, tm), lambda i: (0, i)),
        ],
        out_shape=[
            jax.ShapeDtypeStruct((n, D_MODEL), STREAM_DTYPE),
            jax.ShapeDtypeStruct((n, PAYLOAD_W), jnp.uint32),
            jax.ShapeDtypeStruct((ROUTE_ROWS, n), jnp.float32),
        ],
        compiler_params=_cparams(("arbitrary",)),
        name="outproj",
    )(h, mp, att, sg, w_out, g1, n2, sh2, sc2, wr, br)


def _payload_parts(pay_ref, rows=slice(None)):
    lo, hi = _unpack_halves(pay_ref[rows, 0:HALF_D])
    x = jnp.concatenate([lo, hi], axis=-1).astype(jnp.bfloat16)
    return x, pltpu.bitcast(pay_ref[rows, HALF_D:], jnp.float32)


def _expert_pair(x, ga, gb, wga, wua, wda, wgb, wub, wdb):
    ha = (_silu(_dot(x, wga)) * _dot(x, wua) * ga).astype(jnp.bfloat16)
    hb = (_silu(_dot(x, wgb)) * _dot(x, wub) * gb).astype(jnp.bfloat16)
    return _dot(ha, wda) + _dot(hb, wdb)


MOE_PART = MOE_TM // 8


def _moe_sorted_kernel(blk_ref, e1_ref, e2_ref, parts_ref, pay_ref,
                       wga_ref, wua_ref, wda_ref, wgb_ref, wub_ref, wdb_ref, o_ref):
    parts = parts_ref[pl.program_id(0)]

    def run(rows):
        x, route = _payload_parts(pay_ref, rows)
        y = _expert_pair(x, route[:, 0:1], route[:, 1:2], wga_ref[...], wua_ref[...], wda_ref[...],
                         wgb_ref[...], wub_ref[...], wdb_ref[...])
        o_ref[rows, :] = _pack_halves(y)

    for used in range(1, MOE_TM // MOE_PART + 1):
        @pl.when(parts == used)
        def _(used=used):
            run(slice(0, used * MOE_PART))


def _moe_sorted_call(blk, e1, e2, parts, pay_sorted, wg, wu, wd):
    n_tiles = blk.shape[0]
    rows = lambda i, blk, e1, e2, parts: (blk[i], 0)
    wa = lambda i, blk, e1, e2, parts: (e1[i], 0, 0)
    wb = lambda i, blk, e1, e2, parts: (e2[i], 0, 0)
    up = lambda m: pl.BlockSpec((None, D_MODEL, D_EXPERT), m)
    down = lambda m: pl.BlockSpec((None, D_EXPERT, D_MODEL), m)
    return pl.pallas_call(
        _moe_sorted_kernel,
        grid_spec=pltpu.PrefetchScalarGridSpec(
            num_scalar_prefetch=4,
            grid=(n_tiles,),
            in_specs=[pl.BlockSpec((MOE_TM, PAYLOAD_W), rows),
                      up(wa), up(wa), down(wa), up(wb), up(wb), down(wb)],
            out_specs=pl.BlockSpec((MOE_TM, HALF_D), rows),
        ),
        out_shape=jax.ShapeDtypeStruct((n_tiles * MOE_TM, HALF_D), jnp.uint32),
        compiler_params=_cparams(("arbitrary",)),
        name="moe_sorted",
    )(blk, e1, e2, parts, pay_sorted, wg, wu, wd, wg, wu, wd)


DENSE_EXPERTS_PER_STEP = 4


def _moe_dense_kernel(*refs, first_step, resume):
    if resume:
        pay_ref, wg_ref, wu_ref, wd_ref, acc_ref, o_ref = refs
    else:
        pay_ref, wg_ref, wu_ref, wd_ref, o_ref = refs
    step = pl.program_id(0)

    @pl.when(step == 0)
    def _():
        o_ref[...] = acc_ref[...] if resume else jnp.zeros_like(o_ref)

    x, route = _payload_parts(pay_ref)
    y = None
    for k in range(DENSE_EXPERTS_PER_STEP):
        ef = ((first_step + step) * DENSE_EXPERTS_PER_STEP + k).astype(jnp.float32)
        gate = (jnp.where(route[:, 2:3] == ef, route[:, 0:1], 0.0)
                + jnp.where(route[:, 3:4] == ef, route[:, 1:2], 0.0))
        he = (_silu(_dot(x, wg_ref[k])) * _dot(x, wu_ref[k]) * gate).astype(jnp.bfloat16)
        yk = _dot(he, wd_ref[k])
        y = yk if y is None else y + yk
    o_ref[...] += y


def _moe_dense_call(pay, wg, wu, wd, first_step, n_steps, acc=None):
    n = pay.shape[0]
    per = DENSE_EXPERTS_PER_STEP
    whole = pl.BlockSpec((n, D_MODEL), lambda e: (0, 0))
    return pl.pallas_call(
        functools.partial(_moe_dense_kernel, first_step=first_step, resume=acc is not None),
        grid=(n_steps,),
        in_specs=[pl.BlockSpec((n, PAYLOAD_W), lambda e: (0, 0)),
                  pl.BlockSpec((per, D_MODEL, D_EXPERT), lambda e: (first_step + e, 0, 0)),
                  pl.BlockSpec((per, D_MODEL, D_EXPERT), lambda e: (first_step + e, 0, 0)),
                  pl.BlockSpec((per, D_EXPERT, D_MODEL), lambda e: (first_step + e, 0, 0))]
                 + ([whole] if acc is not None else []),
        out_specs=whole,
        out_shape=jax.ShapeDtypeStruct((n, D_MODEL), jnp.float32),
        compiler_params=_cparams(("arbitrary",)),
        name="moe_dense",
    )(pay, wg, wu, wd, *([acc] if acc is not None else []))


def _moe_out(y_ref):
    if y_ref.dtype == jnp.uint32:
        return jnp.concatenate(_unpack_halves(y_ref[...]), axis=-1)
    return y_ref[...]


def _residual_kernel(h_ref, y_ref, g_ref, o_ref):
    o_ref[...] = h_ref[...].astype(jnp.float32) + g_ref[...] * _moe_out(y_ref)


def _residual_call(h, y, g, tm):
    n = h.shape[0]
    row = pl.BlockSpec((tm, D_MODEL), lambda i: (i, 0))
    return pl.pallas_call(
        _residual_kernel,
        grid=(n // tm,),
        in_specs=[row, pl.BlockSpec((tm, y.shape[1]), lambda i: (i, 0)),
                  pl.BlockSpec((1, D_MODEL), lambda i: (0, 0))],
        out_specs=row,
        out_shape=jax.ShapeDtypeStruct((n, D_MODEL), jnp.float32),
        compiler_params=_cparams(("arbitrary",)),
        name="residual",
    )(h, y, g)


SC_ROWS = 128


SC_CORES = 2
SC_SUBCORES = 16
SC_WORKERS = SC_CORES * SC_SUBCORES


def _sc_mesh():
    return plsc.VectorSubcoreMesh(core_axis_name="core", subcore_axis_name="subcore")


def _sc_worker():
    return lax.axis_index("subcore") * SC_CORES + lax.axis_index("core")


def _scatter_rows(x, dest, n_out):
    n, w = x.shape

    per_worker = n // SC_WORKERS
    assert per_worker % SC_ROWS == 0

    @functools.partial(pl.kernel, out_type=jax.ShapeDtypeStruct((n_out, w), x.dtype), mesh=_sc_mesh(),
                       scratch_types=[pltpu.VMEM((SC_ROWS,), jnp.int32), pltpu.VMEM((SC_ROWS, w), x.dtype)])
    def scatter(x_hbm, i_hbm, o_hbm, idx_v, rows_v):
        first = _sc_worker() * per_worker

        @pl.loop(0, per_worker // SC_ROWS)
        def _(i):
            base = pl.multiple_of(first + i * SC_ROWS, SC_ROWS)
            pltpu.sync_copy(i_hbm.at[pl.ds(base, SC_ROWS)], idx_v)
            pltpu.sync_copy(x_hbm.at[pl.ds(base, SC_ROWS)], rows_v)
            pltpu.sync_copy(rows_v, o_hbm.at[idx_v])

    return scatter(x, dest)


def _gather_rows(x, src):
    n = src.shape[0]
    w = x.shape[1]

    per_worker = n // SC_WORKERS
    assert per_worker % SC_ROWS == 0

    @functools.partial(pl.kernel, out_type=jax.ShapeDtypeStruct((n, w), x.dtype), mesh=_sc_mesh(),
                       scratch_types=[pltpu.VMEM((SC_ROWS,), jnp.int32), pltpu.VMEM((SC_ROWS, w), x.dtype)])
    def gather(x_hbm, i_hbm, o_hbm, idx_v, rows_v):
        first = _sc_worker() * per_worker

        @pl.loop(0, per_worker // SC_ROWS)
        def _(i):
            base = pl.multiple_of(first + i * SC_ROWS, SC_ROWS)
            pltpu.sync_copy(i_hbm.at[pl.ds(base, SC_ROWS)], idx_v)
            pltpu.sync_copy(x_hbm.at[idx_v], rows_v)
            pltpu.sync_copy(rows_v, o_hbm.at[pl.ds(base, SC_ROWS)])

    return gather(x, src)


def _routing_plan(cls, n_tiles):
    onehot = (cls[:, None] == jnp.arange(N_CLASSES, dtype=jnp.int32)[None, :]).astype(jnp.int32)
    counts = jnp.sum(onehot, axis=0)
    rank = jnp.sum((jnp.cumsum(onehot, axis=0) - onehot) * onehot, axis=1)
    tiles = (counts + MOE_TM - 1) // MOE_TM
    tile_end = jnp.cumsum(tiles)
    tile_start = tile_end - tiles
    first_len = counts - (tiles - 1) * MOE_TM
    first_of = jnp.sum(onehot * first_len[None, :], axis=1)
    skip = jnp.where(rank >= first_of, MOE_TM - first_of, 0)
    dest = jnp.sum(onehot * tile_start[None, :], axis=1) * MOE_TM + rank + skip
    nact = tile_end[-1]
    tile = jnp.arange(n_tiles, dtype=jnp.int32)
    blk = jnp.minimum(tile, nact - 1)
    tile_cls = jnp.sum((blk[:, None] >= tile_end[None, :]).astype(jnp.int32), axis=1)
    e1 = jnp.asarray(CLASS_E1)[tile_cls]
    e2 = jnp.asarray(CLASS_E2)[tile_cls]
    valid = jnp.where(blk == tile_start[tile_cls], first_len[tile_cls], MOE_TM)
    parts = jnp.where(tile >= nact, 0, (valid + MOE_PART - 1) // MOE_PART)
    return dest.astype(jnp.int32), blk, e1, e2, parts.astype(jnp.int32)


def _row_tile(n, prefer=512):
    return next(t for t in (prefer, 512, 256) if n % t == 0)


def kernel(x, c, ctx, c_ctx, w_ada, b_ada, norm1, w_in, pool_w, pool_scale, q_norm, k_norm, rpb,
           sg_w, sg_b, sg_norm, w_out, norm2, w_router, b_router, w_gate, w_up, w_down):
    depth = w_ada.shape[0]
    n = x.shape[1]
    lc = ctx.shape[1]
    assert x.shape[0] == 1 and x.shape[2] == D_MODEL, "one sample of (tokens, 1024) features"
    assert n % (GRID_W * NA_ROWS_PER_BLOCK) == 0 and n % (SC_WORKERS * SC_ROWS) == 0 and lc % 256 == 0
    bf = jnp.bfloat16
    lat_stream = (x[0],)
    h_ctx = ctx[0]

    cond = jnp.stack([c[0], c_ctx], axis=1)
    mod = _ada_call(cond, w_ada, b_ada)

    wr_t = w_router.T
    wr_hi = wr_t.astype(bf)
    wr_lo = (wr_t - wr_hi.astype(jnp.float32)).astype(bf)
    wr = jnp.concatenate([wr_hi, wr_lo], axis=0)
    br = b_router.reshape(N_EXPERTS, 1)
    n_tiles = n // MOE_TM + N_CLASSES
    w_in_bf = w_in.astype(bf)
    w_out_bf = w_out.astype(bf)

    for l in range(depth):
        last = l == depth - 1
        qg =(q_norm[l] * (HEAD_DIM ** -0.5 * LOG2_E)).reshape(1, NA_DIM)
        kg = k_norm[l].reshape(1, NA_DIM)
        sgn = sg_norm[l].reshape(1, SG_DIM)
        sgw = sg_w[l].astype(bf).reshape(SG_DIM // LANES, 2 * SG_CHUNK, SG_CHUNK)
        sgb = jnp.broadcast_to(sg_b[l].reshape(SG_DIM // LANES, 2 * SG_CHUNK, 1),
                               (SG_DIM // LANES, 2 * SG_CHUNK, LANES))
        pool_bd = jax.scipy.linalg.block_diag(*[pool_w[l, g] for g in range(len(POOL_WINDOWS))]).astype(bf)
        pscale = pool_scale[l].reshape(1, POOL_DIM)
        bias = _natten_bias(rpb[l])
        n1 = norm1[l].reshape(1, D_MODEL)
        n2 = norm2[l].reshape(1, D_MODEL)

        def mods(row):
            return [mod[l, row:row + 1, i * D_MODEL:(i + 1) * D_MODEL] for i in range(6)]

        sh1, sc1, g1, sh2, sc2, g2 = mods(0)
        csh1, csc1, cg1, csh2, csc2, cg2 = mods(1)

        tc = _row_tile(lc)
        mix_pool_c, q_c, k_c, v_c, sg_c = _inproj_call((h_ctx,), n1, csh1, csc1, w_in_bf, l, qg, kg,
                                                       sgn, sgw, sgb, pool_bd, pscale, tc, strips=False)

        outs = _inproj_call(lat_stream, n1, sh1, sc1, w_in_bf, l, qg, kg, sgn, sgw, sgb,
                            pool_bd, pscale, _row_tile(n, INPROJ_TILE), strips=True)
        mix_pool, q, k, v, sg = outs[:5]
        h_lat = outs[5] if len(lat_stream) == 3 else lat_stream[0]
        att, wg_l, wu_l, wd_l = _natten_call(q, k, v, k_c, v_c, bias, w_gate, w_up, w_down, l)
        h1, pay, route = _outproj_call(h_lat, mix_pool, att, sg, w_out_bf, l, g1, n2, sh2, sc2, wr, br,
                                       _row_tile(n, OUTPROJ_TILE))
        cls = route[4].astype(jnp.int32)
        dest, blk, e1, e2, parts = _routing_plan(cls, n_tiles)
        pay_sorted = _scatter_rows(pay, dest, n_tiles * MOE_TM)

        if not last:
            dense_steps = N_EXPERTS // DENSE_EXPERTS_PER_STEP
            att_c = _ctxatt_call(q_c, k_c, v_c)
            h1_c, pay_c, _ = _outproj_call(h_ctx, mix_pool_c, att_c, sg_c, w_out_bf, l, cg1, n2, csh2, csc2,
                                           wr, br, tc)
            y_c = _moe_dense_call(pay_c, wg_l, wu_l, wd_l, 0, dense_steps // 2)
            parts, y_c = lax.optimization_barrier((parts, y_c))

        y_sorted = _moe_sorted_call(blk, e1, e2, parts, pay_sorted, wg_l, wu_l, wd_l)
        y = _gather_rows(y_sorted, dest)
        lat_stream = (h1, y, g2)

        if not last:
            y_c = _moe_dense_call(pay_c, wg_l, wu_l, wd_l, dense_steps // 2, dense_steps - dense_steps // 2,
                                  acc=y_c)
            h_ctx = _residual_call(h1_c, y_c, cg2, tc)

    return _residual_call(*lat_stream, _row_tile(n, RESIDUAL_TILE))[None]
```

```python
import functools

import jax
import jax.numpy as jnp
import numpy as np
from jax import lax
from jax.experimental import pallas as pl
from jax.experimental.pallas import tpu as pltpu
from jax.experimental.pallas import tpu_sc as plsc

D_MODEL = 1024
GRID_W = 64
HEAD_DIM = 64
POOL_WINDOWS = (2, 4, 8, 16)
POOL_DIM = 256
NA_HEADS = 8
NA_DIM = 512
NA_WIN_ROWS = 8
NA_WIN_COLS = 16
SG_DIM = 256
SG_CHUNK = 128
Q_OFF = POOL_DIM
K_OFF = Q_OFF + NA_DIM
V_OFF = K_OFF + NA_DIM
U_OFF = V_OFF + NA_DIM
G_OFF = U_OFF + SG_DIM
IN_DIM = G_OFF + SG_DIM
N_EXPERTS = 16
GROUP_SIZE = 4
D_EXPERT = 512
EPS = 1e-6

LANES = 128
SUBLANES = 8
HEAD_PAIRS = NA_DIM // LANES
VMEM_LIMIT = 48 * 1024 * 1024

PAIRS = ((0, 1), (0, 2), (1, 2), (1, 3), (0, 3), (2, 3))
N_GROUPS = N_EXPERTS // GROUP_SIZE
N_CLASSES = N_GROUPS * len(PAIRS)
CLASS_E1 = np.array([GROUP_SIZE * g + i for g in range(N_GROUPS) for (i, j) in PAIRS], np.int32)
CLASS_E2 = np.array([GROUP_SIZE * g + j for g in range(N_GROUPS) for (i, j) in PAIRS], np.int32)

ROUTE_ROWS = 8
HALF_D = D_MODEL // 2
PAYLOAD_W = HALF_D + LANES
MOE_TM = 1024
NEG_BIG = -1e30
LOG2_E = 1.4426950408889634


def _cparams(sem):
    return pltpu.CompilerParams(dimension_semantics=sem, vmem_limit_bytes=VMEM_LIMIT)


def _dot(a, b):
    return jnp.dot(a, b, preferred_element_type=jnp.float32)


def _dot_nt(a, b):
    return lax.dot_general(a, b, (((1,), (1,)), ((), ())), preferred_element_type=jnp.float32)


def _gelu_tanh(x):
    return 0.5 * x * (1.0 + jnp.tanh(0.7978845608028654 * (x + 0.044715 * (x * x * x))))


def _silu(x):
    return x * (1.0 / (1.0 + jnp.exp(-x)))


def _pack_halves(x):
    w = x.shape[1] // 2
    lo = pltpu.bitcast(x[:, :w].astype(jnp.bfloat16).astype(jnp.float32), jnp.uint32) >> 16
    hi = pltpu.bitcast(x[:, w:].astype(jnp.bfloat16).astype(jnp.float32), jnp.uint32) & jnp.uint32(0xFFFF0000)
    return lo | hi


def _unpack_halves(words):
    lo = pltpu.bitcast(words << 16, jnp.float32)
    hi = pltpu.bitcast(words & jnp.uint32(0xFFFF0000), jnp.float32)
    return lo, hi


def _ada_kernel(cond_ref, w_ref, b_ref, o_ref):
    cond = _silu(cond_ref[...])
    w = w_ref[...]
    rows = [jnp.sum(w * cond[:, r:r + 1], axis=0, keepdims=True) + b_ref[...] for r in range(2)]
    o_ref[...] = jnp.concatenate(rows + [jnp.zeros((SUBLANES - 2, w.shape[1]), jnp.float32)], axis=0)


def _ada_call(cond, w_ada, b_ada):
    depth = w_ada.shape[0]
    tn = 1536
    return pl.pallas_call(
        _ada_kernel,
        grid=(depth, 6 * D_MODEL // tn),
        in_specs=[
            pl.BlockSpec((D_MODEL, 2), lambda l, j: (0, 0)),
            pl.BlockSpec((None, D_MODEL, tn), lambda l, j: (l, 0, j)),
            pl.BlockSpec((None, 1, tn), lambda l, j: (l, 0, j)),
        ],
        out_specs=pl.BlockSpec((None, SUBLANES, tn), lambda l, j: (l, 0, j)),
        out_shape=jax.ShapeDtypeStruct((depth, SUBLANES, 6 * D_MODEL), jnp.float32),
        compiler_params=_cparams(("arbitrary", "arbitrary")),
        name="adaln",
    )(cond, w_ada, b_ada.reshape(depth, 1, 6 * D_MODEL))


def _norm_modulate(x, n_ref, sh_ref, sc_ref):
    ms = jnp.mean(x * x, axis=-1, keepdims=True)
    return ((x * lax.rsqrt(ms + EPS)) * (n_ref[...] * (1.0 + sc_ref[...])) + sh_ref[...]).astype(jnp.bfloat16)


def _head_rms_scale(a):
    low = lax.broadcasted_iota(jnp.int32, (a.shape[0], LANES), 1) < HEAD_DIM
    blocks = []
    for p in range(a.shape[1] // LANES):
        sq = jnp.square(a[:, p * LANES:(p + 1) * LANES])
        s_lo = jnp.sum(jnp.where(low, sq, 0.0), axis=-1, keepdims=True)
        s_hi = jnp.sum(jnp.where(low, 0.0, sq), axis=-1, keepdims=True)
        blocks.append(lax.rsqrt(jnp.where(low, s_lo, s_hi) * (1.0 / HEAD_DIM) + EPS))
    return jnp.concatenate(blocks, axis=-1)


STRIP_W = 8
N_STRIPS = GRID_W // STRIP_W


def _store_keys(ref, x):
    if len(ref.shape) == 2:
        ref[...] = x.astype(jnp.bfloat16)
        return
    pair = 2 * STRIP_W
    for s in range(N_STRIPS):
        for rp in range(x.shape[0] // (2 * GRID_W)):
            top = 2 * rp * GRID_W + s * STRIP_W
            rows = jnp.concatenate([x[top:top + STRIP_W], x[top + GRID_W:top + GRID_W + STRIP_W]], axis=0)
            ref[s, rp * pair:(rp + 1) * pair, :] = rows.astype(jnp.bfloat16)


POOL_HALO = 8
HALO_BLOCK = 16
STREAM_DTYPE = jnp.bfloat16


POOL_EDGE = 16


def _pool_mix(xe_ref, w_ref, scale_ref, tm, seq_len):
    low = lax.broadcasted_iota(jnp.int32, (tm, LANES), 1) < HEAD_DIM
    t_edge = pl.program_id(0) * tm + lax.broadcasted_iota(jnp.int32, (POOL_EDGE, LANES), 0)

    def window_mean(s, half):
        mean = s * (1.0 / (2 * half))

        def clip_fix(t):
            count = (jnp.minimum(t + half, seq_len) - jnp.maximum(t - half, 0)).astype(jnp.float32)
            return (2.0 * half) / count

        return jnp.concatenate([mean[:POOL_EDGE] * clip_fix(t_edge),
                                mean[POOL_EDGE:tm - POOL_EDGE],
                                mean[tm - POOL_EDGE:] * clip_fix(t_edge + (tm - POOL_EDGE))], axis=0)

    def window_sums(xs, n_levels):
        sums = []
        s = xs
        for k in range(n_levels):
            step = 1 << k
            s = s[:-step] + s[step:]
            sums.append(s)
        return sums

    outs = []
    for half_block, windows in enumerate(((2, 4), (8, 16))):
        xs = xe_ref[:, half_block * LANES:(half_block + 1) * LANES]
        sums = window_sums(xs, int(np.log2(windows[1])))
        parts = []
        for w in windows:
            half = w // 2
            s = sums[int(np.log2(w)) - 1][POOL_HALO - half:POOL_HALO - half + tm]
            parts.append(window_mean(s, half))
        mean = jnp.where(low, parts[0], parts[1])
        outs.append(mean - xs[POOL_HALO:POOL_HALO + tm])
    d = jnp.concatenate(outs, axis=-1).astype(jnp.bfloat16)
    return (_dot(d, w_ref[...]) * scale_ref[...]).astype(jnp.bfloat16)


def _inproj_kernel(*refs, pending, seq_len):
    n_stream = 7 if pending else 3
    stream, refs = refs[:n_stream], refs[n_stream:]
    if pending:
        h_ref, y_ref, g_ref, hp_ref, hn_ref, yp_ref, yn_ref = stream
        hres_ref, refs = refs[-2], refs[:-2] + refs[-1:]
        x = h_ref[...].astype(jnp.float32) + g_ref[...] * _moe_out(y_ref)
        hres_ref[...] = x.astype(hres_ref.dtype)
        before = hp_ref[...].astype(jnp.float32) + g_ref[...] * _moe_out(yp_ref)
        after = hn_ref[...].astype(jnp.float32) + g_ref[...] * _moe_out(yn_ref)
    else:
        h_ref, hp_ref, hn_ref = stream
        x = h_ref[...].astype(jnp.float32)
        before = hp_ref[...].astype(jnp.float32)
        after = hn_ref[...].astype(jnp.float32)
    x_halo = jnp.concatenate([before[HALO_BLOCK - POOL_HALO:], after[:POOL_HALO]], axis=0)
    (n1_ref, sh_ref, sc_ref, w_ref, qg_ref, kg_ref, sgn_ref, sgw_ref, sgb_ref,
     pw_ref, ps_ref, pool_ref, q_ref, k_ref, v_ref, sg_ref, xe_ref) = refs
    tm = h_ref.shape[0]
    i = pl.program_id(0)
    hn = _norm_modulate(x, n1_ref, sh_ref, sc_ref)
    hn_halo = _norm_modulate(x_halo, n1_ref, sh_ref, sc_ref)

    a_halo = _dot(hn_halo, w_ref[:, 0:Q_OFF])
    xe_ref[0:POOL_HALO, :] = jnp.where(i > 0, a_halo[:POOL_HALO], 0.0)
    xe_ref[POOL_HALO:POOL_HALO + tm, :] = _dot(hn, w_ref[:, 0:Q_OFF])
    xe_ref[POOL_HALO + tm:, :] = jnp.where(i < pl.num_programs(0) - 1, a_halo[POOL_HALO:], 0.0)

    a_g = _dot(hn, w_ref[:, G_OFF:IN_DIM])
    a_u = _dot(hn, w_ref[:, U_OFF:G_OFF])
    a_q = _dot(hn, w_ref[:, Q_OFF:K_OFF])
    a_k = _dot(hn, w_ref[:, K_OFF:V_OFF])
    _store_keys(v_ref, _dot(hn, w_ref[:, V_OFF:U_OFF]))

    gv = _gelu_tanh(a_g)
    q_ref[...] = (a_q * _head_rms_scale(a_q) * qg_ref[...]).astype(jnp.bfloat16)
    _store_keys(k_ref, a_k * _head_rms_scale(a_k) * kg_ref[...])

    u = _gelu_tanh(a_u)
    vn = (gv * _head_rms_scale(gv) * sgn_ref[...]).astype(jnp.bfloat16)
    low = lax.broadcasted_iota(jnp.int32, (SG_CHUNK, LANES), 1) < HEAD_DIM
    for c in range(tm // SG_CHUNK):
        rows = slice(c * SG_CHUNK, (c + 1) * SG_CHUNK)
        for s in range(SG_DIM // LANES):
            cols = slice(s * LANES, (s + 1) * LANES)
            m = _dot(sgw_ref[s], vn[rows, cols]) + sgb_ref[s]
            mixed = jnp.where(low, m[:SG_CHUNK], m[SG_CHUNK:])
            sg_ref[rows, cols] = (u[rows, cols] * mixed).astype(jnp.bfloat16)

    pool_ref[...] = _pool_mix(xe_ref, pw_ref, ps_ref, tm, seq_len)


def _inproj_call(stream, n1, sh, sc, w_in, layer, qg, kg, sgn, sgw, sgb, pool_w, pool_scale, tm, strips):
    pending = len(stream) == 3
    n = stream[0].shape[0]
    assert n % tm == 0 and tm % SG_CHUNK == 0 and tm % (2 * GRID_W) == 0
    if strips:
        kv_spec = pl.BlockSpec((N_STRIPS, tm // N_STRIPS, NA_DIM), lambda i: (0, i, 0))
        kv_shape = jax.ShapeDtypeStruct((N_STRIPS, n // N_STRIPS, NA_DIM), jnp.bfloat16)
    else:
        kv_spec = pl.BlockSpec((tm, NA_DIM), lambda i: (i, 0))
        kv_shape = jax.ShapeDtypeStruct((n, NA_DIM), jnp.bfloat16)
    row = lambda i: (i, 0)
    fixed2 = lambda i: (0, 0)
    fixed3 = lambda i: (0, 0, 0)
    vec = lambda w: pl.BlockSpec((1, w), fixed2)
    rows = pl.BlockSpec((tm, D_MODEL), row)
    per_tile = tm // HALO_BLOCK
    before = lambda i: (jnp.maximum(i * per_tile - 1, 0), 0)
    after = lambda i: (jnp.minimum((i + 1) * per_tile, n // HALO_BLOCK - 1), 0)
    halo = lambda w, m: pl.BlockSpec((HALO_BLOCK, w), m)
    h = stream[0]
    if pending:
        y, g = stream[1], stream[2]
        yw = y.shape[1]
        args = [h, y, g, h, h, y, y]
        stream_specs = [rows, pl.BlockSpec((tm, yw), row), vec(D_MODEL),
                        halo(D_MODEL, before), halo(D_MODEL, after), halo(yw, before), halo(yw, after)]
    else:
        args = [h, h, h]
        stream_specs = [rows, halo(D_MODEL, before), halo(D_MODEL, after)]
    extra_out_specs = [rows] if pending else []
    extra_out_shape = [jax.ShapeDtypeStruct((n, D_MODEL), STREAM_DTYPE)] if pending else []
    return pl.pallas_call(
        functools.partial(_inproj_kernel, pending=pending, seq_len=n),
        grid=(n // tm,),
        in_specs=stream_specs + [
            vec(D_MODEL), vec(D_MODEL), vec(D_MODEL),
            pl.BlockSpec((None, D_MODEL, IN_DIM), lambda i: (layer, 0, 0)),
            vec(NA_DIM), vec(NA_DIM),
            vec(SG_DIM),
            pl.BlockSpec((SG_DIM // LANES, 2 * SG_CHUNK, SG_CHUNK), fixed3),
            pl.BlockSpec((SG_DIM // LANES, 2 * SG_CHUNK, LANES), fixed3),
            pl.BlockSpec((POOL_DIM, POOL_DIM), fixed2),
            vec(POOL_DIM),
        ],
        out_specs=[
            pl.BlockSpec((tm, POOL_DIM), row),
            pl.BlockSpec((tm, NA_DIM), row),
            kv_spec,
            kv_spec,
            pl.BlockSpec((tm, SG_DIM), row),
        ] + extra_out_specs,
        out_shape=[
            jax.ShapeDtypeStruct((n, POOL_DIM), jnp.bfloat16),
            jax.ShapeDtypeStruct((n, NA_DIM), jnp.bfloat16),
            kv_shape,
            kv_shape,
            jax.ShapeDtypeStruct((n, SG_DIM), jnp.bfloat16),
        ] + extra_out_shape,
        scratch_shapes=[pltpu.VMEM((tm + 2 * POOL_HALO, POOL_DIM), jnp.float32)],
        compiler_params=_cparams(("arbitrary",)),
        name="inproj",
    )(*args, n1, sh, sc, w_in, qg, kg, sgn, sgw, sgb, pool_w, pool_scale)


NA_ROWS_PER_BLOCK = 64
NA_GROUP_ROWS = 4
NA_WINDOW_ROWS = NA_GROUP_ROWS + NA_WIN_ROWS
NA_BLOCK = NA_ROWS_PER_BLOCK * GRID_W
NA_GROUP = NA_GROUP_ROWS * GRID_W
NA_HALF_COLS = GRID_W // 2
NA_CHAIN = NA_GROUP_ROWS * NA_HALF_COLS
NA_HALF_STRIPS = 5
NA_HALF_COL0 = (0, GRID_W - NA_HALF_STRIPS * STRIP_W)
NA_RUN = NA_WINDOW_ROWS * STRIP_W
NA_LOCAL = 512
NA_STRIP_BLOCK = NA_ROWS_PER_BLOCK * STRIP_W
NA_STRIP_HALO = (NA_WIN_ROWS // 2) * STRIP_W
NA_EDGE_FIRST, NA_EDGE_NONE, NA_EDGE_LAST = 0, 1, 2


def _stack_heads(x, low):
    zero = jnp.zeros_like(x)
    return jnp.concatenate([jnp.where(low, x, zero), jnp.where(low, zero, x)], axis=0)


def _natten_kernel(q_ref, kp_ref, kc_ref, kn_ref, vp_ref, vc_ref, vn_ref, kx_ref, vx_ref, bias_ref,
                   wg_ref, wu_ref, wd_ref,
                   o_ref, wg_bf_ref, wu_bf_ref, wd_bf_ref, kwin_ref, vwin_ref, vxe_ref, *, grid_rows):
    b = pl.program_id(1)
    wg_bf_ref[...] = wg_ref[...].astype(jnp.bfloat16)
    wu_bf_ref[...] = wu_ref[...].astype(jnp.bfloat16)
    wd_bf_ref[...] = wd_ref[...].astype(jnp.bfloat16)
    top, bottom = NA_STRIP_HALO, NA_STRIP_HALO + NA_STRIP_BLOCK
    kwin_ref[:, 0:top, :] = kp_ref[...]
    kwin_ref[:, top:bottom, :] = kc_ref[...]
    kwin_ref[:, bottom:, :] = kn_ref[...]
    vwin_ref[:, 0:top, 0:LANES] = vp_ref[...]
    vwin_ref[:, top:bottom, 0:LANES] = vc_ref[...]
    vwin_ref[:, bottom:, 0:LANES] = vn_ref[...]
    vwin_ref[:, :, LANES:] = jnp.ones(vwin_ref.shape[:2] + (LANES,), jnp.bfloat16)
    vxe_ref[:, 0:LANES] = vx_ref[...]
    vxe_ref[:, LANES:] = jnp.ones((vxe_ref.shape[0], LANES), jnp.bfloat16)
    low_q = lax.broadcasted_iota(jnp.int32, (NA_CHAIN, LANES), 1) < HEAD_DIM
    n_pad = NA_LOCAL - NA_HALF_STRIPS * NA_RUN

    def window_start(g):
        r0 = b * NA_ROWS_PER_BLOCK + g * NA_GROUP_ROWS
        ws = jnp.clip(r0 - NA_WIN_ROWS // 2, 0, grid_rows - NA_WINDOW_ROWS)
        edge = jnp.where(r0 == 0, NA_EDGE_FIRST,
                         jnp.where(r0 == grid_rows - NA_GROUP_ROWS, NA_EDGE_LAST, NA_EDGE_NONE))
        start = pl.multiple_of((ws - b * NA_ROWS_PER_BLOCK + NA_WIN_ROWS // 2) * STRIP_W, NA_STRIP_HALO)
        return start, edge

    def local_window(win_ref, g, half):
        start, _ = window_start(g)
        s0 = NA_HALF_COL0[half] // STRIP_W
        runs = [win_ref[s, pl.ds(start, NA_RUN), :] for s in range(s0, s0 + NA_HALF_STRIPS)]
        return jnp.concatenate(runs + [jnp.zeros((n_pad, win_ref.shape[2]), jnp.bfloat16)], axis=0)

    def query_rows(g, half, j):
        first = g * NA_GROUP + j * GRID_W + half * NA_HALF_COLS
        return slice(first, first + NA_HALF_COLS)

    def scores(c):
        g, half = divmod(c, 2)
        _, edge = window_start(g)
        qh = jnp.concatenate([q_ref[query_rows(g, half, j), :] for j in range(NA_GROUP_ROWS)], axis=0)
        lhs = _stack_heads(qh, low_q)
        kl = local_window(kwin_ref, g, half)
        return jnp.concatenate([_dot_nt(lhs, kl).astype(jnp.bfloat16) + bias_ref[edge, half],
                                _dot_nt(lhs, kx_ref[...]).astype(jnp.bfloat16)], axis=-1)

    n_chains = 2 * (NA_ROWS_PER_BLOCK // NA_GROUP_ROWS)
    s_next = scores(0)
    for c in range(n_chains):
        s = s_next
        if c + 1 < n_chains:
            s_next = scores(c + 1)
        g, half = divmod(c, 2)
        vl = local_window(vwin_ref, g, half)
        m = jnp.max(s, axis=-1, keepdims=True)
        pb = jnp.exp2(s - m)
        o = _dot(pb[:, :NA_LOCAL], vl) + _dot(pb[:, NA_LOCAL:], vxe_ref[...])
        o = o[:, :LANES] * (1.0 / o[:, LANES:])
        o = jnp.where(low_q, o[:NA_CHAIN], o[NA_CHAIN:]).astype(jnp.bfloat16)
        for j in range(NA_GROUP_ROWS):
            o_ref[query_rows(g, half, j), :] = o[j * NA_HALF_COLS:(j + 1) * NA_HALF_COLS]


def _natten_call(q, k, v, k_ctx, v_ctx, bias, w_gate, w_up, w_down, layer):
    n = q.shape[0]
    grid_rows = n // GRID_W
    assert grid_rows % NA_ROWS_PER_BLOCK == 0 and grid_rows >= 2 * NA_ROWS_PER_BLOCK
    nblk = n // NA_BLOCK
    steps = HEAD_PAIRS * nblk
    depth = w_gate.shape[0]
    up_rows = N_EXPERTS * D_MODEL
    down_rows = N_EXPERTS * D_EXPERT
    assert up_rows % steps == 0 and down_rows % steps == 0
    wg2 = w_gate.reshape(depth * up_rows, D_EXPERT)
    wu2 = w_up.reshape(depth * up_rows, D_EXPERT)
    wd2 = w_down.reshape(depth * down_rows, D_MODEL)
    up_in = pl.BlockSpec((up_rows // steps, D_EXPERT), lambda p, b: (layer * steps + p * nblk + b, 0))
    down_in = pl.BlockSpec((down_rows // steps, D_MODEL), lambda p, b: (layer * steps + p * nblk + b, 0))
    up_out = pl.BlockSpec((up_rows // steps, D_EXPERT), lambda p, b: (p * nblk + b, 0))
    down_out = pl.BlockSpec((down_rows // steps, D_MODEL), lambda p, b: (p * nblk + b, 0))
    n_halo = n // N_STRIPS // NA_STRIP_HALO
    hb = NA_STRIP_BLOCK // NA_STRIP_HALO
    rows = pl.BlockSpec((NA_BLOCK, LANES), lambda p, b: (b, p))
    cur = pl.BlockSpec((N_STRIPS, NA_STRIP_BLOCK, LANES), lambda p, b: (0, b, p))
    prev = pl.BlockSpec((N_STRIPS, NA_STRIP_HALO, LANES), lambda p, b: (0, jnp.maximum(b * hb - 1, 0), p))
    nxt = pl.BlockSpec((N_STRIPS, NA_STRIP_HALO, LANES),
                       lambda p, b: (0, jnp.minimum((b + 1) * hb, n_halo - 1), p))
    ctx = pl.BlockSpec((k_ctx.shape[0], LANES), lambda p, b: (0, p))
    win_rows = NA_STRIP_BLOCK + 2 * NA_STRIP_HALO
    att, wg_bf, wu_bf, wd_bf = pl.pallas_call(
        functools.partial(_natten_kernel, grid_rows=grid_rows),
        grid=(HEAD_PAIRS, nblk),
        in_specs=[rows, prev, cur, nxt, prev, cur, nxt, ctx, ctx,
                  pl.BlockSpec((None, 3, 2, 2 * NA_CHAIN, NA_LOCAL), lambda p, b: (p, 0, 0, 0, 0)),
                  up_in, up_in, down_in],
        out_specs=[rows, up_out, up_out, down_out],
        out_shape=[jax.ShapeDtypeStruct((n, NA_DIM), jnp.bfloat16),
                   jax.ShapeDtypeStruct((up_rows, D_EXPERT), jnp.bfloat16),
                   jax.ShapeDtypeStruct((up_rows, D_EXPERT), jnp.bfloat16),
                   jax.ShapeDtypeStruct((down_rows, D_MODEL), jnp.bfloat16)],
        scratch_shapes=[pltpu.VMEM((N_STRIPS, win_rows, LANES), jnp.bfloat16),
                        pltpu.VMEM((N_STRIPS, win_rows, 2 * LANES), jnp.bfloat16),
                        pltpu.VMEM((k_ctx.shape[0], 2 * LANES), jnp.bfloat16)],
        compiler_params=_cparams(("arbitrary", "arbitrary")),
        name="natten",
    )(q, k, k, k, v, v, v, k_ctx, v_ctx, bias, wg2, wu2, wd2)
    return (att, wg_bf.reshape(N_EXPERTS, D_MODEL, D_EXPERT), wu_bf.reshape(N_EXPERTS, D_MODEL, D_EXPERT),
            wd_bf.reshape(N_EXPERTS, D_EXPERT, D_MODEL))


def _natten_bias(rpb):
    cols = np.arange(GRID_W)
    col_start = np.clip(cols - NA_WIN_COLS // 2, 0, GRID_W - NA_WIN_COLS)
    kc = np.arange(GRID_W)
    in_win = (kc[None, :] >= col_start[:, None]) & (kc[None, :] < col_start[:, None] + NA_WIN_COLS)
    dc = kc[None, :] - cols[:, None] + NA_WIN_COLS - 1
    sel = (np.arange(2 * NA_WIN_COLS - 1)[:, None, None] == dc[None]) & in_win[None]
    t2 = jnp.einsum("hdj,jqk->hdqk", rpb, jnp.asarray(sel, jnp.float32), precision=lax.Precision.HIGHEST)
    t2 = jnp.where(in_win[None, None], t2 * LOG2_E, NEG_BIG)
    neg = jnp.full((NA_HEADS, 1, GRID_W, GRID_W), NEG_BIG, jnp.float32)
    t2e = jnp.concatenate([neg, t2, neg], axis=1)
    u = jnp.concatenate([t2e[:, :-1], t2e[:, 1:]], axis=-1)
    u = u.reshape(HEAD_PAIRS, 2, 2 * NA_WIN_ROWS, GRID_W, LANES)

    place = np.zeros((2, NA_WINDOW_ROWS * GRID_W, NA_LOCAL), np.float32)
    for half, c0 in enumerate(NA_HALF_COL0):
        for a in range(NA_WINDOW_ROWS):
            for kcol in range(c0, c0 + NA_HALF_STRIPS * STRIP_W):
                s, c8 = divmod(kcol - c0, STRIP_W)
                place[half, a * GRID_W + kcol, s * NA_RUN + a * STRIP_W + c8] = 1.0
    outside = np.full((3, NA_GROUP_ROWS, NA_LOCAL), NEG_BIG, np.float32)
    for edge in (NA_EDGE_FIRST, NA_EDGE_NONE, NA_EDGE_LAST):
        for j in range(NA_GROUP_ROWS):
            lo, _ = _window_rows(edge, j)
            for s in range(NA_HALF_STRIPS):
                outside[edge, j, s * NA_RUN + lo * STRIP_W:s * NA_RUN + (lo + NA_WIN_ROWS) * STRIP_W] = 0.0
    return pl.pallas_call(
        _bias_expand_kernel,
        grid=(HEAD_PAIRS,),
        in_specs=[pl.BlockSpec((None, 2, 2 * NA_WIN_ROWS, GRID_W, LANES), lambda p: (p, 0, 0, 0, 0)),
                  pl.BlockSpec(place.shape, lambda p: (0, 0, 0)),
                  pl.BlockSpec(outside.shape, lambda p: (0, 0, 0))],
        out_specs=pl.BlockSpec((None, 3, 2, 2 * NA_CHAIN, NA_LOCAL), lambda p: (p, 0, 0, 0, 0)),
        out_shape=jax.ShapeDtypeStruct((HEAD_PAIRS, 3, 2, 2 * NA_CHAIN, NA_LOCAL), jnp.bfloat16),
        compiler_params=_cparams(("arbitrary",)),
        name="bias_expand",
    )(u, jnp.asarray(place, jnp.bfloat16), jnp.asarray(outside))


def _window_rows(edge, j):
    if edge == NA_EDGE_FIRST:
        return 0, NA_WIN_ROWS - 1 - j
    if edge == NA_EDGE_NONE:
        return j, NA_WIN_ROWS // 2 - 1
    return NA_WINDOW_ROWS - NA_WIN_ROWS, NA_WIN_ROWS // 2 - 1 - j


def _bias_expand_kernel(u_ref, place_ref, outside_ref, o_ref):
    low = lax.broadcasted_iota(jnp.int32, (NA_HALF_COLS, LANES), 1) < GRID_W
    zero = jnp.zeros((NA_HALF_COLS, LANES), jnp.float32)
    for edge in (NA_EDGE_FIRST, NA_EDGE_NONE, NA_EDGE_LAST):
        for half in range(2):
            q0 = half * NA_HALF_COLS
            blocks, masks = [], []
            for hd in range(2):
                for j in range(NA_GROUP_ROWS):
                    lo, base = _window_rows(edge, j)
                    tiles = []
                    for i in range(NA_WINDOW_ROWS // 2):
                        a0, a1 = 2 * i, 2 * i + 1
                        ok0 = lo <= a0 < lo + NA_WIN_ROWS
                        ok1 = lo <= a1 < lo + NA_WIN_ROWS
                        if not (ok0 or ok1):
                            tile = zero
                        else:
                            tile = u_ref[hd, base + a1 - lo, q0:q0 + NA_HALF_COLS, :]
                            if not ok0:
                                tile = jnp.where(low, zero, tile)
                            if not ok1:
                                tile = jnp.where(low, tile, zero)
                        tiles.append(tile)
                    blocks.append(jnp.concatenate(tiles, axis=-1))
                    masks.append(jnp.broadcast_to(outside_ref[edge, j:j + 1, :], (NA_HALF_COLS, NA_LOCAL)))
            lhs = jnp.concatenate(blocks, axis=0).astype(jnp.bfloat16)
            placed = _dot(lhs, place_ref[half]) + jnp.concatenate(masks, axis=0)
            o_ref[edge, half] = placed.astype(o_ref.dtype)


def _ctxatt_kernel(q_ref, k_ref, v_ref, o_ref):
    lc = q_ref.shape[0]
    low = lax.broadcasted_iota(jnp.int32, (lc, LANES), 1) < HEAD_DIM
    lhs = _stack_heads(q_ref[...], low)
    s = _dot_nt(lhs, k_ref[...])
    m = jnp.max(s, axis=-1, keepdims=True)
    p = jnp.exp2(s - m)
    denom = jnp.sum(p, axis=-1, keepdims=True)
    o = _dot(p.astype(jnp.bfloat16), v_ref[...]) * (1.0 / denom)
    o_ref[...] = jnp.where(low, o[:lc], o[lc:]).astype(jnp.bfloat16)


def _ctxatt_call(q, k, v):
    lc = q.shape[0]
    spec = pl.BlockSpec((lc, LANES), lambda p: (0, p))
    return pl.pallas_call(
        _ctxatt_kernel,
        grid=(HEAD_PAIRS,),
        in_specs=[spec, spec, spec],
        out_specs=spec,
        out_shape=jax.ShapeDtypeStruct((lc, NA_DIM), jnp.bfloat16),
        compiler_params=_cparams(("arbitrary",)),
        name="ctxatt",
    )(q, k, v)


OUTPROJ_CHAIN = 256
OUTPROJ_TILE = 1024
INPROJ_TILE = 1024
RESIDUAL_TILE = 2048


def _outproj_kernel(h_ref, mp_ref, att_ref, sg_ref, wo_ref, g1_ref, n2_ref, sh_ref, sc_ref,
                    wr_ref, br_ref, h1_ref, pay_ref, route_ref):
    chains = [slice(c * OUTPROJ_CHAIN, (c + 1) * OUTPROJ_CHAIN) for c in range(h_ref.shape[0] // OUTPROJ_CHAIN)]
    h1s = []
    for rows in chains:
        mix = (_dot(mp_ref[rows, :], wo_ref[0:POOL_DIM, :])
               + _dot(att_ref[rows, :], wo_ref[POOL_DIM:POOL_DIM + NA_DIM, :])
               + _dot(sg_ref[rows, :], wo_ref[POOL_DIM + NA_DIM:, :]))
        h1 = h_ref[rows, :].astype(jnp.float32) + g1_ref[...] * mix
        h1_ref[rows, :] = h1.astype(h1_ref.dtype)
        h1s.append(h1)
    for rows, h1 in zip(chains, h1s):
        _outproj_route(rows, h1, n2_ref, sh_ref, sc_ref, wr_ref, br_ref, pay_ref, route_ref)


def _outproj_route(rows, h1, n2_ref, sh_ref, sc_ref, wr_ref, br_ref, pay_ref, route_ref):
    tm = OUTPROJ_CHAIN
    ms = jnp.mean(h1 * h1, axis=-1, keepdims=True)
    hm = (h1 * lax.rsqrt(ms + EPS)) * (n2_ref[...] * (1.0 + sc_ref[...])) + sh_ref[...]
    pay_ref[rows, 0:HALF_D] = _pack_halves(hm)

    hm_hi = hm.astype(jnp.bfloat16)
    lt = _dot_nt(wr_ref[...], hm_hi)
    logits = lt[:N_EXPERTS] + lt[N_EXPERTS:] + br_ref[...]
    e = jnp.exp(logits - jnp.max(logits, axis=0, keepdims=True))

    best = ga = gb = e1 = e2 = cls = None
    for c in range(N_CLASSES):
        a, b2 = int(CLASS_E1[c]), int(CLASS_E2[c])
        ea, eb = e[a:a + 1, :], e[b2:b2 + 1, :]
        s = ea + eb
        if best is None:
            best, ga, gb = s, ea, eb
            e1 = jnp.full_like(s, float(a))
            e2 = jnp.full_like(s, float(b2))
            cls = jnp.zeros_like(s)
        else:
            better = s > best
            best = jnp.where(better, s, best)
            ga = jnp.where(better, ea, ga)
            gb = jnp.where(better, eb, gb)
            e1 = jnp.where(better, float(a), e1)
            e2 = jnp.where(better, float(b2), e2)
            cls = jnp.where(better, float(c), cls)
    inv = 1.0 / best
    row = lax.broadcasted_iota(jnp.int32, (ROUTE_ROWS, tm), 0)
    rec = jnp.where(row == 0, ga * inv,
          jnp.where(row == 1, gb * inv,
          jnp.where(row == 2, e1,
          jnp.where(row == 3, e2,
          jnp.where(row == 4, cls, 0.0)))))
    route_ref[:, rows] = rec
    wide = jnp.concatenate([rec, jnp.zeros((LANES - ROUTE_ROWS, tm), jnp.float32)], axis=0)
    pay_ref[rows, HALF_D:] = pltpu.bitcast(wide.T, jnp.uint32)


def _outproj_call(h, mp, att, sg, w_out, layer, g1, n2, sh2, sc2, wr, br, tm):
    n = h.shape[0]
    assert n % tm == 0 and tm % OUTPROJ_CHAIN == 0
    row = lambda i: (i, 0)
    fixed = lambda i: (0, 0)
    vec = pl.BlockSpec((1, D_MODEL), fixed)
    return pl.pallas_call(
        _outproj_kernel,
        grid=(n // tm,),
        in_specs=[
            pl.BlockSpec((tm, D_MODEL), row),
            pl.BlockSpec((tm, POOL_DIM), row),
            pl.BlockSpec((tm, NA_DIM), row),
            pl.BlockSpec((tm, SG_DIM), row),
            pl.BlockSpec((None, D_MODEL, D_MODEL), lambda i: (layer, 0, 0)),
            vec, vec, vec, vec,
            pl.BlockSpec((2 * N_EXPERTS, D_MODEL), fixed),
            pl.BlockSpec((N_EXPERTS, 1), fixed),
        ],
        out_specs=[
            pl.BlockSpec((tm, D_MODEL), row),
            pl.BlockSpec((tm, PAYLOAD_W), row),
            pl.BlockSpec((ROUTE_ROWS, tm), lambda i: (0, i)),
        ],
        out_shape=[
            jax.ShapeDtypeStruct((n, D_MODEL), STREAM_DTYPE),
            jax.ShapeDtypeStruct((n, PAYLOAD_W), jnp.uint32),
            jax.ShapeDtypeStruct((ROUTE_ROWS, n), jnp.float32),
        ],
        compiler_params=_cparams(("arbitrary",)),
        name="outproj",
    )(h, mp, att, sg, w_out, g1, n2, sh2, sc2, wr, br)


def _payload_parts(pay_ref, rows=slice(None)):
    lo, hi = _unpack_halves(pay_ref[rows, 0:HALF_D])
    x = jnp.concatenate([lo, hi], axis=-1).astype(jnp.bfloat16)
    return x, pltpu.bitcast(pay_ref[rows, HALF_D:], jnp.float32)


def _expert_pair(x, ga, gb, wga, wua, wda, wgb, wub, wdb):
    ha = (_silu(_dot(x, wga)) * _dot(x, wua) * ga).astype(jnp.bfloat16)
    hb = (_silu(_dot(x, wgb)) * _dot(x, wub) * gb).astype(jnp.bfloat16)
    return _dot(ha, wda) + _dot(hb, wdb)


MOE_PART = MOE_TM // 8


def _moe_sorted_kernel(blk_ref, e1_ref, e2_ref, parts_ref, pay_ref,
                       wga_ref, wua_ref, wda_ref, wgb_ref, wub_ref, wdb_ref, o_ref):
    parts = parts_ref[pl.program_id(0)]

    def run(rows):
        x, route = _payload_parts(pay_ref, rows)
        y = _expert_pair(x, route[:, 0:1], route[:, 1:2], wga_ref[...], wua_ref[...], wda_ref[...],
                         wgb_ref[...], wub_ref[...], wdb_ref[...])
        o_ref[rows, :] = _pack_halves(y)

    for used in range(1, MOE_TM // MOE_PART + 1):
        @pl.when(parts == used)
        def _(used=used):
            run(slice(0, used * MOE_PART))


def _moe_sorted_call(blk, e1, e2, parts, pay_sorted, wg, wu, wd):
    n_tiles = blk.shape[0]
    rows = lambda i, blk, e1, e2, parts: (blk[i], 0)
    wa = lambda i, blk, e1, e2, parts: (e1[i], 0, 0)
    wb = lambda i, blk, e1, e2, parts: (e2[i], 0, 0)
    up = lambda m: pl.BlockSpec((None, D_MODEL, D_EXPERT), m)
    down = lambda m: pl.BlockSpec((None, D_EXPERT, D_MODEL), m)
    return pl.pallas_call(
        _moe_sorted_kernel,
        grid_spec=pltpu.PrefetchScalarGridSpec(
            num_scalar_prefetch=4,
            grid=(n_tiles,),
            in_specs=[pl.BlockSpec((MOE_TM, PAYLOAD_W), rows),
                      up(wa), up(wa), down(wa), up(wb), up(wb), down(wb)],
            out_specs=pl.BlockSpec((MOE_TM, HALF_D), rows),
        ),
        out_shape=jax.ShapeDtypeStruct((n_tiles * MOE_TM, HALF_D), jnp.uint32),
        compiler_params=_cparams(("arbitrary",)),
        name="moe_sorted",
    )(blk, e1, e2, parts, pay_sorted, wg, wu, wd, wg, wu, wd)


DENSE_EXPERTS_PER_STEP = 4


def _moe_dense_kernel(*refs, first_step, resume):
    if resume:
        pay_ref, wg_ref, wu_ref, wd_ref, acc_ref, o_ref = refs
    else:
        pay_ref, wg_ref, wu_ref, wd_ref, o_ref = refs
    step = pl.program_id(0)

    @pl.when(step == 0)
    def _():
        o_ref[...] = acc_ref[...] if resume else jnp.zeros_like(o_ref)

    x, route = _payload_parts(pay_ref)
    y = None
    for k in range(DENSE_EXPERTS_PER_STEP):
        ef = ((first_step + step) * DENSE_EXPERTS_PER_STEP + k).astype(jnp.float32)
        gate = (jnp.where(route[:, 2:3] == ef, route[:, 0:1], 0.0)
                + jnp.where(route[:, 3:4] == ef, route[:, 1:2], 0.0))
        he = (_silu(_dot(x, wg_ref[k])) * _dot(x, wu_ref[k]) * gate).astype(jnp.bfloat16)
        yk = _dot(he, wd_ref[k])
        y = yk if y is None else y + yk
    o_ref[...] += y


def _moe_dense_call(pay, wg, wu, wd, first_step, n_steps, acc=None):
    n = pay.shape[0]
    per = DENSE_EXPERTS_PER_STEP
    whole = pl.BlockSpec((n, D_MODEL), lambda e: (0, 0))
    return pl.pallas_call(
        functools.partial(_moe_dense_kernel, first_step=first_step, resume=acc is not None),
        grid=(n_steps,),
        in_specs=[pl.BlockSpec((n, PAYLOAD_W), lambda e: (0, 0)),
                  pl.BlockSpec((per, D_MODEL, D_EXPERT), lambda e: (first_step + e, 0, 0)),
                  pl.BlockSpec((per, D_MODEL, D_EXPERT), lambda e: (first_step + e, 0, 0)),
                  pl.BlockSpec((per, D_EXPERT, D_MODEL), lambda e: (first_step + e, 0, 0))]
                 + ([whole] if acc is not None else []),
        out_specs=whole,
        out_shape=jax.ShapeDtypeStruct((n, D_MODEL), jnp.float32),
        compiler_params=_cparams(("arbitrary",)),
        name="moe_dense",
    )(pay, wg, wu, wd, *([acc] if acc is not None else []))


def _moe_out(y_ref):
    if y_ref.dtype == jnp.uint32:
        return jnp.concatenate(_unpack_halves(y_ref[...]), axis=-1)
    return y_ref[...]


def _residual_kernel(h_ref, y_ref, g_ref, o_ref):
    o_ref[...] = h_ref[...].astype(jnp.float32) + g_ref[...] * _moe_out(y_ref)


def _residual_call(h, y, g, tm):
    n = h.shape[0]
    row = pl.BlockSpec((tm, D_MODEL), lambda i: (i, 0))
    return pl.pallas_call(
        _residual_kernel,
        grid=(n // tm,),
        in_specs=[row, pl.BlockSpec((tm, y.shape[1]), lambda i: (i, 0)),
                  pl.BlockSpec((1, D_MODEL), lambda i: (0, 0))],
        out_specs=row,
        out_shape=jax.ShapeDtypeStruct((n, D_MODEL), jnp.float32),
        compiler_params=_cparams(("arbitrary",)),
        name="residual",
    )(h, y, g)


SC_ROWS = 128


SC_CORES = 2
SC_SUBCORES = 16
SC_WORKERS = SC_CORES * SC_SUBCORES


def _sc_mesh():
    return plsc.VectorSubcoreMesh(core_axis_name="core", subcore_axis_name="subcore")


def _sc_worker():
    return lax.axis_index("subcore") * SC_CORES + lax.axis_index("core")


def _scatter_rows(x, dest, n_out):
    n, w = x.shape

    per_worker = n // SC_WORKERS
    assert per_worker % SC_ROWS == 0

    @functools.partial(pl.kernel, out_type=jax.ShapeDtypeStruct((n_out, w), x.dtype), mesh=_sc_mesh(),
                       scratch_types=[pltpu.VMEM((SC_ROWS,), jnp.int32), pltpu.VMEM((SC_ROWS, w), x.dtype)])
    def scatter(x_hbm, i_hbm, o_hbm, idx_v, rows_v):
        first = _sc_worker() * per_worker

        @pl.loop(0, per_worker // SC_ROWS)
        def _(i):
            base = pl.multiple_of(first + i * SC_ROWS, SC_ROWS)
            pltpu.sync_copy(i_hbm.at[pl.ds(base, SC_ROWS)], idx_v)
            pltpu.sync_copy(x_hbm.at[pl.ds(base, SC_ROWS)], rows_v)
            pltpu.sync_copy(rows_v, o_hbm.at[idx_v])

    return scatter(x, dest)


def _gather_rows(x, src):
    n = src.shape[0]
    w = x.shape[1]

    per_worker = n // SC_WORKERS
    assert per_worker % SC_ROWS == 0

    @functools.partial(pl.kernel, out_type=jax.ShapeDtypeStruct((n, w), x.dtype), mesh=_sc_mesh(),
                       scratch_types=[pltpu.VMEM((SC_ROWS,), jnp.int32), pltpu.VMEM((SC_ROWS, w), x.dtype)])
    def gather(x_hbm, i_hbm, o_hbm, idx_v, rows_v):
        first = _sc_worker() * per_worker

        @pl.loop(0, per_worker // SC_ROWS)
        def _(i):
            base = pl.multiple_of(first + i * SC_ROWS, SC_ROWS)
            pltpu.sync_copy(i_hbm.at[pl.ds(base, SC_ROWS)], idx_v)
            pltpu.sync_copy(x_hbm.at[idx_v], rows_v)
            pltpu.sync_copy(rows_v, o_hbm.at[pl.ds(base, SC_ROWS)])

    return gather(x, src)


def _routing_plan(cls, n_tiles):
    onehot = (cls[:, None] == jnp.arange(N_CLASSES, dtype=jnp.int32)[None, :]).astype(jnp.int32)
    counts = jnp.sum(onehot, axis=0)
    rank = jnp.sum((jnp.cumsum(onehot, axis=0) - onehot) * onehot, axis=1)
    tiles = (counts + MOE_TM - 1) // MOE_TM
    tile_end = jnp.cumsum(tiles)
    tile_start = tile_end - tiles
    first_len = counts - (tiles - 1) * MOE_TM
    first_of = jnp.sum(onehot * first_len[None, :], axis=1)
    skip = jnp.where(rank >= first_of, MOE_TM - first_of, 0)
    dest = jnp.sum(onehot * tile_start[None, :], axis=1) * MOE_TM + rank + skip
    nact = tile_end[-1]
    tile = jnp.arange(n_tiles, dtype=jnp.int32)
    blk = jnp.minimum(tile, nact - 1)
    tile_cls = jnp.sum((blk[:, None] >= tile_end[None, :]).astype(jnp.int32), axis=1)
    e1 = jnp.asarray(CLASS_E1)[tile_cls]
    e2 = jnp.asarray(CLASS_E2)[tile_cls]
    valid = jnp.where(blk == tile_start[tile_cls], first_len[tile_cls], MOE_TM)
    parts = jnp.where(tile >= nact, 0, (valid + MOE_PART - 1) // MOE_PART)
    return dest.astype(jnp.int32), blk, e1, e2, parts.astype(jnp.int32)


def _row_tile(n, prefer=512):
    return next(t for t in (prefer, 512, 256) if n % t == 0)


def kernel(x, c, ctx, c_ctx, w_ada, b_ada, norm1, w_in, pool_w, pool_scale, q_norm, k_norm, rpb,
           sg_w, sg_b, sg_norm, w_out, norm2, w_router, b_router, w_gate, w_up, w_down):
    depth = w_ada.shape[0]
    n = x.shape[1]
    lc = ctx.shape[1]
    assert x.shape[0] == 1 and x.shape[2] == D_MODEL, "one sample of (tokens, 1024) features"
    assert n % (GRID_W * NA_ROWS_PER_BLOCK) == 0 and n % (SC_WORKERS * SC_ROWS) == 0 and lc % 256 == 0
    bf = jnp.bfloat16
    lat_stream = (x[0],)
    h_ctx = ctx[0]

    cond = jnp.stack([c[0], c_ctx], axis=1)
    mod = _ada_call(cond, w_ada, b_ada)

    wr_t = w_router.T
    wr_hi = wr_t.astype(bf)
    wr_lo = (wr_t - wr_hi.astype(jnp.float32)).astype(bf)
    wr = jnp.concatenate([wr_hi, wr_lo], axis=0)
    br = b_router.reshape(N_EXPERTS, 1)
    n_tiles = n // MOE_TM + N_CLASSES
    w_in_bf = w_in.astype(bf)
    w_out_bf = w_out.astype(bf)

    for l in range(depth):
        last = l == depth - 1
        qg =(q_norm[l] * (HEAD_DIM ** -0.5 * LOG2_E)).reshape(1, NA_DIM)
        kg = k_norm[l].reshape(1, NA_DIM)
        sgn = sg_norm[l].reshape(1, SG_DIM)
        sgw = sg_w[l].astype(bf).reshape(SG_DIM // LANES, 2 * SG_CHUNK, SG_CHUNK)
        sgb = jnp.broadcast_to(sg_b[l].reshape(SG_DIM // LANES, 2 * SG_CHUNK, 1),
                               (SG_DIM // LANES, 2 * SG_CHUNK, LANES))
        pool_bd = jax.scipy.linalg.block_diag(*[pool_w[l, g] for g in range(len(POOL_WINDOWS))]).astype(bf)
        pscale = pool_scale[l].reshape(1, POOL_DIM)
        bias = _natten_bias(rpb[l])
        n1 = norm1[l].reshape(1, D_MODEL)
        n2 = norm2[l].reshape(1, D_MODEL)

        def mods(row):
            return [mod[l, row:row + 1, i * D_MODEL:(i + 1) * D_MODEL] for i in range(6)]

        sh1, sc1, g1, sh2, sc2, g2 = mods(0)
        csh1, csc1, cg1, csh2, csc2, cg2 = mods(1)

        tc = _row_tile(lc)
        mix_pool_c, q_c, k_c, v_c, sg_c = _inproj_call((h_ctx,), n1, csh1, csc1, w_in_bf, l, qg, kg,
                                                       sgn, sgw, sgb, pool_bd, pscale, tc, strips=False)

        outs = _inproj_call(lat_stream, n1, sh1, sc1, w_in_bf, l, qg, kg, sgn, sgw, sgb,
                            pool_bd, pscale, _row_tile(n, INPROJ_TILE), strips=True)
        mix_pool, q, k, v, sg = outs[:5]
        h_lat = outs[5] if len(lat_stream) == 3 else lat_stream[0]
        att, wg_l, wu_l, wd_l = _natten_call(q, k, v, k_c, v_c, bias, w_gate, w_up, w_down, l)
        h1, pay, route = _outproj_call(h_lat, mix_pool, att, sg, w_out_bf, l, g1, n2, sh2, sc2, wr, br,
                                       _row_tile(n, OUTPROJ_TILE))
        cls = route[4].astype(jnp.int32)
        dest, blk, e1, e2, parts = _routing_plan(cls, n_tiles)
        pay_sorted = _scatter_rows(pay, dest, n_tiles * MOE_TM)

        if not last:
            dense_steps = N_EXPERTS // DENSE_EXPERTS_PER_STEP
            att_c = _ctxatt_call(q_c, k_c, v_c)
            h1_c, pay_c, _ = _outproj_call(h_ctx, mix_pool_c, att_c, sg_c, w_out_bf, l, cg1, n2, csh2, csc2,
                                           wr, br, tc)
            y_c = _moe_dense_call(pay_c, wg_l, wu_l, wd_l, 0, dense_steps // 2)
            parts, y_c = lax.optimization_barrier((parts, y_c))

        y_sorted = _moe_sorted_call(blk, e1, e2, parts, pay_sorted, wg_l, wu_l, wd_l)
        y = _gather_rows(y_sorted, dest)
        lat_stream = (h1, y, g2)

        if not last:
            y_c = _moe_dense_call(pay_c, wg_l, wu_l, wd_l, dense_steps // 2, dense_steps - dense_steps // 2,
                                  acc=y_c)
            h_ctx = _residual_call(h1_c, y_c, cg2, tc)

    return _residual_call(*lat_stream, _row_tile(n, RESIDUAL_TILE))[None]
```

```python
import functools

import jax
import jax.numpy as jnp
import numpy as np
from jax import lax
from jax.experimental import pallas as pl
from jax.experimental.pallas import tpu as pltpu
from jax.experimental.pallas import tpu_sc as plsc

D_MODEL = 1024
GRID_W = 64
HEAD_DIM = 64
POOL_WINDOWS = (2, 4, 8, 16)
POOL_DIM = 256
NA_HEADS = 8
NA_DIM = 512
NA_WIN_ROWS = 8
NA_WIN_COLS = 16
SG_DIM = 256
SG_CHUNK = 128
Q_OFF = POOL_DIM
K_OFF = Q_OFF + NA_DIM
V_OFF = K_OFF + NA_DIM
U_OFF = V_OFF + NA_DIM
G_OFF = U_OFF + SG_DIM
IN_DIM = G_OFF + SG_DIM
N_EXPERTS = 16
GROUP_SIZE = 4
D_EXPERT = 512
EPS = 1e-6

LANES = 128
SUBLANES = 8
HEAD_PAIRS = NA_DIM // LANES
VMEM_LIMIT = 48 * 1024 * 1024

PAIRS = ((0, 1), (0, 2), (1, 2), (1, 3), (0, 3), (2, 3))
N_GROUPS = N_EXPERTS // GROUP_SIZE
N_CLASSES = N_GROUPS * len(PAIRS)
CLASS_E1 = np.array([GROUP_SIZE * g + i for g in range(N_GROUPS) for (i, j) in PAIRS], np.int32)
CLASS_E2 = np.array([GROUP_SIZE * g + j for g in range(N_GROUPS) for (i, j) in PAIRS], np.int32)

ROUTE_ROWS = 8
HALF_D = D_MODEL // 2
PAYLOAD_W = HALF_D + LANES
MOE_TM = 512
NEG_BIG = -1e30
LOG2_E = 1.4426950408889634


def _cparams(sem):
    return pltpu.CompilerParams(dimension_semantics=sem, vmem_limit_bytes=VMEM_LIMIT)


def _dot(a, b):
    return jnp.dot(a, b, preferred_element_type=jnp.float32)


def _dot_nt(a, b):
    return lax.dot_general(a, b, (((1,), (1,)), ((), ())), preferred_element_type=jnp.float32)


def _gelu_tanh(x):
    return 0.5 * x * (1.0 + jnp.tanh(0.7978845608028654 * (x + 0.044715 * (x * x * x))))


def _silu(x):
    return x * (1.0 / (1.0 + jnp.exp(-x)))


def _pack_halves(x):
    w = x.shape[1] // 2
    lo = pltpu.bitcast(x[:, :w].astype(jnp.bfloat16).astype(jnp.float32), jnp.uint32) >> 16
    hi = pltpu.bitcast(x[:, w:].astype(jnp.bfloat16).astype(jnp.float32), jnp.uint32) & jnp.uint32(0xFFFF0000)
    return lo | hi


def _unpack_halves(words):
    lo = pltpu.bitcast(words << 16, jnp.float32)
    hi = pltpu.bitcast(words & jnp.uint32(0xFFFF0000), jnp.float32)
    return lo, hi


def _ada_kernel(cond_ref, w_ref, b_ref, o_ref):
    cond = _silu(cond_ref[...])
    w = w_ref[...]
    rows = [jnp.sum(w * cond[:, r:r + 1], axis=0, keepdims=True) + b_ref[...] for r in range(2)]
    o_ref[...] = jnp.concatenate(rows + [jnp.zeros((SUBLANES - 2, w.shape[1]), jnp.float32)], axis=0)


def _ada_call(cond, w_ada, b_ada):
    depth = w_ada.shape[0]
    tn = 1536
    return pl.pallas_call(
        _ada_kernel,
        grid=(depth, 6 * D_MODEL // tn),
        in_specs=[
            pl.BlockSpec((D_MODEL, 2), lambda l, j: (0, 0)),
            pl.BlockSpec((None, D_MODEL, tn), lambda l, j: (l, 0, j)),
            pl.BlockSpec((None, 1, tn), lambda l, j: (l, 0, j)),
        ],
        out_specs=pl.BlockSpec((None, SUBLANES, tn), lambda l, j: (l, 0, j)),
        out_shape=jax.ShapeDtypeStruct((depth, SUBLANES, 6 * D_MODEL), jnp.float32),
        compiler_params=_cparams(("arbitrary", "arbitrary")),
        name="adaln",
    )(cond, w_ada, b_ada.reshape(depth, 1, 6 * D_MODEL))


def _norm_modulate(x, n_ref, sh_ref, sc_ref):
    ms = jnp.mean(x * x, axis=-1, keepdims=True)
    return ((x * lax.rsqrt(ms + EPS)) * (n_ref[...] * (1.0 + sc_ref[...])) + sh_ref[...]).astype(jnp.bfloat16)


def _head_rms_scale(a):
    low = lax.broadcasted_iota(jnp.int32, (a.shape[0], LANES), 1) < HEAD_DIM
    blocks = []
    for p in range(a.shape[1] // LANES):
        sq = jnp.square(a[:, p * LANES:(p + 1) * LANES])
        s_lo = jnp.sum(jnp.where(low, sq, 0.0), axis=-1, keepdims=True)
        s_hi = jnp.sum(jnp.where(low, 0.0, sq), axis=-1, keepdims=True)
        blocks.append(lax.rsqrt(jnp.where(low, s_lo, s_hi) * (1.0 / HEAD_DIM) + EPS))
    return jnp.concatenate(blocks, axis=-1)


STRIP_W = 8
N_STRIPS = GRID_W // STRIP_W


def _store_keys(ref, x):
    if len(ref.shape) == 2:
        ref[...] = x.astype(jnp.bfloat16)
        return
    pair = 2 * STRIP_W
    for s in range(N_STRIPS):
        for rp in range(x.shape[0] // (2 * GRID_W)):
            top = 2 * rp * GRID_W + s * STRIP_W
            rows = jnp.concatenate([x[top:top + STRIP_W], x[top + GRID_W:top + GRID_W + STRIP_W]], axis=0)
            ref[s, rp * pair:(rp + 1) * pair, :] = rows.astype(jnp.bfloat16)


POOL_HALO = 8
HALO_BLOCK = 16
STREAM_DTYPE = jnp.bfloat16


POOL_EDGE = 16


def _pool_mix(xe_ref, w_ref, scale_ref, tm, seq_len):
    low = lax.broadcasted_iota(jnp.int32, (tm, LANES), 1) < HEAD_DIM
    t_edge = pl.program_id(0) * tm + lax.broadcasted_iota(jnp.int32, (POOL_EDGE, LANES), 0)

    def window_mean(s, half):
        mean = s * (1.0 / (2 * half))

        def clip_fix(t):
            count = (jnp.minimum(t + half, seq_len) - jnp.maximum(t - half, 0)).astype(jnp.float32)
            return (2.0 * half) / count

        return jnp.concatenate([mean[:POOL_EDGE] * clip_fix(t_edge),
                                mean[POOL_EDGE:tm - POOL_EDGE],
                                mean[tm - POOL_EDGE:] * clip_fix(t_edge + (tm - POOL_EDGE))], axis=0)

    def window_sums(xs, n_levels):
        sums = []
        s = xs
        for k in range(n_levels):
            step = 1 << k
            s = s[:-step] + s[step:]
            sums.append(s)
        return sums

    outs = []
    for half_block, windows in enumerate(((2, 4), (8, 16))):
        xs = xe_ref[:, half_block * LANES:(half_block + 1) * LANES]
        sums = window_sums(xs, int(np.log2(windows[1])))
        parts = []
        for w in windows:
            half = w // 2
            s = sums[int(np.log2(w)) - 1][POOL_HALO - half:POOL_HALO - half + tm]
            parts.append(window_mean(s, half))
        mean = jnp.where(low, parts[0], parts[1])
        outs.append(mean - xs[POOL_HALO:POOL_HALO + tm])
    d = jnp.concatenate(outs, axis=-1).astype(jnp.bfloat16)
    return (_dot(d, w_ref[...]) * scale_ref[...]).astype(jnp.bfloat16)


def _inproj_kernel(*refs, pending, seq_len):
    n_stream = 7 if pending else 3
    stream, refs = refs[:n_stream], refs[n_stream:]
    if pending:
        h_ref, y_ref, g_ref, hp_ref, hn_ref, yp_ref, yn_ref = stream
        hres_ref, refs = refs[-2], refs[:-2] + refs[-1:]
        x = h_ref[...].astype(jnp.float32) + g_ref[...] * _moe_out(y_ref)
        hres_ref[...] = x.astype(hres_ref.dtype)
        before = hp_ref[...].astype(jnp.float32) + g_ref[...] * _moe_out(yp_ref)
        after = hn_ref[...].astype(jnp.float32) + g_ref[...] * _moe_out(yn_ref)
    else:
        h_ref, hp_ref, hn_ref = stream
        x = h_ref[...].astype(jnp.float32)
        before = hp_ref[...].astype(jnp.float32)
        after = hn_ref[...].astype(jnp.float32)
    x_halo = jnp.concatenate([before[HALO_BLOCK - POOL_HALO:], after[:POOL_HALO]], axis=0)
    (n1_ref, sh_ref, sc_ref, w_ref, qg_ref, kg_ref, sgn_ref, sgw_ref, sgb_ref,
     pw_ref, ps_ref, pool_ref, q_ref, k_ref, v_ref, sg_ref, xe_ref) = refs
    tm = h_ref.shape[0]
    i = pl.program_id(0)
    hn = _norm_modulate(x, n1_ref, sh_ref, sc_ref)
    hn_halo = _norm_modulate(x_halo, n1_ref, sh_ref, sc_ref)

    a_halo = _dot(hn_halo, w_ref[:, 0:Q_OFF])
    xe_ref[0:POOL_HALO, :] = jnp.where(i > 0, a_halo[:POOL_HALO], 0.0)
    xe_ref[POOL_HALO:POOL_HALO + tm, :] = _dot(hn, w_ref[:, 0:Q_OFF])
    xe_ref[POOL_HALO + tm:, :] = jnp.where(i < pl.num_programs(0) - 1, a_halo[POOL_HALO:], 0.0)

    a_g = _dot(hn, w_ref[:, G_OFF:IN_DIM])
    a_u = _dot(hn, w_ref[:, U_OFF:G_OFF])
    a_q = _dot(hn, w_ref[:, Q_OFF:K_OFF])
    a_k = _dot(hn, w_ref[:, K_OFF:V_OFF])
    _store_keys(v_ref, _dot(hn, w_ref[:, V_OFF:U_OFF]))

    gv = _gelu_tanh(a_g)
    q_ref[...] = (a_q * _head_rms_scale(a_q) * qg_ref[...]).astype(jnp.bfloat16)
    _store_keys(k_ref, a_k * _head_rms_scale(a_k) * kg_ref[...])

    u = _gelu_tanh(a_u)
    vn = (gv * _head_rms_scale(gv) * sgn_ref[...]).astype(jnp.bfloat16)
    low = lax.broadcasted_iota(jnp.int32, (SG_CHUNK, LANES), 1) < HEAD_DIM
    for c in range(tm // SG_CHUNK):
        rows = slice(c * SG_CHUNK, (c + 1) * SG_CHUNK)
        for s in range(SG_DIM // LANES):
            cols = slice(s * LANES, (s + 1) * LANES)
            m = _dot(sgw_ref[s], vn[rows, cols]) + sgb_ref[s]
            mixed = jnp.where(low, m[:SG_CHUNK], m[SG_CHUNK:])
            sg_ref[rows, cols] = (u[rows, cols] * mixed).astype(jnp.bfloat16)

    pool_ref[...] = _pool_mix(xe_ref, pw_ref, ps_ref, tm, seq_len)


def _inproj_call(stream, n1, sh, sc, w_in, layer, qg, kg, sgn, sgw, sgb, pool_w, pool_scale, tm, strips):
    pending = len(stream) == 3
    n = stream[0].shape[0]
    assert n % tm == 0 and tm % SG_CHUNK == 0 and tm % (2 * GRID_W) == 0
    if strips:
        kv_spec = pl.BlockSpec((N_STRIPS, tm // N_STRIPS, NA_DIM), lambda i: (0, i, 0))
        kv_shape = jax.ShapeDtypeStruct((N_STRIPS, n // N_STRIPS, NA_DIM), jnp.bfloat16)
    else:
        kv_spec = pl.BlockSpec((tm, NA_DIM), lambda i: (i, 0))
        kv_shape = jax.ShapeDtypeStruct((n, NA_DIM), jnp.bfloat16)
    row = lambda i: (i, 0)
    fixed2 = lambda i: (0, 0)
    fixed3 = lambda i: (0, 0, 0)
    vec = lambda w: pl.BlockSpec((1, w), fixed2)
    rows = pl.BlockSpec((tm, D_MODEL), row)
    per_tile = tm // HALO_BLOCK
    before = lambda i: (jnp.maximum(i * per_tile - 1, 0), 0)
    after = lambda i: (jnp.minimum((i + 1) * per_tile, n // HALO_BLOCK - 1), 0)
    halo = lambda w, m: pl.BlockSpec((HALO_BLOCK, w), m)
    h = stream[0]
    if pending:
        y, g = stream[1], stream[2]
        yw = y.shape[1]
        args = [h, y, g, h, h, y, y]
        stream_specs = [rows, pl.BlockSpec((tm, yw), row), vec(D_MODEL),
                        halo(D_MODEL, before), halo(D_MODEL, after), halo(yw, before), halo(yw, after)]
    else:
        args = [h, h, h]
        stream_specs = [rows, halo(D_MODEL, before), halo(D_MODEL, after)]
    extra_out_specs = [rows] if pending else []
    extra_out_shape = [jax.ShapeDtypeStruct((n, D_MODEL), STREAM_DTYPE)] if pending else []
    return pl.pallas_call(
        functools.partial(_inproj_kernel, pending=pending, seq_len=n),
        grid=(n // tm,),
        in_specs=stream_specs + [
            vec(D_MODEL), vec(D_MODEL), vec(D_MODEL),
            pl.BlockSpec((None, D_MODEL, IN_DIM), lambda i: (layer, 0, 0)),
            vec(NA_DIM), vec(NA_DIM),
            vec(SG_DIM),
            pl.BlockSpec((SG_DIM // LANES, 2 * SG_CHUNK, SG_CHUNK), fixed3),
            pl.BlockSpec((SG_DIM // LANES, 2 * SG_CHUNK, LANES), fixed3),
            pl.BlockSpec((POOL_DIM, POOL_DIM), fixed2),
            vec(POOL_DIM),
        ],
        out_specs=[
            pl.BlockSpec((tm, POOL_DIM), row),
            pl.BlockSpec((tm, NA_DIM), row),
            kv_spec,
            kv_spec,
            pl.BlockSpec((tm, SG_DIM), row),
        ] + extra_out_specs,
        out_shape=[
            jax.ShapeDtypeStruct((n, POOL_DIM), jnp.bfloat16),
            jax.ShapeDtypeStruct((n, NA_DIM), jnp.bfloat16),
            kv_shape,
            kv_shape,
            jax.ShapeDtypeStruct((n, SG_DIM), jnp.bfloat16),
        ] + extra_out_shape,
        scratch_shapes=[pltpu.VMEM((tm + 2 * POOL_HALO, POOL_DIM), jnp.float32)],
        compiler_params=_cparams(("arbitrary",)),
        name="inproj",
    )(*args, n1, sh, sc, w_in, qg, kg, sgn, sgw, sgb, pool_w, pool_scale)


NA_ROWS_PER_BLOCK = 64
NA_GROUP_ROWS = 4
NA_WINDOW_ROWS = NA_GROUP_ROWS + NA_WIN_ROWS
NA_BLOCK = NA_ROWS_PER_BLOCK * GRID_W
NA_GROUP = NA_GROUP_ROWS * GRID_W
NA_HALF_COLS = GRID_W // 2
NA_CHAIN = NA_GROUP_ROWS * NA_HALF_COLS
NA_HALF_STRIPS = 5
NA_HALF_COL0 = (0, GRID_W - NA_HALF_STRIPS * STRIP_W)
NA_RUN = NA_WINDOW_ROWS * STRIP_W
NA_LOCAL = 512
NA_STRIP_BLOCK = NA_ROWS_PER_BLOCK * STRIP_W
NA_STRIP_HALO = (NA_WIN_ROWS // 2) * STRIP_W
NA_EDGE_FIRST, NA_EDGE_NONE, NA_EDGE_LAST = 0, 1, 2


def _stack_heads(x, low):
    zero = jnp.zeros_like(x)
    return jnp.concatenate([jnp.where(low, x, zero), jnp.where(low, zero, x)], axis=0)


def _natten_kernel(q_ref, kp_ref, kc_ref, kn_ref, vp_ref, vc_ref, vn_ref, kx_ref, vx_ref, bias_ref,
                   wg_ref, wu_ref, wd_ref,
                   o_ref, wg_bf_ref, wu_bf_ref, wd_bf_ref, kwin_ref, vwin_ref, vxe_ref, *, grid_rows):
    b = pl.program_id(1)
    wg_bf_ref[...] = wg_ref[...].astype(jnp.bfloat16)
    wu_bf_ref[...] = wu_ref[...].astype(jnp.bfloat16)
    wd_bf_ref[...] = wd_ref[...].astype(jnp.bfloat16)
    top, bottom = NA_STRIP_HALO, NA_STRIP_HALO + NA_STRIP_BLOCK
    kwin_ref[:, 0:top, :] = kp_ref[...]
    kwin_ref[:, top:bottom, :] = kc_ref[...]
    kwin_ref[:, bottom:, :] = kn_ref[...]
    vwin_ref[:, 0:top, 0:LANES] = vp_ref[...]
    vwin_ref[:, top:bottom, 0:LANES] = vc_ref[...]
    vwin_ref[:, bottom:, 0:LANES] = vn_ref[...]
    vwin_ref[:, :, LANES:] = jnp.ones(vwin_ref.shape[:2] + (LANES,), jnp.bfloat16)
    vxe_ref[:, 0:LANES] = vx_ref[...]
    vxe_ref[:, LANES:] = jnp.ones((vxe_ref.shape[0], LANES), jnp.bfloat16)
    low_q = lax.broadcasted_iota(jnp.int32, (NA_CHAIN, LANES), 1) < HEAD_DIM
    n_pad = NA_LOCAL - NA_HALF_STRIPS * NA_RUN

    def window_start(g):
        r0 = b * NA_ROWS_PER_BLOCK + g * NA_GROUP_ROWS
        ws = jnp.clip(r0 - NA_WIN_ROWS // 2, 0, grid_rows - NA_WINDOW_ROWS)
        edge = jnp.where(r0 == 0, NA_EDGE_FIRST,
                         jnp.where(r0 == grid_rows - NA_GROUP_ROWS, NA_EDGE_LAST, NA_EDGE_NONE))
        start = pl.multiple_of((ws - b * NA_ROWS_PER_BLOCK + NA_WIN_ROWS // 2) * STRIP_W, NA_STRIP_HALO)
        return start, edge

    def local_window(win_ref, g, half):
        start, _ = window_start(g)
        s0 = NA_HALF_COL0[half] // STRIP_W
        runs = [win_ref[s, pl.ds(start, NA_RUN), :] for s in range(s0, s0 + NA_HALF_STRIPS)]
        return jnp.concatenate(runs + [jnp.zeros((n_pad, win_ref.shape[2]), jnp.bfloat16)], axis=0)

    def query_rows(g, half, j):
        first = g * NA_GROUP + j * GRID_W + half * NA_HALF_COLS
        return slice(first, first + NA_HALF_COLS)

    def scores(c):
        g, half = divmod(c, 2)
        _, edge = window_start(g)
        qh = jnp.concatenate([q_ref[query_rows(g, half, j), :] for j in range(NA_GROUP_ROWS)], axis=0)
        lhs = _stack_heads(qh, low_q)
        kl = local_window(kwin_ref, g, half)
        return jnp.concatenate([_dot_nt(lhs, kl).astype(jnp.bfloat16) + bias_ref[edge, half],
                                _dot_nt(lhs, kx_ref[...]).astype(jnp.bfloat16)], axis=-1)

    n_chains = 2 * (NA_ROWS_PER_BLOCK // NA_GROUP_ROWS)
    s_next = scores(0)
    for c in range(n_chains):
        s = s_next
        if c + 1 < n_chains:
            s_next = scores(c + 1)
        g, half = divmod(c, 2)
        vl = local_window(vwin_ref, g, half)
        m = jnp.max(s, axis=-1, keepdims=True)
        pb = jnp.exp2(s - m)
        o = _dot(pb[:, :NA_LOCAL], vl) + _dot(pb[:, NA_LOCAL:], vxe_ref[...])
        o = o[:, :LANES] * (1.0 / o[:, LANES:])
        o = jnp.where(low_q, o[:NA_CHAIN], o[NA_CHAIN:]).astype(jnp.bfloat16)
        for j in range(NA_GROUP_ROWS):
            o_ref[query_rows(g, half, j), :] = o[j * NA_HALF_COLS:(j + 1) * NA_HALF_COLS]


def _natten_call(q, k, v, k_ctx, v_ctx, bias, w_gate, w_up, w_down, layer):
    n = q.shape[0]
    grid_rows = n // GRID_W
    assert grid_rows % NA_ROWS_PER_BLOCK == 0 and grid_rows >= 2 * NA_ROWS_PER_BLOCK
    nblk = n // NA_BLOCK
    steps = HEAD_PAIRS * nblk
    depth = w_gate.shape[0]
    up_rows = N_EXPERTS * D_MODEL
    down_rows = N_EXPERTS * D_EXPERT
    assert up_rows % steps == 0 and down_rows % steps == 0
    wg2 = w_gate.reshape(depth * up_rows, D_EXPERT)
    wu2 = w_up.reshape(depth * up_rows, D_EXPERT)
    wd2 = w_down.reshape(depth * down_rows, D_MODEL)
    up_in = pl.BlockSpec((up_rows // steps, D_EXPERT), lambda p, b: (layer * steps + p * nblk + b, 0))
    down_in = pl.BlockSpec((down_rows // steps, D_MODEL), lambda p, b: (layer * steps + p * nblk + b, 0))
    up_out = pl.BlockSpec((up_rows // steps, D_EXPERT), lambda p, b: (p * nblk + b, 0))
    down_out = pl.BlockSpec((down_rows // steps, D_MODEL), lambda p, b: (p * nblk + b, 0))
    n_halo = n // N_STRIPS // NA_STRIP_HALO
    hb = NA_STRIP_BLOCK // NA_STRIP_HALO
    rows = pl.BlockSpec((NA_BLOCK, LANES), lambda p, b: (b, p))
    cur = pl.BlockSpec((N_STRIPS, NA_STRIP_BLOCK, LANES), lambda p, b: (0, b, p))
    prev = pl.BlockSpec((N_STRIPS, NA_STRIP_HALO, LANES), lambda p, b: (0, jnp.maximum(b * hb - 1, 0), p))
    nxt = pl.BlockSpec((N_STRIPS, NA_STRIP_HALO, LANES),
                       lambda p, b: (0, jnp.minimum((b + 1) * hb, n_halo - 1), p))
    ctx = pl.BlockSpec((k_ctx.shape[0], LANES), lambda p, b: (0, p))
    win_rows = NA_STRIP_BLOCK + 2 * NA_STRIP_HALO
    att, wg_bf, wu_bf, wd_bf = pl.pallas_call(
        functools.partial(_natten_kernel, grid_rows=grid_rows),
        grid=(HEAD_PAIRS, nblk),
        in_specs=[rows, prev, cur, nxt, prev, cur, nxt, ctx, ctx,
                  pl.BlockSpec((None, 3, 2, 2 * NA_CHAIN, NA_LOCAL), lambda p, b: (p, 0, 0, 0, 0)),
                  up_in, up_in, down_in],
        out_specs=[rows, up_out, up_out, down_out],
        out_shape=[jax.ShapeDtypeStruct((n, NA_DIM), jnp.bfloat16),
                   jax.ShapeDtypeStruct((up_rows, D_EXPERT), jnp.bfloat16),
                   jax.ShapeDtypeStruct((up_rows, D_EXPERT), jnp.bfloat16),
                   jax.ShapeDtypeStruct((down_rows, D_MODEL), jnp.bfloat16)],
        scratch_shapes=[pltpu.VMEM((N_STRIPS, win_rows, LANES), jnp.bfloat16),
                        pltpu.VMEM((N_STRIPS, win_rows, 2 * LANES), jnp.bfloat16),
                        pltpu.VMEM((k_ctx.shape[0], 2 * LANES), jnp.bfloat16)],
        compiler_params=_cparams(("arbitrary", "arbitrary")),
        name="natten",
    )(q, k, k, k, v, v, v, k_ctx, v_ctx, bias, wg2, wu2, wd2)
    return (att, wg_bf.reshape(N_EXPERTS, D_MODEL, D_EXPERT), wu_bf.reshape(N_EXPERTS, D_MODEL, D_EXPERT),
            wd_bf.reshape(N_EXPERTS, D_EXPERT, D_MODEL))


def _natten_bias(rpb):
    cols = np.arange(GRID_W)
    col_start = np.clip(cols - NA_WIN_COLS // 2, 0, GRID_W - NA_WIN_COLS)
    kc = np.arange(GRID_W)
    in_win = (kc[None, :] >= col_start[:, None]) & (kc[None, :] < col_start[:, None] + NA_WIN_COLS)
    dc = kc[None, :] - cols[:, None] + NA_WIN_COLS - 1
    sel = (np.arange(2 * NA_WIN_COLS - 1)[:, None, None] == dc[None]) & in_win[None]
    t2 = jnp.einsum("hdj,jqk->hdqk", rpb, jnp.asarray(sel, jnp.float32), precision=lax.Precision.HIGHEST)
    t2 = jnp.where(in_win[None, None], t2 * LOG2_E, NEG_BIG)
    neg = jnp.full((NA_HEADS, 1, GRID_W, GRID_W), NEG_BIG, jnp.float32)
    t2e = jnp.concatenate([neg, t2, neg], axis=1)
    u = jnp.concatenate([t2e[:, :-1], t2e[:, 1:]], axis=-1)
    u = u.reshape(HEAD_PAIRS, 2, 2 * NA_WIN_ROWS, GRID_W, LANES)

    place = np.zeros((2, NA_WINDOW_ROWS * GRID_W, NA_LOCAL), np.float32)
    for half, c0 in enumerate(NA_HALF_COL0):
        for a in range(NA_WINDOW_ROWS):
            for kcol in range(c0, c0 + NA_HALF_STRIPS * STRIP_W):
                s, c8 = divmod(kcol - c0, STRIP_W)
                place[half, a * GRID_W + kcol, s * NA_RUN + a * STRIP_W + c8] = 1.0
    outside = np.full((3, NA_GROUP_ROWS, NA_LOCAL), NEG_BIG, np.float32)
    for edge in (NA_EDGE_FIRST, NA_EDGE_NONE, NA_EDGE_LAST):
        for j in range(NA_GROUP_ROWS):
            lo, _ = _window_rows(edge, j)
            for s in range(NA_HALF_STRIPS):
                outside[edge, j, s * NA_RUN + lo * STRIP_W:s * NA_RUN + (lo + NA_WIN_ROWS) * STRIP_W] = 0.0
    return pl.pallas_call(
        _bias_expand_kernel,
        grid=(HEAD_PAIRS,),
        in_specs=[pl.BlockSpec((None, 2, 2 * NA_WIN_ROWS, GRID_W, LANES), lambda p: (p, 0, 0, 0, 0)),
                  pl.BlockSpec(place.shape, lambda p: (0, 0, 0)),
                  pl.BlockSpec(outside.shape, lambda p: (0, 0, 0))],
        out_specs=pl.BlockSpec((None, 3, 2, 2 * NA_CHAIN, NA_LOCAL), lambda p: (p, 0, 0, 0, 0)),
        out_shape=jax.ShapeDtypeStruct((HEAD_PAIRS, 3, 2, 2 * NA_CHAIN, NA_LOCAL), jnp.bfloat16),
        compiler_params=_cparams(("arbitrary",)),
        name="bias_expand",
    )(u, jnp.asarray(place, jnp.bfloat16), jnp.asarray(outside))


def _window_rows(edge, j):
    if edge == NA_EDGE_FIRST:
        return 0, NA_WIN_ROWS - 1 - j
    if edge == NA_EDGE_NONE:
        return j, NA_WIN_ROWS // 2 - 1
    return NA_WINDOW_ROWS - NA_WIN_ROWS, NA_WIN_ROWS // 2 - 1 - j


def _bias_expand_kernel(u_ref, place_ref, outside_ref, o_ref):
    low = lax.broadcasted_iota(jnp.int32, (NA_HALF_COLS, LANES), 1) < GRID_W
    zero = jnp.zeros((NA_HALF_COLS, LANES), jnp.float32)
    for edge in (NA_EDGE_FIRST, NA_EDGE_NONE, NA_EDGE_LAST):
        for half in range(2):
            q0 = half * NA_HALF_COLS
            blocks, masks = [], []
            for hd in range(2):
                for j in range(NA_GROUP_ROWS):
                    lo, base = _window_rows(edge, j)
                    tiles = []
                    for i in range(NA_WINDOW_ROWS // 2):
                        a0, a1 = 2 * i, 2 * i + 1
                        ok0 = lo <= a0 < lo + NA_WIN_ROWS
                        ok1 = lo <= a1 < lo + NA_WIN_ROWS
                        if not (ok0 or ok1):
                            tile = zero
                        else:
                            tile = u_ref[hd, base + a1 - lo, q0:q0 + NA_HALF_COLS, :]
                            if not ok0:
                                tile = jnp.where(low, zero, tile)
                            if not ok1:
                                tile = jnp.where(low, tile, zero)
                        tiles.append(tile)
                    blocks.append(jnp.concatenate(tiles, axis=-1))
                    masks.append(jnp.broadcast_to(outside_ref[edge, j:j + 1, :], (NA_HALF_COLS, NA_LOCAL)))
            lhs = jnp.concatenate(blocks, axis=0).astype(jnp.bfloat16)
            placed = _dot(lhs, place_ref[half]) + jnp.concatenate(masks, axis=0)
            o_ref[edge, half] = placed.astype(o_ref.dtype)


def _ctxatt_kernel(q_ref, k_ref, v_ref, o_ref):
    lc = q_ref.shape[0]
    low = lax.broadcasted_iota(jnp.int32, (lc, LANES), 1) < HEAD_DIM
    lhs = _stack_heads(q_ref[...], low)
    s = _dot_nt(lhs, k_ref[...])
    m = jnp.max(s, axis=-1, keepdims=True)
    p = jnp.exp2(s - m)
    denom = jnp.sum(p, axis=-1, keepdims=True)
    o = _dot(p.astype(jnp.bfloat16), v_ref[...]) * (1.0 / denom)
    o_ref[...] = jnp.where(low, o[:lc], o[lc:]).astype(jnp.bfloat16)


def _ctxatt_call(q, k, v):
    lc = q.shape[0]
    spec = pl.BlockSpec((lc, LANES), lambda p: (0, p))
    return pl.pallas_call(
        _ctxatt_kernel,
        grid=(HEAD_PAIRS,),
        in_specs=[spec, spec, spec],
        out_specs=spec,
        out_shape=jax.ShapeDtypeStruct((lc, NA_DIM), jnp.bfloat16),
        compiler_params=_cparams(("arbitrary",)),
        name="ctxatt",
    )(q, k, v)


OUTPROJ_CHAIN = 256
OUTPROJ_TILE = 1024
INPROJ_TILE = 1024
RESIDUAL_TILE = 2048


def _outproj_kernel(h_ref, mp_ref, att_ref, sg_ref, wo_ref, g1_ref, n2_ref, sh_ref, sc_ref,
                    wr_ref, br_ref, h1_ref, pay_ref, route_ref):
    chains = [slice(c * OUTPROJ_CHAIN, (c + 1) * OUTPROJ_CHAIN) for c in range(h_ref.shape[0] // OUTPROJ_CHAIN)]
    h1s = []
    for rows in chains:
        mix = (_dot(mp_ref[rows, :], wo_ref[0:POOL_DIM, :])
               + _dot(att_ref[rows, :], wo_ref[POOL_DIM:POOL_DIM + NA_DIM, :])
               + _dot(sg_ref[rows, :], wo_ref[POOL_DIM + NA_DIM:, :]))
        h1 = h_ref[rows, :].astype(jnp.float32) + g1_ref[...] * mix
        h1_ref[rows, :] = h1.astype(h1_ref.dtype)
        h1s.append(h1)
    for rows, h1 in zip(chains, h1s):
        _outproj_route(rows, h1, n2_ref, sh_ref, sc_ref, wr_ref, br_ref, pay_ref, route_ref)


def _outproj_route(rows, h1, n2_ref, sh_ref, sc_ref, wr_ref, br_ref, pay_ref, route_ref):
    tm = OUTPROJ_CHAIN
    ms = jnp.mean(h1 * h1, axis=-1, keepdims=True)
    hm = (h1 * lax.rsqrt(ms + EPS)) * (n2_ref[...] * (1.0 + sc_ref[...])) + sh_ref[...]
    pay_ref[rows, 0:HALF_D] = _pack_halves(hm)

    hm_hi = hm.astype(jnp.bfloat16)
    lt = _dot_nt(wr_ref[...], hm_hi)
    logits = lt[:N_EXPERTS] + lt[N_EXPERTS:] + br_ref[...]
    e = jnp.exp(logits - jnp.max(logits, axis=0, keepdims=True))

    best = ga = gb = e1 = e2 = cls = None
    for c in range(N_CLASSES):
        a, b2 = int(CLASS_E1[c]), int(CLASS_E2[c])
        ea, eb = e[a:a + 1, :], e[b2:b2 + 1, :]
        s = ea + eb
        if best is None:
            best, ga, gb = s, ea, eb
            e1 = jnp.full_like(s, float(a))
            e2 = jnp.full_like(s, float(b2))
            cls = jnp.zeros_like(s)
        else:
            better = s > best
            best = jnp.where(better, s, best)
            ga = jnp.where(better, ea, ga)
            gb = jnp.where(better, eb, gb)
            e1 = jnp.where(better, float(a), e1)
            e2 = jnp.where(better, float(b2), e2)
            cls = jnp.where(better, float(c), cls)
    inv = 1.0 / best
    row = lax.broadcasted_iota(jnp.int32, (ROUTE_ROWS, tm), 0)
    rec = jnp.where(row == 0, ga * inv,
          jnp.where(row == 1, gb * inv,
          jnp.where(row == 2, e1,
          jnp.where(row == 3, e2,
          jnp.where(row == 4, cls, 0.0)))))
    route_ref[:, rows] = rec
    wide = jnp.concatenate([rec, jnp.zeros((LANES - ROUTE_ROWS, tm), jnp.float32)], axis=0)
    pay_ref[rows, HALF_D:] = pltpu.bitcast(wide.T, jnp.uint32)


def _outproj_call(h, mp, att, sg, w_out, layer, g1, n2, sh2, sc2, wr, br, tm):
    n = h.shape[0]
    assert n % tm == 0 and tm % OUTPROJ_CHAIN == 0
    row = lambda i: (i, 0)
    fixed = lambda i: (0, 0)
    vec = pl.BlockSpec((1, D_MODEL), fixed)
    return pl.pallas_call(
        _outproj_kernel,
        grid=(n // tm,),
        in_specs=[
            pl.BlockSpec((tm, D_MODEL), row),
            pl.BlockSpec((tm, POOL_DIM), row),
            pl.BlockSpec((tm, NA_DIM), row),
            pl.BlockSpec((tm, SG_DIM), row),
            pl.BlockSpec((None, D_MODEL, D_MODEL), lambda i: (layer, 0, 0)),
            vec, vec, vec, vec,
            pl.BlockSpec((2 * N_EXPERTS, D_MODEL), fixed),
            pl.BlockSpec((N_EXPERTS, 1), fixed),
        ],
        out_specs=[
            pl.BlockSpec((tm, D_MODEL), row),
            pl.BlockSpec((tm, PAYLOAD_W), row),
            pl.BlockSpec((ROUTE_ROWS, tm), lambda i: (0, i)),
        ],
        out_shape=[
            jax.ShapeDtypeStruct((n, D_MODEL), STREAM_DTYPE),
            jax.ShapeDtypeStruct((n, PAYLOAD_W), jnp.uint32),
            jax.ShapeDtypeStruct((ROUTE_ROWS, n), jnp.float32),
        ],
        compiler_params=_cparams(("arbitrary",)),
        name="outproj",
    )(h, mp, att, sg, w_out, g1, n2, sh2, sc2, wr, br)


def _payload_parts(pay_ref, rows=slice(None)):
    lo, hi = _unpack_halves(pay_ref[rows, 0:HALF_D])
    x = jnp.concatenate([lo, hi], axis=-1).astype(jnp.bfloat16)
    return x, pltpu.bitcast(pay_ref[rows, HALF_D:], jnp.float32)


def _expert_pair(x, ga, gb, wga, wua, wda, wgb, wub, wdb):
    ha = (_silu(_dot(x, wga)) * _dot(x, wua) * ga).astype(jnp.bfloat16)
    hb = (_silu(_dot(x, wgb)) * _dot(x, wub) * gb).astype(jnp.bfloat16)
    return _dot(ha, wda) + _dot(hb, wdb)


MOE_PART = MOE_TM // 8


def _moe_sorted_kernel(blk_ref, e1_ref, e2_ref, parts_ref, pay_ref,
                       wga_ref, wua_ref, wda_ref, wgb_ref, wub_ref, wdb_ref, o_ref):
    parts = parts_ref[pl.program_id(0)]

    def run(rows):
        x, route = _payload_parts(pay_ref, rows)
        y = _expert_pair(x, route[:, 0:1], route[:, 1:2], wga_ref[...], wua_ref[...], wda_ref[...],
                         wgb_ref[...], wub_ref[...], wdb_ref[...])
        o_ref[rows, :] = _pack_halves(y)

    for used in range(1, MOE_TM // MOE_PART + 1):
        @pl.when(parts == used)
        def _(used=used):
            run(slice(0, used * MOE_PART))


def _moe_sorted_call(blk, e1, e2, parts, pay_sorted, wg, wu, wd):
    n_tiles = blk.shape[0]
    rows = lambda i, blk, e1, e2, parts: (blk[i], 0)
    wa = lambda i, blk, e1, e2, parts: (e1[i], 0, 0)
    wb = lambda i, blk, e1, e2, parts: (e2[i], 0, 0)
    up = lambda m: pl.BlockSpec((None, D_MODEL, D_EXPERT), m)
    down = lambda m: pl.BlockSpec((None, D_EXPERT, D_MODEL), m)
    return pl.pallas_call(
        _moe_sorted_kernel,
        grid_spec=pltpu.PrefetchScalarGridSpec(
            num_scalar_prefetch=4,
            grid=(n_tiles,),
            in_specs=[pl.BlockSpec((MOE_TM, PAYLOAD_W), rows),
                      up(wa), up(wa), down(wa), up(wb), up(wb), down(wb)],
            out_specs=pl.BlockSpec((MOE_TM, HALF_D), rows),
        ),
        out_shape=jax.ShapeDtypeStruct((n_tiles * MOE_TM, HALF_D), jnp.uint32),
        compiler_params=_cparams(("arbitrary",)),
        name="moe_sorted",
    )(blk, e1, e2, parts, pay_sorted, wg, wu, wd, wg, wu, wd)


DENSE_EXPERTS_PER_STEP = 4


def _moe_dense_kernel(*refs, first_step, resume):
    if resume:
        pay_ref, wg_ref, wu_ref, wd_ref, acc_ref, o_ref = refs
    else:
        pay_ref, wg_ref, wu_ref, wd_ref, o_ref = refs
    step = pl.program_id(0)

    @pl.when(step == 0)
    def _():
        o_ref[...] = acc_ref[...] if resume else jnp.zeros_like(o_ref)

    x, route = _payload_parts(pay_ref)
    y = None
    for k in range(DENSE_EXPERTS_PER_STEP):
        ef = ((first_step + step) * DENSE_EXPERTS_PER_STEP + k).astype(jnp.float32)
        gate = (jnp.where(route[:, 2:3] == ef, route[:, 0:1], 0.0)
                + jnp.where(route[:, 3:4] == ef, route[:, 1:2], 0.0))
        he = (_silu(_dot(x, wg_ref[k])) * _dot(x, wu_ref[k]) * gate).astype(jnp.bfloat16)
        yk = _dot(he, wd_ref[k])
        y = yk if y is None else y + yk
    o_ref[...] += y


def _moe_dense_call(pay, wg, wu, wd, first_step, n_steps, acc=None):
    n = pay.shape[0]
    per = DENSE_EXPERTS_PER_STEP
    whole = pl.BlockSpec((n, D_MODEL), lambda e: (0, 0))
    return pl.pallas_call(
        functools.partial(_moe_dense_kernel, first_step=first_step, resume=acc is not None),
        grid=(n_steps,),
        in_specs=[pl.BlockSpec((n, PAYLOAD_W), lambda e: (0, 0)),
                  pl.BlockSpec((per, D_MODEL, D_EXPERT), lambda e: (first_step + e, 0, 0)),
                  pl.BlockSpec((per, D_MODEL, D_EXPERT), lambda e: (first_step + e, 0, 0)),
                  pl.BlockSpec((per, D_EXPERT, D_MODEL), lambda e: (first_step + e, 0, 0))]
                 + ([whole] if acc is not None else []),
        out_specs=whole,
        out_shape=jax.ShapeDtypeStruct((n, D_MODEL), jnp.float32),
        compiler_params=_cparams(("arbitrary",)),
        name="moe_dense",
    )(pay, wg, wu, wd, *([acc] if acc is not None else []))


def _moe_out(y_ref):
    if y_ref.dtype == jnp.uint32:
        return jnp.concatenate(_unpack_halves(y_ref[...]), axis=-1)
    return y_ref[...]


def _residual_kernel(h_ref, y_ref, g_ref, *rest):
    o_ref = rest[-1]
    o_ref[...] = h_ref[...].astype(jnp.float32) + g_ref[...] * _moe_out(y_ref)


def _residual_call(h, y, g, tm, first_row=0, into=None):
    n = h.shape[0]
    first = first_row // tm
    row = pl.BlockSpec((tm, D_MODEL), lambda i: (first + i, 0))
    in_specs = [row, pl.BlockSpec((tm, y.shape[1]), lambda i: (i, 0)),
                pl.BlockSpec((1, D_MODEL), lambda i: (0, 0))]
    args = [h, y, g]
    aliases = {}
    if into is not None:
        in_specs.append(pl.BlockSpec(memory_space=pl.ANY))
        args.append(into)
        aliases = {len(args) - 1: 0}
    return pl.pallas_call(
        _residual_kernel,
        grid=(y.shape[0] // tm,),
        in_specs=in_specs,
        out_specs=row,
        out_shape=jax.ShapeDtypeStruct((n, D_MODEL), jnp.float32),
        input_output_aliases=aliases,
        compiler_params=_cparams(("arbitrary",)),
        name="residual",
    )(*args)


SC_ROWS = 128


SC_CORES = 2
SC_SUBCORES = 16
SC_WORKERS = SC_CORES * SC_SUBCORES


def _sc_mesh():
    return plsc.VectorSubcoreMesh(core_axis_name="core", subcore_axis_name="subcore")


def _sc_worker():
    return lax.axis_index("subcore") * SC_CORES + lax.axis_index("core")


def _scatter_rows(x, dest, n_out, first_row=0):
    n, w = dest.shape[0], x.shape[1]

    per_worker = n // SC_WORKERS
    assert per_worker % SC_ROWS == 0 and first_row % SC_ROWS == 0

    @functools.partial(pl.kernel, out_type=jax.ShapeDtypeStruct((n_out, w), x.dtype), mesh=_sc_mesh(),
                       scratch_types=[pltpu.VMEM((SC_ROWS,), jnp.int32), pltpu.VMEM((SC_ROWS, w), x.dtype)])
    def scatter(x_hbm, i_hbm, o_hbm, idx_v, rows_v):
        first = _sc_worker() * per_worker

        @pl.loop(0, per_worker // SC_ROWS)
        def _(i):
            base = pl.multiple_of(first + i * SC_ROWS, SC_ROWS)
            pltpu.sync_copy(i_hbm.at[pl.ds(base, SC_ROWS)], idx_v)
            pltpu.sync_copy(x_hbm.at[pl.ds(first_row + base, SC_ROWS)], rows_v)
            pltpu.sync_copy(rows_v, o_hbm.at[idx_v])

    return scatter(x, dest)


def _gather_rows(x, src):
    n = src.shape[0]
    w = x.shape[1]

    per_worker = n // SC_WORKERS
    assert per_worker % SC_ROWS == 0

    @functools.partial(pl.kernel, out_type=jax.ShapeDtypeStruct((n, w), x.dtype), mesh=_sc_mesh(),
                       scratch_types=[pltpu.VMEM((SC_ROWS,), jnp.int32), pltpu.VMEM((SC_ROWS, w), x.dtype)])
    def gather(x_hbm, i_hbm, o_hbm, idx_v, rows_v):
        first = _sc_worker() * per_worker

        @pl.loop(0, per_worker // SC_ROWS)
        def _(i):
            base = pl.multiple_of(first + i * SC_ROWS, SC_ROWS)
            pltpu.sync_copy(i_hbm.at[pl.ds(base, SC_ROWS)], idx_v)
            pltpu.sync_copy(x_hbm.at[idx_v], rows_v)
            pltpu.sync_copy(rows_v, o_hbm.at[pl.ds(base, SC_ROWS)])

    return gather(x, src)


def _routing_plan(cls, n_tiles):
    onehot = (cls[:, None] == jnp.arange(N_CLASSES, dtype=jnp.int32)[None, :]).astype(jnp.int32)
    counts = jnp.sum(onehot, axis=0)
    rank = jnp.sum((jnp.cumsum(onehot, axis=0) - onehot) * onehot, axis=1)
    tiles = (counts + MOE_TM - 1) // MOE_TM
    tile_end = jnp.cumsum(tiles)
    tile_start = tile_end - tiles
    first_len = counts - (tiles - 1) * MOE_TM
    first_of = jnp.sum(onehot * first_len[None, :], axis=1)
    skip = jnp.where(rank >= first_of, MOE_TM - first_of, 0)
    dest = jnp.sum(onehot * tile_start[None, :], axis=1) * MOE_TM + rank + skip
    nact = tile_end[-1]
    tile = jnp.arange(n_tiles, dtype=jnp.int32)
    blk = jnp.minimum(tile, nact - 1)
    tile_cls = jnp.sum((blk[:, None] >= tile_end[None, :]).astype(jnp.int32), axis=1)
    e1 = jnp.asarray(CLASS_E1)[tile_cls]
    e2 = jnp.asarray(CLASS_E2)[tile_cls]
    valid = jnp.where(blk == tile_start[tile_cls], first_len[tile_cls], MOE_TM)
    parts = jnp.where(tile >= nact, 0, (valid + MOE_PART - 1) // MOE_PART)
    return dest.astype(jnp.int32), blk, e1, e2, parts.astype(jnp.int32)


def _row_tile(n, prefer=512):
    return next(t for t in (prefer, 512, 256) if n % t == 0)


def kernel(x, c, ctx, c_ctx, w_ada, b_ada, norm1, w_in, pool_w, pool_scale, q_norm, k_norm, rpb,
           sg_w, sg_b, sg_norm, w_out, norm2, w_router, b_router, w_gate, w_up, w_down):
    depth = w_ada.shape[0]
    n = x.shape[1]
    lc = ctx.shape[1]
    assert x.shape[0] == 1 and x.shape[2] == D_MODEL, "one sample of (tokens, 1024) features"
    assert n % (GRID_W * NA_ROWS_PER_BLOCK) == 0 and n % (2 * SC_WORKERS * SC_ROWS) == 0 and lc % 256 == 0
    bf = jnp.bfloat16
    lat_stream = (x[0],)
    h_ctx = ctx[0]

    cond = jnp.stack([c[0], c_ctx], axis=1)
    mod = _ada_call(cond, w_ada, b_ada)

    wr_t = w_router.T
    wr_hi = wr_t.astype(bf)
    wr_lo = (wr_t - wr_hi.astype(jnp.float32)).astype(bf)
    wr = jnp.concatenate([wr_hi, wr_lo], axis=0)
    br = b_router.reshape(N_EXPERTS, 1)
    n_tiles = n // MOE_TM + N_CLASSES
    w_in_bf = w_in.astype(bf)
    w_out_bf = w_out.astype(bf)

    for l in range(depth):
        last = l == depth - 1
        qg =(q_norm[l] * (HEAD_DIM ** -0.5 * LOG2_E)).reshape(1, NA_DIM)
        kg = k_norm[l].reshape(1, NA_DIM)
        sgn = sg_norm[l].reshape(1, SG_DIM)
        sgw = sg_w[l].astype(bf).reshape(SG_DIM // LANES, 2 * SG_CHUNK, SG_CHUNK)
        sgb = jnp.broadcast_to(sg_b[l].reshape(SG_DIM // LANES, 2 * SG_CHUNK, 1),
                               (SG_DIM // LANES, 2 * SG_CHUNK, LANES))
        pool_bd = jax.scipy.linalg.block_diag(*[pool_w[l, g] for g in range(len(POOL_WINDOWS))]).astype(bf)
        pscale = pool_scale[l].reshape(1, POOL_DIM)
        bias = _natten_bias(rpb[l])
        n1 = norm1[l].reshape(1, D_MODEL)
        n2 = norm2[l].reshape(1, D_MODEL)

        def mods(row):
            return [mod[l, row:row + 1, i * D_MODEL:(i + 1) * D_MODEL] for i in range(6)]

        sh1, sc1, g1, sh2, sc2, g2 = mods(0)
        csh1, csc1, cg1, csh2, csc2, cg2 = mods(1)

        tc = _row_tile(lc)
        mix_pool_c, q_c, k_c, v_c, sg_c = _inproj_call((h_ctx,), n1, csh1, csc1, w_in_bf, l, qg, kg,
                                                       sgn, sgw, sgb, pool_bd, pscale, tc, strips=False)

        outs = _inproj_call(lat_stream, n1, sh1, sc1, w_in_bf, l, qg, kg, sgn, sgw, sgb,
                            pool_bd, pscale, _row_tile(n, INPROJ_TILE), strips=True)
        mix_pool, q, k, v, sg = outs[:5]
        h_lat = outs[5] if len(lat_stream) == 3 else lat_stream[0]
        att, wg_l, wu_l, wd_l = _natten_call(q, k, v, k_c, v_c, bias, w_gate, w_up, w_down, l)
        h1, pay, route = _outproj_call(h_lat, mix_pool, att, sg, w_out_bf, l, g1, n2, sh2, sc2, wr, br,
                                       _row_tile(n, OUTPROJ_TILE))
        cls = route[4].astype(jnp.int32)

        if last:
            half = n // 2
            half_tiles = half // MOE_TM + N_CLASSES
            ys = []
            for first_row in (0, half):
                dest, blk, e1, e2, parts = _routing_plan(cls[first_row:first_row + half], half_tiles)
                pay_sorted = _scatter_rows(pay, dest, half_tiles * MOE_TM, first_row=first_row)
                y_sorted = _moe_sorted_call(blk, e1, e2, parts, pay_sorted, wg_l, wu_l, wd_l)
                ys.append(_gather_rows(y_sorted, dest))
            tr = _row_tile(half, RESIDUAL_TILE)
            out = _residual_call(h1, ys[0], g2, tr)
            out = _residual_call(h1, ys[1], g2, tr, first_row=half, into=out)
            return out[None]

        dest, blk, e1, e2, parts = _routing_plan(cls, n_tiles)
        pay_sorted = _scatter_rows(pay, dest, n_tiles * MOE_TM)

        dense_steps = N_EXPERTS // DENSE_EXPERTS_PER_STEP
        att_c = _ctxatt_call(q_c, k_c, v_c)
        h1_c, pay_c, _ = _outproj_call(h_ctx, mix_pool_c, att_c, sg_c, w_out_bf, l, cg1, n2, csh2, csc2,
                                       wr, br, tc)
        y_c = _moe_dense_call(pay_c, wg_l, wu_l, wd_l, 0, dense_steps // 2)
        parts, y_c = lax.optimization_barrier((parts, y_c))

        y_sorted = _moe_sorted_call(blk, e1, e2, parts, pay_sorted, wg_l, wu_l, wd_l)
        lat_stream = (h1, _gather_rows(y_sorted, dest), g2)

        y_c = _moe_dense_call(pay_c, wg_l, wu_l, wd_l, dense_steps // 2, dense_steps - dense_steps // 2,
                              acc=y_c)
        h_ctx = _residual_call(h1_c, y_c, cg2, tc)

    raise AssertionError("at least one layer")
```

```python
import functools

import jax
import jax.numpy as jnp
import numpy as np
from jax import lax
from jax.experimental import pallas as pl
from jax.experimental.pallas import tpu as pltpu
from jax.experimental.pallas import tpu_sc as plsc

D_MODEL = 1024
GRID_W = 64
HEAD_DIM = 64
POOL_WINDOWS = (2, 4, 8, 16)
POOL_DIM = 256
NA_HEADS = 8
NA_DIM = 512
NA_WIN_ROWS = 8
NA_WIN_COLS = 16
SG_DIM = 256
SG_CHUNK = 128
Q_OFF = POOL_DIM
K_OFF = Q_OFF + NA_DIM
V_OFF = K_OFF + NA_DIM
U_OFF = V_OFF + NA_DIM
G_OFF = U_OFF + SG_DIM
IN_DIM = G_OFF + SG_DIM
N_EXPERTS = 16
GROUP_SIZE = 4
D_EXPERT = 512
EPS = 1e-6

LANES = 128
SUBLANES = 8
HEAD_PAIRS = NA_DIM // LANES
VMEM_LIMIT = 48 * 1024 * 1024

PAIRS = ((0, 1), (0, 2), (1, 2), (1, 3), (0, 3), (2, 3))
N_GROUPS = N_EXPERTS // GROUP_SIZE
N_CLASSES = N_GROUPS * len(PAIRS)
CLASS_E1 = np.array([GROUP_SIZE * g + i for g in range(N_GROUPS) for (i, j) in PAIRS], np.int32)
CLASS_E2 = np.array([GROUP_SIZE * g + j for g in range(N_GROUPS) for (i, j) in PAIRS], np.int32)

ROUTE_ROWS = 8
HALF_D = D_MODEL // 2
PAYLOAD_W = HALF_D + LANES
MOE_TM = 512
NEG_BIG = -1e30
LOG2_E = 1.4426950408889634


def _cparams(sem):
    return pltpu.CompilerParams(dimension_semantics=sem, vmem_limit_bytes=VMEM_LIMIT)


def _dot(a, b):
    return jnp.dot(a, b, preferred_element_type=jnp.float32)


def _dot_nt(a, b):
    return lax.dot_general(a, b, (((1,), (1,)), ((), ())), preferred_element_type=jnp.float32)


def _gelu_tanh(x):
    return 0.5 * x * (1.0 + jnp.tanh(0.7978845608028654 * (x + 0.044715 * (x * x * x))))


def _silu(x):
    return x * (1.0 / (1.0 + jnp.exp(-x)))


def _pack_halves(x):
    w = x.shape[1] // 2
    lo = pltpu.bitcast(x[:, :w].astype(jnp.bfloat16).astype(jnp.float32), jnp.uint32) >> 16
    hi = pltpu.bitcast(x[:, w:].astype(jnp.bfloat16).astype(jnp.float32), jnp.uint32) & jnp.uint32(0xFFFF0000)
    return lo | hi


def _unpack_halves(words):
    lo = pltpu.bitcast(words << 16, jnp.float32)
    hi = pltpu.bitcast(words & jnp.uint32(0xFFFF0000), jnp.float32)
    return lo, hi


def _ada_kernel(cond_ref, w_ref, b_ref, o_ref):
    cond = _silu(cond_ref[...])
    w = w_ref[...]
    rows = [jnp.sum(w * cond[:, r:r + 1], axis=0, keepdims=True) + b_ref[...] for r in range(2)]
    o_ref[...] = jnp.concatenate(rows + [jnp.zeros((SUBLANES - 2, w.shape[1]), jnp.float32)], axis=0)


def _ada_call(cond, w_ada, b_ada):
    depth = w_ada.shape[0]
    tn = 1536
    return pl.pallas_call(
        _ada_kernel,
        grid=(depth, 6 * D_MODEL // tn),
        in_specs=[
            pl.BlockSpec((D_MODEL, 2), lambda l, j: (0, 0)),
            pl.BlockSpec((None, D_MODEL, tn), lambda l, j: (l, 0, j)),
            pl.BlockSpec((None, 1, tn), lambda l, j: (l, 0, j)),
        ],
        out_specs=pl.BlockSpec((None, SUBLANES, tn), lambda l, j: (l, 0, j)),
        out_shape=jax.ShapeDtypeStruct((depth, SUBLANES, 6 * D_MODEL), jnp.float32),
        compiler_params=_cparams(("arbitrary", "arbitrary")),
        name="adaln",
    )(cond, w_ada, b_ada.reshape(depth, 1, 6 * D_MODEL))


def _norm_modulate(x, n_ref, sh_ref, sc_ref):
    ms = jnp.mean(x * x, axis=-1, keepdims=True)
    return ((x * lax.rsqrt(ms + EPS)) * (n_ref[...] * (1.0 + sc_ref[...])) + sh_ref[...]).astype(jnp.bfloat16)


def _head_rms_scale(a):
    low = lax.broadcasted_iota(jnp.int32, (a.shape[0], LANES), 1) < HEAD_DIM
    blocks = []
    for p in range(a.shape[1] // LANES):
        sq = jnp.square(a[:, p * LANES:(p + 1) * LANES])
        s_lo = jnp.sum(jnp.where(low, sq, 0.0), axis=-1, keepdims=True)
        s_hi = jnp.sum(jnp.where(low, 0.0, sq), axis=-1, keepdims=True)
        blocks.append(lax.rsqrt(jnp.where(low, s_lo, s_hi) * (1.0 / HEAD_DIM) + EPS))
    return jnp.concatenate(blocks, axis=-1)


STRIP_W = 8
N_STRIPS = GRID_W // STRIP_W


def _store_keys(ref, x):
    if len(ref.shape) == 2:
        ref[...] = x.astype(jnp.bfloat16)
        return
    pair = 2 * STRIP_W
    for s in range(N_STRIPS):
        for rp in range(x.shape[0] // (2 * GRID_W)):
            top = 2 * rp * GRID_W + s * STRIP_W
            rows = jnp.concatenate([x[top:top + STRIP_W], x[top + GRID_W:top + GRID_W + STRIP_W]], axis=0)
            ref[s, rp * pair:(rp + 1) * pair, :] = rows.astype(jnp.bfloat16)


POOL_HALO = 8
HALO_BLOCK = 16
STREAM_DTYPE = jnp.bfloat16


POOL_EDGE = 16


def _pool_mix(xe_ref, w_ref, scale_ref, tm, seq_len):
    low = lax.broadcasted_iota(jnp.int32, (tm, LANES), 1) < HEAD_DIM
    t_edge = pl.program_id(0) * tm + lax.broadcasted_iota(jnp.int32, (POOL_EDGE, LANES), 0)

    def window_mean(s, half):
        mean = s * (1.0 / (2 * half))

        def clip_fix(t):
            count = (jnp.minimum(t + half, seq_len) - jnp.maximum(t - half, 0)).astype(jnp.float32)
            return (2.0 * half) / count

        return jnp.concatenate([mean[:POOL_EDGE] * clip_fix(t_edge),
                                mean[POOL_EDGE:tm - POOL_EDGE],
                                mean[tm - POOL_EDGE:] * clip_fix(t_edge + (tm - POOL_EDGE))], axis=0)

    def window_sums(xs, n_levels):
        sums = []
        s = xs
        for k in range(n_levels):
            step = 1 << k
            s = s[:-step] + s[step:]
            sums.append(s)
        return sums

    outs = []
    for half_block, windows in enumerate(((2, 4), (8, 16))):
        xs = xe_ref[:, half_block * LANES:(half_block + 1) * LANES]
        sums = window_sums(xs, int(np.log2(windows[1])))
        parts = []
        for w in windows:
            half = w // 2
            s = sums[int(np.log2(w)) - 1][POOL_HALO - half:POOL_HALO - half + tm]
            parts.append(window_mean(s, half))
        mean = jnp.where(low, parts[0], parts[1])
        outs.append(mean - xs[POOL_HALO:POOL_HALO + tm])
    d = jnp.concatenate(outs, axis=-1).astype(jnp.bfloat16)
    return (_dot(d, w_ref[...]) * scale_ref[...]).astype(jnp.bfloat16)


def _inproj_kernel(*refs, pending, seq_len):
    n_stream = 7 if pending else 3
    stream, refs = refs[:n_stream], refs[n_stream:]
    if pending:
        h_ref, y_ref, g_ref, hp_ref, hn_ref, yp_ref, yn_ref = stream
        hres_ref, refs = refs[-2], refs[:-2] + refs[-1:]
        x = h_ref[...].astype(jnp.float32) + g_ref[...] * _moe_out(y_ref)
        hres_ref[...] = x.astype(hres_ref.dtype)
        before = hp_ref[...].astype(jnp.float32) + g_ref[...] * _moe_out(yp_ref)
        after = hn_ref[...].astype(jnp.float32) + g_ref[...] * _moe_out(yn_ref)
    else:
        h_ref, hp_ref, hn_ref = stream
        x = h_ref[...].astype(jnp.float32)
        before = hp_ref[...].astype(jnp.float32)
        after = hn_ref[...].astype(jnp.float32)
    x_halo = jnp.concatenate([before[HALO_BLOCK - POOL_HALO:], after[:POOL_HALO]], axis=0)
    (n1_ref, sh_ref, sc_ref, w_ref, qg_ref, kg_ref, sgn_ref, sgw_ref, sgb_ref,
     pw_ref, ps_ref, pool_ref, q_ref, k_ref, v_ref, sg_ref, xe_ref) = refs
    tm = h_ref.shape[0]
    i = pl.program_id(0)
    hn = _norm_modulate(x, n1_ref, sh_ref, sc_ref)
    hn_halo = _norm_modulate(x_halo, n1_ref, sh_ref, sc_ref)

    a_halo = _dot(hn_halo, w_ref[:, 0:Q_OFF])
    xe_ref[0:POOL_HALO, :] = jnp.where(i > 0, a_halo[:POOL_HALO], 0.0)
    xe_ref[POOL_HALO:POOL_HALO + tm, :] = _dot(hn, w_ref[:, 0:Q_OFF])
    xe_ref[POOL_HALO + tm:, :] = jnp.where(i < pl.num_programs(0) - 1, a_halo[POOL_HALO:], 0.0)

    a_g = _dot(hn, w_ref[:, G_OFF:IN_DIM])
    a_u = _dot(hn, w_ref[:, U_OFF:G_OFF])
    a_q = _dot(hn, w_ref[:, Q_OFF:K_OFF])
    a_k = _dot(hn, w_ref[:, K_OFF:V_OFF])
    _store_keys(v_ref, _dot(hn, w_ref[:, V_OFF:U_OFF]))

    gv = _gelu_tanh(a_g)
    q_ref[...] = (a_q * _head_rms_scale(a_q) * qg_ref[...]).astype(jnp.bfloat16)
    _store_keys(k_ref, a_k * _head_rms_scale(a_k) * kg_ref[...])

    u = _gelu_tanh(a_u)
    vn = (gv * _head_rms_scale(gv) * sgn_ref[...]).astype(jnp.bfloat16)
    low = lax.broadcasted_iota(jnp.int32, (SG_CHUNK, LANES), 1) < HEAD_DIM
    for c in range(tm // SG_CHUNK):
        rows = slice(c * SG_CHUNK, (c + 1) * SG_CHUNK)
        for s in range(SG_DIM // LANES):
            cols = slice(s * LANES, (s + 1) * LANES)
            m = _dot(sgw_ref[s], vn[rows, cols]) + sgb_ref[s]
            mixed = jnp.where(low, m[:SG_CHUNK], m[SG_CHUNK:])
            sg_ref[rows, cols] = (u[rows, cols] * mixed).astype(jnp.bfloat16)

    pool_ref[...] = _pool_mix(xe_ref, pw_ref, ps_ref, tm, seq_len)


def _inproj_call(stream, n1, sh, sc, w_in, layer, qg, kg, sgn, sgw, sgb, pool_w, pool_scale, tm, strips):
    pending = len(stream) == 3
    n = stream[0].shape[0]
    assert n % tm == 0 and tm % SG_CHUNK == 0 and tm % (2 * GRID_W) == 0
    if strips:
        kv_spec = pl.BlockSpec((N_STRIPS, tm // N_STRIPS, NA_DIM), lambda i: (0, i, 0))
        kv_shape = jax.ShapeDtypeStruct((N_STRIPS, n // N_STRIPS, NA_DIM), jnp.bfloat16)
    else:
        kv_spec = pl.BlockSpec((tm, NA_DIM), lambda i: (i, 0))
        kv_shape = jax.ShapeDtypeStruct((n, NA_DIM), jnp.bfloat16)
    row = lambda i: (i, 0)
    fixed2 = lambda i: (0, 0)
    fixed3 = lambda i: (0, 0, 0)
    vec = lambda w: pl.BlockSpec((1, w), fixed2)
    rows = pl.BlockSpec((tm, D_MODEL), row)
    per_tile = tm // HALO_BLOCK
    before = lambda i: (jnp.maximum(i * per_tile - 1, 0), 0)
    after = lambda i: (jnp.minimum((i + 1) * per_tile, n // HALO_BLOCK - 1), 0)
    halo = lambda w, m: pl.BlockSpec((HALO_BLOCK, w), m)
    h = stream[0]
    if pending:
        y, g = stream[1], stream[2]
        yw = y.shape[1]
        args = [h, y, g, h, h, y, y]
        stream_specs = [rows, pl.BlockSpec((tm, yw), row), vec(D_MODEL),
                        halo(D_MODEL, before), halo(D_MODEL, after), halo(yw, before), halo(yw, after)]
    else:
        args = [h, h, h]
        stream_specs = [rows, halo(D_MODEL, before), halo(D_MODEL, after)]
    extra_out_specs = [rows] if pending else []
    extra_out_shape = [jax.ShapeDtypeStruct((n, D_MODEL), STREAM_DTYPE)] if pending else []
    return pl.pallas_call(
        functools.partial(_inproj_kernel, pending=pending, seq_len=n),
        grid=(n // tm,),
        in_specs=stream_specs + [
            vec(D_MODEL), vec(D_MODEL), vec(D_MODEL),
            pl.BlockSpec((None, D_MODEL, IN_DIM), lambda i: (layer, 0, 0)),
            vec(NA_DIM), vec(NA_DIM),
            vec(SG_DIM),
            pl.BlockSpec((SG_DIM // LANES, 2 * SG_CHUNK, SG_CHUNK), fixed3),
            pl.BlockSpec((SG_DIM // LANES, 2 * SG_CHUNK, LANES), fixed3),
            pl.BlockSpec((POOL_DIM, POOL_DIM), fixed2),
            vec(POOL_DIM),
        ],
        out_specs=[
            pl.BlockSpec((tm, POOL_DIM), row),
            pl.BlockSpec((tm, NA_DIM), row),
            kv_spec,
            kv_spec,
            pl.BlockSpec((tm, SG_DIM), row),
        ] + extra_out_specs,
        out_shape=[
            jax.ShapeDtypeStruct((n, POOL_DIM), jnp.bfloat16),
            jax.ShapeDtypeStruct((n, NA_DIM), jnp.bfloat16),
            kv_shape,
            kv_shape,
            jax.ShapeDtypeStruct((n, SG_DIM), jnp.bfloat16),
        ] + extra_out_shape,
        scratch_shapes=[pltpu.VMEM((tm + 2 * POOL_HALO, POOL_DIM), jnp.float32)],
        compiler_params=_cparams(("arbitrary",)),
        name="inproj",
    )(*args, n1, sh, sc, w_in, qg, kg, sgn, sgw, sgb, pool_w, pool_scale)


NA_ROWS_PER_BLOCK = 64
NA_GROUP_ROWS = 4
NA_WINDOW_ROWS = NA_GROUP_ROWS + NA_WIN_ROWS
NA_BLOCK = NA_ROWS_PER_BLOCK * GRID_W
NA_GROUP = NA_GROUP_ROWS * GRID_W
NA_HALF_COLS = GRID_W // 2
NA_CHAIN = NA_GROUP_ROWS * NA_HALF_COLS
NA_HALF_STRIPS = 5
NA_HALF_COL0 = (0, GRID_W - NA_HALF_STRIPS * STRIP_W)
NA_RUN = NA_WINDOW_ROWS * STRIP_W
NA_LOCAL = 512
NA_STRIP_BLOCK = NA_ROWS_PER_BLOCK * STRIP_W
NA_STRIP_HALO = (NA_WIN_ROWS // 2) * STRIP_W
NA_EDGE_FIRST, NA_EDGE_NONE, NA_EDGE_LAST = 0, 1, 2


def _stack_heads(x, low):
    zero = jnp.zeros_like(x)
    return jnp.concatenate([jnp.where(low, x, zero), jnp.where(low, zero, x)], axis=0)


def _natten_kernel(q_ref, kw_ref, vp_ref, vc_ref, vn_ref, kx_ref, vx_ref, bias_ref,
                   wg_ref, wu_ref, wd_ref,
                   o_ref, wg_bf_ref, wu_bf_ref, wd_bf_ref, vwin_ref, vxe_ref, *, grid_rows):
    b = pl.program_id(1)
    wg_bf_ref[...] = wg_ref[...].astype(jnp.bfloat16)
    wu_bf_ref[...] = wu_ref[...].astype(jnp.bfloat16)
    wd_bf_ref[...] = wd_ref[...].astype(jnp.bfloat16)
    top, bottom = NA_STRIP_HALO, NA_STRIP_HALO + NA_STRIP_BLOCK
    k_first = jnp.clip(b * NA_STRIP_BLOCK - NA_STRIP_HALO, 0, grid_rows * STRIP_W - kw_ref.shape[1])
    vwin_ref[:, 0:top, 0:LANES] = vp_ref[...]
    vwin_ref[:, top:bottom, 0:LANES] = vc_ref[...]
    vwin_ref[:, bottom:, 0:LANES] = vn_ref[...]
    vwin_ref[:, :, LANES:] = jnp.ones(vwin_ref.shape[:2] + (LANES,), jnp.bfloat16)
    vxe_ref[:, 0:LANES] = vx_ref[...]
    vxe_ref[:, LANES:] = jnp.ones((vxe_ref.shape[0], LANES), jnp.bfloat16)
    low_q = lax.broadcasted_iota(jnp.int32, (NA_CHAIN, LANES), 1) < HEAD_DIM
    n_pad = NA_LOCAL - NA_HALF_STRIPS * NA_RUN

    def window_start(g):
        r0 = b * NA_ROWS_PER_BLOCK + g * NA_GROUP_ROWS
        ws = jnp.clip(r0 - NA_WIN_ROWS // 2, 0, grid_rows - NA_WINDOW_ROWS)
        edge = jnp.where(r0 == 0, NA_EDGE_FIRST,
                         jnp.where(r0 == grid_rows - NA_GROUP_ROWS, NA_EDGE_LAST, NA_EDGE_NONE))
        start = pl.multiple_of((ws - b * NA_ROWS_PER_BLOCK + NA_WIN_ROWS // 2) * STRIP_W, NA_STRIP_HALO)
        return start, edge

    def key_start(g):
        r0 = b * NA_ROWS_PER_BLOCK + g * NA_GROUP_ROWS
        ws = jnp.clip(r0 - NA_WIN_ROWS // 2, 0, grid_rows - NA_WINDOW_ROWS)
        return pl.multiple_of(ws * STRIP_W - k_first, NA_STRIP_HALO)

    def local_window(win_ref, start, half):
        s0 = NA_HALF_COL0[half] // STRIP_W
        runs = [win_ref[s, pl.ds(start, NA_RUN), :] for s in range(s0, s0 + NA_HALF_STRIPS)]
        return jnp.concatenate(runs + [jnp.zeros((n_pad, win_ref.shape[2]), jnp.bfloat16)], axis=0)

    def query_rows(g, half, j):
        first = g * NA_GROUP + j * GRID_W + half * NA_HALF_COLS
        return slice(first, first + NA_HALF_COLS)

    def scores(c):
        g, half = divmod(c, 2)
        _, edge = window_start(g)
        qh = jnp.concatenate([q_ref[query_rows(g, half, j), :] for j in range(NA_GROUP_ROWS)], axis=0)
        lhs = _stack_heads(qh, low_q)
        kl = local_window(kw_ref, key_start(g), half)
        return jnp.concatenate([_dot_nt(lhs, kl).astype(jnp.bfloat16) + bias_ref[edge, half],
                                _dot_nt(lhs, kx_ref[...]).astype(jnp.bfloat16)], axis=-1)

    n_chains = 2 * (NA_ROWS_PER_BLOCK // NA_GROUP_ROWS)
    s_next = scores(0)
    for c in range(n_chains):
        s = s_next
        if c + 1 < n_chains:
            s_next = scores(c + 1)
        g, half = divmod(c, 2)
        vl = local_window(vwin_ref, window_start(g)[0], half)
        m = jnp.max(s, axis=-1, keepdims=True)
        pb = jnp.exp2(s - m)
        o = _dot(pb[:, :NA_LOCAL], vl) + _dot(pb[:, NA_LOCAL:], vxe_ref[...])
        o = o[:, :LANES] * (1.0 / o[:, LANES:])
        o = jnp.where(low_q, o[:NA_CHAIN], o[NA_CHAIN:]).astype(jnp.bfloat16)
        for j in range(NA_GROUP_ROWS):
            o_ref[query_rows(g, half, j), :] = o[j * NA_HALF_COLS:(j + 1) * NA_HALF_COLS]


def _natten_call(q, k, v, k_ctx, v_ctx, bias, w_gate, w_up, w_down, layer):
    n = q.shape[0]
    grid_rows = n // GRID_W
    assert grid_rows % NA_ROWS_PER_BLOCK == 0 and grid_rows >= 2 * NA_ROWS_PER_BLOCK
    nblk = n // NA_BLOCK
    steps = HEAD_PAIRS * nblk
    depth = w_gate.shape[0]
    up_rows = N_EXPERTS * D_MODEL
    down_rows = N_EXPERTS * D_EXPERT
    assert up_rows % steps == 0 and down_rows % steps == 0
    wg2 = w_gate.reshape(depth * up_rows, D_EXPERT)
    wu2 = w_up.reshape(depth * up_rows, D_EXPERT)
    wd2 = w_down.reshape(depth * down_rows, D_MODEL)
    up_in = pl.BlockSpec((up_rows // steps, D_EXPERT), lambda p, b: (layer * steps + p * nblk + b, 0))
    down_in = pl.BlockSpec((down_rows // steps, D_MODEL), lambda p, b: (layer * steps + p * nblk + b, 0))
    up_out = pl.BlockSpec((up_rows // steps, D_EXPERT), lambda p, b: (p * nblk + b, 0))
    down_out = pl.BlockSpec((down_rows // steps, D_MODEL), lambda p, b: (p * nblk + b, 0))
    n_halo = n // N_STRIPS // NA_STRIP_HALO
    hb = NA_STRIP_BLOCK // NA_STRIP_HALO
    rows = pl.BlockSpec((NA_BLOCK, LANES), lambda p, b: (b, p))
    cur = pl.BlockSpec((N_STRIPS, NA_STRIP_BLOCK, LANES), lambda p, b: (0, b, p))
    prev = pl.BlockSpec((N_STRIPS, NA_STRIP_HALO, LANES), lambda p, b: (0, jnp.maximum(b * hb - 1, 0), p))
    nxt = pl.BlockSpec((N_STRIPS, NA_STRIP_HALO, LANES),
                       lambda p, b: (0, jnp.minimum((b + 1) * hb, n_halo - 1), p))
    ctx = pl.BlockSpec((k_ctx.shape[0], LANES), lambda p, b: (0, p))
    win_rows = NA_STRIP_BLOCK + 2 * NA_STRIP_HALO
    strip_len = n // N_STRIPS
    kwin = pl.BlockSpec(
        (pl.Element(N_STRIPS), pl.Element(win_rows), pl.Element(LANES)),
        lambda p, b: (0, pl.multiple_of(jnp.clip(b * NA_STRIP_BLOCK - NA_STRIP_HALO, 0, strip_len - win_rows),
                                        NA_STRIP_HALO), pl.multiple_of(p * LANES, LANES)))
    att, wg_bf, wu_bf, wd_bf = pl.pallas_call(
        functools.partial(_natten_kernel, grid_rows=grid_rows),
        grid=(HEAD_PAIRS, nblk),
        in_specs=[rows, kwin, prev, cur, nxt, ctx, ctx,
                  pl.BlockSpec((None, 3, 2, 2 * NA_CHAIN, NA_LOCAL), lambda p, b: (p, 0, 0, 0, 0)),
                  up_in, up_in, down_in],
        out_specs=[rows, up_out, up_out, down_out],
        out_shape=[jax.ShapeDtypeStruct((n, NA_DIM), jnp.bfloat16),
                   jax.ShapeDtypeStruct((up_rows, D_EXPERT), jnp.bfloat16),
                   jax.ShapeDtypeStruct((up_rows, D_EXPERT), jnp.bfloat16),
                   jax.ShapeDtypeStruct((down_rows, D_MODEL), jnp.bfloat16)],
        scratch_shapes=[pltpu.VMEM((N_STRIPS, win_rows, 2 * LANES), jnp.bfloat16),
                        pltpu.VMEM((k_ctx.shape[0], 2 * LANES), jnp.bfloat16)],
        compiler_params=_cparams(("arbitrary", "arbitrary")),
        name="natten",
    )(q, k, v, v, v, k_ctx, v_ctx, bias, wg2, wu2, wd2)
    return (att, wg_bf.reshape(N_EXPERTS, D_MODEL, D_EXPERT), wu_bf.reshape(N_EXPERTS, D_MODEL, D_EXPERT),
            wd_bf.reshape(N_EXPERTS, D_EXPERT, D_MODEL))


def _natten_bias(rpb):
    cols = np.arange(GRID_W)
    col_start = np.clip(cols - NA_WIN_COLS // 2, 0, GRID_W - NA_WIN_COLS)
    kc = np.arange(GRID_W)
    in_win = (kc[None, :] >= col_start[:, None]) & (kc[None, :] < col_start[:, None] + NA_WIN_COLS)
    dc = kc[None, :] - cols[:, None] + NA_WIN_COLS - 1
    sel = (np.arange(2 * NA_WIN_COLS - 1)[:, None, None] == dc[None]) & in_win[None]
    t2 = jnp.einsum("hdj,jqk->hdqk", rpb, jnp.asarray(sel, jnp.float32), precision=lax.Precision.HIGHEST)
    t2 = jnp.where(in_win[None, None], t2 * LOG2_E, NEG_BIG)
    neg = jnp.full((NA_HEADS, 1, GRID_W, GRID_W), NEG_BIG, jnp.float32)
    t2e = jnp.concatenate([neg, t2, neg], axis=1)
    u = jnp.concatenate([t2e[:, :-1], t2e[:, 1:]], axis=-1)
    u = u.reshape(HEAD_PAIRS, 2, 2 * NA_WIN_ROWS, GRID_W, LANES)

    place = np.zeros((2, NA_WINDOW_ROWS * GRID_W, NA_LOCAL), np.float32)
    for half, c0 in enumerate(NA_HALF_COL0):
        for a in range(NA_WINDOW_ROWS):
            for kcol in range(c0, c0 + NA_HALF_STRIPS * STRIP_W):
                s, c8 = divmod(kcol - c0, STRIP_W)
                place[half, a * GRID_W + kcol, s * NA_RUN + a * STRIP_W + c8] = 1.0
    outside = np.full((3, NA_GROUP_ROWS, NA_LOCAL), NEG_BIG, np.float32)
    for edge in (NA_EDGE_FIRST, NA_EDGE_NONE, NA_EDGE_LAST):
        for j in range(NA_GROUP_ROWS):
            lo, _ = _window_rows(edge, j)
            for s in range(NA_HALF_STRIPS):
                outside[edge, j, s * NA_RUN + lo * STRIP_W:s * NA_RUN + (lo + NA_WIN_ROWS) * STRIP_W] = 0.0
    return pl.pallas_call(
        _bias_expand_kernel,
        grid=(HEAD_PAIRS,),
        in_specs=[pl.BlockSpec((None, 2, 2 * NA_WIN_ROWS, GRID_W, LANES), lambda p: (p, 0, 0, 0, 0)),
                  pl.BlockSpec(place.shape, lambda p: (0, 0, 0)),
                  pl.BlockSpec(outside.shape, lambda p: (0, 0, 0))],
        out_specs=pl.BlockSpec((None, 3, 2, 2 * NA_CHAIN, NA_LOCAL), lambda p: (p, 0, 0, 0, 0)),
        out_shape=jax.ShapeDtypeStruct((HEAD_PAIRS, 3, 2, 2 * NA_CHAIN, NA_LOCAL), jnp.bfloat16),
        compiler_params=_cparams(("arbitrary",)),
        name="bias_expand",
    )(u, jnp.asarray(place, jnp.bfloat16), jnp.asarray(outside))


def _window_rows(edge, j):
    if edge == NA_EDGE_FIRST:
        return 0, NA_WIN_ROWS - 1 - j
    if edge == NA_EDGE_NONE:
        return j, NA_WIN_ROWS // 2 - 1
    return NA_WINDOW_ROWS - NA_WIN_ROWS, NA_WIN_ROWS // 2 - 1 - j


def _bias_expand_kernel(u_ref, place_ref, outside_ref, o_ref):
    low = lax.broadcasted_iota(jnp.int32, (NA_HALF_COLS, LANES), 1) < GRID_W
    zero = jnp.zeros((NA_HALF_COLS, LANES), jnp.float32)
    for edge in (NA_EDGE_FIRST, NA_EDGE_NONE, NA_EDGE_LAST):
        for half in range(2):
            q0 = half * NA_HALF_COLS
            blocks, masks = [], []
            for hd in range(2):
                for j in range(NA_GROUP_ROWS):
                    lo, base = _window_rows(edge, j)
                    tiles = []
                    for i in range(NA_WINDOW_ROWS // 2):
                        a0, a1 = 2 * i, 2 * i + 1
                        ok0 = lo <= a0 < lo + NA_WIN_ROWS
                        ok1 = lo <= a1 < lo + NA_WIN_ROWS
                        if not (ok0 or ok1):
                            tile = zero
                        else:
                            tile = u_ref[hd, base + a1 - lo, q0:q0 + NA_HALF_COLS, :]
                            if not ok0:
                                tile = jnp.where(low, zero, tile)
                            if not ok1:
                                tile = jnp.where(low, tile, zero)
                        tiles.append(tile)
                    blocks.append(jnp.concatenate(tiles, axis=-1))
                    masks.append(jnp.broadcast_to(outside_ref[edge, j:j + 1, :], (NA_HALF_COLS, NA_LOCAL)))
            lhs = jnp.concatenate(blocks, axis=0).astype(jnp.bfloat16)
            placed = _dot(lhs, place_ref[half]) + jnp.concatenate(masks, axis=0)
            o_ref[edge, half] = placed.astype(o_ref.dtype)


def _ctxatt_kernel(q_ref, k_ref, v_ref, o_ref):
    lc = q_ref.shape[0]
    low = lax.broadcasted_iota(jnp.int32, (lc, LANES), 1) < HEAD_DIM
    lhs = _stack_heads(q_ref[...], low)
    s = _dot_nt(lhs, k_ref[...])
    m = jnp.max(s, axis=-1, keepdims=True)
    p = jnp.exp2(s - m)
    denom = jnp.sum(p, axis=-1, keepdims=True)
    o = _dot(p.astype(jnp.bfloat16), v_ref[...]) * (1.0 / denom)
    o_ref[...] = jnp.where(low, o[:lc], o[lc:]).astype(jnp.bfloat16)


def _ctxatt_call(q, k, v):
    lc = q.shape[0]
    spec = pl.BlockSpec((lc, LANES), lambda p: (0, p))
    return pl.pallas_call(
        _ctxatt_kernel,
        grid=(HEAD_PAIRS,),
        in_specs=[spec, spec, spec],
        out_specs=spec,
        out_shape=jax.ShapeDtypeStruct((lc, NA_DIM), jnp.bfloat16),
        compiler_params=_cparams(("arbitrary",)),
        name="ctxatt",
    )(q, k, v)


OUTPROJ_CHAIN = 256
OUTPROJ_TILE = 1024
INPROJ_TILE = 1024
RESIDUAL_TILE = 2048


def _outproj_kernel(h_ref, mp_ref, att_ref, sg_ref, wo_ref, g1_ref, n2_ref, sh_ref, sc_ref,
                    wr_ref, br_ref, h1_ref, pay_ref, route_ref):
    chains = [slice(c * OUTPROJ_CHAIN, (c + 1) * OUTPROJ_CHAIN) for c in range(h_ref.shape[0] // OUTPROJ_CHAIN)]
    h1s = []
    for rows in chains:
        mix = (_dot(mp_ref[rows, :], wo_ref[0:POOL_DIM, :])
               + _dot(att_ref[rows, :], wo_ref[POOL_DIM:POOL_DIM + NA_DIM, :])
               + _dot(sg_ref[rows, :], wo_ref[POOL_DIM + NA_DIM:, :]))
        h1 = h_ref[rows, :].astype(jnp.float32) + g1_ref[...] * mix
        h1_ref[rows, :] = h1.astype(h1_ref.dtype)
        h1s.append(h1)
    for rows, h1 in zip(chains, h1s):
        _outproj_route(rows, h1, n2_ref, sh_ref, sc_ref, wr_ref, br_ref, pay_ref, route_ref)


def _outproj_route(rows, h1, n2_ref, sh_ref, sc_ref, wr_ref, br_ref, pay_ref, route_ref):
    tm = OUTPROJ_CHAIN
    ms = jnp.mean(h1 * h1, axis=-1, keepdims=True)
    hm = (h1 * lax.rsqrt(ms + EPS)) * (n2_ref[...] * (1.0 + sc_ref[...])) + sh_ref[...]
    pay_ref[rows, 0:HALF_D] = _pack_halves(hm)

    hm_hi = hm.astype(jnp.bfloat16)
    lt = _dot_nt(wr_ref[...], hm_hi)
    logits = lt[:N_EXPERTS] + lt[N_EXPERTS:] + br_ref[...]
    e = jnp.exp(logits - jnp.max(logits, axis=0, keepdims=True))

    best = ga = gb = e1 = e2 = cls = None
    for c in range(N_CLASSES):
        a, b2 = int(CLASS_E1[c]), int(CLASS_E2[c])
        ea, eb = e[a:a + 1, :], e[b2:b2 + 1, :]
        s = ea + eb
        if best is None:
            best, ga, gb = s, ea, eb
            e1 = jnp.full_like(s, float(a))
            e2 = jnp.full_like(s, float(b2))
            cls = jnp.zeros_like(s)
        else:
            better = s > best
            best = jnp.where(better, s, best)
            ga = jnp.where(better, ea, ga)
            gb = jnp.where(better, eb, gb)
            e1 = jnp.where(better, float(a), e1)
            e2 = jnp.where(better, float(b2), e2)
            cls = jnp.where(better, float(c), cls)
    inv = 1.0 / best
    row = lax.broadcasted_iota(jnp.int32, (ROUTE_ROWS, tm), 0)
    rec = jnp.where(row == 0, ga * inv,
          jnp.where(row == 1, gb * inv,
          jnp.where(row == 2, e1,
          jnp.where(row == 3, e2,
          jnp.where(row == 4, cls, 0.0)))))
    route_ref[:, rows] = rec
    wide = jnp.concatenate([rec, jnp.zeros((LANES - ROUTE_ROWS, tm), jnp.float32)], axis=0)
    pay_ref[rows, HALF_D:] = pltpu.bitcast(wide.T, jnp.uint32)


def _outproj_call(h, mp, att, sg, w_out, layer, g1, n2, sh2, sc2, wr, br, tm):
    n = h.shape[0]
    assert n % tm == 0 and tm % OUTPROJ_CHAIN == 0
    row = lambda i: (i, 0)
    fixed = lambda i: (0, 0)
    vec = pl.BlockSpec((1, D_MODEL), fixed)
    return pl.pallas_call(
        _outproj_kernel,
        grid=(n // tm,),
        in_specs=[
            pl.BlockSpec((tm, D_MODEL), row),
            pl.BlockSpec((tm, POOL_DIM), row),
            pl.BlockSpec((tm, NA_DIM), row),
            pl.BlockSpec((tm, SG_DIM), row),
            pl.BlockSpec((None, D_MODEL, D_MODEL), lambda i: (layer, 0, 0)),
            vec, vec, vec, vec,
            pl.BlockSpec((2 * N_EXPERTS, D_MODEL), fixed),
            pl.BlockSpec((N_EXPERTS, 1), fixed),
        ],
        out_specs=[
            pl.BlockSpec((tm, D_MODEL), row),
            pl.BlockSpec((tm, PAYLOAD_W), row),
            pl.BlockSpec((ROUTE_ROWS, tm), lambda i: (0, i)),
        ],
        out_shape=[
            jax.ShapeDtypeStruct((n, D_MODEL), STREAM_DTYPE),
            jax.ShapeDtypeStruct((n, PAYLOAD_W), jnp.uint32),
            jax.ShapeDtypeStruct((ROUTE_ROWS, n), jnp.float32),
        ],
        compiler_params=_cparams(("arbitrary",)),
        name="outproj",
    )(h, mp, att, sg, w_out, g1, n2, sh2, sc2, wr, br)


def _payload_parts(pay_ref, rows=slice(None)):
    lo, hi = _unpack_halves(pay_ref[rows, 0:HALF_D])
    x = jnp.concatenate([lo, hi], axis=-1).astype(jnp.bfloat16)
    return x, pltpu.bitcast(pay_ref[rows, HALF_D:], jnp.float32)


def _expert_pair(x, ga, gb, wga, wua, wda, wgb, wub, wdb):
    ha = (_silu(_dot(x, wga)) * _dot(x, wua) * ga).astype(jnp.bfloat16)
    hb = (_silu(_dot(x, wgb)) * _dot(x, wub) * gb).astype(jnp.bfloat16)
    return _dot(ha, wda) + _dot(hb, wdb)


MOE_PART = MOE_TM // 8


def _moe_sorted_kernel(blk_ref, e1_ref, e2_ref, parts_ref, pay_ref,
                       wga_ref, wua_ref, wda_ref, wgb_ref, wub_ref, wdb_ref, o_ref):
    parts = parts_ref[pl.program_id(0)]

    def run(rows):
        x, route = _payload_parts(pay_ref, rows)
        y = _expert_pair(x, route[:, 0:1], route[:, 1:2], wga_ref[...], wua_ref[...], wda_ref[...],
                         wgb_ref[...], wub_ref[...], wdb_ref[...])
        o_ref[rows, :] = _pack_halves(y)

    for used in range(1, MOE_TM // MOE_PART + 1):
        @pl.when(parts == used)
        def _(used=used):
            run(slice(0, used * MOE_PART))


def _moe_sorted_call(blk, e1, e2, parts, pay_sorted, wg, wu, wd):
    n_tiles = blk.shape[0]
    rows = lambda i, blk, e1, e2, parts: (blk[i], 0)
    wa = lambda i, blk, e1, e2, parts: (e1[i], 0, 0)
    wb = lambda i, blk, e1, e2, parts: (e2[i], 0, 0)
    up = lambda m: pl.BlockSpec((None, D_MODEL, D_EXPERT), m)
    down = lambda m: pl.BlockSpec((None, D_EXPERT, D_MODEL), m)
    return pl.pallas_call(
        _moe_sorted_kernel,
        grid_spec=pltpu.PrefetchScalarGridSpec(
            num_scalar_prefetch=4,
            grid=(n_tiles,),
            in_specs=[pl.BlockSpec((MOE_TM, PAYLOAD_W), rows),
                      up(wa), up(wa), down(wa), up(wb), up(wb), down(wb)],
            out_specs=pl.BlockSpec((MOE_TM, HALF_D), rows),
        ),
        out_shape=jax.ShapeDtypeStruct((n_tiles * MOE_TM, HALF_D), jnp.uint32),
        compiler_params=_cparams(("arbitrary",)),
        name="moe_sorted",
    )(blk, e1, e2, parts, pay_sorted, wg, wu, wd, wg, wu, wd)


DENSE_EXPERTS_PER_STEP = 4


def _moe_dense_kernel(*refs, first_step, resume):
    if resume:
        pay_ref, wg_ref, wu_ref, wd_ref, acc_ref, o_ref = refs
    else:
        pay_ref, wg_ref, wu_ref, wd_ref, o_ref = refs
    step = pl.program_id(0)

    @pl.when(step == 0)
    def _():
        o_ref[...] = acc_ref[...] if resume else jnp.zeros_like(o_ref)

    x, route = _payload_parts(pay_ref)
    y = None
    for k in range(DENSE_EXPERTS_PER_STEP):
        ef = ((first_step + step) * DENSE_EXPERTS_PER_STEP + k).astype(jnp.float32)
        gate = (jnp.where(route[:, 2:3] == ef, route[:, 0:1], 0.0)
                + jnp.where(route[:, 3:4] == ef, route[:, 1:2], 0.0))
        he = (_silu(_dot(x, wg_ref[k])) * _dot(x, wu_ref[k]) * gate).astype(jnp.bfloat16)
        yk = _dot(he, wd_ref[k])
        y = yk if y is None else y + yk
    o_ref[...] += y


def _moe_dense_call(pay, wg, wu, wd, first_step, n_steps, acc=None):
    n = pay.shape[0]
    per = DENSE_EXPERTS_PER_STEP
    whole = pl.BlockSpec((n, D_MODEL), lambda e: (0, 0))
    return pl.pallas_call(
        functools.partial(_moe_dense_kernel, first_step=first_step, resume=acc is not None),
        grid=(n_steps,),
        in_specs=[pl.BlockSpec((n, PAYLOAD_W), lambda e: (0, 0)),
                  pl.BlockSpec((per, D_MODEL, D_EXPERT), lambda e: (first_step + e, 0, 0)),
                  pl.BlockSpec((per, D_MODEL, D_EXPERT), lambda e: (first_step + e, 0, 0)),
                  pl.BlockSpec((per, D_EXPERT, D_MODEL), lambda e: (first_step + e, 0, 0))]
                 + ([whole] if acc is not None else []),
        out_specs=whole,
        out_shape=jax.ShapeDtypeStruct((n, D_MODEL), jnp.float32),
        compiler_params=_cparams(("arbitrary",)),
        name="moe_dense",
    )(pay, wg, wu, wd, *([acc] if acc is not None else []))


def _moe_out(y_ref):
    if y_ref.dtype == jnp.uint32:
        return jnp.concatenate(_unpack_halves(y_ref[...]), axis=-1)
    return y_ref[...]


def _residual_kernel(h_ref, y_ref, g_ref, o_ref):
    o_ref[...] = h_ref[...].astype(jnp.float32) + g_ref[...] * _moe_out(y_ref)


def _residual_call(h, y, g, tm):
    n = h.shape[0]
    row = pl.BlockSpec((tm, D_MODEL), lambda i: (i, 0))
    return pl.pallas_call(
        _residual_kernel,
        grid=(n // tm,),
        in_specs=[row, pl.BlockSpec((tm, y.shape[1]), lambda i: (i, 0)),
                  pl.BlockSpec((1, D_MODEL), lambda i: (0, 0))],
        out_specs=row,
        out_shape=jax.ShapeDtypeStruct((n, D_MODEL), jnp.float32),
        compiler_params=_cparams(("arbitrary",)),
        name="residual",
    )(h, y, g)


SC_ROWS = 128


SC_CORES = 2
SC_SUBCORES = 16
SC_WORKERS = SC_CORES * SC_SUBCORES


def _sc_mesh():
    return plsc.VectorSubcoreMesh(core_axis_name="core", subcore_axis_name="subcore")


def _sc_worker():
    return lax.axis_index("subcore") * SC_CORES + lax.axis_index("core")


def _scatter_rows(x, dest, n_out):
    n, w = x.shape

    per_worker = n // SC_WORKERS
    assert per_worker % SC_ROWS == 0

    @functools.partial(pl.kernel, out_type=jax.ShapeDtypeStruct((n_out, w), x.dtype), mesh=_sc_mesh(),
                       scratch_types=[pltpu.VMEM((SC_ROWS,), jnp.int32), pltpu.VMEM((SC_ROWS, w), x.dtype)])
    def scatter(x_hbm, i_hbm, o_hbm, idx_v, rows_v):
        first = _sc_worker() * per_worker

        @pl.loop(0, per_worker // SC_ROWS)
        def _(i):
            base = pl.multiple_of(first + i * SC_ROWS, SC_ROWS)
            pltpu.sync_copy(i_hbm.at[pl.ds(base, SC_ROWS)], idx_v)
            pltpu.sync_copy(x_hbm.at[pl.ds(base, SC_ROWS)], rows_v)
            pltpu.sync_copy(rows_v, o_hbm.at[idx_v])

    return scatter(x, dest)


def _gather_rows(x, src):
    n = src.shape[0]
    w = x.shape[1]

    per_worker = n // SC_WORKERS
    assert per_worker % SC_ROWS == 0

    @functools.partial(pl.kernel, out_type=jax.ShapeDtypeStruct((n, w), x.dtype), mesh=_sc_mesh(),
                       scratch_types=[pltpu.VMEM((SC_ROWS,), jnp.int32), pltpu.VMEM((SC_ROWS, w), x.dtype)])
    def gather(x_hbm, i_hbm, o_hbm, idx_v, rows_v):
        first = _sc_worker() * per_worker

        @pl.loop(0, per_worker // SC_ROWS)
        def _(i):
            base = pl.multiple_of(first + i * SC_ROWS, SC_ROWS)
            pltpu.sync_copy(i_hbm.at[pl.ds(base, SC_ROWS)], idx_v)
            pltpu.sync_copy(x_hbm.at[idx_v], rows_v)
            pltpu.sync_copy(rows_v, o_hbm.at[pl.ds(base, SC_ROWS)])

    return gather(x, src)


def _routing_plan(cls, n_tiles):
    onehot = (cls[:, None] == jnp.arange(N_CLASSES, dtype=jnp.int32)[None, :]).astype(jnp.int32)
    counts = jnp.sum(onehot, axis=0)
    rank = jnp.sum((jnp.cumsum(onehot, axis=0) - onehot) * onehot, axis=1)
    tiles = (counts + MOE_TM - 1) // MOE_TM
    tile_end = jnp.cumsum(tiles)
    tile_start = tile_end - tiles
    first_len = counts - (tiles - 1) * MOE_TM
    first_of = jnp.sum(onehot * first_len[None, :], axis=1)
    skip = jnp.where(rank >= first_of, MOE_TM - first_of, 0)
    dest = jnp.sum(onehot * tile_start[None, :], axis=1) * MOE_TM + rank + skip
    nact = tile_end[-1]
    tile = jnp.arange(n_tiles, dtype=jnp.int32)
    blk = jnp.minimum(tile, nact - 1)
    tile_cls = jnp.sum((blk[:, None] >= tile_end[None, :]).astype(jnp.int32), axis=1)
    e1 = jnp.asarray(CLASS_E1)[tile_cls]
    e2 = jnp.asarray(CLASS_E2)[tile_cls]
    valid = jnp.where(blk == tile_start[tile_cls], first_len[tile_cls], MOE_TM)
    parts = jnp.where(tile >= nact, 0, (valid + MOE_PART - 1) // MOE_PART)
    return dest.astype(jnp.int32), blk, e1, e2, parts.astype(jnp.int32)


def _row_tile(n, prefer=512):
    return next(t for t in (prefer, 512, 256) if n % t == 0)


def kernel(x, c, ctx, c_ctx, w_ada, b_ada, norm1, w_in, pool_w, pool_scale, q_norm, k_norm, rpb,
           sg_w, sg_b, sg_norm, w_out, norm2, w_router, b_router, w_gate, w_up, w_down):
    depth = w_ada.shape[0]
    n = x.shape[1]
    lc = ctx.shape[1]
    assert x.shape[0] == 1 and x.shape[2] == D_MODEL, "one sample of (tokens, 1024) features"
    assert n % (GRID_W * NA_ROWS_PER_BLOCK) == 0 and n % (SC_WORKERS * SC_ROWS) == 0 and lc % 256 == 0
    bf = jnp.bfloat16
    lat_stream = (x[0],)
    h_ctx = ctx[0]

    cond = jnp.stack([c[0], c_ctx], axis=1)
    mod = _ada_call(cond, w_ada, b_ada)

    wr_t = w_router.T
    wr_hi = wr_t.astype(bf)
    wr_lo = (wr_t - wr_hi.astype(jnp.float32)).astype(bf)
    wr = jnp.concatenate([wr_hi, wr_lo], axis=0)
    br = b_router.reshape(N_EXPERTS, 1)
    n_tiles = n // MOE_TM + N_CLASSES
    w_in_bf = w_in.astype(bf)
    w_out_bf = w_out.astype(bf)

    for l in range(depth):
        last = l == depth - 1
        qg =(q_norm[l] * (HEAD_DIM ** -0.5 * LOG2_E)).reshape(1, NA_DIM)
        kg = k_norm[l].reshape(1, NA_DIM)
        sgn = sg_norm[l].reshape(1, SG_DIM)
        sgw = sg_w[l].astype(bf).reshape(SG_DIM // LANES, 2 * SG_CHUNK, SG_CHUNK)
        sgb = jnp.broadcast_to(sg_b[l].reshape(SG_DIM // LANES, 2 * SG_CHUNK, 1),
                               (SG_DIM // LANES, 2 * SG_CHUNK, LANES))
        pool_bd = jax.scipy.linalg.block_diag(*[pool_w[l, g] for g in range(len(POOL_WINDOWS))]).astype(bf)
        pscale = pool_scale[l].reshape(1, POOL_DIM)
        bias = _natten_bias(rpb[l])
        n1 = norm1[l].reshape(1, D_MODEL)
        n2 = norm2[l].reshape(1, D_MODEL)

        def mods(row):
            return [mod[l, row:row + 1, i * D_MODEL:(i + 1) * D_MODEL] for i in range(6)]

        sh1, sc1, g1, sh2, sc2, g2 = mods(0)
        csh1, csc1, cg1, csh2, csc2, cg2 = mods(1)

        tc = _row_tile(lc)
        mix_pool_c, q_c, k_c, v_c, sg_c = _inproj_call((h_ctx,), n1, csh1, csc1, w_in_bf, l, qg, kg,
                                                       sgn, sgw, sgb, pool_bd, pscale, tc, strips=False)

        outs = _inproj_call(lat_stream, n1, sh1, sc1, w_in_bf, l, qg, kg, sgn, sgw, sgb,
                            pool_bd, pscale, _row_tile(n, INPROJ_TILE), strips=True)
        mix_pool, q, k, v, sg = outs[:5]
        h_lat = outs[5] if len(lat_stream) == 3 else lat_stream[0]
        att, wg_l, wu_l, wd_l = _natten_call(q, k, v, k_c, v_c, bias, w_gate, w_up, w_down, l)
        h1, pay, route = _outproj_call(h_lat, mix_pool, att, sg, w_out_bf, l, g1, n2, sh2, sc2, wr, br,
                                       _row_tile(n, OUTPROJ_TILE))
        cls = route[4].astype(jnp.int32)
        dest, blk, e1, e2, parts = _routing_plan(cls, n_tiles)
        pay_sorted = _scatter_rows(pay, dest, n_tiles * MOE_TM)

        if not last:
            dense_steps = N_EXPERTS // DENSE_EXPERTS_PER_STEP
            att_c = _ctxatt_call(q_c, k_c, v_c)
            h1_c, pay_c, _ = _outproj_call(h_ctx, mix_pool_c, att_c, sg_c, w_out_bf, l, cg1, n2, csh2, csc2,
                                           wr, br, tc)
            y_c = _moe_dense_call(pay_c, wg_l, wu_l, wd_l, 0, dense_steps // 2)
            parts, y_c = lax.optimization_barrier((parts, y_c))

        y_sorted = _moe_sorted_call(blk, e1, e2, parts, pay_sorted, wg_l, wu_l, wd_l)
        y = _gather_rows(y_sorted, dest)
        lat_stream = (h1, y, g2)

        if not last:
            y_c = _moe_dense_call(pay_c, wg_l, wu_l, wd_l, dense_steps // 2, dense_steps - dense_steps // 2,
                                  acc=y_c)
            h_ctx = _residual_call(h1_c, y_c, cg2, tc)

    return _residual_call(*lat_stream, _row_tile(n, RESIDUAL_TILE))[None]
```

```python
import functools

import jax
import jax.numpy as jnp
import numpy as np
from jax import lax
from jax.experimental import pallas as pl
from jax.experimental.pallas import tpu as pltpu
from jax.experimental.pallas import tpu_sc as plsc

D_MODEL = 1024
GRID_W = 64
HEAD_DIM = 64
POOL_WINDOWS = (2, 4, 8, 16)
POOL_DIM = 256
NA_HEADS = 8
NA_DIM = 512
NA_WIN_ROWS = 8
NA_WIN_COLS = 16
SG_DIM = 256
SG_CHUNK = 128
Q_OFF = POOL_DIM
K_OFF = Q_OFF + NA_DIM
V_OFF = K_OFF + NA_DIM
U_OFF = V_OFF + NA_DIM
G_OFF = U_OFF + SG_DIM
IN_DIM = G_OFF + SG_DIM
N_EXPERTS = 16
GROUP_SIZE = 4
D_EXPERT = 512
EPS = 1e-6

LANES = 128
SUBLANES = 8
HEAD_PAIRS = NA_DIM // LANES
VMEM_LIMIT = 48 * 1024 * 1024

PAIRS = ((0, 1), (0, 2), (1, 2), (1, 3), (0, 3), (2, 3))
N_GROUPS = N_EXPERTS // GROUP_SIZE
N_CLASSES = N_GROUPS * len(PAIRS)
CLASS_E1 = np.array([GROUP_SIZE * g + i for g in range(N_GROUPS) for (i, j) in PAIRS], np.int32)
CLASS_E2 = np.array([GROUP_SIZE * g + j for g in range(N_GROUPS) for (i, j) in PAIRS], np.int32)

ROUTE_ROWS = 8
HALF_D = D_MODEL // 2
PAYLOAD_W = HALF_D + LANES
MOE_TM = 512
NEG_BIG = -1e30
LOG2_E = 1.4426950408889634


def _cparams(sem):
    return pltpu.CompilerParams(dimension_semantics=sem, vmem_limit_bytes=VMEM_LIMIT)


def _dot(a, b):
    return jnp.dot(a, b, preferred_element_type=jnp.float32)


def _dot_nt(a, b):
    return lax.dot_general(a, b, (((1,), (1,)), ((), ())), preferred_element_type=jnp.float32)


def _gelu_tanh(x):
    return 0.5 * x * (1.0 + jnp.tanh(0.7978845608028654 * (x + 0.044715 * (x * x * x))))


def _silu(x):
    return x * (1.0 / (1.0 + jnp.exp(-x)))


def _pack_halves(x):
    w = x.shape[1] // 2
    lo = pltpu.bitcast(x[:, :w].astype(jnp.bfloat16).astype(jnp.float32), jnp.uint32) >> 16
    hi = pltpu.bitcast(x[:, w:].astype(jnp.bfloat16).astype(jnp.float32), jnp.uint32) & jnp.uint32(0xFFFF0000)
    return lo | hi


def _unpack_halves(words):
    lo = pltpu.bitcast(words << 16, jnp.float32)
    hi = pltpu.bitcast(words & jnp.uint32(0xFFFF0000), jnp.float32)
    return lo, hi


def _ada_kernel(cond_ref, w_ref, b_ref, o_ref):
    cond = _silu(cond_ref[...])
    w = w_ref[...]
    rows = [jnp.sum(w * cond[:, r:r + 1], axis=0, keepdims=True) + b_ref[...] for r in range(2)]
    o_ref[...] = jnp.concatenate(rows + [jnp.zeros((SUBLANES - 2, w.shape[1]), jnp.float32)], axis=0)


def _ada_call(cond, w_ada, b_ada):
    depth = w_ada.shape[0]
    tn = 1536
    return pl.pallas_call(
        _ada_kernel,
        grid=(depth, 6 * D_MODEL // tn),
        in_specs=[
            pl.BlockSpec((D_MODEL, 2), lambda l, j: (0, 0)),
            pl.BlockSpec((None, D_MODEL, tn), lambda l, j: (l, 0, j)),
            pl.BlockSpec((None, 1, tn), lambda l, j: (l, 0, j)),
        ],
        out_specs=pl.BlockSpec((None, SUBLANES, tn), lambda l, j: (l, 0, j)),
        out_shape=jax.ShapeDtypeStruct((depth, SUBLANES, 6 * D_MODEL), jnp.float32),
        compiler_params=_cparams(("arbitrary", "arbitrary")),
        name="adaln",
    )(cond, w_ada, b_ada.reshape(depth, 1, 6 * D_MODEL))


def _norm_modulate(x, n_ref, sh_ref, sc_ref):
    ms = jnp.mean(x * x, axis=-1, keepdims=True)
    return ((x * lax.rsqrt(ms + EPS)) * (n_ref[...] * (1.0 + sc_ref[...])) + sh_ref[...]).astype(jnp.bfloat16)


def _head_rms_scale(a):
    low = lax.broadcasted_iota(jnp.int32, (a.shape[0], LANES), 1) < HEAD_DIM
    blocks = []
    for p in range(a.shape[1] // LANES):
        sq = jnp.square(a[:, p * LANES:(p + 1) * LANES])
        s_lo = jnp.sum(jnp.where(low, sq, 0.0), axis=-1, keepdims=True)
        s_hi = jnp.sum(jnp.where(low, 0.0, sq), axis=-1, keepdims=True)
        blocks.append(lax.rsqrt(jnp.where(low, s_lo, s_hi) * (1.0 / HEAD_DIM) + EPS))
    return jnp.concatenate(blocks, axis=-1)


STRIP_W = 8
N_STRIPS = GRID_W // STRIP_W


def _store_keys(ref, x):
    if len(ref.shape) == 2:
        ref[...] = x.astype(jnp.bfloat16)
        return
    pair = 2 * STRIP_W
    for s in range(N_STRIPS):
        for rp in range(x.shape[0] // (2 * GRID_W)):
            top = 2 * rp * GRID_W + s * STRIP_W
            rows = jnp.concatenate([x[top:top + STRIP_W], x[top + GRID_W:top + GRID_W + STRIP_W]], axis=0)
            ref[s, rp * pair:(rp + 1) * pair, :] = rows.astype(jnp.bfloat16)


POOL_HALO = 8
HALO_BLOCK = 16
STREAM_DTYPE = jnp.bfloat16


POOL_EDGE = 16


def _pool_mix(xe_ref, w_ref, scale_ref, tm, seq_len):
    low = lax.broadcasted_iota(jnp.int32, (tm, LANES), 1) < HEAD_DIM
    t_edge = pl.program_id(0) * tm + lax.broadcasted_iota(jnp.int32, (POOL_EDGE, LANES), 0)

    def window_mean(s, half):
        mean = s * (1.0 / (2 * half))

        def clip_fix(t):
            count = (jnp.minimum(t + half, seq_len) - jnp.maximum(t - half, 0)).astype(jnp.float32)
            return (2.0 * half) / count

        return jnp.concatenate([mean[:POOL_EDGE] * clip_fix(t_edge),
                                mean[POOL_EDGE:tm - POOL_EDGE],
                                mean[tm - POOL_EDGE:] * clip_fix(t_edge + (tm - POOL_EDGE))], axis=0)

    def window_sums(xs, n_levels):
        sums = []
        s = xs
        for k in range(n_levels):
            step = 1 << k
            s = s[:-step] + s[step:]
            sums.append(s)
        return sums

    outs = []
    for half_block, windows in enumerate(((2, 4), (8, 16))):
        xs = xe_ref[:, half_block * LANES:(half_block + 1) * LANES]
        sums = window_sums(xs, int(np.log2(windows[1])))
        parts = []
        for w in windows:
            half = w // 2
            s = sums[int(np.log2(w)) - 1][POOL_HALO - half:POOL_HALO - half + tm]
            parts.append(window_mean(s, half))
        mean = jnp.where(low, parts[0], parts[1])
        outs.append(mean - xs[POOL_HALO:POOL_HALO + tm])
    d = jnp.concatenate(outs, axis=-1).astype(jnp.bfloat16)
    return (_dot(d, w_ref[...]) * scale_ref[...]).astype(jnp.bfloat16)


def _inproj_kernel(*refs, pending, seq_len):
    n_stream = 7 if pending else 3
    stream, refs = refs[:n_stream], refs[n_stream:]
    if pending:
        h_ref, y_ref, g_ref, hp_ref, hn_ref, yp_ref, yn_ref = stream
        hres_ref, refs = refs[-2], refs[:-2] + refs[-1:]
        x = h_ref[...].astype(jnp.float32) + g_ref[...] * _moe_out(y_ref)
        hres_ref[...] = x.astype(hres_ref.dtype)
        before = hp_ref[...].astype(jnp.float32) + g_ref[...] * _moe_out(yp_ref)
        after = hn_ref[...].astype(jnp.float32) + g_ref[...] * _moe_out(yn_ref)
    else:
        h_ref, hp_ref, hn_ref = stream
        x = h_ref[...].astype(jnp.float32)
        before = hp_ref[...].astype(jnp.float32)
        after = hn_ref[...].astype(jnp.float32)
    x_halo = jnp.concatenate([before[HALO_BLOCK - POOL_HALO:], after[:POOL_HALO]], axis=0)
    (n1_ref, sh_ref, sc_ref, w_ref, qg_ref, kg_ref, sgn_ref, sgw_ref, sgb_ref,
     pw_ref, ps_ref, pool_ref, q_ref, k_ref, v_ref, sg_ref, xe_ref) = refs
    tm = h_ref.shape[0]
    i = pl.program_id(0)
    hn = _norm_modulate(x, n1_ref, sh_ref, sc_ref)
    hn_halo = _norm_modulate(x_halo, n1_ref, sh_ref, sc_ref)

    a_halo = _dot(hn_halo, w_ref[:, 0:Q_OFF])
    xe_ref[0:POOL_HALO, :] = jnp.where(i > 0, a_halo[:POOL_HALO], 0.0)
    xe_ref[POOL_HALO:POOL_HALO + tm, :] = _dot(hn, w_ref[:, 0:Q_OFF])
    xe_ref[POOL_HALO + tm:, :] = jnp.where(i < pl.num_programs(0) - 1, a_halo[POOL_HALO:], 0.0)

    a_g = _dot(hn, w_ref[:, G_OFF:IN_DIM])
    a_u = _dot(hn, w_ref[:, U_OFF:G_OFF])
    a_q = _dot(hn, w_ref[:, Q_OFF:K_OFF])
    a_k = _dot(hn, w_ref[:, K_OFF:V_OFF])
    _store_keys(v_ref, _dot(hn, w_ref[:, V_OFF:U_OFF]))

    gv = _gelu_tanh(a_g)
    q_ref[...] = (a_q * _head_rms_scale(a_q) * qg_ref[...]).astype(jnp.bfloat16)
    _store_keys(k_ref, a_k * _head_rms_scale(a_k) * kg_ref[...])

    u = _gelu_tanh(a_u)
    vn = (gv * _head_rms_scale(gv) * sgn_ref[...]).astype(jnp.bfloat16)
    low = lax.broadcasted_iota(jnp.int32, (SG_CHUNK, LANES), 1) < HEAD_DIM
    for c in range(tm // SG_CHUNK):
        rows = slice(c * SG_CHUNK, (c + 1) * SG_CHUNK)
        for s in range(SG_DIM // LANES):
            cols = slice(s * LANES, (s + 1) * LANES)
            m = _dot(sgw_ref[s], vn[rows, cols]) + sgb_ref[s]
            mixed = jnp.where(low, m[:SG_CHUNK], m[SG_CHUNK:])
            sg_ref[rows, cols] = (u[rows, cols] * mixed).astype(jnp.bfloat16)

    pool_ref[...] = _pool_mix(xe_ref, pw_ref, ps_ref, tm, seq_len)


def _inproj_call(stream, n1, sh, sc, w_in, layer, qg, kg, sgn, sgw, sgb, pool_w, pool_scale, tm, strips):
    pending = len(stream) == 3
    n = stream[0].shape[0]
    assert n % tm == 0 and tm % SG_CHUNK == 0 and tm % (2 * GRID_W) == 0
    if strips:
        kv_spec = pl.BlockSpec((N_STRIPS, tm // N_STRIPS, NA_DIM), lambda i: (0, i, 0))
        kv_shape = jax.ShapeDtypeStruct((N_STRIPS, n // N_STRIPS, NA_DIM), jnp.bfloat16)
    else:
        kv_spec = pl.BlockSpec((tm, NA_DIM), lambda i: (i, 0))
        kv_shape = jax.ShapeDtypeStruct((n, NA_DIM), jnp.bfloat16)
    row = lambda i: (i, 0)
    fixed2 = lambda i: (0, 0)
    fixed3 = lambda i: (0, 0, 0)
    vec = lambda w: pl.BlockSpec((1, w), fixed2)
    rows = pl.BlockSpec((tm, D_MODEL), row)
    per_tile = tm // HALO_BLOCK
    before = lambda i: (jnp.maximum(i * per_tile - 1, 0), 0)
    after = lambda i: (jnp.minimum((i + 1) * per_tile, n // HALO_BLOCK - 1), 0)
    halo = lambda w, m: pl.BlockSpec((HALO_BLOCK, w), m)
    h = stream[0]
    if pending:
        y, g = stream[1], stream[2]
        yw = y.shape[1]
        args = [h, y, g, h, h, y, y]
        stream_specs = [rows, pl.BlockSpec((tm, yw), row), vec(D_MODEL),
                        halo(D_MODEL, before), halo(D_MODEL, after), halo(yw, before), halo(yw, after)]
    else:
        args = [h, h, h]
        stream_specs = [rows, halo(D_MODEL, before), halo(D_MODEL, after)]
    extra_out_specs = [rows] if pending else []
    extra_out_shape = [jax.ShapeDtypeStruct((n, D_MODEL), STREAM_DTYPE)] if pending else []
    return pl.pallas_call(
        functools.partial(_inproj_kernel, pending=pending, seq_len=n),
        grid=(n // tm,),
        in_specs=stream_specs + [
            vec(D_MODEL), vec(D_MODEL), vec(D_MODEL),
            pl.BlockSpec((None, D_MODEL, IN_DIM), lambda i: (layer, 0, 0)),
            vec(NA_DIM), vec(NA_DIM),
            vec(SG_DIM),
            pl.BlockSpec((SG_DIM // LANES, 2 * SG_CHUNK, SG_CHUNK), fixed3),
            pl.BlockSpec((SG_DIM // LANES, 2 * SG_CHUNK, LANES), fixed3),
            pl.BlockSpec((POOL_DIM, POOL_DIM), fixed2),
            vec(POOL_DIM),
        ],
        out_specs=[
            pl.BlockSpec((tm, POOL_DIM), row),
            pl.BlockSpec((tm, NA_DIM), row),
            kv_spec,
            kv_spec,
            pl.BlockSpec((tm, SG_DIM), row),
        ] + extra_out_specs,
        out_shape=[
            jax.ShapeDtypeStruct((n, POOL_DIM), jnp.bfloat16),
            jax.ShapeDtypeStruct((n, NA_DIM), jnp.bfloat16),
            kv_shape,
            kv_shape,
            jax.ShapeDtypeStruct((n, SG_DIM), jnp.bfloat16),
        ] + extra_out_shape,
        scratch_shapes=[pltpu.VMEM((tm + 2 * POOL_HALO, POOL_DIM), jnp.float32)],
        compiler_params=_cparams(("arbitrary",)),
        name="inproj",
    )(*args, n1, sh, sc, w_in, qg, kg, sgn, sgw, sgb, pool_w, pool_scale)


NA_ROWS_PER_BLOCK = 64
NA_GROUP_ROWS = 4
NA_WINDOW_ROWS = NA_GROUP_ROWS + NA_WIN_ROWS
NA_BLOCK = NA_ROWS_PER_BLOCK * GRID_W
NA_GROUP = NA_GROUP_ROWS * GRID_W
NA_HALF_COLS = GRID_W // 2
NA_CHAIN = NA_GROUP_ROWS * NA_HALF_COLS
NA_HALF_STRIPS = 5
NA_HALF_COL0 = (0, GRID_W - NA_HALF_STRIPS * STRIP_W)
NA_RUN = NA_WINDOW_ROWS * STRIP_W
NA_LOCAL = 512
NA_STRIP_BLOCK = NA_ROWS_PER_BLOCK * STRIP_W
NA_STRIP_HALO = (NA_WIN_ROWS // 2) * STRIP_W
NA_EDGE_FIRST, NA_EDGE_NONE, NA_EDGE_LAST = 0, 1, 2


def _stack_heads(x, low):
    zero = jnp.zeros_like(x)
    return jnp.concatenate([jnp.where(low, x, zero), jnp.where(low, zero, x)], axis=0)


def _natten_kernel(q_ref, kp_ref, kc_ref, kn_ref, vp_ref, vc_ref, vn_ref, kx_ref, vx_ref, bias_ref,
                   wg_ref, wu_ref, wd_ref,
                   o_ref, wgu_bf_ref, wd_bf_ref, kwin_ref, vwin_ref, vxe_ref, *, grid_rows):
    b = pl.program_id(1)
    wgu_bf_ref[:, 0:D_EXPERT] = wg_ref[...].astype(jnp.bfloat16)
    wgu_bf_ref[:, D_EXPERT:] = wu_ref[...].astype(jnp.bfloat16)
    wd_bf_ref[...] = wd_ref[...].astype(jnp.bfloat16)
    top, bottom = NA_STRIP_HALO, NA_STRIP_HALO + NA_STRIP_BLOCK
    kwin_ref[:, 0:top, :] = kp_ref[...]
    kwin_ref[:, top:bottom, :] = kc_ref[...]
    kwin_ref[:, bottom:, :] = kn_ref[...]
    vwin_ref[:, 0:top, 0:LANES] = vp_ref[...]
    vwin_ref[:, top:bottom, 0:LANES] = vc_ref[...]
    vwin_ref[:, bottom:, 0:LANES] = vn_ref[...]
    vwin_ref[:, :, LANES:] = jnp.ones(vwin_ref.shape[:2] + (LANES,), jnp.bfloat16)
    vxe_ref[:, 0:LANES] = vx_ref[...]
    vxe_ref[:, LANES:] = jnp.ones((vxe_ref.shape[0], LANES), jnp.bfloat16)
    low_q = lax.broadcasted_iota(jnp.int32, (NA_CHAIN, LANES), 1) < HEAD_DIM
    n_pad = NA_LOCAL - NA_HALF_STRIPS * NA_RUN

    def window_start(g):
        r0 = b * NA_ROWS_PER_BLOCK + g * NA_GROUP_ROWS
        ws = jnp.clip(r0 - NA_WIN_ROWS // 2, 0, grid_rows - NA_WINDOW_ROWS)
        edge = jnp.where(r0 == 0, NA_EDGE_FIRST,
                         jnp.where(r0 == grid_rows - NA_GROUP_ROWS, NA_EDGE_LAST, NA_EDGE_NONE))
        start = pl.multiple_of((ws - b * NA_ROWS_PER_BLOCK + NA_WIN_ROWS // 2) * STRIP_W, NA_STRIP_HALO)
        return start, edge

    def local_window(win_ref, g, half):
        start, _ = window_start(g)
        s0 = NA_HALF_COL0[half] // STRIP_W
        runs = [win_ref[s, pl.ds(start, NA_RUN), :] for s in range(s0, s0 + NA_HALF_STRIPS)]
        return jnp.concatenate(runs + [jnp.zeros((n_pad, win_ref.shape[2]), jnp.bfloat16)], axis=0)

    def query_rows(g, half, j):
        first = g * NA_GROUP + j * GRID_W + half * NA_HALF_COLS
        return slice(first, first + NA_HALF_COLS)

    def scores(c):
        g, half = divmod(c, 2)
        _, edge = window_start(g)
        qh = jnp.concatenate([q_ref[query_rows(g, half, j), :] for j in range(NA_GROUP_ROWS)], axis=0)
        lhs = _stack_heads(qh, low_q)
        kl = local_window(kwin_ref, g, half)
        return jnp.concatenate([_dot_nt(lhs, kl).astype(jnp.bfloat16) + bias_ref[edge, half],
                                _dot_nt(lhs, kx_ref[...]).astype(jnp.bfloat16)], axis=-1)

    n_chains = 2 * (NA_ROWS_PER_BLOCK // NA_GROUP_ROWS)
    s_next = scores(0)
    for c in range(n_chains):
        s = s_next
        if c + 1 < n_chains:
            s_next = scores(c + 1)
        g, half = divmod(c, 2)
        vl = local_window(vwin_ref, g, half)
        m = jnp.max(s, axis=-1, keepdims=True)
        pb = jnp.exp2(s - m)
        o = _dot(pb[:, :NA_LOCAL], vl) + _dot(pb[:, NA_LOCAL:], vxe_ref[...])
        o = o[:, :LANES] * (1.0 / o[:, LANES:])
        o = jnp.where(low_q, o[:NA_CHAIN], o[NA_CHAIN:]).astype(jnp.bfloat16)
        for j in range(NA_GROUP_ROWS):
            o_ref[query_rows(g, half, j), :] = o[j * NA_HALF_COLS:(j + 1) * NA_HALF_COLS]


def _natten_call(q, k, v, k_ctx, v_ctx, bias, w_gate, w_up, w_down, layer):
    n = q.shape[0]
    grid_rows = n // GRID_W
    assert grid_rows % NA_ROWS_PER_BLOCK == 0 and grid_rows >= 2 * NA_ROWS_PER_BLOCK
    nblk = n // NA_BLOCK
    steps = HEAD_PAIRS * nblk
    depth = w_gate.shape[0]
    up_rows = N_EXPERTS * D_MODEL
    down_rows = N_EXPERTS * D_EXPERT
    assert up_rows % steps == 0 and down_rows % steps == 0
    wg2 = w_gate.reshape(depth * up_rows, D_EXPERT)
    wu2 = w_up.reshape(depth * up_rows, D_EXPERT)
    wd2 = w_down.reshape(depth * down_rows, D_MODEL)
    up_in = pl.BlockSpec((up_rows // steps, D_EXPERT), lambda p, b: (layer * steps + p * nblk + b, 0))
    down_in = pl.BlockSpec((down_rows // steps, D_MODEL), lambda p, b: (layer * steps + p * nblk + b, 0))
    up_out = pl.BlockSpec((up_rows // steps, 2 * D_EXPERT), lambda p, b: (p * nblk + b, 0))
    down_out = pl.BlockSpec((down_rows // steps, D_MODEL), lambda p, b: (p * nblk + b, 0))
    n_halo = n // N_STRIPS // NA_STRIP_HALO
    hb = NA_STRIP_BLOCK // NA_STRIP_HALO
    rows = pl.BlockSpec((NA_BLOCK, LANES), lambda p, b: (b, p))
    cur = pl.BlockSpec((N_STRIPS, NA_STRIP_BLOCK, LANES), lambda p, b: (0, b, p))
    prev = pl.BlockSpec((N_STRIPS, NA_STRIP_HALO, LANES), lambda p, b: (0, jnp.maximum(b * hb - 1, 0), p))
    nxt = pl.BlockSpec((N_STRIPS, NA_STRIP_HALO, LANES),
                       lambda p, b: (0, jnp.minimum((b + 1) * hb, n_halo - 1), p))
    ctx = pl.BlockSpec((k_ctx.shape[0], LANES), lambda p, b: (0, p))
    win_rows = NA_STRIP_BLOCK + 2 * NA_STRIP_HALO
    att, wgu_bf, wd_bf = pl.pallas_call(
        functools.partial(_natten_kernel, grid_rows=grid_rows),
        grid=(HEAD_PAIRS, nblk),
        in_specs=[rows, prev, cur, nxt, prev, cur, nxt, ctx, ctx,
                  pl.BlockSpec((None, 3, 2, 2 * NA_CHAIN, NA_LOCAL), lambda p, b: (p, 0, 0, 0, 0)),
                  up_in, up_in, down_in],
        out_specs=[rows, up_out, down_out],
        out_shape=[jax.ShapeDtypeStruct((n, NA_DIM), jnp.bfloat16),
                   jax.ShapeDtypeStruct((up_rows, 2 * D_EXPERT), jnp.bfloat16),
                   jax.ShapeDtypeStruct((down_rows, D_MODEL), jnp.bfloat16)],
        scratch_shapes=[pltpu.VMEM((N_STRIPS, win_rows, LANES), jnp.bfloat16),
                        pltpu.VMEM((N_STRIPS, win_rows, 2 * LANES), jnp.bfloat16),
                        pltpu.VMEM((k_ctx.shape[0], 2 * LANES), jnp.bfloat16)],
        compiler_params=_cparams(("arbitrary", "arbitrary")),
        name="natten",
    )(q, k, k, k, v, v, v, k_ctx, v_ctx, bias, wg2, wu2, wd2)
    return (att, wgu_bf.reshape(N_EXPERTS, D_MODEL, 2 * D_EXPERT), wd_bf.reshape(N_EXPERTS, D_EXPERT, D_MODEL))


def _natten_bias(rpb):
    cols = np.arange(GRID_W)
    col_start = np.clip(cols - NA_WIN_COLS // 2, 0, GRID_W - NA_WIN_COLS)
    kc = np.arange(GRID_W)
    in_win = (kc[None, :] >= col_start[:, None]) & (kc[None, :] < col_start[:, None] + NA_WIN_COLS)
    dc = kc[None, :] - cols[:, None] + NA_WIN_COLS - 1
    sel = (np.arange(2 * NA_WIN_COLS - 1)[:, None, None] == dc[None]) & in_win[None]
    t2 = jnp.einsum("hdj,jqk->hdqk", rpb, jnp.asarray(sel, jnp.float32), precision=lax.Precision.HIGHEST)
    t2 = jnp.where(in_win[None, None], t2 * LOG2_E, NEG_BIG)
    neg = jnp.full((NA_HEADS, 1, GRID_W, GRID_W), NEG_BIG, jnp.float32)
    t2e = jnp.concatenate([neg, t2, neg], axis=1)
    u = jnp.concatenate([t2e[:, :-1], t2e[:, 1:]], axis=-1)
    u = u.reshape(HEAD_PAIRS, 2, 2 * NA_WIN_ROWS, GRID_W, LANES)

    place = np.zeros((2, NA_WINDOW_ROWS * GRID_W, NA_LOCAL), np.float32)
    for half, c0 in enumerate(NA_HALF_COL0):
        for a in range(NA_WINDOW_ROWS):
            for kcol in range(c0, c0 + NA_HALF_STRIPS * STRIP_W):
                s, c8 = divmod(kcol - c0, STRIP_W)
                place[half, a * GRID_W + kcol, s * NA_RUN + a * STRIP_W + c8] = 1.0
    outside = np.full((3, NA_GROUP_ROWS, NA_LOCAL), NEG_BIG, np.float32)
    for edge in (NA_EDGE_FIRST, NA_EDGE_NONE, NA_EDGE_LAST):
        for j in range(NA_GROUP_ROWS):
            lo, _ = _window_rows(edge, j)
            for s in range(NA_HALF_STRIPS):
                outside[edge, j, s * NA_RUN + lo * STRIP_W:s * NA_RUN + (lo + NA_WIN_ROWS) * STRIP_W] = 0.0
    return pl.pallas_call(
        _bias_expand_kernel,
        grid=(HEAD_PAIRS,),
        in_specs=[pl.BlockSpec((None, 2, 2 * NA_WIN_ROWS, GRID_W, LANES), lambda p: (p, 0, 0, 0, 0)),
                  pl.BlockSpec(place.shape, lambda p: (0, 0, 0)),
                  pl.BlockSpec(outside.shape, lambda p: (0, 0, 0))],
        out_specs=pl.BlockSpec((None, 3, 2, 2 * NA_CHAIN, NA_LOCAL), lambda p: (p, 0, 0, 0, 0)),
        out_shape=jax.ShapeDtypeStruct((HEAD_PAIRS, 3, 2, 2 * NA_CHAIN, NA_LOCAL), jnp.bfloat16),
        compiler_params=_cparams(("arbitrary",)),
        name="bias_expand",
    )(u, jnp.asarray(place, jnp.bfloat16), jnp.asarray(outside))


def _window_rows(edge, j):
    if edge == NA_EDGE_FIRST:
        return 0, NA_WIN_ROWS - 1 - j
    if edge == NA_EDGE_NONE:
        return j, NA_WIN_ROWS // 2 - 1
    return NA_WINDOW_ROWS - NA_WIN_ROWS, NA_WIN_ROWS // 2 - 1 - j


def _bias_expand_kernel(u_ref, place_ref, outside_ref, o_ref):
    low = lax.broadcasted_iota(jnp.int32, (NA_HALF_COLS, LANES), 1) < GRID_W
    zero = jnp.zeros((NA_HALF_COLS, LANES), jnp.float32)
    for edge in (NA_EDGE_FIRST, NA_EDGE_NONE, NA_EDGE_LAST):
        for half in range(2):
            q0 = half * NA_HALF_COLS
            blocks, masks = [], []
            for hd in range(2):
                for j in range(NA_GROUP_ROWS):
                    lo, base = _window_rows(edge, j)
                    tiles = []
                    for i in range(NA_WINDOW_ROWS // 2):
                        a0, a1 = 2 * i, 2 * i + 1
                        ok0 = lo <= a0 < lo + NA_WIN_ROWS
                        ok1 = lo <= a1 < lo + NA_WIN_ROWS
                        if not (ok0 or ok1):
                            tile = zero
                        else:
                            tile = u_ref[hd, base + a1 - lo, q0:q0 + NA_HALF_COLS, :]
                            if not ok0:
                                tile = jnp.where(low, zero, tile)
                            if not ok1:
                                tile = jnp.where(low, tile, zero)
                        tiles.append(tile)
                    blocks.append(jnp.concatenate(tiles, axis=-1))
                    masks.append(jnp.broadcast_to(outside_ref[edge, j:j + 1, :], (NA_HALF_COLS, NA_LOCAL)))
            lhs = jnp.concatenate(blocks, axis=0).astype(jnp.bfloat16)
            placed = _dot(lhs, place_ref[half]) + jnp.concatenate(masks, axis=0)
            o_ref[edge, half] = placed.astype(o_ref.dtype)


def _ctxatt_kernel(q_ref, k_ref, v_ref, o_ref):
    lc = q_ref.shape[0]
    low = lax.broadcasted_iota(jnp.int32, (lc, LANES), 1) < HEAD_DIM
    lhs = _stack_heads(q_ref[...], low)
    s = _dot_nt(lhs, k_ref[...])
    m = jnp.max(s, axis=-1, keepdims=True)
    p = jnp.exp2(s - m)
    denom = jnp.sum(p, axis=-1, keepdims=True)
    o = _dot(p.astype(jnp.bfloat16), v_ref[...]) * (1.0 / denom)
    o_ref[...] = jnp.where(low, o[:lc], o[lc:]).astype(jnp.bfloat16)


def _ctxatt_call(q, k, v):
    lc = q.shape[0]
    spec = pl.BlockSpec((lc, LANES), lambda p: (0, p))
    return pl.pallas_call(
        _ctxatt_kernel,
        grid=(HEAD_PAIRS,),
        in_specs=[spec, spec, spec],
        out_specs=spec,
        out_shape=jax.ShapeDtypeStruct((lc, NA_DIM), jnp.bfloat16),
        compiler_params=_cparams(("arbitrary",)),
        name="ctxatt",
    )(q, k, v)


OUTPROJ_CHAIN = 256
OUTPROJ_TILE = 1024
INPROJ_TILE = 1024
RESIDUAL_TILE = 2048


def _outproj_kernel(h_ref, mp_ref, att_ref, sg_ref, wo_ref, g1_ref, n2_ref, sh_ref, sc_ref,
                    wr_ref, br_ref, h1_ref, pay_ref, route_ref):
    chains = [slice(c * OUTPROJ_CHAIN, (c + 1) * OUTPROJ_CHAIN) for c in range(h_ref.shape[0] // OUTPROJ_CHAIN)]
    h1s = []
    for rows in chains:
        mix = (_dot(mp_ref[rows, :], wo_ref[0:POOL_DIM, :])
               + _dot(att_ref[rows, :], wo_ref[POOL_DIM:POOL_DIM + NA_DIM, :])
               + _dot(sg_ref[rows, :], wo_ref[POOL_DIM + NA_DIM:, :]))
        h1 = h_ref[rows, :].astype(jnp.float32) + g1_ref[...] * mix
        h1_ref[rows, :] = h1.astype(h1_ref.dtype)
        h1s.append(h1)
    for rows, h1 in zip(chains, h1s):
        _outproj_route(rows, h1, n2_ref, sh_ref, sc_ref, wr_ref, br_ref, pay_ref, route_ref)


def _outproj_route(rows, h1, n2_ref, sh_ref, sc_ref, wr_ref, br_ref, pay_ref, route_ref):
    tm = OUTPROJ_CHAIN
    ms = jnp.mean(h1 * h1, axis=-1, keepdims=True)
    hm = (h1 * lax.rsqrt(ms + EPS)) * (n2_ref[...] * (1.0 + sc_ref[...])) + sh_ref[...]
    pay_ref[rows, 0:HALF_D] = _pack_halves(hm)

    hm_hi = hm.astype(jnp.bfloat16)
    lt = _dot_nt(wr_ref[...], hm_hi)
    logits = lt[:N_EXPERTS] + lt[N_EXPERTS:] + br_ref[...]
    e = jnp.exp(logits - jnp.max(logits, axis=0, keepdims=True))

    best = ga = gb = e1 = e2 = cls = None
    for c in range(N_CLASSES):
        a, b2 = int(CLASS_E1[c]), int(CLASS_E2[c])
        ea, eb = e[a:a + 1, :], e[b2:b2 + 1, :]
        s = ea + eb
        if best is None:
            best, ga, gb = s, ea, eb
            e1 = jnp.full_like(s, float(a))
            e2 = jnp.full_like(s, float(b2))
            cls = jnp.zeros_like(s)
        else:
            better = s > best
            best = jnp.where(better, s, best)
            ga = jnp.where(better, ea, ga)
            gb = jnp.where(better, eb, gb)
            e1 = jnp.where(better, float(a), e1)
            e2 = jnp.where(better, float(b2), e2)
            cls = jnp.where(better, float(c), cls)
    inv = 1.0 / best
    row = lax.broadcasted_iota(jnp.int32, (ROUTE_ROWS, tm), 0)
    rec = jnp.where(row == 0, ga * inv,
          jnp.where(row == 1, gb * inv,
          jnp.where(row == 2, e1,
          jnp.where(row == 3, e2,
          jnp.where(row == 4, cls, 0.0)))))
    route_ref[:, rows] = rec
    wide = jnp.concatenate([rec, jnp.zeros((LANES - ROUTE_ROWS, tm), jnp.float32)], axis=0)
    pay_ref[rows, HALF_D:] = pltpu.bitcast(wide.T, jnp.uint32)


def _outproj_call(h, mp, att, sg, w_out, layer, g1, n2, sh2, sc2, wr, br, tm):
    n = h.shape[0]
    assert n % tm == 0 and tm % OUTPROJ_CHAIN == 0
    row = lambda i: (i, 0)
    fixed = lambda i: (0, 0)
    vec = pl.BlockSpec((1, D_MODEL), fixed)
    return pl.pallas_call(
        _outproj_kernel,
        grid=(n // tm,),
        in_specs=[
            pl.BlockSpec((tm, D_MODEL), row),
            pl.BlockSpec((tm, POOL_DIM), row),
            pl.BlockSpec((tm, NA_DIM), row),
            pl.BlockSpec((tm, SG_DIM), row),
            pl.BlockSpec((None, D_MODEL, D_MODEL), lambda i: (layer, 0, 0)),
            vec, vec, vec, vec,
            pl.BlockSpec((2 * N_EXPERTS, D_MODEL), fixed),
            pl.BlockSpec((N_EXPERTS, 1), fixed),
        ],
        out_specs=[
            pl.BlockSpec((tm, D_MODEL), row),
            pl.BlockSpec((tm, PAYLOAD_W), row),
            pl.BlockSpec((ROUTE_ROWS, tm), lambda i: (0, i)),
        ],
        out_shape=[
            jax.ShapeDtypeStruct((n, D_MODEL), STREAM_DTYPE),
            jax.ShapeDtypeStruct((n, PAYLOAD_W), jnp.uint32),
            jax.ShapeDtypeStruct((ROUTE_ROWS, n), jnp.float32),
        ],
        compiler_params=_cparams(("arbitrary",)),
        name="outproj",
    )(h, mp, att, sg, w_out, g1, n2, sh2, sc2, wr, br)


def _payload_parts(pay_ref, rows=slice(None)):
    lo, hi = _unpack_halves(pay_ref[rows, 0:HALF_D])
    x = jnp.concatenate([lo, hi], axis=-1).astype(jnp.bfloat16)
    return x, pltpu.bitcast(pay_ref[rows, HALF_D:], jnp.float32)


def _expert_hidden(x, wgu, gate):
    gu = _dot(x, wgu)
    return (_silu(gu[:, :D_EXPERT]) * gu[:, D_EXPERT:] * gate).astype(jnp.bfloat16)


def _expert_pair(x, ga, gb, wgua, wda, wgub, wdb):
    ha = _expert_hidden(x, wgua, ga)
    hb = _expert_hidden(x, wgub, gb)
    return _dot(ha, wda) + _dot(hb, wdb)


MOE_PART = MOE_TM // 8


def _moe_sorted_kernel(blk_ref, e1_ref, e2_ref, parts_ref, pay_ref,
                       wgua_ref, wda_ref, wgub_ref, wdb_ref, o_ref):
    parts = parts_ref[pl.program_id(0)]

    def run(rows):
        x, route = _payload_parts(pay_ref, rows)
        y = _expert_pair(x, route[:, 0:1], route[:, 1:2], wgua_ref[...], wda_ref[...],
                         wgub_ref[...], wdb_ref[...])
        o_ref[rows, :] = _pack_halves(y)

    for used in range(1, MOE_TM // MOE_PART + 1):
        @pl.when(parts == used)
        def _(used=used):
            run(slice(0, used * MOE_PART))


def _moe_sorted_call(blk, e1, e2, parts, pay_sorted, wgu, wd):
    n_tiles = blk.shape[0]
    rows = lambda i, blk, e1, e2, parts: (blk[i], 0)
    wa = lambda i, blk, e1, e2, parts: (e1[i], 0, 0)
    wb = lambda i, blk, e1, e2, parts: (e2[i], 0, 0)
    up = lambda m: pl.BlockSpec((None, D_MODEL, 2 * D_EXPERT), m)
    down = lambda m: pl.BlockSpec((None, D_EXPERT, D_MODEL), m)
    return pl.pallas_call(
        _moe_sorted_kernel,
        grid_spec=pltpu.PrefetchScalarGridSpec(
            num_scalar_prefetch=4,
            grid=(n_tiles,),
            in_specs=[pl.BlockSpec((MOE_TM, PAYLOAD_W), rows),
                      up(wa), down(wa), up(wb), down(wb)],
            out_specs=pl.BlockSpec((MOE_TM, HALF_D), rows),
        ),
        out_shape=jax.ShapeDtypeStruct((n_tiles * MOE_TM, HALF_D), jnp.uint32),
        compiler_params=_cparams(("arbitrary",)),
        name="moe_sorted",
    )(blk, e1, e2, parts, pay_sorted, wgu, wd, wgu, wd)


DENSE_EXPERTS_PER_STEP = 4


def _moe_dense_kernel(*refs, first_step, resume):
    if resume:
        pay_ref, wgu_ref, wd_ref, acc_ref, o_ref = refs
    else:
        pay_ref, wgu_ref, wd_ref, o_ref = refs
    step = pl.program_id(0)

    @pl.when(step == 0)
    def _():
        o_ref[...] = acc_ref[...] if resume else jnp.zeros_like(o_ref)

    x, route = _payload_parts(pay_ref)
    y = None
    for k in range(DENSE_EXPERTS_PER_STEP):
        ef = ((first_step + step) * DENSE_EXPERTS_PER_STEP + k).astype(jnp.float32)
        gate = (jnp.where(route[:, 2:3] == ef, route[:, 0:1], 0.0)
                + jnp.where(route[:, 3:4] == ef, route[:, 1:2], 0.0))
        yk = _dot(_expert_hidden(x, wgu_ref[k], gate), wd_ref[k])
        y = yk if y is None else y + yk
    o_ref[...] += y


def _moe_dense_call(pay, wgu, wd, first_step, n_steps, acc=None):
    n = pay.shape[0]
    per = DENSE_EXPERTS_PER_STEP
    whole = pl.BlockSpec((n, D_MODEL), lambda e: (0, 0))
    return pl.pallas_call(
        functools.partial(_moe_dense_kernel, first_step=first_step, resume=acc is not None),
        grid=(n_steps,),
        in_specs=[pl.BlockSpec((n, PAYLOAD_W), lambda e: (0, 0)),
                  pl.BlockSpec((per, D_MODEL, 2 * D_EXPERT), lambda e: (first_step + e, 0, 0)),
                  pl.BlockSpec((per, D_EXPERT, D_MODEL), lambda e: (first_step + e, 0, 0))]
                 + ([whole] if acc is not None else []),
        out_specs=whole,
        out_shape=jax.ShapeDtypeStruct((n, D_MODEL), jnp.float32),
        compiler_params=_cparams(("arbitrary",)),
        name="moe_dense",
    )(pay, wgu, wd, *([acc] if acc is not None else []))


def _moe_out(y_ref):
    if y_ref.dtype == jnp.uint32:
        return jnp.concatenate(_unpack_halves(y_ref[...]), axis=-1)
    return y_ref[...]


def _residual_kernel(h_ref, y_ref, g_ref, o_ref):
    o_ref[...] = h_ref[...].astype(jnp.float32) + g_ref[...] * _moe_out(y_ref)


def _residual_call(h, y, g, tm):
    n = h.shape[0]
    row = pl.BlockSpec((tm, D_MODEL), lambda i: (i, 0))
    return pl.pallas_call(
        _residual_kernel,
        grid=(n // tm,),
        in_specs=[row, pl.BlockSpec((tm, y.shape[1]), lambda i: (i, 0)),
                  pl.BlockSpec((1, D_MODEL), lambda i: (0, 0))],
        out_specs=row,
        out_shape=jax.ShapeDtypeStruct((n, D_MODEL), jnp.float32),
        compiler_params=_cparams(("arbitrary",)),
        name="residual",
    )(h, y, g)


SC_ROWS = 128


SC_CORES = 2
SC_SUBCORES = 16
SC_WORKERS = SC_CORES * SC_SUBCORES


def _sc_mesh():
    return plsc.VectorSubcoreMesh(core_axis_name="core", subcore_axis_name="subcore")


def _sc_worker():
    return lax.axis_index("subcore") * SC_CORES + lax.axis_index("core")


def _scatter_rows(x, dest, n_out):
    n, w = x.shape

    per_worker = n // SC_WORKERS
    assert per_worker % SC_ROWS == 0

    @functools.partial(pl.kernel, out_type=jax.ShapeDtypeStruct((n_out, w), x.dtype), mesh=_sc_mesh(),
                       scratch_types=[pltpu.VMEM((SC_ROWS,), jnp.int32), pltpu.VMEM((SC_ROWS, w), x.dtype)])
    def scatter(x_hbm, i_hbm, o_hbm, idx_v, rows_v):
        first = _sc_worker() * per_worker

        @pl.loop(0, per_worker // SC_ROWS)
        def _(i):
            base = pl.multiple_of(first + i * SC_ROWS, SC_ROWS)
            pltpu.sync_copy(i_hbm.at[pl.ds(base, SC_ROWS)], idx_v)
            pltpu.sync_copy(x_hbm.at[pl.ds(base, SC_ROWS)], rows_v)
            pltpu.sync_copy(rows_v, o_hbm.at[idx_v])

    return scatter(x, dest)


def _gather_rows(x, src):
    n = src.shape[0]
    w = x.shape[1]

    per_worker = n // SC_WORKERS
    assert per_worker % SC_ROWS == 0

    @functools.partial(pl.kernel, out_type=jax.ShapeDtypeStruct((n, w), x.dtype), mesh=_sc_mesh(),
                       scratch_types=[pltpu.VMEM((SC_ROWS,), jnp.int32), pltpu.VMEM((SC_ROWS, w), x.dtype)])
    def gather(x_hbm, i_hbm, o_hbm, idx_v, rows_v):
        first = _sc_worker() * per_worker

        @pl.loop(0, per_worker // SC_ROWS)
        def _(i):
            base = pl.multiple_of(first + i * SC_ROWS, SC_ROWS)
            pltpu.sync_copy(i_hbm.at[pl.ds(base, SC_ROWS)], idx_v)
            pltpu.sync_copy(x_hbm.at[idx_v], rows_v)
            pltpu.sync_copy(rows_v, o_hbm.at[pl.ds(base, SC_ROWS)])

    return gather(x, src)


def _routing_plan(cls, n_tiles):
    onehot = (cls[:, None] == jnp.arange(N_CLASSES, dtype=jnp.int32)[None, :]).astype(jnp.int32)
    counts = jnp.sum(onehot, axis=0)
    rank = jnp.sum((jnp.cumsum(onehot, axis=0) - onehot) * onehot, axis=1)
    tiles = (counts + MOE_TM - 1) // MOE_TM
    tile_end = jnp.cumsum(tiles)
    tile_start = tile_end - tiles
    first_len = counts - (tiles - 1) * MOE_TM
    first_of = jnp.sum(onehot * first_len[None, :], axis=1)
    skip = jnp.where(rank >= first_of, MOE_TM - first_of, 0)
    dest = jnp.sum(onehot * tile_start[None, :], axis=1) * MOE_TM + rank + skip
    nact = tile_end[-1]
    tile = jnp.arange(n_tiles, dtype=jnp.int32)
    blk = jnp.minimum(tile, nact - 1)
    tile_cls = jnp.sum((blk[:, None] >= tile_end[None, :]).astype(jnp.int32), axis=1)
    e1 = jnp.asarray(CLASS_E1)[tile_cls]
    e2 = jnp.asarray(CLASS_E2)[tile_cls]
    valid = jnp.where(blk == tile_start[tile_cls], first_len[tile_cls], MOE_TM)
    parts = jnp.where(tile >= nact, 0, (valid + MOE_PART - 1) // MOE_PART)
    return dest.astype(jnp.int32), blk, e1, e2, parts.astype(jnp.int32)


def _row_tile(n, prefer=512):
    return next(t for t in (prefer, 512, 256) if n % t == 0)


def kernel(x, c, ctx, c_ctx, w_ada, b_ada, norm1, w_in, pool_w, pool_scale, q_norm, k_norm, rpb,
           sg_w, sg_b, sg_norm, w_out, norm2, w_router, b_router, w_gate, w_up, w_down):
    depth = w_ada.shape[0]
    n = x.shape[1]
    lc = ctx.shape[1]
    assert x.shape[0] == 1 and x.shape[2] == D_MODEL, "one sample of (tokens, 1024) features"
    assert n % (GRID_W * NA_ROWS_PER_BLOCK) == 0 and n % (SC_WORKERS * SC_ROWS) == 0 and lc % 256 == 0
    bf = jnp.bfloat16
    lat_stream = (x[0],)
    h_ctx = ctx[0]

    cond = jnp.stack([c[0], c_ctx], axis=1)
    mod = _ada_call(cond, w_ada, b_ada)

    wr_t = w_router.T
    wr_hi = wr_t.astype(bf)
    wr_lo = (wr_t - wr_hi.astype(jnp.float32)).astype(bf)
    wr = jnp.concatenate([wr_hi, wr_lo], axis=0)
    br = b_router.reshape(N_EXPERTS, 1)
    n_tiles = n // MOE_TM + N_CLASSES
    w_in_bf = w_in.astype(bf)
    w_out_bf = w_out.astype(bf)

    for l in range(depth):
        last = l == depth - 1
        qg =(q_norm[l] * (HEAD_DIM ** -0.5 * LOG2_E)).reshape(1, NA_DIM)
        kg = k_norm[l].reshape(1, NA_DIM)
        sgn = sg_norm[l].reshape(1, SG_DIM)
        sgw = sg_w[l].astype(bf).reshape(SG_DIM // LANES, 2 * SG_CHUNK, SG_CHUNK)
        sgb = jnp.broadcast_to(sg_b[l].reshape(SG_DIM // LANES, 2 * SG_CHUNK, 1),
                               (SG_DIM // LANES, 2 * SG_CHUNK, LANES))
        pool_bd = jax.scipy.linalg.block_diag(*[pool_w[l, g] for g in range(len(POOL_WINDOWS))]).astype(bf)
        pscale = pool_scale[l].reshape(1, POOL_DIM)
        bias = _natten_bias(rpb[l])
        n1 = norm1[l].reshape(1, D_MODEL)
        n2 = norm2[l].reshape(1, D_MODEL)

        def mods(row):
            return [mod[l, row:row + 1, i * D_MODEL:(i + 1) * D_MODEL] for i in range(6)]

        sh1, sc1, g1, sh2, sc2, g2 = mods(0)
        csh1, csc1, cg1, csh2, csc2, cg2 = mods(1)

        tc = _row_tile(lc)
        mix_pool_c, q_c, k_c, v_c, sg_c = _inproj_call((h_ctx,), n1, csh1, csc1, w_in_bf, l, qg, kg,
                                                       sgn, sgw, sgb, pool_bd, pscale, tc, strips=False)

        outs = _inproj_call(lat_stream, n1, sh1, sc1, w_in_bf, l, qg, kg, sgn, sgw, sgb,
                            pool_bd, pscale, _row_tile(n, INPROJ_TILE), strips=True)
        mix_pool, q, k, v, sg = outs[:5]
        h_lat = outs[5] if len(lat_stream) == 3 else lat_stream[0]
        att, wgu_l, wd_l = _natten_call(q, k, v, k_c, v_c, bias, w_gate, w_up, w_down, l)
        h1, pay, route = _outproj_call(h_lat, mix_pool, att, sg, w_out_bf, l, g1, n2, sh2, sc2, wr, br,
                                       _row_tile(n, OUTPROJ_TILE))
        cls = route[4].astype(jnp.int32)
        dest, blk, e1, e2, parts = _routing_plan(cls, n_tiles)
        pay_sorted = _scatter_rows(pay, dest, n_tiles * MOE_TM)

        if not last:
            dense_steps = N_EXPERTS // DENSE_EXPERTS_PER_STEP
            att_c = _ctxatt_call(q_c, k_c, v_c)
            h1_c, pay_c, _ = _outproj_call(h_ctx, mix_pool_c, att_c, sg_c, w_out_bf, l, cg1, n2, csh2, csc2,
                                           wr, br, tc)
            y_c = _moe_dense_call(pay_c, wgu_l, wd_l, 0, dense_steps // 2)
            parts, y_c = lax.optimization_barrier((parts, y_c))

        y_sorted = _moe_sorted_call(blk, e1, e2, parts, pay_sorted, wgu_l, wd_l)
        y = _gather_rows(y_sorted, dest)
        lat_stream = (h1, y, g2)

        if not last:
            y_c = _moe_dense_call(pay_c, wgu_l, wd_l, dense_steps // 2, dense_steps - dense_steps // 2,
                                  acc=y_c)
            h_ctx = _residual_call(h1_c, y_c, cg2, tc)

    return _residual_call(*lat_stream, _row_tile(n, RESIDUAL_TILE))[None]
```

```python
import functools

import jax
import jax.numpy as jnp
import numpy as np
from jax import lax
from jax.experimental import pallas as pl
from jax.experimental.pallas import tpu as pltpu
from jax.experimental.pallas import tpu_sc as plsc

D_MODEL = 1024
GRID_W = 64
HEAD_DIM = 64
POOL_WINDOWS = (2, 4, 8, 16)
POOL_DIM = 256
NA_HEADS = 8
NA_DIM = 512
NA_WIN_ROWS = 8
NA_WIN_COLS = 16
SG_DIM = 256
SG_CHUNK = 128
Q_OFF = POOL_DIM
K_OFF = Q_OFF + NA_DIM
V_OFF = K_OFF + NA_DIM
U_OFF = V_OFF + NA_DIM
G_OFF = U_OFF + SG_DIM
IN_DIM = G_OFF + SG_DIM
N_EXPERTS = 16
GROUP_SIZE = 4
D_EXPERT = 512
EPS = 1e-6

LANES = 128
SUBLANES = 8
HEAD_PAIRS = NA_DIM // LANES
VMEM_LIMIT = 48 * 1024 * 1024

PAIRS = ((0, 1), (0, 2), (1, 2), (1, 3), (0, 3), (2, 3))
N_GROUPS = N_EXPERTS // GROUP_SIZE
N_CLASSES = N_GROUPS * len(PAIRS)
CLASS_E1 = np.array([GROUP_SIZE * g + i for g in range(N_GROUPS) for (i, j) in PAIRS], np.int32)
CLASS_E2 = np.array([GROUP_SIZE * g + j for g in range(N_GROUPS) for (i, j) in PAIRS], np.int32)

ROUTE_ROWS = 8
HALF_D = D_MODEL // 2
PAYLOAD_W = HALF_D + LANES
MOE_TM = 512
NEG_BIG = -1e30
LOG2_E = 1.4426950408889634


def _cparams(sem):
    return pltpu.CompilerParams(dimension_semantics=sem, vmem_limit_bytes=VMEM_LIMIT)


def _dot(a, b):
    return jnp.dot(a, b, preferred_element_type=jnp.float32)


def _dot_nt(a, b):
    return lax.dot_general(a, b, (((1,), (1,)), ((), ())), preferred_element_type=jnp.float32)


def _gelu_tanh(x):
    k1, k2 = -2.0 * 0.7978845608028654, -2.0 * 0.7978845608028654 * 0.044715
    return x * (1.0 / (1.0 + jnp.exp(x * (k1 + k2 * (x * x)))))


def _silu(x):
    return x * (1.0 / (1.0 + jnp.exp(-x)))


def _pack_halves(x):
    w = x.shape[1] // 2
    lo = pltpu.bitcast(x[:, :w].astype(jnp.bfloat16).astype(jnp.float32), jnp.uint32) >> 16
    hi = pltpu.bitcast(x[:, w:].astype(jnp.bfloat16).astype(jnp.float32), jnp.uint32) & jnp.uint32(0xFFFF0000)
    return lo | hi


def _unpack_halves(words):
    lo = pltpu.bitcast(words << 16, jnp.float32)
    hi = pltpu.bitcast(words & jnp.uint32(0xFFFF0000), jnp.float32)
    return lo, hi


def _ada_kernel(cond_ref, w_ref, b_ref, o_ref):
    cond = _silu(cond_ref[...])
    w = w_ref[...]
    rows = [jnp.sum(w * cond[:, r:r + 1], axis=0, keepdims=True) + b_ref[...] for r in range(2)]
    o_ref[...] = jnp.concatenate(rows + [jnp.zeros((SUBLANES - 2, w.shape[1]), jnp.float32)], axis=0)


def _ada_call(cond, w_ada, b_ada):
    depth = w_ada.shape[0]
    tn = 1536
    return pl.pallas_call(
        _ada_kernel,
        grid=(depth, 6 * D_MODEL // tn),
        in_specs=[
            pl.BlockSpec((D_MODEL, 2), lambda l, j: (0, 0)),
            pl.BlockSpec((None, D_MODEL, tn), lambda l, j: (l, 0, j)),
            pl.BlockSpec((None, 1, tn), lambda l, j: (l, 0, j)),
        ],
        out_specs=pl.BlockSpec((None, SUBLANES, tn), lambda l, j: (l, 0, j)),
        out_shape=jax.ShapeDtypeStruct((depth, SUBLANES, 6 * D_MODEL), jnp.float32),
        compiler_params=_cparams(("arbitrary", "arbitrary")),
        name="adaln",
    )(cond, w_ada, b_ada.reshape(depth, 1, 6 * D_MODEL))


def _norm_modulate(x, n_ref, sh_ref, sc_ref):
    ms = jnp.mean(x * x, axis=-1, keepdims=True)
    return ((x * lax.rsqrt(ms + EPS)) * (n_ref[...] * (1.0 + sc_ref[...])) + sh_ref[...]).astype(jnp.bfloat16)


def _head_rms_scale(a):
    low = lax.broadcasted_iota(jnp.int32, (a.shape[0], LANES), 1) < HEAD_DIM
    blocks = []
    for p in range(a.shape[1] // LANES):
        sq = jnp.square(a[:, p * LANES:(p + 1) * LANES])
        s_lo = jnp.sum(jnp.where(low, sq, 0.0), axis=-1, keepdims=True)
        s_hi = jnp.sum(jnp.where(low, 0.0, sq), axis=-1, keepdims=True)
        blocks.append(lax.rsqrt(jnp.where(low, s_lo, s_hi) * (1.0 / HEAD_DIM) + EPS))
    return jnp.concatenate(blocks, axis=-1)


STRIP_W = 8
N_STRIPS = GRID_W // STRIP_W


def _store_keys(ref, x):
    if len(ref.shape) == 2:
        ref[...] = x.astype(jnp.bfloat16)
        return
    pair = 2 * STRIP_W
    for s in range(N_STRIPS):
        for rp in range(x.shape[0] // (2 * GRID_W)):
            top = 2 * rp * GRID_W + s * STRIP_W
            rows = jnp.concatenate([x[top:top + STRIP_W], x[top + GRID_W:top + GRID_W + STRIP_W]], axis=0)
            ref[s, rp * pair:(rp + 1) * pair, :] = rows.astype(jnp.bfloat16)


POOL_HALO = 8
HALO_BLOCK = 16
STREAM_DTYPE = jnp.bfloat16


POOL_EDGE = 16


def _pool_mix(xe_ref, w_ref, scale_ref, tm, seq_len):
    low = lax.broadcasted_iota(jnp.int32, (tm, LANES), 1) < HEAD_DIM
    t_edge = pl.program_id(0) * tm + lax.broadcasted_iota(jnp.int32, (POOL_EDGE, LANES), 0)

    def window_mean(s, half):
        mean = s * (1.0 / (2 * half))

        def clip_fix(t):
            count = (jnp.minimum(t + half, seq_len) - jnp.maximum(t - half, 0)).astype(jnp.float32)
            return (2.0 * half) / count

        return jnp.concatenate([mean[:POOL_EDGE] * clip_fix(t_edge),
                                mean[POOL_EDGE:tm - POOL_EDGE],
                                mean[tm - POOL_EDGE:] * clip_fix(t_edge + (tm - POOL_EDGE))], axis=0)

    def window_sums(xs, n_levels):
        sums = []
        s = xs
        for k in range(n_levels):
            step = 1 << k
            s = s[:-step] + s[step:]
            sums.append(s)
        return sums

    outs = []
    for half_block, windows in enumerate(((2, 4), (8, 16))):
        xs = xe_ref[:, half_block * LANES:(half_block + 1) * LANES]
        sums = window_sums(xs, int(np.log2(windows[1])))
        parts = []
        for w in windows:
            half = w // 2
            s = sums[int(np.log2(w)) - 1][POOL_HALO - half:POOL_HALO - half + tm]
            parts.append(window_mean(s, half))
        mean = jnp.where(low, parts[0], parts[1])
        outs.append(mean - xs[POOL_HALO:POOL_HALO + tm])
    d = jnp.concatenate(outs, axis=-1).astype(jnp.bfloat16)
    return (_dot(d, w_ref[...]) * scale_ref[...]).astype(jnp.bfloat16)


def _inproj_kernel(*refs, pending, seq_len):
    n_stream = 7 if pending else 3
    stream, refs = refs[:n_stream], refs[n_stream:]
    if pending:
        h_ref, y_ref, g_ref, hp_ref, hn_ref, yp_ref, yn_ref = stream
        hres_ref, refs = refs[-2], refs[:-2] + refs[-1:]
        x = h_ref[...].astype(jnp.float32) + g_ref[...] * _moe_out(y_ref)
        hres_ref[...] = x.astype(hres_ref.dtype)
        before = hp_ref[...].astype(jnp.float32) + g_ref[...] * _moe_out(yp_ref)
        after = hn_ref[...].astype(jnp.float32) + g_ref[...] * _moe_out(yn_ref)
    else:
        h_ref, hp_ref, hn_ref = stream
        x = h_ref[...].astype(jnp.float32)
        before = hp_ref[...].astype(jnp.float32)
        after = hn_ref[...].astype(jnp.float32)
    x_halo = jnp.concatenate([before[HALO_BLOCK - POOL_HALO:], after[:POOL_HALO]], axis=0)
    (n1_ref, sh_ref, sc_ref, w_ref, qg_ref, kg_ref, sgn_ref, sgw_ref, sgb_ref,
     pw_ref, ps_ref, pool_ref, q_ref, k_ref, v_ref, sg_ref, xe_ref) = refs
    tm = h_ref.shape[0]
    i = pl.program_id(0)
    hn = _norm_modulate(x, n1_ref, sh_ref, sc_ref)
    hn_halo = _norm_modulate(x_halo, n1_ref, sh_ref, sc_ref)

    a_halo = _dot(hn_halo, w_ref[:, 0:Q_OFF])
    xe_ref[0:POOL_HALO, :] = jnp.where(i > 0, a_halo[:POOL_HALO], 0.0)
    xe_ref[POOL_HALO:POOL_HALO + tm, :] = _dot(hn, w_ref[:, 0:Q_OFF])
    xe_ref[POOL_HALO + tm:, :] = jnp.where(i < pl.num_programs(0) - 1, a_halo[POOL_HALO:], 0.0)

    a_g = _dot(hn, w_ref[:, G_OFF:IN_DIM])
    a_u = _dot(hn, w_ref[:, U_OFF:G_OFF])
    a_q = _dot(hn, w_ref[:, Q_OFF:K_OFF])
    a_k = _dot(hn, w_ref[:, K_OFF:V_OFF])
    _store_keys(v_ref, _dot(hn, w_ref[:, V_OFF:U_OFF]))

    gv = _gelu_tanh(a_g)
    q_ref[...] = (a_q * _head_rms_scale(a_q) * qg_ref[...]).astype(jnp.bfloat16)
    _store_keys(k_ref, a_k * _head_rms_scale(a_k) * kg_ref[...])

    u = _gelu_tanh(a_u)
    vn = (gv * _head_rms_scale(gv) * sgn_ref[...]).astype(jnp.bfloat16)
    low = lax.broadcasted_iota(jnp.int32, (SG_CHUNK, LANES), 1) < HEAD_DIM
    for c in range(tm // SG_CHUNK):
        rows = slice(c * SG_CHUNK, (c + 1) * SG_CHUNK)
        for s in range(SG_DIM // LANES):
            cols = slice(s * LANES, (s + 1) * LANES)
            m = _dot(sgw_ref[s], vn[rows, cols]) + sgb_ref[s]
            mixed = jnp.where(low, m[:SG_CHUNK], m[SG_CHUNK:])
            sg_ref[rows, cols] = (u[rows, cols] * mixed).astype(jnp.bfloat16)

    pool_ref[...] = _pool_mix(xe_ref, pw_ref, ps_ref, tm, seq_len)


def _inproj_call(stream, n1, sh, sc, w_in, layer, qg, kg, sgn, sgw, sgb, pool_w, pool_scale, tm, strips):
    pending = len(stream) == 3
    n = stream[0].shape[0]
    assert n % tm == 0 and tm % SG_CHUNK == 0 and tm % (2 * GRID_W) == 0
    if strips:
        kv_spec = pl.BlockSpec((N_STRIPS, tm // N_STRIPS, NA_DIM), lambda i: (0, i, 0))
        kv_shape = jax.ShapeDtypeStruct((N_STRIPS, n // N_STRIPS, NA_DIM), jnp.bfloat16)
    else:
        kv_spec = pl.BlockSpec((tm, NA_DIM), lambda i: (i, 0))
        kv_shape = jax.ShapeDtypeStruct((n, NA_DIM), jnp.bfloat16)
    row = lambda i: (i, 0)
    fixed2 = lambda i: (0, 0)
    fixed3 = lambda i: (0, 0, 0)
    vec = lambda w: pl.BlockSpec((1, w), fixed2)
    rows = pl.BlockSpec((tm, D_MODEL), row)
    per_tile = tm // HALO_BLOCK
    before = lambda i: (jnp.maximum(i * per_tile - 1, 0), 0)
    after = lambda i: (jnp.minimum((i + 1) * per_tile, n // HALO_BLOCK - 1), 0)
    halo = lambda w, m: pl.BlockSpec((HALO_BLOCK, w), m)
    h = stream[0]
    if pending:
        y, g = stream[1], stream[2]
        yw = y.shape[1]
        args = [h, y, g, h, h, y, y]
        stream_specs = [rows, pl.BlockSpec((tm, yw), row), vec(D_MODEL),
                        halo(D_MODEL, before), halo(D_MODEL, after), halo(yw, before), halo(yw, after)]
    else:
        args = [h, h, h]
        stream_specs = [rows, halo(D_MODEL, before), halo(D_MODEL, after)]
    extra_out_specs = [rows] if pending else []
    extra_out_shape = [jax.ShapeDtypeStruct((n, D_MODEL), STREAM_DTYPE)] if pending else []
    return pl.pallas_call(
        functools.partial(_inproj_kernel, pending=pending, seq_len=n),
        grid=(n // tm,),
        in_specs=stream_specs + [
            vec(D_MODEL), vec(D_MODEL), vec(D_MODEL),
            pl.BlockSpec((None, D_MODEL, IN_DIM), lambda i: (layer, 0, 0)),
            vec(NA_DIM), vec(NA_DIM),
            vec(SG_DIM),
            pl.BlockSpec((SG_DIM // LANES, 2 * SG_CHUNK, SG_CHUNK), fixed3),
            pl.BlockSpec((SG_DIM // LANES, 2 * SG_CHUNK, LANES), fixed3),
            pl.BlockSpec((POOL_DIM, POOL_DIM), fixed2),
            vec(POOL_DIM),
        ],
        out_specs=[
            pl.BlockSpec((tm, POOL_DIM), row),
            pl.BlockSpec((tm, NA_DIM), row),
            kv_spec,
            kv_spec,
            pl.BlockSpec((tm, SG_DIM), row),
        ] + extra_out_specs,
        out_shape=[
            jax.ShapeDtypeStruct((n, POOL_DIM), jnp.bfloat16),
            jax.ShapeDtypeStruct((n, NA_DIM), jnp.bfloat16),
            kv_shape,
            kv_shape,
            jax.ShapeDtypeStruct((n, SG_DIM), jnp.bfloat16),
        ] + extra_out_shape,
        scratch_shapes=[pltpu.VMEM((tm + 2 * POOL_HALO, POOL_DIM), jnp.float32)],
        compiler_params=_cparams(("arbitrary",)),
        name="inproj",
    )(*args, n1, sh, sc, w_in, qg, kg, sgn, sgw, sgb, pool_w, pool_scale)


NA_ROWS_PER_BLOCK = 64
NA_GROUP_ROWS = 4
NA_WINDOW_ROWS = NA_GROUP_ROWS + NA_WIN_ROWS
NA_BLOCK = NA_ROWS_PER_BLOCK * GRID_W
NA_GROUP = NA_GROUP_ROWS * GRID_W
NA_HALF_COLS = GRID_W // 2
NA_CHAIN = NA_GROUP_ROWS * NA_HALF_COLS
NA_HALF_STRIPS = 5
NA_HALF_COL0 = (0, GRID_W - NA_HALF_STRIPS * STRIP_W)
NA_RUN = NA_WINDOW_ROWS * STRIP_W
NA_LOCAL = 512
NA_STRIP_BLOCK = NA_ROWS_PER_BLOCK * STRIP_W
NA_STRIP_HALO = (NA_WIN_ROWS // 2) * STRIP_W
NA_EDGE_FIRST, NA_EDGE_NONE, NA_EDGE_LAST = 0, 1, 2


def _stack_heads(x, low):
    zero = jnp.zeros_like(x)
    return jnp.concatenate([jnp.where(low, x, zero), jnp.where(low, zero, x)], axis=0)


def _natten_kernel(q_ref, kp_ref, kc_ref, kn_ref, vp_ref, vc_ref, vn_ref, kx_ref, vx_ref, bias_ref,
                   wg_ref, wu_ref, wd_ref,
                   o_ref, wgu_bf_ref, wd_bf_ref, kwin_ref, vwin_ref, vxe_ref, *, grid_rows):
    b = pl.program_id(1)
    wgu_bf_ref[:, 0:D_EXPERT] = wg_ref[...].astype(jnp.bfloat16)
    wgu_bf_ref[:, D_EXPERT:] = wu_ref[...].astype(jnp.bfloat16)
    wd_bf_ref[...] = wd_ref[...].astype(jnp.bfloat16)
    top, bottom = NA_STRIP_HALO, NA_STRIP_HALO + NA_STRIP_BLOCK
    kwin_ref[:, 0:top, :] = kp_ref[...]
    kwin_ref[:, top:bottom, :] = kc_ref[...]
    kwin_ref[:, bottom:, :] = kn_ref[...]
    vwin_ref[:, 0:top, 0:LANES] = vp_ref[...]
    vwin_ref[:, top:bottom, 0:LANES] = vc_ref[...]
    vwin_ref[:, bottom:, 0:LANES] = vn_ref[...]
    vwin_ref[:, :, LANES:] = jnp.ones(vwin_ref.shape[:2] + (LANES,), jnp.bfloat16)
    vxe_ref[:, 0:LANES] = vx_ref[...]
    vxe_ref[:, LANES:] = jnp.ones((vxe_ref.shape[0], LANES), jnp.bfloat16)
    low_q = lax.broadcasted_iota(jnp.int32, (NA_CHAIN, LANES), 1) < HEAD_DIM
    n_pad = NA_LOCAL - NA_HALF_STRIPS * NA_RUN

    def window_start(g):
        r0 = b * NA_ROWS_PER_BLOCK + g * NA_GROUP_ROWS
        ws = jnp.clip(r0 - NA_WIN_ROWS // 2, 0, grid_rows - NA_WINDOW_ROWS)
        edge = jnp.where(r0 == 0, NA_EDGE_FIRST,
                         jnp.where(r0 == grid_rows - NA_GROUP_ROWS, NA_EDGE_LAST, NA_EDGE_NONE))
        start = pl.multiple_of((ws - b * NA_ROWS_PER_BLOCK + NA_WIN_ROWS // 2) * STRIP_W, NA_STRIP_HALO)
        return start, edge

    def local_window(win_ref, g, half):
        start, _ = window_start(g)
        s0 = NA_HALF_COL0[half] // STRIP_W
        runs = [win_ref[s, pl.ds(start, NA_RUN), :] for s in range(s0, s0 + NA_HALF_STRIPS)]
        return jnp.concatenate(runs + [jnp.zeros((n_pad, win_ref.shape[2]), jnp.bfloat16)], axis=0)

    def query_rows(g, half, j):
        first = g * NA_GROUP + j * GRID_W + half * NA_HALF_COLS
        return slice(first, first + NA_HALF_COLS)

    def scores(c):
        g, half = divmod(c, 2)
        _, edge = window_start(g)
        qh = jnp.concatenate([q_ref[query_rows(g, half, j), :] for j in range(NA_GROUP_ROWS)], axis=0)
        lhs = _stack_heads(qh, low_q)
        kl = local_window(kwin_ref, g, half)
        return jnp.concatenate([_dot_nt(lhs, kl).astype(jnp.bfloat16) + bias_ref[edge, half],
                                _dot_nt(lhs, kx_ref[...]).astype(jnp.bfloat16)], axis=-1)

    n_chains = 2 * (NA_ROWS_PER_BLOCK // NA_GROUP_ROWS)
    s_next = scores(0)
    for c in range(n_chains):
        s = s_next
        if c + 1 < n_chains:
            s_next = scores(c + 1)
        g, half = divmod(c, 2)
        vl = local_window(vwin_ref, g, half)
        m = jnp.max(s, axis=-1, keepdims=True)
        pb = jnp.exp2(s - m)
        o = _dot(pb[:, :NA_LOCAL], vl) + _dot(pb[:, NA_LOCAL:], vxe_ref[...])
        o = o[:, :LANES] * (1.0 / o[:, LANES:])
        o = jnp.where(low_q, o[:NA_CHAIN], o[NA_CHAIN:]).astype(jnp.bfloat16)
        for j in range(NA_GROUP_ROWS):
            o_ref[query_rows(g, half, j), :] = o[j * NA_HALF_COLS:(j + 1) * NA_HALF_COLS]


def _natten_call(q, k, v, k_ctx, v_ctx, bias, w_gate, w_up, w_down, layer):
    n = q.shape[0]
    grid_rows = n // GRID_W
    assert grid_rows % NA_ROWS_PER_BLOCK == 0 and grid_rows >= 2 * NA_ROWS_PER_BLOCK
    nblk = n // NA_BLOCK
    steps = HEAD_PAIRS * nblk
    depth = w_gate.shape[0]
    up_rows = N_EXPERTS * D_MODEL
    down_rows = N_EXPERTS * D_EXPERT
    assert up_rows % steps == 0 and down_rows % steps == 0
    wg2 = w_gate.reshape(depth * up_rows, D_EXPERT)
    wu2 = w_up.reshape(depth * up_rows, D_EXPERT)
    wd2 = w_down.reshape(depth * down_rows, D_MODEL)
    up_in = pl.BlockSpec((up_rows // steps, D_EXPERT), lambda p, b: (layer * steps + p * nblk + b, 0))
    down_in = pl.BlockSpec((down_rows // steps, D_MODEL), lambda p, b: (layer * steps + p * nblk + b, 0))
    up_out = pl.BlockSpec((up_rows // steps, 2 * D_EXPERT), lambda p, b: (p * nblk + b, 0))
    down_out = pl.BlockSpec((down_rows // steps, D_MODEL), lambda p, b: (p * nblk + b, 0))
    n_halo = n // N_STRIPS // NA_STRIP_HALO
    hb = NA_STRIP_BLOCK // NA_STRIP_HALO
    rows = pl.BlockSpec((NA_BLOCK, LANES), lambda p, b: (b, p))
    cur = pl.BlockSpec((N_STRIPS, NA_STRIP_BLOCK, LANES), lambda p, b: (0, b, p))
    prev = pl.BlockSpec((N_STRIPS, NA_STRIP_HALO, LANES), lambda p, b: (0, jnp.maximum(b * hb - 1, 0), p))
    nxt = pl.BlockSpec((N_STRIPS, NA_STRIP_HALO, LANES),
                       lambda p, b: (0, jnp.minimum((b + 1) * hb, n_halo - 1), p))
    ctx = pl.BlockSpec((k_ctx.shape[0], LANES), lambda p, b: (0, p))
    win_rows = NA_STRIP_BLOCK + 2 * NA_STRIP_HALO
    att, wgu_bf, wd_bf = pl.pallas_call(
        functools.partial(_natten_kernel, grid_rows=grid_rows),
        grid=(HEAD_PAIRS, nblk),
        in_specs=[rows, prev, cur, nxt, prev, cur, nxt, ctx, ctx,
                  pl.BlockSpec((None, 3, 2, 2 * NA_CHAIN, NA_LOCAL), lambda p, b: (p, 0, 0, 0, 0)),
                  up_in, up_in, down_in],
        out_specs=[rows, up_out, down_out],
        out_shape=[jax.ShapeDtypeStruct((n, NA_DIM), jnp.bfloat16),
                   jax.ShapeDtypeStruct((up_rows, 2 * D_EXPERT), jnp.bfloat16),
                   jax.ShapeDtypeStruct((down_rows, D_MODEL), jnp.bfloat16)],
        scratch_shapes=[pltpu.VMEM((N_STRIPS, win_rows, LANES), jnp.bfloat16),
                        pltpu.VMEM((N_STRIPS, win_rows, 2 * LANES), jnp.bfloat16),
                        pltpu.VMEM((k_ctx.shape[0], 2 * LANES), jnp.bfloat16)],
        compiler_params=_cparams(("arbitrary", "arbitrary")),
        name="natten",
    )(q, k, k, k, v, v, v, k_ctx, v_ctx, bias, wg2, wu2, wd2)
    return (att, wgu_bf.reshape(N_EXPERTS, D_MODEL, 2 * D_EXPERT), wd_bf.reshape(N_EXPERTS, D_EXPERT, D_MODEL))


def _natten_bias(rpb):
    cols = np.arange(GRID_W)
    col_start = np.clip(cols - NA_WIN_COLS // 2, 0, GRID_W - NA_WIN_COLS)
    kc = np.arange(GRID_W)
    in_win = (kc[None, :] >= col_start[:, None]) & (kc[None, :] < col_start[:, None] + NA_WIN_COLS)
    dc = kc[None, :] - cols[:, None] + NA_WIN_COLS - 1
    sel = (np.arange(2 * NA_WIN_COLS - 1)[:, None, None] == dc[None]) & in_win[None]
    t2 = jnp.einsum("hdj,jqk->hdqk", rpb, jnp.asarray(sel, jnp.float32), precision=lax.Precision.HIGHEST)
    t2 = jnp.where(in_win[None, None], t2 * LOG2_E, NEG_BIG)
    neg = jnp.full((NA_HEADS, 1, GRID_W, GRID_W), NEG_BIG, jnp.float32)
    t2e = jnp.concatenate([neg, t2, neg], axis=1)
    u = jnp.concatenate([t2e[:, :-1], t2e[:, 1:]], axis=-1)
    u = u.reshape(HEAD_PAIRS, 2, 2 * NA_WIN_ROWS, GRID_W, LANES)

    place = np.zeros((2, NA_WINDOW_ROWS * GRID_W, NA_LOCAL), np.float32)
    for half, c0 in enumerate(NA_HALF_COL0):
        for a in range(NA_WINDOW_ROWS):
            for kcol in range(c0, c0 + NA_HALF_STRIPS * STRIP_W):
                s, c8 = divmod(kcol - c0, STRIP_W)
                place[half, a * GRID_W + kcol, s * NA_RUN + a * STRIP_W + c8] = 1.0
    outside = np.full((3, NA_GROUP_ROWS, NA_LOCAL), NEG_BIG, np.float32)
    for edge in (NA_EDGE_FIRST, NA_EDGE_NONE, NA_EDGE_LAST):
        for j in range(NA_GROUP_ROWS):
            lo, _ = _window_rows(edge, j)
            for s in range(NA_HALF_STRIPS):
                outside[edge, j, s * NA_RUN + lo * STRIP_W:s * NA_RUN + (lo + NA_WIN_ROWS) * STRIP_W] = 0.0
    return pl.pallas_call(
        _bias_expand_kernel,
        grid=(HEAD_PAIRS,),
        in_specs=[pl.BlockSpec((None, 2, 2 * NA_WIN_ROWS, GRID_W, LANES), lambda p: (p, 0, 0, 0, 0)),
                  pl.BlockSpec(place.shape, lambda p: (0, 0, 0)),
                  pl.BlockSpec(outside.shape, lambda p: (0, 0, 0))],
        out_specs=pl.BlockSpec((None, 3, 2, 2 * NA_CHAIN, NA_LOCAL), lambda p: (p, 0, 0, 0, 0)),
        out_shape=jax.ShapeDtypeStruct((HEAD_PAIRS, 3, 2, 2 * NA_CHAIN, NA_LOCAL), jnp.bfloat16),
        compiler_params=_cparams(("arbitrary",)),
        name="bias_expand",
    )(u, jnp.asarray(place, jnp.bfloat16), jnp.asarray(outside))


def _window_rows(edge, j):
    if edge == NA_EDGE_FIRST:
        return 0, NA_WIN_ROWS - 1 - j
    if edge == NA_EDGE_NONE:
        return j, NA_WIN_ROWS // 2 - 1
    return NA_WINDOW_ROWS - NA_WIN_ROWS, NA_WIN_ROWS // 2 - 1 - j


def _bias_expand_kernel(u_ref, place_ref, outside_ref, o_ref):
    low = lax.broadcasted_iota(jnp.int32, (NA_HALF_COLS, LANES), 1) < GRID_W
    zero = jnp.zeros((NA_HALF_COLS, LANES), jnp.float32)
    for edge in (NA_EDGE_FIRST, NA_EDGE_NONE, NA_EDGE_LAST):
        for half in range(2):
            q0 = half * NA_HALF_COLS
            blocks, masks = [], []
            for hd in range(2):
                for j in range(NA_GROUP_ROWS):
                    lo, base = _window_rows(edge, j)
                    tiles = []
                    for i in range(NA_WINDOW_ROWS // 2):
                        a0, a1 = 2 * i, 2 * i + 1
                        ok0 = lo <= a0 < lo + NA_WIN_ROWS
                        ok1 = lo <= a1 < lo + NA_WIN_ROWS
                        if not (ok0 or ok1):
                            tile = zero
                        else:
                            tile = u_ref[hd, base + a1 - lo, q0:q0 + NA_HALF_COLS, :]
                            if not ok0:
                                tile = jnp.where(low, zero, tile)
                            if not ok1:
                                tile = jnp.where(low, tile, zero)
                        tiles.append(tile)
                    blocks.append(jnp.concatenate(tiles, axis=-1))
                    masks.append(jnp.broadcast_to(outside_ref[edge, j:j + 1, :], (NA_HALF_COLS, NA_LOCAL)))
            lhs = jnp.concatenate(blocks, axis=0).astype(jnp.bfloat16)
            placed = _dot(lhs, place_ref[half]) + jnp.concatenate(masks, axis=0)
            o_ref[edge, half] = placed.astype(o_ref.dtype)


def _ctxatt_kernel(q_ref, k_ref, v_ref, o_ref):
    lc = q_ref.shape[0]
    low = lax.broadcasted_iota(jnp.int32, (lc, LANES), 1) < HEAD_DIM
    lhs = _stack_heads(q_ref[...], low)
    s = _dot_nt(lhs, k_ref[...])
    m = jnp.max(s, axis=-1, keepdims=True)
    p = jnp.exp2(s - m)
    denom = jnp.sum(p, axis=-1, keepdims=True)
    o = _dot(p.astype(jnp.bfloat16), v_ref[...]) * (1.0 / denom)
    o_ref[...] = jnp.where(low, o[:lc], o[lc:]).astype(jnp.bfloat16)


def _ctxatt_call(q, k, v):
    lc = q.shape[0]
    spec = pl.BlockSpec((lc, LANES), lambda p: (0, p))
    return pl.pallas_call(
        _ctxatt_kernel,
        grid=(HEAD_PAIRS,),
        in_specs=[spec, spec, spec],
        out_specs=spec,
        out_shape=jax.ShapeDtypeStruct((lc, NA_DIM), jnp.bfloat16),
        compiler_params=_cparams(("arbitrary",)),
        name="ctxatt",
    )(q, k, v)


OUTPROJ_CHAIN = 256
OUTPROJ_TILE = 1024
INPROJ_TILE = 1024
RESIDUAL_TILE = 2048


def _outproj_kernel(h_ref, mp_ref, att_ref, sg_ref, wo_ref, g1_ref, n2_ref, sh_ref, sc_ref,
                    wr_ref, br_ref, h1_ref, pay_ref, route_ref):
    chains = [slice(c * OUTPROJ_CHAIN, (c + 1) * OUTPROJ_CHAIN) for c in range(h_ref.shape[0] // OUTPROJ_CHAIN)]
    h1s = []
    for rows in chains:
        mix = (_dot(mp_ref[rows, :], wo_ref[0:POOL_DIM, :])
               + _dot(att_ref[rows, :], wo_ref[POOL_DIM:POOL_DIM + NA_DIM, :])
               + _dot(sg_ref[rows, :], wo_ref[POOL_DIM + NA_DIM:, :]))
        h1 = h_ref[rows, :].astype(jnp.float32) + g1_ref[...] * mix
        h1_ref[rows, :] = h1.astype(h1_ref.dtype)
        h1s.append(h1)
    for rows, h1 in zip(chains, h1s):
        _outproj_route(rows, h1, n2_ref, sh_ref, sc_ref, wr_ref, br_ref, pay_ref, route_ref)


def _outproj_route(rows, h1, n2_ref, sh_ref, sc_ref, wr_ref, br_ref, pay_ref, route_ref):
    tm = OUTPROJ_CHAIN
    ms = jnp.mean(h1 * h1, axis=-1, keepdims=True)
    hm = (h1 * lax.rsqrt(ms + EPS)) * (n2_ref[...] * (1.0 + sc_ref[...])) + sh_ref[...]
    pay_ref[rows, 0:HALF_D] = _pack_halves(hm)

    hm_hi = hm.astype(jnp.bfloat16)
    lt = _dot_nt(wr_ref[...], hm_hi)
    logits = lt[:N_EXPERTS] + lt[N_EXPERTS:] + br_ref[...]
    e = jnp.exp(logits - jnp.max(logits, axis=0, keepdims=True))

    best = ga = gb = e1 = e2 = cls = None
    for c in range(N_CLASSES):
        a, b2 = int(CLASS_E1[c]), int(CLASS_E2[c])
        ea, eb = e[a:a + 1, :], e[b2:b2 + 1, :]
        s = ea + eb
        if best is None:
            best, ga, gb = s, ea, eb
            e1 = jnp.full_like(s, float(a))
            e2 = jnp.full_like(s, float(b2))
            cls = jnp.zeros_like(s)
        else:
            better = s > best
            best = jnp.where(better, s, best)
            ga = jnp.where(better, ea, ga)
            gb = jnp.where(better, eb, gb)
            e1 = jnp.where(better, float(a), e1)
            e2 = jnp.where(better, float(b2), e2)
            cls = jnp.where(better, float(c), cls)
    inv = 1.0 / best
    row = lax.broadcasted_iota(jnp.int32, (ROUTE_ROWS, tm), 0)
    rec = jnp.where(row == 0, ga * inv,
          jnp.where(row == 1, gb * inv,
          jnp.where(row == 2, e1,
          jnp.where(row == 3, e2,
          jnp.where(row == 4, cls, 0.0)))))
    route_ref[:, rows] = rec
    wide = jnp.concatenate([rec, jnp.zeros((LANES - ROUTE_ROWS, tm), jnp.float32)], axis=0)
    pay_ref[rows, HALF_D:] = pltpu.bitcast(wide.T, jnp.uint32)


def _outproj_call(h, mp, att, sg, w_out, layer, g1, n2, sh2, sc2, wr, br, tm):
    n = h.shape[0]
    assert n % tm == 0 and tm % OUTPROJ_CHAIN == 0
    row = lambda i: (i, 0)
    fixed = lambda i: (0, 0)
    vec = pl.BlockSpec((1, D_MODEL), fixed)
    return pl.pallas_call(
        _outproj_kernel,
        grid=(n // tm,),
        in_specs=[
            pl.BlockSpec((tm, D_MODEL), row),
            pl.BlockSpec((tm, POOL_DIM), row),
            pl.BlockSpec((tm, NA_DIM), row),
            pl.BlockSpec((tm, SG_DIM), row),
            pl.BlockSpec((None, D_MODEL, D_MODEL), lambda i: (layer, 0, 0)),
            vec, vec, vec, vec,
            pl.BlockSpec((2 * N_EXPERTS, D_MODEL), fixed),
            pl.BlockSpec((N_EXPERTS, 1), fixed),
        ],
        out_specs=[
            pl.BlockSpec((tm, D_MODEL), row),
            pl.BlockSpec((tm, PAYLOAD_W), row),
            pl.BlockSpec((ROUTE_ROWS, tm), lambda i: (0, i)),
        ],
        out_shape=[
            jax.ShapeDtypeStruct((n, D_MODEL), STREAM_DTYPE),
            jax.ShapeDtypeStruct((n, PAYLOAD_W), jnp.uint32),
            jax.ShapeDtypeStruct((ROUTE_ROWS, n), jnp.float32),
        ],
        compiler_params=_cparams(("arbitrary",)),
        name="outproj",
    )(h, mp, att, sg, w_out, g1, n2, sh2, sc2, wr, br)


def _payload_parts(pay_ref, rows=slice(None)):
    lo, hi = _unpack_halves(pay_ref[rows, 0:HALF_D])
    x = jnp.concatenate([lo, hi], axis=-1).astype(jnp.bfloat16)
    return x, pltpu.bitcast(pay_ref[rows, HALF_D:], jnp.float32)


def _expert_hidden(x, wgu, gate):
    gu = _dot(x, wgu)
    return (_silu(gu[:, :D_EXPERT]) * gu[:, D_EXPERT:] * gate).astype(jnp.bfloat16)


def _expert_pair(x, ga, gb, wgua, wda, wgub, wdb):
    ha = _expert_hidden(x, wgua, ga)
    hb = _expert_hidden(x, wgub, gb)
    return _dot(ha, wda) + _dot(hb, wdb)


MOE_PART = MOE_TM // 8


def _moe_sorted_kernel(blk_ref, e1_ref, e2_ref, parts_ref, pay_ref,
                       wgua_ref, wda_ref, wgub_ref, wdb_ref, o_ref):
    parts = parts_ref[pl.program_id(0)]

    def run(rows):
        x, route = _payload_parts(pay_ref, rows)
        y = _expert_pair(x, route[:, 0:1], route[:, 1:2], wgua_ref[...], wda_ref[...],
                         wgub_ref[...], wdb_ref[...])
        o_ref[rows, :] = _pack_halves(y)

    for used in range(1, MOE_TM // MOE_PART + 1):
        @pl.when(parts == used)
        def _(used=used):
            run(slice(0, used * MOE_PART))


def _moe_sorted_call(blk, e1, e2, parts, pay_sorted, wgu, wd):
    n_tiles = blk.shape[0]
    rows = lambda i, blk, e1, e2, parts: (blk[i], 0)
    wa = lambda i, blk, e1, e2, parts: (e1[i], 0, 0)
    wb = lambda i, blk, e1, e2, parts: (e2[i], 0, 0)
    up = lambda m: pl.BlockSpec((None, D_MODEL, 2 * D_EXPERT), m)
    down = lambda m: pl.BlockSpec((None, D_EXPERT, D_MODEL), m)
    return pl.pallas_call(
        _moe_sorted_kernel,
        grid_spec=pltpu.PrefetchScalarGridSpec(
            num_scalar_prefetch=4,
            grid=(n_tiles,),
            in_specs=[pl.BlockSpec((MOE_TM, PAYLOAD_W), rows),
                      up(wa), down(wa), up(wb), down(wb)],
            out_specs=pl.BlockSpec((MOE_TM, HALF_D), rows),
        ),
        out_shape=jax.ShapeDtypeStruct((n_tiles * MOE_TM, HALF_D), jnp.uint32),
        compiler_params=_cparams(("arbitrary",)),
        name="moe_sorted",
    )(blk, e1, e2, parts, pay_sorted, wgu, wd, wgu, wd)


DENSE_EXPERTS_PER_STEP = 4


def _moe_dense_kernel(*refs, first_step, resume):
    if resume:
        pay_ref, wgu_ref, wd_ref, acc_ref, o_ref = refs
    else:
        pay_ref, wgu_ref, wd_ref, o_ref = refs
    step = pl.program_id(0)

    @pl.when(step == 0)
    def _():
        o_ref[...] = acc_ref[...] if resume else jnp.zeros_like(o_ref)

    x, route = _payload_parts(pay_ref)
    y = None
    for k in range(DENSE_EXPERTS_PER_STEP):
        ef = ((first_step + step) * DENSE_EXPERTS_PER_STEP + k).astype(jnp.float32)
        gate = (jnp.where(route[:, 2:3] == ef, route[:, 0:1], 0.0)
                + jnp.where(route[:, 3:4] == ef, route[:, 1:2], 0.0))
        yk = _dot(_expert_hidden(x, wgu_ref[k], gate), wd_ref[k])
        y = yk if y is None else y + yk
    o_ref[...] += y


def _moe_dense_call(pay, wgu, wd, first_step, n_steps, acc=None):
    n = pay.shape[0]
    per = DENSE_EXPERTS_PER_STEP
    whole = pl.BlockSpec((n, D_MODEL), lambda e: (0, 0))
    return pl.pallas_call(
        functools.partial(_moe_dense_kernel, first_step=first_step, resume=acc is not None),
        grid=(n_steps,),
        in_specs=[pl.BlockSpec((n, PAYLOAD_W), lambda e: (0, 0)),
                  pl.BlockSpec((per, D_MODEL, 2 * D_EXPERT), lambda e: (first_step + e, 0, 0)),
                  pl.BlockSpec((per, D_EXPERT, D_MODEL), lambda e: (first_step + e, 0, 0))]
                 + ([whole] if acc is not None else []),
        out_specs=whole,
        out_shape=jax.ShapeDtypeStruct((n, D_MODEL), jnp.float32),
        compiler_params=_cparams(("arbitrary",)),
        name="moe_dense",
    )(pay, wgu, wd, *([acc] if acc is not None else []))


def _moe_out(y_ref):
    if y_ref.dtype == jnp.uint32:
        return jnp.concatenate(_unpack_halves(y_ref[...]), axis=-1)
    return y_ref[...]


def _residual_kernel(h_ref, y_ref, g_ref, o_ref):
    o_ref[...] = h_ref[...].astype(jnp.float32) + g_ref[...] * _moe_out(y_ref)


def _residual_call(h, y, g, tm):
    n = h.shape[0]
    row = pl.BlockSpec((tm, D_MODEL), lambda i: (i, 0))
    return pl.pallas_call(
        _residual_kernel,
        grid=(n // tm,),
        in_specs=[row, pl.BlockSpec((tm, y.shape[1]), lambda i: (i, 0)),
                  pl.BlockSpec((1, D_MODEL), lambda i: (0, 0))],
        out_specs=row,
        out_shape=jax.ShapeDtypeStruct((n, D_MODEL), jnp.float32),
        compiler_params=_cparams(("arbitrary",)),
        name="residual",
    )(h, y, g)


SC_ROWS = 128


SC_CORES = 2
SC_SUBCORES = 16
SC_WORKERS = SC_CORES * SC_SUBCORES


def _sc_mesh():
    return plsc.VectorSubcoreMesh(core_axis_name="core", subcore_axis_name="subcore")


def _sc_worker():
    return lax.axis_index("subcore") * SC_CORES + lax.axis_index("core")


def _scatter_rows(x, dest, n_out):
    n, w = x.shape

    per_worker = n // SC_WORKERS
    assert per_worker % SC_ROWS == 0

    @functools.partial(pl.kernel, out_type=jax.ShapeDtypeStruct((n_out, w), x.dtype), mesh=_sc_mesh(),
                       scratch_types=[pltpu.VMEM((SC_ROWS,), jnp.int32), pltpu.VMEM((SC_ROWS, w), x.dtype)])
    def scatter(x_hbm, i_hbm, o_hbm, idx_v, rows_v):
        first = _sc_worker() * per_worker

        @pl.loop(0, per_worker // SC_ROWS)
        def _(i):
            base = pl.multiple_of(first + i * SC_ROWS, SC_ROWS)
            pltpu.sync_copy(i_hbm.at[pl.ds(base, SC_ROWS)], idx_v)
            pltpu.sync_copy(x_hbm.at[pl.ds(base, SC_ROWS)], rows_v)
            pltpu.sync_copy(rows_v, o_hbm.at[idx_v])

    return scatter(x, dest)


def _gather_rows(x, src):
    n = src.shape[0]
    w = x.shape[1]

    per_worker = n // SC_WORKERS
    assert per_worker % SC_ROWS == 0

    @functools.partial(pl.kernel, out_type=jax.ShapeDtypeStruct((n, w), x.dtype), mesh=_sc_mesh(),
                       scratch_types=[pltpu.VMEM((SC_ROWS,), jnp.int32), pltpu.VMEM((SC_ROWS, w), x.dtype)])
    def gather(x_hbm, i_hbm, o_hbm, idx_v, rows_v):
        first = _sc_worker() * per_worker

        @pl.loop(0, per_worker // SC_ROWS)
        def _(i):
            base = pl.multiple_of(first + i * SC_ROWS, SC_ROWS)
            pltpu.sync_copy(i_hbm.at[pl.ds(base, SC_ROWS)], idx_v)
            pltpu.sync_copy(x_hbm.at[idx_v], rows_v)
            pltpu.sync_copy(rows_v, o_hbm.at[pl.ds(base, SC_ROWS)])

    return gather(x, src)


def _routing_plan(cls, n_tiles):
    onehot = (cls[:, None] == jnp.arange(N_CLASSES, dtype=jnp.int32)[None, :]).astype(jnp.int32)
    counts = jnp.sum(onehot, axis=0)
    rank = jnp.sum((jnp.cumsum(onehot, axis=0) - onehot) * onehot, axis=1)
    tiles = (counts + MOE_TM - 1) // MOE_TM
    tile_end = jnp.cumsum(tiles)
    tile_start = tile_end - tiles
    first_len = counts - (tiles - 1) * MOE_TM
    first_of = jnp.sum(onehot * first_len[None, :], axis=1)
    skip = jnp.where(rank >= first_of, MOE_TM - first_of, 0)
    dest = jnp.sum(onehot * tile_start[None, :], axis=1) * MOE_TM + rank + skip
    nact = tile_end[-1]
    tile = jnp.arange(n_tiles, dtype=jnp.int32)
    blk = jnp.minimum(tile, nact - 1)
    tile_cls = jnp.sum((blk[:, None] >= tile_end[None, :]).astype(jnp.int32), axis=1)
    e1 = jnp.asarray(CLASS_E1)[tile_cls]
    e2 = jnp.asarray(CLASS_E2)[tile_cls]
    valid = jnp.where(blk == tile_start[tile_cls], first_len[tile_cls], MOE_TM)
    parts = jnp.where(tile >= nact, 0, (valid + MOE_PART - 1) // MOE_PART)
    return dest.astype(jnp.int32), blk, e1, e2, parts.astype(jnp.int32)


def _row_tile(n, prefer=512):
    return next(t for t in (prefer, 512, 256) if n % t == 0)


def kernel(x, c, ctx, c_ctx, w_ada, b_ada, norm1, w_in, pool_w, pool_scale, q_norm, k_norm, rpb,
           sg_w, sg_b, sg_norm, w_out, norm2, w_router, b_router, w_gate, w_up, w_down):
    depth = w_ada.shape[0]
    n = x.shape[1]
    lc = ctx.shape[1]
    assert x.shape[0] == 1 and x.shape[2] == D_MODEL, "one sample of (tokens, 1024) features"
    assert n % (GRID_W * NA_ROWS_PER_BLOCK) == 0 and n % (SC_WORKERS * SC_ROWS) == 0 and lc % 256 == 0
    bf = jnp.bfloat16
    lat_stream = (x[0],)
    h_ctx = ctx[0]

    cond = jnp.stack([c[0], c_ctx], axis=1)
    mod = _ada_call(cond, w_ada, b_ada)

    wr_t = w_router.T
    wr_hi = wr_t.astype(bf)
    wr_lo = (wr_t - wr_hi.astype(jnp.float32)).astype(bf)
    wr = jnp.concatenate([wr_hi, wr_lo], axis=0)
    br = b_router.reshape(N_EXPERTS, 1)
    n_tiles = n // MOE_TM + N_CLASSES
    w_in_bf = w_in.astype(bf)
    w_out_bf = w_out.astype(bf)

    for l in range(depth):
        last = l == depth - 1
        qg =(q_norm[l] * (HEAD_DIM ** -0.5 * LOG2_E)).reshape(1, NA_DIM)
        kg = k_norm[l].reshape(1, NA_DIM)
        sgn = sg_norm[l].reshape(1, SG_DIM)
        sgw = sg_w[l].astype(bf).reshape(SG_DIM // LANES, 2 * SG_CHUNK, SG_CHUNK)
        sgb = jnp.broadcast_to(sg_b[l].reshape(SG_DIM // LANES, 2 * SG_CHUNK, 1),
                               (SG_DIM // LANES, 2 * SG_CHUNK, LANES))
        pool_bd = jax.scipy.linalg.block_diag(*[pool_w[l, g] for g in range(len(POOL_WINDOWS))]).astype(bf)
        pscale = pool_scale[l].reshape(1, POOL_DIM)
        bias = _natten_bias(rpb[l])
        n1 = norm1[l].reshape(1, D_MODEL)
        n2 = norm2[l].reshape(1, D_MODEL)

        def mods(row):
            return [mod[l, row:row + 1, i * D_MODEL:(i + 1) * D_MODEL] for i in range(6)]

        sh1, sc1, g1, sh2, sc2, g2 = mods(0)
        csh1, csc1, cg1, csh2, csc2, cg2 = mods(1)

        tc = _row_tile(lc)
        mix_pool_c, q_c, k_c, v_c, sg_c = _inproj_call((h_ctx,), n1, csh1, csc1, w_in_bf, l, qg, kg,
                                                       sgn, sgw, sgb, pool_bd, pscale, tc, strips=False)

        outs = _inproj_call(lat_stream, n1, sh1, sc1, w_in_bf, l, qg, kg, sgn, sgw, sgb,
                            pool_bd, pscale, _row_tile(n, INPROJ_TILE), strips=True)
        mix_pool, q, k, v, sg = outs[:5]
        h_lat = outs[5] if len(lat_stream) == 3 else lat_stream[0]
        att, wgu_l, wd_l = _natten_call(q, k, v, k_c, v_c, bias, w_gate, w_up, w_down, l)
        h1, pay, route = _outproj_call(h_lat, mix_pool, att, sg, w_out_bf, l, g1, n2, sh2, sc2, wr, br,
                                       _row_tile(n, OUTPROJ_TILE))
        cls = route[4].astype(jnp.int32)
        dest, blk, e1, e2, parts = _routing_plan(cls, n_tiles)
        pay_sorted = _scatter_rows(pay, dest, n_tiles * MOE_TM)

        if not last:
            dense_steps = N_EXPERTS // DENSE_EXPERTS_PER_STEP
            att_c = _ctxatt_call(q_c, k_c, v_c)
            h1_c, pay_c, _ = _outproj_call(h_ctx, mix_pool_c, att_c, sg_c, w_out_bf, l, cg1, n2, csh2, csc2,
                                           wr, br, tc)
            y_c = _moe_dense_call(pay_c, wgu_l, wd_l, 0, dense_steps // 2)
            parts, y_c = lax.optimization_barrier((parts, y_c))

        y_sorted = _moe_sorted_call(blk, e1, e2, parts, pay_sorted, wgu_l, wd_l)
        y = _gather_rows(y_sorted, dest)
        lat_stream = (h1, y, g2)

        if not last:
            y_c = _moe_dense_call(pay_c, wgu_l, wd_l, dense_steps // 2, dense_steps - dense_steps // 2,
                                  acc=y_c)
            h_ctx = _residual_call(h1_c, y_c, cg2, tc)

    return _residual_call(*lat_stream, _row_tile(n, RESIDUAL_TILE))[None]
```
